```python
import jax
import jax.numpy as jnp
from jax import lax
import numpy as np

D_MODEL = 1024
BATCH = 16
SEQ = 256
DEPTH = 2
DEC_BATCH = 8
DEC_SEQ = 2048
PAST_LEN = 256

EPS = 1e-6
NEG_INF = -1e30
GRID_W = 64
N_A_LAYERS = (DEPTH + 1) // 2
N_C_LAYERS = DEPTH // 2
H_A = 4
DK_A = 128
DV_A = 128
A_KEY = H_A * DK_A
A_WIDTH = H_A * DV_A
HGRN_CHUNK = 32
H_B = 8
DH_B = 64
B_WIDTH = H_B * DH_B
NA_KH = 8
NA_KW = 16
NA_COL_BLOCK = 16
NA_BAND = NA_COL_BLOCK + NA_KW
NA_N_COL_BLOCKS = GRID_W // NA_COL_BLOCK
CTX_Q_BLOCK = 128
EVEN_SPLITS = (A_KEY, A_KEY, A_KEY, A_WIDTH, A_WIDTH, B_WIDTH, B_WIDTH, B_WIDTH, B_WIDTH)
EVEN_IN = 3 * A_KEY + 2 * A_WIDTH + 4 * B_WIDTH
W_C = 1024
H_C = 8
BW_C = W_C // H_C
CONV_W = 4
RG_C = 8.0

kernel_name = 'bidir_hgrn2_natten_rglru_prefix_dit_step'


def rmsnorm(x, g):
    x32 = x.astype(jnp.float32)
    y = x32 * lax.rsqrt(jnp.mean(x32 * x32, axis=-1, keepdims=True) + EPS)
    return (y * g.astype(jnp.float32)).astype(x.dtype)


def adaln(cond, w_mod, b_mod):
    m = (jax.nn.silu(cond) @ w_mod + b_mod)[:, None, :]
    return jnp.split(m, 3, axis=-1)


def split_heads(t, n_heads):
    B, T, _ = t.shape
    return t.reshape(B, T, n_heads, -1).transpose(0, 2, 1, 3)


def merge_heads(t):
    B, H, T, d = t.shape
    return t.transpose(0, 2, 1, 3).reshape(B, T, H * d)


def hgrn2_chunkwise(q, k, log_f, v, s0):
    B, H, T, DK = q.shape
    DV = v.shape[-1]
    n = T // HGRN_CHUNK
    shp = (B, H, n, HGRN_CHUNK)
    q, k, log_f = (t.reshape(shp + (DK,)) for t in (q, k, log_f))
    v = v.reshape(shp + (DV,))
    b = jnp.cumsum(log_f, axis=3)
    b_end = b[:, :, :, -1:, :]
    q_dec = q * jnp.exp(b)
    k_inv = k * jnp.exp(-b)
    lower = jnp.tril(jnp.ones((HGRN_CHUNK, HGRN_CHUNK), dtype=bool))
    att = jnp.where(lower, jnp.einsum('bhntd,bhnsd->bhnts', q_dec, k_inv), 0.0)
    o_intra = jnp.einsum('bhnts,bhnsv->bhntv', att, v)
    chunk_kv = jnp.einsum('bhnsd,bhnsv->nbhdv', k * jnp.exp(b_end - b), v)
    chunk_decay = jnp.moveaxis(jnp.exp(b_end[:, :, :, 0, :]), 2, 0)

    def step(S, inp):
        dec, kv = inp
        return dec[..., None] * S + kv, S

    s_fin, s_prev = lax.scan(step, s0, (chunk_decay, chunk_kv))
    o_inter = jnp.einsum('bhntd,nbhdv->bhntv', q_dec, s_prev)
    return (o_intra + o_inter).reshape(B, H, T, DV), s_fin


def hgrn2_bidirectional(q, z_fwd, z_bwd, v, lb, s0):
    qh = split_heads(q.astype(jnp.float32), H_A)
    vh = split_heads(v.astype(jnp.float32), H_A)
    outs, finals = [], []
    for d, z in enumerate((z_fwd, z_bwd)):
        z32 = z.astype(jnp.float32)
        f = lb[d] + (1.0 - lb[d]) * jax.nn.sigmoid(z32)
        k = (1.0 - lb[d]) * jax.nn.sigmoid(-z32)
        args = [qh, split_heads(k, H_A), split_heads(jnp.log(f), H_A), vh]
        if d == 1:
            args = [jnp.flip(t, axis=2) for t in args]
        o, s_fin = hgrn2_chunkwise(*args, s0[:, d].astype(jnp.float32))
        outs.append(jnp.flip(o, axis=2) if d == 1 else o)
        finals.append(s_fin)
    return outs[0] + outs[1], jnp.stack(finals, axis=1)


def context_attention(q, k, v):
    B, H, Tc, dh = q.shape
    scale = dh ** -0.5
    qb = jnp.moveaxis(q.reshape(B, H, Tc // CTX_Q_BLOCK, CTX_Q_BLOCK, dh), 2, 0)

    def block(qi):
        s = jnp.einsum('bhqd,bhkd->bhqk', qi, k).astype(jnp.float32) * scale
        p = jax.nn.softmax(s, axis=-1).astype(v.dtype)
        return jnp.einsum('bhqk,bhkd->bhqd', p, v)

    o = lax.map(block, qb)
    return jnp.moveaxis(o, 0, 2).reshape(B, H, Tc, dh)


def neighbourhood_attention(q, k, v, k_ctx, v_ctx, rel_bias):
    B, H, T, dh = q.shape
    rows = T // GRID_W
    kh = min(NA_KH, rows)
    scale = dh ** -0.5
    cols = np.arange(GRID_W).reshape(NA_N_COL_BLOCKS, NA_COL_BLOCK)
    band_start = np.clip(np.arange(NA_N_COL_BLOCKS) * NA_COL_BLOCK - NA_KW // 2, 0, GRID_W - NA_BAND)
    col_idx = (band_start[:, None] + np.arange(NA_BAND)).astype(np.int32)
    win_start = np.clip(cols - NA_KW // 2, 0, GRID_W - NA_KW)[..., None]
    key_col = col_idx[:, None, :]
    valid = jnp.asarray((key_col >= win_start) & (key_col < win_start + NA_KW))[:, :, None, :]
    dc_idx = np.clip(key_col - cols[..., None] + NA_KW - 1, 0, 2 * NA_KW - 2).astype(np.int32)[:, :, None, :]
    bias_tab = rel_bias.astype(jnp.float32)
    k_grid = k.reshape(B, H, rows, GRID_W, dh)
    v_grid = v.reshape(B, H, rows, GRID_W, dh)
    k_ctx = k_ctx.astype(q.dtype)
    v_ctx = v_ctx.astype(q.dtype)
    n_loc = kh * NA_BAND

    def row_block(args):
        r, q_r = args
        rs = jnp.clip(r - kh // 2, 0, rows - kh)
        k_blk = lax.dynamic_slice_in_dim(k_grid, rs, kh, axis=2)[:, :, :, col_idx]
        v_blk = lax.dynamic_slice_in_dim(v_grid, rs, kh, axis=2)[:, :, :, col_idx]
        qb = q_r.reshape(B, H, NA_N_COL_BLOCKS, NA_COL_BLOCK, dh)
        dr_idx = rs + jnp.arange(kh) - r + NA_KH - 1
        bias = bias_tab[:, dr_idx[None, None, :, None], dc_idx]
        s_loc = jnp.einsum('bhjqd,bhkjnd->bhjqkn', qb, k_blk).astype(jnp.float32) * scale + bias
        s_loc = jnp.where(valid, s_loc, NEG_INF)
        s_ctx = jnp.einsum('bhjqd,bhcd->bhjqc', qb, k_ctx).astype(jnp.float32) * scale
        s = jnp.concatenate([s_loc.reshape(B, H, NA_N_COL_BLOCKS, NA_COL_BLOCK, n_loc), s_ctx], axis=-1)
        p = jax.nn.softmax(s, axis=-1).astype(v.dtype)
        p_loc = p[..., :n_loc].reshape(B, H, NA_N_COL_BLOCKS, NA_COL_BLOCK, kh, NA_BAND)
        o = (jnp.einsum('bhjqkn,bhkjnd->bhjqd', p_loc, v_blk)
             + jnp.einsum('bhjqc,bhcd->bhjqd', p[..., n_loc:], v_ctx))
        return o.reshape(B, H, GRID_W, dh)

    q_rows = jnp.moveaxis(q.reshape(B, H, rows, GRID_W, dh), 2, 0)
    o = lax.map(row_block, (jnp.arange(rows), q_rows))
    return jnp.moveaxis(o, 0, 2).reshape(B, H, T, dh)


def even_mixer(h, w_in, w_out, lb, out_gain, rel_bias, s0, ctx_kv):
    parts = jnp.split(h @ w_in, np.cumsum(EVEN_SPLITS)[:-1].tolist(), axis=-1)
    q_a, z_f, z_b, v_a, g_a, q_b, k_b, v_b, g_b = parts
    o_a, s_fin = hgrn2_bidirectional(q_a, z_f, z_b, v_a, lb, s0)
    o_a = merge_heads(rmsnorm(o_a, out_gain[None, :, None, :])).astype(h.dtype)
    qh, kh, vh = (split_heads(t, H_B) for t in (q_b, k_b, v_b))
    if ctx_kv is None:
        o_b = context_attention(qh, kh, vh)
    else:
        o_b = neighbourhood_attention(qh, kh, vh, ctx_kv[0], ctx_kv[1], rel_bias)
    y = jnp.concatenate([o_a * jax.nn.silu(g_a), merge_heads(o_b) * jax.nn.silu(g_b)], axis=-1)
    return y @ w_out, s_fin, kh, vh


def centred_dwconv(x, w, b):
    left = CONV_W // 2
    T = x.shape[1]
    xp = jnp.pad(x, ((0, 0), (left, CONV_W - 1 - left), (0, 0)))
    return sum(xp[:, j:j + T] * w[j] for j in range(CONV_W)) + b


def linear_recurrence(a, u, h0):
    u = u.at[:, 0].add(a[:, 0] * h0)

    def combine(left, right):
        a_l, u_l = left
        a_r, u_r = right
        return a_l * a_r, a_r * u_l + u_r

    _, hs = lax.associative_scan(combine, (a, u), axis=1)
    return hs, hs[:, -1]


def odd_mixer(h, w_in, w_out, conv_w, conv_b, gate_w, gate_b, lam, s0):
    xb, g = jnp.split(h @ w_in, 2, axis=-1)
    xc = centred_dwconv(xb, conv_w, conv_b).astype(jnp.float32)
    B, T, _ = xc.shape
    xblk = xc.reshape(B, T, H_C, BW_C)
    outs, finals = [], []
    for d in range(2):
        gates = jnp.einsum('bthi,ghij->gbthj', xblk, gate_w[d].astype(jnp.float32)).reshape(2, B, T, W_C)
        gates = gates + gate_b[d].astype(jnp.float32)[:, None, None, :]
        r = jax.nn.sigmoid(gates[0])
        i = jax.nn.sigmoid(gates[1])
        log_a = -RG_C * r * jax.nn.softplus(-lam[d].astype(jnp.float32))
        a = jnp.exp(log_a)
        u = jnp.sqrt(-jnp.expm1(2.0 * log_a)) * (i * xc)
        if d == 1:
            a, u = jnp.flip(a, axis=1), jnp.flip(u, axis=1)
        hs, h_fin = linear_recurrence(a, u, s0[:, d].astype(jnp.float32))
        outs.append(jnp.flip(hs, axis=1) if d == 1 else hs)
        finals.append(h_fin)
    y = ((outs[0] + outs[1]) * jax.nn.silu(g.astype(jnp.float32))).astype(h.dtype)
    return y @ w_out, jnp.stack(finals, axis=1)


def setup_inputs(seed: int = 0) -> dict:
    key = jax.random.key(seed)
    ks = jax.random.split(key, 24)

    def nrm(k, shape, s):
        return jax.random.normal(k, shape, jnp.float32) * s

    x_prompt = nrm(ks[0], (BATCH, SEQ, D_MODEL), 1.0)
    x_sample = nrm(ks[1], (DEC_BATCH, DEC_SEQ, D_MODEL), 1.0)
    state_hgrn = nrm(ks[2], (DEC_BATCH, N_A_LAYERS, 2, H_A, DK_A, DV_A), 0.5)
    cache_na_k = nrm(ks[3], (DEC_BATCH, N_A_LAYERS, H_B, PAST_LEN, DH_B), 1.0)
    cache_na_v = nrm(ks[4], (DEC_BATCH, N_A_LAYERS, H_B, PAST_LEN, DH_B), 1.0)
    state_rglru = nrm(ks[5], (DEC_BATCH, N_C_LAYERS, 2, W_C), 0.5)
    c = nrm(ks[6], (DEC_BATCH, D_MODEL), 1.0)
    c_ctx = nrm(ks[7], (D_MODEL,), 1.0)
    norm_gain = 1.0 + nrm(ks[8], (DEPTH, D_MODEL), 0.02)
    w_mod = nrm(ks[9], (DEPTH, D_MODEL, 3 * D_MODEL), D_MODEL ** -0.5)
    b_mod = nrm(ks[10], (DEPTH, 3 * D_MODEL), 0.02)
    w_in_even = nrm(ks[11], (N_A_LAYERS, D_MODEL, EVEN_IN), D_MODEL ** -0.5)
    w_out_even = nrm(ks[12], (N_A_LAYERS, A_WIDTH + B_WIDTH, D_MODEL), (A_WIDTH + B_WIDTH) ** -0.5)
    hgrn_lb_logits = nrm(ks[13], (2, N_A_LAYERS + 1, A_KEY), 0.5)
    hgrn_out_gain = 1.0 + nrm(ks[14], (N_A_LAYERS, H_A, DV_A), 0.02)
    na_rel_bias = nrm(ks[15], (N_A_LAYERS, H_B, 2 * NA_KH - 1, 2 * NA_KW - 1), 0.1)
    w_in_odd = nrm(ks[16], (N_C_LAYERS, D_MODEL, 2 * W_C), D_MODEL ** -0.5)
    w_out_odd = nrm(ks[17], (N_C_LAYERS, W_C, D_MODEL), W_C ** -0.5)
    conv_w = nrm(ks[18], (N_C_LAYERS, CONV_W, W_C), CONV_W ** -0.5)
    conv_b = nrm(ks[19], (N_C_LAYERS, W_C), 0.02)
    rg_gate_w = nrm(ks[20], (N_C_LAYERS, 2, 2, H_C, BW_C, BW_C), BW_C ** -0.5)
    rg_gate_b = nrm(ks[21], (N_C_LAYERS, 2, 2, W_C), 0.02)
    a_pow = jax.random.uniform(ks[22], (N_C_LAYERS, 2, W_C), jnp.float32, 0.9, 0.999)
    a_base = a_pow ** (1.0 / RG_C)
    rg_lambda = jnp.log(a_base) - jnp.log1p(-a_base)
    final_gain = 1.0 + nrm(ks[23], (D_MODEL,), 0.02)
    return {'x_prompt': x_prompt, 'x_sample': x_sample, 'state_hgrn': state_hgrn,
            'cache_na_k': cache_na_k, 'cache_na_v': cache_na_v, 'state_rglru': state_rglru,
            'c': c, 'c_ctx': c_ctx, 'norm_gain': norm_gain, 'w_mod': w_mod, 'b_mod': b_mod,
            'w_in_even': w_in_even, 'w_out_even': w_out_even, 'hgrn_lb_logits': hgrn_lb_logits,
            'hgrn_out_gain': hgrn_out_gain, 'na_rel_bias': na_rel_bias,
            'w_in_odd': w_in_odd, 'w_out_odd': w_out_odd, 'conv_w': conv_w, 'conv_b': conv_b,
            'rg_gate_w': rg_gate_w, 'rg_gate_b': rg_gate_b, 'rg_lambda': rg_lambda,
            'final_gain': final_gain}


def reference(x_prompt, x_sample, state_hgrn, cache_na_k, cache_na_v, state_rglru, c,
              c_ctx, norm_gain, w_mod, b_mod, w_in_even, w_out_even, hgrn_lb_logits,
              hgrn_out_gain, na_rel_bias, w_in_odd, w_out_odd, conv_w, conv_b,
              rg_gate_w, rg_gate_b, rg_lambda, final_gain):
    lb_all = jnp.cumsum(jax.nn.softmax(hgrn_lb_logits.astype(jnp.float32), axis=1), axis=1)
    xc, xs = x_prompt, x_sample
    cond_ctx = c_ctx[None, :]
    new_hgrn, new_k, new_v, new_rg = [], [], [], []
    for l in range(DEPTH):
        sh_c, sc_c, gt_c = adaln(cond_ctx, w_mod[l], b_mod[l])
        sh_s, sc_s, gt_s = adaln(c, w_mod[l], b_mod[l])
        h_c = rmsnorm(xc, norm_gain[l]) * (1 + sc_c) + sh_c
        h_s = rmsnorm(xs, norm_gain[l]) * (1 + sc_s) + sh_s
        j = l // 2
        if l % 2 == 0:
            s_zero = jnp.zeros((xc.shape[0], 2, H_A, DK_A, DV_A), jnp.float32)
            m_c, s_fin, k_c, v_c = even_mixer(h_c, w_in_even[j], w_out_even[j], lb_all[:, j],
                                              hgrn_out_gain[j], na_rel_bias[j], s_zero, None)
            m_s, _, _, _ = even_mixer(h_s, w_in_even[j], w_out_even[j], lb_all[:, j],
                                      hgrn_out_gain[j], na_rel_bias[j], state_hgrn[:, j],
                                      (cache_na_k[:, j], cache_na_v[:, j]))
            new_hgrn.append(s_fin)
            new_k.append(k_c)
            new_v.append(v_c)
        else:
            s_zero = jnp.zeros((xc.shape[0], 2, W_C), jnp.float32)
            m_c, s_fin = odd_mixer(h_c, w_in_odd[j], w_out_odd[j], conv_w[j], conv_b[j],
                                   rg_gate_w[j], rg_gate_b[j], rg_lambda[j], s_zero)
            m_s, _ = odd_mixer(h_s, w_in_odd[j], w_out_odd[j], conv_w[j], conv_b[j],
                               rg_gate_w[j], rg_gate_b[j], rg_lambda[j], state_rglru[:, j])
            new_rg.append(s_fin)
        xc = xc + gt_c * m_c
        xs = xs + gt_s * m_s
    y_prompt = rmsnorm(xc, final_gain)
    y_sample = rmsnorm(xs, final_gain)
    return (y_prompt, y_sample, jnp.stack(new_hgrn, axis=1), jnp.stack(new_k, axis=1),
            jnp.stack(new_v, axis=1), jnp.stack(new_rg, axis=1))
```

```python
import functools

import jax
import jax.numpy as jnp
from jax import lax
from jax.experimental import pallas as pl
from jax.experimental.pallas import tpu as pltpu

F32 = jnp.float32
BF16 = jnp.bfloat16

D_MODEL = 1024
EPS = 1e-6
NEG_INF = -1e30
H_A = 4
DK_A = 128
HGRN_CHUNK = 32
HGRN_ROWS = 256
H_B = 8
DH_B = 64
GRID_W = 64
NA_KH = 8
NA_KW = 16
NA_GROUP = 4
W_C = 1024
H_C = 8
BW_C = W_C // H_C
RG_C = 8.0
RG_ROWS = 256
N_SEG = 8
LANES = 128
VMEM_LIMIT = 48 * 1024 * 1024

NT_DIMS = (((1,), (1,)), ((), ()))
TN_DIMS = (((0,), (0,)), ((), ()))


def _silu(x):
    return x * jax.nn.sigmoid(x)


def _cparams(n_axes):
    return pltpu.CompilerParams(dimension_semantics=("arbitrary",) * n_axes,
                                vmem_limit_bytes=VMEM_LIMIT)


def _mod_kernel(cond_ref, w_ref, b_ref, o_ref):
    s = _silu(cond_ref[...])
    o_ref[0] = jnp.dot(s.astype(BF16), w_ref[0], preferred_element_type=F32) + b_ref[0]


def _modulation(cond, w_mod, b_mod):
    depth = w_mod.shape[0]
    n_rows = cond.shape[0]
    return pl.pallas_call(
        _mod_kernel,
        out_shape=jax.ShapeDtypeStruct((depth, n_rows, 3 * D_MODEL), F32),
        grid=(depth, 3),
        in_specs=[
            pl.BlockSpec((n_rows, D_MODEL), lambda l, n: (0, 0)),
            pl.BlockSpec((1, D_MODEL, D_MODEL), lambda l, n: (l, 0, n)),
            pl.BlockSpec((1, 1, D_MODEL), lambda l, n: (l, 0, n)),
        ],
        out_specs=pl.BlockSpec((1, n_rows, D_MODEL), lambda l, n: (l, 0, n)),
        compiler_params=_cparams(2),
        name="adaln_mod",
    )(cond, w_mod, b_mod)


def _inproj_kernel(x_ref, mod_ref, gain_ref, w_ref, *out_refs, outs):
    x = x_ref[0]
    var = jnp.mean(x * x, axis=-1, keepdims=True)
    y = x * lax.rsqrt(var + EPS) * gain_ref[...]
    h = y * (1.0 + mod_ref[0, 1:2, :]) + mod_ref[0, 0:1, :]
    hb = h.astype(BF16)
    step = 512
    for o_ref, (col0, width, _) in zip(out_refs, outs):
        for c in range(0, width, step):
            r = jnp.dot(hb, w_ref[:, col0 + c:col0 + c + step], preferred_element_type=F32)
            o_ref[0, :, c:c + step] = r.astype(o_ref.dtype)


def _inproj(x, mod, gain, w, outs, tm, shared_mod):
    B, T, _ = x.shape
    n_cols = w.shape[1]
    mod_map = (lambda b, t: (0, 0, 0)) if shared_mod else (lambda b, t: (b, 0, 0))
    return pl.pallas_call(
        functools.partial(_inproj_kernel, outs=outs),
        out_shape=[jax.ShapeDtypeStruct((B, T, wd), dt) for _, wd, dt in outs],
        grid=(B, T // tm),
        in_specs=[
            pl.BlockSpec((1, tm, D_MODEL), lambda b, t: (b, t, 0)),
            pl.BlockSpec((1, 3, D_MODEL), mod_map),
            pl.BlockSpec((1, D_MODEL), lambda b, t: (0, 0)),
            pl.BlockSpec((D_MODEL, n_cols), lambda b, t: (0, 0)),
        ],
        out_specs=[pl.BlockSpec((1, tm, wd), lambda b, t: (b, t, 0)) for _, wd, _ in outs],
        compiler_params=_cparams(2),
        name="in_proj",
    )(x, mod, gain, w)


def _outproj_kernel(y_ref, x_ref, mod_ref, w_ref, *rest, final):
    m = jnp.dot(y_ref[0], w_ref[...], preferred_element_type=F32)
    xn = x_ref[0] + mod_ref[0, 2:3, :] * m
    if final:
        gain_ref, o_ref = rest
        var = jnp.mean(xn * xn, axis=-1, keepdims=True)
        xn = xn * lax.rsqrt(var + EPS) * gain_ref[...]
    else:
        (o_ref,) = rest
    o_ref[0] = xn


def _outproj(y, x, mod, w, tm, shared_mod, final_gain=None):
    B, T, _ = x.shape
    final = final_gain is not None
    mod_map = (lambda b, t: (0, 0, 0)) if shared_mod else (lambda b, t: (b, 0, 0))
    in_specs = [
        pl.BlockSpec((1, tm, D_MODEL), lambda b, t: (b, t, 0)),
        pl.BlockSpec((1, tm, D_MODEL), lambda b, t: (b, t, 0)),
        pl.BlockSpec((1, 3, D_MODEL), mod_map),
        pl.BlockSpec((D_MODEL, D_MODEL), lambda b, t: (0, 0)),
    ]
    args = [y, x, mod, w]
    if final:
        in_specs.append(pl.BlockSpec((1, D_MODEL), lambda b, t: (0, 0)))
        args.append(final_gain)
    return pl.pallas_call(
        functools.partial(_outproj_kernel, final=final),
        out_shape=jax.ShapeDtypeStruct((B, T, D_MODEL), F32),
        grid=(B, T // tm),
        in_specs=in_specs,
        out_specs=pl.BlockSpec((1, tm, D_MODEL), lambda b, t: (b, t, 0)),
        compiler_params=_cparams(2),
        name="out_proj",
    )(*args)


def _hgrn_kernel(q_ref, zf_ref, zb_ref, v_ref, g_ref, lgt_ref, gain_ref, *rest, seq, layer, has_s0, emit_state):
    rest = list(rest)
    s0_ref = rest.pop(0) if has_s0 else None
    o_ref = rest.pop(0)
    sfin_ref = rest.pop(0) if emit_state else None
    acc_ref, mst_ref, msk_ref = rest
    R = HGRN_ROWS
    n_blk = seq // R
    n_chunk = R // HGRN_CHUNK

    ti = lax.broadcasted_iota(jnp.int32, (R, R), 0)
    tj = lax.broadcasted_iota(jnp.int32, (R, R), 1)
    shift = HGRN_CHUNK.bit_length() - 1
    same = lax.shift_right_logical(ti, shift) == lax.shift_right_logical(tj, shift)
    one = jnp.ones((R, R), F32)
    zero = jnp.zeros((R, R), F32)
    incl = (jnp.where(same, jnp.where(tj <= ti, one, zero), zero),
            jnp.where(same, jnp.where(tj >= ti, one, zero), zero))
    excl = (jnp.where(same, jnp.where(tj > ti, one, zero), zero),
            jnp.where(same, jnp.where(tj < ti, one, zero), zero))
    for d in range(2):
        msk_ref[d] = incl[d]
        mst_ref[d, 0:R, :] = incl[d].astype(BF16)
        mst_ref[d, R:2 * R, :] = excl[d].astype(BF16)

    lgt = [lgt_ref[:, i, :] for i in range(lgt_ref.shape[1])]
    lmax = functools.reduce(jnp.maximum, lgt)
    ex = [jnp.exp(t - lmax) for t in lgt]
    lb_all = sum(ex[:layer + 1]) / sum(ex)
    gain = gain_ref[0]

    for d in range(2):
        z_ref = zf_ref if d == 0 else zb_ref
        lb = lb_all[d:d + 1, :]
        oml = 1.0 - lb

        def body(i, st, d=d, z_ref=z_ref, lb=lb, oml=oml):
            blk = i if d == 0 else n_blk - 1 - i
            r0 = pl.multiple_of(blk * R, R)
            rows = pl.ds(r0, R)
            z = z_ref[0, rows, :]
            q = q_ref[0, rows, :]
            v = v_ref[0, rows, :]
            e = jnp.exp(-jnp.abs(z))
            r = 1.0 / (1.0 + e)
            er = e * r
            pos = z >= 0.0
            f = lb + oml * jnp.where(pos, r, er)
            k = oml * jnp.where(pos, er, r)
            logf = jnp.log(f)
            hi = logf.astype(BF16)
            lo = (logf - hi.astype(F32)).astype(BF16)
            cs = jnp.dot(mst_ref[d], jnp.concatenate([hi, lo], axis=1), preferred_element_type=F32)
            b = cs[0:R, 0:LANES] + cs[0:R, LANES:2 * LANES]
            bex = cs[R:2 * R, 0:LANES] + cs[R:2 * R, LANES:2 * LANES]
            qd = (q * jnp.exp(b)).astype(BF16)
            ki = (k * jnp.exp(-b)).astype(BF16)
            kd = k * jnp.exp(bex)
            att = lax.dot_general(qd, ki, NT_DIMS, preferred_element_type=F32)
            att = jnp.where(msk_ref[d] > 0.5, att, 0.0)
            o = jnp.dot(att.astype(BF16), v.astype(BF16), preferred_element_type=F32)
            btot = b + bex
            pieces = [None] * n_chunk
            for cc in range(n_chunk):
                c = cc if d == 0 else n_chunk - 1 - cc
                lo_r, hi_r = c * HGRN_CHUNK, (c + 1) * HGRN_CHUNK
                pieces[c] = lax.dot_general(qd[lo_r:hi_r], st.astype(BF16), NT_DIMS,
                                            preferred_element_type=F32)
                kv_t = lax.dot_general(v[lo_r:hi_r], kd[lo_r:hi_r], TN_DIMS, preferred_element_type=F32)
                st = st * jnp.exp(btot[lo_r:lo_r + 1, :]) + kv_t
            o = o + jnp.concatenate(pieces, axis=0)
            if d == 0:
                acc_ref[rows, :] = o
            else:
                tot = acc_ref[rows, :] + o
                var = jnp.mean(tot * tot, axis=-1, keepdims=True)
                y = tot * lax.rsqrt(var + EPS) * gain
                o_ref[0, rows, :] = (y * _silu(g_ref[0, rows, :])).astype(o_ref.dtype)
            return st

        if has_s0:
            st0 = s0_ref[0, d, 0].T
        else:
            st0 = jnp.zeros((DK_A, DK_A), F32)
        st = lax.fori_loop(0, n_blk, body, st0)
        if emit_state:
            sfin_ref[0, d, 0] = st.T


def _hgrn(ya, lgt, layer, gain, s0, emit_state):
    B, T, _ = ya.shape
    has_s0 = s0 is not None

    def col(k):
        return pl.BlockSpec((1, T, LANES), lambda b, h, k=k: (b, 0, k * H_A + h))

    in_specs = [col(0), col(1), col(2), col(3), col(4),
                pl.BlockSpec((2, lgt.shape[1], LANES), lambda b, h: (0, 0, h)),
                pl.BlockSpec((1, 1, LANES), lambda b, h: (h, 0, 0))]
    args = [ya, ya, ya, ya, ya, lgt, gain]
    if has_s0:
        in_specs.append(pl.BlockSpec((1, 2, 1, DK_A, DK_A), lambda b, h: (b, 0, h, 0, 0)))
        args.append(s0)
    out_shape = [jax.ShapeDtypeStruct((B, T, H_A * DK_A), BF16)]
    out_specs = [pl.BlockSpec((1, T, LANES), lambda b, h: (b, 0, h))]
    if emit_state:
        out_shape.append(jax.ShapeDtypeStruct((B, 2, H_A, DK_A, DK_A), F32))
        out_specs.append(pl.BlockSpec((1, 2, 1, DK_A, DK_A), lambda b, h: (b, 0, h, 0, 0)))
    res = pl.pallas_call(
        functools.partial(_hgrn_kernel, seq=T, layer=layer, has_s0=has_s0, emit_state=emit_state),
        out_shape=out_shape,
        grid=(B, H_A),
        in_specs=in_specs,
        out_specs=out_specs,
        scratch_shapes=[pltpu.VMEM((T, LANES), F32),
                        pltpu.VMEM((2, 2 * HGRN_ROWS, HGRN_ROWS), BF16),
                        pltpu.VMEM((2, HGRN_ROWS, HGRN_ROWS), F32)],
        compiler_params=_cparams(2),
        name="hgrn2",
    )(*args)
    return res if emit_state else (res[0], None)


def _head_masks():
    lane = lax.broadcasted_iota(jnp.int32, (1, LANES), 1)
    return lane < DH_B, lane >= DH_B


def _ctx_attn_kernel(q_ref, k_ref, v_ref, g_ref, o_ref):
    q, k, v = q_ref[0], k_ref[0], v_ref[0]
    scale = DH_B ** -0.5
    masks = _head_masks()
    outs = []
    for h in range(2):
        qm = jnp.where(masks[h], q, jnp.zeros_like(q))
        s = lax.dot_general(qm, k, NT_DIMS, preferred_element_type=F32) * scale
        m = jnp.max(s, axis=-1, keepdims=True)
        e = jnp.exp(s - m)
        p = e / jnp.sum(e, axis=-1, keepdims=True)
        outs.append(jnp.dot(p.astype(BF16), v, preferred_element_type=F32))
    o = jnp.where(masks[0], outs[0], outs[1])
    o_ref[0] = (o * _silu(g_ref[0].astype(F32))).astype(o_ref.dtype)


def _ctx_attn(yb):
    B, T, _ = yb.shape
    n_pair = H_B // 2

    def col(k):
        return pl.BlockSpec((1, T, LANES), lambda b, p, k=k: (b, 0, k * n_pair + p))

    return pl.pallas_call(
        _ctx_attn_kernel,
        out_shape=jax.ShapeDtypeStruct((B, T, H_B * DH_B), BF16),
        grid=(B, n_pair),
        in_specs=[col(0), col(1), col(2), col(3)],
        out_specs=pl.BlockSpec((1, T, LANES), lambda b, p: (b, 0, p)),
        compiler_params=_cparams(2),
        name="ctx_attn",
    )(yb, yb, yb, yb)


N_DR = 2 * NA_KH - 1
N_DC = 2 * NA_KW - 1
N_TAB = N_DR - 1


def _nat_kernel(rb_ref, q_ref, k_ref, v_ref, g_ref, kc_ref, vc_ref, o_ref, tab_ref, *, rows):
    p = pl.program_id(0)
    scale = DH_B ** -0.5
    kh = min(NA_KH, rows)
    masks = _head_masks()

    @pl.when(pl.program_id(1) == 0)
    def _build_tables():
        c = lax.broadcasted_iota(jnp.int32, (GRID_W, LANES), 0)
        lane = lax.broadcasted_iota(jnp.int32, (GRID_W, LANES), 1)
        kcol = lane & (GRID_W - 1)
        upper = lane >= GRID_W
        ws = jnp.clip(c - NA_KW // 2, 0, GRID_W - NA_KW)
        neg = jnp.full((GRID_W, LANES), NEG_INF, F32)
        diag = kcol - c + (NA_KW - 1)
        for h in range(2):
            base = (2 * p + h) * (N_DR * N_DC)

            def per_dr(i, _, base=base, h=h):
                def per_dc(dd, acc):
                    lo = rb_ref[base + i * N_DC + dd]
                    hi = rb_ref[base + (i + 1) * N_DC + dd]
                    return jnp.where(diag == dd, jnp.where(upper, hi, lo), acc)

                acc = lax.fori_loop(0, N_DC, per_dc, neg)
                acc = jnp.where(kcol >= ws, jnp.where(kcol < ws + NA_KW, acc, neg), neg)
                tab_ref[h, i] = acc
                return 0

            lax.fori_loop(0, N_TAB, per_dr, 0)

    kc, vc = kc_ref[0], vc_ref[0]
    n_keys = kh * GRID_W
    n_q = NA_GROUP * GRID_W

    def group(gi, _):
        r_first = gi * NA_GROUP
        q0 = pl.multiple_of(r_first * GRID_W, n_q)
        qg = q_ref[0, pl.ds(q0, n_q), :]
        outs = []
        for h in range(2):
            qm = jnp.where(masks[h], qg, jnp.zeros_like(qg))
            s_ctx = lax.dot_general(qm, kc, NT_DIMS, preferred_element_type=F32) * scale
            e_ctx, o_loc, denom = [], [], []
            for i in range(NA_GROUP):
                r = r_first + i
                rs = jnp.clip(r - kh // 2, 0, rows - kh)
                k0 = pl.multiple_of(rs * GRID_W, GRID_W)
                kl = k_ref[0, pl.ds(k0, n_keys), :]
                vl = v_ref[0, pl.ds(k0, n_keys), :]
                dr0 = rs - r + (NA_KH - 1)
                bias = jnp.concatenate([tab_ref[h, dr0 + 2 * m] for m in range(kh // 2)], axis=1)
                s_loc = lax.dot_general(qm[i * GRID_W:(i + 1) * GRID_W], kl, NT_DIMS,
                                        preferred_element_type=F32) * scale + bias
                sc = s_ctx[i * GRID_W:(i + 1) * GRID_W]
                m = jnp.maximum(jnp.max(s_loc, axis=-1, keepdims=True), jnp.max(sc, axis=-1, keepdims=True))
                e = jnp.exp(s_loc - m)
                ec = jnp.exp(sc - m)
                denom.append(jnp.sum(e, axis=-1, keepdims=True) + jnp.sum(ec, axis=-1, keepdims=True))
                o_loc.append(jnp.dot(e.astype(BF16), vl, preferred_element_type=F32))
                e_ctx.append(ec.astype(BF16))
            o_c = jnp.dot(jnp.concatenate(e_ctx, axis=0), vc, preferred_element_type=F32)
            outs.append((jnp.concatenate(o_loc, axis=0) + o_c) / jnp.concatenate(denom, axis=0))
        o = jnp.where(masks[0], outs[0], outs[1])
        gate = g_ref[0, pl.ds(q0, n_q), :].astype(F32)
        o_ref[0, pl.ds(q0, n_q), :] = (o * _silu(gate)).astype(o_ref.dtype)
        return 0

    lax.fori_loop(0, rows // NA_GROUP, group, 0)


def _nat(yb, kc, vc, rel_bias):
    B, T, _ = yb.shape
    Tc = kc.shape[1]
    n_pair = H_B // 2
    rows = T // GRID_W

    def col(k):
        return pl.BlockSpec((1, T, LANES), lambda p, b, k=k: (b, 0, k * n_pair + p))

    ctx = pl.BlockSpec((1, Tc, LANES), lambda p, b: (b, 0, p))
    return pl.pallas_call(
        functools.partial(_nat_kernel, rows=rows),
        out_shape=jax.ShapeDtypeStruct((B, T, H_B * DH_B), BF16),
        grid=(n_pair, B),
        in_specs=[pl.BlockSpec(memory_space=pltpu.SMEM), col(0), col(1), col(2), col(3), ctx, ctx],
        out_specs=pl.BlockSpec((1, T, LANES), lambda p, b: (b, 0, p)),
        scratch_shapes=[pltpu.VMEM((2, N_TAB, GRID_W, LANES), F32)],
        compiler_params=_cparams(2),
        name="nbr_attn",
    )(rel_bias.reshape(-1), yb, yb, yb, yb, kc, vc)


def _seg_len(seq):
    length = -(-seq // N_SEG)
    while length % 8 != 4:
        length += 1
    return length


def _rglru_kernel(x_ref, g_ref, cw_ref, cb_ref, wg_ref, bg_ref, lam_ref, *rest, seq, has_s0, emit_state):
    rest = list(rest)
    s0_ref = rest.pop(0) if has_s0 else None
    o_ref = rest.pop(0)
    hfin_ref = rest.pop(0) if emit_state else None
    xpad_ref = rest.pop(0)
    a_refs, u_refs, h_refs, p_refs = rest[0:2], rest[2:4], rest[4:6], rest[6:8]
    L = _seg_len(seq)
    n_pad = N_SEG * L - seq
    RB = RG_ROWS

    xpad_ref[0:8, :] = jnp.zeros((8, LANES), F32)
    xpad_ref[seq + 8:seq + 16, :] = jnp.zeros((8, LANES), F32)
    xpad_ref[8:seq + 8, :] = x_ref[0]
    for d in range(2):
        a_refs[d][seq:seq + n_pad, :] = jnp.ones((n_pad, LANES), F32)
        u_refs[d][seq:seq + n_pad, :] = jnp.zeros((n_pad, LANES), F32)

    nl = -lam_ref[...]
    sp = jnp.maximum(nl, 0.0) + jnp.log1p(jnp.exp(-jnp.abs(nl)))
    cw = cw_ref[...]
    cbias = cb_ref[...]
    bg = bg_ref[0]

    def gates(blk, _):
        r0 = pl.multiple_of(blk * RB, RB)
        xm = xpad_ref[pl.ds(r0, RB + 16), :]
        xc = cw[0:1] * xm[6:6 + RB] + cw[1:2] * xm[7:7 + RB]
        xc = xc + cw[2:3] * xm[8:8 + RB]
        xc = xc + cw[3:4] * xm[9:9 + RB] + cbias
        gt = jnp.dot(xc.astype(BF16), wg_ref[0], preferred_element_type=F32) + bg
        for d in range(2):
            rg = jax.nn.sigmoid(gt[:, (2 * d) * LANES:(2 * d + 1) * LANES])
            ig = jax.nn.sigmoid(gt[:, (2 * d + 1) * LANES:(2 * d + 2) * LANES])
            la = (-RG_C * rg) * sp[d:d + 1, :]
            a = jnp.exp(la)
            u = jnp.sqrt(-jnp.tanh(la) * (1.0 + a * a)) * (ig * xc)
            a_refs[d][pl.ds(r0, RB), :] = a
            u_refs[d][pl.ds(r0, RB), :] = u
        return 0

    lax.fori_loop(0, seq // RB, gates, 0)

    unroll = 4
    finals = []
    for d in range(2):
        a_ref, u_ref, h_ref, p_ref = a_refs[d], u_refs[d], h_refs[d], p_refs[d]

        def scan(i, carry, d=d, a_ref=a_ref, u_ref=u_ref, h_ref=h_ref, p_ref=p_ref):
            h, pr = carry
            for k in range(unroll):
                t = i * unroll + k
                if d == 1:
                    t = L - 1 - t
                idx = pl.ds(t, N_SEG, stride=L)
                a = a_ref[idx, :]
                h = a * h + u_ref[idx, :]
                pr = pr * a
                h_ref[idx, :] = h
                p_ref[idx, :] = pr
            return h, pr

        h_end, p_end = lax.fori_loop(0, L // unroll, scan,
                                     (jnp.zeros((N_SEG, LANES), F32), jnp.ones((N_SEG, LANES), F32)))
        if has_s0:
            c = s0_ref[0, d:d + 1, :]
        else:
            c = jnp.zeros((1, LANES), F32)
        cin = [None] * N_SEG
        for jj in range(N_SEG):
            j = jj if d == 0 else N_SEG - 1 - jj
            cin[j] = c
            c = h_end[j:j + 1, :] + p_end[j:j + 1, :] * c
        finals.append(c)
        cin = jnp.concatenate(cin, axis=0)

        def fix(i, _, h_ref=h_ref, p_ref=p_ref, cin=cin):
            for k in range(unroll):
                idx = pl.ds(i * unroll + k, N_SEG, stride=L)
                h_ref[idx, :] = h_ref[idx, :] + p_ref[idx, :] * cin
            return 0

        lax.fori_loop(0, L // unroll, fix, 0)

    if emit_state:
        hfin_ref[0] = jnp.concatenate(finals, axis=0)

    def combine(blk, _):
        rows = pl.ds(pl.multiple_of(blk * RB, RB), RB)
        y = (h_refs[0][rows, :] + h_refs[1][rows, :]) * _silu(g_ref[0, rows, :])
        o_ref[0, rows, :] = y.astype(o_ref.dtype)
        return 0

    lax.fori_loop(0, seq // RB, combine, 0)


def _rglru(xg, conv_w, conv_b, wg, bg, lam, s0, emit_state):
    B, T, _ = xg.shape
    has_s0 = s0 is not None
    n_rows = N_SEG * _seg_len(T)
    in_specs = [
        pl.BlockSpec((1, T, LANES), lambda b, c: (b, 0, c)),
        pl.BlockSpec((1, T, LANES), lambda b, c: (b, 0, H_C + c)),
        pl.BlockSpec((4, LANES), lambda b, c: (0, c)),
        pl.BlockSpec((1, LANES), lambda b, c: (0, c)),
        pl.BlockSpec((1, BW_C, 4 * BW_C), lambda b, c: (c, 0, 0)),
        pl.BlockSpec((1, 1, 4 * BW_C), lambda b, c: (c, 0, 0)),
        pl.BlockSpec((2, LANES), lambda b, c: (0, c)),
    ]
    args = [xg, xg, conv_w, conv_b, wg, bg, lam]
    if has_s0:
        in_specs.append(pl.BlockSpec((1, 2, LANES), lambda b, c: (b, 0, c)))
        args.append(s0)
    out_shape = [jax.ShapeDtypeStruct((B, T, W_C), BF16)]
    out_specs = [pl.BlockSpec((1, T, LANES), lambda b, c: (b, 0, c))]
    if emit_state:
        out_shape.append(jax.ShapeDtypeStruct((B, 2, W_C), F32))
        out_specs.append(pl.BlockSpec((1, 2, LANES), lambda b, c: (b, 0, c)))
    res = pl.pallas_call(
        functools.partial(_rglru_kernel, seq=T, has_s0=has_s0, emit_state=emit_state),
        out_shape=out_shape,
        grid=(B, H_C),
        in_specs=in_specs,
        out_specs=out_specs,
        scratch_shapes=[pltpu.VMEM((T + 16, LANES), F32)] + [pltpu.VMEM((n_rows, LANES), F32)] * 8,
        compiler_params=_cparams(2),
        name="rglru",
    )(*args)
    return res if emit_state else (res[0], None)


A_COLS = 5 * H_A * DK_A
B_COLS = 4 * H_B * DH_B


def kernel(x_prompt, x_sample, state_hgrn, cache_na_k, cache_na_v, state_rglru, c, c_ctx, norm_gain, w_mod, b_mod, w_in_even, w_out_even, hgrn_lb_logits, hgrn_out_gain, na_rel_bias, w_in_odd, w_out_odd, conv_w, conv_b, rg_gate_w, rg_gate_b, rg_lambda, final_gain):
    n_ctx = x_prompt.shape[0]
    n_lat = x_sample.shape[0]
    depth = w_mod.shape[0]

    cond = jnp.zeros((16, D_MODEL), F32).at[0].set(c_ctx).at[1:1 + n_lat].set(c)
    mod = _modulation(cond, w_mod.astype(BF16), b_mod.reshape(depth, 1, 3 * D_MODEL))
    mod = mod.reshape(depth, 16, 3, D_MODEL)

    xc, xs = x_prompt, x_sample
    new_hgrn, new_k, new_v, new_rg = [], [], [], []
    for l in range(depth):
        j = l // 2
        mod_c, mod_s = mod[l, 0:1], mod[l, 1:1 + n_lat]
        gain = norm_gain[l].reshape(1, D_MODEL)
        last = l == depth - 1
        fgain = final_gain.reshape(1, D_MODEL) if last else None
        if l % 2 == 0:
            w_in = w_in_even[j].astype(BF16)
            w_out = w_out_even[j].astype(BF16)
            outs_s = ((0, A_COLS, F32), (A_COLS, B_COLS, BF16))
            outs_c = outs_s + ((A_COLS + H_B * DH_B, 2 * H_B * DH_B, F32),)
            ya_c, yb_c, kv_c = _inproj(xc, mod_c, gain, w_in, outs_c, 256, True)
            ya_s, yb_s = _inproj(xs, mod_s, gain, w_in, outs_s, 512, False)
            hgain = hgrn_out_gain[j].reshape(H_A, 1, DK_A)
            oa_c, s_fin = _hgrn(ya_c, hgrn_lb_logits, j, hgain, None, True)
            oa_s, _ = _hgrn(ya_s, hgrn_lb_logits, j, hgain, state_hgrn[:, j], False)
            ob_c = _ctx_attn(yb_c)
            tc = cache_na_k.shape[3]
            kc = cache_na_k[:, j].transpose(0, 2, 1, 3).reshape(n_lat, tc, H_B * DH_B).astype(BF16)
            vc = cache_na_v[:, j].transpose(0, 2, 1, 3).reshape(n_lat, tc, H_B * DH_B).astype(BF16)
            ob_s = _nat(yb_s, kc, vc, na_rel_bias[j])
            y_c = jnp.concatenate([oa_c, ob_c], axis=-1)
            y_s = jnp.concatenate([oa_s, ob_s], axis=-1)
            xc = _outproj(y_c, xc, mod_c, w_out, 256, True, fgain)
            xs = _outproj(y_s, xs, mod_s, w_out, 512, False, fgain)
            t_c = kv_c.shape[1]
            heads = kv_c.reshape(n_ctx, t_c, 2, H_B, DH_B).transpose(2, 0, 3, 1, 4)
            new_hgrn.append(s_fin)
            new_k.append(heads[0])
            new_v.append(heads[1])
        else:
            w_in = w_in_odd[j].astype(BF16)
            w_out = w_out_odd[j].astype(BF16)
            outs = ((0, 2 * W_C, F32),)
            (xg_c,) = _inproj(xc, mod_c, gain, w_in, outs, 256, True)
            (xg_s,) = _inproj(xs, mod_s, gain, w_in, outs, 512, False)
            wg = rg_gate_w[j].transpose(2, 3, 0, 1, 4).reshape(H_C, BW_C, 4 * BW_C).astype(BF16)
            bg = rg_gate_b[j].reshape(2, 2, H_C, BW_C).transpose(2, 0, 1, 3).reshape(H_C, 1, 4 * BW_C)
            cb = conv_b[j].reshape(1, W_C)
            y_c, h_fin = _rglru(xg_c, conv_w[j], cb, wg, bg, rg_lambda[j], None, True)
            y_s, _ = _rglru(xg_s, conv_w[j], cb, wg, bg, rg_lambda[j], state_rglru[:, j], False)
            xc = _outproj(y_c, xc, mod_c, w_out, 256, True, fgain)
            xs = _outproj(y_s, xs, mod_s, w_out, 512, False, fgain)
            new_rg.append(h_fin)
    return (xc, xs, jnp.stack(new_hgrn, axis=1), jnp.stack(new_k, axis=1),
            jnp.stack(new_v, axis=1), jnp.stack(new_rg, axis=1))
```

```python
import functools

import jax
import jax.numpy as jnp
from jax import lax
from jax.experimental import pallas as pl
from jax.experimental.pallas import tpu as pltpu

F32 = jnp.float32
BF16 = jnp.bfloat16

D_MODEL = 1024
EPS = 1e-6
NEG_INF = -1e30
H_A = 4
DK_A = 128
HGRN_CHUNK = 32
HGRN_ROWS = 256
H_B = 8
DH_B = 64
GRID_W = 64
NA_KH = 8
NA_KW = 16
NA_GROUP = 4
W_C = 1024
H_C = 8
BW_C = W_C // H_C
RG_C = 8.0
RG_ROWS = 256
N_SEG = 8
LANES = 128
VMEM_LIMIT = 48 * 1024 * 1024

NT_DIMS = (((1,), (1,)), ((), ()))
TN_DIMS = (((0,), (0,)), ((), ()))


def _silu(x):
    return x * jax.nn.sigmoid(x)


def _cparams(n_axes):
    return pltpu.CompilerParams(dimension_semantics=("arbitrary",) * n_axes,
                                vmem_limit_bytes=VMEM_LIMIT)


def _mod_kernel(cond_ref, w_ref, b_ref, o_ref):
    s = _silu(cond_ref[...])
    o_ref[0] = jnp.dot(s.astype(BF16), w_ref[0], preferred_element_type=F32) + b_ref[0]


def _modulation(cond, w_mod, b_mod):
    depth = w_mod.shape[0]
    n_rows = cond.shape[0]
    return pl.pallas_call(
        _mod_kernel,
        out_shape=jax.ShapeDtypeStruct((depth, n_rows, 3 * D_MODEL), F32),
        grid=(depth, 3),
        in_specs=[
            pl.BlockSpec((n_rows, D_MODEL), lambda l, n: (0, 0)),
            pl.BlockSpec((1, D_MODEL, D_MODEL), lambda l, n: (l, 0, n)),
            pl.BlockSpec((1, 1, D_MODEL), lambda l, n: (l, 0, n)),
        ],
        out_specs=pl.BlockSpec((1, n_rows, D_MODEL), lambda l, n: (l, 0, n)),
        compiler_params=_cparams(2),
        name="adaln_mod",
    )(cond, w_mod, b_mod)


def _inproj_kernel(x_ref, mod_ref, gain_ref, w_ref, *out_refs, outs):
    x = x_ref[0]
    var = jnp.mean(x * x, axis=-1, keepdims=True)
    y = x * lax.rsqrt(var + EPS) * gain_ref[...]
    h = y * (1.0 + mod_ref[0, 1:2, :]) + mod_ref[0, 0:1, :]
    hb = h.astype(BF16)
    step = 512
    for o_ref, (col0, width, _) in zip(out_refs, outs):
        for c in range(0, width, step):
            r = jnp.dot(hb, w_ref[:, col0 + c:col0 + c + step], preferred_element_type=F32)
            o_ref[0, :, c:c + step] = r.astype(o_ref.dtype)


def _inproj(x, mod, gain, w, outs, tm, shared_mod):
    B, T, _ = x.shape
    n_cols = w.shape[1]
    mod_map = (lambda b, t: (0, 0, 0)) if shared_mod else (lambda b, t: (b, 0, 0))
    return pl.pallas_call(
        functools.partial(_inproj_kernel, outs=outs),
        out_shape=[jax.ShapeDtypeStruct((B, T, wd), dt) for _, wd, dt in outs],
        grid=(B, T // tm),
        in_specs=[
            pl.BlockSpec((1, tm, D_MODEL), lambda b, t: (b, t, 0)),
            pl.BlockSpec((1, 3, D_MODEL), mod_map),
            pl.BlockSpec((1, D_MODEL), lambda b, t: (0, 0)),
            pl.BlockSpec((D_MODEL, n_cols), lambda b, t: (0, 0)),
        ],
        out_specs=[pl.BlockSpec((1, tm, wd), lambda b, t: (b, t, 0)) for _, wd, _ in outs],
        compiler_params=_cparams(2),
        name="in_proj",
    )(x, mod, gain, w)


def _outproj_kernel(y_ref, x_ref, mod_ref, w_ref, *rest, final):
    m = jnp.dot(y_ref[0], w_ref[...], preferred_element_type=F32)
    xn = x_ref[0] + mod_ref[0, 2:3, :] * m
    if final:
        gain_ref, o_ref = rest
        var = jnp.mean(xn * xn, axis=-1, keepdims=True)
        xn = xn * lax.rsqrt(var + EPS) * gain_ref[...]
    else:
        (o_ref,) = rest
    o_ref[0] = xn


def _outproj(y, x, mod, w, tm, shared_mod, final_gain=None):
    B, T, _ = x.shape
    final = final_gain is not None
    mod_map = (lambda b, t: (0, 0, 0)) if shared_mod else (lambda b, t: (b, 0, 0))
    in_specs = [
        pl.BlockSpec((1, tm, D_MODEL), lambda b, t: (b, t, 0)),
        pl.BlockSpec((1, tm, D_MODEL), lambda b, t: (b, t, 0)),
        pl.BlockSpec((1, 3, D_MODEL), mod_map),
        pl.BlockSpec((D_MODEL, D_MODEL), lambda b, t: (0, 0)),
    ]
    args = [y, x, mod, w]
    if final:
        in_specs.append(pl.BlockSpec((1, D_MODEL), lambda b, t: (0, 0)))
        args.append(final_gain)
    return pl.pallas_call(
        functools.partial(_outproj_kernel, final=final),
        out_shape=jax.ShapeDtypeStruct((B, T, D_MODEL), F32),
        grid=(B, T // tm),
        in_specs=in_specs,
        out_specs=pl.BlockSpec((1, tm, D_MODEL), lambda b, t: (b, t, 0)),
        compiler_params=_cparams(2),
        name="out_proj",
    )(*args)


def _hgrn_kernel(q_ref, zf_ref, zb_ref, v_ref, g_ref, lgt_ref, gain_ref, *rest, seq, layer, has_s0, emit_state):
    rest = list(rest)
    s0_ref = rest.pop(0) if has_s0 else None
    o_ref = rest.pop(0)
    sfin_ref = rest.pop(0) if emit_state else None
    acc_ref, qd_ref, kv_ref, dec_ref, mst_ref, msk_ref = rest
    R = HGRN_ROWS
    C = HGRN_CHUNK
    n_blk = seq // R
    n_chunk = R // C
    n_all = seq // C

    @pl.when((pl.program_id(0) == 0) & (pl.program_id(1) == 0))
    def _build_masks():
        ti = lax.broadcasted_iota(jnp.int32, (R, R), 0)
        tj = lax.broadcasted_iota(jnp.int32, (R, R), 1)
        shift = C.bit_length() - 1
        same = lax.shift_right_logical(ti, shift) == lax.shift_right_logical(tj, shift)
        one = jnp.ones((R, R), F32)
        zero = jnp.zeros((R, R), F32)
        incl = (jnp.where(same, jnp.where(tj <= ti, one, zero), zero),
                jnp.where(same, jnp.where(tj >= ti, one, zero), zero))
        excl = (jnp.where(same, jnp.where(tj > ti, one, zero), zero),
                jnp.where(same, jnp.where(tj < ti, one, zero), zero))
        for d in range(2):
            msk_ref[d] = incl[d]
            mst_ref[d, 0:R, :] = incl[d].astype(BF16)
            mst_ref[d, R:2 * R, :] = excl[d].astype(BF16)

    lgt = [lgt_ref[:, i, :] for i in range(lgt_ref.shape[1])]
    lmax = functools.reduce(jnp.maximum, lgt)
    ex = [jnp.exp(t - lmax) for t in lgt]
    lb_all = sum(ex[:layer + 1]) / sum(ex)
    gain = gain_ref[0]

    def intra(blk, _):
        r0 = pl.multiple_of(blk * R, R)
        rows = pl.ds(r0, R)
        q = q_ref[0, rows, :]
        v = v_ref[0, rows, :]
        vb = v.astype(BF16)
        for d in range(2):
            z = (zf_ref if d == 0 else zb_ref)[0, rows, :]
            lb = lb_all[d:d + 1, :]
            oml = 1.0 - lb
            e = jnp.exp(-jnp.abs(z))
            r = 1.0 / (1.0 + e)
            er = e * r
            pos = z >= 0.0
            f = lb + oml * jnp.where(pos, r, er)
            k = oml * jnp.where(pos, er, r)
            logf = jnp.log(f)
            hi = logf.astype(BF16)
            lo = (logf - hi.astype(F32)).astype(BF16)
            cs = jnp.dot(mst_ref[d], jnp.concatenate([hi, lo], axis=1), preferred_element_type=F32)
            b = cs[0:R, 0:LANES] + cs[0:R, LANES:2 * LANES]
            bex = cs[R:2 * R, 0:LANES] + cs[R:2 * R, LANES:2 * LANES]
            qd = (q * jnp.exp(b)).astype(BF16)
            ki = (k * jnp.exp(-b)).astype(BF16)
            kd = k * jnp.exp(bex)
            att = lax.dot_general(qd, ki, NT_DIMS, preferred_element_type=F32)
            att = jnp.where(msk_ref[d] > 0.5, att, 0.0)
            acc_ref[d, rows, :] = jnp.dot(att.astype(BF16), vb, preferred_element_type=F32)
            qd_ref[d, rows, :] = qd
            btot = b + bex
            for c in range(n_chunk):
                lo_r, hi_r = c * C, (c + 1) * C
                idx = blk * n_chunk + c
                kv_ref[d, idx] = lax.dot_general(v[lo_r:hi_r], kd[lo_r:hi_r], TN_DIMS,
                                                 preferred_element_type=F32)
                dec_ref[d, idx] = jnp.exp(btot[lo_r:lo_r + 8, :])
        return 0

    lax.fori_loop(0, n_blk, intra, 0)

    unroll = 4

    def inter(i, sts):
        sts = list(sts)
        for u in range(unroll):
            n = i * unroll + u
            for d in range(2):
                c = n if d == 0 else n_all - 1 - n
                rows = pl.ds(pl.multiple_of(c * C, C), C)
                acc_ref[d, rows, :] += lax.dot_general(qd_ref[d, rows, :], sts[d].astype(BF16), NT_DIMS,
                                                       preferred_element_type=F32)
                dec = jnp.concatenate([dec_ref[d, c]] * (DK_A // 8), axis=0)
                sts[d] = sts[d] * dec + kv_ref[d, c]
        return tuple(sts)

    if has_s0:
        st0 = (s0_ref[0, 0, 0].T, s0_ref[0, 1, 0].T)
    else:
        st0 = (jnp.zeros((DK_A, DK_A), F32),) * 2
    sts = lax.fori_loop(0, n_all // unroll, inter, st0)
    if emit_state:
        for d in range(2):
            sfin_ref[0, d, 0] = sts[d].T

    def finish(blk, _):
        rows = pl.ds(pl.multiple_of(blk * R, R), R)
        tot = acc_ref[0, rows, :] + acc_ref[1, rows, :]
        var = jnp.mean(tot * tot, axis=-1, keepdims=True)
        y = tot * lax.rsqrt(var + EPS) * gain
        o_ref[0, rows, :] = (y * _silu(g_ref[0, rows, :])).astype(o_ref.dtype)
        return 0

    lax.fori_loop(0, n_blk, finish, 0)


def _hgrn(ya, lgt, layer, gain, s0, emit_state):
    B, T, _ = ya.shape
    has_s0 = s0 is not None

    def col(k):
        return pl.BlockSpec((1, T, LANES), lambda b, h, k=k: (b, 0, k * H_A + h))

    in_specs = [col(0), col(1), col(2), col(3), col(4),
                pl.BlockSpec((2, lgt.shape[1], LANES), lambda b, h: (0, 0, h)),
                pl.BlockSpec((1, 1, LANES), lambda b, h: (h, 0, 0))]
    args = [ya, ya, ya, ya, ya, lgt, gain]
    if has_s0:
        in_specs.append(pl.BlockSpec((1, 2, 1, DK_A, DK_A), lambda b, h: (b, 0, h, 0, 0)))
        args.append(s0)
    out_shape = [jax.ShapeDtypeStruct((B, T, H_A * DK_A), BF16)]
    out_specs = [pl.BlockSpec((1, T, LANES), lambda b, h: (b, 0, h))]
    if emit_state:
        out_shape.append(jax.ShapeDtypeStruct((B, 2, H_A, DK_A, DK_A), F32))
        out_specs.append(pl.BlockSpec((1, 2, 1, DK_A, DK_A), lambda b, h: (b, 0, h, 0, 0)))
    res = pl.pallas_call(
        functools.partial(_hgrn_kernel, seq=T, layer=layer, has_s0=has_s0, emit_state=emit_state),
        out_shape=out_shape,
        grid=(B, H_A),
        in_specs=in_specs,
        out_specs=out_specs,
        scratch_shapes=[pltpu.VMEM((2, T, LANES), F32),
                        pltpu.VMEM((2, T, LANES), BF16),
                        pltpu.VMEM((2, T // HGRN_CHUNK, DK_A, DK_A), F32),
                        pltpu.VMEM((2, T // HGRN_CHUNK, 8, LANES), F32),
                        pltpu.VMEM((2, 2 * HGRN_ROWS, HGRN_ROWS), BF16),
                        pltpu.VMEM((2, HGRN_ROWS, HGRN_ROWS), F32)],
        compiler_params=_cparams(2),
        name="hgrn2",
    )(*args)
    return res if emit_state else (res[0], None)


def _head_masks():
    lane = lax.broadcasted_iota(jnp.int32, (1, LANES), 1)
    return lane < DH_B, lane >= DH_B


def _ctx_attn_kernel(q_ref, k_ref, v_ref, g_ref, o_ref):
    q, k, v = q_ref[0], k_ref[0], v_ref[0]
    scale = DH_B ** -0.5
    masks = _head_masks()
    outs = []
    for h in range(2):
        qm = jnp.where(masks[h], q, jnp.zeros_like(q))
        s = lax.dot_general(qm, k, NT_DIMS, preferred_element_type=F32) * scale
        m = jnp.max(s, axis=-1, keepdims=True)
        e = jnp.exp(s - m)
        p = e / jnp.sum(e, axis=-1, keepdims=True)
        outs.append(jnp.dot(p.astype(BF16), v, preferred_element_type=F32))
    o = jnp.where(masks[0], outs[0], outs[1])
    o_ref[0] = (o * _silu(g_ref[0].astype(F32))).astype(o_ref.dtype)


def _ctx_attn(yb):
    B, T, _ = yb.shape
    n_pair = H_B // 2

    def col(k):
        return pl.BlockSpec((1, T, LANES), lambda b, p, k=k: (b, 0, k * n_pair + p))

    return pl.pallas_call(
        _ctx_attn_kernel,
        out_shape=jax.ShapeDtypeStruct((B, T, H_B * DH_B), BF16),
        grid=(B, n_pair),
        in_specs=[col(0), col(1), col(2), col(3)],
        out_specs=pl.BlockSpec((1, T, LANES), lambda b, p: (b, 0, p)),
        compiler_params=_cparams(2),
        name="ctx_attn",
    )(yb, yb, yb, yb)


N_DR = 2 * NA_KH - 1
N_DC = 2 * NA_KW - 1
N_TAB = N_DR - 1


def _nat_kernel(rb_ref, q_ref, k_ref, v_ref, g_ref, kc_ref, vc_ref, o_ref, tab_ref, *, rows):
    p = pl.program_id(0)
    scale = DH_B ** -0.5
    kh = min(NA_KH, rows)
    masks = _head_masks()

    @pl.when(pl.program_id(1) == 0)
    def _build_tables():
        c = lax.broadcasted_iota(jnp.int32, (GRID_W, LANES), 0)
        lane = lax.broadcasted_iota(jnp.int32, (GRID_W, LANES), 1)
        kcol = lane & (GRID_W - 1)
        upper = lane >= GRID_W
        ws = jnp.clip(c - NA_KW // 2, 0, GRID_W - NA_KW)
        neg = jnp.full((GRID_W, LANES), NEG_INF, F32)
        diag = kcol - c + (NA_KW - 1)
        for h in range(2):
            base = (2 * p + h) * (N_DR * N_DC)

            def per_dr(i, _, base=base, h=h):
                def per_dc(dd, acc):
                    lo = rb_ref[base + i * N_DC + dd]
                    hi = rb_ref[base + (i + 1) * N_DC + dd]
                    return jnp.where(diag == dd, jnp.where(upper, hi, lo), acc)

                acc = lax.fori_loop(0, N_DC, per_dc, neg)
                acc = jnp.where(kcol >= ws, jnp.where(kcol < ws + NA_KW, acc, neg), neg)
                tab_ref[h, i] = acc
                return 0

            lax.fori_loop(0, N_TAB, per_dr, 0)

    kc, vc = kc_ref[0], vc_ref[0]
    n_keys = kh * GRID_W
    n_q = NA_GROUP * GRID_W

    def group(gi, _):
        r_first = gi * NA_GROUP
        q0 = pl.multiple_of(r_first * GRID_W, n_q)
        qg = q_ref[0, pl.ds(q0, n_q), :]
        outs = []
        for h in range(2):
            qm = jnp.where(masks[h], qg, jnp.zeros_like(qg))
            s_ctx = lax.dot_general(qm, kc, NT_DIMS, preferred_element_type=F32) * scale
            e_ctx, o_loc, denom = [], [], []
            for i in range(NA_GROUP):
                r = r_first + i
                rs = jnp.clip(r - kh // 2, 0, rows - kh)
                k0 = pl.multiple_of(rs * GRID_W, GRID_W)
                kl = k_ref[0, pl.ds(k0, n_keys), :]
                vl = v_ref[0, pl.ds(k0, n_keys), :]
                dr0 = rs - r + (NA_KH - 1)
                bias = jnp.concatenate([tab_ref[h, dr0 + 2 * m] for m in range(kh // 2)], axis=1)
                s_loc = lax.dot_general(qm[i * GRID_W:(i + 1) * GRID_W], kl, NT_DIMS,
                                        preferred_element_type=F32) * scale + bias
                sc = s_ctx[i * GRID_W:(i + 1) * GRID_W]
                m = jnp.maximum(jnp.max(s_loc, axis=-1, keepdims=True), jnp.max(sc, axis=-1, keepdims=True))
                e = jnp.exp(s_loc - m)
                ec = jnp.exp(sc - m)
                denom.append(jnp.sum(e, axis=-1, keepdims=True) + jnp.sum(ec, axis=-1, keepdims=True))
                o_loc.append(jnp.dot(e.astype(BF16), vl, preferred_element_type=F32))
                e_ctx.append(ec.astype(BF16))
            o_c = jnp.dot(jnp.concatenate(e_ctx, axis=0), vc, preferred_element_type=F32)
            outs.append((jnp.concatenate(o_loc, axis=0) + o_c) / jnp.concatenate(denom, axis=0))
        o = jnp.where(masks[0], outs[0], outs[1])
        gate = g_ref[0, pl.ds(q0, n_q), :].astype(F32)
        o_ref[0, pl.ds(q0, n_q), :] = (o * _silu(gate)).astype(o_ref.dtype)
        return 0

    lax.fori_loop(0, rows // NA_GROUP, group, 0)


def _nat(yb, kc, vc, rel_bias):
    B, T, _ = yb.shape
    Tc = kc.shape[1]
    n_pair = H_B // 2
    rows = T // GRID_W

    def col(k):
        return pl.BlockSpec((1, T, LANES), lambda p, b, k=k: (b, 0, k * n_pair + p))

    ctx = pl.BlockSpec((1, Tc, LANES), lambda p, b: (b, 0, p))
    return pl.pallas_call(
        functools.partial(_nat_kernel, rows=rows),
        out_shape=jax.ShapeDtypeStruct((B, T, H_B * DH_B), BF16),
        grid=(n_pair, B),
        in_specs=[pl.BlockSpec(memory_space=pltpu.SMEM), col(0), col(1), col(2), col(3), ctx, ctx],
        out_specs=pl.BlockSpec((1, T, LANES), lambda p, b: (b, 0, p)),
        scratch_shapes=[pltpu.VMEM((2, N_TAB, GRID_W, LANES), F32)],
        compiler_params=_cparams(2),
        name="nbr_attn",
    )(rel_bias.reshape(-1), yb, yb, yb, yb, kc, vc)


def _seg_len(seq):
    length = -(-seq // N_SEG)
    while length % 8 != 4:
        length += 1
    return length


def _rglru_kernel(x_ref, g_ref, cw_ref, cb_ref, wg_ref, bg_ref, lam_ref, *rest, seq, has_s0, emit_state):
    rest = list(rest)
    s0_ref = rest.pop(0) if has_s0 else None
    o_ref = rest.pop(0)
    hfin_ref = rest.pop(0) if emit_state else None
    xpad_ref = rest.pop(0)
    a_refs, u_refs, h_refs, p_refs = rest[0:2], rest[2:4], rest[4:6], rest[6:8]
    L = _seg_len(seq)
    n_pad = N_SEG * L - seq
    RB = RG_ROWS

    xpad_ref[0:8, :] = jnp.zeros((8, LANES), F32)
    xpad_ref[seq + 8:seq + 16, :] = jnp.zeros((8, LANES), F32)
    xpad_ref[8:seq + 8, :] = x_ref[0]
    for d in range(2):
        a_refs[d][seq:seq + n_pad, :] = jnp.ones((n_pad, LANES), F32)
        u_refs[d][seq:seq + n_pad, :] = jnp.zeros((n_pad, LANES), F32)

    nl = -lam_ref[...]
    sp = jnp.maximum(nl, 0.0) + jnp.log1p(jnp.exp(-jnp.abs(nl)))
    cw = cw_ref[...]
    cbias = cb_ref[...]
    bg = bg_ref[0]

    def gates(blk, _):
        r0 = pl.multiple_of(blk * RB, RB)
        xm = xpad_ref[pl.ds(r0, RB + 16), :]
        xc = cw[0:1] * xm[6:6 + RB] + cw[1:2] * xm[7:7 + RB]
        xc = xc + cw[2:3] * xm[8:8 + RB]
        xc = xc + cw[3:4] * xm[9:9 + RB] + cbias
        gt = jnp.dot(xc.astype(BF16), wg_ref[0], preferred_element_type=F32) + bg
        for d in range(2):
            rg = jax.nn.sigmoid(gt[:, (2 * d) * LANES:(2 * d + 1) * LANES])
            ig = jax.nn.sigmoid(gt[:, (2 * d + 1) * LANES:(2 * d + 2) * LANES])
            la = (-RG_C * rg) * sp[d:d + 1, :]
            a = jnp.exp(la)
            u = jnp.sqrt(-jnp.tanh(la) * (1.0 + a * a)) * (ig * xc)
            a_refs[d][pl.ds(r0, RB), :] = a
            u_refs[d][pl.ds(r0, RB), :] = u
        return 0

    lax.fori_loop(0, seq // RB, gates, 0)

    unroll = 4
    finals = []
    for d in range(2):
        a_ref, u_ref, h_ref, p_ref = a_refs[d], u_refs[d], h_refs[d], p_refs[d]

        def scan(i, carry, d=d, a_ref=a_ref, u_ref=u_ref, h_ref=h_ref, p_ref=p_ref):
            h, pr = carry
            for k in range(unroll):
                t = i * unroll + k
                if d == 1:
                    t = L - 1 - t
                idx = pl.ds(t, N_SEG, stride=L)
                a = a_ref[idx, :]
                h = a * h + u_ref[idx, :]
                pr = pr * a
                h_ref[idx, :] = h
                p_ref[idx, :] = pr
            return h, pr

        h_end, p_end = lax.fori_loop(0, L // unroll, scan,
                                     (jnp.zeros((N_SEG, LANES), F32), jnp.ones((N_SEG, LANES), F32)))
        if has_s0:
            c = s0_ref[0, d:d + 1, :]
        else:
            c = jnp.zeros((1, LANES), F32)
        cin = [None] * N_SEG
        for jj in range(N_SEG):
            j = jj if d == 0 else N_SEG - 1 - jj
            cin[j] = c
            c = h_end[j:j + 1, :] + p_end[j:j + 1, :] * c
        finals.append(c)
        cin = jnp.concatenate(cin, axis=0)

        def fix(i, _, h_ref=h_ref, p_ref=p_ref, cin=cin):
            for k in range(unroll):
                idx = pl.ds(i * unroll + k, N_SEG, stride=L)
                h_ref[idx, :] = h_ref[idx, :] + p_ref[idx, :] * cin
            return 0

        lax.fori_loop(0, L // unroll, fix, 0)

    if emit_state:
        hfin_ref[0] = jnp.concatenate(finals, axis=0)

    def combine(blk, _):
        rows = pl.ds(pl.multiple_of(blk * RB, RB), RB)
        y = (h_refs[0][rows, :] + h_refs[1][rows, :]) * _silu(g_ref[0, rows, :])
        o_ref[0, rows, :] = y.astype(o_ref.dtype)
        return 0

    lax.fori_loop(0, seq // RB, combine, 0)


def _rglru(xg, conv_w, conv_b, wg, bg, lam, s0, emit_state):
    B, T, _ = xg.shape
    has_s0 = s0 is not None
    n_rows = N_SEG * _seg_len(T)
    in_specs = [
        pl.BlockSpec((1, T, LANES), lambda b, c: (b, 0, c)),
        pl.BlockSpec((1, T, LANES), lambda b, c: (b, 0, H_C + c)),
        pl.BlockSpec((4, LANES), lambda b, c: (0, c)),
        pl.BlockSpec((1, LANES), lambda b, c: (0, c)),
        pl.BlockSpec((1, BW_C, 4 * BW_C), lambda b, c: (c, 0, 0)),
        pl.BlockSpec((1, 1, 4 * BW_C), lambda b, c: (c, 0, 0)),
        pl.BlockSpec((2, LANES), lambda b, c: (0, c)),
    ]
    args = [xg, xg, conv_w, conv_b, wg, bg, lam]
    if has_s0:
        in_specs.append(pl.BlockSpec((1, 2, LANES), lambda b, c: (b, 0, c)))
        args.append(s0)
    out_shape = [jax.ShapeDtypeStruct((B, T, W_C), BF16)]
    out_specs = [pl.BlockSpec((1, T, LANES), lambda b, c: (b, 0, c))]
    if emit_state:
        out_shape.append(jax.ShapeDtypeStruct((B, 2, W_C), F32))
        out_specs.append(pl.BlockSpec((1, 2, LANES), lambda b, c: (b, 0, c)))
    res = pl.pallas_call(
        functools.partial(_rglru_kernel, seq=T, has_s0=has_s0, emit_state=emit_state),
        out_shape=out_shape,
        grid=(B, H_C),
        in_specs=in_specs,
        out_specs=out_specs,
        scratch_shapes=[pltpu.VMEM((T + 16, LANES), F32)] + [pltpu.VMEM((n_rows, LANES), F32)] * 8,
        compiler_params=_cparams(2),
        name="rglru",
    )(*args)
    return res if emit_state else (res[0], None)


A_COLS = 5 * H_A * DK_A
B_COLS = 4 * H_B * DH_B


def kernel(x_prompt, x_sample, state_hgrn, cache_na_k, cache_na_v, state_rglru, c, c_ctx, norm_gain, w_mod, b_mod, w_in_even, w_out_even, hgrn_lb_logits, hgrn_out_gain, na_rel_bias, w_in_odd, w_out_odd, conv_w, conv_b, rg_gate_w, rg_gate_b, rg_lambda, final_gain):
    n_ctx = x_prompt.shape[0]
    n_lat = x_sample.shape[0]
    depth = w_mod.shape[0]

    cond = jnp.zeros((16, D_MODEL), F32).at[0].set(c_ctx).at[1:1 + n_lat].set(c)
    mod = _modulation(cond, w_mod.astype(BF16), b_mod.reshape(depth, 1, 3 * D_MODEL))
    mod = mod.reshape(depth, 16, 3, D_MODEL)

    xc, xs = x_prompt, x_sample
    new_hgrn, new_k, new_v, new_rg = [], [], [], []
    for l in range(depth):
        j = l // 2
        mod_c, mod_s = mod[l, 0:1], mod[l, 1:1 + n_lat]
        gain = norm_gain[l].reshape(1, D_MODEL)
        last = l == depth - 1
        fgain = final_gain.reshape(1, D_MODEL) if last else None
        if l % 2 == 0:
            w_in = w_in_even[j].astype(BF16)
            w_out = w_out_even[j].astype(BF16)
            outs_s = ((0, A_COLS, F32), (A_COLS, B_COLS, BF16))
            outs_c = outs_s + ((A_COLS + H_B * DH_B, 2 * H_B * DH_B, F32),)
            ya_c, yb_c, kv_c = _inproj(xc, mod_c, gain, w_in, outs_c, 256, True)
            ya_s, yb_s = _inproj(xs, mod_s, gain, w_in, outs_s, 512, False)
            hgain = hgrn_out_gain[j].reshape(H_A, 1, DK_A)
            oa_c, s_fin = _hgrn(ya_c, hgrn_lb_logits, j, hgain, None, True)
            oa_s, _ = _hgrn(ya_s, hgrn_lb_logits, j, hgain, state_hgrn[:, j], False)
            ob_c = _ctx_attn(yb_c)
            tc = cache_na_k.shape[3]
            kc = cache_na_k[:, j].transpose(0, 2, 1, 3).reshape(n_lat, tc, H_B * DH_B).astype(BF16)
            vc = cache_na_v[:, j].transpose(0, 2, 1, 3).reshape(n_lat, tc, H_B * DH_B).astype(BF16)
            ob_s = _nat(yb_s, kc, vc, na_rel_bias[j])
            y_c = jnp.concatenate([oa_c, ob_c], axis=-1)
            y_s = jnp.concatenate([oa_s, ob_s], axis=-1)
            xc = _outproj(y_c, xc, mod_c, w_out, 256, True, fgain)
            xs = _outproj(y_s, xs, mod_s, w_out, 512, False, fgain)
            t_c = kv_c.shape[1]
            heads = kv_c.reshape(n_ctx, t_c, 2, H_B, DH_B).transpose(2, 0, 3, 1, 4)
            new_hgrn.append(s_fin)
            new_k.append(heads[0])
            new_v.append(heads[1])
        else:
            w_in = w_in_odd[j].astype(BF16)
            w_out = w_out_odd[j].astype(BF16)
            outs = ((0, 2 * W_C, F32),)
            (xg_c,) = _inproj(xc, mod_c, gain, w_in, outs, 256, True)
            (xg_s,) = _inproj(xs, mod_s, gain, w_in, outs, 512, False)
            wg = rg_gate_w[j].transpose(2, 3, 0, 1, 4).reshape(H_C, BW_C, 4 * BW_C).astype(BF16)
            bg = rg_gate_b[j].reshape(2, 2, H_C, BW_C).transpose(2, 0, 1, 3).reshape(H_C, 1, 4 * BW_C)
            cb = conv_b[j].reshape(1, W_C)
            y_c, h_fin = _rglru(xg_c, conv_w[j], cb, wg, bg, rg_lambda[j], None, True)
            y_s, _ = _rglru(xg_s, conv_w[j], cb, wg, bg, rg_lambda[j], state_rglru[:, j], False)
            xc = _outproj(y_c, xc, mod_c, w_out, 256, True, fgain)
            xs = _outproj(y_s, xs, mod_s, w_out, 512, False, fgain)
            new_rg.append(h_fin)
    return (xc, xs, jnp.stack(new_hgrn, axis=1), jnp.stack(new_k, axis=1),
            jnp.stack(new_v, axis=1), jnp.stack(new_rg, axis=1))
```

```python
import functools

import jax
import jax.numpy as jnp
from jax import lax
from jax.experimental import pallas as pl
from jax.experimental.pallas import tpu as pltpu

F32 = jnp.float32
BF16 = jnp.bfloat16

D_MODEL = 1024
EPS = 1e-6
NEG_INF = -1e30
H_A = 4
DK_A = 128
HGRN_CHUNK = 32
HGRN_ROWS = 256
H_B = 8
DH_B = 64
GRID_W = 64
NA_KH = 8
NA_KW = 16
NA_GROUP = 8
W_C = 1024
H_C = 8
BW_C = W_C // H_C
RG_C = 8.0
RG_ROWS = 256
N_SEG = 8
LANES = 128
VMEM_LIMIT = 48 * 1024 * 1024

NT_DIMS = (((1,), (1,)), ((), ()))
TN_DIMS = (((0,), (0,)), ((), ()))


def _silu(x):
    return x * jax.nn.sigmoid(x)


def _cparams(n_axes):
    return pltpu.CompilerParams(dimension_semantics=("arbitrary",) * n_axes,
                                vmem_limit_bytes=VMEM_LIMIT)


def _mod_kernel(cond_ref, w_ref, b_ref, o_ref):
    s = _silu(cond_ref[...])
    o_ref[0] = jnp.dot(s.astype(BF16), w_ref[0], preferred_element_type=F32) + b_ref[0]


def _modulation(cond, w_mod, b_mod):
    depth = w_mod.shape[0]
    n_rows = cond.shape[0]
    return pl.pallas_call(
        _mod_kernel,
        out_shape=jax.ShapeDtypeStruct((depth, n_rows, 3 * D_MODEL), F32),
        grid=(depth, 3),
        in_specs=[
            pl.BlockSpec((n_rows, D_MODEL), lambda l, n: (0, 0)),
            pl.BlockSpec((1, D_MODEL, D_MODEL), lambda l, n: (l, 0, n)),
            pl.BlockSpec((1, 1, D_MODEL), lambda l, n: (l, 0, n)),
        ],
        out_specs=pl.BlockSpec((1, n_rows, D_MODEL), lambda l, n: (l, 0, n)),
        compiler_params=_cparams(2),
        name="adaln_mod",
    )(cond, w_mod, b_mod)


def _inproj_kernel(x_ref, mod_ref, gain_ref, w_ref, *out_refs, outs):
    x = x_ref[0]
    var = jnp.mean(x * x, axis=-1, keepdims=True)
    y = x * lax.rsqrt(var + EPS) * gain_ref[...]
    h = y * (1.0 + mod_ref[0, 1:2, :]) + mod_ref[0, 0:1, :]
    hb = h.astype(BF16)
    step = 512
    for o_ref, (col0, width, _) in zip(out_refs, outs):
        for c in range(0, width, step):
            r = jnp.dot(hb, w_ref[:, col0 + c:col0 + c + step], preferred_element_type=F32)
            o_ref[0, :, c:c + step] = r.astype(o_ref.dtype)


def _inproj(x, mod, gain, w, outs, tm, shared_mod):
    B, T, _ = x.shape
    n_cols = w.shape[1]
    mod_map = (lambda b, t: (0, 0, 0)) if shared_mod else (lambda b, t: (b, 0, 0))
    return pl.pallas_call(
        functools.partial(_inproj_kernel, outs=outs),
        out_shape=[jax.ShapeDtypeStruct((B, T, wd), dt) for _, wd, dt in outs],
        grid=(B, T // tm),
        in_specs=[
            pl.BlockSpec((1, tm, D_MODEL), lambda b, t: (b, t, 0)),
            pl.BlockSpec((1, 3, D_MODEL), mod_map),
            pl.BlockSpec((1, D_MODEL), lambda b, t: (0, 0)),
            pl.BlockSpec((D_MODEL, n_cols), lambda b, t: (0, 0)),
        ],
        out_specs=[pl.BlockSpec((1, tm, wd), lambda b, t: (b, t, 0)) for _, wd, _ in outs],
        compiler_params=_cparams(2),
        name="in_proj",
    )(x, mod, gain, w)


def _outproj_kernel(y_ref, x_ref, mod_ref, w_ref, *rest, final):
    m = jnp.dot(y_ref[0], w_ref[...], preferred_element_type=F32)
    xn = x_ref[0] + mod_ref[0, 2:3, :] * m
    if final:
        gain_ref, o_ref = rest
        var = jnp.mean(xn * xn, axis=-1, keepdims=True)
        xn = xn * lax.rsqrt(var + EPS) * gain_ref[...]
    else:
        (o_ref,) = rest
    o_ref[0] = xn


def _outproj(y, x, mod, w, tm, shared_mod, final_gain=None):
    B, T, _ = x.shape
    final = final_gain is not None
    mod_map = (lambda b, t: (0, 0, 0)) if shared_mod else (lambda b, t: (b, 0, 0))
    in_specs = [
        pl.BlockSpec((1, tm, D_MODEL), lambda b, t: (b, t, 0)),
        pl.BlockSpec((1, tm, D_MODEL), lambda b, t: (b, t, 0)),
        pl.BlockSpec((1, 3, D_MODEL), mod_map),
        pl.BlockSpec((D_MODEL, D_MODEL), lambda b, t: (0, 0)),
    ]
    args = [y, x, mod, w]
    if final:
        in_specs.append(pl.BlockSpec((1, D_MODEL), lambda b, t: (0, 0)))
        args.append(final_gain)
    return pl.pallas_call(
        functools.partial(_outproj_kernel, final=final),
        out_shape=jax.ShapeDtypeStruct((B, T, D_MODEL), F32),
        grid=(B, T // tm),
        in_specs=in_specs,
        out_specs=pl.BlockSpec((1, tm, D_MODEL), lambda b, t: (b, t, 0)),
        compiler_params=_cparams(2),
        name="out_proj",
    )(*args)


def _hgrn_kernel(q_ref, zf_ref, zb_ref, v_ref, g_ref, lgt_ref, gain_ref, *rest, seq, layer, has_s0, emit_state):
    rest = list(rest)
    s0_ref = rest.pop(0) if has_s0 else None
    o_ref = rest.pop(0)
    sfin_ref = rest.pop(0) if emit_state else None
    acc_ref, qd_ref, kv_ref, dec_ref, mst_ref, msk_ref = rest
    R = HGRN_ROWS
    C = HGRN_CHUNK
    n_blk = seq // R
    n_chunk = R // C
    n_all = seq // C

    @pl.when((pl.program_id(0) == 0) & (pl.program_id(1) == 0))
    def _build_masks():
        ti = lax.broadcasted_iota(jnp.int32, (R, R), 0)
        tj = lax.broadcasted_iota(jnp.int32, (R, R), 1)
        shift = C.bit_length() - 1
        same = lax.shift_right_logical(ti, shift) == lax.shift_right_logical(tj, shift)
        one = jnp.ones((R, R), F32)
        zero = jnp.zeros((R, R), F32)
        incl = (jnp.where(same, jnp.where(tj <= ti, one, zero), zero),
                jnp.where(same, jnp.where(tj >= ti, one, zero), zero))
        excl = (jnp.where(same, jnp.where(tj > ti, one, zero), zero),
                jnp.where(same, jnp.where(tj < ti, one, zero), zero))
        for d in range(2):
            msk_ref[d] = incl[d]
            mst_ref[d, 0:R, :] = incl[d].astype(BF16)
            mst_ref[d, R:2 * R, :] = excl[d].astype(BF16)

    lgt = [lgt_ref[:, i, :] for i in range(lgt_ref.shape[1])]
    lmax = functools.reduce(jnp.maximum, lgt)
    ex = [jnp.exp(t - lmax) for t in lgt]
    lb_all = sum(ex[:layer + 1]) / sum(ex)
    gain = gain_ref[0]

    def intra(blk, _):
        r0 = pl.multiple_of(blk * R, R)
        rows = pl.ds(r0, R)
        q = q_ref[0, rows, :]
        v = v_ref[0, rows, :]
        vb = v.astype(BF16)
        for d in range(2):
            z = (zf_ref if d == 0 else zb_ref)[0, rows, :]
            lb = lb_all[d:d + 1, :]
            oml = 1.0 - lb
            e = jnp.exp(-jnp.abs(z))
            r = 1.0 / (1.0 + e)
            er = e * r
            pos = z >= 0.0
            f = lb + oml * jnp.where(pos, r, er)
            k = oml * jnp.where(pos, er, r)
            logf = jnp.log(f)
            hi = logf.astype(BF16)
            lo = (logf - hi.astype(F32)).astype(BF16)
            cs = jnp.dot(mst_ref[d], jnp.concatenate([hi, lo], axis=1), preferred_element_type=F32)
            b = cs[0:R, 0:LANES] + cs[0:R, LANES:2 * LANES]
            bex = cs[R:2 * R, 0:LANES] + cs[R:2 * R, LANES:2 * LANES]
            qd = (q * jnp.exp(b)).astype(BF16)
            ki = (k * jnp.exp(-b)).astype(BF16)
            kd = k * jnp.exp(bex)
            att = lax.dot_general(qd, ki, NT_DIMS, preferred_element_type=F32)
            att = jnp.where(msk_ref[d] > 0.5, att, 0.0)
            acc_ref[d, rows, :] = jnp.dot(att.astype(BF16), vb, preferred_element_type=F32)
            qd_ref[d, rows, :] = qd
            btot = b + bex
            for c in range(n_chunk):
                lo_r, hi_r = c * C, (c + 1) * C
                idx = blk * n_chunk + c
                kv_ref[d, idx] = lax.dot_general(v[lo_r:hi_r], kd[lo_r:hi_r], TN_DIMS,
                                                 preferred_element_type=F32)
                dec_ref[d, idx] = jnp.exp(btot[lo_r:lo_r + 8, :])
        return 0

    lax.fori_loop(0, n_blk, intra, 0)

    unroll = 4

    def inter(i, sts):
        sts = list(sts)
        for u in range(unroll):
            n = i * unroll + u
            for d in range(2):
                c = n if d == 0 else n_all - 1 - n
                rows = pl.ds(pl.multiple_of(c * C, C), C)
                acc_ref[d, rows, :] += lax.dot_general(qd_ref[d, rows, :], sts[d].astype(BF16), NT_DIMS,
                                                       preferred_element_type=F32)
                dec = jnp.concatenate([dec_ref[d, c]] * (DK_A // 8), axis=0)
                sts[d] = sts[d] * dec + kv_ref[d, c]
        return tuple(sts)

    if has_s0:
        st0 = (s0_ref[0, 0, 0].T, s0_ref[0, 1, 0].T)
    else:
        st0 = (jnp.zeros((DK_A, DK_A), F32),) * 2
    sts = lax.fori_loop(0, n_all // unroll, inter, st0)
    if emit_state:
        for d in range(2):
            sfin_ref[0, d, 0] = sts[d].T

    def finish(blk, _):
        rows = pl.ds(pl.multiple_of(blk * R, R), R)
        tot = acc_ref[0, rows, :] + acc_ref[1, rows, :]
        var = jnp.mean(tot * tot, axis=-1, keepdims=True)
        y = tot * lax.rsqrt(var + EPS) * gain
        o_ref[0, rows, :] = (y * _silu(g_ref[0, rows, :])).astype(o_ref.dtype)
        return 0

    lax.fori_loop(0, n_blk, finish, 0)


def _hgrn(ya, lgt, layer, gain, s0, emit_state):
    B, T, _ = ya.shape
    has_s0 = s0 is not None

    def col(k):
        return pl.BlockSpec((1, T, LANES), lambda b, h, k=k: (b, 0, k * H_A + h))

    in_specs = [col(0), col(1), col(2), col(3), col(4),
                pl.BlockSpec((2, lgt.shape[1], LANES), lambda b, h: (0, 0, h)),
                pl.BlockSpec((1, 1, LANES), lambda b, h: (h, 0, 0))]
    args = [ya, ya, ya, ya, ya, lgt, gain]
    if has_s0:
        in_specs.append(pl.BlockSpec((1, 2, 1, DK_A, DK_A), lambda b, h: (b, 0, h, 0, 0)))
        args.append(s0)
    out_shape = [jax.ShapeDtypeStruct((B, T, H_A * DK_A), BF16)]
    out_specs = [pl.BlockSpec((1, T, LANES), lambda b, h: (b, 0, h))]
    if emit_state:
        out_shape.append(jax.ShapeDtypeStruct((B, 2, H_A, DK_A, DK_A), F32))
        out_specs.append(pl.BlockSpec((1, 2, 1, DK_A, DK_A), lambda b, h: (b, 0, h, 0, 0)))
    res = pl.pallas_call(
        functools.partial(_hgrn_kernel, seq=T, layer=layer, has_s0=has_s0, emit_state=emit_state),
        out_shape=out_shape,
        grid=(B, H_A),
        in_specs=in_specs,
        out_specs=out_specs,
        scratch_shapes=[pltpu.VMEM((2, T, LANES), F32),
                        pltpu.VMEM((2, T, LANES), BF16),
                        pltpu.VMEM((2, T // HGRN_CHUNK, DK_A, DK_A), F32),
                        pltpu.VMEM((2, T // HGRN_CHUNK, 8, LANES), F32),
                        pltpu.VMEM((2, 2 * HGRN_ROWS, HGRN_ROWS), BF16),
                        pltpu.VMEM((2, HGRN_ROWS, HGRN_ROWS), F32)],
        compiler_params=_cparams(2),
        name="hgrn2",
    )(*args)
    return res if emit_state else (res[0], None)


def _head_masks():
    lane = lax.broadcasted_iota(jnp.int32, (1, LANES), 1)
    return lane < DH_B, lane >= DH_B


def _ctx_attn_kernel(q_ref, k_ref, v_ref, g_ref, o_ref):
    q, k, v = q_ref[0], k_ref[0], v_ref[0]
    scale = DH_B ** -0.5
    masks = _head_masks()
    outs = []
    for h in range(2):
        qm = jnp.where(masks[h], q, jnp.zeros_like(q))
        s = lax.dot_general(qm, k, NT_DIMS, preferred_element_type=F32) * scale
        m = jnp.max(s, axis=-1, keepdims=True)
        e = jnp.exp(s - m)
        p = e / jnp.sum(e, axis=-1, keepdims=True)
        outs.append(jnp.dot(p.astype(BF16), v, preferred_element_type=F32))
    o = jnp.where(masks[0], outs[0], outs[1])
    o_ref[0] = (o * _silu(g_ref[0].astype(F32))).astype(o_ref.dtype)


def _ctx_attn(yb):
    B, T, _ = yb.shape
    n_pair = H_B // 2

    def col(k):
        return pl.BlockSpec((1, T, LANES), lambda b, p, k=k: (b, 0, k * n_pair + p))

    return pl.pallas_call(
        _ctx_attn_kernel,
        out_shape=jax.ShapeDtypeStruct((B, T, H_B * DH_B), BF16),
        grid=(B, n_pair),
        in_specs=[col(0), col(1), col(2), col(3)],
        out_specs=pl.BlockSpec((1, T, LANES), lambda b, p: (b, 0, p)),
        compiler_params=_cparams(2),
        name="ctx_attn",
    )(yb, yb, yb, yb)


N_DR = 2 * NA_KH - 1
N_DC = 2 * NA_KW - 1
N_TAB = N_DR - 1


def _nat_kernel(rb_ref, q_ref, k_ref, v_ref, g_ref, kc_ref, vc_ref, o_ref,
                tab_ref, qs_ref, s_ref, p_ref, r_ref, oc_ref, *, rows):
    p = pl.program_id(0)
    scale = DH_B ** -0.5
    kh = min(NA_KH, rows)
    masks = _head_masks()

    @pl.when(pl.program_id(1) == 0)
    def _build_tables():
        c = lax.broadcasted_iota(jnp.int32, (GRID_W, LANES), 0)
        lane = lax.broadcasted_iota(jnp.int32, (GRID_W, LANES), 1)
        kcol = lane & (GRID_W - 1)
        upper = lane >= GRID_W
        ws = jnp.clip(c - NA_KW // 2, 0, GRID_W - NA_KW)
        neg = jnp.full((GRID_W, LANES), NEG_INF, F32)
        diag = kcol - c + (NA_KW - 1)
        for h in range(2):
            base = (2 * p + h) * (N_DR * N_DC)

            def per_dr(i, _, base=base, h=h):
                def per_dc(dd, acc):
                    lo = rb_ref[base + i * N_DC + dd]
                    hi = rb_ref[base + (i + 1) * N_DC + dd]
                    return jnp.where(diag == dd, jnp.where(upper, hi, lo), acc)

                acc = lax.fori_loop(0, N_DC, per_dc, neg)
                acc = jnp.where(kcol >= ws, jnp.where(kcol < ws + NA_KW, acc, neg), neg)
                tab_ref[h, i] = acc
                return 0

            lax.fori_loop(0, N_TAB, per_dr, 0)

    kc, vc = kc_ref[0], vc_ref[0]
    n_keys = kh * GRID_W
    n_ctx = kc.shape[0]
    G = NA_GROUP
    W2 = 2 * GRID_W

    def group(gi, _):
        r_first = gi * G
        q0 = pl.multiple_of(r_first * GRID_W, G * GRID_W)
        for i in range(G):
            qi = q_ref[0, pl.ds(q0 + i * GRID_W, GRID_W), :] * scale
            for h in range(2):
                qs_ref[i * W2 + h * GRID_W:i * W2 + (h + 1) * GRID_W, :] = jnp.where(
                    masks[h], qi, jnp.zeros_like(qi))
        s_ref[:, n_keys:n_keys + n_ctx] = lax.dot_general(qs_ref[...], kc, NT_DIMS,
                                                          preferred_element_type=F32)
        windows = []
        for i in range(G):
            r = r_first + i
            rs = jnp.clip(r - kh // 2, 0, rows - kh)
            k0 = pl.multiple_of(rs * GRID_W, GRID_W)
            windows.append(k0)
            dr0 = rs - r + (NA_KH - 1)
            bias = jnp.concatenate(
                [jnp.concatenate([tab_ref[h, dr0 + 2 * m] for m in range(kh // 2)], axis=1)
                 for h in range(2)], axis=0)
            s_ref[i * W2:(i + 1) * W2, 0:n_keys] = lax.dot_general(
                qs_ref[i * W2:(i + 1) * W2, :], k_ref[0, pl.ds(k0, n_keys), :], NT_DIMS,
                preferred_element_type=F32) + bias
        for i in range(G):
            s = s_ref[i * W2:(i + 1) * W2, :]
            e = jnp.exp(s - jnp.max(s, axis=-1, keepdims=True))
            p_ref[i * W2:(i + 1) * W2, :] = e.astype(BF16)
            rinv = 1.0 / jnp.sum(e, axis=-1, keepdims=True)
            r_ref[i * W2:(i + 1) * W2, :] = jnp.broadcast_to(rinv, (W2, LANES))
        oc_ref[...] = jnp.dot(p_ref[:, n_keys:n_keys + n_ctx], vc, preferred_element_type=F32)
        for i in range(G):
            o = jnp.dot(p_ref[i * W2:(i + 1) * W2, 0:n_keys], v_ref[0, pl.ds(windows[i], n_keys), :],
                        preferred_element_type=F32)
            o = (o + oc_ref[i * W2:(i + 1) * W2, :]) * r_ref[i * W2:(i + 1) * W2, :]
            o = jnp.where(masks[0], o[0:GRID_W], o[GRID_W:W2])
            out_rows = pl.ds(q0 + i * GRID_W, GRID_W)
            gate = g_ref[0, out_rows, :].astype(F32)
            o_ref[0, out_rows, :] = (o * _silu(gate)).astype(o_ref.dtype)
        return 0

    lax.fori_loop(0, rows // G, group, 0)


def _nat(yb, kc, vc, rel_bias):
    B, T, _ = yb.shape
    Tc = kc.shape[1]
    n_pair = H_B // 2
    rows = T // GRID_W
    n_stack = NA_GROUP * 2 * GRID_W
    n_keys = min(NA_KH, rows) * GRID_W

    def col(k):
        return pl.BlockSpec((1, T, LANES), lambda p, b, k=k: (b, 0, k * n_pair + p))

    ctx =pl.BlockSpec((1, Tc, LANES), lambda p, b: (b, 0, p))
    return pl.pallas_call(
        functools.partial(_nat_kernel, rows=rows),
        out_shape=jax.ShapeDtypeStruct((B, T, H_B * DH_B), BF16),
        grid=(n_pair, B),
        in_specs=[pl.BlockSpec(memory_space=pltpu.SMEM), col(0), col(1), col(2), col(3), ctx, ctx],
        out_specs=pl.BlockSpec((1, T, LANES), lambda p, b: (b, 0, p)),
        scratch_shapes=[pltpu.VMEM((2, N_TAB, GRID_W, LANES), F32),
                        pltpu.VMEM((n_stack, LANES), BF16),
                        pltpu.VMEM((n_stack, n_keys + Tc), F32),
                        pltpu.VMEM((n_stack, n_keys + Tc), BF16),
                        pltpu.VMEM((n_stack, LANES), F32),
                        pltpu.VMEM((n_stack, LANES), F32)],
        compiler_params=_cparams(2),
        name="nbr_attn",
    )(rel_bias.reshape(-1), yb, yb, yb, yb, kc, vc)


def _seg_len(seq):
    length = -(-seq // N_SEG)
    while length % 8 != 4:
        length += 1
    return length


def _rglru_kernel(x_ref, g_ref, cw_ref, cb_ref, wg_ref, bg_ref, lam_ref, *rest, seq, has_s0, emit_state):
    rest = list(rest)
    s0_ref = rest.pop(0) if has_s0 else None
    o_ref = rest.pop(0)
    hfin_ref = rest.pop(0) if emit_state else None
    xpad_ref = rest.pop(0)
    a_refs, u_refs, h_refs, p_refs = rest[0:2], rest[2:4], rest[4:6], rest[6:8]
    L = _seg_len(seq)
    n_pad = N_SEG * L - seq
    RB = RG_ROWS

    xpad_ref[0:8, :] = jnp.zeros((8, LANES), F32)
    xpad_ref[seq + 8:seq + 16, :] = jnp.zeros((8, LANES), F32)
    xpad_ref[8:seq + 8, :] = x_ref[0]
    for d in range(2):
        a_refs[d][seq:seq + n_pad, :] = jnp.ones((n_pad, LANES), F32)
        u_refs[d][seq:seq + n_pad, :] = jnp.zeros((n_pad, LANES), F32)

    nl = -lam_ref[...]
    sp = jnp.maximum(nl, 0.0) + jnp.log1p(jnp.exp(-jnp.abs(nl)))
    cw = cw_ref[...]
    cbias = cb_ref[...]
    bg = bg_ref[0]

    def gates(blk, _):
        r0 = pl.multiple_of(blk * RB, RB)
        xm = xpad_ref[pl.ds(r0, RB + 16), :]
        xc = cw[0:1] * xm[6:6 + RB] + cw[1:2] * xm[7:7 + RB]
        xc = xc + cw[2:3] * xm[8:8 + RB]
        xc = xc + cw[3:4] * xm[9:9 + RB] + cbias
        gt = jnp.dot(xc.astype(BF16), wg_ref[0], preferred_element_type=F32) + bg
        for d in range(2):
            rg = jax.nn.sigmoid(gt[:, (2 * d) * LANES:(2 * d + 1) * LANES])
            ig = jax.nn.sigmoid(gt[:, (2 * d + 1) * LANES:(2 * d + 2) * LANES])
            la = (-RG_C * rg) * sp[d:d + 1, :]
            a = jnp.exp(la)
            u = jnp.sqrt(-jnp.tanh(la) * (1.0 + a * a)) * (ig * xc)
            a_refs[d][pl.ds(r0, RB), :] = a
            u_refs[d][pl.ds(r0, RB), :] = u
        return 0

    lax.fori_loop(0, seq // RB, gates, 0)

    unroll = 4
    finals = []
    for d in range(2):
        a_ref, u_ref, h_ref, p_ref = a_refs[d], u_refs[d], h_refs[d], p_refs[d]

        def scan(i, carry, d=d, a_ref=a_ref, u_ref=u_ref, h_ref=h_ref, p_ref=p_ref):
            h, pr = carry
            for k in range(unroll):
                t = i * unroll + k
                if d == 1:
                    t = L - 1 - t
                idx = pl.ds(t, N_SEG, stride=L)
                a = a_ref[idx, :]
                h = a * h + u_ref[idx, :]
                pr = pr * a
                h_ref[idx, :] = h
                p_ref[idx, :] = pr
            return h, pr

        h_end, p_end = lax.fori_loop(0, L // unroll, scan,
                                     (jnp.zeros((N_SEG, LANES), F32), jnp.ones((N_SEG, LANES), F32)))
        if has_s0:
            c = s0_ref[0, d:d + 1, :]
        else:
            c = jnp.zeros((1, LANES), F32)
        cin = [None] * N_SEG
        for jj in range(N_SEG):
            j = jj if d == 0 else N_SEG - 1 - jj
            cin[j] = c
            c = h_end[j:j + 1, :] + p_end[j:j + 1, :] * c
        finals.append(c)
        cin = jnp.concatenate(cin, axis=0)

        def fix(i, _, h_ref=h_ref, p_ref=p_ref, cin=cin):
            for k in range(unroll):
                idx = pl.ds(i * unroll + k, N_SEG, stride=L)
                h_ref[idx, :] = h_ref[idx, :] + p_ref[idx, :] * cin
            return 0

        lax.fori_loop(0, L // unroll, fix, 0)

    if emit_state:
        hfin_ref[0] = jnp.concatenate(finals, axis=0)

    def combine(blk, _):
        rows = pl.ds(pl.multiple_of(blk * RB, RB), RB)
        y = (h_refs[0][rows, :] + h_refs[1][rows, :]) * _silu(g_ref[0, rows, :])
        o_ref[0, rows, :] = y.astype(o_ref.dtype)
        return 0

    lax.fori_loop(0, seq // RB, combine, 0)


def _rglru(xg, conv_w, conv_b, wg, bg, lam, s0, emit_state):
    B, T, _ = xg.shape
    has_s0 = s0 is not None
    n_rows = N_SEG * _seg_len(T)
    in_specs = [
        pl.BlockSpec((1, T, LANES), lambda b, c: (b, 0, c)),
        pl.BlockSpec((1, T, LANES), lambda b, c: (b, 0, H_C + c)),
        pl.BlockSpec((4, LANES), lambda b, c: (0, c)),
        pl.BlockSpec((1, LANES), lambda b, c: (0, c)),
        pl.BlockSpec((1, BW_C, 4 * BW_C), lambda b, c: (c, 0, 0)),
        pl.BlockSpec((1, 1, 4 * BW_C), lambda b, c: (c, 0, 0)),
        pl.BlockSpec((2, LANES), lambda b, c: (0, c)),
    ]
    args = [xg, xg, conv_w, conv_b, wg, bg, lam]
    if has_s0:
        in_specs.append(pl.BlockSpec((1, 2, LANES), lambda b, c: (b, 0, c)))
        args.append(s0)
    out_shape = [jax.ShapeDtypeStruct((B, T, W_C), BF16)]
    out_specs = [pl.BlockSpec((1, T, LANES), lambda b, c: (b, 0, c))]
    if emit_state:
        out_shape.append(jax.ShapeDtypeStruct((B, 2, W_C), F32))
        out_specs.append(pl.BlockSpec((1, 2, LANES), lambda b, c: (b, 0, c)))
    res = pl.pallas_call(
        functools.partial(_rglru_kernel, seq=T, has_s0=has_s0, emit_state=emit_state),
        out_shape=out_shape,
        grid=(B, H_C),
        in_specs=in_specs,
        out_specs=out_specs,
        scratch_shapes=[pltpu.VMEM((T + 16, LANES), F32)] + [pltpu.VMEM((n_rows, LANES), F32)] * 8,
        compiler_params=_cparams(2),
        name="rglru",
    )(*args)
    return res if emit_state else (res[0], None)


A_COLS = 5 * H_A * DK_A
B_COLS = 4 * H_B * DH_B


def kernel(x_prompt, x_sample, state_hgrn, cache_na_k, cache_na_v, state_rglru, c, c_ctx, norm_gain, w_mod, b_mod, w_in_even, w_out_even, hgrn_lb_logits, hgrn_out_gain, na_rel_bias, w_in_odd, w_out_odd, conv_w, conv_b, rg_gate_w, rg_gate_b, rg_lambda, final_gain):
    n_ctx = x_prompt.shape[0]
    n_lat = x_sample.shape[0]
    depth = w_mod.shape[0]

    cond = jnp.zeros((16, D_MODEL), F32).at[0].set(c_ctx).at[1:1 + n_lat].set(c)
    mod = _modulation(cond, w_mod.astype(BF16), b_mod.reshape(depth, 1, 3 * D_MODEL))
    mod = mod.reshape(depth, 16, 3, D_MODEL)

    xc, xs = x_prompt, x_sample
    new_hgrn, new_k, new_v, new_rg = [], [], [], []
    for l in range(depth):
        j = l // 2
        mod_c, mod_s = mod[l, 0:1], mod[l, 1:1 + n_lat]
        gain = norm_gain[l].reshape(1, D_MODEL)
        last = l == depth - 1
        fgain = final_gain.reshape(1, D_MODEL) if last else None
        if l % 2 == 0:
            w_in = w_in_even[j].astype(BF16)
            w_out = w_out_even[j].astype(BF16)
            outs_s = ((0, A_COLS, F32), (A_COLS, B_COLS, BF16))
            outs_c = outs_s + ((A_COLS + H_B * DH_B, 2 * H_B * DH_B, F32),)
            ya_c, yb_c, kv_c = _inproj(xc, mod_c, gain, w_in, outs_c, 256, True)
            ya_s, yb_s = _inproj(xs, mod_s, gain, w_in, outs_s, 512, False)
            hgain = hgrn_out_gain[j].reshape(H_A, 1, DK_A)
            oa_c, s_fin = _hgrn(ya_c, hgrn_lb_logits, j, hgain, None, True)
            oa_s, _ = _hgrn(ya_s, hgrn_lb_logits, j, hgain, state_hgrn[:, j], False)
            ob_c = _ctx_attn(yb_c)
            tc = cache_na_k.shape[3]
            kc = cache_na_k[:, j].transpose(0, 2, 1, 3).reshape(n_lat, tc, H_B * DH_B).astype(BF16)
            vc = cache_na_v[:, j].transpose(0, 2, 1, 3).reshape(n_lat, tc, H_B * DH_B).astype(BF16)
            ob_s = _nat(yb_s, kc, vc, na_rel_bias[j])
            y_c = jnp.concatenate([oa_c, ob_c], axis=-1)
            y_s = jnp.concatenate([oa_s, ob_s], axis=-1)
            xc = _outproj(y_c, xc, mod_c, w_out, 256, True, fgain)
            xs = _outproj(y_s, xs, mod_s, w_out, 512, False, fgain)
            t_c = kv_c.shape[1]
            heads = kv_c.reshape(n_ctx, t_c, 2, H_B, DH_B).transpose(2, 0, 3, 1, 4)
            new_hgrn.append(s_fin)
            new_k.append(heads[0])
            new_v.append(heads[1])
        else:
            w_in = w_in_odd[j].astype(BF16)
            w_out = w_out_odd[j].astype(BF16)
            outs = ((0, 2 * W_C, F32),)
            (xg_c,) = _inproj(xc, mod_c, gain, w_in, outs, 256, True)
            (xg_s,) = _inproj(xs, mod_s, gain, w_in, outs, 512, False)
            wg = rg_gate_w[j].transpose(2, 3, 0, 1, 4).reshape(H_C, BW_C, 4 * BW_C).astype(BF16)
            bg = rg_gate_b[j].reshape(2, 2, H_C, BW_C).transpose(2, 0, 1, 3).reshape(H_C, 1, 4 * BW_C)
            cb = conv_b[j].reshape(1, W_C)
            y_c, h_fin = _rglru(xg_c, conv_w[j], cb, wg, bg, rg_lambda[j], None, True)
            y_s, _ = _rglru(xg_s, conv_w[j], cb, wg, bg, rg_lambda[j], state_rglru[:, j], False)
            xc = _outproj(y_c, xc, mod_c, w_out, 256, True, fgain)
            xs = _outproj(y_s, xs, mod_s, w_out, 512, False, fgain)
            new_rg.append(h_fin)
    return (xc, xs, jnp.stack(new_hgrn, axis=1), jnp.stack(new_k, axis=1),
            jnp.stack(new_v, axis=1), jnp.stack(new_rg, axis=1))
```

```python
import functools

import jax
import jax.numpy as jnp
from jax import lax
from jax.experimental import pallas as pl
from jax.experimental.pallas import tpu as pltpu

F32 = jnp.float32
BF16 = jnp.bfloat16

D_MODEL = 1024
EPS = 1e-6
NEG_INF = -1e30
H_A = 4
DK_A = 128
HGRN_CHUNK = 32
HGRN_ROWS = 256
H_B = 8
DH_B = 64
GRID_W = 64
NA_KH = 8
NA_KW = 16
NA_GROUP = 8
W_C = 1024
H_C = 8
BW_C = W_C // H_C
RG_C = 8.0
RG_ROWS = 256
N_SEG = 8
LANES = 128
VMEM_LIMIT = 48 * 1024 * 1024

NT_DIMS = (((1,), (1,)), ((), ()))
TN_DIMS = (((0,), (0,)), ((), ()))


def _silu(x):
    return x * jax.nn.sigmoid(x)


def _cparams(n_axes):
    return pltpu.CompilerParams(dimension_semantics=("arbitrary",) * n_axes,
                                vmem_limit_bytes=VMEM_LIMIT)


def _mod_kernel(cond_ref, w_ref, b_ref, o_ref):
    s = _silu(cond_ref[...])
    o_ref[0] = jnp.dot(s.astype(BF16), w_ref[0], preferred_element_type=F32) + b_ref[0]


def _modulation(cond, w_mod, b_mod):
    depth = w_mod.shape[0]
    n_rows = cond.shape[0]
    return pl.pallas_call(
        _mod_kernel,
        out_shape=jax.ShapeDtypeStruct((depth, n_rows, 3 * D_MODEL), F32),
        grid=(depth, 3),
        in_specs=[
            pl.BlockSpec((n_rows, D_MODEL), lambda l, n: (0, 0)),
            pl.BlockSpec((1, D_MODEL, D_MODEL), lambda l, n: (l, 0, n)),
            pl.BlockSpec((1, 1, D_MODEL), lambda l, n: (l, 0, n)),
        ],
        out_specs=pl.BlockSpec((1, n_rows, D_MODEL), lambda l, n: (l, 0, n)),
        compiler_params=_cparams(2),
        name="adaln_mod",
    )(cond, w_mod, b_mod)


def _inproj_kernel(x_ref, mod_ref, gain_ref, w_ref, *out_refs, outs):
    x = x_ref[0]
    var = jnp.mean(x * x, axis=-1, keepdims=True)
    y = x * lax.rsqrt(var + EPS) * gain_ref[...]
    h = y * (1.0 + mod_ref[0, 1:2, :]) + mod_ref[0, 0:1, :]
    hb = h.astype(BF16)
    step = 512
    for o_ref, (col0, width, _) in zip(out_refs, outs):
        for c in range(0, width, step):
            r = jnp.dot(hb, w_ref[:, col0 + c:col0 + c + step], preferred_element_type=F32)
            o_ref[0, :, c:c + step] = r.astype(o_ref.dtype)


def _inproj(x, mod, gain, w, outs, tm, shared_mod):
    B, T, _ = x.shape
    n_cols = w.shape[1]
    mod_map = (lambda b, t: (0, 0, 0)) if shared_mod else (lambda b, t: (b, 0, 0))
    return pl.pallas_call(
        functools.partial(_inproj_kernel, outs=outs),
        out_shape=[jax.ShapeDtypeStruct((B, T, wd), dt) for _, wd, dt in outs],
        grid=(B, T // tm),
        in_specs=[
            pl.BlockSpec((1, tm, D_MODEL), lambda b, t: (b, t, 0)),
            pl.BlockSpec((1, 3, D_MODEL), mod_map),
            pl.BlockSpec((1, D_MODEL), lambda b, t: (0, 0)),
            pl.BlockSpec((D_MODEL, n_cols), lambda b, t: (0, 0)),
        ],
        out_specs=[pl.BlockSpec((1, tm, wd), lambda b, t: (b, t, 0)) for _, wd, _ in outs],
        compiler_params=_cparams(2),
        name="in_proj",
    )(x, mod, gain, w)


def _outproj_kernel(y_ref, x_ref, mod_ref, w_ref, *rest, final):
    m = jnp.dot(y_ref[0], w_ref[...], preferred_element_type=F32)
    xn = x_ref[0] + mod_ref[0, 2:3, :] * m
    if final:
        gain_ref, o_ref = rest
        var = jnp.mean(xn * xn, axis=-1, keepdims=True)
        xn = xn * lax.rsqrt(var + EPS) * gain_ref[...]
    else:
        (o_ref,) = rest
    o_ref[0] = xn


def _outproj(y, x, mod, w, tm, shared_mod, final_gain=None):
    B, T, _ = x.shape
    final = final_gain is not None
    mod_map = (lambda b, t: (0, 0, 0)) if shared_mod else (lambda b, t: (b, 0, 0))
    in_specs = [
        pl.BlockSpec((1, tm, D_MODEL), lambda b, t: (b, t, 0)),
        pl.BlockSpec((1, tm, D_MODEL), lambda b, t: (b, t, 0)),
        pl.BlockSpec((1, 3, D_MODEL), mod_map),
        pl.BlockSpec((D_MODEL, D_MODEL), lambda b, t: (0, 0)),
    ]
    args = [y, x, mod, w]
    if final:
        in_specs.append(pl.BlockSpec((1, D_MODEL), lambda b, t: (0, 0)))
        args.append(final_gain)
    return pl.pallas_call(
        functools.partial(_outproj_kernel, final=final),
        out_shape=jax.ShapeDtypeStruct((B, T, D_MODEL), F32),
        grid=(B, T // tm),
        in_specs=in_specs,
        out_specs=pl.BlockSpec((1, tm, D_MODEL), lambda b, t: (b, t, 0)),
        compiler_params=_cparams(2),
        name="out_proj",
    )(*args)


def _hgrn_kernel(q_ref, zf_ref, zb_ref, v_ref, g_ref, lgt_ref, gain_ref, *rest, seq, layer, has_s0, emit_state):
    rest = list(rest)
    s0_ref = rest.pop(0) if has_s0 else None
    o_ref = rest.pop(0)
    sfin_ref = rest.pop(0) if emit_state else None
    acc_ref, qd_ref, ki_ref, kd_ref, kv_ref, st_ref, dec_ref, mst_ref, msk_ref, mexp_ref = rest
    R = HGRN_ROWS
    C = HGRN_CHUNK
    n_blk = seq // R
    n_chunk = R // C
    n_all = seq // C

    @pl.when((pl.program_id(0) == 0) & (pl.program_id(1) == 0))
    def _build_masks():
        ti = lax.broadcasted_iota(jnp.int32, (R, R), 0)
        tj = lax.broadcasted_iota(jnp.int32, (R, R), 1)
        shift = C.bit_length() - 1
        same = lax.shift_right_logical(ti, shift) == lax.shift_right_logical(tj, shift)
        one = jnp.ones((R, R), F32)
        zero = jnp.zeros((R, R), F32)
        incl = (jnp.where(same, jnp.where(tj <= ti, one, zero), zero),
                jnp.where(same, jnp.where(tj >= ti, one, zero), zero))
        for d in range(2):
            msk_ref[d] = incl[d]
            mst_ref[d] = incl[d].astype(BF16)
        rr = lax.broadcasted_iota(jnp.int32, (R, n_chunk * LANES), 0)
        cc = lax.broadcasted_iota(jnp.int32, (R, n_chunk * LANES), 1)
        own = lax.shift_right_logical(rr, shift) == lax.shift_right_logical(cc, LANES.bit_length() - 1)
        mexp_ref[...] = jnp.where(own, 1.0, 0.0).astype(BF16)

    lgt = [lgt_ref[:, i, :] for i in range(lgt_ref.shape[1])]
    lmax = functools.reduce(jnp.maximum, lgt)
    ex = [jnp.exp(t - lmax) for t in lgt]
    lb_all = sum(ex[:layer + 1]) / sum(ex)
    gain = gain_ref[0]

    blocks_per_trip = 2 if n_blk % 2 == 0 else 1

    def gates(i, _):
        for u in range(blocks_per_trip):
            blk = i * blocks_per_trip + u
            rows = pl.ds(pl.multiple_of(blk * R, R), R)
            q = q_ref[0, rows, :]
            for d in range(2):
                z = (zf_ref if d == 0 else zb_ref)[0, rows, :]
                lb = lb_all[d:d + 1, :]
                oml = 1.0 - lb
                e = jnp.exp(-jnp.abs(z))
                r = 1.0 / (1.0 + e)
                er = e * r
                pos = z >= 0.0
                f = lb + oml * jnp.where(pos, r, er)
                k = oml * jnp.where(pos, er, r)
                logf = jnp.log(f)
                hi = logf.astype(BF16)
                lo = (logf - hi.astype(F32)).astype(BF16)
                cs = jnp.dot(mst_ref[d], jnp.concatenate([hi, lo], axis=1), preferred_element_type=F32)
                b = cs[:, 0:LANES] + cs[:, LANES:2 * LANES]
                ends = [c * C + (C - 1 if d == 0 else 0) for c in range(n_chunk)]
                btot = jnp.concatenate([jnp.broadcast_to(b[t:t + 1, :], (C, LANES)) for t in ends], axis=0)
                qd_ref[d, rows, :] = (q * jnp.exp(b)).astype(BF16)
                ki_ref[d, rows, :] = (k * jnp.exp(-b)).astype(BF16)
                kd_ref[d, rows, :] = (k * jnp.exp(btot - b)).astype(BF16)
                for c in range(n_chunk):
                    dec_ref[d, blk * n_chunk + c] = jnp.exp(btot[c * C:c * C + 8, :])
        return 0

    lax.fori_loop(0, n_blk // blocks_per_trip, gates, 0)

    def intra(i, _):
        for u in range(blocks_per_trip):
            blk = i * blocks_per_trip + u
            rows = pl.ds(pl.multiple_of(blk * R, R), R)
            v = v_ref[0, rows, :]
            vb = v.astype(BF16)
            vt = v.T.astype(BF16)
            for d in range(2):
                att = lax.dot_general(qd_ref[d, rows, :], ki_ref[d, rows, :], NT_DIMS,
                                      preferred_element_type=F32)
                att = jnp.where(msk_ref[d] > 0.5, att, 0.0)
                acc_ref[d, rows, :] = jnp.dot(att.astype(BF16), vb, preferred_element_type=F32)
                kd_exp = jnp.concatenate([kd_ref[d, rows, :]] * n_chunk, axis=1) * mexp_ref[...]
                kv_all = jnp.dot(vt, kd_exp, preferred_element_type=F32)
                for c in range(n_chunk):
                    kv_ref[d, blk * n_chunk + c] = kv_all[:, c * LANES:(c + 1) * LANES]
        return 0

    lax.fori_loop(0, n_blk // blocks_per_trip, intra, 0)

    unroll = 4

    def states(i, sts):
        sts = list(sts)
        for u in range(unroll):
            n = i * unroll + u
            for d in range(2):
                c = n if d == 0 else n_all - 1 - n
                st_ref[d, c] = sts[d].astype(BF16)
                dec = jnp.concatenate([dec_ref[d, c]] * (DK_A // 8), axis=0)
                sts[d] = sts[d] * dec + kv_ref[d, c]
        return tuple(sts)

    if has_s0:
        st0 = (s0_ref[0, 0, 0].T, s0_ref[0, 1, 0].T)
    else:
        st0 = (jnp.zeros((DK_A, DK_A), F32),) * 2
    sts = lax.fori_loop(0, n_all // unroll, states, st0)
    if emit_state:
        for d in range(2):
            sfin_ref[0, d, 0] = sts[d].T

    def finish(blk, _):
        rows = pl.ds(pl.multiple_of(blk * R, R), R)
        tot = acc_ref[0, rows, :] + acc_ref[1, rows, :]
        for d in range(2):
            pieces = []
            for c in range(n_chunk):
                crow = pl.ds(pl.multiple_of(blk * R + c * C, C), C)
                pieces.append(lax.dot_general(qd_ref[d, crow, :], st_ref[d, blk * n_chunk + c], NT_DIMS,
                                              preferred_element_type=F32))
            tot = tot + jnp.concatenate(pieces, axis=0)
        var = jnp.mean(tot * tot, axis=-1, keepdims=True)
        y = tot * lax.rsqrt(var + EPS) * gain
        o_ref[0, rows, :] = (y * _silu(g_ref[0, rows, :])).astype(o_ref.dtype)
        return 0

    lax.fori_loop(0, n_blk, finish, 0)


def _hgrn(ya, lgt, layer, gain, s0, emit_state):
    B, T, _ = ya.shape
    has_s0 = s0 is not None

    def col(k):
        return pl.BlockSpec((1, T, LANES), lambda b, h, k=k: (b, 0, k * H_A + h))

    in_specs = [col(0), col(1), col(2), col(3), col(4),
                pl.BlockSpec((2, lgt.shape[1], LANES), lambda b, h: (0, 0, h)),
                pl.BlockSpec((1, 1, LANES), lambda b, h: (h, 0, 0))]
    args = [ya, ya, ya, ya, ya, lgt, gain]
    if has_s0:
        in_specs.append(pl.BlockSpec((1, 2, 1, DK_A, DK_A), lambda b, h: (b, 0, h, 0, 0)))
        args.append(s0)
    out_shape = [jax.ShapeDtypeStruct((B, T, H_A * DK_A), BF16)]
    out_specs = [pl.BlockSpec((1, T, LANES), lambda b, h: (b, 0, h))]
    if emit_state:
        out_shape.append(jax.ShapeDtypeStruct((B, 2, H_A, DK_A, DK_A), F32))
        out_specs.append(pl.BlockSpec((1, 2, 1, DK_A, DK_A), lambda b, h: (b, 0, h, 0, 0)))
    res = pl.pallas_call(
        functools.partial(_hgrn_kernel, seq=T, layer=layer, has_s0=has_s0, emit_state=emit_state),
        out_shape=out_shape,
        grid=(B, H_A),
        in_specs=in_specs,
        out_specs=out_specs,
        scratch_shapes=[pltpu.VMEM((2, T, LANES), F32),
                        pltpu.VMEM((2, T, LANES), BF16),
                        pltpu.VMEM((2, T, LANES), BF16),
                        pltpu.VMEM((2, T, LANES), BF16),
                        pltpu.VMEM((2, T // HGRN_CHUNK, DK_A, DK_A), F32),
                        pltpu.VMEM((2, T // HGRN_CHUNK, DK_A, DK_A), BF16),
                        pltpu.VMEM((2, T // HGRN_CHUNK, 8, LANES), F32),
                        pltpu.VMEM((2, HGRN_ROWS, HGRN_ROWS), BF16),
                        pltpu.VMEM((2, HGRN_ROWS, HGRN_ROWS), F32),
                        pltpu.VMEM((HGRN_ROWS, HGRN_ROWS // HGRN_CHUNK * LANES), BF16)],
        compiler_params=_cparams(2),
        name="hgrn2",
    )(*args)
    return res if emit_state else (res[0], None)


def _head_masks():
    lane = lax.broadcasted_iota(jnp.int32, (1, LANES), 1)
    return lane < DH_B, lane >= DH_B


def _ctx_attn_kernel(q_ref, k_ref, v_ref, g_ref, o_ref):
    q, k, v = q_ref[0], k_ref[0], v_ref[0]
    scale = DH_B ** -0.5
    masks = _head_masks()
    outs = []
    for h in range(2):
        qm = jnp.where(masks[h], q, jnp.zeros_like(q))
        s = lax.dot_general(qm, k, NT_DIMS, preferred_element_type=F32) * scale
        m = jnp.max(s, axis=-1, keepdims=True)
        e = jnp.exp(s - m)
        p = e / jnp.sum(e, axis=-1, keepdims=True)
        outs.append(jnp.dot(p.astype(BF16), v, preferred_element_type=F32))
    o = jnp.where(masks[0], outs[0], outs[1])
    o_ref[0] = (o * _silu(g_ref[0].astype(F32))).astype(o_ref.dtype)


def _ctx_attn(yb):
    B, T, _ = yb.shape
    n_pair = H_B // 2

    def col(k):
        return pl.BlockSpec((1, T, LANES), lambda b, p, k=k: (b, 0, k * n_pair + p))

    return pl.pallas_call(
        _ctx_attn_kernel,
        out_shape=jax.ShapeDtypeStruct((B, T, H_B * DH_B), BF16),
        grid=(B, n_pair),
        in_specs=[col(0), col(1), col(2), col(3)],
        out_specs=pl.BlockSpec((1, T, LANES), lambda b, p: (b, 0, p)),
        compiler_params=_cparams(2),
        name="ctx_attn",
    )(yb, yb, yb, yb)


N_DR = 2 * NA_KH - 1
N_DC = 2 * NA_KW - 1
N_TAB = N_DR - 1


def _nat_kernel(rb_ref, q_ref, k_ref, v_ref, g_ref, kc_ref, vc_ref, o_ref,
                tab_ref, qs_ref, s_ref, p_ref, r_ref, oc_ref, *, rows):
    p = pl.program_id(0)
    scale = DH_B ** -0.5
    kh = min(NA_KH, rows)
    masks = _head_masks()

    @pl.when(pl.program_id(1) == 0)
    def _build_tables():
        c = lax.broadcasted_iota(jnp.int32, (GRID_W, LANES), 0)
        lane = lax.broadcasted_iota(jnp.int32, (GRID_W, LANES), 1)
        kcol = lane & (GRID_W - 1)
        upper = lane >= GRID_W
        ws = jnp.clip(c - NA_KW // 2, 0, GRID_W - NA_KW)
        neg = jnp.full((GRID_W, LANES), NEG_INF, F32)
        diag = kcol - c + (NA_KW - 1)
        for h in range(2):
            base = (2 * p + h) * (N_DR * N_DC)

            def per_dr(i, _, base=base, h=h):
                def per_dc(dd, acc):
                    lo = rb_ref[base + i * N_DC + dd]
                    hi = rb_ref[base + (i + 1) * N_DC + dd]
                    return jnp.where(diag == dd, jnp.where(upper, hi, lo), acc)

                acc = lax.fori_loop(0, N_DC, per_dc, neg)
                acc = jnp.where(kcol >= ws, jnp.where(kcol < ws + NA_KW, acc, neg), neg)
                tab_ref[h, i] = acc
                return 0

            lax.fori_loop(0, N_TAB, per_dr, 0)

    kc, vc = kc_ref[0], vc_ref[0]
    n_keys = kh * GRID_W
    n_ctx = kc.shape[0]
    G = NA_GROUP
    W2 = 2 * GRID_W

    def group(gi, _):
        r_first = gi * G
        q0 = pl.multiple_of(r_first * GRID_W, G * GRID_W)
        for i in range(G):
            qi = q_ref[0, pl.ds(q0 + i * GRID_W, GRID_W), :] * scale
            for h in range(2):
                qs_ref[i * W2 + h * GRID_W:i * W2 + (h + 1) * GRID_W, :] = jnp.where(
                    masks[h], qi, jnp.zeros_like(qi))
        s_ref[:, n_keys:n_keys + n_ctx] = lax.dot_general(qs_ref[...], kc, NT_DIMS,
                                                          preferred_element_type=F32)
        windows = []
        for i in range(G):
            r = r_first + i
            rs = jnp.clip(r - kh // 2, 0, rows - kh)
            k0 = pl.multiple_of(rs * GRID_W, GRID_W)
            windows.append(k0)
            dr0 = rs - r + (NA_KH - 1)
            bias = jnp.concatenate(
                [jnp.concatenate([tab_ref[h, dr0 + 2 * m] for m in range(kh // 2)], axis=1)
                 for h in range(2)], axis=0)
            s_ref[i * W2:(i + 1) * W2, 0:n_keys] = lax.dot_general(
                qs_ref[i * W2:(i + 1) * W2, :], k_ref[0, pl.ds(k0, n_keys), :], NT_DIMS,
                preferred_element_type=F32) + bias
        for i in range(G):
            s = s_ref[i * W2:(i + 1) * W2, :]
            e = jnp.exp(s - jnp.max(s, axis=-1, keepdims=True))
            p_ref[i * W2:(i + 1) * W2, :] = e.astype(BF16)
            rinv = 1.0 / jnp.sum(e, axis=-1, keepdims=True)
            r_ref[i * W2:(i + 1) * W2, :] = jnp.broadcast_to(rinv, (W2, LANES))
        oc_ref[...] = jnp.dot(p_ref[:, n_keys:n_keys + n_ctx], vc, preferred_element_type=F32)
        for i in range(G):
            o = jnp.dot(p_ref[i * W2:(i + 1) * W2, 0:n_keys], v_ref[0, pl.ds(windows[i], n_keys), :],
                        preferred_element_type=F32)
            o = (o + oc_ref[i * W2:(i + 1) * W2, :]) * r_ref[i * W2:(i + 1) * W2, :]
            o = jnp.where(masks[0], o[0:GRID_W], o[GRID_W:W2])
            out_rows = pl.ds(q0 + i * GRID_W, GRID_W)
            gate = g_ref[0, out_rows, :].astype(F32)
            o_ref[0, out_rows, :] = (o * _silu(gate)).astype(o_ref.dtype)
        return 0

    lax.fori_loop(0, rows // G, group, 0)


def _nat(yb, kc, vc, rel_bias):
    B, T, _ = yb.shape
    Tc = kc.shape[1]
    n_pair = H_B // 2
    rows = T // GRID_W
    n_stack = NA_GROUP * 2 * GRID_W
    n_keys = min(NA_KH, rows) * GRID_W

    def col(k):
        return pl.BlockSpec((1, T, LANES), lambda p, b, k=k: (b, 0, k * n_pair + p))

    ctx =pl.BlockSpec((1, Tc, LANES), lambda p, b: (b, 0, p))
    return pl.pallas_call(
        functools.partial(_nat_kernel, rows=rows),
        out_shape=jax.ShapeDtypeStruct((B, T, H_B * DH_B), BF16),
        grid=(n_pair, B),
        in_specs=[pl.BlockSpec(memory_space=pltpu.SMEM), col(0), col(1), col(2), col(3), ctx, ctx],
        out_specs=pl.BlockSpec((1, T, LANES), lambda p, b: (b, 0, p)),
        scratch_shapes=[pltpu.VMEM((2, N_TAB, GRID_W, LANES), F32),
                        pltpu.VMEM((n_stack, LANES), BF16),
                        pltpu.VMEM((n_stack, n_keys + Tc), F32),
                        pltpu.VMEM((n_stack, n_keys + Tc), BF16),
                        pltpu.VMEM((n_stack, LANES), F32),
                        pltpu.VMEM((n_stack, LANES), F32)],
        compiler_params=_cparams(2),
        name="nbr_attn",
    )(rel_bias.reshape(-1), yb, yb, yb, yb, kc, vc)


def _seg_len(seq):
    length = -(-seq // N_SEG)
    while length % 8 != 4:
        length += 1
    return length


def _rglru_kernel(x_ref, g_ref, cw_ref, cb_ref, wg_ref, bg_ref, lam_ref, *rest, seq, has_s0, emit_state):
    rest = list(rest)
    s0_ref = rest.pop(0) if has_s0 else None
    o_ref = rest.pop(0)
    hfin_ref = rest.pop(0) if emit_state else None
    xpad_ref = rest.pop(0)
    a_refs, u_refs, h_refs, p_refs = rest[0:2], rest[2:4], rest[4:6], rest[6:8]
    L = _seg_len(seq)
    n_pad = N_SEG * L - seq
    RB = RG_ROWS

    xpad_ref[0:8, :] = jnp.zeros((8, LANES), F32)
    xpad_ref[seq + 8:seq + 16, :] = jnp.zeros((8, LANES), F32)
    xpad_ref[8:seq + 8, :] = x_ref[0]
    for d in range(2):
        a_refs[d][seq:seq + n_pad, :] = jnp.ones((n_pad, LANES), F32)
        u_refs[d][seq:seq + n_pad, :] = jnp.zeros((n_pad, LANES), F32)

    nl = -lam_ref[...]
    sp = jnp.maximum(nl, 0.0) + jnp.log1p(jnp.exp(-jnp.abs(nl)))
    cw = cw_ref[...]
    cbias = cb_ref[...]
    bg = bg_ref[0]

    def gates(blk, _):
        r0 = pl.multiple_of(blk * RB, RB)
        xm = xpad_ref[pl.ds(r0, RB + 16), :]
        xc = cw[0:1] * xm[6:6 + RB] + cw[1:2] * xm[7:7 + RB]
        xc = xc + cw[2:3] * xm[8:8 + RB]
        xc = xc + cw[3:4] * xm[9:9 + RB] + cbias
        gt = jnp.dot(xc.astype(BF16), wg_ref[0], preferred_element_type=F32) + bg
        for d in range(2):
            rg = jax.nn.sigmoid(gt[:, (2 * d) * LANES:(2 * d + 1) * LANES])
            ig = jax.nn.sigmoid(gt[:, (2 * d + 1) * LANES:(2 * d + 2) * LANES])
            la = (-RG_C * rg) * sp[d:d + 1, :]
            a = jnp.exp(la)
            u = jnp.sqrt(-jnp.tanh(la) * (1.0 + a * a)) * (ig * xc)
            a_refs[d][pl.ds(r0, RB), :] = a
            u_refs[d][pl.ds(r0, RB), :] = u
        return 0

    lax.fori_loop(0, seq // RB, gates, 0)

    unroll = 4
    finals = []
    for d in range(2):
        a_ref, u_ref, h_ref, p_ref = a_refs[d], u_refs[d], h_refs[d], p_refs[d]

        def scan(i, carry, d=d, a_ref=a_ref, u_ref=u_ref, h_ref=h_ref, p_ref=p_ref):
            h, pr = carry
            for k in range(unroll):
                t = i * unroll + k
                if d == 1:
                    t = L - 1 - t
                idx = pl.ds(t, N_SEG, stride=L)
                a = a_ref[idx, :]
                h = a * h + u_ref[idx, :]
                pr = pr * a
                h_ref[idx, :] = h
                p_ref[idx, :] = pr
            return h, pr

        h_end, p_end = lax.fori_loop(0, L // unroll, scan,
                                     (jnp.zeros((N_SEG, LANES), F32), jnp.ones((N_SEG, LANES), F32)))
        if has_s0:
            c = s0_ref[0, d:d + 1, :]
        else:
            c = jnp.zeros((1, LANES), F32)
        cin = [None] * N_SEG
        for jj in range(N_SEG):
            j = jj if d == 0 else N_SEG - 1 - jj
            cin[j] = c
            c = h_end[j:j + 1, :] + p_end[j:j + 1, :] * c
        finals.append(c)
        cin = jnp.concatenate(cin, axis=0)

        def fix(i, _, h_ref=h_ref, p_ref=p_ref, cin=cin):
            for k in range(unroll):
                idx = pl.ds(i * unroll + k, N_SEG, stride=L)
                h_ref[idx, :] = h_ref[idx, :] + p_ref[idx, :] * cin
            return 0

        lax.fori_loop(0, L // unroll, fix, 0)

    if emit_state:
        hfin_ref[0] = jnp.concatenate(finals, axis=0)

    def combine(blk, _):
        rows = pl.ds(pl.multiple_of(blk * RB, RB), RB)
        y = (h_refs[0][rows, :] + h_refs[1][rows, :]) * _silu(g_ref[0, rows, :])
        o_ref[0, rows, :] = y.astype(o_ref.dtype)
        return 0

    lax.fori_loop(0, seq // RB, combine, 0)


def _rglru(xg, conv_w, conv_b, wg, bg, lam, s0, emit_state):
    B, T, _ = xg.shape
    has_s0 = s0 is not None
    n_rows = N_SEG * _seg_len(T)
    in_specs = [
        pl.BlockSpec((1, T, LANES), lambda b, c: (b, 0, c)),
        pl.BlockSpec((1, T, LANES), lambda b, c: (b, 0, H_C + c)),
        pl.BlockSpec((4, LANES), lambda b, c: (0, c)),
        pl.BlockSpec((1, LANES), lambda b, c: (0, c)),
        pl.BlockSpec((1, BW_C, 4 * BW_C), lambda b, c: (c, 0, 0)),
        pl.BlockSpec((1, 1, 4 * BW_C), lambda b, c: (c, 0, 0)),
        pl.BlockSpec((2, LANES), lambda b, c: (0, c)),
    ]
    args = [xg, xg, conv_w, conv_b, wg, bg, lam]
    if has_s0:
        in_specs.append(pl.BlockSpec((1, 2, LANES), lambda b, c: (b, 0, c)))
        args.append(s0)
    out_shape = [jax.ShapeDtypeStruct((B, T, W_C), BF16)]
    out_specs = [pl.BlockSpec((1, T, LANES), lambda b, c: (b, 0, c))]
    if emit_state:
        out_shape.append(jax.ShapeDtypeStruct((B, 2, W_C), F32))
        out_specs.append(pl.BlockSpec((1, 2, LANES), lambda b, c: (b, 0, c)))
    res = pl.pallas_call(
        functools.partial(_rglru_kernel, seq=T, has_s0=has_s0, emit_state=emit_state),
        out_shape=out_shape,
        grid=(B, H_C),
        in_specs=in_specs,
        out_specs=out_specs,
        scratch_shapes=[pltpu.VMEM((T + 16, LANES), F32)] + [pltpu.VMEM((n_rows, LANES), F32)] * 8,
        compiler_params=_cparams(2),
        name="rglru",
    )(*args)
    return res if emit_state else (res[0], None)


A_COLS = 5 * H_A * DK_A
B_COLS = 4 * H_B * DH_B


def kernel(x_prompt, x_sample, state_hgrn, cache_na_k, cache_na_v, state_rglru, c, c_ctx, norm_gain, w_mod, b_mod, w_in_even, w_out_even, hgrn_lb_logits, hgrn_out_gain, na_rel_bias, w_in_odd, w_out_odd, conv_w, conv_b, rg_gate_w, rg_gate_b, rg_lambda, final_gain):
    n_ctx = x_prompt.shape[0]
    n_lat = x_sample.shape[0]
    depth = w_mod.shape[0]

    cond = jnp.zeros((16, D_MODEL), F32).at[0].set(c_ctx).at[1:1 + n_lat].set(c)
    mod = _modulation(cond, w_mod.astype(BF16), b_mod.reshape(depth, 1, 3 * D_MODEL))
    mod = mod.reshape(depth, 16, 3, D_MODEL)

    xc, xs = x_prompt, x_sample
    new_hgrn, new_k, new_v, new_rg = [], [], [], []
    for l in range(depth):
        j = l // 2
        mod_c, mod_s = mod[l, 0:1], mod[l, 1:1 + n_lat]
        gain = norm_gain[l].reshape(1, D_MODEL)
        last = l == depth - 1
        fgain = final_gain.reshape(1, D_MODEL) if last else None
        if l % 2 == 0:
            w_in = w_in_even[j].astype(BF16)
            w_out = w_out_even[j].astype(BF16)
            outs_s = ((0, A_COLS, F32), (A_COLS, B_COLS, BF16))
            outs_c = outs_s + ((A_COLS + H_B * DH_B, 2 * H_B * DH_B, F32),)
            ya_c, yb_c, kv_c = _inproj(xc, mod_c, gain, w_in, outs_c, 256, True)
            ya_s, yb_s = _inproj(xs, mod_s, gain, w_in, outs_s, 512, False)
            hgain = hgrn_out_gain[j].reshape(H_A, 1, DK_A)
            oa_c, s_fin = _hgrn(ya_c, hgrn_lb_logits, j, hgain, None, True)
            oa_s, _ = _hgrn(ya_s, hgrn_lb_logits, j, hgain, state_hgrn[:, j], False)
            ob_c = _ctx_attn(yb_c)
            tc = cache_na_k.shape[3]
            kc = cache_na_k[:, j].transpose(0, 2, 1, 3).reshape(n_lat, tc, H_B * DH_B).astype(BF16)
            vc = cache_na_v[:, j].transpose(0, 2, 1, 3).reshape(n_lat, tc, H_B * DH_B).astype(BF16)
            ob_s = _nat(yb_s, kc, vc, na_rel_bias[j])
            y_c = jnp.concatenate([oa_c, ob_c], axis=-1)
            y_s = jnp.concatenate([oa_s, ob_s], axis=-1)
            xc = _outproj(y_c, xc, mod_c, w_out, 256, True, fgain)
            xs = _outproj(y_s, xs, mod_s, w_out, 512, False, fgain)
            t_c = kv_c.shape[1]
            heads = kv_c.reshape(n_ctx, t_c, 2, H_B, DH_B).transpose(2, 0, 3, 1, 4)
            new_hgrn.append(s_fin)
            new_k.append(heads[0])
            new_v.append(heads[1])
        else:
            w_in = w_in_odd[j].astype(BF16)
            w_out = w_out_odd[j].astype(BF16)
            outs = ((0, 2 * W_C, F32),)
            (xg_c,) = _inproj(xc, mod_c, gain, w_in, outs, 256, True)
            (xg_s,) = _inproj(xs, mod_s, gain, w_in, outs, 512, False)
            wg = rg_gate_w[j].transpose(2, 3, 0, 1, 4).reshape(H_C, BW_C, 4 * BW_C).astype(BF16)
            bg = rg_gate_b[j].reshape(2, 2, H_C, BW_C).transpose(2, 0, 1, 3).reshape(H_C, 1, 4 * BW_C)
            cb = conv_b[j].reshape(1, W_C)
            y_c, h_fin = _rglru(xg_c, conv_w[j], cb, wg, bg, rg_lambda[j], None, True)
            y_s, _ = _rglru(xg_s, conv_w[j], cb, wg, bg, rg_lambda[j], state_rglru[:, j], False)
            xc = _outproj(y_c, xc, mod_c, w_out, 256, True, fgain)
            xs = _outproj(y_s, xs, mod_s, w_out, 512, False, fgain)
            new_rg.append(h_fin)
    return (xc, xs, jnp.stack(new_hgrn, axis=1), jnp.stack(new_k, axis=1),
            jnp.stack(new_v, axis=1), jnp.stack(new_rg, axis=1))
```

```python
import functools

import jax
import jax.numpy as jnp
from jax import lax
from jax.experimental import pallas as pl
from jax.experimental.pallas import tpu as pltpu

F32 = jnp.float32
BF16 = jnp.bfloat16

D_MODEL = 1024
EPS = 1e-6
NEG_INF = -1e30
H_A = 4
DK_A = 128
HGRN_CHUNK = 32
HGRN_ROWS = 256
H_B = 8
DH_B = 64
GRID_W = 64
NA_KH = 8
NA_KW = 16
NA_GROUP = 8
W_C = 1024
H_C = 8
BW_C = W_C // H_C
RG_C = 8.0
RG_ROWS = 256
RG_SLABS = 2
N_SEG = 16
LANES = 128
VMEM_LIMIT = 48 * 1024 * 1024

NT_DIMS = (((1,), (1,)), ((), ()))
TN_DIMS = (((0,), (0,)), ((), ()))


def _silu(x):
    return x * jax.nn.sigmoid(x)


def _cparams(n_axes):
    return pltpu.CompilerParams(dimension_semantics=("arbitrary",) * n_axes,
                                vmem_limit_bytes=VMEM_LIMIT)


def _mod_kernel(cond_ref, w_ref, b_ref, o_ref):
    s = _silu(cond_ref[...])
    o_ref[0] = jnp.dot(s.astype(BF16), w_ref[0], preferred_element_type=F32) + b_ref[0]


def _modulation(cond, w_mod, b_mod):
    depth = w_mod.shape[0]
    n_rows = cond.shape[0]
    return pl.pallas_call(
        _mod_kernel,
        out_shape=jax.ShapeDtypeStruct((depth, n_rows, 3 * D_MODEL), F32),
        grid=(depth, 3),
        in_specs=[
            pl.BlockSpec((n_rows, D_MODEL), lambda l, n: (0, 0)),
            pl.BlockSpec((1, D_MODEL, D_MODEL), lambda l, n: (l, 0, n)),
            pl.BlockSpec((1, 1, D_MODEL), lambda l, n: (l, 0, n)),
        ],
        out_specs=pl.BlockSpec((1, n_rows, D_MODEL), lambda l, n: (l, 0, n)),
        compiler_params=_cparams(2),
        name="adaln_mod",
    )(cond, w_mod, b_mod)


def _inproj_kernel(x_ref, mod_ref, gain_ref, w_ref, *out_refs, outs):
    x = x_ref[0]
    var = jnp.mean(x * x, axis=-1, keepdims=True)
    y = x * lax.rsqrt(var + EPS) * gain_ref[...]
    h = y * (1.0 + mod_ref[0, 1:2, :]) + mod_ref[0, 0:1, :]
    hb = h.astype(BF16)
    step = 512
    for o_ref, (col0, width, _) in zip(out_refs, outs):
        for c in range(0, width, step):
            r = jnp.dot(hb, w_ref[:, col0 + c:col0 + c + step], preferred_element_type=F32)
            o_ref[0, :, c:c + step] = r.astype(o_ref.dtype)


def _inproj(x, mod, gain, w, outs, tm, shared_mod):
    B, T, _ = x.shape
    n_cols = w.shape[1]
    mod_map = (lambda b, t: (0, 0, 0)) if shared_mod else (lambda b, t: (b, 0, 0))
    return pl.pallas_call(
        functools.partial(_inproj_kernel, outs=outs),
        out_shape=[jax.ShapeDtypeStruct((B, T, wd), dt) for _, wd, dt in outs],
        grid=(B, T // tm),
        in_specs=[
            pl.BlockSpec((1, tm, D_MODEL), lambda b, t: (b, t, 0)),
            pl.BlockSpec((1, 3, D_MODEL), mod_map),
            pl.BlockSpec((1, D_MODEL), lambda b, t: (0, 0)),
            pl.BlockSpec((D_MODEL, n_cols), lambda b, t: (0, 0)),
        ],
        out_specs=[pl.BlockSpec((1, tm, wd), lambda b, t: (b, t, 0)) for _, wd, _ in outs],
        compiler_params=_cparams(2),
        name="in_proj",
    )(x, mod, gain, w)


def _outproj_kernel(y_ref, x_ref, mod_ref, w_ref, *rest, final):
    m = jnp.dot(y_ref[0], w_ref[...], preferred_element_type=F32)
    xn = x_ref[0] + mod_ref[0, 2:3, :] * m
    if final:
        gain_ref, o_ref = rest
        var = jnp.mean(xn * xn, axis=-1, keepdims=True)
        xn = xn * lax.rsqrt(var + EPS) * gain_ref[...]
    else:
        (o_ref,) = rest
    o_ref[0] = xn


def _outproj(y, x, mod, w, tm, shared_mod, final_gain=None):
    B, T, _ = x.shape
    final = final_gain is not None
    mod_map = (lambda b, t: (0, 0, 0)) if shared_mod else (lambda b, t: (b, 0, 0))
    in_specs = [
        pl.BlockSpec((1, tm, D_MODEL), lambda b, t: (b, t, 0)),
        pl.BlockSpec((1, tm, D_MODEL), lambda b, t: (b, t, 0)),
        pl.BlockSpec((1, 3, D_MODEL), mod_map),
        pl.BlockSpec((D_MODEL, D_MODEL), lambda b, t: (0, 0)),
    ]
    args = [y, x, mod, w]
    if final:
        in_specs.append(pl.BlockSpec((1, D_MODEL), lambda b, t: (0, 0)))
        args.append(final_gain)
    return pl.pallas_call(
        functools.partial(_outproj_kernel, final=final),
        out_shape=jax.ShapeDtypeStruct((B, T, D_MODEL), F32),
        grid=(B, T // tm),
        in_specs=in_specs,
        out_specs=pl.BlockSpec((1, tm, D_MODEL), lambda b, t: (b, t, 0)),
        compiler_params=_cparams(2),
        name="out_proj",
    )(*args)


def _hgrn_kernel(q_ref, zf_ref, zb_ref, v_ref, g_ref, lgt_ref, gain_ref, *rest, seq, layer, has_s0, emit_state):
    rest = list(rest)
    s0_ref = rest.pop(0) if has_s0 else None
    o_ref = rest.pop(0)
    sfin_ref = rest.pop(0) if emit_state else None
    acc_ref, qd_ref, ki_ref, kd_ref, kv_ref, st_ref, dec_ref, mst_ref, msk_ref, mexp_ref = rest
    R = HGRN_ROWS
    C = HGRN_CHUNK
    n_blk = seq // R
    n_chunk = R // C
    n_all = seq // C

    @pl.when((pl.program_id(0) == 0) & (pl.program_id(1) == 0))
    def _build_masks():
        ti = lax.broadcasted_iota(jnp.int32, (R, R), 0)
        tj = lax.broadcasted_iota(jnp.int32, (R, R), 1)
        shift = C.bit_length() - 1
        same = lax.shift_right_logical(ti, shift) == lax.shift_right_logical(tj, shift)
        one = jnp.ones((R, R), F32)
        zero = jnp.zeros((R, R), F32)
        incl = (jnp.where(same, jnp.where(tj <= ti, one, zero), zero),
                jnp.where(same, jnp.where(tj >= ti, one, zero), zero))
        for d in range(2):
            msk_ref[d] = incl[d]
            mst_ref[d] = incl[d].astype(BF16)
        rr = lax.broadcasted_iota(jnp.int32, (R, n_chunk * LANES), 0)
        cc = lax.broadcasted_iota(jnp.int32, (R, n_chunk * LANES), 1)
        own = lax.shift_right_logical(rr, shift) == lax.shift_right_logical(cc, LANES.bit_length() - 1)
        mexp_ref[...] = jnp.where(own, 1.0, 0.0).astype(BF16)

    lgt = [lgt_ref[:, i, :] for i in range(lgt_ref.shape[1])]
    lmax = functools.reduce(jnp.maximum, lgt)
    ex = [jnp.exp(t - lmax) for t in lgt]
    lb_all = sum(ex[:layer + 1]) / sum(ex)
    gain = gain_ref[0]

    blocks_per_trip = 2 if n_blk % 2 == 0 else 1

    def gates(i, _):
        for u in range(blocks_per_trip):
            blk = i * blocks_per_trip + u
            rows = pl.ds(pl.multiple_of(blk * R, R), R)
            q = q_ref[0, rows, :]
            for d in range(2):
                z = (zf_ref if d == 0 else zb_ref)[0, rows, :]
                lb = lb_all[d:d + 1, :]
                oml = 1.0 - lb
                e = jnp.exp(-jnp.abs(z))
                r = 1.0 / (1.0 + e)
                er = e * r
                pos = z >= 0.0
                f = lb + oml * jnp.where(pos, r, er)
                k = oml * jnp.where(pos, er, r)
                logf = jnp.log(f)
                hi = logf.astype(BF16)
                lo = (logf - hi.astype(F32)).astype(BF16)
                cs = jnp.dot(mst_ref[d], jnp.concatenate([hi, lo], axis=1), preferred_element_type=F32)
                b = cs[:, 0:LANES] + cs[:, LANES:2 * LANES]
                ends = [c * C + (C - 1 if d == 0 else 0) for c in range(n_chunk)]
                btot = jnp.concatenate([jnp.broadcast_to(b[t:t + 1, :], (C, LANES)) for t in ends], axis=0)
                qd_ref[d, rows, :] = (q * jnp.exp(b)).astype(BF16)
                ki_ref[d, rows, :] = (k * jnp.exp(-b)).astype(BF16)
                kd_ref[d, rows, :] = (k * jnp.exp(btot - b)).astype(BF16)
                for c in range(n_chunk):
                    dec_ref[d, blk * n_chunk + c] = jnp.exp(btot[c * C:c * C + 8, :])
        return 0

    lax.fori_loop(0, n_blk // blocks_per_trip, gates, 0)

    def intra(i, _):
        for u in range(blocks_per_trip):
            blk = i * blocks_per_trip + u
            rows = pl.ds(pl.multiple_of(blk * R, R), R)
            v = v_ref[0, rows, :]
            vb = v.astype(BF16)
            vt = v.T.astype(BF16)
            for d in range(2):
                att = lax.dot_general(qd_ref[d, rows, :], ki_ref[d, rows, :], NT_DIMS,
                                      preferred_element_type=F32)
                att = jnp.where(msk_ref[d] > 0.5, att, 0.0)
                acc_ref[d, rows, :] = jnp.dot(att.astype(BF16), vb, preferred_element_type=F32)
                kd_exp = jnp.concatenate([kd_ref[d, rows, :]] * n_chunk, axis=1) * mexp_ref[...]
                kv_all = jnp.dot(vt, kd_exp, preferred_element_type=F32)
                for c in range(n_chunk):
                    kv_ref[d, blk * n_chunk + c] = kv_all[:, c * LANES:(c + 1) * LANES]
        return 0

    lax.fori_loop(0, n_blk // blocks_per_trip, intra, 0)

    unroll = 4

    def states(i, sts):
        sts = list(sts)
        for u in range(unroll):
            n = i * unroll + u
            for d in range(2):
                c = n if d == 0 else n_all - 1 - n
                st_ref[d, c] = sts[d].astype(BF16)
                dec = jnp.concatenate([dec_ref[d, c]] * (DK_A // 8), axis=0)
                sts[d] = sts[d] * dec + kv_ref[d, c]
        return tuple(sts)

    if has_s0:
        st0 = (s0_ref[0, 0, 0].T, s0_ref[0, 1, 0].T)
    else:
        st0 = (jnp.zeros((DK_A, DK_A), F32),) * 2
    sts = lax.fori_loop(0, n_all // unroll, states, st0)
    if emit_state:
        for d in range(2):
            sfin_ref[0, d, 0] = sts[d].T

    def finish(blk, _):
        rows = pl.ds(pl.multiple_of(blk * R, R), R)
        tot = acc_ref[0, rows, :] + acc_ref[1, rows, :]
        for d in range(2):
            pieces = []
            for c in range(n_chunk):
                crow = pl.ds(pl.multiple_of(blk * R + c * C, C), C)
                pieces.append(lax.dot_general(qd_ref[d, crow, :], st_ref[d, blk * n_chunk + c], NT_DIMS,
                                              preferred_element_type=F32))
            tot = tot + jnp.concatenate(pieces, axis=0)
        var = jnp.mean(tot * tot, axis=-1, keepdims=True)
        y = tot * lax.rsqrt(var + EPS) * gain
        o_ref[0, rows, :] = (y * _silu(g_ref[0, rows, :])).astype(o_ref.dtype)
        return 0

    lax.fori_loop(0, n_blk, finish, 0)


def _hgrn(ya, lgt, layer, gain, s0, emit_state):
    B, T, _ = ya.shape
    has_s0 = s0 is not None

    def col(k):
        return pl.BlockSpec((1, T, LANES), lambda b, h, k=k: (b, 0, k * H_A + h))

    in_specs = [col(0), col(1), col(2), col(3), col(4),
                pl.BlockSpec((2, lgt.shape[1], LANES), lambda b, h: (0, 0, h)),
                pl.BlockSpec((1, 1, LANES), lambda b, h: (h, 0, 0))]
    args = [ya, ya, ya, ya, ya, lgt, gain]
    if has_s0:
        in_specs.append(pl.BlockSpec((1, 2, 1, DK_A, DK_A), lambda b, h: (b, 0, h, 0, 0)))
        args.append(s0)
    out_shape = [jax.ShapeDtypeStruct((B, T, H_A * DK_A), BF16)]
    out_specs = [pl.BlockSpec((1, T, LANES), lambda b, h: (b, 0, h))]
    if emit_state:
        out_shape.append(jax.ShapeDtypeStruct((B, 2, H_A, DK_A, DK_A), F32))
        out_specs.append(pl.BlockSpec((1, 2, 1, DK_A, DK_A), lambda b, h: (b, 0, h, 0, 0)))
    res = pl.pallas_call(
        functools.partial(_hgrn_kernel, seq=T, layer=layer, has_s0=has_s0, emit_state=emit_state),
        out_shape=out_shape,
        grid=(B, H_A),
        in_specs=in_specs,
        out_specs=out_specs,
        scratch_shapes=[pltpu.VMEM((2, T, LANES), F32),
                        pltpu.VMEM((2, T, LANES), BF16),
                        pltpu.VMEM((2, T, LANES), BF16),
                        pltpu.VMEM((2, T, LANES), BF16),
                        pltpu.VMEM((2, T // HGRN_CHUNK, DK_A, DK_A), F32),
                        pltpu.VMEM((2, T // HGRN_CHUNK, DK_A, DK_A), BF16),
                        pltpu.VMEM((2, T // HGRN_CHUNK, 8, LANES), F32),
                        pltpu.VMEM((2, HGRN_ROWS, HGRN_ROWS), BF16),
                        pltpu.VMEM((2, HGRN_ROWS, HGRN_ROWS), F32),
                        pltpu.VMEM((HGRN_ROWS, HGRN_ROWS // HGRN_CHUNK * LANES), BF16)],
        compiler_params=_cparams(2),
        name="hgrn2",
    )(*args)
    return res if emit_state else (res[0], None)


def _head_masks():
    lane = lax.broadcasted_iota(jnp.int32, (1, LANES), 1)
    return lane < DH_B, lane >= DH_B


def _ctx_attn_kernel(q_ref, k_ref, v_ref, g_ref, o_ref):
    scale = DH_B ** -0.5
    masks = _head_masks()
    T = q_ref.shape[1]
    for p in range(H_B // 2):
        cols = slice(p * LANES, (p + 1) * LANES)
        q = q_ref[0, :, cols] * scale
        qs = jnp.concatenate([jnp.where(masks[h], q, jnp.zeros_like(q)) for h in range(2)], axis=0)
        s = lax.dot_general(qs, k_ref[0, :, cols], NT_DIMS, preferred_element_type=F32)
        e = jnp.exp(s - jnp.max(s, axis=-1, keepdims=True))
        pr = e / jnp.sum(e, axis=-1, keepdims=True)
        o = jnp.dot(pr.astype(BF16), v_ref[0, :, cols], preferred_element_type=F32)
        o = jnp.where(masks[0], o[0:T], o[T:2 * T])
        o_ref[0, :, cols] = (o * _silu(g_ref[0, :, cols].astype(F32))).astype(o_ref.dtype)


def _ctx_attn(yb):
    B, T, _ = yb.shape
    width = H_B * DH_B

    def col(k):
        return pl.BlockSpec((1, T, width), lambda b, k=k: (b, 0, k))

    return pl.pallas_call(
        _ctx_attn_kernel,
        out_shape=jax.ShapeDtypeStruct((B, T, width), BF16),
        grid=(B,),
        in_specs=[col(0), col(1), col(2), col(3)],
        out_specs=pl.BlockSpec((1, T, width), lambda b: (b, 0, 0)),
        compiler_params=_cparams(1),
        name="ctx_attn",
    )(yb, yb, yb, yb)


N_DR = 2 * NA_KH - 1
N_DC = 2 * NA_KW - 1
N_TAB = N_DR - 1


def _nat_kernel(rb_ref, q_ref, k_ref, v_ref, g_ref, kc_ref, vc_ref, o_ref,
                tab_ref, qs_ref, s_ref, p_ref, r_ref, oc_ref, *, rows):
    p = pl.program_id(0)
    scale = DH_B ** -0.5
    kh = min(NA_KH, rows)
    masks = _head_masks()

    @pl.when(pl.program_id(1) == 0)
    def _build_tables():
        c = lax.broadcasted_iota(jnp.int32, (GRID_W, LANES), 0)
        lane = lax.broadcasted_iota(jnp.int32, (GRID_W, LANES), 1)
        kcol = lane & (GRID_W - 1)
        upper = lane >= GRID_W
        ws = jnp.clip(c - NA_KW // 2, 0, GRID_W - NA_KW)
        neg = jnp.full((GRID_W, LANES), NEG_INF, F32)
        diag = kcol - c + (NA_KW - 1)
        for h in range(2):
            base = (2 * p + h) * (N_DR * N_DC)

            def per_dr(i, _, base=base, h=h):
                def per_dc(dd, acc):
                    lo = rb_ref[base + i * N_DC + dd]
                    hi = rb_ref[base + (i + 1) * N_DC + dd]
                    return jnp.where(diag == dd, jnp.where(upper, hi, lo), acc)

                acc = lax.fori_loop(0, N_DC, per_dc, neg)
                acc = jnp.where(kcol >= ws, jnp.where(kcol < ws + NA_KW, acc, neg), neg)
                tab_ref[h, i] = acc
                return 0

            lax.fori_loop(0, N_TAB, per_dr, 0)

    kc, vc = kc_ref[0], vc_ref[0]
    n_keys = kh * GRID_W
    n_ctx = kc.shape[0]
    G = NA_GROUP
    W2 = 2 * GRID_W

    def group(gi, _):
        r_first = gi * G
        q0 = pl.multiple_of(r_first * GRID_W, G * GRID_W)
        for i in range(G):
            qi = q_ref[0, pl.ds(q0 + i * GRID_W, GRID_W), :] * scale
            for h in range(2):
                qs_ref[i * W2 + h * GRID_W:i * W2 + (h + 1) * GRID_W, :] = jnp.where(
                    masks[h], qi, jnp.zeros_like(qi))
        s_ref[:, n_keys:n_keys + n_ctx] = lax.dot_general(qs_ref[...], kc, NT_DIMS,
                                                          preferred_element_type=F32)
        windows = []
        for i in range(G):
            r = r_first + i
            rs = jnp.clip(r - kh // 2, 0, rows - kh)
            k0 = pl.multiple_of(rs * GRID_W, GRID_W)
            windows.append(k0)
            dr0 = rs - r + (NA_KH - 1)
            bias = jnp.concatenate(
                [jnp.concatenate([tab_ref[h, dr0 + 2 * m] for m in range(kh // 2)], axis=1)
                 for h in range(2)], axis=0)
            s_ref[i * W2:(i + 1) * W2, 0:n_keys] = lax.dot_general(
                qs_ref[i * W2:(i + 1) * W2, :], k_ref[0, pl.ds(k0, n_keys), :], NT_DIMS,
                preferred_element_type=F32) + bias
        for i in range(G):
            s = s_ref[i * W2:(i + 1) * W2, :]
            e = jnp.exp(s - jnp.max(s, axis=-1, keepdims=True))
            p_ref[i * W2:(i + 1) * W2, :] = e.astype(BF16)
            rinv = 1.0 / jnp.sum(e, axis=-1, keepdims=True)
            r_ref[i * W2:(i + 1) * W2, :] = jnp.broadcast_to(rinv, (W2, LANES))
        oc_ref[...] = jnp.dot(p_ref[:, n_keys:n_keys + n_ctx], vc, preferred_element_type=F32)
        for i in range(G):
            o = jnp.dot(p_ref[i * W2:(i + 1) * W2, 0:n_keys], v_ref[0, pl.ds(windows[i], n_keys), :],
                        preferred_element_type=F32)
            o = (o + oc_ref[i * W2:(i + 1) * W2, :]) * r_ref[i * W2:(i + 1) * W2, :]
            o = jnp.where(masks[0], o[0:GRID_W], o[GRID_W:W2])
            out_rows = pl.ds(q0 + i * GRID_W, GRID_W)
            gate = g_ref[0, out_rows, :].astype(F32)
            o_ref[0, out_rows, :] = (o * _silu(gate)).astype(o_ref.dtype)
        return 0

    lax.fori_loop(0, rows // G, group, 0)


def _nat(yb, kc, vc, rel_bias):
    B, T, _ = yb.shape
    Tc = kc.shape[1]
    n_pair = H_B // 2
    rows = T // GRID_W
    n_stack = NA_GROUP * 2 * GRID_W
    n_keys = min(NA_KH, rows) * GRID_W

    def col(k):
        return pl.BlockSpec((1, T, LANES), lambda p, b, k=k: (b, 0, k * n_pair + p))

    ctx =pl.BlockSpec((1, Tc, LANES), lambda p, b: (b, 0, p))
    return pl.pallas_call(
        functools.partial(_nat_kernel, rows=rows),
        out_shape=jax.ShapeDtypeStruct((B, T, H_B * DH_B), BF16),
        grid=(n_pair, B),
        in_specs=[pl.BlockSpec(memory_space=pltpu.SMEM), col(0), col(1), col(2), col(3), ctx, ctx],
        out_specs=pl.BlockSpec((1, T, LANES), lambda p, b: (b, 0, p)),
        scratch_shapes=[pltpu.VMEM((2, N_TAB, GRID_W, LANES), F32),
                        pltpu.VMEM((n_stack, LANES), BF16),
                        pltpu.VMEM((n_stack, n_keys + Tc), F32),
                        pltpu.VMEM((n_stack, n_keys + Tc), BF16),
                        pltpu.VMEM((n_stack, LANES), F32),
                        pltpu.VMEM((n_stack, LANES), F32)],
        compiler_params=_cparams(2),
        name="nbr_attn",
    )(rel_bias.reshape(-1), yb, yb, yb, yb, kc, vc)


def _seg_len(seq):
    length = -(-seq // N_SEG)
    while length % 8 != 4:
        length += 1
    return length


def _rglru_kernel(x_ref, g_ref, cw_ref, cb_ref, wg_ref, bg_ref, lam_ref, *rest, seq, has_s0, emit_state):
    rest = list(rest)
    s0_ref = rest.pop(0) if has_s0 else None
    o_ref = rest.pop(0)
    hfin_ref = rest.pop(0) if emit_state else None
    xpad_ref, a_ref, u_ref, h_ref, p_ref = rest
    L = _seg_len(seq)
    n_pad = N_SEG * L - seq
    RB = RG_ROWS
    CB = RG_SLABS
    n_tile = N_SEG // 8
    chains = [(d, j, s) for d in range(2) for j in range(CB) for s in range(n_tile)]

    xpad_ref[0:8, :] = jnp.zeros((8, CB * LANES), F32)
    xpad_ref[seq + 8:seq + 16, :] = jnp.zeros((8, CB * LANES), F32)
    xpad_ref[8:seq + 8, :] = x_ref[0]
    for d in range(2):
        for j in range(CB):
            a_ref[d, j, seq:seq + n_pad, :] = jnp.ones((n_pad, LANES), F32)
            u_ref[d, j, seq:seq + n_pad, :] = jnp.zeros((n_pad, LANES), F32)

    nl = -lam_ref[...]
    sp = jnp.maximum(nl, 0.0) + jnp.log1p(jnp.exp(-jnp.abs(nl)))
    cw = cw_ref[...]
    cbias = cb_ref[...]

    def gates(blk, _):
        r0 = pl.multiple_of(blk * RB, RB)
        xm = xpad_ref[pl.ds(r0, RB + 16), :]
        xc = cw[0:1] * xm[6:6 + RB] + cw[1:2] * xm[7:7 + RB]
        xc = xc + cw[2:3] * xm[8:8 + RB]
        xc = xc + cw[3:4] * xm[9:9 + RB] + cbias
        for j in range(CB):
            xj = xc[:, j * LANES:(j + 1) * LANES]
            gt = jnp.dot(xj.astype(BF16), wg_ref[j], preferred_element_type=F32) + bg_ref[j]
            for d in range(2):
                rg = jax.nn.sigmoid(gt[:, (2 * d) * LANES:(2 * d + 1) * LANES])
                ig = jax.nn.sigmoid(gt[:, (2 * d + 1) * LANES:(2 * d + 2) * LANES])
                la = (-RG_C * rg) * sp[d:d + 1, j * LANES:(j + 1) * LANES]
                a = jnp.exp(la)
                y = -jnp.tanh(la) * (1.0 + a * a)
                root = jnp.where(y > 0.0, y * lax.rsqrt(y), 0.0)
                a_ref[d, j, pl.ds(r0, RB), :] = a
                u_ref[d, j, pl.ds(r0, RB), :] = root * (ig * xj)
        return 0

    lax.fori_loop(0, seq // RB, gates, 0)

    def seg_rows(d, s, step):
        t = step if d == 0 else L - 1 - step
        return pl.ds(t + s * 8 * L, 8, stride=L)

    unroll = 4

    def scan(i, carry):
        carry = list(carry)
        for k in range(unroll):
            for n, (d, j, s) in enumerate(chains):
                h, pr = carry[n]
                idx = seg_rows(d, s, i * unroll + k)
                a = a_ref[d, j, idx, :]
                h = a * h + u_ref[d, j, idx, :]
                pr = pr * a
                h_ref[d, j, idx, :] = h
                p_ref[d, j, idx, :] = pr
                carry[n] = (h, pr)
        return tuple(carry)

    zero = jnp.zeros((8, LANES), F32)
    one = jnp.ones((8, LANES), F32)
    ends = lax.fori_loop(0, L // unroll, scan, ((zero, one),) * len(chains))

    cins = {}
    finals = [[None] * CB for _ in range(2)]
    for d in range(2):
        for j in range(CB):
            if has_s0:
                c = s0_ref[0, d:d + 1, j * LANES:(j + 1) * LANES]
            else:
                c = jnp.zeros((1, LANES), F32)
            cin = [None] * N_SEG
            for kk in range(N_SEG):
                seg = kk if d == 0 else N_SEG - 1 - kk
                s, row = divmod(seg, 8)
                h_end, p_end = ends[chains.index((d, j, s))]
                cin[seg] = c
                c = h_end[row:row + 1, :] + p_end[row:row + 1, :] * c
            finals[d][j] = c
            for s in range(n_tile):
                cins[(d, j, s)] = jnp.concatenate(cin[s * 8:(s + 1) * 8], axis=0)

    def fix(i, _):
        for k in range(unroll):
            for (d, j, s) in chains:
                idx = seg_rows(0, s, i * unroll + k)
                h_ref[d, j, idx, :] = h_ref[d, j, idx, :] + p_ref[d, j, idx, :] * cins[(d, j, s)]
        return 0

    lax.fori_loop(0, L // unroll, fix, 0)

    if emit_state:
        hfin_ref[0] = jnp.concatenate([jnp.concatenate(finals[d], axis=1) for d in range(2)], axis=0)

    def combine(blk, _):
        rows = pl.ds(pl.multiple_of(blk * RB, RB), RB)
        hs = jnp.concatenate([h_ref[0, j, rows, :] + h_ref[1, j, rows, :] for j in range(CB)], axis=1)
        o_ref[0, rows, :] = (hs * _silu(g_ref[0, rows, :])).astype(o_ref.dtype)
        return 0

    lax.fori_loop(0, seq // RB, combine, 0)


def _rglru(xg, conv_w, conv_b, wg, bg, lam, s0, emit_state):
    B, T, _ = xg.shape
    has_s0 = s0 is not None
    n_rows = N_SEG * _seg_len(T)
    CB = RG_SLABS
    wide = CB * LANES
    n_steps = H_C // CB
    in_specs = [
        pl.BlockSpec((1, T, wide), lambda b, c: (b, 0, c)),
        pl.BlockSpec((1, T, wide), lambda b, c: (b, 0, n_steps + c)),
        pl.BlockSpec((4, wide), lambda b, c: (0, c)),
        pl.BlockSpec((1, wide), lambda b, c: (0, c)),
        pl.BlockSpec((CB, BW_C, 4 * BW_C), lambda b, c: (c, 0, 0)),
        pl.BlockSpec((CB, 1, 4 * BW_C), lambda b, c: (c, 0, 0)),
        pl.BlockSpec((2, wide), lambda b, c: (0, c)),
    ]
    args = [xg, xg, conv_w, conv_b, wg, bg, lam]
    if has_s0:
        in_specs.append(pl.BlockSpec((1, 2, wide), lambda b, c: (b, 0, c)))
        args.append(s0)
    out_shape = [jax.ShapeDtypeStruct((B, T, W_C), BF16)]
    out_specs = [pl.BlockSpec((1, T, wide), lambda b, c: (b, 0, c))]
    if emit_state:
        out_shape.append(jax.ShapeDtypeStruct((B, 2, W_C), F32))
        out_specs.append(pl.BlockSpec((1, 2, wide), lambda b, c: (b, 0, c)))
    res = pl.pallas_call(
        functools.partial(_rglru_kernel, seq=T, has_s0=has_s0, emit_state=emit_state),
        out_shape=out_shape,
        grid=(B, n_steps),
        in_specs=in_specs,
        out_specs=out_specs,
        scratch_shapes=[pltpu.VMEM((T + 16, wide), F32)] + [pltpu.VMEM((2, CB, n_rows, LANES), F32)] * 4,
        compiler_params=_cparams(2),
        name="rglru",
    )(*args)
    return res if emit_state else (res[0], None)


A_COLS = 5 * H_A * DK_A
B_COLS = 4 * H_B * DH_B


def kernel(x_prompt, x_sample, state_hgrn, cache_na_k, cache_na_v, state_rglru, c, c_ctx, norm_gain, w_mod, b_mod, w_in_even, w_out_even, hgrn_lb_logits, hgrn_out_gain, na_rel_bias, w_in_odd, w_out_odd, conv_w, conv_b, rg_gate_w, rg_gate_b, rg_lambda, final_gain):
    n_ctx = x_prompt.shape[0]
    n_lat = x_sample.shape[0]
    depth = w_mod.shape[0]

    cond = jnp.zeros((16, D_MODEL), F32).at[0].set(c_ctx).at[1:1 + n_lat].set(c)
    mod = _modulation(cond, w_mod.astype(BF16), b_mod.reshape(depth, 1, 3 * D_MODEL))
    mod = mod.reshape(depth, 16, 3, D_MODEL)

    xc, xs = x_prompt, x_sample
    new_hgrn, new_k, new_v, new_rg = [], [], [], []
    for l in range(depth):
        j = l // 2
        mod_c, mod_s = mod[l, 0:1], mod[l, 1:1 + n_lat]
        gain = norm_gain[l].reshape(1, D_MODEL)
        last = l == depth - 1
        fgain = final_gain.reshape(1, D_MODEL) if last else None
        if l % 2 == 0:
            w_in = w_in_even[j].astype(BF16)
            w_out = w_out_even[j].astype(BF16)
            outs_s = ((0, A_COLS, F32), (A_COLS, B_COLS, BF16))
            outs_c = outs_s + ((A_COLS + H_B * DH_B, 2 * H_B * DH_B, F32),)
            ya_c, yb_c, kv_c = _inproj(xc, mod_c, gain, w_in, outs_c, 256, True)
            ya_s, yb_s = _inproj(xs, mod_s, gain, w_in, outs_s, 512, False)
            hgain = hgrn_out_gain[j].reshape(H_A, 1, DK_A)
            oa_c, s_fin = _hgrn(ya_c, hgrn_lb_logits, j, hgain, None, True)
            oa_s, _ = _hgrn(ya_s, hgrn_lb_logits, j, hgain, state_hgrn[:, j], False)
            ob_c = _ctx_attn(yb_c)
            tc = cache_na_k.shape[3]
            kc = cache_na_k[:, j].transpose(0, 2, 1, 3).reshape(n_lat, tc, H_B * DH_B).astype(BF16)
            vc = cache_na_v[:, j].transpose(0, 2, 1, 3).reshape(n_lat, tc, H_B * DH_B).astype(BF16)
            ob_s = _nat(yb_s, kc, vc, na_rel_bias[j])
            y_c = jnp.concatenate([oa_c, ob_c], axis=-1)
            y_s = jnp.concatenate([oa_s, ob_s], axis=-1)
            xc = _outproj(y_c, xc, mod_c, w_out, 256, True, fgain)
            xs = _outproj(y_s, xs, mod_s, w_out, 512, False, fgain)
            t_c = kv_c.shape[1]
            heads = kv_c.reshape(n_ctx, t_c, 2, H_B, DH_B).transpose(2, 0, 3, 1, 4)
            new_hgrn.append(s_fin)
            new_k.append(heads[0])
            new_v.append(heads[1])
        else:
            w_in = w_in_odd[j].astype(BF16)
            w_out = w_out_odd[j].astype(BF16)
            outs = ((0, 2 * W_C, F32),)
            (xg_c,) = _inproj(xc, mod_c, gain, w_in, outs, 256, True)
            (xg_s,) = _inproj(xs, mod_s, gain, w_in, outs, 512, False)
            wg = rg_gate_w[j].transpose(2, 3, 0, 1, 4).reshape(H_C, BW_C, 4 * BW_C).astype(BF16)
            bg = rg_gate_b[j].reshape(2, 2, H_C, BW_C).transpose(2, 0, 1, 3).reshape(H_C, 1, 4 * BW_C)
            cb = conv_b[j].reshape(1, W_C)
            y_c, h_fin = _rglru(xg_c, conv_w[j], cb, wg, bg, rg_lambda[j], None, True)
            y_s, _ = _rglru(xg_s, conv_w[j], cb, wg, bg, rg_lambda[j], state_rglru[:, j], False)
            xc = _outproj(y_c, xc, mod_c, w_out, 256, True, fgain)
            xs = _outproj(y_s, xs, mod_s, w_out, 512, False, fgain)
            new_rg.append(h_fin)
    return (xc, xs, jnp.stack(new_hgrn, axis=1), jnp.stack(new_k, axis=1),
            jnp.stack(new_v, axis=1), jnp.stack(new_rg, axis=1))
```

```python
import functools

import jax
import jax.numpy as jnp
from jax import lax
from jax.experimental import pallas as pl
from jax.experimental.pallas import tpu as pltpu

F32 = jnp.float32
BF16 = jnp.bfloat16

D_MODEL = 1024
EPS = 1e-6
NEG_INF = -1e30
H_A = 4
DK_A = 128
HGRN_CHUNK = 32
HGRN_ROWS = 256
H_B = 8
DH_B = 64
GRID_W = 64
NA_KH = 8
NA_KW = 16
NA_GROUP = 8
W_C = 1024
H_C = 8
BW_C = W_C // H_C
RG_C = 8.0
RG_ROWS = 256
RG_SLABS = 2
N_SEG = 16
LANES = 128
VMEM_LIMIT = 48 * 1024 * 1024

NT_DIMS = (((1,), (1,)), ((), ()))
TN_DIMS = (((0,), (0,)), ((), ()))


def _silu(x):
    return x * jax.nn.sigmoid(x)


def _cparams(n_axes):
    return pltpu.CompilerParams(dimension_semantics=("arbitrary",) * n_axes,
                                vmem_limit_bytes=VMEM_LIMIT)


def _mod_kernel(cond_ref, w_ref, b_ref, o_ref):
    s = _silu(cond_ref[...])
    o_ref[0] = jnp.dot(s.astype(BF16), w_ref[0].astype(BF16), preferred_element_type=F32) + b_ref[0]


def _modulation(cond, w_mod, b_mod):
    depth = w_mod.shape[0]
    n_rows = cond.shape[0]
    return pl.pallas_call(
        _mod_kernel,
        out_shape=jax.ShapeDtypeStruct((depth, n_rows, 3 * D_MODEL), F32),
        grid=(depth, 3),
        in_specs=[
            pl.BlockSpec((n_rows, D_MODEL), lambda l, n: (0, 0)),
            pl.BlockSpec((1, D_MODEL, D_MODEL), lambda l, n: (l, 0, n)),
            pl.BlockSpec((1, 1, D_MODEL), lambda l, n: (l, 0, n)),
        ],
        out_specs=pl.BlockSpec((1, n_rows, D_MODEL), lambda l, n: (l, 0, n)),
        compiler_params=_cparams(2),
        name="adaln_mod",
    )(cond, w_mod, b_mod)


def _inproj_kernel(x_ref, mod_ref, gain_ref, w_ref, *out_refs, outs):
    x = x_ref[0]
    var = jnp.mean(x * x, axis=-1, keepdims=True)
    y = x * lax.rsqrt(var + EPS) * gain_ref[...]
    h = y * (1.0 + mod_ref[0, 1:2, :]) + mod_ref[0, 0:1, :]
    hb = h.astype(BF16)
    step = 512
    for o_ref, (col0, width, _) in zip(out_refs, outs):
        for c in range(0, width, step):
            r = jnp.dot(hb, w_ref[:, col0 + c:col0 + c + step], preferred_element_type=F32)
            o_ref[0, :, c:c + step] = r.astype(o_ref.dtype)


def _inproj(x, mod, gain, w, outs, tm, shared_mod):
    B, T, _ = x.shape
    n_cols = w.shape[1]
    mod_map = (lambda b, t: (0, 0, 0)) if shared_mod else (lambda b, t: (b, 0, 0))
    return pl.pallas_call(
        functools.partial(_inproj_kernel, outs=outs),
        out_shape=[jax.ShapeDtypeStruct((B, T, wd), dt) for _, wd, dt in outs],
        grid=(B, T // tm),
        in_specs=[
            pl.BlockSpec((1, tm, D_MODEL), lambda b, t: (b, t, 0)),
            pl.BlockSpec((1, 3, D_MODEL), mod_map),
            pl.BlockSpec((1, D_MODEL), lambda b, t: (0, 0)),
            pl.BlockSpec((D_MODEL, n_cols), lambda b, t: (0, 0)),
        ],
        out_specs=[pl.BlockSpec((1, tm, wd), lambda b, t: (b, t, 0)) for _, wd, _ in outs],
        compiler_params=_cparams(2),
        name="in_proj",
    )(x, mod, gain, w)


def _outproj_kernel(*refs, n_y, final):
    y_refs, (x_ref, mod_ref, w_ref), rest = refs[:n_y], refs[n_y:n_y + 3], refs[n_y + 3:]
    m = None
    row = 0
    for y_ref in y_refs:
        width = y_ref.shape[-1]
        part = jnp.dot(y_ref[0], w_ref[row:row + width, :], preferred_element_type=F32)
        m = part if m is None else m + part
        row += width
    xn = x_ref[0] + mod_ref[0, 2:3, :] * m
    if final:
        gain_ref, o_ref = rest
        var = jnp.mean(xn * xn, axis=-1, keepdims=True)
        xn = xn * lax.rsqrt(var + EPS) * gain_ref[...]
    else:
        (o_ref,) = rest
    o_ref[0] = xn


def _outproj(ys, x, mod, w, tm, shared_mod, final_gain=None):
    B, T, _ = x.shape
    final = final_gain is not None
    mod_map = (lambda b, t: (0, 0, 0)) if shared_mod else (lambda b, t: (b, 0, 0))
    in_specs = [pl.BlockSpec((1, tm, y.shape[-1]), lambda b, t: (b, t, 0)) for y in ys] + [
        pl.BlockSpec((1, tm, D_MODEL), lambda b, t: (b, t, 0)),
        pl.BlockSpec((1, 3, D_MODEL), mod_map),
        pl.BlockSpec((w.shape[0], D_MODEL), lambda b, t: (0, 0)),
    ]
    args = list(ys) + [x, mod, w]
    if final:
        in_specs.append(pl.BlockSpec((1, D_MODEL), lambda b, t: (0, 0)))
        args.append(final_gain)
    return pl.pallas_call(
        functools.partial(_outproj_kernel, n_y=len(ys), final=final),
        out_shape=jax.ShapeDtypeStruct((B, T, D_MODEL), F32),
        grid=(B, T // tm),
        in_specs=in_specs,
        out_specs=pl.BlockSpec((1, tm, D_MODEL), lambda b, t: (b, t, 0)),
        compiler_params=_cparams(2),
        name="out_proj",
    )(*args)


def _hgrn_kernel(q_ref, zf_ref, zb_ref, v_ref, g_ref, lgt_ref, gain_ref, *rest, seq, layer, has_s0, emit_state):
    rest = list(rest)
    s0_ref = rest.pop(0) if has_s0 else None
    o_ref = rest.pop(0)
    sfin_ref = rest.pop(0) if emit_state else None
    acc_ref, qd_ref, ki_ref, kd_ref, kv_ref, st_ref, dec_ref, mst_ref, msk_ref, mexp_ref = rest
    R = HGRN_ROWS
    C = HGRN_CHUNK
    n_blk = seq // R
    n_chunk = R // C
    n_all = seq // C

    @pl.when((pl.program_id(0) == 0) & (pl.program_id(1) == 0))
    def _build_masks():
        ti = lax.broadcasted_iota(jnp.int32, (R, R), 0)
        tj = lax.broadcasted_iota(jnp.int32, (R, R), 1)
        shift = C.bit_length() - 1
        same = lax.shift_right_logical(ti, shift) == lax.shift_right_logical(tj, shift)
        one = jnp.ones((R, R), F32)
        zero = jnp.zeros((R, R), F32)
        incl = (jnp.where(same, jnp.where(tj <= ti, one, zero), zero),
                jnp.where(same, jnp.where(tj >= ti, one, zero), zero))
        for d in range(2):
            msk_ref[d] = incl[d]
            mst_ref[d] = incl[d].astype(BF16)
        rr = lax.broadcasted_iota(jnp.int32, (R, n_chunk * LANES), 0)
        cc = lax.broadcasted_iota(jnp.int32, (R, n_chunk * LANES), 1)
        own = lax.shift_right_logical(rr, shift) == lax.shift_right_logical(cc, LANES.bit_length() - 1)
        mexp_ref[...] = jnp.where(own, 1.0, 0.0).astype(BF16)

    lgt = [lgt_ref[:, i, :] for i in range(lgt_ref.shape[1])]
    lmax = functools.reduce(jnp.maximum, lgt)
    ex = [jnp.exp(t - lmax) for t in lgt]
    lb_all = sum(ex[:layer + 1]) / sum(ex)
    gain = gain_ref[0]

    def block_rows(blk):
        return pl.ds(blk * R if isinstance(blk, int) else pl.multiple_of(blk * R, R), R)

    def gates(blk, d):
        rows = block_rows(blk)
        z = (zf_ref if d == 0 else zb_ref)[0, rows, :]
        lb = lb_all[d:d + 1, :]
        oml = 1.0 - lb
        e = jnp.exp(-jnp.abs(z))
        r = 1.0 / (1.0 + e)
        er = e * r
        pos = z >= 0.0
        f = lb + oml * jnp.where(pos, r, er)
        k = oml * jnp.where(pos, er, r)
        logf = jnp.log(f)
        yield
        hi = logf.astype(BF16)
        lo = (logf - hi.astype(F32)).astype(BF16)
        cs = jnp.dot(mst_ref[d], jnp.concatenate([hi, lo], axis=1), preferred_element_type=F32)
        b = cs[:, 0:LANES] + cs[:, LANES:2 * LANES]
        yield
        ends = [c * C + (C - 1 if d == 0 else 0) for c in range(n_chunk)]
        btot = jnp.concatenate([jnp.broadcast_to(b[t:t + 1, :], (C, LANES)) for t in ends], axis=0)
        qd_ref[d, rows, :] = (q_ref[0, rows, :] * jnp.exp(b)).astype(BF16)
        ki_ref[d, rows, :] = (k * jnp.exp(-b)).astype(BF16)
        yield
        kd_ref[d, rows, :] = (k * jnp.exp(btot - b)).astype(BF16)
        for c in range(n_chunk):
            dec_ref[d, blk * n_chunk + c] = jnp.exp(btot[c * C:c * C + 8, :])

    def intra(blk, d):
        rows = block_rows(blk)
        att = lax.dot_general(qd_ref[d, rows, :], ki_ref[d, rows, :], NT_DIMS, preferred_element_type=F32)
        yield
        att = jnp.where(msk_ref[d] > 0.5, att, 0.0)
        v = v_ref[0, rows, :]
        acc_ref[d, rows, :] = jnp.dot(att.astype(BF16), v.astype(BF16), preferred_element_type=F32)
        yield
        kd_exp = jnp.concatenate([kd_ref[d, rows, :]] * n_chunk, axis=1) * mexp_ref[...]
        kv_all = jnp.dot(v.T.astype(BF16), kd_exp, preferred_element_type=F32)
        yield
        for c in range(n_chunk):
            kv_ref[d, blk * n_chunk + c] = kv_all[:, c * LANES:(c + 1) * LANES]

    def interleave(*chains):
        chains = list(chains)
        while chains:
            for chain in list(chains):
                if next(chain, "done") == "done":
                    chains.remove(chain)

    interleave(gates(0, 0), gates(0, 1))

    def piped(blk, _):
        interleave(intra(blk, 0), gates(blk + 1, 0), intra(blk, 1), gates(blk + 1, 1))
        return 0

    lax.fori_loop(0, n_blk - 1, piped, 0)
    interleave(intra(n_blk - 1, 0), intra(n_blk - 1, 1))

    unroll = 4

    def states(i, sts):
        sts = list(sts)
        for u in range(unroll):
            n = i * unroll + u
            for d in range(2):
                c = n if d == 0 else n_all - 1 - n
                st_ref[d, c] = sts[d].astype(BF16)
                dec = jnp.concatenate([dec_ref[d, c]] * (DK_A // 8), axis=0)
                sts[d] = sts[d] * dec + kv_ref[d, c]
        return tuple(sts)

    if has_s0:
        st0 = (s0_ref[0, 0, 0].T, s0_ref[0, 1, 0].T)
    else:
        st0 = (jnp.zeros((DK_A, DK_A), F32),) * 2
    sts = lax.fori_loop(0, n_all // unroll, states, st0)
    if emit_state:
        for d in range(2):
            sfin_ref[0, d, 0] = sts[d].T

    def finish(blk, _):
        rows = pl.ds(pl.multiple_of(blk * R, R), R)
        tot = acc_ref[0, rows, :] + acc_ref[1, rows, :]
        for d in range(2):
            pieces = []
            for c in range(n_chunk):
                crow = pl.ds(pl.multiple_of(blk * R + c * C, C), C)
                pieces.append(lax.dot_general(qd_ref[d, crow, :], st_ref[d, blk * n_chunk + c], NT_DIMS,
                                              preferred_element_type=F32))
            tot = tot + jnp.concatenate(pieces, axis=0)
        var = jnp.mean(tot * tot, axis=-1, keepdims=True)
        y = tot * lax.rsqrt(var + EPS) * gain
        o_ref[0, rows, :] = (y * _silu(g_ref[0, rows, :])).astype(o_ref.dtype)
        return 0

    lax.fori_loop(0, n_blk, finish, 0)


def _hgrn(ya, lgt, layer, gain, s0, emit_state):
    B, T, _ = ya.shape
    has_s0 = s0 is not None

    def col(k):
        return pl.BlockSpec((1, T, LANES), lambda b, h, k=k: (b, 0, k * H_A + h))

    in_specs = [col(0), col(1), col(2), col(3), col(4),
                pl.BlockSpec((2, lgt.shape[1], LANES), lambda b, h: (0, 0, h)),
                pl.BlockSpec((1, 1, LANES), lambda b, h: (h, 0, 0))]
    args = [ya, ya, ya, ya, ya, lgt, gain]
    if has_s0:
        in_specs.append(pl.BlockSpec((1, 2, 1, DK_A, DK_A), lambda b, h: (b, 0, h, 0, 0)))
        args.append(s0)
    out_shape = [jax.ShapeDtypeStruct((B, T, H_A * DK_A), BF16)]
    out_specs = [pl.BlockSpec((1, T, LANES), lambda b, h: (b, 0, h))]
    if emit_state:
        out_shape.append(jax.ShapeDtypeStruct((B, 2, H_A, DK_A, DK_A), F32))
        out_specs.append(pl.BlockSpec((1, 2, 1, DK_A, DK_A), lambda b, h: (b, 0, h, 0, 0)))
    res = pl.pallas_call(
        functools.partial(_hgrn_kernel, seq=T, layer=layer, has_s0=has_s0, emit_state=emit_state),
        out_shape=out_shape,
        grid=(B, H_A),
        in_specs=in_specs,
        out_specs=out_specs,
        scratch_shapes=[pltpu.VMEM((2, T, LANES), F32),
                        pltpu.VMEM((2, T, LANES), BF16),
                        pltpu.VMEM((2, T, LANES), BF16),
                        pltpu.VMEM((2, T, LANES), BF16),
                        pltpu.VMEM((2, T // HGRN_CHUNK, DK_A, DK_A), F32),
                        pltpu.VMEM((2, T // HGRN_CHUNK, DK_A, DK_A), BF16),
                        pltpu.VMEM((2, T // HGRN_CHUNK, 8, LANES), F32),
                        pltpu.VMEM((2, HGRN_ROWS, HGRN_ROWS), BF16),
                        pltpu.VMEM((2, HGRN_ROWS, HGRN_ROWS), F32),
                        pltpu.VMEM((HGRN_ROWS, HGRN_ROWS // HGRN_CHUNK * LANES), BF16)],
        compiler_params=_cparams(2),
        name="hgrn2",
    )(*args)
    return res if emit_state else (res[0], None)


def _head_masks():
    lane = lax.broadcasted_iota(jnp.int32, (1, LANES), 1)
    return lane < DH_B, lane >= DH_B


def _ctx_attn_kernel(q_ref, k_ref, v_ref, g_ref, o_ref):
    scale = DH_B ** -0.5
    masks = _head_masks()
    T = q_ref.shape[1]
    for p in range(H_B // 2):
        cols = slice(p * LANES, (p + 1) * LANES)
        q = q_ref[0, :, cols] * scale
        qs = jnp.concatenate([jnp.where(masks[h], q, jnp.zeros_like(q)) for h in range(2)], axis=0)
        s = lax.dot_general(qs, k_ref[0, :, cols], NT_DIMS, preferred_element_type=F32)
        e = jnp.exp(s - jnp.max(s, axis=-1, keepdims=True))
        pr = e / jnp.sum(e, axis=-1, keepdims=True)
        o = jnp.dot(pr.astype(BF16), v_ref[0, :, cols], preferred_element_type=F32)
        o = jnp.where(masks[0], o[0:T], o[T:2 * T])
        o_ref[0, :, cols] = (o * _silu(g_ref[0, :, cols].astype(F32))).astype(o_ref.dtype)


def _ctx_attn(yb):
    B, T, _ = yb.shape
    width = H_B * DH_B

    def col(k):
        return pl.BlockSpec((1, T, width), lambda b, k=k: (b, 0, k))

    return pl.pallas_call(
        _ctx_attn_kernel,
        out_shape=jax.ShapeDtypeStruct((B, T, width), BF16),
        grid=(B,),
        in_specs=[col(0), col(1), col(2), col(3)],
        out_specs=pl.BlockSpec((1, T, width), lambda b: (b, 0, 0)),
        compiler_params=_cparams(1),
        name="ctx_attn",
    )(yb, yb, yb, yb)


N_DR = 2 * NA_KH - 1
N_DC = 2 * NA_KW - 1
N_TAB = N_DR - 1


def _nat_kernel(rb_ref, q_ref, k_ref, v_ref, g_ref, kc_ref, vc_ref, o_ref,
                tab_ref, qs_ref, s_ref, p_ref, r_ref, oc_ref, *, rows):
    p = pl.program_id(0)
    scale = DH_B ** -0.5
    kh = min(NA_KH, rows)
    masks = _head_masks()

    @pl.when(pl.program_id(1) == 0)
    def _build_tables():
        c = lax.broadcasted_iota(jnp.int32, (GRID_W, LANES), 0)
        lane = lax.broadcasted_iota(jnp.int32, (GRID_W, LANES), 1)
        kcol = lane & (GRID_W - 1)
        upper = lane >= GRID_W
        ws = jnp.clip(c - NA_KW // 2, 0, GRID_W - NA_KW)
        neg = jnp.full((GRID_W, LANES), NEG_INF, F32)
        diag = kcol - c + (NA_KW - 1)
        for h in range(2):
            base = (2 * p + h) * (N_DR * N_DC)

            def per_dr(i, _, base=base, h=h):
                def per_dc(dd, acc):
                    lo = rb_ref[base + i * N_DC + dd]
                    hi = rb_ref[base + (i + 1) * N_DC + dd]
                    return jnp.where(diag == dd, jnp.where(upper, hi, lo), acc)

                acc = lax.fori_loop(0, N_DC, per_dc, neg)
                acc = jnp.where(kcol >= ws, jnp.where(kcol < ws + NA_KW, acc, neg), neg)
                tab_ref[h, i] = acc
                return 0

            lax.fori_loop(0, N_TAB, per_dr, 0)

    kc, vc = kc_ref[0], vc_ref[0]
    n_keys = kh * GRID_W
    n_ctx = kc.shape[0]
    G = NA_GROUP
    W2 = 2 * GRID_W

    def group(gi, _):
        r_first = gi * G
        q0 = pl.multiple_of(r_first * GRID_W, G * GRID_W)
        for i in range(G):
            qi = q_ref[0, pl.ds(q0 + i * GRID_W, GRID_W), :] * scale
            for h in range(2):
                qs_ref[i * W2 + h * GRID_W:i * W2 + (h + 1) * GRID_W, :] = jnp.where(
                    masks[h], qi, jnp.zeros_like(qi))
        s_ref[:, n_keys:n_keys + n_ctx] = lax.dot_general(qs_ref[...], kc, NT_DIMS,
                                                          preferred_element_type=F32)
        windows = []
        for i in range(G):
            r = r_first + i
            rs = jnp.clip(r - kh // 2, 0, rows - kh)
            k0 = pl.multiple_of(rs * GRID_W, GRID_W)
            windows.append(k0)
            dr0 = rs - r + (NA_KH - 1)
            bias = jnp.concatenate(
                [jnp.concatenate([tab_ref[h, dr0 + 2 * m] for m in range(kh // 2)], axis=1)
                 for h in range(2)], axis=0)
            s_ref[i * W2:(i + 1) * W2, 0:n_keys] = lax.dot_general(
                qs_ref[i * W2:(i + 1) * W2, :], k_ref[0, pl.ds(k0, n_keys), :], NT_DIMS,
                preferred_element_type=F32) + bias
        for i in range(G):
            s = s_ref[i * W2:(i + 1) * W2, :]
            e = jnp.exp(s - jnp.max(s, axis=-1, keepdims=True))
            p_ref[i * W2:(i + 1) * W2, :] = e.astype(BF16)
            rinv = 1.0 / jnp.sum(e, axis=-1, keepdims=True)
            r_ref[i * W2:(i + 1) * W2, :] = jnp.broadcast_to(rinv, (W2, LANES))
        oc_ref[...] = jnp.dot(p_ref[:, n_keys:n_keys + n_ctx], vc, preferred_element_type=F32)
        for i in range(G):
            o = jnp.dot(p_ref[i * W2:(i + 1) * W2, 0:n_keys], v_ref[0, pl.ds(windows[i], n_keys), :],
                        preferred_element_type=F32)
            o = (o + oc_ref[i * W2:(i + 1) * W2, :]) * r_ref[i * W2:(i + 1) * W2, :]
            o = jnp.where(masks[0], o[0:GRID_W], o[GRID_W:W2])
            out_rows = pl.ds(q0 + i * GRID_W, GRID_W)
            gate = g_ref[0, out_rows, :].astype(F32)
            o_ref[0, out_rows, :] = (o * _silu(gate)).astype(o_ref.dtype)
        return 0

    lax.fori_loop(0, rows // G, group, 0)


def _nat(yb, kc, vc, rel_bias):
    B, T, _ = yb.shape
    Tc = kc.shape[1]
    n_pair = H_B // 2
    rows = T // GRID_W
    n_stack = NA_GROUP * 2 * GRID_W
    n_keys = min(NA_KH, rows) * GRID_W

    def col(k):
        return pl.BlockSpec((1, T, LANES), lambda p, b, k=k: (b, 0, k * n_pair + p))

    ctx =pl.BlockSpec((1, Tc, LANES), lambda p, b: (b, 0, p))
    return pl.pallas_call(
        functools.partial(_nat_kernel, rows=rows),
        out_shape=jax.ShapeDtypeStruct((B, T, H_B * DH_B), BF16),
        grid=(n_pair, B),
        in_specs=[pl.BlockSpec(memory_space=pltpu.SMEM), col(0), col(1), col(2), col(3), ctx, ctx],
        out_specs=pl.BlockSpec((1, T, LANES), lambda p, b: (b, 0, p)),
        scratch_shapes=[pltpu.VMEM((2, N_TAB, GRID_W, LANES), F32),
                        pltpu.VMEM((n_stack, LANES), BF16),
                        pltpu.VMEM((n_stack, n_keys + Tc), F32),
                        pltpu.VMEM((n_stack, n_keys + Tc), BF16),
                        pltpu.VMEM((n_stack, LANES), F32),
                        pltpu.VMEM((n_stack, LANES), F32)],
        compiler_params=_cparams(2),
        name="nbr_attn",
    )(rel_bias.reshape(-1), yb, yb, yb, yb, kc, vc)


def _seg_len(seq):
    length = -(-seq // N_SEG)
    while length % 8 != 4:
        length += 1
    return length


def _rglru_kernel(x_ref, g_ref, cw_ref, cb_ref, wg_ref, bg_ref, lam_ref, *rest, seq, has_s0, emit_state):
    rest = list(rest)
    s0_ref = rest.pop(0) if has_s0 else None
    o_ref = rest.pop(0)
    hfin_ref = rest.pop(0) if emit_state else None
    xpad_ref, a_ref, u_ref, h_ref, p_ref = rest
    L = _seg_len(seq)
    n_pad = N_SEG * L - seq
    RB = RG_ROWS
    CB = RG_SLABS
    n_tile = N_SEG // 8
    chains = [(d, j, s) for d in range(2) for j in range(CB) for s in range(n_tile)]

    xpad_ref[0:8, :] = jnp.zeros((8, CB * LANES), F32)
    xpad_ref[seq + 8:seq + 16, :] = jnp.zeros((8, CB * LANES), F32)
    xpad_ref[8:seq + 8, :] = x_ref[0]
    for d in range(2):
        for j in range(CB):
            a_ref[d, j, seq:seq + n_pad, :] = jnp.ones((n_pad, LANES), F32)
            u_ref[d, j, seq:seq + n_pad, :] = jnp.zeros((n_pad, LANES), F32)

    nl = -lam_ref[...]
    sp = jnp.maximum(nl, 0.0) + jnp.log1p(jnp.exp(-jnp.abs(nl)))
    cw = cw_ref[...]
    cbias = cb_ref[...]

    def gates(blk, _):
        r0 = pl.multiple_of(blk * RB, RB)
        xm = xpad_ref[pl.ds(r0, RB + 16), :]
        xc = cw[0:1] * xm[6:6 + RB] + cw[1:2] * xm[7:7 + RB]
        xc = xc + cw[2:3] * xm[8:8 + RB]
        xc = xc + cw[3:4] * xm[9:9 + RB] + cbias
        for j in range(CB):
            xj = xc[:, j * LANES:(j + 1) * LANES]
            gt = jnp.dot(xj.astype(BF16), wg_ref[j], preferred_element_type=F32) + bg_ref[j]
            for d in range(2):
                rg = jax.nn.sigmoid(gt[:, (2 * d) * LANES:(2 * d + 1) * LANES])
                ig = jax.nn.sigmoid(gt[:, (2 * d + 1) * LANES:(2 * d + 2) * LANES])
                la = (-RG_C * rg) * sp[d:d + 1, j * LANES:(j + 1) * LANES]
                a = jnp.exp(la)
                y = -jnp.tanh(la) * (1.0 + a * a)
                root = jnp.where(y > 0.0, y * lax.rsqrt(y), 0.0)
                a_ref[d, j, pl.ds(r0, RB), :] = a
                u_ref[d, j, pl.ds(r0, RB), :] = root * (ig * xj)
        return 0

    lax.fori_loop(0, seq // RB, gates, 0)

    def seg_rows(d, s, step):
        t = step if d == 0 else L - 1 - step
        return pl.ds(t + s * 8 * L, 8, stride=L)

    unroll = 4

    def scan(i, carry):
        carry = list(carry)
        for k in range(unroll):
            for n, (d, j, s) in enumerate(chains):
                h, pr = carry[n]
                idx = seg_rows(d, s, i * unroll + k)
                a = a_ref[d, j, idx, :]
                h = a * h + u_ref[d, j, idx, :]
                pr = pr * a
                h_ref[d, j, idx, :] = h
                p_ref[d, j, idx, :] = pr
                carry[n] = (h, pr)
        return tuple(carry)

    zero = jnp.zeros((8, LANES), F32)
    one = jnp.ones((8, LANES), F32)
    ends = lax.fori_loop(0, L // unroll, scan, ((zero, one),) * len(chains))

    cins = {}
    finals = [[None] * CB for _ in range(2)]
    for d in range(2):
        for j in range(CB):
            if has_s0:
                c = s0_ref[0, d:d + 1, j * LANES:(j + 1) * LANES]
            else:
                c = jnp.zeros((1, LANES), F32)
            cin = [None] * N_SEG
            for kk in range(N_SEG):
                seg = kk if d == 0 else N_SEG - 1 - kk
                s, row = divmod(seg, 8)
                h_end, p_end = ends[chains.index((d, j, s))]
                cin[seg] = c
                c = h_end[row:row + 1, :] + p_end[row:row + 1, :] * c
            finals[d][j] = c
            for s in range(n_tile):
                cins[(d, j, s)] = jnp.concatenate(cin[s * 8:(s + 1) * 8], axis=0)

    def fix(i, _):
        for k in range(unroll):
            for (d, j, s) in chains:
                idx = seg_rows(0, s, i * unroll + k)
                h_ref[d, j, idx, :] = h_ref[d, j, idx, :] + p_ref[d, j, idx, :] * cins[(d, j, s)]
        return 0

    lax.fori_loop(0, L // unroll, fix, 0)

    if emit_state:
        hfin_ref[0] = jnp.concatenate([jnp.concatenate(finals[d], axis=1) for d in range(2)], axis=0)

    def combine(blk, _):
        rows = pl.ds(pl.multiple_of(blk * RB, RB), RB)
        hs = jnp.concatenate([h_ref[0, j, rows, :] + h_ref[1, j, rows, :] for j in range(CB)], axis=1)
        o_ref[0, rows, :] = (hs * _silu(g_ref[0, rows, :])).astype(o_ref.dtype)
        return 0

    lax.fori_loop(0, seq // RB, combine, 0)


def _rglru(xg, conv_w, conv_b, wg, bg, lam, s0, emit_state):
    B, T, _ = xg.shape
    has_s0 = s0 is not None
    n_rows = N_SEG * _seg_len(T)
    CB = RG_SLABS
    wide = CB * LANES
    n_steps = H_C // CB
    in_specs = [
        pl.BlockSpec((1, T, wide), lambda b, c: (b, 0, c)),
        pl.BlockSpec((1, T, wide), lambda b, c: (b, 0, n_steps + c)),
        pl.BlockSpec((4, wide), lambda b, c: (0, c)),
        pl.BlockSpec((1, wide), lambda b, c: (0, c)),
        pl.BlockSpec((CB, BW_C, 4 * BW_C), lambda b, c: (c, 0, 0)),
        pl.BlockSpec((CB, 1, 4 * BW_C), lambda b, c: (c, 0, 0)),
        pl.BlockSpec((2, wide), lambda b, c: (0, c)),
    ]
    args = [xg, xg, conv_w, conv_b, wg, bg, lam]
    if has_s0:
        in_specs.append(pl.BlockSpec((1, 2, wide), lambda b, c: (b, 0, c)))
        args.append(s0)
    out_shape = [jax.ShapeDtypeStruct((B, T, W_C), BF16)]
    out_specs = [pl.BlockSpec((1, T, wide), lambda b, c: (b, 0, c))]
    if emit_state:
        out_shape.append(jax.ShapeDtypeStruct((B, 2, W_C), F32))
        out_specs.append(pl.BlockSpec((1, 2, wide), lambda b, c: (b, 0, c)))
    res = pl.pallas_call(
        functools.partial(_rglru_kernel, seq=T, has_s0=has_s0, emit_state=emit_state),
        out_shape=out_shape,
        grid=(B, n_steps),
        in_specs=in_specs,
        out_specs=out_specs,
        scratch_shapes=[pltpu.VMEM((T + 16, wide), F32)] + [pltpu.VMEM((2, CB, n_rows, LANES), F32)] * 4,
        compiler_params=_cparams(2),
        name="rglru",
    )(*args)
    return res if emit_state else (res[0], None)


A_COLS = 5 * H_A * DK_A
B_COLS = 4 * H_B * DH_B


def kernel(x_prompt, x_sample, state_hgrn, cache_na_k, cache_na_v, state_rglru, c, c_ctx, norm_gain, w_mod, b_mod, w_in_even, w_out_even, hgrn_lb_logits, hgrn_out_gain, na_rel_bias, w_in_odd, w_out_odd, conv_w, conv_b, rg_gate_w, rg_gate_b, rg_lambda, final_gain):
    n_ctx = x_prompt.shape[0]
    n_lat = x_sample.shape[0]
    depth = w_mod.shape[0]

    cond = jnp.zeros((16, D_MODEL), F32).at[0].set(c_ctx).at[1:1 + n_lat].set(c)
    mod = _modulation(cond, w_mod, b_mod.reshape(depth, 1, 3 * D_MODEL))
    mod = mod.reshape(depth, 16, 3, D_MODEL)

    xc, xs = x_prompt, x_sample
    new_hgrn, new_k, new_v, new_rg = [], [], [], []
    for l in range(depth):
        j = l // 2
        mod_c, mod_s = mod[l, 0:1], mod[l, 1:1 + n_lat]
        gain = norm_gain[l].reshape(1, D_MODEL)
        last = l == depth - 1
        fgain = final_gain.reshape(1, D_MODEL) if last else None
        if l % 2 == 0:
            w_in = w_in_even[j].astype(BF16)
            w_out = w_out_even[j].astype(BF16)
            outs_s = ((0, A_COLS, F32), (A_COLS, B_COLS, BF16))
            outs_c = outs_s + ((A_COLS + H_B * DH_B, 2 * H_B * DH_B, F32),)
            ya_c, yb_c, kv_c = _inproj(xc, mod_c, gain, w_in, outs_c, 256, True)
            ya_s, yb_s = _inproj(xs, mod_s, gain, w_in, outs_s, 512, False)
            hgain = hgrn_out_gain[j].reshape(H_A, 1, DK_A)
            oa_c, s_fin = _hgrn(ya_c, hgrn_lb_logits, j, hgain, None, True)
            oa_s, _ = _hgrn(ya_s, hgrn_lb_logits, j, hgain, state_hgrn[:, j], False)
            ob_c = _ctx_attn(yb_c)
            tc = cache_na_k.shape[3]
            kc = cache_na_k[:, j].transpose(0, 2, 1, 3).reshape(n_lat, tc, H_B * DH_B).astype(BF16)
            vc = cache_na_v[:, j].transpose(0, 2, 1, 3).reshape(n_lat, tc, H_B * DH_B).astype(BF16)
            ob_s = _nat(yb_s, kc, vc, na_rel_bias[j])
            xc = _outproj((oa_c, ob_c), xc, mod_c, w_out, 256, True, fgain)
            xs = _outproj((oa_s, ob_s), xs, mod_s, w_out, 512, False, fgain)
            t_c = kv_c.shape[1]
            heads = kv_c.reshape(n_ctx, t_c, 2, H_B, DH_B).transpose(2, 0, 3, 1, 4)
            new_hgrn.append(s_fin)
            new_k.append(heads[0])
            new_v.append(heads[1])
        else:
            w_in = w_in_odd[j].astype(BF16)
            w_out = w_out_odd[j].astype(BF16)
            outs = ((0, 2 * W_C, F32),)
            (xg_c,) = _inproj(xc, mod_c, gain, w_in, outs, 256, True)
            (xg_s,) = _inproj(xs, mod_s, gain, w_in, outs, 512, False)
            wg = rg_gate_w[j].transpose(2, 3, 0, 1, 4).reshape(H_C, BW_C, 4 * BW_C).astype(BF16)
            bg = rg_gate_b[j].reshape(2, 2, H_C, BW_C).transpose(2, 0, 1, 3).reshape(H_C, 1, 4 * BW_C)
            cb = conv_b[j].reshape(1, W_C)
            y_c, h_fin = _rglru(xg_c, conv_w[j], cb, wg, bg, rg_lambda[j], None, True)
            y_s, _ = _rglru(xg_s, conv_w[j], cb, wg, bg, rg_lambda[j], state_rglru[:, j], False)
            xc = _outproj((y_c,), xc, mod_c, w_out, 256, True, fgain)
            xs = _outproj((y_s,), xs, mod_s, w_out, 512, False, fgain)
            new_rg.append(h_fin)
    return (xc, xs, jnp.stack(new_hgrn, axis=1), jnp.stack(new_k, axis=1),
            jnp.stack(new_v, axis=1), jnp.stack(new_rg, axis=1))
```

```python
import functools

import jax
import jax.numpy as jnp
from jax import lax
from jax.experimental import pallas as pl
from jax.experimental.pallas import tpu as pltpu

F32 = jnp.float32
BF16 = jnp.bfloat16

D_MODEL = 1024
EPS = 1e-6
NEG_INF = -1e30
H_A = 4
DK_A = 128
HGRN_CHUNK = 32
HGRN_ROWS = 256
H_B = 8
DH_B = 64
GRID_W = 64
NA_KH = 8
NA_KW = 16
NA_GROUP = 8
W_C = 1024
H_C = 8
BW_C = W_C // H_C
RG_C = 8.0
RG_ROWS = 256
RG_SLABS = 2
N_SEG = 16
LANES = 128
VMEM_LIMIT = 48 * 1024 * 1024

NT_DIMS = (((1,), (1,)), ((), ()))
TN_DIMS = (((0,), (0,)), ((), ()))


def _silu(x):
    return x * jax.nn.sigmoid(x)


def _cparams(n_axes):
    return pltpu.CompilerParams(dimension_semantics=("arbitrary",) * n_axes,
                                vmem_limit_bytes=VMEM_LIMIT)


def _mod_kernel(cond_ref, w_ref, b_ref, o_ref):
    s = _silu(cond_ref[...])
    o_ref[0] = jnp.dot(s.astype(BF16), w_ref[0].astype(BF16), preferred_element_type=F32) + b_ref[0]


def _modulation(cond, w_mod, b_mod):
    depth = w_mod.shape[0]
    n_rows = cond.shape[0]
    return pl.pallas_call(
        _mod_kernel,
        out_shape=jax.ShapeDtypeStruct((depth, n_rows, 3 * D_MODEL), F32),
        grid=(depth, 3),
        in_specs=[
            pl.BlockSpec((n_rows, D_MODEL), lambda l, n: (0, 0)),
            pl.BlockSpec((1, D_MODEL, D_MODEL), lambda l, n: (l, 0, n)),
            pl.BlockSpec((1, 1, D_MODEL), lambda l, n: (l, 0, n)),
        ],
        out_specs=pl.BlockSpec((1, n_rows, D_MODEL), lambda l, n: (l, 0, n)),
        compiler_params=_cparams(2),
        name="adaln_mod",
    )(cond, w_mod, b_mod)


def _inproj_kernel(x_ref, mod_ref, gain_ref, w_ref, *out_refs, outs):
    x = x_ref[0]
    var = jnp.mean(x * x, axis=-1, keepdims=True)
    y = x * lax.rsqrt(var + EPS) * gain_ref[...]
    h = y * (1.0 + mod_ref[0, 1:2, :]) + mod_ref[0, 0:1, :]
    hb = h.astype(BF16)
    step = 512
    for o_ref, (col0, width, _) in zip(out_refs, outs):
        for c in range(0, width, step):
            r = jnp.dot(hb, w_ref[:, col0 + c:col0 + c + step], preferred_element_type=F32)
            o_ref[0, :, c:c + step] = r.astype(o_ref.dtype)


def _inproj(x, mod, gain, w, outs, tm, shared_mod):
    B, T, _ = x.shape
    n_cols = w.shape[1]
    mod_map = (lambda b, t: (0, 0, 0)) if shared_mod else (lambda b, t: (b, 0, 0))
    return pl.pallas_call(
        functools.partial(_inproj_kernel, outs=outs),
        out_shape=[jax.ShapeDtypeStruct((B, T, wd), dt) for _, wd, dt in outs],
        grid=(B, T // tm),
        in_specs=[
            pl.BlockSpec((1, tm, D_MODEL), lambda b, t: (b, t, 0)),
            pl.BlockSpec((1, 3, D_MODEL), mod_map),
            pl.BlockSpec((1, D_MODEL), lambda b, t: (0, 0)),
            pl.BlockSpec((D_MODEL, n_cols), lambda b, t: (0, 0)),
        ],
        out_specs=[pl.BlockSpec((1, tm, wd), lambda b, t: (b, t, 0)) for _, wd, _ in outs],
        compiler_params=_cparams(2),
        name="in_proj",
    )(x, mod, gain, w)


def _outproj_kernel(*refs, n_y, final):
    y_refs, (x_ref, mod_ref, w_ref), rest = refs[:n_y], refs[n_y:n_y + 3], refs[n_y + 3:]
    m = None
    row = 0
    for y_ref in y_refs:
        width = y_ref.shape[-1]
        part = jnp.dot(y_ref[0], w_ref[row:row + width, :], preferred_element_type=F32)
        m = part if m is None else m + part
        row += width
    xn = x_ref[0] + mod_ref[0, 2:3, :] * m
    if final:
        gain_ref, o_ref = rest
        var = jnp.mean(xn * xn, axis=-1, keepdims=True)
        xn = xn * lax.rsqrt(var + EPS) * gain_ref[...]
    else:
        (o_ref,) = rest
    o_ref[0] = xn


def _outproj(ys, x, mod, w, tm, shared_mod, final_gain=None):
    B, T, _ = x.shape
    final = final_gain is not None
    mod_map = (lambda b, t: (0, 0, 0)) if shared_mod else (lambda b, t: (b, 0, 0))
    in_specs = [pl.BlockSpec((1, tm, y.shape[-1]), lambda b, t: (b, t, 0)) for y in ys] + [
        pl.BlockSpec((1, tm, D_MODEL), lambda b, t: (b, t, 0)),
        pl.BlockSpec((1, 3, D_MODEL), mod_map),
        pl.BlockSpec((w.shape[0], D_MODEL), lambda b, t: (0, 0)),
    ]
    args = list(ys) + [x, mod, w]
    if final:
        in_specs.append(pl.BlockSpec((1, D_MODEL), lambda b, t: (0, 0)))
        args.append(final_gain)
    return pl.pallas_call(
        functools.partial(_outproj_kernel, n_y=len(ys), final=final),
        out_shape=jax.ShapeDtypeStruct((B, T, D_MODEL), F32),
        grid=(B, T // tm),
        in_specs=in_specs,
        out_specs=pl.BlockSpec((1, tm, D_MODEL), lambda b, t: (b, t, 0)),
        compiler_params=_cparams(2),
        name="out_proj",
    )(*args)


def _hgrn_kernel(q_ref, zf_ref, zb_ref, v_ref, g_ref, lgt_ref, gain_ref, *rest, seq, layer, has_s0, emit_state):
    rest = list(rest)
    s0_ref = rest.pop(0) if has_s0 else None
    o_ref = rest.pop(0)
    sfin_ref = rest.pop(0) if emit_state else None
    acc_ref, qd_ref, ki_ref, kd_ref, kv_ref, st_ref, dec_ref, mst_ref, msk_ref, mexp_ref = rest
    R = HGRN_ROWS
    C = HGRN_CHUNK
    n_blk = seq // R
    n_chunk = R // C
    n_all = seq // C

    @pl.when((pl.program_id(0) == 0) & (pl.program_id(1) == 0))
    def _build_masks():
        ti = lax.broadcasted_iota(jnp.int32, (R, R), 0)
        tj = lax.broadcasted_iota(jnp.int32, (R, R), 1)
        shift = C.bit_length() - 1
        same = lax.shift_right_logical(ti, shift) == lax.shift_right_logical(tj, shift)
        one = jnp.ones((R, R), F32)
        zero = jnp.zeros((R, R), F32)
        incl = (jnp.where(same, jnp.where(tj <= ti, one, zero), zero),
                jnp.where(same, jnp.where(tj >= ti, one, zero), zero))
        for d in range(2):
            msk_ref[d] = incl[d]
            mst_ref[d] = incl[d].astype(BF16)
        rr = lax.broadcasted_iota(jnp.int32, (R, n_chunk * LANES), 0)
        cc = lax.broadcasted_iota(jnp.int32, (R, n_chunk * LANES), 1)
        own = lax.shift_right_logical(rr, shift) == lax.shift_right_logical(cc, LANES.bit_length() - 1)
        mexp_ref[...] = jnp.where(own, 1.0, 0.0).astype(BF16)

    lgt = [lgt_ref[:, i, :] for i in range(lgt_ref.shape[1])]
    lmax = functools.reduce(jnp.maximum, lgt)
    ex = [jnp.exp(t - lmax) for t in lgt]
    lb_all = sum(ex[:layer + 1]) / sum(ex)
    gain = gain_ref[0]

    blocks_per_trip = 2 if n_blk % 2 == 0 else 1

    def gates(i, _):
        for u in range(blocks_per_trip):
            blk = i * blocks_per_trip + u
            rows = pl.ds(pl.multiple_of(blk * R, R), R)
            q = q_ref[0, rows, :]
            for d in range(2):
                z = (zf_ref if d == 0 else zb_ref)[0, rows, :]
                lb = lb_all[d:d + 1, :]
                oml = 1.0 - lb
                e = jnp.exp(-jnp.abs(z))
                r = 1.0 / (1.0 + e)
                er = e * r
                pos = z >= 0.0
                f = lb + oml * jnp.where(pos, r, er)
                k = oml * jnp.where(pos, er, r)
                logf = jnp.log(f)
                hi = logf.astype(BF16)
                lo = (logf - hi.astype(F32)).astype(BF16)
                cs = jnp.dot(mst_ref[d], jnp.concatenate([hi, lo], axis=1), preferred_element_type=F32)
                b = cs[:, 0:LANES] + cs[:, LANES:2 * LANES]
                ends = [c * C + (C - 1 if d == 0 else 0) for c in range(n_chunk)]
                btot = jnp.concatenate([jnp.broadcast_to(b[t:t + 1, :], (C, LANES)) for t in ends], axis=0)
                qd_ref[d, rows, :] = (q * jnp.exp(b)).astype(BF16)
                ki_ref[d, rows, :] = (k * jnp.exp(-b)).astype(BF16)
                kd_ref[d, rows, :] = (k * jnp.exp(btot - b)).astype(BF16)
                for c in range(n_chunk):
                    dec_ref[d, blk * n_chunk + c] = jnp.exp(btot[c * C:c * C + 8, :])
        return 0

    lax.fori_loop(0, n_blk // blocks_per_trip, gates, 0)

    def intra(i, _):
        for u in range(blocks_per_trip):
            blk = i * blocks_per_trip + u
            rows = pl.ds(pl.multiple_of(blk * R, R), R)
            v = v_ref[0, rows, :]
            vb = v.astype(BF16)
            vt = v.T.astype(BF16)
            for d in range(2):
                att = lax.dot_general(qd_ref[d, rows, :], ki_ref[d, rows, :], NT_DIMS,
                                      preferred_element_type=F32)
                att = jnp.where(msk_ref[d] > 0.5, att, 0.0)
                acc_ref[d, rows, :] = jnp.dot(att.astype(BF16), vb, preferred_element_type=F32)
                kd_exp = jnp.concatenate([kd_ref[d, rows, :]] * n_chunk, axis=1) * mexp_ref[...]
                kv_all = jnp.dot(vt, kd_exp, preferred_element_type=F32)
                for c in range(n_chunk):
                    kv_ref[d, blk * n_chunk + c] = kv_all[:, c * LANES:(c + 1) * LANES]
        return 0

    lax.fori_loop(0, n_blk // blocks_per_trip, intra, 0)

    unroll = 4

    def states(i, sts):
        sts = list(sts)
        for u in range(unroll):
            n = i * unroll + u
            for d in range(2):
                c = n if d == 0 else n_all - 1 - n
                st_ref[d, c] = sts[d].astype(BF16)
                dec = jnp.concatenate([dec_ref[d, c]] * (DK_A // 8), axis=0)
                sts[d] = sts[d] * dec + kv_ref[d, c]
        return tuple(sts)

    if has_s0:
        st0 = (s0_ref[0, 0, 0].T, s0_ref[0, 1, 0].T)
    else:
        st0 = (jnp.zeros((DK_A, DK_A), F32),) * 2
    sts = lax.fori_loop(0, n_all // unroll, states, st0)
    if emit_state:
        for d in range(2):
            sfin_ref[0, d, 0] = sts[d].T

    def finish(blk, _):
        rows = pl.ds(pl.multiple_of(blk * R, R), R)
        tot = acc_ref[0, rows, :] + acc_ref[1, rows, :]
        for d in range(2):
            pieces = []
            for c in range(n_chunk):
                crow = pl.ds(pl.multiple_of(blk * R + c * C, C), C)
                pieces.append(lax.dot_general(qd_ref[d, crow, :], st_ref[d, blk * n_chunk + c], NT_DIMS,
                                              preferred_element_type=F32))
            tot = tot + jnp.concatenate(pieces, axis=0)
        var = jnp.mean(tot * tot, axis=-1, keepdims=True)
        y = tot * lax.rsqrt(var + EPS) * gain
        o_ref[0, rows, :] = (y * _silu(g_ref[0, rows, :])).astype(o_ref.dtype)
        return 0

    lax.fori_loop(0, n_blk, finish, 0)


def _hgrn(ya, lgt, layer, gain, s0, emit_state):
    B, T, _ = ya.shape
    has_s0 = s0 is not None

    def col(k):
        return pl.BlockSpec((1, T, LANES), lambda b, h, k=k: (b, 0, k * H_A + h))

    in_specs = [col(0), col(1), col(2), col(3), col(4),
                pl.BlockSpec((2, lgt.shape[1], LANES), lambda b, h: (0, 0, h)),
                pl.BlockSpec((1, 1, LANES), lambda b, h: (h, 0, 0))]
    args = [ya, ya, ya, ya, ya, lgt, gain]
    if has_s0:
        in_specs.append(pl.BlockSpec((1, 2, 1, DK_A, DK_A), lambda b, h: (b, 0, h, 0, 0)))
        args.append(s0)
    out_shape = [jax.ShapeDtypeStruct((B, T, H_A * DK_A), BF16)]
    out_specs = [pl.BlockSpec((1, T, LANES), lambda b, h: (b, 0, h))]
    if emit_state:
        out_shape.append(jax.ShapeDtypeStruct((B, 2, H_A, DK_A, DK_A), F32))
        out_specs.append(pl.BlockSpec((1, 2, 1, DK_A, DK_A), lambda b, h: (b, 0, h, 0, 0)))
    res = pl.pallas_call(
        functools.partial(_hgrn_kernel, seq=T, layer=layer, has_s0=has_s0, emit_state=emit_state),
        out_shape=out_shape,
        grid=(B, H_A),
        in_specs=in_specs,
        out_specs=out_specs,
        scratch_shapes=[pltpu.VMEM((2, T, LANES), F32),
                        pltpu.VMEM((2, T, LANES), BF16),
                        pltpu.VMEM((2, T, LANES), BF16),
                        pltpu.VMEM((2, T, LANES), BF16),
                        pltpu.VMEM((2, T // HGRN_CHUNK, DK_A, DK_A), F32),
                        pltpu.VMEM((2, T // HGRN_CHUNK, DK_A, DK_A), BF16),
                        pltpu.VMEM((2, T // HGRN_CHUNK, 8, LANES), F32),
                        pltpu.VMEM((2, HGRN_ROWS, HGRN_ROWS), BF16),
                        pltpu.VMEM((2, HGRN_ROWS, HGRN_ROWS), F32),
                        pltpu.VMEM((HGRN_ROWS, HGRN_ROWS // HGRN_CHUNK * LANES), BF16)],
        compiler_params=_cparams(2),
        name="hgrn2",
    )(*args)
    return res if emit_state else (res[0], None)


def _head_masks():
    lane = lax.broadcasted_iota(jnp.int32, (1, LANES), 1)
    return lane < DH_B, lane >= DH_B


def _ctx_attn_kernel(q_ref, k_ref, v_ref, g_ref, o_ref):
    scale = DH_B ** -0.5
    masks = _head_masks()
    T = q_ref.shape[1]
    for p in range(H_B // 2):
        cols = slice(p * LANES, (p + 1) * LANES)
        q = q_ref[0, :, cols] * scale
        qs = jnp.concatenate([jnp.where(masks[h], q, jnp.zeros_like(q)) for h in range(2)], axis=0)
        s = lax.dot_general(qs, k_ref[0, :, cols], NT_DIMS, preferred_element_type=F32)
        e = jnp.exp(s - jnp.max(s, axis=-1, keepdims=True))
        pr = e / jnp.sum(e, axis=-1, keepdims=True)
        o = jnp.dot(pr.astype(BF16), v_ref[0, :, cols], preferred_element_type=F32)
        o = jnp.where(masks[0], o[0:T], o[T:2 * T])
        o_ref[0, :, cols] = (o * _silu(g_ref[0, :, cols].astype(F32))).astype(o_ref.dtype)


def _ctx_attn(yb):
    B, T, _ = yb.shape
    width = H_B * DH_B

    def col(k):
        return pl.BlockSpec((1, T, width), lambda b, k=k: (b, 0, k))

    return pl.pallas_call(
        _ctx_attn_kernel,
        out_shape=jax.ShapeDtypeStruct((B, T, width), BF16),
        grid=(B,),
        in_specs=[col(0), col(1), col(2), col(3)],
        out_specs=pl.BlockSpec((1, T, width), lambda b: (b, 0, 0)),
        compiler_params=_cparams(1),
        name="ctx_attn",
    )(yb, yb, yb, yb)


N_DR = 2 * NA_KH - 1
N_DC = 2 * NA_KW - 1
N_TAB = N_DR - 1


def _nat_kernel(rb_ref, q_ref, k_ref, v_ref, g_ref, kc_ref, vc_ref, o_ref,
                tab_ref, qs_ref, s_ref, p_ref, r_ref, oc_ref, *, rows):
    p = pl.program_id(0)
    scale = DH_B ** -0.5
    kh = min(NA_KH, rows)
    masks = _head_masks()

    @pl.when(pl.program_id(1) == 0)
    def _build_tables():
        c = lax.broadcasted_iota(jnp.int32, (GRID_W, LANES), 0)
        lane = lax.broadcasted_iota(jnp.int32, (GRID_W, LANES), 1)
        kcol = lane & (GRID_W - 1)
        upper = lane >= GRID_W
        ws = jnp.clip(c - NA_KW // 2, 0, GRID_W - NA_KW)
        neg = jnp.full((GRID_W, LANES), NEG_INF, F32)
        diag = kcol - c + (NA_KW - 1)
        for h in range(2):
            base = (2 * p + h) * (N_DR * N_DC)

            def per_dr(i, _, base=base, h=h):
                def per_dc(dd, acc):
                    lo = rb_ref[base + i * N_DC + dd]
                    hi = rb_ref[base + (i + 1) * N_DC + dd]
                    return jnp.where(diag == dd, jnp.where(upper, hi, lo), acc)

                acc = lax.fori_loop(0, N_DC, per_dc, neg)
                acc = jnp.where(kcol >= ws, jnp.where(kcol < ws + NA_KW, acc, neg), neg)
                tab_ref[h, i] = acc
                return 0

            lax.fori_loop(0, N_TAB, per_dr, 0)

    kc, vc = kc_ref[0], vc_ref[0]
    n_keys = kh * GRID_W
    n_ctx = kc.shape[0]
    G = NA_GROUP
    W2 = 2 * GRID_W

    def group(gi, _):
        r_first = gi * G
        q0 = pl.multiple_of(r_first * GRID_W, G * GRID_W)
        for i in range(G):
            qi = q_ref[0, pl.ds(q0 + i * GRID_W, GRID_W), :] * scale
            for h in range(2):
                qs_ref[i * W2 + h * GRID_W:i * W2 + (h + 1) * GRID_W, :] = jnp.where(
                    masks[h], qi, jnp.zeros_like(qi))
        s_ref[:, n_keys:n_keys + n_ctx] = lax.dot_general(qs_ref[...], kc, NT_DIMS,
                                                          preferred_element_type=F32)
        windows = []
        for i in range(G):
            r = r_first + i
            rs = jnp.clip(r - kh // 2, 0, rows - kh)
            k0 = pl.multiple_of(rs * GRID_W, GRID_W)
            windows.append(k0)
            dr0 = rs - r + (NA_KH - 1)
            bias = jnp.concatenate(
                [jnp.concatenate([tab_ref[h, dr0 + 2 * m] for m in range(kh // 2)], axis=1)
                 for h in range(2)], axis=0)
            s_ref[i * W2:(i + 1) * W2, 0:n_keys] = lax.dot_general(
                qs_ref[i * W2:(i + 1) * W2, :], k_ref[0, pl.ds(k0, n_keys), :], NT_DIMS,
                preferred_element_type=F32) + bias
        for i in range(G):
            s = s_ref[i * W2:(i + 1) * W2, :]
            e = jnp.exp(s - jnp.max(s, axis=-1, keepdims=True))
            p_ref[i * W2:(i + 1) * W2, :] = e.astype(BF16)
            rinv = 1.0 / jnp.sum(e, axis=-1, keepdims=True)
            r_ref[i * W2:(i + 1) * W2, :] = jnp.broadcast_to(rinv, (W2, LANES))
        oc_ref[...] = jnp.dot(p_ref[:, n_keys:n_keys + n_ctx], vc, preferred_element_type=F32)
        for i in range(G):
            o = jnp.dot(p_ref[i * W2:(i + 1) * W2, 0:n_keys], v_ref[0, pl.ds(windows[i], n_keys), :],
                        preferred_element_type=F32)
            o = (o + oc_ref[i * W2:(i + 1) * W2, :]) * r_ref[i * W2:(i + 1) * W2, :]
            o = jnp.where(masks[0], o[0:GRID_W], o[GRID_W:W2])
            out_rows = pl.ds(q0 + i * GRID_W, GRID_W)
            gate = g_ref[0, out_rows, :].astype(F32)
            o_ref[0, out_rows, :] = (o * _silu(gate)).astype(o_ref.dtype)
        return 0

    lax.fori_loop(0, rows // G, group, 0)


def _nat(yb, kc, vc, rel_bias):
    B, T, _ = yb.shape
    Tc = kc.shape[1]
    n_pair = H_B // 2
    rows = T // GRID_W
    n_stack = NA_GROUP * 2 * GRID_W
    n_keys = min(NA_KH, rows) * GRID_W

    def col(k):
        return pl.BlockSpec((1, T, LANES), lambda p, b, k=k: (b, 0, k * n_pair + p))

    ctx =pl.BlockSpec((1, Tc, LANES), lambda p, b: (b, 0, p))
    return pl.pallas_call(
        functools.partial(_nat_kernel, rows=rows),
        out_shape=jax.ShapeDtypeStruct((B, T, H_B * DH_B), BF16),
        grid=(n_pair, B),
        in_specs=[pl.BlockSpec(memory_space=pltpu.SMEM), col(0), col(1), col(2), col(3), ctx, ctx],
        out_specs=pl.BlockSpec((1, T, LANES), lambda p, b: (b, 0, p)),
        scratch_shapes=[pltpu.VMEM((2, N_TAB, GRID_W, LANES), F32),
                        pltpu.VMEM((n_stack, LANES), BF16),
                        pltpu.VMEM((n_stack, n_keys + Tc), F32),
                        pltpu.VMEM((n_stack, n_keys + Tc), BF16),
                        pltpu.VMEM((n_stack, LANES), F32),
                        pltpu.VMEM((n_stack, LANES), F32)],
        compiler_params=_cparams(2),
        name="nbr_attn",
    )(rel_bias.reshape(-1), yb, yb, yb, yb, kc, vc)


def _seg_len(seq):
    length = -(-seq // N_SEG)
    while length % 8 != 4:
        length += 1
    return length


def _rglru_kernel(x_ref, g_ref, cw_ref, cb_ref, wg_ref, bg_ref, lam_ref, *rest, seq, has_s0, emit_state):
    rest = list(rest)
    s0_ref = rest.pop(0) if has_s0 else None
    o_ref = rest.pop(0)
    hfin_ref = rest.pop(0) if emit_state else None
    xpad_ref, a_ref, u_ref, h_ref, p_ref = rest
    L = _seg_len(seq)
    n_pad = N_SEG * L - seq
    RB = RG_ROWS
    CB = RG_SLABS
    n_tile = N_SEG // 8
    chains = [(d, j, s) for d in range(2) for j in range(CB) for s in range(n_tile)]

    xpad_ref[0:8, :] = jnp.zeros((8, CB * LANES), F32)
    xpad_ref[seq + 8:seq + 16, :] = jnp.zeros((8, CB * LANES), F32)
    xpad_ref[8:seq + 8, :] = x_ref[0]
    for d in range(2):
        for j in range(CB):
            a_ref[d, j, seq:seq + n_pad, :] = jnp.ones((n_pad, LANES), F32)
            u_ref[d, j, seq:seq + n_pad, :] = jnp.zeros((n_pad, LANES), F32)

    nl = -lam_ref[...]
    sp = jnp.maximum(nl, 0.0) + jnp.log1p(jnp.exp(-jnp.abs(nl)))
    cw = cw_ref[...]
    cbias = cb_ref[...]

    def gates(blk, _):
        r0 = pl.multiple_of(blk * RB, RB)
        xm = xpad_ref[pl.ds(r0, RB + 16), :]
        xc = cw[0:1] * xm[6:6 + RB] + cw[1:2] * xm[7:7 + RB]
        xc = xc + cw[2:3] * xm[8:8 + RB]
        xc = xc + cw[3:4] * xm[9:9 + RB] + cbias
        for j in range(CB):
            xj = xc[:, j * LANES:(j + 1) * LANES]
            gt = jnp.dot(xj.astype(BF16), wg_ref[j], preferred_element_type=F32) + bg_ref[j]
            for d in range(2):
                rg = jax.nn.sigmoid(gt[:, (2 * d) * LANES:(2 * d + 1) * LANES])
                ig = jax.nn.sigmoid(gt[:, (2 * d + 1) * LANES:(2 * d + 2) * LANES])
                la = (-RG_C * rg) * sp[d:d + 1, j * LANES:(j + 1) * LANES]
                a = jnp.exp(la)
                y = -jnp.tanh(la) * (1.0 + a * a)
                root = jnp.where(y > 0.0, y * lax.rsqrt(y), 0.0)
                a_ref[d, j, pl.ds(r0, RB), :] = a
                u_ref[d, j, pl.ds(r0, RB), :] = root * (ig * xj)
        return 0

    lax.fori_loop(0, seq // RB, gates, 0)

    def seg_rows(d, s, step):
        t = step if d == 0 else L - 1 - step
        return pl.ds(t + s * 8 * L, 8, stride=L)

    unroll = 4

    def scan(i, carry):
        carry = list(carry)
        for k in range(unroll):
            for n, (d, j, s) in enumerate(chains):
                h, pr = carry[n]
                idx = seg_rows(d, s, i * unroll + k)
                a = a_ref[d, j, idx, :]
                h = a * h + u_ref[d, j, idx, :]
                pr = pr * a
                h_ref[d, j, idx, :] = h
                p_ref[d, j, idx, :] = pr
                carry[n] = (h, pr)
        return tuple(carry)

    zero = jnp.zeros((8, LANES), F32)
    one = jnp.ones((8, LANES), F32)
    ends = lax.fori_loop(0, L // unroll, scan, ((zero, one),) * len(chains))

    cins = {}
    finals = [[None] * CB for _ in range(2)]
    for d in range(2):
        for j in range(CB):
            if has_s0:
                c = s0_ref[0, d:d + 1, j * LANES:(j + 1) * LANES]
            else:
                c = jnp.zeros((1, LANES), F32)
            cin = [None] * N_SEG
            for kk in range(N_SEG):
                seg = kk if d == 0 else N_SEG - 1 - kk
                s, row = divmod(seg, 8)
                h_end, p_end = ends[chains.index((d, j, s))]
                cin[seg] = c
                c = h_end[row:row + 1, :] + p_end[row:row + 1, :] * c
            finals[d][j] = c
            for s in range(n_tile):
                cins[(d, j, s)] = jnp.concatenate(cin[s * 8:(s + 1) * 8], axis=0)

    def fix(i, _):
        for k in range(unroll):
            for (d, j, s) in chains:
                idx = seg_rows(0, s, i * unroll + k)
                h_ref[d, j, idx, :] = h_ref[d, j, idx, :] + p_ref[d, j, idx, :] * cins[(d, j, s)]
        return 0

    lax.fori_loop(0, L // unroll, fix, 0)

    if emit_state:
        hfin_ref[0] = jnp.concatenate([jnp.concatenate(finals[d], axis=1) for d in range(2)], axis=0)

    def combine(blk, _):
        rows = pl.ds(pl.multiple_of(blk * RB, RB), RB)
        hs = jnp.concatenate([h_ref[0, j, rows, :] + h_ref[1, j, rows, :] for j in range(CB)], axis=1)
        o_ref[0, rows, :] = (hs * _silu(g_ref[0, rows, :])).astype(o_ref.dtype)
        return 0

    lax.fori_loop(0, seq // RB, combine, 0)


def _rglru(xg, conv_w, conv_b, wg, bg, lam, s0, emit_state):
    B, T, _ = xg.shape
    has_s0 = s0 is not None
    n_rows = N_SEG * _seg_len(T)
    CB = RG_SLABS
    wide = CB * LANES
    n_steps = H_C // CB
    in_specs = [
        pl.BlockSpec((1, T, wide), lambda b, c: (b, 0, c)),
        pl.BlockSpec((1, T, wide), lambda b, c: (b, 0, n_steps + c)),
        pl.BlockSpec((4, wide), lambda b, c: (0, c)),
        pl.BlockSpec((1, wide), lambda b, c: (0, c)),
        pl.BlockSpec((CB, BW_C, 4 * BW_C), lambda b, c: (c, 0, 0)),
        pl.BlockSpec((CB, 1, 4 * BW_C), lambda b, c: (c, 0, 0)),
        pl.BlockSpec((2, wide), lambda b, c: (0, c)),
    ]
    args = [xg, xg, conv_w, conv_b, wg, bg, lam]
    if has_s0:
        in_specs.append(pl.BlockSpec((1, 2, wide), lambda b, c: (b, 0, c)))
        args.append(s0)
    out_shape = [jax.ShapeDtypeStruct((B, T, W_C), BF16)]
    out_specs = [pl.BlockSpec((1, T, wide), lambda b, c: (b, 0, c))]
    if emit_state:
        out_shape.append(jax.ShapeDtypeStruct((B, 2, W_C), F32))
        out_specs.append(pl.BlockSpec((1, 2, wide), lambda b, c: (b, 0, c)))
    res = pl.pallas_call(
        functools.partial(_rglru_kernel, seq=T, has_s0=has_s0, emit_state=emit_state),
        out_shape=out_shape,
        grid=(B, n_steps),
        in_specs=in_specs,
        out_specs=out_specs,
        scratch_shapes=[pltpu.VMEM((T + 16, wide), F32)] + [pltpu.VMEM((2, CB, n_rows, LANES), F32)] * 4,
        compiler_params=_cparams(2),
        name="rglru",
    )(*args)
    return res if emit_state else (res[0], None)


A_COLS = 5 * H_A * DK_A
B_COLS = 4 * H_B * DH_B


def kernel(x_prompt, x_sample, state_hgrn, cache_na_k, cache_na_v, state_rglru, c, c_ctx, norm_gain, w_mod, b_mod, w_in_even, w_out_even, hgrn_lb_logits, hgrn_out_gain, na_rel_bias, w_in_odd, w_out_odd, conv_w, conv_b, rg_gate_w, rg_gate_b, rg_lambda, final_gain):
    n_ctx = x_prompt.shape[0]
    n_lat = x_sample.shape[0]
    depth = w_mod.shape[0]

    cond = jnp.zeros((16, D_MODEL), F32).at[0].set(c_ctx).at[1:1 + n_lat].set(c)
    mod = _modulation(cond, w_mod, b_mod.reshape(depth, 1, 3 * D_MODEL))
    mod = mod.reshape(depth, 16, 3, D_MODEL)

    t_ctx = x_prompt.shape[1]

    def flat(a):
        return a.reshape(1, n_ctx * t_ctx, a.shape[-1])

    def unflat(a):
        return a.reshape(n_ctx, t_ctx, a.shape[-1])

    xc, xs = x_prompt, x_sample
    new_hgrn, new_k, new_v, new_rg = [], [], [], []
    for l in range(depth):
        j = l // 2
        mod_c, mod_s = mod[l, 0:1], mod[l, 1:1 + n_lat]
        gain = norm_gain[l].reshape(1, D_MODEL)
        last = l == depth - 1
        fgain = final_gain.reshape(1, D_MODEL) if last else None
        if l % 2 == 0:
            w_in = w_in_even[j].astype(BF16)
            w_out = w_out_even[j].astype(BF16)
            outs_s = ((0, A_COLS, F32), (A_COLS, B_COLS, BF16))
            outs_c = outs_s + ((A_COLS + H_B * DH_B, 2 * H_B * DH_B, F32),)
            ya_c, yb_c, kv_c = (unflat(t) for t in _inproj(flat(xc), mod_c, gain, w_in, outs_c, 512, True))
            ya_s, yb_s = _inproj(xs, mod_s, gain, w_in, outs_s, 512, False)
            hgain = hgrn_out_gain[j].reshape(H_A, 1, DK_A)
            oa_c, s_fin = _hgrn(ya_c, hgrn_lb_logits, j, hgain, None, True)
            oa_s, _ = _hgrn(ya_s, hgrn_lb_logits, j, hgain, state_hgrn[:, j], False)
            ob_c = _ctx_attn(yb_c)
            tc = cache_na_k.shape[3]
            kc = cache_na_k[:, j].transpose(0, 2, 1, 3).reshape(n_lat, tc, H_B * DH_B).astype(BF16)
            vc = cache_na_v[:, j].transpose(0, 2, 1, 3).reshape(n_lat, tc, H_B * DH_B).astype(BF16)
            ob_s = _nat(yb_s, kc, vc, na_rel_bias[j])
            xc = unflat(_outproj((flat(oa_c), flat(ob_c)), flat(xc), mod_c, w_out, 1024, True, fgain))
            xs = _outproj((oa_s, ob_s), xs, mod_s, w_out, 1024, False, fgain)
            t_c = kv_c.shape[1]
            heads = kv_c.reshape(n_ctx, t_c, 2, H_B, DH_B).transpose(2, 0, 3, 1, 4)
            new_hgrn.append(s_fin)
            new_k.append(heads[0])
            new_v.append(heads[1])
        else:
            w_in = w_in_odd[j].astype(BF16)
            w_out = w_out_odd[j].astype(BF16)
            outs = ((0, 2 * W_C, F32),)
            (xg_c,) = _inproj(flat(xc), mod_c, gain, w_in, outs, 1024, True)
            xg_c = unflat(xg_c)
            (xg_s,) = _inproj(xs, mod_s, gain, w_in, outs, 1024, False)
            wg = rg_gate_w[j].transpose(2, 3, 0, 1, 4).reshape(H_C, BW_C, 4 * BW_C).astype(BF16)
            bg = rg_gate_b[j].reshape(2, 2, H_C, BW_C).transpose(2, 0, 1, 3).reshape(H_C, 1, 4 * BW_C)
            cb = conv_b[j].reshape(1, W_C)
            y_c, h_fin = _rglru(xg_c, conv_w[j], cb, wg, bg, rg_lambda[j], None, True)
            y_s, _ = _rglru(xg_s, conv_w[j], cb, wg, bg, rg_lambda[j], state_rglru[:, j], False)
            xc = unflat(_outproj((flat(y_c),), flat(xc), mod_c, w_out, 1024, True, fgain))
            xs = _outproj((y_s,), xs, mod_s, w_out, 1024, False, fgain)
            new_rg.append(h_fin)
    return (xc, xs, jnp.stack(new_hgrn, axis=1), jnp.stack(new_k, axis=1),
            jnp.stack(new_v, axis=1), jnp.stack(new_rg, axis=1))
```

```python
import functools

import jax
import jax.numpy as jnp
from jax import lax
from jax.experimental import pallas as pl
from jax.experimental.pallas import tpu as pltpu

F32 = jnp.float32
BF16 = jnp.bfloat16

D_MODEL = 1024
EPS = 1e-6
NEG_INF = -1e30
H_A = 4
DK_A = 128
HGRN_CHUNK = 32
HGRN_ROWS = 256
H_B = 8
DH_B = 64
GRID_W = 64
NA_KH = 8
NA_KW = 16
NA_GROUP = 8
W_C = 1024
H_C = 8
BW_C = W_C // H_C
RG_C = 8.0
RG_ROWS = 256
RG_SLABS = 2
N_SEG = 16
LANES = 128
VMEM_LIMIT = 48 * 1024 * 1024

NT_DIMS = (((1,), (1,)), ((), ()))
TN_DIMS = (((0,), (0,)), ((), ()))


def _silu(x):
    return x * jax.nn.sigmoid(x)


def _cparams(n_axes):
    return pltpu.CompilerParams(dimension_semantics=("arbitrary",) * n_axes,
                                vmem_limit_bytes=VMEM_LIMIT)


def _mod_kernel(cond_ref, w_ref, b_ref, o_ref):
    s = _silu(cond_ref[...])
    o_ref[0] = jnp.dot(s.astype(BF16), w_ref[0].astype(BF16), preferred_element_type=F32) + b_ref[0]


def _modulation(cond, w_mod, b_mod):
    depth = w_mod.shape[0]
    n_rows = cond.shape[0]
    return pl.pallas_call(
        _mod_kernel,
        out_shape=jax.ShapeDtypeStruct((depth, n_rows, 3 * D_MODEL), F32),
        grid=(depth, 3),
        in_specs=[
            pl.BlockSpec((n_rows, D_MODEL), lambda l, n: (0, 0)),
            pl.BlockSpec((1, D_MODEL, D_MODEL), lambda l, n: (l, 0, n)),
            pl.BlockSpec((1, 1, D_MODEL), lambda l, n: (l, 0, n)),
        ],
        out_specs=pl.BlockSpec((1, n_rows, D_MODEL), lambda l, n: (l, 0, n)),
        compiler_params=_cparams(2),
        name="adaln_mod",
    )(cond, w_mod, b_mod)


def _inproj_kernel(x_ref, mod_ref, gain_ref, w_ref, *out_refs, outs):
    x = x_ref[0]
    var = jnp.mean(x * x, axis=-1, keepdims=True)
    y = x * lax.rsqrt(var + EPS) * gain_ref[...]
    h = y * (1.0 + mod_ref[0, 1:2, :]) + mod_ref[0, 0:1, :]
    hb = h.astype(BF16)
    step = 512
    for o_ref, (col0, width, _) in zip(out_refs, outs):
        for c in range(0, width, step):
            r = jnp.dot(hb, w_ref[:, col0 + c:col0 + c + step], preferred_element_type=F32)
            o_ref[0, :, c:c + step] = r.astype(o_ref.dtype)


def _inproj(x, mod, gain, w, outs, tm, shared_mod):
    B, T, _ = x.shape
    n_cols = w.shape[1]
    mod_map = (lambda b, t: (0, 0, 0)) if shared_mod else (lambda b, t: (b, 0, 0))
    return pl.pallas_call(
        functools.partial(_inproj_kernel, outs=outs),
        out_shape=[jax.ShapeDtypeStruct((B, T, wd), dt) for _, wd, dt in outs],
        grid=(B, T // tm),
        in_specs=[
            pl.BlockSpec((1, tm, D_MODEL), lambda b, t: (b, t, 0)),
            pl.BlockSpec((1, 3, D_MODEL), mod_map),
            pl.BlockSpec((1, D_MODEL), lambda b, t: (0, 0)),
            pl.BlockSpec((D_MODEL, n_cols), lambda b, t: (0, 0)),
        ],
        out_specs=[pl.BlockSpec((1, tm, wd), lambda b, t: (b, t, 0)) for _, wd, _ in outs],
        compiler_params=_cparams(2),
        name="in_proj",
    )(x, mod, gain, w)


def _outproj_kernel(*refs, n_y, final):
    y_refs, (x_ref, mod_ref, w_ref), rest = refs[:n_y], refs[n_y:n_y + 3], refs[n_y + 3:]
    m = None
    row = 0
    for y_ref in y_refs:
        width = y_ref.shape[-1]
        part = jnp.dot(y_ref[0], w_ref[row:row + width, :], preferred_element_type=F32)
        m = part if m is None else m + part
        row += width
    xn = x_ref[0] + mod_ref[0, 2:3, :] * m
    if final:
        gain_ref, o_ref = rest
        var = jnp.mean(xn * xn, axis=-1, keepdims=True)
        xn = xn * lax.rsqrt(var + EPS) * gain_ref[...]
    else:
        (o_ref,) = rest
    o_ref[0] = xn


def _outproj(ys, x, mod, w, tm, shared_mod, final_gain=None):
    B, T, _ = x.shape
    final = final_gain is not None
    mod_map = (lambda b, t: (0, 0, 0)) if shared_mod else (lambda b, t: (b, 0, 0))
    in_specs = [pl.BlockSpec((1, tm, y.shape[-1]), lambda b, t: (b, t, 0)) for y in ys] + [
        pl.BlockSpec((1, tm, D_MODEL), lambda b, t: (b, t, 0)),
        pl.BlockSpec((1, 3, D_MODEL), mod_map),
        pl.BlockSpec((w.shape[0], D_MODEL), lambda b, t: (0, 0)),
    ]
    args = list(ys) + [x, mod, w]
    if final:
        in_specs.append(pl.BlockSpec((1, D_MODEL), lambda b, t: (0, 0)))
        args.append(final_gain)
    return pl.pallas_call(
        functools.partial(_outproj_kernel, n_y=len(ys), final=final),
        out_shape=jax.ShapeDtypeStruct((B, T, D_MODEL), F32),
        grid=(B, T // tm),
        in_specs=in_specs,
        out_specs=pl.BlockSpec((1, tm, D_MODEL), lambda b, t: (b, t, 0)),
        compiler_params=_cparams(2),
        name="out_proj",
    )(*args)


def _hgrn_kernel(q_ref, zf_ref, zb_ref, v_ref, g_ref, lgt_ref, gain_ref, *rest, seq, layer, has_s0, emit_state):
    rest = list(rest)
    s0_ref = rest.pop(0) if has_s0 else None
    o_ref = rest.pop(0)
    sfin_ref = rest.pop(0) if emit_state else None
    acc_ref, qd_ref, ki_ref, kd_ref, kv_ref, st_ref, dec_ref, mst_ref, msk_ref, mexp_ref = rest
    R = HGRN_ROWS
    C = HGRN_CHUNK
    n_blk = seq // R
    n_chunk = R // C
    n_all = seq // C

    @pl.when((pl.program_id(0) == 0) & (pl.program_id(1) == 0))
    def _build_masks():
        ti = lax.broadcasted_iota(jnp.int32, (R, R), 0)
        tj = lax.broadcasted_iota(jnp.int32, (R, R), 1)
        shift = C.bit_length() - 1
        same = lax.shift_right_logical(ti, shift) == lax.shift_right_logical(tj, shift)
        one = jnp.ones((R, R), F32)
        zero = jnp.zeros((R, R), F32)
        incl = (jnp.where(same, jnp.where(tj <= ti, one, zero), zero),
                jnp.where(same, jnp.where(tj >= ti, one, zero), zero))
        for d in range(2):
            msk_ref[d] = incl[d]
            mst_ref[d] = incl[d].astype(BF16)
        rr = lax.broadcasted_iota(jnp.int32, (R, n_chunk * LANES), 0)
        cc = lax.broadcasted_iota(jnp.int32, (R, n_chunk * LANES), 1)
        own = lax.shift_right_logical(rr, shift) == lax.shift_right_logical(cc, LANES.bit_length() - 1)
        mexp_ref[...] = jnp.where(own, 1.0, 0.0).astype(BF16)

    lgt = [lgt_ref[:, i, :] for i in range(lgt_ref.shape[1])]
    lmax = functools.reduce(jnp.maximum, lgt)
    ex = [jnp.exp(t - lmax) for t in lgt]
    lb_all = sum(ex[:layer + 1]) / sum(ex)
    gain = gain_ref[0]

    blocks_per_trip = 2 if n_blk % 2 == 0 else 1

    def gates(i, _):
        for u in range(blocks_per_trip):
            blk = i * blocks_per_trip + u
            rows = pl.ds(pl.multiple_of(blk * R, R), R)
            q = q_ref[0, rows, :]
            for d in range(2):
                z = (zf_ref if d == 0 else zb_ref)[0, rows, :]
                lb = lb_all[d:d + 1, :]
                oml = 1.0 - lb
                e = jnp.exp(-jnp.abs(z))
                r = 1.0 / (1.0 + e)
                er = e * r
                pos = z >= 0.0
                f = lb + oml * jnp.where(pos, r, er)
                k = oml * jnp.where(pos, er, r)
                logf = jnp.log(f)
                hi = logf.astype(BF16)
                lo = (logf - hi.astype(F32)).astype(BF16)
                cs = jnp.dot(mst_ref[d], jnp.concatenate([hi, lo], axis=1), preferred_element_type=F32)
                b = cs[:, 0:LANES] + cs[:, LANES:2 * LANES]
                ends = [c * C + (C - 1 if d == 0 else 0) for c in range(n_chunk)]
                btot = jnp.concatenate([jnp.broadcast_to(b[t:t + 1, :], (C, LANES)) for t in ends], axis=0)
                qd_ref[d, rows, :] = (q * jnp.exp(b)).astype(BF16)
                ki_ref[d, rows, :] = (k * jnp.exp(-b)).astype(BF16)
                kd_ref[d, rows, :] = (k * jnp.exp(btot - b)).astype(BF16)
                for c in range(n_chunk):
                    dec_ref[d, blk * n_chunk + c] = jnp.exp(btot[c * C:c * C + 8, :])
        return 0

    lax.fori_loop(0, n_blk // blocks_per_trip, gates, 0)

    def intra(i, _):
        for u in range(blocks_per_trip):
            blk = i * blocks_per_trip + u
            rows = pl.ds(pl.multiple_of(blk * R, R), R)
            v = v_ref[0, rows, :]
            vb = v.astype(BF16)
            vt = v.T.astype(BF16)
            for d in range(2):
                att = lax.dot_general(qd_ref[d, rows, :], ki_ref[d, rows, :], NT_DIMS,
                                      preferred_element_type=F32)
                att = jnp.where(msk_ref[d] > 0.5, att, 0.0)
                acc_ref[d, rows, :] = jnp.dot(att.astype(BF16), vb, preferred_element_type=F32)
                kd_exp = jnp.concatenate([kd_ref[d, rows, :]] * n_chunk, axis=1) * mexp_ref[...]
                kv_all = jnp.dot(vt, kd_exp, preferred_element_type=F32)
                for c in range(n_chunk):
                    kv_ref[d, blk * n_chunk + c] = kv_all[:, c * LANES:(c + 1) * LANES]
        return 0

    lax.fori_loop(0, n_blk // blocks_per_trip, intra, 0)

    unroll = 4

    def states(i, sts):
        sts = list(sts)
        for u in range(unroll):
            n = i * unroll + u
            for d in range(2):
                c = n if d == 0 else n_all - 1 - n
                st_ref[d, c] = sts[d].T.astype(BF16)
                dec = jnp.concatenate([dec_ref[d, c]] * (DK_A // 8), axis=0)
                sts[d] = sts[d] * dec + kv_ref[d, c]
        return tuple(sts)

    if has_s0:
        st0 = (s0_ref[0, 0, 0].T, s0_ref[0, 1, 0].T)
    else:
        st0 = (jnp.zeros((DK_A, DK_A), F32),) * 2
    sts = lax.fori_loop(0, n_all // unroll, states, st0)
    if emit_state:
        for d in range(2):
            sfin_ref[0, d, 0] = sts[d].T

    def finish(blk, _):
        rows = pl.ds(pl.multiple_of(blk * R, R), R)
        tot = acc_ref[0, rows, :] + acc_ref[1, rows, :]
        for d in range(2):
            pieces = []
            for c in range(n_chunk):
                crow = pl.ds(pl.multiple_of(blk * R + c * C, C), C)
                pieces.append(jnp.dot(qd_ref[d, crow, :], st_ref[d, blk * n_chunk + c],
                                      preferred_element_type=F32))
            tot = tot + jnp.concatenate(pieces, axis=0)
        var = jnp.mean(tot * tot, axis=-1, keepdims=True)
        y = tot * lax.rsqrt(var + EPS) * gain
        o_ref[0, rows, :] = (y * _silu(g_ref[0, rows, :])).astype(o_ref.dtype)
        return 0

    lax.fori_loop(0, n_blk, finish, 0)


def _hgrn(ya, lgt, layer, gain, s0, emit_state):
    B, T, _ = ya.shape
    has_s0 = s0 is not None

    def col(k):
        return pl.BlockSpec((1, T, LANES), lambda b, h, k=k: (b, 0, k * H_A + h))

    in_specs = [col(0), col(1), col(2), col(3), col(4),
                pl.BlockSpec((2, lgt.shape[1], LANES), lambda b, h: (0, 0, h)),
                pl.BlockSpec((1, 1, LANES), lambda b, h: (h, 0, 0))]
    args = [ya, ya, ya, ya, ya, lgt, gain]
    if has_s0:
        in_specs.append(pl.BlockSpec((1, 2, 1, DK_A, DK_A), lambda b, h: (b, 0, h, 0, 0)))
        args.append(s0)
    out_shape = [jax.ShapeDtypeStruct((B, T, H_A * DK_A), BF16)]
    out_specs = [pl.BlockSpec((1, T, LANES), lambda b, h: (b, 0, h))]
    if emit_state:
        out_shape.append(jax.ShapeDtypeStruct((B, 2, H_A, DK_A, DK_A), F32))
        out_specs.append(pl.BlockSpec((1, 2, 1, DK_A, DK_A), lambda b, h: (b, 0, h, 0, 0)))
    res = pl.pallas_call(
        functools.partial(_hgrn_kernel, seq=T, layer=layer, has_s0=has_s0, emit_state=emit_state),
        out_shape=out_shape,
        grid=(B, H_A),
        in_specs=in_specs,
        out_specs=out_specs,
        scratch_shapes=[pltpu.VMEM((2, T, LANES), F32),
                        pltpu.VMEM((2, T, LANES), BF16),
                        pltpu.VMEM((2, T, LANES), BF16),
                        pltpu.VMEM((2, T, LANES), BF16),
                        pltpu.VMEM((2, T // HGRN_CHUNK, DK_A, DK_A), F32),
                        pltpu.VMEM((2, T // HGRN_CHUNK, DK_A, DK_A), BF16),
                        pltpu.VMEM((2, T // HGRN_CHUNK, 8, LANES), F32),
                        pltpu.VMEM((2, HGRN_ROWS, HGRN_ROWS), BF16),
                        pltpu.VMEM((2, HGRN_ROWS, HGRN_ROWS), F32),
                        pltpu.VMEM((HGRN_ROWS, HGRN_ROWS // HGRN_CHUNK * LANES), BF16)],
        compiler_params=_cparams(2),
        name="hgrn2",
    )(*args)
    return res if emit_state else (res[0], None)


def _head_masks():
    lane = lax.broadcasted_iota(jnp.int32, (1, LANES), 1)
    return lane < DH_B, lane >= DH_B


def _ctx_attn_kernel(q_ref, k_ref, v_ref, g_ref, kv_ref, o_ref, newk_ref, newv_ref):
    scale = DH_B ** -0.5
    masks = _head_masks()
    T = q_ref.shape[1]
    for h in range(H_B):
        newk_ref[0, h] = kv_ref[0, :, h * DH_B:(h + 1) * DH_B]
        newv_ref[0, h] = kv_ref[0, :, (H_B + h) * DH_B:(H_B + h + 1) * DH_B]
    for p in range(H_B // 2):
        cols = slice(p * LANES, (p + 1) * LANES)
        q = q_ref[0, :, cols] * scale
        qs = jnp.concatenate([jnp.where(masks[h], q, jnp.zeros_like(q)) for h in range(2)], axis=0)
        s = lax.dot_general(qs, k_ref[0, :, cols], NT_DIMS, preferred_element_type=F32)
        e = jnp.exp(s - jnp.max(s, axis=-1, keepdims=True))
        pr = e / jnp.sum(e, axis=-1, keepdims=True)
        o = jnp.dot(pr.astype(BF16), v_ref[0, :, cols], preferred_element_type=F32)
        o = jnp.where(masks[0], o[0:T], o[T:2 * T])
        o_ref[0, :, cols] = (o * _silu(g_ref[0, :, cols].astype(F32))).astype(o_ref.dtype)


def _ctx_attn(yb, kv):
    B, T, _ = yb.shape
    width = H_B * DH_B

    def col(k):
        return pl.BlockSpec((1, T, width), lambda b, k=k: (b, 0, k))

    cache = pl.BlockSpec((1, H_B, T, DH_B), lambda b: (b, 0, 0, 0))
    return pl.pallas_call(
        _ctx_attn_kernel,
        out_shape=[jax.ShapeDtypeStruct((B, T, width), BF16),
                   jax.ShapeDtypeStruct((B, H_B, T, DH_B), F32),
                   jax.ShapeDtypeStruct((B, H_B, T, DH_B), F32)],
        grid=(B,),
        in_specs=[col(0), col(1), col(2), col(3), pl.BlockSpec((1, T, 2 * width), lambda b: (b, 0, 0))],
        out_specs=[pl.BlockSpec((1, T, width), lambda b: (b, 0, 0)), cache, cache],
        compiler_params=_cparams(1),
        name="ctx_attn",
    )(yb, yb, yb, yb, kv)


N_DR = 2 * NA_KH - 1
N_DC = 2 * NA_KW - 1
N_TAB = N_DR - 1


def _nat_kernel(rb_ref, q_ref, k_ref, v_ref, g_ref, kc_ref, vc_ref, o_ref,
                tab_ref, qs_ref, s_ref, p_ref, r_ref, oc_ref, *, rows):
    p = pl.program_id(0)
    scale = DH_B ** -0.5
    kh = min(NA_KH, rows)
    masks = _head_masks()

    @pl.when(pl.program_id(1) == 0)
    def _build_tables():
        c = lax.broadcasted_iota(jnp.int32, (GRID_W, LANES), 0)
        lane = lax.broadcasted_iota(jnp.int32, (GRID_W, LANES), 1)
        kcol = lane & (GRID_W - 1)
        upper = lane >= GRID_W
        ws = jnp.clip(c - NA_KW // 2, 0, GRID_W - NA_KW)
        neg = jnp.full((GRID_W, LANES), NEG_INF, F32)
        diag = kcol - c + (NA_KW - 1)
        for h in range(2):
            base = (2 * p + h) * (N_DR * N_DC)

            def per_dr(i, _, base=base, h=h):
                def per_dc(dd, acc):
                    lo = rb_ref[base + i * N_DC + dd]
                    hi = rb_ref[base + (i + 1) * N_DC + dd]
                    return jnp.where(diag == dd, jnp.where(upper, hi, lo), acc)

                acc = lax.fori_loop(0, N_DC, per_dc, neg)
                acc = jnp.where(kcol >= ws, jnp.where(kcol < ws + NA_KW, acc, neg), neg)
                tab_ref[h, i] = acc
                return 0

            lax.fori_loop(0, N_TAB, per_dr, 0)

    kc = jnp.concatenate([kc_ref[0, 0], kc_ref[0, 1]], axis=1).astype(BF16)
    vc = jnp.concatenate([vc_ref[0, 0], vc_ref[0, 1]], axis=1).astype(BF16)
    n_keys = kh * GRID_W
    n_ctx = kc.shape[0]
    G = NA_GROUP
    W2 = 2 * GRID_W

    def group(gi, _):
        r_first = gi * G
        q0 = pl.multiple_of(r_first * GRID_W, G * GRID_W)
        for i in range(G):
            qi = q_ref[0, pl.ds(q0 + i * GRID_W, GRID_W), :] * scale
            for h in range(2):
                qs_ref[i * W2 + h * GRID_W:i * W2 + (h + 1) * GRID_W, :] = jnp.where(
                    masks[h], qi, jnp.zeros_like(qi))
        s_ref[:, n_keys:n_keys + n_ctx] = lax.dot_general(qs_ref[...], kc, NT_DIMS,
                                                          preferred_element_type=F32)
        windows = []
        for i in range(G):
            r = r_first + i
            rs = jnp.clip(r - kh // 2, 0, rows - kh)
            k0 = pl.multiple_of(rs * GRID_W, GRID_W)
            windows.append(k0)
            dr0 = rs - r + (NA_KH - 1)
            bias = jnp.concatenate(
                [jnp.concatenate([tab_ref[h, dr0 + 2 * m] for m in range(kh // 2)], axis=1)
                 for h in range(2)], axis=0)
            s_ref[i * W2:(i + 1) * W2, 0:n_keys] = lax.dot_general(
                qs_ref[i * W2:(i + 1) * W2, :], k_ref[0, pl.ds(k0, n_keys), :], NT_DIMS,
                preferred_element_type=F32) + bias
        for i in range(G):
            s = s_ref[i * W2:(i + 1) * W2, :]
            e = jnp.exp(s - jnp.max(s, axis=-1, keepdims=True))
            p_ref[i * W2:(i + 1) * W2, :] = e.astype(BF16)
            rinv = 1.0 / jnp.sum(e, axis=-1, keepdims=True)
            r_ref[i * W2:(i + 1) * W2, :] = jnp.broadcast_to(rinv, (W2, LANES))
        oc_ref[...] = jnp.dot(p_ref[:, n_keys:n_keys + n_ctx], vc, preferred_element_type=F32)
        for i in range(G):
            o = jnp.dot(p_ref[i * W2:(i + 1) * W2, 0:n_keys], v_ref[0, pl.ds(windows[i], n_keys), :],
                        preferred_element_type=F32)
            o = (o + oc_ref[i * W2:(i + 1) * W2, :]) * r_ref[i * W2:(i + 1) * W2, :]
            o = jnp.where(masks[0], o[0:GRID_W], o[GRID_W:W2])
            out_rows = pl.ds(q0 + i * GRID_W, GRID_W)
            gate = g_ref[0, out_rows, :].astype(F32)
            o_ref[0, out_rows, :] = (o * _silu(gate)).astype(o_ref.dtype)
        return 0

    lax.fori_loop(0, rows // G, group, 0)


def _nat(yb, kc, vc, rel_bias):
    B, T, _ = yb.shape
    Tc = kc.shape[2]
    n_pair = H_B // 2
    rows = T // GRID_W
    n_stack = NA_GROUP * 2 * GRID_W
    n_keys = min(NA_KH, rows) * GRID_W

    def col(k):
        return pl.BlockSpec((1, T, LANES), lambda p, b, k=k: (b, 0, k * n_pair + p))

    ctx = pl.BlockSpec((1, 2, Tc, DH_B), lambda p, b: (b, p, 0, 0))
    return pl.pallas_call(
        functools.partial(_nat_kernel, rows=rows),
        out_shape=jax.ShapeDtypeStruct((B, T, H_B * DH_B), BF16),
        grid=(n_pair, B),
        in_specs=[pl.BlockSpec(memory_space=pltpu.SMEM), col(0), col(1), col(2), col(3), ctx, ctx],
        out_specs=pl.BlockSpec((1, T, LANES), lambda p, b: (b, 0, p)),
        scratch_shapes=[pltpu.VMEM((2, N_TAB, GRID_W, LANES), F32),
                        pltpu.VMEM((n_stack, LANES), BF16),
                        pltpu.VMEM((n_stack, n_keys + Tc), F32),
                        pltpu.VMEM((n_stack, n_keys + Tc), BF16),
                        pltpu.VMEM((n_stack, LANES), F32),
                        pltpu.VMEM((n_stack, LANES), F32)],
        compiler_params=_cparams(2),
        name="nbr_attn",
    )(rel_bias.reshape(-1), yb, yb, yb, yb, kc, vc)


def _seg_len(seq):
    length = -(-seq // N_SEG)
    while length % 8 != 4:
        length += 1
    return length


def _rglru_kernel(x_ref, g_ref, cw_ref, cb_ref, wg_ref, bg_ref, lam_ref, *rest, seq, has_s0, emit_state):
    rest = list(rest)
    s0_ref = rest.pop(0) if has_s0 else None
    o_ref = rest.pop(0)
    hfin_ref = rest.pop(0) if emit_state else None
    xpad_ref, a_ref, u_ref, h_ref, p_ref = rest
    L = _seg_len(seq)
    n_pad = N_SEG * L - seq
    RB = RG_ROWS
    CB = RG_SLABS
    n_tile = N_SEG // 8
    chains = [(d, j, s) for d in range(2) for j in range(CB) for s in range(n_tile)]

    xpad_ref[0:8, :] = jnp.zeros((8, CB * LANES), F32)
    xpad_ref[seq + 8:seq + 16, :] = jnp.zeros((8, CB * LANES), F32)
    xpad_ref[8:seq + 8, :] = x_ref[0]
    for d in range(2):
        for j in range(CB):
            a_ref[d, j, seq:seq + n_pad, :] = jnp.ones((n_pad, LANES), F32)
            u_ref[d, j, seq:seq + n_pad, :] = jnp.zeros((n_pad, LANES), F32)

    nl = -lam_ref[...]
    sp = jnp.maximum(nl, 0.0) + jnp.log1p(jnp.exp(-jnp.abs(nl)))
    cw = cw_ref[...]
    cbias = cb_ref[...]

    def gates(blk, _):
        r0 = pl.multiple_of(blk * RB, RB)
        xm = xpad_ref[pl.ds(r0, RB + 16), :]
        xc = cw[0:1] * xm[6:6 + RB] + cw[1:2] * xm[7:7 + RB]
        xc = xc + cw[2:3] * xm[8:8 + RB]
        xc = xc + cw[3:4] * xm[9:9 + RB] + cbias
        for j in range(CB):
            xj = xc[:, j * LANES:(j + 1) * LANES]
            gt = jnp.dot(xj.astype(BF16), wg_ref[j], preferred_element_type=F32) + bg_ref[j]
            for d in range(2):
                rg = jax.nn.sigmoid(gt[:, (2 * d) * LANES:(2 * d + 1) * LANES])
                ig = jax.nn.sigmoid(gt[:, (2 * d + 1) * LANES:(2 * d + 2) * LANES])
                la = (-RG_C * rg) * sp[d:d + 1, j * LANES:(j + 1) * LANES]
                a = jnp.exp(la)
                y = -jnp.tanh(la) * (1.0 + a * a)
                root = jnp.where(y > 0.0, y * lax.rsqrt(y), 0.0)
                a_ref[d, j, pl.ds(r0, RB), :] = a
                u_ref[d, j, pl.ds(r0, RB), :] = root * (ig * xj)
        return 0

    lax.fori_loop(0, seq // RB, gates, 0)

    def seg_rows(d, s, step):
        t = step if d == 0 else L - 1 - step
        return pl.ds(t + s * 8 * L, 8, stride=L)

    unroll = 4

    def scan(i, carry):
        carry = list(carry)
        for k in range(unroll):
            for n, (d, j, s) in enumerate(chains):
                h, pr = carry[n]
                idx = seg_rows(d, s, i * unroll + k)
                a = a_ref[d, j, idx, :]
                h = a * h + u_ref[d, j, idx, :]
                pr = pr * a
                h_ref[d, j, idx, :] = h
                p_ref[d, j, idx, :] = pr
                carry[n] = (h, pr)
        return tuple(carry)

    zero = jnp.zeros((8, LANES), F32)
    one = jnp.ones((8, LANES), F32)
    ends = lax.fori_loop(0, L // unroll, scan, ((zero, one),) * len(chains))

    cins = {}
    finals = [[None] * CB for _ in range(2)]
    for d in range(2):
        for j in range(CB):
            if has_s0:
                c = s0_ref[0, d:d + 1, j * LANES:(j + 1) * LANES]
            else:
                c = jnp.zeros((1, LANES), F32)
            cin = [None] * N_SEG
            for kk in range(N_SEG):
                seg = kk if d == 0 else N_SEG - 1 - kk
                s, row = divmod(seg, 8)
                h_end, p_end = ends[chains.index((d, j, s))]
                cin[seg] = c
                c = h_end[row:row + 1, :] + p_end[row:row + 1, :] * c
            finals[d][j] = c
            for s in range(n_tile):
                cins[(d, j, s)] = jnp.concatenate(cin[s * 8:(s + 1) * 8], axis=0)

    def fix(i, _):
        for k in range(unroll):
            for (d, j, s) in chains:
                idx = seg_rows(0, s, i * unroll + k)
                h_ref[d, j, idx, :] = h_ref[d, j, idx, :] + p_ref[d, j, idx, :] * cins[(d, j, s)]
        return 0

    lax.fori_loop(0, L // unroll, fix, 0)

    if emit_state:
        hfin_ref[0] = jnp.concatenate([jnp.concatenate(finals[d], axis=1) for d in range(2)], axis=0)

    def combine(blk, _):
        rows = pl.ds(pl.multiple_of(blk * RB, RB), RB)
        hs = jnp.concatenate([h_ref[0, j, rows, :] + h_ref[1, j, rows, :] for j in range(CB)], axis=1)
        o_ref[0, rows, :] = (hs * _silu(g_ref[0, rows, :])).astype(o_ref.dtype)
        return 0

    lax.fori_loop(0, seq // RB, combine, 0)


def _rglru(xg, conv_w, conv_b, wg, bg, lam, s0, emit_state):
    B, T, _ = xg.shape
    has_s0 = s0 is not None
    n_rows = N_SEG * _seg_len(T)
    CB = RG_SLABS
    wide = CB * LANES
    n_steps = H_C // CB
    in_specs = [
        pl.BlockSpec((1, T, wide), lambda b, c: (b, 0, c)),
        pl.BlockSpec((1, T, wide), lambda b, c: (b, 0, n_steps + c)),
        pl.BlockSpec((4, wide), lambda b, c: (0, c)),
        pl.BlockSpec((1, wide), lambda b, c: (0, c)),
        pl.BlockSpec((CB, BW_C, 4 * BW_C), lambda b, c: (c, 0, 0)),
        pl.BlockSpec((CB, 1, 4 * BW_C), lambda b, c: (c, 0, 0)),
        pl.BlockSpec((2, wide), lambda b, c: (0, c)),
    ]
    args = [xg, xg, conv_w, conv_b, wg, bg, lam]
    if has_s0:
        in_specs.append(pl.BlockSpec((1, 2, wide), lambda b, c: (b, 0, c)))
        args.append(s0)
    out_shape = [jax.ShapeDtypeStruct((B, T, W_C), BF16)]
    out_specs = [pl.BlockSpec((1, T, wide), lambda b, c: (b, 0, c))]
    if emit_state:
        out_shape.append(jax.ShapeDtypeStruct((B, 2, W_C), F32))
        out_specs.append(pl.BlockSpec((1, 2, wide), lambda b, c: (b, 0, c)))
    res = pl.pallas_call(
        functools.partial(_rglru_kernel, seq=T, has_s0=has_s0, emit_state=emit_state),
        out_shape=out_shape,
        grid=(B, n_steps),
        in_specs=in_specs,
        out_specs=out_specs,
        scratch_shapes=[pltpu.VMEM((T + 16, wide), F32)] + [pltpu.VMEM((2, CB, n_rows, LANES), F32)] * 4,
        compiler_params=_cparams(2),
        name="rglru",
    )(*args)
    return res if emit_state else (res[0], None)


A_COLS = 5 * H_A * DK_A
B_COLS = 4 * H_B * DH_B


def kernel(x_prompt, x_sample, state_hgrn, cache_na_k, cache_na_v, state_rglru, c, c_ctx, norm_gain, w_mod, b_mod, w_in_even, w_out_even, hgrn_lb_logits, hgrn_out_gain, na_rel_bias, w_in_odd, w_out_odd, conv_w, conv_b, rg_gate_w, rg_gate_b, rg_lambda, final_gain):
    n_ctx = x_prompt.shape[0]
    n_lat = x_sample.shape[0]
    depth = w_mod.shape[0]

    cond = jnp.zeros((16, D_MODEL), F32).at[0].set(c_ctx).at[1:1 + n_lat].set(c)
    mod = _modulation(cond, w_mod, b_mod.reshape(depth, 1, 3 * D_MODEL))
    mod = mod.reshape(depth, 16, 3, D_MODEL)

    t_ctx = x_prompt.shape[1]

    def flat(a):
        return a.reshape(1, n_ctx * t_ctx, a.shape[-1])

    def unflat(a):
        return a.reshape(n_ctx, t_ctx, a.shape[-1])

    xc, xs = x_prompt, x_sample
    new_hgrn, new_k, new_v, new_rg = [], [], [], []
    for l in range(depth):
        j = l // 2
        mod_c, mod_s = mod[l, 0:1], mod[l, 1:1 + n_lat]
        gain = norm_gain[l].reshape(1, D_MODEL)
        last = l == depth - 1
        fgain = final_gain.reshape(1, D_MODEL) if last else None
        if l % 2 == 0:
            w_in = w_in_even[j].astype(BF16)
            w_out = w_out_even[j].astype(BF16)
            outs_s = ((0, A_COLS, F32), (A_COLS, B_COLS, BF16))
            outs_c = outs_s + ((A_COLS + H_B * DH_B, 2 * H_B * DH_B, F32),)
            ya_c, yb_c, kv_c = (unflat(t) for t in _inproj(flat(xc), mod_c, gain, w_in, outs_c, 512, True))
            ya_s, yb_s = _inproj(xs, mod_s, gain, w_in, outs_s, 512, False)
            hgain = hgrn_out_gain[j].reshape(H_A, 1, DK_A)
            oa_c, s_fin = _hgrn(ya_c, hgrn_lb_logits, j, hgain, None, True)
            oa_s, _ = _hgrn(ya_s, hgrn_lb_logits, j, hgain, state_hgrn[:, j], False)
            ob_c, k_c, v_c = _ctx_attn(yb_c, kv_c)
            ob_s = _nat(yb_s, cache_na_k[:, j], cache_na_v[:, j], na_rel_bias[j])
            xc = unflat(_outproj((flat(oa_c), flat(ob_c)), flat(xc), mod_c, w_out, 1024, True, fgain))
            xs = _outproj((oa_s, ob_s), xs, mod_s, w_out, 1024, False, fgain)
            new_hgrn.append(s_fin)
            new_k.append(k_c)
            new_v.append(v_c)
        else:
            w_in = w_in_odd[j].astype(BF16)
            w_out = w_out_odd[j].astype(BF16)
            outs = ((0, 2 * W_C, F32),)
            (xg_c,) = _inproj(flat(xc), mod_c, gain, w_in, outs, 1024, True)
            xg_c = unflat(xg_c)
            (xg_s,) = _inproj(xs, mod_s, gain, w_in, outs, 1024, False)
            wg = rg_gate_w[j].transpose(2, 3, 0, 1, 4).reshape(H_C, BW_C, 4 * BW_C).astype(BF16)
            bg = rg_gate_b[j].reshape(2, 2, H_C, BW_C).transpose(2, 0, 1, 3).reshape(H_C, 1, 4 * BW_C)
            cb = conv_b[j].reshape(1, W_C)
            y_c, h_fin = _rglru(xg_c, conv_w[j], cb, wg, bg, rg_lambda[j], None, True)
            y_s, _ = _rglru(xg_s, conv_w[j], cb, wg, bg, rg_lambda[j], state_rglru[:, j], False)
            xc = unflat(_outproj((flat(y_c),), flat(xc), mod_c, w_out, 1024, True, fgain))
            xs = _outproj((y_s,), xs, mod_s, w_out, 1024, False, fgain)
            new_rg.append(h_fin)
    return (xc, xs, jnp.stack(new_hgrn, axis=1), jnp.stack(new_k, axis=1),
            jnp.stack(new_v, axis=1), jnp.stack(new_rg, axis=1))
```

```python
import functools

import jax
import jax.numpy as jnp
from jax import lax
from jax.experimental import pallas as pl
from jax.experimental.pallas import tpu as pltpu

F32 = jnp.float32
BF16 = jnp.bfloat16

D_MODEL = 1024
EPS = 1e-6
NEG_INF = -1e30
H_A = 4
DK_A = 128
HGRN_CHUNK = 32
HGRN_ROWS = 256
H_B = 8
DH_B = 64
GRID_W = 64
NA_KH = 8
NA_KW = 16
NA_GROUP = 8
W_C = 1024
H_C = 8
BW_C = W_C // H_C
RG_C = 8.0
RG_ROWS = 256
RG_SLABS = 2
N_SEG = 16
LANES = 128
VMEM_LIMIT = 48 * 1024 * 1024

NT_DIMS = (((1,), (1,)), ((), ()))
TN_DIMS = (((0,), (0,)), ((), ()))


def _silu(x):
    return x * jax.nn.sigmoid(x)


def _cparams(n_axes):
    return pltpu.CompilerParams(dimension_semantics=("arbitrary",) * n_axes,
                                vmem_limit_bytes=VMEM_LIMIT)


def _mod_kernel(cond_ref, w_ref, b_ref, o_ref):
    s = _silu(cond_ref[...])
    o_ref[0] = jnp.dot(s.astype(BF16), w_ref[0].astype(BF16), preferred_element_type=F32) + b_ref[0]


def _modulation(cond, w_mod, b_mod):
    depth = w_mod.shape[0]
    n_rows = cond.shape[0]
    return pl.pallas_call(
        _mod_kernel,
        out_shape=jax.ShapeDtypeStruct((depth, n_rows, 3 * D_MODEL), F32),
        grid=(depth, 3),
        in_specs=[
            pl.BlockSpec((n_rows, D_MODEL), lambda l, n: (0, 0)),
            pl.BlockSpec((1, D_MODEL, D_MODEL), lambda l, n: (l, 0, n)),
            pl.BlockSpec((1, 1, D_MODEL), lambda l, n: (l, 0, n)),
        ],
        out_specs=pl.BlockSpec((1, n_rows, D_MODEL), lambda l, n: (l, 0, n)),
        compiler_params=_cparams(2),
        name="adaln_mod",
    )(cond, w_mod, b_mod)


def _project(x, mod_ref, gain_ref, w_ref, out_refs, outs):
    var = jnp.mean(x * x, axis=-1, keepdims=True)
    y = x * lax.rsqrt(var + EPS) * gain_ref[...]
    h = y * (1.0 + mod_ref[0, 1:2, :]) + mod_ref[0, 0:1, :]
    hb = h.astype(BF16)
    step = 512
    for o_ref, (col0, width, _) in zip(out_refs, outs):
        for c in range(0, width, step):
            r = jnp.dot(hb, w_ref[:, col0 + c:col0 + c + step], preferred_element_type=F32)
            o_ref[0, :, c:c + step] = r.astype(o_ref.dtype)


def _inproj_kernel(x_ref, mod_ref, gain_ref, w_ref, *out_refs, outs):
    _project(x_ref[0], mod_ref, gain_ref, w_ref, out_refs, outs)


def _inproj(x, mod, gain, w, outs, tm, shared_mod):
    B, T, _ = x.shape
    n_cols = w.shape[1]
    mod_map = (lambda b, t: (0, 0, 0)) if shared_mod else (lambda b, t: (b, 0, 0))
    return pl.pallas_call(
        functools.partial(_inproj_kernel, outs=outs),
        out_shape=[jax.ShapeDtypeStruct((B, T, wd), dt) for _, wd, dt in outs],
        grid=(B, T // tm),
        in_specs=[
            pl.BlockSpec((1, tm, D_MODEL), lambda b, t: (b, t, 0)),
            pl.BlockSpec((1, 3, D_MODEL), mod_map),
            pl.BlockSpec((1, D_MODEL), lambda b, t: (0, 0)),
            pl.BlockSpec((D_MODEL, n_cols), lambda b, t: (0, 0)),
        ],
        out_specs=[pl.BlockSpec((1, tm, wd), lambda b, t: (b, t, 0)) for _, wd, _ in outs],
        compiler_params=_cparams(2),
        name="in_proj",
    )(x, mod, gain, w)


def _outproj_kernel(*refs, n_y, final, next_outs):
    y_refs, (x_ref, mod_ref, w_ref), rest = refs[:n_y], refs[n_y:n_y + 3], refs[n_y + 3:]
    m = None
    row = 0
    for y_ref in y_refs:
        width = y_ref.shape[-1]
        part = jnp.dot(y_ref[0], w_ref[row:row + width, :], preferred_element_type=F32)
        m = part if m is None else m + part
        row += width
    xn = x_ref[0] + mod_ref[0, 2:3, :] * m
    if final:
        gain_ref, o_ref = rest
        var = jnp.mean(xn * xn, axis=-1, keepdims=True)
        o_ref[0] = xn * lax.rsqrt(var + EPS) * gain_ref[...]
    else:
        modn_ref, gainn_ref, wn_ref, o_ref = rest[:4]
        o_ref[0] = xn
        _project(xn, modn_ref, gainn_ref, wn_ref, rest[4:], next_outs)


def _outproj(ys, x, mod, w, tm, shared_mod, final_gain=None, next_proj=None):
    B, T, _ = x.shape
    final = final_gain is not None
    mod_map = (lambda b, t: (0, 0, 0)) if shared_mod else (lambda b, t: (b, 0, 0))
    row_block = pl.BlockSpec((1, tm, D_MODEL), lambda b, t: (b, t, 0))
    vec = pl.BlockSpec((1, D_MODEL), lambda b, t: (0, 0))
    in_specs = [pl.BlockSpec((1, tm, y.shape[-1]), lambda b, t: (b, t, 0)) for y in ys] + [
        row_block,
        pl.BlockSpec((1, 3, D_MODEL), mod_map),
        pl.BlockSpec((w.shape[0], D_MODEL), lambda b, t: (0, 0)),
    ]
    args = list(ys) + [x, mod, w]
    out_shape = [jax.ShapeDtypeStruct((B, T, D_MODEL), F32)]
    out_specs = [row_block]
    next_outs = None
    if final:
        in_specs.append(vec)
        args.append(final_gain)
    else:
        mod_n, gain_n, w_n, next_outs = next_proj
        in_specs += [pl.BlockSpec((1, 3, D_MODEL), mod_map), vec,
                     pl.BlockSpec((D_MODEL, w_n.shape[1]), lambda b, t: (0, 0))]
        args += [mod_n, gain_n, w_n]
        out_shape += [jax.ShapeDtypeStruct((B, T, wd), dt) for _, wd, dt in next_outs]
        out_specs += [pl.BlockSpec((1, tm, wd), lambda b, t: (b, t, 0)) for _, wd, _ in next_outs]
    return pl.pallas_call(
        functools.partial(_outproj_kernel, n_y=len(ys), final=final, next_outs=next_outs),
        out_shape=out_shape,
        grid=(B, T // tm),
        in_specs=in_specs,
        out_specs=out_specs,
        compiler_params=_cparams(2),
        name="out_proj",
    )(*args)


def _hgrn_kernel(q_ref, zf_ref, zb_ref, v_ref, g_ref, lgt_ref, gain_ref, *rest, seq, layer, has_s0, emit_state):
    rest = list(rest)
    s0_ref = rest.pop(0) if has_s0 else None
    o_ref = rest.pop(0)
    sfin_ref = rest.pop(0) if emit_state else None
    acc_ref, qd_ref, ki_ref, kd_ref, kv_ref, st_ref, dec_ref, mst_ref, msk_ref, mexp_ref = rest
    R = HGRN_ROWS
    C = HGRN_CHUNK
    n_blk = seq // R
    n_chunk = R // C
    n_all = seq // C

    @pl.when((pl.program_id(0) == 0) & (pl.program_id(1) == 0))
    def _build_masks():
        ti = lax.broadcasted_iota(jnp.int32, (R, R), 0)
        tj = lax.broadcasted_iota(jnp.int32, (R, R), 1)
        shift = C.bit_length() - 1
        same = lax.shift_right_logical(ti, shift) == lax.shift_right_logical(tj, shift)
        one = jnp.ones((R, R), F32)
        zero = jnp.zeros((R, R), F32)
        incl = (jnp.where(same, jnp.where(tj <= ti, one, zero), zero),
                jnp.where(same, jnp.where(tj >= ti, one, zero), zero))
        for d in range(2):
            msk_ref[d] = incl[d]
            mst_ref[d] = incl[d].astype(BF16)
        rr = lax.broadcasted_iota(jnp.int32, (R, n_chunk * LANES), 0)
        cc = lax.broadcasted_iota(jnp.int32, (R, n_chunk * LANES), 1)
        own = lax.shift_right_logical(rr, shift) == lax.shift_right_logical(cc, LANES.bit_length() - 1)
        mexp_ref[...] = jnp.where(own, 1.0, 0.0).astype(BF16)

    lgt = [lgt_ref[:, i, :] for i in range(lgt_ref.shape[1])]
    lmax = functools.reduce(jnp.maximum, lgt)
    ex = [jnp.exp(t - lmax) for t in lgt]
    lb_all = sum(ex[:layer + 1]) / sum(ex)
    gain = gain_ref[0]

    blocks_per_trip = 2 if n_blk % 2 == 0 else 1

    def gates(i, _):
        for u in range(blocks_per_trip):
            blk = i * blocks_per_trip + u
            rows = pl.ds(pl.multiple_of(blk * R, R), R)
            q = q_ref[0, rows, :]
            for d in range(2):
                z = (zf_ref if d == 0 else zb_ref)[0, rows, :]
                lb = lb_all[d:d + 1, :]
                oml = 1.0 - lb
                e = jnp.exp(-jnp.abs(z))
                r = 1.0 / (1.0 + e)
                er = e * r
                pos = z >= 0.0
                f = lb + oml * jnp.where(pos, r, er)
                k = oml * jnp.where(pos, er, r)
                logf = jnp.log(f)
                hi = logf.astype(BF16)
                lo = (logf - hi.astype(F32)).astype(BF16)
                cs = jnp.dot(mst_ref[d], jnp.concatenate([hi, lo], axis=1), preferred_element_type=F32)
                b = cs[:, 0:LANES] + cs[:, LANES:2 * LANES]
                ends = [c * C + (C - 1 if d == 0 else 0) for c in range(n_chunk)]
                btot = jnp.concatenate([jnp.broadcast_to(b[t:t + 1, :], (C, LANES)) for t in ends], axis=0)
                qd_ref[d, rows, :] = (q * jnp.exp(b)).astype(BF16)
                ki_ref[d, rows, :] = (k * jnp.exp(-b)).astype(BF16)
                kd_ref[d, rows, :] = (k * jnp.exp(btot - b)).astype(BF16)
                for c in range(n_chunk):
                    dec_ref[d, blk * n_chunk + c] = jnp.exp(btot[c * C:c * C + 8, :])
        return 0

    lax.fori_loop(0, n_blk // blocks_per_trip, gates, 0)

    def intra(i, _):
        for u in range(blocks_per_trip):
            blk = i * blocks_per_trip + u
            rows = pl.ds(pl.multiple_of(blk * R, R), R)
            v = v_ref[0, rows, :]
            vb = v.astype(BF16)
            vt = v.T.astype(BF16)
            for d in range(2):
                att = lax.dot_general(qd_ref[d, rows, :], ki_ref[d, rows, :], NT_DIMS,
                                      preferred_element_type=F32)
                att = jnp.where(msk_ref[d] > 0.5, att, 0.0)
                acc_ref[d, rows, :] = jnp.dot(att.astype(BF16), vb, preferred_element_type=F32)
                kd_exp = jnp.concatenate([kd_ref[d, rows, :]] * n_chunk, axis=1) * mexp_ref[...]
                kv_all = jnp.dot(vt, kd_exp, preferred_element_type=F32)
                for c in range(n_chunk):
                    kv_ref[d, blk * n_chunk + c] = kv_all[:, c * LANES:(c + 1) * LANES]
        return 0

    lax.fori_loop(0, n_blk // blocks_per_trip, intra, 0)

    unroll = 4

    def states(i, sts):
        sts = list(sts)
        for u in range(unroll):
            n = i * unroll + u
            for d in range(2):
                c = n if d == 0 else n_all - 1 - n
                st_ref[d, c] = sts[d].astype(BF16)
                dec = jnp.concatenate([dec_ref[d, c]] * (DK_A // 8), axis=0)
                sts[d] = sts[d] * dec + kv_ref[d, c]
        return tuple(sts)

    if has_s0:
        st0 = (s0_ref[0, 0, 0].T, s0_ref[0, 1, 0].T)
    else:
        st0 = (jnp.zeros((DK_A, DK_A), F32),) * 2
    sts = lax.fori_loop(0, n_all // unroll, states, st0)
    if emit_state:
        for d in range(2):
            sfin_ref[0, d, 0] = sts[d].T

    def finish(blk, _):
        rows = pl.ds(pl.multiple_of(blk * R, R), R)
        tot = acc_ref[0, rows, :] + acc_ref[1, rows, :]
        for d in range(2):
            pieces = []
            for c in range(n_chunk):
                crow = pl.ds(pl.multiple_of(blk * R + c * C, C), C)
                pieces.append(lax.dot_general(qd_ref[d, crow, :], st_ref[d, blk * n_chunk + c], NT_DIMS,
                                              preferred_element_type=F32))
            tot = tot + jnp.concatenate(pieces, axis=0)
        var = jnp.mean(tot * tot, axis=-1, keepdims=True)
        y = tot * lax.rsqrt(var + EPS) * gain
        o_ref[0, rows, :] = (y * _silu(g_ref[0, rows, :])).astype(o_ref.dtype)
        return 0

    lax.fori_loop(0, n_blk, finish, 0)


def _hgrn(ya, lgt, layer, gain, s0, emit_state):
    B, T, _ = ya.shape
    has_s0 = s0 is not None

    def col(k):
        return pl.BlockSpec((1, T, LANES), lambda b, h, k=k: (b, 0, k * H_A + h))

    in_specs = [col(0), col(1), col(2), col(3), col(4),
                pl.BlockSpec((2, lgt.shape[1], LANES), lambda b, h: (0, 0, h)),
                pl.BlockSpec((1, 1, LANES), lambda b, h: (h, 0, 0))]
    args = [ya, ya, ya, ya, ya, lgt, gain]
    if has_s0:
        in_specs.append(pl.BlockSpec((1, 2, 1, DK_A, DK_A), lambda b, h: (b, 0, h, 0, 0)))
        args.append(s0)
    out_shape = [jax.ShapeDtypeStruct((B, T, H_A * DK_A), BF16)]
    out_specs = [pl.BlockSpec((1, T, LANES), lambda b, h: (b, 0, h))]
    if emit_state:
        out_shape.append(jax.ShapeDtypeStruct((B, 2, H_A, DK_A, DK_A), F32))
        out_specs.append(pl.BlockSpec((1, 2, 1, DK_A, DK_A), lambda b, h: (b, 0, h, 0, 0)))
    res = pl.pallas_call(
        functools.partial(_hgrn_kernel, seq=T, layer=layer, has_s0=has_s0, emit_state=emit_state),
        out_shape=out_shape,
        grid=(B, H_A),
        in_specs=in_specs,
        out_specs=out_specs,
        scratch_shapes=[pltpu.VMEM((2, T, LANES), F32),
                        pltpu.VMEM((2, T, LANES), BF16),
                        pltpu.VMEM((2, T, LANES), BF16),
                        pltpu.VMEM((2, T, LANES), BF16),
                        pltpu.VMEM((2, T // HGRN_CHUNK, DK_A, DK_A), F32),
                        pltpu.VMEM((2, T // HGRN_CHUNK, DK_A, DK_A), BF16),
                        pltpu.VMEM((2, T // HGRN_CHUNK, 8, LANES), F32),
                        pltpu.VMEM((2, HGRN_ROWS, HGRN_ROWS), BF16),
                        pltpu.VMEM((2, HGRN_ROWS, HGRN_ROWS), F32),
                        pltpu.VMEM((HGRN_ROWS, HGRN_ROWS // HGRN_CHUNK * LANES), BF16)],
        compiler_params=_cparams(2),
        name="hgrn2",
    )(*args)
    return res if emit_state else (res[0], None)


def _head_masks():
    lane = lax.broadcasted_iota(jnp.int32, (1, LANES), 1)
    return lane < DH_B, lane >= DH_B


def _ctx_attn_kernel(q_ref, k_ref, v_ref, g_ref, kv_ref, o_ref, newk_ref, newv_ref):
    scale = DH_B ** -0.5
    masks = _head_masks()
    T = q_ref.shape[1]
    for h in range(H_B):
        newk_ref[0, h] = kv_ref[0, :, h * DH_B:(h + 1) * DH_B]
        newv_ref[0, h] = kv_ref[0, :, (H_B + h) * DH_B:(H_B + h + 1) * DH_B]
    for p in range(H_B // 2):
        cols = slice(p * LANES, (p + 1) * LANES)
        q = q_ref[0, :, cols] * scale
        qs = jnp.concatenate([jnp.where(masks[h], q, jnp.zeros_like(q)) for h in range(2)], axis=0)
        s = lax.dot_general(qs, k_ref[0, :, cols], NT_DIMS, preferred_element_type=F32)
        e = jnp.exp(s - jnp.max(s, axis=-1, keepdims=True))
        pr = e / jnp.sum(e, axis=-1, keepdims=True)
        o = jnp.dot(pr.astype(BF16), v_ref[0, :, cols], preferred_element_type=F32)
        o = jnp.where(masks[0], o[0:T], o[T:2 * T])
        o_ref[0, :, cols] = (o * _silu(g_ref[0, :, cols].astype(F32))).astype(o_ref.dtype)


def _ctx_attn(yb, kv):
    B, T, _ = yb.shape
    width = H_B * DH_B

    def col(k):
        return pl.BlockSpec((1, T, width), lambda b, k=k: (b, 0, k))

    cache = pl.BlockSpec((1, H_B, T, DH_B), lambda b: (b, 0, 0, 0))
    return pl.pallas_call(
        _ctx_attn_kernel,
        out_shape=[jax.ShapeDtypeStruct((B, T, width), BF16),
                   jax.ShapeDtypeStruct((B, H_B, T, DH_B), F32),
                   jax.ShapeDtypeStruct((B, H_B, T, DH_B), F32)],
        grid=(B,),
        in_specs=[col(0), col(1), col(2), col(3), pl.BlockSpec((1, T, 2 * width), lambda b: (b, 0, 0))],
        out_specs=[pl.BlockSpec((1, T, width), lambda b: (b, 0, 0)), cache, cache],
        compiler_params=_cparams(1),
        name="ctx_attn",
    )(yb, yb, yb, yb, kv)


N_DR = 2 * NA_KH - 1
N_DC = 2 * NA_KW - 1
N_TAB = N_DR - 1


def _nat_kernel(rb_ref, q_ref, k_ref, v_ref, g_ref, kc_ref, vc_ref, o_ref,
                tab_ref, qs_ref, s_ref, p_ref, r_ref, oc_ref, *, rows):
    p = pl.program_id(0)
    scale = DH_B ** -0.5
    kh = min(NA_KH, rows)
    masks = _head_masks()

    @pl.when(pl.program_id(1) == 0)
    def _build_tables():
        c = lax.broadcasted_iota(jnp.int32, (GRID_W, LANES), 0)
        lane = lax.broadcasted_iota(jnp.int32, (GRID_W, LANES), 1)
        kcol = lane & (GRID_W - 1)
        upper = lane >= GRID_W
        ws = jnp.clip(c - NA_KW // 2, 0, GRID_W - NA_KW)
        neg = jnp.full((GRID_W, LANES), NEG_INF, F32)
        diag = kcol - c + (NA_KW - 1)
        for h in range(2):
            base = (2 * p + h) * (N_DR * N_DC)

            def per_dr(i, _, base=base, h=h):
                def per_dc(dd, acc):
                    lo = rb_ref[base + i * N_DC + dd]
                    hi = rb_ref[base + (i + 1) * N_DC + dd]
                    return jnp.where(diag == dd, jnp.where(upper, hi, lo), acc)

                acc = lax.fori_loop(0, N_DC, per_dc, neg)
                acc = jnp.where(kcol >= ws, jnp.where(kcol < ws + NA_KW, acc, neg), neg)
                tab_ref[h, i] = acc
                return 0

            lax.fori_loop(0, N_TAB, per_dr, 0)

    kc = jnp.concatenate([kc_ref[0, 0], kc_ref[0, 1]], axis=1).astype(BF16)
    vc = jnp.concatenate([vc_ref[0, 0], vc_ref[0, 1]], axis=1).astype(BF16)
    n_keys = kh * GRID_W
    n_ctx = kc.shape[0]
    G = NA_GROUP
    W2 = 2 * GRID_W

    def group(gi, _):
        r_first = gi * G
        q0 = pl.multiple_of(r_first * GRID_W, G * GRID_W)
        for i in range(G):
            qi = q_ref[0, pl.ds(q0 + i * GRID_W, GRID_W), :] * scale
            for h in range(2):
                qs_ref[i * W2 + h * GRID_W:i * W2 + (h + 1) * GRID_W, :] = jnp.where(
                    masks[h], qi, jnp.zeros_like(qi))
        s_ref[:, n_keys:n_keys + n_ctx] = lax.dot_general(qs_ref[...], kc, NT_DIMS,
                                                          preferred_element_type=F32)
        windows = []
        for i in range(G):
            r = r_first + i
            rs = jnp.clip(r - kh // 2, 0, rows - kh)
            k0 = pl.multiple_of(rs * GRID_W, GRID_W)
            windows.append(k0)
            dr0 = rs - r + (NA_KH - 1)
            bias = jnp.concatenate(
                [jnp.concatenate([tab_ref[h, dr0 + 2 * m] for m in range(kh // 2)], axis=1)
                 for h in range(2)], axis=0)
            s_ref[i * W2:(i + 1) * W2, 0:n_keys] = lax.dot_general(
                qs_ref[i * W2:(i + 1) * W2, :], k_ref[0, pl.ds(k0, n_keys), :], NT_DIMS,
                preferred_element_type=F32) + bias
        for i in range(G):
            s = s_ref[i * W2:(i + 1) * W2, :]
            e = jnp.exp(s - jnp.max(s, axis=-1, keepdims=True))
            p_ref[i * W2:(i + 1) * W2, :] = e.astype(BF16)
            rinv = 1.0 / jnp.sum(e, axis=-1, keepdims=True)
            r_ref[i * W2:(i + 1) * W2, :] = jnp.broadcast_to(rinv, (W2, LANES))
        oc_ref[...] = jnp.dot(p_ref[:, n_keys:n_keys + n_ctx], vc, preferred_element_type=F32)
        for i in range(G):
            o = jnp.dot(p_ref[i * W2:(i + 1) * W2, 0:n_keys], v_ref[0, pl.ds(windows[i], n_keys), :],
                        preferred_element_type=F32)
            o = (o + oc_ref[i * W2:(i + 1) * W2, :]) * r_ref[i * W2:(i + 1) * W2, :]
            o = jnp.where(masks[0], o[0:GRID_W], o[GRID_W:W2])
            out_rows = pl.ds(q0 + i * GRID_W, GRID_W)
            gate = g_ref[0, out_rows, :].astype(F32)
            o_ref[0, out_rows, :] = (o * _silu(gate)).astype(o_ref.dtype)
        return 0

    lax.fori_loop(0, rows // G, group, 0)


def _nat(yb, kc, vc, rel_bias):
    B, T, _ = yb.shape
    Tc = kc.shape[2]
    n_pair = H_B // 2
    rows = T // GRID_W
    n_stack = NA_GROUP * 2 * GRID_W
    n_keys = min(NA_KH, rows) * GRID_W

    def col(k):
        return pl.BlockSpec((1, T, LANES), lambda p, b, k=k: (b, 0, k * n_pair + p))

    ctx = pl.BlockSpec((1, 2, Tc, DH_B), lambda p, b: (b, p, 0, 0))
    return pl.pallas_call(
        functools.partial(_nat_kernel, rows=rows),
        out_shape=jax.ShapeDtypeStruct((B, T, H_B * DH_B), BF16),
        grid=(n_pair, B),
        in_specs=[pl.BlockSpec(memory_space=pltpu.SMEM), col(0), col(1), col(2), col(3), ctx, ctx],
        out_specs=pl.BlockSpec((1, T, LANES), lambda p, b: (b, 0, p)),
        scratch_shapes=[pltpu.VMEM((2, N_TAB, GRID_W, LANES), F32),
                        pltpu.VMEM((n_stack, LANES), BF16),
                        pltpu.VMEM((n_stack, n_keys + Tc), F32),
                        pltpu.VMEM((n_stack, n_keys + Tc), BF16),
                        pltpu.VMEM((n_stack, LANES), F32),
                        pltpu.VMEM((n_stack, LANES), F32)],
        compiler_params=_cparams(2),
        name="nbr_attn",
    )(rel_bias.reshape(-1), yb, yb, yb, yb, kc, vc)


def _seg_len(seq):
    length = -(-seq // N_SEG)
    while length % 8 != 4:
        length += 1
    return length


def _rglru_kernel(x_ref, g_ref, cw_ref, cb_ref, wg_ref, bg_ref, lam_ref, *rest, seq, has_s0, emit_state):
    rest = list(rest)
    s0_ref = rest.pop(0) if has_s0 else None
    o_ref = rest.pop(0)
    hfin_ref = rest.pop(0) if emit_state else None
    xpad_ref, a_ref, u_ref, h_ref, p_ref = rest
    L = _seg_len(seq)
    n_pad = N_SEG * L - seq
    RB = RG_ROWS
    CB = RG_SLABS
    n_tile = N_SEG // 8
    chains = [(d, j, s) for d in range(2) for j in range(CB) for s in range(n_tile)]

    xpad_ref[0:8, :] = jnp.zeros((8, CB * LANES), F32)
    xpad_ref[seq + 8:seq + 16, :] = jnp.zeros((8, CB * LANES), F32)
    xpad_ref[8:seq + 8, :] = x_ref[0]
    for d in range(2):
        for j in range(CB):
            a_ref[d, j, seq:seq + n_pad, :] = jnp.ones((n_pad, LANES), F32)
            u_ref[d, j, seq:seq + n_pad, :] = jnp.zeros((n_pad, LANES), F32)

    nl = -lam_ref[...]
    sp = jnp.maximum(nl, 0.0) + jnp.log1p(jnp.exp(-jnp.abs(nl)))
    cw = cw_ref[...]
    cbias = cb_ref[...]

    def gates(blk, _):
        r0 = pl.multiple_of(blk * RB, RB)
        xm = xpad_ref[pl.ds(r0, RB + 16), :]
        xc = cw[0:1] * xm[6:6 + RB] + cw[1:2] * xm[7:7 + RB]
        xc = xc + cw[2:3] * xm[8:8 + RB]
        xc = xc + cw[3:4] * xm[9:9 + RB] + cbias
        for j in range(CB):
            xj = xc[:, j * LANES:(j + 1) * LANES]
            gt = jnp.dot(xj.astype(BF16), wg_ref[j], preferred_element_type=F32) + bg_ref[j]
            for d in range(2):
                rg = jax.nn.sigmoid(gt[:, (2 * d) * LANES:(2 * d + 1) * LANES])
                ig = jax.nn.sigmoid(gt[:, (2 * d + 1) * LANES:(2 * d + 2) * LANES])
                la = (-RG_C * rg) * sp[d:d + 1, j * LANES:(j + 1) * LANES]
                a = jnp.exp(la)
                y = -jnp.tanh(la) * (1.0 + a * a)
                root = jnp.where(y > 0.0, y * lax.rsqrt(y), 0.0)
                a_ref[d, j, pl.ds(r0, RB), :] = a
                u_ref[d, j, pl.ds(r0, RB), :] = root * (ig * xj)
        return 0

    lax.fori_loop(0, seq // RB, gates, 0)

    def seg_rows(d, s, step):
        t = step if d == 0 else L - 1 - step
        return pl.ds(t + s * 8 * L, 8, stride=L)

    unroll = 4

    def scan(i, carry):
        carry = list(carry)
        for k in range(unroll):
            for n, (d, j, s) in enumerate(chains):
                h, pr = carry[n]
                idx = seg_rows(d, s, i * unroll + k)
                a = a_ref[d, j, idx, :]
                h = a * h + u_ref[d, j, idx, :]
                pr = pr * a
                h_ref[d, j, idx, :] = h
                p_ref[d, j, idx, :] = pr
                carry[n] = (h, pr)
        return tuple(carry)

    zero = jnp.zeros((8, LANES), F32)
    one = jnp.ones((8, LANES), F32)
    ends = lax.fori_loop(0, L // unroll, scan, ((zero, one),) * len(chains))

    cins = {}
    finals = [[None] * CB for _ in range(2)]
    for d in range(2):
        for j in range(CB):
            if has_s0:
                c = s0_ref[0, d:d + 1, j * LANES:(j + 1) * LANES]
            else:
                c = jnp.zeros((1, LANES), F32)
            cin = [None] * N_SEG
            for kk in range(N_SEG):
                seg = kk if d == 0 else N_SEG - 1 - kk
                s, row = divmod(seg, 8)
                h_end, p_end = ends[chains.index((d, j, s))]
                cin[seg] = c
                c = h_end[row:row + 1, :] + p_end[row:row + 1, :] * c
            finals[d][j] = c
            for s in range(n_tile):
                cins[(d, j, s)] = jnp.concatenate(cin[s * 8:(s + 1) * 8], axis=0)

    def fix(i, _):
        for k in range(unroll):
            for (d, j, s) in chains:
                idx = seg_rows(0, s, i * unroll + k)
                h_ref[d, j, idx, :] = h_ref[d, j, idx, :] + p_ref[d, j, idx, :] * cins[(d, j, s)]
        return 0

    lax.fori_loop(0, L // unroll, fix, 0)

    if emit_state:
        hfin_ref[0] = jnp.concatenate([jnp.concatenate(finals[d], axis=1) for d in range(2)], axis=0)

    def combine(blk, _):
        rows = pl.ds(pl.multiple_of(blk * RB, RB), RB)
        hs = jnp.concatenate([h_ref[0, j, rows, :] + h_ref[1, j, rows, :] for j in range(CB)], axis=1)
        o_ref[0, rows, :] = (hs * _silu(g_ref[0, rows, :])).astype(o_ref.dtype)
        return 0

    lax.fori_loop(0, seq // RB, combine, 0)


def _rglru(xg, conv_w, conv_b, wg, bg, lam, s0, emit_state):
    B, T, _ = xg.shape
    has_s0 = s0 is not None
    n_rows = N_SEG * _seg_len(T)
    CB = RG_SLABS
    wide = CB * LANES
    n_steps = H_C // CB
    in_specs = [
        pl.BlockSpec((1, T, wide), lambda b, c: (b, 0, c)),
        pl.BlockSpec((1, T, wide), lambda b, c: (b, 0, n_steps + c)),
        pl.BlockSpec((4, wide), lambda b, c: (0, c)),
        pl.BlockSpec((1, wide), lambda b, c: (0, c)),
        pl.BlockSpec((CB, BW_C, 4 * BW_C), lambda b, c: (c, 0, 0)),
        pl.BlockSpec((CB, 1, 4 * BW_C), lambda b, c: (c, 0, 0)),
        pl.BlockSpec((2, wide), lambda b, c: (0, c)),
    ]
    args = [xg, xg, conv_w, conv_b, wg, bg, lam]
    if has_s0:
        in_specs.append(pl.BlockSpec((1, 2, wide), lambda b, c: (b, 0, c)))
        args.append(s0)
    out_shape = [jax.ShapeDtypeStruct((B, T, W_C), BF16)]
    out_specs = [pl.BlockSpec((1, T, wide), lambda b, c: (b, 0, c))]
    if emit_state:
        out_shape.append(jax.ShapeDtypeStruct((B, 2, W_C), F32))
        out_specs.append(pl.BlockSpec((1, 2, wide), lambda b, c: (b, 0, c)))
    res = pl.pallas_call(
        functools.partial(_rglru_kernel, seq=T, has_s0=has_s0, emit_state=emit_state),
        out_shape=out_shape,
        grid=(B, n_steps),
        in_specs=in_specs,
        out_specs=out_specs,
        scratch_shapes=[pltpu.VMEM((T + 16, wide), F32)] + [pltpu.VMEM((2, CB, n_rows, LANES), F32)] * 4,
        compiler_params=_cparams(2),
        name="rglru",
    )(*args)
    return res if emit_state else (res[0], None)


A_COLS = 5 * H_A * DK_A
B_COLS = 4 * H_B * DH_B


def kernel(x_prompt, x_sample, state_hgrn, cache_na_k, cache_na_v, state_rglru, c, c_ctx, norm_gain, w_mod, b_mod, w_in_even, w_out_even, hgrn_lb_logits, hgrn_out_gain, na_rel_bias, w_in_odd, w_out_odd, conv_w, conv_b, rg_gate_w, rg_gate_b, rg_lambda, final_gain):
    n_ctx = x_prompt.shape[0]
    n_lat = x_sample.shape[0]
    depth = w_mod.shape[0]

    cond = jnp.zeros((16, D_MODEL), F32).at[0].set(c_ctx).at[1:1 + n_lat].set(c)
    mod = _modulation(cond, w_mod, b_mod.reshape(depth, 1, 3 * D_MODEL))
    mod = mod.reshape(depth, 16, 3, D_MODEL)

    t_ctx = x_prompt.shape[1]

    def flat(a):
        return a.reshape(1, n_ctx * t_ctx, a.shape[-1])

    def unflat(a):
        return a.reshape(n_ctx, t_ctx, a.shape[-1])

    def in_proj_params(l):
        if l % 2 == 0:
            outs_s = ((0, A_COLS, F32), (A_COLS, B_COLS, BF16))
            outs_c = outs_s + ((A_COLS + H_B * DH_B, 2 * H_B * DH_B, F32),)
            return w_in_even[l // 2].astype(BF16), outs_c, outs_s
        outs = ((0, 2 * W_C, F32),)
        return w_in_odd[l // 2].astype(BF16), outs, outs

    xc, xs = x_prompt, x_sample
    new_hgrn, new_k, new_v, new_rg = [], [], [], []
    proj_c = proj_s = None
    for l in range(depth):
        j = l // 2
        mod_c, mod_s = mod[l, 0:1], mod[l, 1:1 + n_lat]
        if proj_c is None:
            gain = norm_gain[l].reshape(1, D_MODEL)
            w_in, outs_c, outs_s = in_proj_params(l)
            proj_c = [unflat(t) for t in _inproj(flat(xc), mod_c, gain, w_in, outs_c, 512, True)]
            proj_s = _inproj(xs, mod_s, gain, w_in, outs_s, 512, False)
        if l % 2 == 0:
            w_out = w_out_even[j].astype(BF16)
            (ya_c, yb_c, kv_c), (ya_s, yb_s) = proj_c, proj_s
            hgain = hgrn_out_gain[j].reshape(H_A, 1, DK_A)
            oa_c, s_fin = _hgrn(ya_c, hgrn_lb_logits, j, hgain, None, True)
            oa_s, _ = _hgrn(ya_s, hgrn_lb_logits, j, hgain, state_hgrn[:, j], False)
            ob_c, k_c, v_c = _ctx_attn(yb_c, kv_c)
            ob_s = _nat(yb_s, cache_na_k[:, j], cache_na_v[:, j], na_rel_bias[j])
            ys_c, ys_s = (oa_c, ob_c), (oa_s, ob_s)
            new_hgrn.append(s_fin)
            new_k.append(k_c)
            new_v.append(v_c)
        else:
            w_out = w_out_odd[j].astype(BF16)
            (xg_c,), (xg_s,) = proj_c, proj_s
            wg = rg_gate_w[j].transpose(2, 3, 0, 1, 4).reshape(H_C, BW_C, 4 * BW_C).astype(BF16)
            bg = rg_gate_b[j].reshape(2, 2, H_C, BW_C).transpose(2, 0, 1, 3).reshape(H_C, 1, 4 * BW_C)
            cb = conv_b[j].reshape(1, W_C)
            y_c, h_fin = _rglru(xg_c, conv_w[j], cb, wg, bg, rg_lambda[j], None, True)
            y_s, _ = _rglru(xg_s, conv_w[j], cb, wg, bg, rg_lambda[j], state_rglru[:, j], False)
            ys_c, ys_s = (y_c,), (y_s,)
            new_rg.append(h_fin)
        ys_c = tuple(flat(y) for y in ys_c)
        if l == depth - 1:
            fgain = final_gain.reshape(1, D_MODEL)
            (xc,) = _outproj(ys_c, flat(xc), mod_c, w_out, 1024, True, final_gain=fgain)
            (xs,) = _outproj(ys_s, xs, mod_s, w_out, 1024, False, final_gain=fgain)
            xc = unflat(xc)
        else:
            gain_n = norm_gain[l + 1].reshape(1, D_MODEL)
            w_n, outs_c, outs_s = in_proj_params(l + 1)
            mod_cn, mod_sn = mod[l + 1, 0:1], mod[l + 1, 1:1 + n_lat]
            xc, *proj_c = _outproj(ys_c, flat(xc), mod_c, w_out, 512, True,
                                   next_proj=(mod_cn, gain_n, w_n, outs_c))
            xs, *proj_s = _outproj(ys_s, xs, mod_s, w_out, 512, False,
                                   next_proj=(mod_sn, gain_n, w_n, outs_s))
            xc = unflat(xc)
            proj_c = [unflat(t) for t in proj_c]
    return (xc, xs, jnp.stack(new_hgrn, axis=1), jnp.stack(new_k, axis=1),
            jnp.stack(new_v, axis=1), jnp.stack(new_rg, axis=1))
```

```python
import functools

import jax
import jax.numpy as jnp
from jax import lax
from jax.experimental import pallas as pl
from jax.experimental.pallas import tpu as pltpu

F32 = jnp.float32
BF16 = jnp.bfloat16

D_MODEL = 1024
EPS = 1e-6
NEG_INF = -1e30
H_A = 4
DK_A = 128
HGRN_CHUNK = 32
HGRN_ROWS = 256
H_B = 8
DH_B = 64
GRID_W = 64
NA_KH = 8
NA_KW = 16
NA_GROUP = 8
W_C = 1024
H_C = 8
BW_C = W_C // H_C
RG_C = 8.0
RG_ROWS = 256
RG_SLABS = 2
N_SEG = 16
LANES = 128
VMEM_LIMIT = 48 * 1024 * 1024

NT_DIMS = (((1,), (1,)), ((), ()))
TN_DIMS = (((0,), (0,)), ((), ()))


def _silu(x):
    half = 0.5 * x
    return half + half * jnp.tanh(half)


def _cparams(n_axes):
    return pltpu.CompilerParams(dimension_semantics=("arbitrary",) * n_axes,
                                vmem_limit_bytes=VMEM_LIMIT)


def _mod_kernel(cond_ref, w_ref, b_ref, o_ref):
    s = _silu(cond_ref[...])
    o_ref[0] = jnp.dot(s.astype(BF16), w_ref[0].astype(BF16), preferred_element_type=F32) + b_ref[0]


def _modulation(cond, w_mod, b_mod):
    depth = w_mod.shape[0]
    n_rows = cond.shape[0]
    return pl.pallas_call(
        _mod_kernel,
        out_shape=jax.ShapeDtypeStruct((depth, n_rows, 3 * D_MODEL), F32),
        grid=(depth, 3),
        in_specs=[
            pl.BlockSpec((n_rows, D_MODEL), lambda l, n: (0, 0)),
            pl.BlockSpec((1, D_MODEL, D_MODEL), lambda l, n: (l, 0, n)),
            pl.BlockSpec((1, 1, D_MODEL), lambda l, n: (l, 0, n)),
        ],
        out_specs=pl.BlockSpec((1, n_rows, D_MODEL), lambda l, n: (l, 0, n)),
        compiler_params=_cparams(2),
        name="adaln_mod",
    )(cond, w_mod, b_mod)


def _project(x, mod_ref, gain_ref, w_ref, out_refs, outs):
    var = jnp.mean(x * x, axis=-1, keepdims=True)
    y = x * lax.rsqrt(var + EPS) * gain_ref[...]
    h = y * (1.0 + mod_ref[0, 1:2, :]) + mod_ref[0, 0:1, :]
    hb = h.astype(BF16)
    step = 512
    for o_ref, (col0, width, _) in zip(out_refs, outs):
        for c in range(0, width, step):
            r = jnp.dot(hb, w_ref[:, col0 + c:col0 + c + step], preferred_element_type=F32)
            o_ref[0, :, c:c + step] = r.astype(o_ref.dtype)


def _inproj_kernel(x_ref, mod_ref, gain_ref, w_ref, *out_refs, outs):
    _project(x_ref[0], mod_ref, gain_ref, w_ref, out_refs, outs)


def _inproj(x, mod, gain, w, outs, tm, shared_mod):
    B, T, _ = x.shape
    n_cols = w.shape[1]
    mod_map = (lambda b, t: (0, 0, 0)) if shared_mod else (lambda b, t: (b, 0, 0))
    return pl.pallas_call(
        functools.partial(_inproj_kernel, outs=outs),
        out_shape=[jax.ShapeDtypeStruct((B, T, wd), dt) for _, wd, dt in outs],
        grid=(B, T // tm),
        in_specs=[
            pl.BlockSpec((1, tm, D_MODEL), lambda b, t: (b, t, 0)),
            pl.BlockSpec((1, 3, D_MODEL), mod_map),
            pl.BlockSpec((1, D_MODEL), lambda b, t: (0, 0)),
            pl.BlockSpec((D_MODEL, n_cols), lambda b, t: (0, 0)),
        ],
        out_specs=[pl.BlockSpec((1, tm, wd), lambda b, t: (b, t, 0)) for _, wd, _ in outs],
        compiler_params=_cparams(2),
        name="in_proj",
    )(x, mod, gain, w)


def _outproj_kernel(*refs, n_y, final, next_outs):
    y_refs, (x_ref, mod_ref, w_ref), rest = refs[:n_y], refs[n_y:n_y + 3], refs[n_y + 3:]
    m = None
    row = 0
    for y_ref in y_refs:
        width = y_ref.shape[-1]
        part = jnp.dot(y_ref[0], w_ref[row:row + width, :], preferred_element_type=F32)
        m = part if m is None else m + part
        row += width
    xn = x_ref[0] + mod_ref[0, 2:3, :] * m
    if final:
        gain_ref, o_ref = rest
        var = jnp.mean(xn * xn, axis=-1, keepdims=True)
        o_ref[0] = xn * lax.rsqrt(var + EPS) * gain_ref[...]
    else:
        modn_ref, gainn_ref, wn_ref, o_ref = rest[:4]
        o_ref[0] = xn
        _project(xn, modn_ref, gainn_ref, wn_ref, rest[4:], next_outs)


def _outproj(ys, x, mod, w, tm, shared_mod, final_gain=None, next_proj=None):
    B, T, _ = x.shape
    final = final_gain is not None
    mod_map = (lambda b, t: (0, 0, 0)) if shared_mod else (lambda b, t: (b, 0, 0))
    row_block = pl.BlockSpec((1, tm, D_MODEL), lambda b, t: (b, t, 0))
    vec = pl.BlockSpec((1, D_MODEL), lambda b, t: (0, 0))
    in_specs = [pl.BlockSpec((1, tm, y.shape[-1]), lambda b, t: (b, t, 0)) for y in ys] + [
        row_block,
        pl.BlockSpec((1, 3, D_MODEL), mod_map),
        pl.BlockSpec((w.shape[0], D_MODEL), lambda b, t: (0, 0)),
    ]
    args = list(ys) + [x, mod, w]
    out_shape = [jax.ShapeDtypeStruct((B, T, D_MODEL), F32)]
    out_specs = [row_block]
    next_outs = None
    if final:
        in_specs.append(vec)
        args.append(final_gain)
    else:
        mod_n, gain_n, w_n, next_outs = next_proj
        in_specs += [pl.BlockSpec((1, 3, D_MODEL), mod_map), vec,
                     pl.BlockSpec((D_MODEL, w_n.shape[1]), lambda b, t: (0, 0))]
        args += [mod_n, gain_n, w_n]
        out_shape += [jax.ShapeDtypeStruct((B, T, wd), dt) for _, wd, dt in next_outs]
        out_specs += [pl.BlockSpec((1, tm, wd), lambda b, t: (b, t, 0)) for _, wd, _ in next_outs]
    return pl.pallas_call(
        functools.partial(_outproj_kernel, n_y=len(ys), final=final, next_outs=next_outs),
        out_shape=out_shape,
        grid=(B, T // tm),
        in_specs=in_specs,
        out_specs=out_specs,
        compiler_params=_cparams(2),
        name="out_proj",
    )(*args)


def _hgrn_kernel(q_ref, zf_ref, zb_ref, v_ref, g_ref, lgt_ref, gain_ref, *rest, seq, layer, has_s0, emit_state):
    rest = list(rest)
    s0_ref = rest.pop(0) if has_s0 else None
    o_ref = rest.pop(0)
    sfin_ref = rest.pop(0) if emit_state else None
    acc_ref, qd_ref, ki_ref, kd_ref, kv_ref, st_ref, dec_ref, mst_ref, msk_ref, mexp_ref = rest
    R = HGRN_ROWS
    C = HGRN_CHUNK
    n_blk = seq // R
    n_chunk = R // C
    n_all = seq // C

    @pl.when((pl.program_id(0) == 0) & (pl.program_id(1) == 0))
    def _build_masks():
        ti = lax.broadcasted_iota(jnp.int32, (R, R), 0)
        tj = lax.broadcasted_iota(jnp.int32, (R, R), 1)
        shift = C.bit_length() - 1
        same = lax.shift_right_logical(ti, shift) == lax.shift_right_logical(tj, shift)
        one = jnp.ones((R, R), F32)
        zero = jnp.zeros((R, R), F32)
        incl = (jnp.where(same, jnp.where(tj <= ti, one, zero), zero),
                jnp.where(same, jnp.where(tj >= ti, one, zero), zero))
        for d in range(2):
            msk_ref[d] = incl[d]
            mst_ref[d] = incl[d].astype(BF16)
        rr = lax.broadcasted_iota(jnp.int32, (R, n_chunk * LANES), 0)
        cc = lax.broadcasted_iota(jnp.int32, (R, n_chunk * LANES), 1)
        own = lax.shift_right_logical(rr, shift) == lax.shift_right_logical(cc, LANES.bit_length() - 1)
        mexp_ref[...] = jnp.where(own, 1.0, 0.0).astype(BF16)

    lgt = [lgt_ref[:, i, :] for i in range(lgt_ref.shape[1])]
    lmax = functools.reduce(jnp.maximum, lgt)
    ex = [jnp.exp(t - lmax) for t in lgt]
    lb_all = sum(ex[:layer + 1]) / sum(ex)
    gain = gain_ref[0]

    blocks_per_trip = 2 if n_blk % 2 == 0 else 1

    def gates(i, _):
        for u in range(blocks_per_trip):
            blk = i * blocks_per_trip + u
            rows = pl.ds(pl.multiple_of(blk * R, R), R)
            q = q_ref[0, rows, :]
            for d in range(2):
                z = (zf_ref if d == 0 else zb_ref)[0, rows, :]
                lb = lb_all[d:d + 1, :]
                oml = 1.0 - lb
                e = jnp.exp(-jnp.abs(z))
                r = 1.0 / (1.0 + e)
                er = e * r
                pos = z >= 0.0
                f = lb + oml * jnp.where(pos, r, er)
                k = oml * jnp.where(pos, er, r)
                logf = jnp.log(f)
                hi = logf.astype(BF16)
                lo = (logf - hi.astype(F32)).astype(BF16)
                cs = jnp.dot(mst_ref[d], jnp.concatenate([hi, lo], axis=1), preferred_element_type=F32)
                b = cs[:, 0:LANES] + cs[:, LANES:2 * LANES]
                ends = [c * C + (C - 1 if d == 0 else 0) for c in range(n_chunk)]
                btot = jnp.concatenate([jnp.broadcast_to(b[t:t + 1, :], (C, LANES)) for t in ends], axis=0)
                qd_ref[d, rows, :] = (q * jnp.exp(b)).astype(BF16)
                ki_ref[d, rows, :] = (k * jnp.exp(-b)).astype(BF16)
                kd_ref[d, rows, :] = (k * jnp.exp(btot - b)).astype(BF16)
                for c in range(n_chunk):
                    dec_ref[d, blk * n_chunk + c] = jnp.exp(btot[c * C:c * C + 8, :])
        return 0

    lax.fori_loop(0, n_blk // blocks_per_trip, gates, 0)

    def intra(i, _):
        for u in range(blocks_per_trip):
            blk = i * blocks_per_trip + u
            rows = pl.ds(pl.multiple_of(blk * R, R), R)
            v = v_ref[0, rows, :]
            vb = v.astype(BF16)
            vt = v.T.astype(BF16)
            for d in range(2):
                att = lax.dot_general(qd_ref[d, rows, :], ki_ref[d, rows, :], NT_DIMS,
                                      preferred_element_type=F32)
                att = jnp.where(msk_ref[d] > 0.5, att, 0.0)
                acc_ref[d, rows, :] = jnp.dot(att.astype(BF16), vb, preferred_element_type=F32)
                kd_exp = jnp.concatenate([kd_ref[d, rows, :]] * n_chunk, axis=1) * mexp_ref[...]
                kv_all = jnp.dot(vt, kd_exp, preferred_element_type=F32)
                for c in range(n_chunk):
                    kv_ref[d, blk * n_chunk + c] = kv_all[:, c * LANES:(c + 1) * LANES]
        return 0

    lax.fori_loop(0, n_blk // blocks_per_trip, intra, 0)

    unroll = 4

    def states(i, sts):
        sts = list(sts)
        for u in range(unroll):
            n = i * unroll + u
            for d in range(2):
                c = n if d == 0 else n_all - 1 - n
                st_ref[d, c] = sts[d].astype(BF16)
                dec = jnp.concatenate([dec_ref[d, c]] * (DK_A // 8), axis=0)
                sts[d] = sts[d] * dec + kv_ref[d, c]
        return tuple(sts)

    if has_s0:
        st0 = (s0_ref[0, 0, 0].T, s0_ref[0, 1, 0].T)
    else:
        st0 = (jnp.zeros((DK_A, DK_A), F32),) * 2
    sts = lax.fori_loop(0, n_all // unroll, states, st0)
    if emit_state:
        for d in range(2):
            sfin_ref[0, d, 0] = sts[d].T

    def finish(blk, _):
        rows = pl.ds(pl.multiple_of(blk * R, R), R)
        tot = acc_ref[0, rows, :] + acc_ref[1, rows, :]
        for d in range(2):
            pieces = []
            for c in range(n_chunk):
                crow = pl.ds(pl.multiple_of(blk * R + c * C, C), C)
                pieces.append(lax.dot_general(qd_ref[d, crow, :], st_ref[d, blk * n_chunk + c], NT_DIMS,
                                              preferred_element_type=F32))
            tot = tot + jnp.concatenate(pieces, axis=0)
        var = jnp.mean(tot * tot, axis=-1, keepdims=True)
        y = tot * lax.rsqrt(var + EPS) * gain
        o_ref[0, rows, :] = (y * _silu(g_ref[0, rows, :])).astype(o_ref.dtype)
        return 0

    lax.fori_loop(0, n_blk, finish, 0)


def _hgrn(ya, lgt, layer, gain, s0, emit_state):
    B, T, _ = ya.shape
    has_s0 = s0 is not None

    def col(k):
        return pl.BlockSpec((1, T, LANES), lambda b, h, k=k: (b, 0, k * H_A + h))

    in_specs = [col(0), col(1), col(2), col(3), col(4),
                pl.BlockSpec((2, lgt.shape[1], LANES), lambda b, h: (0, 0, h)),
                pl.BlockSpec((1, 1, LANES), lambda b, h: (h, 0, 0))]
    args = [ya, ya, ya, ya, ya, lgt, gain]
    if has_s0:
        in_specs.append(pl.BlockSpec((1, 2, 1, DK_A, DK_A), lambda b, h: (b, 0, h, 0, 0)))
        args.append(s0)
    out_shape = [jax.ShapeDtypeStruct((B, T, H_A * DK_A), BF16)]
    out_specs = [pl.BlockSpec((1, T, LANES), lambda b, h: (b, 0, h))]
    if emit_state:
        out_shape.append(jax.ShapeDtypeStruct((B, 2, H_A, DK_A, DK_A), F32))
        out_specs.append(pl.BlockSpec((1, 2, 1, DK_A, DK_A), lambda b, h: (b, 0, h, 0, 0)))
    res = pl.pallas_call(
        functools.partial(_hgrn_kernel, seq=T, layer=layer, has_s0=has_s0, emit_state=emit_state),
        out_shape=out_shape,
        grid=(B, H_A),
        in_specs=in_specs,
        out_specs=out_specs,
        scratch_shapes=[pltpu.VMEM((2, T, LANES), F32),
                        pltpu.VMEM((2, T, LANES), BF16),
                        pltpu.VMEM((2, T, LANES), BF16),
                        pltpu.VMEM((2, T, LANES), BF16),
                        pltpu.VMEM((2, T // HGRN_CHUNK, DK_A, DK_A), F32),
                        pltpu.VMEM((2, T // HGRN_CHUNK, DK_A, DK_A), BF16),
                        pltpu.VMEM((2, T // HGRN_CHUNK, 8, LANES), F32),
                        pltpu.VMEM((2, HGRN_ROWS, HGRN_ROWS), BF16),
                        pltpu.VMEM((2, HGRN_ROWS, HGRN_ROWS), F32),
                        pltpu.VMEM((HGRN_ROWS, HGRN_ROWS // HGRN_CHUNK * LANES), BF16)],
        compiler_params=_cparams(2),
        name="hgrn2",
    )(*args)
    return res if emit_state else (res[0], None)


def _head_masks():
    lane = lax.broadcasted_iota(jnp.int32, (1, LANES), 1)
    return lane < DH_B, lane >= DH_B


def _ctx_attn_kernel(q_ref, k_ref, v_ref, g_ref, kv_ref, o_ref, newk_ref, newv_ref):
    scale = DH_B ** -0.5
    masks = _head_masks()
    T = q_ref.shape[1]
    for h in range(H_B):
        newk_ref[0, h] = kv_ref[0, :, h * DH_B:(h + 1) * DH_B]
        newv_ref[0, h] = kv_ref[0, :, (H_B + h) * DH_B:(H_B + h + 1) * DH_B]
    for p in range(H_B // 2):
        cols = slice(p * LANES, (p + 1) * LANES)
        q = q_ref[0, :, cols] * scale
        qs = jnp.concatenate([jnp.where(masks[h], q, jnp.zeros_like(q)) for h in range(2)], axis=0)
        s = lax.dot_general(qs, k_ref[0, :, cols], NT_DIMS, preferred_element_type=F32)
        e = jnp.exp(s - jnp.max(s, axis=-1, keepdims=True))
        pr = e / jnp.sum(e, axis=-1, keepdims=True)
        o = jnp.dot(pr.astype(BF16), v_ref[0, :, cols], preferred_element_type=F32)
        o = jnp.where(masks[0], o[0:T], o[T:2 * T])
        o_ref[0, :, cols] = (o * _silu(g_ref[0, :, cols].astype(F32))).astype(o_ref.dtype)


def _ctx_attn(yb, kv):
    B, T, _ = yb.shape
    width = H_B * DH_B

    def col(k):
        return pl.BlockSpec((1, T, width), lambda b, k=k: (b, 0, k))

    cache = pl.BlockSpec((1, H_B, T, DH_B), lambda b: (b, 0, 0, 0))
    return pl.pallas_call(
        _ctx_attn_kernel,
        out_shape=[jax.ShapeDtypeStruct((B, T, width), BF16),
                   jax.ShapeDtypeStruct((B, H_B, T, DH_B), F32),
                   jax.ShapeDtypeStruct((B, H_B, T, DH_B), F32)],
        grid=(B,),
        in_specs=[col(0), col(1), col(2), col(3), pl.BlockSpec((1, T, 2 * width), lambda b: (b, 0, 0))],
        out_specs=[pl.BlockSpec((1, T, width), lambda b: (b, 0, 0)), cache, cache],
        compiler_params=_cparams(1),
        name="ctx_attn",
    )(yb, yb, yb, yb, kv)


N_DR = 2 * NA_KH - 1
N_DC = 2 * NA_KW - 1
N_TAB = N_DR - 1


def _nat_kernel(rb_ref, q_ref, k_ref, v_ref, g_ref, kc_ref, vc_ref, o_ref,
                tab_ref, qs_ref, s_ref, p_ref, r_ref, oc_ref, *, rows):
    p = pl.program_id(0)
    scale = DH_B ** -0.5
    kh = min(NA_KH, rows)
    masks = _head_masks()

    @pl.when(pl.program_id(1) == 0)
    def _build_tables():
        c = lax.broadcasted_iota(jnp.int32, (GRID_W, LANES), 0)
        lane = lax.broadcasted_iota(jnp.int32, (GRID_W, LANES), 1)
        kcol = lane & (GRID_W - 1)
        upper = lane >= GRID_W
        ws = jnp.clip(c - NA_KW // 2, 0, GRID_W - NA_KW)
        neg = jnp.full((GRID_W, LANES), NEG_INF, F32)
        diag = kcol - c + (NA_KW - 1)
        for h in range(2):
            base = (2 * p + h) * (N_DR * N_DC)

            def per_dr(i, _, base=base, h=h):
                def per_dc(dd, acc):
                    lo = rb_ref[base + i * N_DC + dd]
                    hi = rb_ref[base + (i + 1) * N_DC + dd]
                    return jnp.where(diag == dd, jnp.where(upper, hi, lo), acc)

                acc = lax.fori_loop(0, N_DC, per_dc, neg)
                acc = jnp.where(kcol >= ws, jnp.where(kcol < ws + NA_KW, acc, neg), neg)
                tab_ref[h, i] = acc
                return 0

            lax.fori_loop(0, N_TAB, per_dr, 0)

    kc = jnp.concatenate([kc_ref[0, 0], kc_ref[0, 1]], axis=1).astype(BF16)
    vc = jnp.concatenate([vc_ref[0, 0], vc_ref[0, 1]], axis=1).astype(BF16)
    n_keys = kh * GRID_W
    n_ctx = kc.shape[0]
    G = NA_GROUP
    W2 = 2 * GRID_W

    def group(gi, _):
        r_first = gi * G
        q0 = pl.multiple_of(r_first * GRID_W, G * GRID_W)
        for i in range(G):
            qi = q_ref[0, pl.ds(q0 + i * GRID_W, GRID_W), :] * scale
            for h in range(2):
                qs_ref[i * W2 + h * GRID_W:i * W2 + (h + 1) * GRID_W, :] = jnp.where(
                    masks[h], qi, jnp.zeros_like(qi))
        s_ref[:, n_keys:n_keys + n_ctx] = lax.dot_general(qs_ref[...], kc, NT_DIMS,
                                                          preferred_element_type=F32)
        windows = []
        for i in range(G):
            r = r_first + i
            rs = jnp.clip(r - kh // 2, 0, rows - kh)
            k0 = pl.multiple_of(rs * GRID_W, GRID_W)
            windows.append(k0)
            dr0 = rs - r + (NA_KH - 1)
            bias = jnp.concatenate(
                [jnp.concatenate([tab_ref[h, dr0 + 2 * m] for m in range(kh // 2)], axis=1)
                 for h in range(2)], axis=0)
            s_ref[i * W2:(i + 1) * W2, 0:n_keys] = lax.dot_general(
                qs_ref[i * W2:(i + 1) * W2, :], k_ref[0, pl.ds(k0, n_keys), :], NT_DIMS,
                preferred_element_type=F32) + bias
        for i in range(G):
            s = s_ref[i * W2:(i + 1) * W2, :]
            e = jnp.exp(s - jnp.max(s, axis=-1, keepdims=True))
            p_ref[i * W2:(i + 1) * W2, :] = e.astype(BF16)
            rinv = 1.0 / jnp.sum(e, axis=-1, keepdims=True)
            r_ref[i * W2:(i + 1) * W2, :] = jnp.broadcast_to(rinv, (W2, LANES))
        oc_ref[...] = jnp.dot(p_ref[:, n_keys:n_keys + n_ctx], vc, preferred_element_type=F32)
        for i in range(G):
            o = jnp.dot(p_ref[i * W2:(i + 1) * W2, 0:n_keys], v_ref[0, pl.ds(windows[i], n_keys), :],
                        preferred_element_type=F32)
            o = (o + oc_ref[i * W2:(i + 1) * W2, :]) * r_ref[i * W2:(i + 1) * W2, :]
            o = jnp.where(masks[0], o[0:GRID_W], o[GRID_W:W2])
            out_rows = pl.ds(q0 + i * GRID_W, GRID_W)
            gate = g_ref[0, out_rows, :].astype(F32)
            o_ref[0, out_rows, :] = (o * _silu(gate)).astype(o_ref.dtype)
        return 0

    lax.fori_loop(0, rows // G, group, 0)


def _nat(yb, kc, vc, rel_bias):
    B, T, _ = yb.shape
    Tc = kc.shape[2]
    n_pair = H_B // 2
    rows = T // GRID_W
    n_stack = NA_GROUP * 2 * GRID_W
    n_keys = min(NA_KH, rows) * GRID_W

    def col(k):
        return pl.BlockSpec((1, T, LANES), lambda p, b, k=k: (b, 0, k * n_pair + p))

    ctx = pl.BlockSpec((1, 2, Tc, DH_B), lambda p, b: (b, p, 0, 0))
    return pl.pallas_call(
        functools.partial(_nat_kernel, rows=rows),
        out_shape=jax.ShapeDtypeStruct((B, T, H_B * DH_B), BF16),
        grid=(n_pair, B),
        in_specs=[pl.BlockSpec(memory_space=pltpu.SMEM), col(0), col(1), col(2), col(3), ctx, ctx],
        out_specs=pl.BlockSpec((1, T, LANES), lambda p, b: (b, 0, p)),
        scratch_shapes=[pltpu.VMEM((2, N_TAB, GRID_W, LANES), F32),
                        pltpu.VMEM((n_stack, LANES), BF16),
                        pltpu.VMEM((n_stack, n_keys + Tc), F32),
                        pltpu.VMEM((n_stack, n_keys + Tc), BF16),
                        pltpu.VMEM((n_stack, LANES), F32),
                        pltpu.VMEM((n_stack, LANES), F32)],
        compiler_params=_cparams(2),
        name="nbr_attn",
    )(rel_bias.reshape(-1), yb, yb, yb, yb, kc, vc)


def _seg_len(seq):
    length = -(-seq // N_SEG)
    while length % 8 != 4:
        length += 1
    return length


def _step_block(seg_len):
    return max(d for d in range(1, seg_len + 1) if seg_len % d == 0 and d * N_SEG <= RG_ROWS * 3 // 2)


def _rglru_kernel(x_ref, g_ref, cw_ref, cb_ref, wg_ref, lam_ref, *rest, seq, has_s0, emit_state):
    rest = list(rest)
    s0_ref = rest.pop(0) if has_s0 else None
    o_ref = rest.pop(0)
    hfin_ref = rest.pop(0) if emit_state else None
    xpad_ref, a_ref, u_ref, h_ref, p_ref, hn_ref = rest
    L = _seg_len(seq)
    n_rows = N_SEG * L
    RB = RG_ROWS
    CB = RG_SLABS
    TB = _step_block(L)
    n_tile = N_SEG // 8
    chains = [(d, j, s) for d in range(2) for j in range(CB) for s in range(n_tile)]

    for j in range(CB):
        xpad_ref[j, 0:8, :] = jnp.zeros((8, LANES), F32)
        xpad_ref[j, 8:seq + 8, :] = x_ref[0, :, j * LANES:(j + 1) * LANES]
        xpad_ref[j, seq + 8:n_rows + 16, :] = jnp.zeros((n_rows + 8 - seq, LANES), F32)

    nl = -lam_ref[...]
    sp = jnp.maximum(nl, 0.0) + jnp.log1p(jnp.exp(-jnp.abs(nl)))
    cw = cw_ref[...]
    cbias = cb_ref[...]
    ones2 = jnp.where(lax.broadcasted_iota(jnp.int32, (TB * N_SEG, LANES), 1) < 2, 1.0, 0.0).astype(BF16)

    def gates(blk, _):
        t0 = blk * TB
        r0 = pl.multiple_of(blk * (TB * N_SEG), TB * N_SEG)
        for j in range(CB):
            lanes = slice(j * LANES, (j + 1) * LANES)
            tiles = []
            for tt in range(TB):
                for s in range(n_tile):
                    taps = [xpad_ref[j, pl.ds(6 + k + t0 + tt + s * 8 * L, 8, stride=L), :] for k in range(4)]
                    xt = cw[0:1, lanes] * taps[0] + cw[1:2, lanes] * taps[1]
                    xt = xt + cw[2:3, lanes] * taps[2]
                    tiles.append(xt + cw[3:4, lanes] * taps[3] + cbias[:, lanes])
            xj = jnp.concatenate(tiles, axis=0)
            gt = jnp.dot(jnp.concatenate([xj.astype(BF16), ones2], axis=1), wg_ref[j],
                         preferred_element_type=F32)
            xh = 0.5 * xj
            for d in range(2):
                th_r = jnp.tanh(gt[:, (2 * d) * LANES:(2 * d + 1) * LANES])
                th_i = jnp.tanh(gt[:, (2 * d + 1) * LANES:(2 * d + 2) * LANES])
                half = (-0.5 * RG_C) * sp[d:d + 1, lanes]
                la = half + half * th_r
                a = jnp.exp(la)
                y = -jnp.tanh(la) * (1.0 + a * a)
                root = jnp.where(y > 0.0, y * lax.rsqrt(y), 0.0)
                a_ref[d, j, pl.ds(r0, TB * N_SEG), :] = a
                u_ref[d, j, pl.ds(r0, TB * N_SEG), :] = root * (xh + xh * th_i)
        return 0

    lax.fori_loop(0, L // TB, gates, 0)

    first_pad = [[min(max(seq - (s * 8 + r) * L, 0), L) for r in range(8)] for s in range(n_tile)]
    sub = lax.broadcasted_iota(jnp.int32, (8, LANES), 0)
    pad_from = []
    for s in range(n_tile):
        if all(f == L for f in first_pad[s]):
            pad_from.append(None)
        else:
            vec = jnp.full((8, LANES), L, jnp.int32)
            for r in range(8):
                vec = jnp.where(sub == r, first_pad[s][r], vec)
            pad_from.append(vec)

    def step_rows(step, s):
        return pl.ds(pl.multiple_of(step * N_SEG + s * 8, 8), 8)

    unroll = 4

    def scan(i, carry):
        carry = list(carry)
        for k in range(unroll):
            for n, (d, j, s) in enumerate(chains):
                h, pr = carry[n]
                t = i * unroll + k
                if d == 1:
                    t = L - 1 - t
                idx = step_rows(t, s)
                a = a_ref[d, j, idx, :]
                u = u_ref[d, j, idx, :]
                if pad_from[s] is not None:
                    live = t < pad_from[s]
                    a = jnp.where(live, a, 1.0)
                    u = jnp.where(live, u, 0.0)
                h = a * h + u
                pr = pr * a
                h_ref[d, j, idx, :] = h
                p_ref[d, j, idx, :] = pr
                carry[n] = (h, pr)
        return tuple(carry)

    zero = jnp.zeros((8, LANES), F32)
    one = jnp.ones((8, LANES), F32)
    ends = lax.fori_loop(0, L // unroll, scan, ((zero, one),) * len(chains))

    cins = {}
    finals = [[None] * CB for _ in range(2)]
    for d in range(2):
        for j in range(CB):
            if has_s0:
                c = s0_ref[0, d:d + 1, j * LANES:(j + 1) * LANES]
            else:
                c = jnp.zeros((1, LANES), F32)
            cin = [None] * N_SEG
            for kk in range(N_SEG):
                seg = kk if d == 0 else N_SEG - 1 - kk
                s, row = divmod(seg, 8)
                h_end, p_end = ends[chains.index((d, j, s))]
                cin[seg] = c
                c = h_end[row:row + 1, :] + p_end[row:row + 1, :] * c
            finals[d][j] = c
            for s in range(n_tile):
                cins[(d, j, s)] = jnp.concatenate(cin[s * 8:(s + 1) * 8], axis=0)

    def fix(i, _):
        for k in range(unroll):
            t = i * unroll + k
            for j in range(CB):
                for s in range(n_tile):
                    idx = step_rows(t, s)
                    parts = [h_ref[d, j, idx, :] + p_ref[d, j, idx, :] * cins[(d, j, s)] for d in range(2)]
                    hn_ref[j, pl.ds(t + s * 8 * L, 8, stride=L), :] = parts[0] + parts[1]
        return 0

    lax.fori_loop(0, L // unroll, fix, 0)

    if emit_state:
        hfin_ref[0] = jnp.concatenate([jnp.concatenate(finals[d], axis=1) for d in range(2)], axis=0)

    def combine(blk, _):
        rows = pl.ds(pl.multiple_of(blk * RB, RB), RB)
        hs = jnp.concatenate([hn_ref[j, rows, :] for j in range(CB)], axis=1)
        o_ref[0, rows, :] = (hs * _silu(g_ref[0, rows, :])).astype(o_ref.dtype)
        return 0

    lax.fori_loop(0, seq // RB, combine, 0)


def _gate_weights(gate_w, gate_b):
    w = (0.5 * gate_w).transpose(2, 3, 0, 1, 4).reshape(H_C, BW_C, 4 * BW_C).astype(BF16)
    b = (0.5 * gate_b).reshape(2, 2, H_C, BW_C).transpose(2, 0, 1, 3).reshape(H_C, 1, 4 * BW_C).astype(F32)
    hi = b.astype(BF16)
    lo = (b - hi.astype(F32)).astype(BF16)
    zeros = jnp.zeros((H_C, BW_C - 2, 4 * BW_C), BF16)
    return jnp.concatenate([w, hi, lo, zeros], axis=1)


def _rglru(xg, conv_w, conv_b, wg, lam, s0, emit_state):
    B, T, _ = xg.shape
    has_s0 = s0 is not None
    n_rows = N_SEG * _seg_len(T)
    CB = RG_SLABS
    wide = CB * LANES
    n_steps = H_C // CB
    in_specs = [
        pl.BlockSpec((1, T, wide), lambda b, c: (b, 0, c)),
        pl.BlockSpec((1, T, wide), lambda b, c: (b, 0, n_steps + c)),
        pl.BlockSpec((4, wide), lambda b, c: (0, c)),
        pl.BlockSpec((1, wide), lambda b, c: (0, c)),
        pl.BlockSpec((CB, 2 * BW_C, 4 * BW_C), lambda b, c: (c, 0, 0)),
        pl.BlockSpec((2, wide), lambda b, c: (0, c)),
    ]
    args = [xg, xg, conv_w, conv_b, wg, lam]
    if has_s0:
        in_specs.append(pl.BlockSpec((1, 2, wide), lambda b, c: (b, 0, c)))
        args.append(s0)
    out_shape = [jax.ShapeDtypeStruct((B, T, W_C), BF16)]
    out_specs = [pl.BlockSpec((1, T, wide), lambda b, c: (b, 0, c))]
    if emit_state:
        out_shape.append(jax.ShapeDtypeStruct((B, 2, W_C), F32))
        out_specs.append(pl.BlockSpec((1, 2, wide), lambda b, c: (b, 0, c)))
    res = pl.pallas_call(
        functools.partial(_rglru_kernel, seq=T, has_s0=has_s0, emit_state=emit_state),
        out_shape=out_shape,
        grid=(B, n_steps),
        in_specs=in_specs,
        out_specs=out_specs,
        scratch_shapes=[pltpu.VMEM((CB, n_rows + 16, LANES), F32)]
        + [pltpu.VMEM((2, CB, n_rows, LANES), F32)] * 4 + [pltpu.VMEM((CB, n_rows, LANES), F32)],
        compiler_params=_cparams(2),
        name="rglru",
    )(*args)
    return res if emit_state else (res[0], None)


A_COLS = 5 * H_A * DK_A
B_COLS = 4 * H_B * DH_B


def kernel(x_prompt, x_sample, state_hgrn, cache_na_k, cache_na_v, state_rglru, c, c_ctx, norm_gain, w_mod, b_mod, w_in_even, w_out_even, hgrn_lb_logits, hgrn_out_gain, na_rel_bias, w_in_odd, w_out_odd, conv_w, conv_b, rg_gate_w, rg_gate_b, rg_lambda, final_gain):
    n_ctx = x_prompt.shape[0]
    n_lat = x_sample.shape[0]
    depth = w_mod.shape[0]

    cond = jnp.zeros((16, D_MODEL), F32).at[0].set(c_ctx).at[1:1 + n_lat].set(c)
    mod = _modulation(cond, w_mod, b_mod.reshape(depth, 1, 3 * D_MODEL))
    mod = mod.reshape(depth, 16, 3, D_MODEL)

    t_ctx = x_prompt.shape[1]

    def flat(a):
        return a.reshape(1, n_ctx * t_ctx, a.shape[-1])

    def unflat(a):
        return a.reshape(n_ctx, t_ctx, a.shape[-1])

    def in_proj_params(l):
        if l % 2 == 0:
            outs_s = ((0, A_COLS, F32), (A_COLS, B_COLS, BF16))
            outs_c = outs_s + ((A_COLS + H_B * DH_B, 2 * H_B * DH_B, F32),)
            return w_in_even[l // 2].astype(BF16), outs_c, outs_s
        outs = ((0, 2 * W_C, F32),)
        return w_in_odd[l // 2].astype(BF16), outs, outs

    xc, xs = x_prompt, x_sample
    new_hgrn, new_k, new_v, new_rg = [], [], [], []
    proj_c = proj_s = None
    for l in range(depth):
        j = l // 2
        mod_c, mod_s = mod[l, 0:1], mod[l, 1:1 + n_lat]
        if proj_c is None:
            gain = norm_gain[l].reshape(1, D_MODEL)
            w_in, outs_c, outs_s = in_proj_params(l)
            proj_c = [unflat(t) for t in _inproj(flat(xc), mod_c, gain, w_in, outs_c, 512, True)]
            proj_s = _inproj(xs, mod_s, gain, w_in, outs_s, 512, False)
        if l % 2 == 0:
            w_out = w_out_even[j].astype(BF16)
            (ya_c, yb_c, kv_c), (ya_s, yb_s) = proj_c, proj_s
            hgain = hgrn_out_gain[j].reshape(H_A, 1, DK_A)
            oa_c, s_fin = _hgrn(ya_c, hgrn_lb_logits, j, hgain, None, True)
            oa_s, _ = _hgrn(ya_s, hgrn_lb_logits, j, hgain, state_hgrn[:, j], False)
            ob_c, k_c, v_c = _ctx_attn(yb_c, kv_c)
            ob_s = _nat(yb_s, cache_na_k[:, j], cache_na_v[:, j], na_rel_bias[j])
            ys_c, ys_s = (oa_c, ob_c), (oa_s, ob_s)
            new_hgrn.append(s_fin)
            new_k.append(k_c)
            new_v.append(v_c)
        else:
            w_out = w_out_odd[j].astype(BF16)
            (xg_c,), (xg_s,) = proj_c, proj_s
            wg = _gate_weights(rg_gate_w[j], rg_gate_b[j])
            cb = conv_b[j].reshape(1, W_C)
            y_c, h_fin = _rglru(xg_c, conv_w[j], cb, wg, rg_lambda[j], None, True)
            y_s, _ = _rglru(xg_s, conv_w[j], cb, wg, rg_lambda[j], state_rglru[:, j], False)
            ys_c, ys_s = (y_c,), (y_s,)
            new_rg.append(h_fin)
        ys_c = tuple(flat(y) for y in ys_c)
        if l == depth - 1:
            fgain = final_gain.reshape(1, D_MODEL)
            (xc,) = _outproj(ys_c, flat(xc), mod_c, w_out, 1024, True, final_gain=fgain)
            (xs,) = _outproj(ys_s, xs, mod_s, w_out, 1024, False, final_gain=fgain)
            xc = unflat(xc)
        else:
            gain_n = norm_gain[l + 1].reshape(1, D_MODEL)
            w_n, outs_c, outs_s = in_proj_params(l + 1)
            mod_cn, mod_sn = mod[l + 1, 0:1], mod[l + 1, 1:1 + n_lat]
            xc, *proj_c = _outproj(ys_c, flat(xc), mod_c, w_out, 512, True,
                                   next_proj=(mod_cn, gain_n, w_n, outs_c))
            xs, *proj_s = _outproj(ys_s, xs, mod_s, w_out, 512, False,
                                   next_proj=(mod_sn, gain_n, w_n, outs_s))
            xc = unflat(xc)
            proj_c = [unflat(t) for t in proj_c]
    return (xc, xs, jnp.stack(new_hgrn, axis=1), jnp.stack(new_k, axis=1),
            jnp.stack(new_v, axis=1), jnp.stack(new_rg, axis=1))
```

```python
import functools

import jax
import jax.numpy as jnp
from jax import lax
from jax.experimental import pallas as pl
from jax.experimental.pallas import tpu as pltpu

F32 = jnp.float32
BF16 = jnp.bfloat16

D_MODEL = 1024
EPS = 1e-6
NEG_INF = -1e30
H_A = 4
DK_A = 128
HGRN_CHUNK = 32
HGRN_ROWS = 256
H_B = 8
DH_B = 64
GRID_W = 64
NA_KH = 8
NA_KW = 16
NA_GROUP = 8
W_C = 1024
H_C = 8
BW_C = W_C // H_C
RG_C = 8.0
RG_ROWS = 256
RG_SLABS = 2
N_SEG = 16
LANES = 128
VMEM_LIMIT = 48 * 1024 * 1024

NT_DIMS = (((1,), (1,)), ((), ()))
TN_DIMS = (((0,), (0,)), ((), ()))


def _silu(x):
    half = 0.5 * x
    return half + half * jnp.tanh(half)


def _cparams(n_axes):
    return pltpu.CompilerParams(dimension_semantics=("arbitrary",) * n_axes,
                                vmem_limit_bytes=VMEM_LIMIT)


def _mod_kernel(cond_ref, w_ref, b_ref, o_ref):
    s = _silu(cond_ref[...])
    o_ref[0] = jnp.dot(s.astype(BF16), w_ref[0].astype(BF16), preferred_element_type=F32) + b_ref[0]


def _modulation(cond, w_mod, b_mod):
    depth = w_mod.shape[0]
    n_rows = cond.shape[0]
    return pl.pallas_call(
        _mod_kernel,
        out_shape=jax.ShapeDtypeStruct((depth, n_rows, 3 * D_MODEL), F32),
        grid=(depth, 3),
        in_specs=[
            pl.BlockSpec((n_rows, D_MODEL), lambda l, n: (0, 0)),
            pl.BlockSpec((1, D_MODEL, D_MODEL), lambda l, n: (l, 0, n)),
            pl.BlockSpec((1, 1, D_MODEL), lambda l, n: (l, 0, n)),
        ],
        out_specs=pl.BlockSpec((1, n_rows, D_MODEL), lambda l, n: (l, 0, n)),
        compiler_params=_cparams(2),
        name="adaln_mod",
    )(cond, w_mod, b_mod)


def _project(x, mod_ref, gain_ref, w_ref, out_refs, outs):
    var = jnp.mean(x * x, axis=-1, keepdims=True)
    y = x * lax.rsqrt(var + EPS) * gain_ref[...]
    h = y * (1.0 + mod_ref[0, 1:2, :]) + mod_ref[0, 0:1, :]
    hb = h.astype(BF16)
    step = 512
    for c in range(0, w_ref.shape[1], step):
        users = [(o_ref, c - col0) for o_ref, (col0, width, _) in zip(out_refs, outs)
                 if col0 <= c < col0 + width]
        if users:
            r = jnp.dot(hb, w_ref[:, c:c + step], preferred_element_type=F32)
            for o_ref, off in users:
                o_ref[0, :, off:off + step] = r.astype(o_ref.dtype)


def _inproj_kernel(x_ref, mod_ref, gain_ref, w_ref, *out_refs, outs):
    _project(x_ref[0], mod_ref, gain_ref, w_ref, out_refs, outs)


def _inproj(x, mod, gain, w, outs, tm, shared_mod):
    B, T, _ = x.shape
    n_cols = w.shape[1]
    mod_map = (lambda b, t: (0, 0, 0)) if shared_mod else (lambda b, t: (b, 0, 0))
    return pl.pallas_call(
        functools.partial(_inproj_kernel, outs=outs),
        out_shape=[jax.ShapeDtypeStruct((B, T, wd), dt) for _, wd, dt in outs],
        grid=(B, T // tm),
        in_specs=[
            pl.BlockSpec((1, tm, D_MODEL), lambda b, t: (b, t, 0)),
            pl.BlockSpec((1, 3, D_MODEL), mod_map),
            pl.BlockSpec((1, D_MODEL), lambda b, t: (0, 0)),
            pl.BlockSpec((D_MODEL, n_cols), lambda b, t: (0, 0)),
        ],
        out_specs=[pl.BlockSpec((1, tm, wd), lambda b, t: (b, t, 0)) for _, wd, _ in outs],
        compiler_params=_cparams(2),
        name="in_proj",
    )(x, mod, gain, w)


def _outproj_kernel(*refs, n_y, final, next_outs):
    y_refs, (x_ref, mod_ref, w_ref), rest = refs[:n_y], refs[n_y:n_y + 3], refs[n_y + 3:]
    m = None
    row = 0
    for y_ref in y_refs:
        width = y_ref.shape[-1]
        part = jnp.dot(y_ref[0], w_ref[row:row + width, :], preferred_element_type=F32)
        m = part if m is None else m + part
        row += width
    xn = x_ref[0] + mod_ref[0, 2:3, :] * m
    if final:
        gain_ref, o_ref = rest
        var = jnp.mean(xn * xn, axis=-1, keepdims=True)
        o_ref[0] = xn * lax.rsqrt(var + EPS) * gain_ref[...]
    else:
        modn_ref, gainn_ref, wn_ref, o_ref = rest[:4]
        o_ref[0] = xn
        _project(xn, modn_ref, gainn_ref, wn_ref, rest[4:], next_outs)


def _outproj(ys, x, mod, w, tm, shared_mod, final_gain=None, next_proj=None):
    B, T, _ = x.shape
    final = final_gain is not None
    mod_map = (lambda b, t: (0, 0, 0)) if shared_mod else (lambda b, t: (b, 0, 0))
    row_block = pl.BlockSpec((1, tm, D_MODEL), lambda b, t: (b, t, 0))
    vec = pl.BlockSpec((1, D_MODEL), lambda b, t: (0, 0))
    in_specs = [pl.BlockSpec((1, tm, y.shape[-1]), lambda b, t: (b, t, 0)) for y in ys] + [
        row_block,
        pl.BlockSpec((1, 3, D_MODEL), mod_map),
        pl.BlockSpec((w.shape[0], D_MODEL), lambda b, t: (0, 0)),
    ]
    args = list(ys) + [x, mod, w]
    out_shape = [jax.ShapeDtypeStruct((B, T, D_MODEL), F32)]
    out_specs = [row_block]
    next_outs = None
    if final:
        in_specs.append(vec)
        args.append(final_gain)
    else:
        mod_n, gain_n, w_n, next_outs = next_proj
        in_specs += [pl.BlockSpec((1, 3, D_MODEL), mod_map), vec,
                     pl.BlockSpec((D_MODEL, w_n.shape[1]), lambda b, t: (0, 0))]
        args += [mod_n, gain_n, w_n]
        out_shape += [jax.ShapeDtypeStruct((B, T, wd), dt) for _, wd, dt in next_outs]
        out_specs += [pl.BlockSpec((1, tm, wd), lambda b, t: (b, t, 0)) for _, wd, _ in next_outs]
    return pl.pallas_call(
        functools.partial(_outproj_kernel, n_y=len(ys), final=final, next_outs=next_outs),
        out_shape=out_shape,
        grid=(B, T // tm),
        in_specs=in_specs,
        out_specs=out_specs,
        compiler_params=_cparams(2),
        name="out_proj",
    )(*args)


def _hgrn_kernel(q_ref, zf_ref, zb_ref, v_ref, g_ref, lgt_ref, gain_ref, *rest, seq, layer, has_s0, emit_state):
    rest = list(rest)
    s0_ref = rest.pop(0) if has_s0 else None
    o_ref = rest.pop(0)
    sfin_ref = rest.pop(0) if emit_state else None
    acc_ref, qd_ref, ki_ref, kd_ref, kv_ref, st_ref, dec_ref, mst_ref, msk_ref, mexp_ref = rest
    R = HGRN_ROWS
    C = HGRN_CHUNK
    n_blk = seq // R
    n_chunk = R // C
    n_all = seq // C

    @pl.when((pl.program_id(0) == 0) & (pl.program_id(1) == 0))
    def _build_masks():
        ti = lax.broadcasted_iota(jnp.int32, (R, R), 0)
        tj = lax.broadcasted_iota(jnp.int32, (R, R), 1)
        shift = C.bit_length() - 1
        same = lax.shift_right_logical(ti, shift) == lax.shift_right_logical(tj, shift)
        one = jnp.ones((R, R), F32)
        zero = jnp.zeros((R, R), F32)
        incl = (jnp.where(same, jnp.where(tj <= ti, one, zero), zero),
                jnp.where(same, jnp.where(tj >= ti, one, zero), zero))
        for d in range(2):
            msk_ref[d] = incl[d]
            mst_ref[d] = incl[d].astype(BF16)
        rr = lax.broadcasted_iota(jnp.int32, (R, n_chunk * LANES), 0)
        cc = lax.broadcasted_iota(jnp.int32, (R, n_chunk * LANES), 1)
        own = lax.shift_right_logical(rr, shift) == lax.shift_right_logical(cc, LANES.bit_length() - 1)
        mexp_ref[...] = jnp.where(own, 1.0, 0.0).astype(BF16)

    lgt = [lgt_ref[:, i, :] for i in range(lgt_ref.shape[1])]
    lmax = functools.reduce(jnp.maximum, lgt)
    ex = [jnp.exp(t - lmax) for t in lgt]
    lb_all = sum(ex[:layer + 1]) / sum(ex)
    gain = gain_ref[0]

    blocks_per_trip = 2 if n_blk % 2 == 0 else 1

    def gates(i, _):
        for u in range(blocks_per_trip):
            blk = i * blocks_per_trip + u
            rows = pl.ds(pl.multiple_of(blk * R, R), R)
            q = q_ref[0, rows, :]
            for d in range(2):
                th = jnp.tanh((zf_ref if d == 0 else zb_ref)[0, rows, :])
                lb = lb_all[d:d + 1, :]
                c = 0.5 * (1.0 - lb)
                ct = c * th
                f = (lb + c) + ct
                k = c - ct
                logf = jnp.log(f)
                hi = logf.astype(BF16)
                lo = (logf - hi.astype(F32)).astype(BF16)
                cs = jnp.dot(mst_ref[d], jnp.concatenate([hi, lo], axis=1), preferred_element_type=F32)
                b = cs[:, 0:LANES] + cs[:, LANES:2 * LANES]
                ends = [c * C + (C - 1 if d == 0 else 0) for c in range(n_chunk)]
                btot = jnp.concatenate([jnp.broadcast_to(b[t:t + 1, :], (C, LANES)) for t in ends], axis=0)
                qd_ref[d, rows, :] = (q * jnp.exp(b)).astype(BF16)
                ki_ref[d, rows, :] = (k * jnp.exp(-b)).astype(BF16)
                kd_ref[d, rows, :] = (k * jnp.exp(btot - b)).astype(BF16)
                for c in range(n_chunk):
                    dec_ref[d, blk * n_chunk + c] = jnp.exp(btot[c * C:c * C + 8, :])
        return 0

    lax.fori_loop(0, n_blk // blocks_per_trip, gates, 0)

    def intra(i, _):
        for u in range(blocks_per_trip):
            blk = i * blocks_per_trip + u
            rows = pl.ds(pl.multiple_of(blk * R, R), R)
            v = v_ref[0, rows, :]
            vb = v.astype(BF16)
            vt = v.T.astype(BF16)
            for d in range(2):
                att = lax.dot_general(qd_ref[d, rows, :], ki_ref[d, rows, :], NT_DIMS,
                                      preferred_element_type=F32)
                att = jnp.where(msk_ref[d] > 0.5, att, 0.0)
                acc_ref[d, rows, :] = jnp.dot(att.astype(BF16), vb, preferred_element_type=F32)
                kd_exp = jnp.concatenate([kd_ref[d, rows, :]] * n_chunk, axis=1) * mexp_ref[...]
                kv_all = jnp.dot(vt, kd_exp, preferred_element_type=F32)
                for c in range(n_chunk):
                    kv_ref[d, blk * n_chunk + c] = kv_all[:, c * LANES:(c + 1) * LANES]
        return 0

    lax.fori_loop(0, n_blk // blocks_per_trip, intra, 0)

    unroll = 4

    def states(i, sts):
        sts = list(sts)
        for u in range(unroll):
            n = i * unroll + u
            for d in range(2):
                c = n if d == 0 else n_all - 1 - n
                st_ref[d, c] = sts[d].astype(BF16)
                dec = jnp.concatenate([dec_ref[d, c]] * (DK_A // 8), axis=0)
                sts[d] = sts[d] * dec + kv_ref[d, c]
        return tuple(sts)

    if has_s0:
        st0 = (s0_ref[0, 0, 0].T, s0_ref[0, 1, 0].T)
    else:
        st0 = (jnp.zeros((DK_A, DK_A), F32),) * 2
    sts = lax.fori_loop(0, n_all // unroll, states, st0)
    if emit_state:
        for d in range(2):
            sfin_ref[0, d, 0] = sts[d].T

    def finish(blk, _):
        rows = pl.ds(pl.multiple_of(blk * R, R), R)
        tot = acc_ref[0, rows, :] + acc_ref[1, rows, :]
        for d in range(2):
            pieces = []
            for c in range(n_chunk):
                crow = pl.ds(pl.multiple_of(blk * R + c * C, C), C)
                pieces.append(lax.dot_general(qd_ref[d, crow, :], st_ref[d, blk * n_chunk + c], NT_DIMS,
                                              preferred_element_type=F32))
            tot = tot + jnp.concatenate(pieces, axis=0)
        var = jnp.mean(tot * tot, axis=-1, keepdims=True)
        y = tot * lax.rsqrt(var + EPS) * gain
        o_ref[0, rows, :] = (y * _silu(g_ref[0, rows, :])).astype(o_ref.dtype)
        return 0

    lax.fori_loop(0, n_blk, finish, 0)


def _hgrn(ya, lgt, layer, gain, s0, emit_state):
    B, T, _ = ya.shape
    has_s0 = s0 is not None

    def col(k):
        return pl.BlockSpec((1, T, LANES), lambda b, h, k=k: (b, 0, k * H_A + h))

    in_specs = [col(0), col(1), col(2), col(3), col(4),
                pl.BlockSpec((2, lgt.shape[1], LANES), lambda b, h: (0, 0, h)),
                pl.BlockSpec((1, 1, LANES), lambda b, h: (h, 0, 0))]
    args = [ya, ya, ya, ya, ya, lgt, gain]
    if has_s0:
        in_specs.append(pl.BlockSpec((1, 2, 1, DK_A, DK_A), lambda b, h: (b, 0, h, 0, 0)))
        args.append(s0)
    out_shape = [jax.ShapeDtypeStruct((B, T, H_A * DK_A), BF16)]
    out_specs = [pl.BlockSpec((1, T, LANES), lambda b, h: (b, 0, h))]
    if emit_state:
        out_shape.append(jax.ShapeDtypeStruct((B, 2, H_A, DK_A, DK_A), F32))
        out_specs.append(pl.BlockSpec((1, 2, 1, DK_A, DK_A), lambda b, h: (b, 0, h, 0, 0)))
    res = pl.pallas_call(
        functools.partial(_hgrn_kernel, seq=T, layer=layer, has_s0=has_s0, emit_state=emit_state),
        out_shape=out_shape,
        grid=(B, H_A),
        in_specs=in_specs,
        out_specs=out_specs,
        scratch_shapes=[pltpu.VMEM((2, T, LANES), F32),
                        pltpu.VMEM((2, T, LANES), BF16),
                        pltpu.VMEM((2, T, LANES), BF16),
                        pltpu.VMEM((2, T, LANES), BF16),
                        pltpu.VMEM((2, T // HGRN_CHUNK, DK_A, DK_A), F32),
                        pltpu.VMEM((2, T // HGRN_CHUNK, DK_A, DK_A), BF16),
                        pltpu.VMEM((2, T // HGRN_CHUNK, 8, LANES), F32),
                        pltpu.VMEM((2, HGRN_ROWS, HGRN_ROWS), BF16),
                        pltpu.VMEM((2, HGRN_ROWS, HGRN_ROWS), F32),
                        pltpu.VMEM((HGRN_ROWS, HGRN_ROWS // HGRN_CHUNK * LANES), BF16)],
        compiler_params=_cparams(2),
        name="hgrn2",
    )(*args)
    return res if emit_state else (res[0], None)


def _head_masks():
    lane = lax.broadcasted_iota(jnp.int32, (1, LANES), 1)
    return lane < DH_B, lane >= DH_B


def _ctx_attn_kernel(q_ref, k_ref, v_ref, g_ref, kv_ref, o_ref, newk_ref, newv_ref):
    scale = DH_B ** -0.5
    masks = _head_masks()
    T = q_ref.shape[1]
    for h in range(H_B):
        newk_ref[0, h] = kv_ref[0, :, h * DH_B:(h + 1) * DH_B]
        newv_ref[0, h] = kv_ref[0, :, (H_B + h) * DH_B:(H_B + h + 1) * DH_B]
    for p in range(H_B // 2):
        cols = slice(p * LANES, (p + 1) * LANES)
        q = q_ref[0, :, cols] * scale
        qs = jnp.concatenate([jnp.where(masks[h], q, jnp.zeros_like(q)) for h in range(2)], axis=0)
        s = lax.dot_general(qs, k_ref[0, :, cols], NT_DIMS, preferred_element_type=F32)
        e = jnp.exp(s - jnp.max(s, axis=-1, keepdims=True))
        pr = e / jnp.sum(e, axis=-1, keepdims=True)
        o = jnp.dot(pr.astype(BF16), v_ref[0, :, cols], preferred_element_type=F32)
        o = jnp.where(masks[0], o[0:T], o[T:2 * T])
        o_ref[0, :, cols] = (o * _silu(g_ref[0, :, cols].astype(F32))).astype(o_ref.dtype)


def _ctx_attn(yb, kv):
    B, T, _ = yb.shape
    width = H_B * DH_B

    def col(k):
        return pl.BlockSpec((1, T, width), lambda b, k=k: (b, 0, k))

    cache = pl.BlockSpec((1, H_B, T, DH_B), lambda b: (b, 0, 0, 0))
    return pl.pallas_call(
        _ctx_attn_kernel,
        out_shape=[jax.ShapeDtypeStruct((B, T, width), BF16),
                   jax.ShapeDtypeStruct((B, H_B, T, DH_B), F32),
                   jax.ShapeDtypeStruct((B, H_B, T, DH_B), F32)],
        grid=(B,),
        in_specs=[col(0), col(1), col(2), col(3), pl.BlockSpec((1, T, 2 * width), lambda b: (b, 0, 0))],
        out_specs=[pl.BlockSpec((1, T, width), lambda b: (b, 0, 0)), cache, cache],
        compiler_params=_cparams(1),
        name="ctx_attn",
    )(yb, yb, yb, yb, kv)


N_DR = 2 * NA_KH - 1
N_DC = 2 * NA_KW - 1
N_TAB = N_DR - 1


def _nat_kernel(rb_ref, q_ref, k_ref, v_ref, g_ref, kc_ref, vc_ref, o_ref,
                tab_ref, qs_ref, s_ref, p_ref, r_ref, oc_ref, *, rows):
    p = pl.program_id(0)
    scale = DH_B ** -0.5
    kh = min(NA_KH, rows)
    masks = _head_masks()

    @pl.when(pl.program_id(1) == 0)
    def _build_tables():
        c = lax.broadcasted_iota(jnp.int32, (GRID_W, LANES), 0)
        lane = lax.broadcasted_iota(jnp.int32, (GRID_W, LANES), 1)
        kcol = lane & (GRID_W - 1)
        upper = lane >= GRID_W
        ws = jnp.clip(c - NA_KW // 2, 0, GRID_W - NA_KW)
        neg = jnp.full((GRID_W, LANES), NEG_INF, F32)
        diag = kcol - c + (NA_KW - 1)
        for h in range(2):
            base = (2 * p + h) * (N_DR * N_DC)

            def per_dr(i, _, base=base, h=h):
                def per_dc(dd, acc):
                    lo = rb_ref[base + i * N_DC + dd]
                    hi = rb_ref[base + (i + 1) * N_DC + dd]
                    return jnp.where(diag == dd, jnp.where(upper, hi, lo), acc)

                acc = lax.fori_loop(0, N_DC, per_dc, neg)
                acc = jnp.where(kcol >= ws, jnp.where(kcol < ws + NA_KW, acc, neg), neg)
                tab_ref[h, i] = acc
                return 0

            lax.fori_loop(0, N_TAB, per_dr, 0)

    kc = jnp.concatenate([kc_ref[0, 0], kc_ref[0, 1]], axis=1).astype(BF16)
    vc = jnp.concatenate([vc_ref[0, 0], vc_ref[0, 1]], axis=1).astype(BF16)
    n_keys = kh * GRID_W
    n_ctx = kc.shape[0]
    G = NA_GROUP
    W2 = 2 * GRID_W

    def group(gi, _):
        r_first = gi * G
        q0 = pl.multiple_of(r_first * GRID_W, G * GRID_W)
        for i in range(G):
            qi = q_ref[0, pl.ds(q0 + i * GRID_W, GRID_W), :] * scale
            for h in range(2):
                qs_ref[i * W2 + h * GRID_W:i * W2 + (h + 1) * GRID_W, :] = jnp.where(
                    masks[h], qi, jnp.zeros_like(qi))
        s_ref[:, n_keys:n_keys + n_ctx] = lax.dot_general(qs_ref[...], kc, NT_DIMS,
                                                          preferred_element_type=F32)
        windows = []
        for i in range(G):
            r = r_first + i
            rs = jnp.clip(r - kh // 2, 0, rows - kh)
            k0 = pl.multiple_of(rs * GRID_W, GRID_W)
            windows.append(k0)
            dr0 = rs - r + (NA_KH - 1)
            bias = jnp.concatenate(
                [jnp.concatenate([tab_ref[h, dr0 + 2 * m] for m in range(kh // 2)], axis=1)
                 for h in range(2)], axis=0)
            s_ref[i * W2:(i + 1) * W2, 0:n_keys] = lax.dot_general(
                qs_ref[i * W2:(i + 1) * W2, :], k_ref[0, pl.ds(k0, n_keys), :], NT_DIMS,
                preferred_element_type=F32) + bias
        for i in range(G):
            s = s_ref[i * W2:(i + 1) * W2, :]
            e = jnp.exp(s - jnp.max(s, axis=-1, keepdims=True))
            p_ref[i * W2:(i + 1) * W2, :] = e.astype(BF16)
            rinv = 1.0 / jnp.sum(e, axis=-1, keepdims=True)
            r_ref[i * W2:(i + 1) * W2, :] = jnp.broadcast_to(rinv, (W2, LANES))
        oc_ref[...] = jnp.dot(p_ref[:, n_keys:n_keys + n_ctx], vc, preferred_element_type=F32)
        for i in range(G):
            o = jnp.dot(p_ref[i * W2:(i + 1) * W2, 0:n_keys], v_ref[0, pl.ds(windows[i], n_keys), :],
                        preferred_element_type=F32)
            o = (o + oc_ref[i * W2:(i + 1) * W2, :]) * r_ref[i * W2:(i + 1) * W2, :]
            o = jnp.where(masks[0], o[0:GRID_W], o[GRID_W:W2])
            out_rows = pl.ds(q0 + i * GRID_W, GRID_W)
            gate = g_ref[0, out_rows, :].astype(F32)
            o_ref[0, out_rows, :] = (o * _silu(gate)).astype(o_ref.dtype)
        return 0

    lax.fori_loop(0, rows // G, group, 0)


def _nat(yb, kc, vc, rel_bias):
    B, T, _ = yb.shape
    Tc = kc.shape[2]
    n_pair = H_B // 2
    rows = T // GRID_W
    n_stack = NA_GROUP * 2 * GRID_W
    n_keys = min(NA_KH, rows) * GRID_W

    def col(k):
        return pl.BlockSpec((1, T, LANES), lambda p, b, k=k: (b, 0, k * n_pair + p))

    ctx = pl.BlockSpec((1, 2, Tc, DH_B), lambda p, b: (b, p, 0, 0))
    return pl.pallas_call(
        functools.partial(_nat_kernel, rows=rows),
        out_shape=jax.ShapeDtypeStruct((B, T, H_B * DH_B), BF16),
        grid=(n_pair, B),
        in_specs=[pl.BlockSpec(memory_space=pltpu.SMEM), col(0), col(1), col(2), col(3), ctx, ctx],
        out_specs=pl.BlockSpec((1, T, LANES), lambda p, b: (b, 0, p)),
        scratch_shapes=[pltpu.VMEM((2, N_TAB, GRID_W, LANES), F32),
                        pltpu.VMEM((n_stack, LANES), BF16),
                        pltpu.VMEM((n_stack, n_keys + Tc), F32),
                        pltpu.VMEM((n_stack, n_keys + Tc), BF16),
                        pltpu.VMEM((n_stack, LANES), F32),
                        pltpu.VMEM((n_stack, LANES), F32)],
        compiler_params=_cparams(2),
        name="nbr_attn",
    )(rel_bias.reshape(-1), yb, yb, yb, yb, kc, vc)


def _seg_len(seq):
    length = -(-seq // N_SEG)
    while length % 8 != 4:
        length += 1
    return length


def _step_block(seg_len):
    return max(d for d in range(1, seg_len + 1) if seg_len % d == 0 and d * N_SEG <= RG_ROWS * 3 // 2)


def _rglru_kernel(x_ref, g_ref, cw_ref, cb_ref, wg_ref, lam_ref, *rest, seq, has_s0, emit_state):
    rest = list(rest)
    s0_ref = rest.pop(0) if has_s0 else None
    o_ref = rest.pop(0)
    hfin_ref = rest.pop(0) if emit_state else None
    xpad_ref, a_ref, u_ref, h_ref, p_ref, hn_ref = rest
    L = _seg_len(seq)
    n_rows = N_SEG * L
    RB = RG_ROWS
    CB = RG_SLABS
    TB = _step_block(L)
    n_tile = N_SEG // 8
    chains = [(d, j, s) for d in range(2) for j in range(CB) for s in range(n_tile)]

    for j in range(CB):
        xpad_ref[j, 0:8, :] = jnp.zeros((8, LANES), F32)
        xpad_ref[j, 8:seq + 8, :] = x_ref[0, :, j * LANES:(j + 1) * LANES]
        xpad_ref[j, seq + 8:n_rows + 16, :] = jnp.zeros((n_rows + 8 - seq, LANES), F32)

    nl = -lam_ref[...]
    sp = jnp.maximum(nl, 0.0) + jnp.log1p(jnp.exp(-jnp.abs(nl)))
    cw = cw_ref[...]
    cbias = cb_ref[...]
    ones2 = jnp.where(lax.broadcasted_iota(jnp.int32, (TB * N_SEG, LANES), 1) < 2, 1.0, 0.0).astype(BF16)

    def gates(blk, _):
        t0 = blk * TB
        r0 = pl.multiple_of(blk * (TB * N_SEG), TB * N_SEG)
        for j in range(CB):
            lanes = slice(j * LANES, (j + 1) * LANES)
            tiles = []
            for tt in range(TB):
                for s in range(n_tile):
                    taps = [xpad_ref[j, pl.ds(6 + k + t0 + tt + s * 8 * L, 8, stride=L), :] for k in range(4)]
                    xt = cw[0:1, lanes] * taps[0] + cw[1:2, lanes] * taps[1]
                    xt = xt + cw[2:3, lanes] * taps[2]
                    tiles.append(xt + cw[3:4, lanes] * taps[3] + cbias[:, lanes])
            xj = jnp.concatenate(tiles, axis=0)
            gt = jnp.dot(jnp.concatenate([xj.astype(BF16), ones2], axis=1), wg_ref[j],
                         preferred_element_type=F32)
            xh = 0.5 * xj
            for d in range(2):
                th_r = jnp.tanh(gt[:, (2 * d) * LANES:(2 * d + 1) * LANES])
                th_i = jnp.tanh(gt[:, (2 * d + 1) * LANES:(2 * d + 2) * LANES])
                half = (-0.5 * RG_C) * sp[d:d + 1, lanes]
                la = half + half * th_r
                a = jnp.exp(la)
                y = -jnp.tanh(la) * (1.0 + a * a)
                root = jnp.where(y > 0.0, y * lax.rsqrt(y), 0.0)
                a_ref[d, j, pl.ds(r0, TB * N_SEG), :] = a
                u_ref[d, j, pl.ds(r0, TB * N_SEG), :] = root * (xh + xh * th_i)
        return 0

    lax.fori_loop(0, L // TB, gates, 0)

    first_pad = [[min(max(seq - (s * 8 + r) * L, 0), L) for r in range(8)] for s in range(n_tile)]
    sub = lax.broadcasted_iota(jnp.int32, (8, LANES), 0)
    pad_from = []
    for s in range(n_tile):
        if all(f == L for f in first_pad[s]):
            pad_from.append(None)
        else:
            vec = jnp.full((8, LANES), L, jnp.int32)
            for r in range(8):
                vec = jnp.where(sub == r, first_pad[s][r], vec)
            pad_from.append(vec)

    def step_rows(step, s):
        return pl.ds(pl.multiple_of(step * N_SEG + s * 8, 8), 8)

    unroll = 4

    def scan(i, carry):
        carry = list(carry)
        for k in range(unroll):
            for n, (d, j, s) in enumerate(chains):
                h, pr = carry[n]
                t = i * unroll + k
                if d == 1:
                    t = L - 1 - t
                idx = step_rows(t, s)
                a = a_ref[d, j, idx, :]
                u = u_ref[d, j, idx, :]
                if pad_from[s] is not None:
                    live = t < pad_from[s]
                    a = jnp.where(live, a, 1.0)
                    u = jnp.where(live, u, 0.0)
                h = a * h + u
                pr = pr * a
                h_ref[d, j, idx, :] = h
                p_ref[d, j, idx, :] = pr
                carry[n] = (h, pr)
        return tuple(carry)

    zero = jnp.zeros((8, LANES), F32)
    one = jnp.ones((8, LANES), F32)
    ends = lax.fori_loop(0, L // unroll, scan, ((zero, one),) * len(chains))

    cins = {}
    finals = [[None] * CB for _ in range(2)]
    for d in range(2):
        for j in range(CB):
            if has_s0:
                c = s0_ref[0, d:d + 1, j * LANES:(j + 1) * LANES]
            else:
                c = jnp.zeros((1, LANES), F32)
            cin = [None] * N_SEG
            for kk in range(N_SEG):
                seg = kk if d == 0 else N_SEG - 1 - kk
                s, row = divmod(seg, 8)
                h_end, p_end = ends[chains.index((d, j, s))]
                cin[seg] = c
                c = h_end[row:row + 1, :] + p_end[row:row + 1, :] * c
            finals[d][j] = c
            for s in range(n_tile):
                cins[(d, j, s)] = jnp.concatenate(cin[s * 8:(s + 1) * 8], axis=0)

    def fix(i, _):
        for k in range(unroll):
            t = i * unroll + k
            for j in range(CB):
                for s in range(n_tile):
                    idx = step_rows(t, s)
                    parts = [h_ref[d, j, idx, :] + p_ref[d, j, idx, :] * cins[(d, j, s)] for d in range(2)]
                    hn_ref[j, pl.ds(t + s * 8 * L, 8, stride=L), :] = parts[0] + parts[1]
        return 0

    lax.fori_loop(0, L // unroll, fix, 0)

    if emit_state:
        hfin_ref[0] = jnp.concatenate([jnp.concatenate(finals[d], axis=1) for d in range(2)], axis=0)

    def combine(blk, _):
        rows = pl.ds(pl.multiple_of(blk * RB, RB), RB)
        hs = jnp.concatenate([hn_ref[j, rows, :] for j in range(CB)], axis=1)
        o_ref[0, rows, :] = (hs * _silu(g_ref[0, rows, :])).astype(o_ref.dtype)
        return 0

    lax.fori_loop(0, seq // RB, combine, 0)


def _gate_weights(gate_w, gate_b):
    w = (0.5 * gate_w).transpose(2, 3, 0, 1, 4).reshape(H_C, BW_C, 4 * BW_C).astype(BF16)
    b = (0.5 * gate_b).reshape(2, 2, H_C, BW_C).transpose(2, 0, 1, 3).reshape(H_C, 1, 4 * BW_C).astype(F32)
    hi = b.astype(BF16)
    lo = (b - hi.astype(F32)).astype(BF16)
    zeros = jnp.zeros((H_C, BW_C - 2, 4 * BW_C), BF16)
    return jnp.concatenate([w, hi, lo, zeros], axis=1)


def _rglru(xg, conv_w, conv_b, wg, lam, s0, emit_state):
    B, T, _ = xg.shape
    has_s0 = s0 is not None
    n_rows = N_SEG * _seg_len(T)
    CB = RG_SLABS
    wide = CB * LANES
    n_steps = H_C // CB
    in_specs = [
        pl.BlockSpec((1, T, wide), lambda b, c: (b, 0, c)),
        pl.BlockSpec((1, T, wide), lambda b, c: (b, 0, n_steps + c)),
        pl.BlockSpec((4, wide), lambda b, c: (0, c)),
        pl.BlockSpec((1, wide), lambda b, c: (0, c)),
        pl.BlockSpec((CB, 2 * BW_C, 4 * BW_C), lambda b, c: (c, 0, 0)),
        pl.BlockSpec((2, wide), lambda b, c: (0, c)),
    ]
    args = [xg, xg, conv_w, conv_b, wg, lam]
    if has_s0:
        in_specs.append(pl.BlockSpec((1, 2, wide), lambda b, c: (b, 0, c)))
        args.append(s0)
    out_shape = [jax.ShapeDtypeStruct((B, T, W_C), BF16)]
    out_specs = [pl.BlockSpec((1, T, wide), lambda b, c: (b, 0, c))]
    if emit_state:
        out_shape.append(jax.ShapeDtypeStruct((B, 2, W_C), F32))
        out_specs.append(pl.BlockSpec((1, 2, wide), lambda b, c: (b, 0, c)))
    res = pl.pallas_call(
        functools.partial(_rglru_kernel, seq=T, has_s0=has_s0, emit_state=emit_state),
        out_shape=out_shape,
        grid=(B, n_steps),
        in_specs=in_specs,
        out_specs=out_specs,
        scratch_shapes=[pltpu.VMEM((CB, n_rows + 16, LANES), F32)]
        + [pltpu.VMEM((2, CB, n_rows, LANES), F32)] * 4 + [pltpu.VMEM((CB, n_rows, LANES), F32)],
        compiler_params=_cparams(2),
        name="rglru",
    )(*args)
    return res if emit_state else (res[0], None)


A_COLS = 5 * H_A * DK_A
B_COLS = 4 * H_B * DH_B


def kernel(x_prompt, x_sample, state_hgrn, cache_na_k, cache_na_v, state_rglru, c, c_ctx, norm_gain, w_mod, b_mod, w_in_even, w_out_even, hgrn_lb_logits, hgrn_out_gain, na_rel_bias, w_in_odd, w_out_odd, conv_w, conv_b, rg_gate_w, rg_gate_b, rg_lambda, final_gain):
    n_ctx = x_prompt.shape[0]
    n_lat = x_sample.shape[0]
    depth = w_mod.shape[0]

    cond = jnp.zeros((16, D_MODEL), F32).at[0].set(c_ctx).at[1:1 + n_lat].set(c)
    mod = _modulation(cond, w_mod, b_mod.reshape(depth, 1, 3 * D_MODEL))
    mod = mod.reshape(depth, 16, 3, D_MODEL)

    t_ctx = x_prompt.shape[1]

    def flat(a):
        return a.reshape(1, n_ctx * t_ctx, a.shape[-1])

    def unflat(a):
        return a.reshape(n_ctx, t_ctx, a.shape[-1])

    def in_proj_params(l):
        if l % 2 == 0:
            outs_s = ((0, A_COLS, F32), (A_COLS, B_COLS, BF16))
            outs_c = outs_s + ((A_COLS + H_B * DH_B, 2 * H_B * DH_B, F32),)
            a_key = H_A * DK_A
            col = jnp.arange(w_in_even.shape[-1])
            halve = jnp.where((col >= a_key) & (col < 3 * a_key), 0.5, 1.0).astype(F32)
            return (w_in_even[l // 2] * halve).astype(BF16), outs_c, outs_s
        outs = ((0, 2 * W_C, F32),)
        return w_in_odd[l // 2].astype(BF16), outs, outs

    xc, xs = x_prompt, x_sample
    new_hgrn, new_k, new_v, new_rg = [], [], [], []
    proj_c = proj_s = None
    for l in range(depth):
        j = l // 2
        mod_c, mod_s = mod[l, 0:1], mod[l, 1:1 + n_lat]
        if proj_c is None:
            gain = norm_gain[l].reshape(1, D_MODEL)
            w_in, outs_c, outs_s = in_proj_params(l)
            proj_c = [unflat(t) for t in _inproj(flat(xc), mod_c, gain, w_in, outs_c, 512, True)]
            proj_s = _inproj(xs, mod_s, gain, w_in, outs_s, 512, False)
        if l % 2 == 0:
            w_out = w_out_even[j].astype(BF16)
            (ya_c, yb_c, kv_c), (ya_s, yb_s) = proj_c, proj_s
            hgain = hgrn_out_gain[j].reshape(H_A, 1, DK_A)
            oa_c, s_fin = _hgrn(ya_c, hgrn_lb_logits, j, hgain, None, True)
            oa_s, _ = _hgrn(ya_s, hgrn_lb_logits, j, hgain, state_hgrn[:, j], False)
            ob_c, k_c, v_c = _ctx_attn(yb_c, kv_c)
            ob_s = _nat(yb_s, cache_na_k[:, j], cache_na_v[:, j], na_rel_bias[j])
            ys_c, ys_s = (oa_c, ob_c), (oa_s, ob_s)
            new_hgrn.append(s_fin)
            new_k.append(k_c)
            new_v.append(v_c)
        else:
            w_out = w_out_odd[j].astype(BF16)
            (xg_c,), (xg_s,) = proj_c, proj_s
            wg = _gate_weights(rg_gate_w[j], rg_gate_b[j])
            cb = conv_b[j].reshape(1, W_C)
            y_c, h_fin = _rglru(xg_c, conv_w[j], cb, wg, rg_lambda[j], None, True)
            y_s, _ = _rglru(xg_s, conv_w[j], cb, wg, rg_lambda[j], state_rglru[:, j], False)
            ys_c, ys_s = (y_c,), (y_s,)
            new_rg.append(h_fin)
        ys_c = tuple(flat(y) for y in ys_c)
        if l == depth - 1:
            fgain = final_gain.reshape(1, D_MODEL)
            (xc,) = _outproj(ys_c, flat(xc), mod_c, w_out, 1024, True, final_gain=fgain)
            (xs,) = _outproj(ys_s, xs, mod_s, w_out, 1024, False, final_gain=fgain)
            xc = unflat(xc)
        else:
            gain_n = norm_gain[l + 1].reshape(1, D_MODEL)
            w_n, outs_c, outs_s = in_proj_params(l + 1)
            mod_cn, mod_sn = mod[l + 1, 0:1], mod[l + 1, 1:1 + n_lat]
            xc, *proj_c = _outproj(ys_c, flat(xc), mod_c, w_out, 512, True,
                                   next_proj=(mod_cn, gain_n, w_n, outs_c))
            xs, *proj_s = _outproj(ys_s, xs, mod_s, w_out, 512, False,
                                   next_proj=(mod_sn, gain_n, w_n, outs_s))
            xc = unflat(xc)
            proj_c = [unflat(t) for t in proj_c]
    return (xc, xs, jnp.stack(new_hgrn, axis=1), jnp.stack(new_k, axis=1),
            jnp.stack(new_v, axis=1), jnp.stack(new_rg, axis=1))
```

```python
import functools

import jax
import jax.numpy as jnp
from jax import lax
from jax.experimental import pallas as pl
from jax.experimental.pallas import tpu as pltpu

F32 = jnp.float32
BF16 = jnp.bfloat16

D_MODEL = 1024
EPS = 1e-6
NEG_INF = -1e30
H_A = 4
DK_A = 128
HGRN_CHUNK = 32
HGRN_ROWS = 256
H_B = 8
DH_B = 64
GRID_W = 64
NA_KH = 8
NA_KW = 16
NA_GROUP = 8
W_C = 1024
H_C = 8
BW_C = W_C // H_C
RG_C = 8.0
RG_ROWS = 256
RG_SLABS = 2
N_SEG = 16
LANES = 128
VMEM_LIMIT = 48 * 1024 * 1024

NT_DIMS = (((1,), (1,)), ((), ()))
TN_DIMS = (((0,), (0,)), ((), ()))


def _silu(x):
    half = 0.5 * x
    return half + half * jnp.tanh(half)


def _cparams(n_axes):
    return pltpu.CompilerParams(dimension_semantics=("arbitrary",) * n_axes,
                                vmem_limit_bytes=VMEM_LIMIT)


def _mod_kernel(cond_ref, w_ref, b_ref, o_ref):
    s = _silu(cond_ref[...])
    o_ref[0] = jnp.dot(s.astype(BF16), w_ref[0].astype(BF16), preferred_element_type=F32) + b_ref[0]


def _modulation(cond, w_mod, b_mod):
    depth = w_mod.shape[0]
    n_rows = cond.shape[0]
    return pl.pallas_call(
        _mod_kernel,
        out_shape=jax.ShapeDtypeStruct((depth, n_rows, 3 * D_MODEL), F32),
        grid=(depth, 3),
        in_specs=[
            pl.BlockSpec((n_rows, D_MODEL), lambda l, n: (0, 0)),
            pl.BlockSpec((1, D_MODEL, D_MODEL), lambda l, n: (l, 0, n)),
            pl.BlockSpec((1, 1, D_MODEL), lambda l, n: (l, 0, n)),
        ],
        out_specs=pl.BlockSpec((1, n_rows, D_MODEL), lambda l, n: (l, 0, n)),
        compiler_params=_cparams(2),
        name="adaln_mod",
    )(cond, w_mod, b_mod)


def _project(x, mod_ref, gain_ref, w_ref, out_refs, outs):
    var = jnp.mean(x * x, axis=-1, keepdims=True)
    y = x * lax.rsqrt(var + EPS) * gain_ref[...]
    h = y * (1.0 + mod_ref[0, 1:2, :]) + mod_ref[0, 0:1, :]
    hb = h.astype(BF16)
    step = 512
    for c in range(0, w_ref.shape[1], step):
        users = [(o_ref, c - col0) for o_ref, (col0, width, _) in zip(out_refs, outs)
                 if col0 <= c < col0 + width]
        if users:
            r = jnp.dot(hb, w_ref[:, c:c + step], preferred_element_type=F32)
            for o_ref, off in users:
                o_ref[0, :, off:off + step] = r.astype(o_ref.dtype)


def _inproj_kernel(x_ref, mod_ref, gain_ref, w_ref, *out_refs, outs):
    _project(x_ref[0], mod_ref, gain_ref, w_ref, out_refs, outs)


def _inproj(x, mod, gain, w, outs, tm, shared_mod):
    B, T, _ = x.shape
    n_cols = w.shape[1]
    mod_map = (lambda b, t: (0, 0, 0)) if shared_mod else (lambda b, t: (b, 0, 0))
    return pl.pallas_call(
        functools.partial(_inproj_kernel, outs=outs),
        out_shape=[jax.ShapeDtypeStruct((B, T, wd), dt) for _, wd, dt in outs],
        grid=(B, T // tm),
        in_specs=[
            pl.BlockSpec((1, tm, D_MODEL), lambda b, t: (b, t, 0)),
            pl.BlockSpec((1, 3, D_MODEL), mod_map),
            pl.BlockSpec((1, D_MODEL), lambda b, t: (0, 0)),
            pl.BlockSpec((D_MODEL, n_cols), lambda b, t: (0, 0)),
        ],
        out_specs=[pl.BlockSpec((1, tm, wd), lambda b, t: (b, t, 0)) for _, wd, _ in outs],
        compiler_params=_cparams(2),
        name="in_proj",
    )(x, mod, gain, w)


def _outproj_kernel(*refs, n_y, final, next_outs):
    y_refs, (x_ref, mod_ref, w_ref), rest = refs[:n_y], refs[n_y:n_y + 3], refs[n_y + 3:]
    m = None
    row = 0
    for y_ref in y_refs:
        width = y_ref.shape[-1]
        part = jnp.dot(y_ref[0], w_ref[row:row + width, :], preferred_element_type=F32)
        m = part if m is None else m + part
        row += width
    xn = x_ref[0] + mod_ref[0, 2:3, :] * m
    if final:
        gain_ref, o_ref = rest
        var = jnp.mean(xn * xn, axis=-1, keepdims=True)
        o_ref[0] = xn * lax.rsqrt(var + EPS) * gain_ref[...]
    else:
        modn_ref, gainn_ref, wn_ref, o_ref = rest[:4]
        o_ref[0] = xn
        _project(xn, modn_ref, gainn_ref, wn_ref, rest[4:], next_outs)


def _outproj(ys, x, mod, w, tm, shared_mod, final_gain=None, next_proj=None):
    B, T, _ = x.shape
    final = final_gain is not None
    mod_map = (lambda b, t: (0, 0, 0)) if shared_mod else (lambda b, t: (b, 0, 0))
    row_block = pl.BlockSpec((1, tm, D_MODEL), lambda b, t: (b, t, 0))
    vec = pl.BlockSpec((1, D_MODEL), lambda b, t: (0, 0))
    in_specs = [pl.BlockSpec((1, tm, y.shape[-1]), lambda b, t: (b, t, 0)) for y in ys] + [
        row_block,
        pl.BlockSpec((1, 3, D_MODEL), mod_map),
        pl.BlockSpec((w.shape[0], D_MODEL), lambda b, t: (0, 0)),
    ]
    args = list(ys) + [x, mod, w]
    out_shape = [jax.ShapeDtypeStruct((B, T, D_MODEL), F32)]
    out_specs = [row_block]
    next_outs = None
    if final:
        in_specs.append(vec)
        args.append(final_gain)
    else:
        mod_n, gain_n, w_n, next_outs = next_proj
        in_specs += [pl.BlockSpec((1, 3, D_MODEL), mod_map), vec,
                     pl.BlockSpec((D_MODEL, w_n.shape[1]), lambda b, t: (0, 0))]
        args += [mod_n, gain_n, w_n]
        out_shape += [jax.ShapeDtypeStruct((B, T, wd), dt) for _, wd, dt in next_outs]
        out_specs += [pl.BlockSpec((1, tm, wd), lambda b, t: (b, t, 0)) for _, wd, _ in next_outs]
    return pl.pallas_call(
        functools.partial(_outproj_kernel, n_y=len(ys), final=final, next_outs=next_outs),
        out_shape=out_shape,
        grid=(B, T // tm),
        in_specs=in_specs,
        out_specs=out_specs,
        compiler_params=_cparams(2),
        name="out_proj",
    )(*args)


def _hgrn_kernel(q_ref, zf_ref, zb_ref, v_ref, g_ref, lgt_ref, gain_ref, *rest, seq, layer, per_block):
    rest = list(rest)
    s0_ref = None if per_block else rest.pop(0)
    o_ref = rest.pop(0)
    sfin_ref = rest.pop(0) if per_block else None
    acc_ref, qd_ref, ki_ref, kd_ref, kv_ref, st_ref, dec_ref, mst_ref, msk_ref, mexp_ref = rest
    R = HGRN_ROWS
    C = HGRN_CHUNK
    n_blk = seq // R
    n_chunk = R // C
    n_all = seq // C

    @pl.when((pl.program_id(0) == 0) & (pl.program_id(1) == 0))
    def _build_masks():
        ti = lax.broadcasted_iota(jnp.int32, (R, R), 0)
        tj = lax.broadcasted_iota(jnp.int32, (R, R), 1)
        shift = C.bit_length() - 1
        same = lax.shift_right_logical(ti, shift) == lax.shift_right_logical(tj, shift)
        one = jnp.ones((R, R), F32)
        zero = jnp.zeros((R, R), F32)
        incl = (jnp.where(same, jnp.where(tj <= ti, one, zero), zero),
                jnp.where(same, jnp.where(tj >= ti, one, zero), zero))
        for d in range(2):
            msk_ref[d] = incl[d]
            mst_ref[d] = incl[d].astype(BF16)
        rr = lax.broadcasted_iota(jnp.int32, (R, n_chunk * LANES), 0)
        cc = lax.broadcasted_iota(jnp.int32, (R, n_chunk * LANES), 1)
        own = lax.shift_right_logical(rr, shift) == lax.shift_right_logical(cc, LANES.bit_length() - 1)
        mexp_ref[...] = jnp.where(own, 1.0, 0.0).astype(BF16)

    lgt = [lgt_ref[:, i, :] for i in range(lgt_ref.shape[1])]
    lmax = functools.reduce(jnp.maximum, lgt)
    ex = [jnp.exp(t - lmax) for t in lgt]
    lb_all = sum(ex[:layer + 1]) / sum(ex)
    gain = gain_ref[0]

    blocks_per_trip = 2 if n_blk % 2 == 0 else 1

    def gates(i, _):
        for u in range(blocks_per_trip):
            blk = i * blocks_per_trip + u
            rows = pl.ds(pl.multiple_of(blk * R, R), R)
            q = q_ref[0, rows, :]
            for d in range(2):
                th = jnp.tanh((zf_ref if d == 0 else zb_ref)[0, rows, :])
                lb = lb_all[d:d + 1, :]
                c = 0.5 * (1.0 - lb)
                ct = c * th
                f = (lb + c) + ct
                k = c - ct
                logf = jnp.log(f)
                hi = logf.astype(BF16)
                lo = (logf - hi.astype(F32)).astype(BF16)
                cs = jnp.dot(mst_ref[d], jnp.concatenate([hi, lo], axis=1), preferred_element_type=F32)
                b = cs[:, 0:LANES] + cs[:, LANES:2 * LANES]
                ends = [c * C + (C - 1 if d == 0 else 0) for c in range(n_chunk)]
                btot = jnp.concatenate([jnp.broadcast_to(b[t:t + 1, :], (C, LANES)) for t in ends], axis=0)
                qd_ref[d, rows, :] = (q * jnp.exp(b)).astype(BF16)
                ki_ref[d, rows, :] = (k * jnp.exp(-b)).astype(BF16)
                kd_ref[d, rows, :] = (k * jnp.exp(btot - b)).astype(BF16)
                for c in range(n_chunk):
                    dec_ref[d, blk * n_chunk + c] = jnp.exp(btot[c * C:c * C + 8, :])
        return 0

    lax.fori_loop(0, n_blk // blocks_per_trip, gates, 0)

    def intra(i, _):
        for u in range(blocks_per_trip):
            blk = i * blocks_per_trip + u
            rows = pl.ds(pl.multiple_of(blk * R, R), R)
            v = v_ref[0, rows, :]
            vb = v.astype(BF16)
            vt = v.T.astype(BF16)
            att_sum = None
            for d in range(2):
                att = lax.dot_general(qd_ref[d, rows, :], ki_ref[d, rows, :], NT_DIMS,
                                      preferred_element_type=F32)
                att = jnp.where(msk_ref[d] > 0.5, att, 0.0)
                att_sum = att if att_sum is None else att_sum + att
                kd_exp = jnp.concatenate([kd_ref[d, rows, :]] * n_chunk, axis=1) * mexp_ref[...]
                kv_all = jnp.dot(vt, kd_exp, preferred_element_type=F32)
                for c in range(n_chunk):
                    kv_ref[d, blk * n_chunk + c] = kv_all[:, c * LANES:(c + 1) * LANES]
            acc_ref[rows, :] = jnp.dot(att_sum.astype(BF16), vb, preferred_element_type=F32)
        return 0

    lax.fori_loop(0, n_blk // blocks_per_trip, intra, 0)

    unroll = 4

    def states(i, sts):
        sts = list(sts)
        for u in range(unroll):
            n = i * unroll + u
            for d in range(2):
                c = n if d == 0 else n_all - 1 - n
                st_ref[d, c] = sts[d].astype(BF16)
                dec = jnp.concatenate([dec_ref[d, c]] * (DK_A // 8), axis=0)
                sts[d] = sts[d] * dec + kv_ref[d, c]
        return tuple(sts)

    def block_states(blk, _):
        for d in range(2):
            st = jnp.zeros((DK_A, DK_A), F32)
            for cc in range(n_chunk):
                c = blk * n_chunk + (cc if d == 0 else n_chunk - 1 - cc)
                st_ref[d, c] = st.astype(BF16)
                dec = jnp.concatenate([dec_ref[d, c]] * (DK_A // 8), axis=0)
                st = st * dec + kv_ref[d, c]
            sfin_ref[blk, d, 0] = st.T
        return 0

    if per_block:
        lax.fori_loop(0, n_blk, block_states, 0)
    else:
        st0 = (s0_ref[0, 0, 0].T, s0_ref[0, 1, 0].T)
        lax.fori_loop(0, n_all // unroll, states, st0)

    def finish(blk, _):
        rows = pl.ds(pl.multiple_of(blk * R, R), R)
        tot = acc_ref[rows, :]
        for d in range(2):
            pieces = []
            for c in range(n_chunk):
                crow = pl.ds(pl.multiple_of(blk * R + c * C, C), C)
                pieces.append(lax.dot_general(qd_ref[d, crow, :], st_ref[d, blk * n_chunk + c], NT_DIMS,
                                              preferred_element_type=F32))
            tot = tot + jnp.concatenate(pieces, axis=0)
        var = jnp.mean(tot * tot, axis=-1, keepdims=True)
        y = tot * lax.rsqrt(var + EPS) * gain
        o_ref[0, rows, :] = (y * _silu(g_ref[0, rows, :])).astype(o_ref.dtype)
        return 0

    lax.fori_loop(0, n_blk, finish, 0)


def _hgrn(ya, lgt, layer, gain, s0):
    B, T, width = ya.shape
    per_block = s0 is None
    if per_block:
        assert T == HGRN_ROWS
        group = 8 if B % 8 == 0 else 1
        n_seq, B, T = B, B // group, group * T
        ya = ya.reshape(B, T, width)

    def col(k):
        return pl.BlockSpec((1, T, LANES), lambda b, h, k=k: (b, 0, k * H_A + h))

    in_specs = [col(0), col(1), col(2), col(3), col(4),
                pl.BlockSpec((2, lgt.shape[1], LANES), lambda b, h: (0, 0, h)),
                pl.BlockSpec((1, 1, LANES), lambda b, h: (h, 0, 0))]
    args = [ya, ya, ya, ya, ya, lgt, gain]
    out_shape = [jax.ShapeDtypeStruct((B, T, H_A * DK_A), BF16)]
    out_specs = [pl.BlockSpec((1, T, LANES), lambda b, h: (b, 0, h))]
    if per_block:
        out_shape.append(jax.ShapeDtypeStruct((n_seq, 2, H_A, DK_A, DK_A), F32))
        out_specs.append(pl.BlockSpec((T // HGRN_ROWS, 2, 1, DK_A, DK_A), lambda b, h: (b, 0, h, 0, 0)))
    else:
        in_specs.append(pl.BlockSpec((1, 2, 1, DK_A, DK_A), lambda b, h: (b, 0, h, 0, 0)))
        args.append(s0)
    res = pl.pallas_call(
        functools.partial(_hgrn_kernel, seq=T, layer=layer, per_block=per_block),
        out_shape=out_shape,
        grid=(B, H_A),
        in_specs=in_specs,
        out_specs=out_specs,
        scratch_shapes=[pltpu.VMEM((T, LANES), F32),
                        pltpu.VMEM((2, T, LANES), BF16),
                        pltpu.VMEM((2, T, LANES), BF16),
                        pltpu.VMEM((2, T, LANES), BF16),
                        pltpu.VMEM((2, T // HGRN_CHUNK, DK_A, DK_A), F32),
                        pltpu.VMEM((2, T // HGRN_CHUNK, DK_A, DK_A), BF16),
                        pltpu.VMEM((2, T // HGRN_CHUNK, 8, LANES), F32),
                        pltpu.VMEM((2, HGRN_ROWS, HGRN_ROWS), BF16),
                        pltpu.VMEM((2, HGRN_ROWS, HGRN_ROWS), F32),
                        pltpu.VMEM((HGRN_ROWS, HGRN_ROWS // HGRN_CHUNK * LANES), BF16)],
        compiler_params=_cparams(2),
        name="hgrn2",
    )(*args)
    if per_block:
        return res[0].reshape(n_seq, HGRN_ROWS, H_A * DK_A), res[1]
    return res[0], None


def _head_masks():
    lane = lax.broadcasted_iota(jnp.int32, (1, LANES), 1)
    return lane < DH_B, lane >= DH_B


def _ctx_attn_kernel(q_ref, k_ref, v_ref, g_ref, kv_ref, o_ref, newk_ref, newv_ref):
    scale = DH_B ** -0.5
    masks = _head_masks()
    T = q_ref.shape[1]
    for h in range(H_B):
        newk_ref[0, h] = kv_ref[0, :, h * DH_B:(h + 1) * DH_B]
        newv_ref[0, h] = kv_ref[0, :, (H_B + h) * DH_B:(H_B + h + 1) * DH_B]
    for p in range(H_B // 2):
        cols = slice(p * LANES, (p + 1) * LANES)
        q = q_ref[0, :, cols] * scale
        qs = jnp.concatenate([jnp.where(masks[h], q, jnp.zeros_like(q)) for h in range(2)], axis=0)
        s = lax.dot_general(qs, k_ref[0, :, cols], NT_DIMS, preferred_element_type=F32)
        e = jnp.exp(s - jnp.max(s, axis=-1, keepdims=True))
        pr = e / jnp.sum(e, axis=-1, keepdims=True)
        o = jnp.dot(pr.astype(BF16), v_ref[0, :, cols], preferred_element_type=F32)
        o = jnp.where(masks[0], o[0:T], o[T:2 * T])
        o_ref[0, :, cols] = (o * _silu(g_ref[0, :, cols].astype(F32))).astype(o_ref.dtype)


def _ctx_attn(yb, kv):
    B, T, _ = yb.shape
    width = H_B * DH_B

    def col(k):
        return pl.BlockSpec((1, T, width), lambda b, k=k: (b, 0, k))

    cache = pl.BlockSpec((1, H_B, T, DH_B), lambda b: (b, 0, 0, 0))
    return pl.pallas_call(
        _ctx_attn_kernel,
        out_shape=[jax.ShapeDtypeStruct((B, T, width), BF16),
                   jax.ShapeDtypeStruct((B, H_B, T, DH_B), F32),
                   jax.ShapeDtypeStruct((B, H_B, T, DH_B), F32)],
        grid=(B,),
        in_specs=[col(0), col(1), col(2), col(3), pl.BlockSpec((1, T, 2 * width), lambda b: (b, 0, 0))],
        out_specs=[pl.BlockSpec((1, T, width), lambda b: (b, 0, 0)), cache, cache],
        compiler_params=_cparams(1),
        name="ctx_attn",
    )(yb, yb, yb, yb, kv)


N_DR = 2 * NA_KH - 1
N_DC = 2 * NA_KW - 1
N_TAB = N_DR - 1


def _nat_kernel(rb_ref, q_ref, k_ref, v_ref, g_ref, kc_ref, vc_ref, o_ref,
                tab_ref, qs_ref, s_ref, p_ref, r_ref, oc_ref, *, rows):
    p = pl.program_id(0)
    scale = DH_B ** -0.5
    kh = min(NA_KH, rows)
    masks = _head_masks()

    @pl.when(pl.program_id(1) == 0)
    def _build_tables():
        c = lax.broadcasted_iota(jnp.int32, (GRID_W, LANES), 0)
        lane = lax.broadcasted_iota(jnp.int32, (GRID_W, LANES), 1)
        kcol = lane & (GRID_W - 1)
        upper = lane >= GRID_W
        ws = jnp.clip(c - NA_KW // 2, 0, GRID_W - NA_KW)
        neg = jnp.full((GRID_W, LANES), NEG_INF, F32)
        diag = kcol - c + (NA_KW - 1)
        for h in range(2):
            base = (2 * p + h) * (N_DR * N_DC)

            def per_dr(i, _, base=base, h=h):
                def per_dc(dd, acc):
                    lo = rb_ref[base + i * N_DC + dd]
                    hi = rb_ref[base + (i + 1) * N_DC + dd]
                    return jnp.where(diag == dd, jnp.where(upper, hi, lo), acc)

                acc = lax.fori_loop(0, N_DC, per_dc, neg)
                acc = jnp.where(kcol >= ws, jnp.where(kcol < ws + NA_KW, acc, neg), neg)
                tab_ref[h, i] = acc
                return 0

            lax.fori_loop(0, N_TAB, per_dr, 0)

    kc = jnp.concatenate([kc_ref[0, 0], kc_ref[0, 1]], axis=1).astype(BF16)
    vc = jnp.concatenate([vc_ref[0, 0], vc_ref[0, 1]], axis=1).astype(BF16)
    n_keys = kh * GRID_W
    n_ctx = kc.shape[0]
    G = NA_GROUP
    W2 = 2 * GRID_W

    def group(gi, _):
        r_first = gi * G
        q0 = pl.multiple_of(r_first * GRID_W, G * GRID_W)
        for i in range(G):
            qi = q_ref[0, pl.ds(q0 + i * GRID_W, GRID_W), :] * scale
            for h in range(2):
                qs_ref[i * W2 + h * GRID_W:i * W2 + (h + 1) * GRID_W, :] = jnp.where(
                    masks[h], qi, jnp.zeros_like(qi))
        s_ref[:, n_keys:n_keys + n_ctx] = lax.dot_general(qs_ref[...], kc, NT_DIMS,
                                                          preferred_element_type=F32)
        windows = []
        for i in range(G):
            r = r_first + i
            rs = jnp.clip(r - kh // 2, 0, rows - kh)
            k0 = pl.multiple_of(rs * GRID_W, GRID_W)
            windows.append(k0)
            dr0 = rs - r + (NA_KH - 1)
            bias = jnp.concatenate(
                [jnp.concatenate([tab_ref[h, dr0 + 2 * m] for m in range(kh // 2)], axis=1)
                 for h in range(2)], axis=0)
            s_ref[i * W2:(i + 1) * W2, 0:n_keys] = lax.dot_general(
                qs_ref[i * W2:(i + 1) * W2, :], k_ref[0, pl.ds(k0, n_keys), :], NT_DIMS,
                preferred_element_type=F32) + bias
        for i in range(G):
            s = s_ref[i * W2:(i + 1) * W2, :]
            e = jnp.exp(s - jnp.max(s, axis=-1, keepdims=True))
            p_ref[i * W2:(i + 1) * W2, :] = e.astype(BF16)
            rinv = 1.0 / jnp.sum(e, axis=-1, keepdims=True)
            r_ref[i * W2:(i + 1) * W2, :] = jnp.broadcast_to(rinv, (W2, LANES))
        oc_ref[...] = jnp.dot(p_ref[:, n_keys:n_keys + n_ctx], vc, preferred_element_type=F32)
        for i in range(G):
            o = jnp.dot(p_ref[i * W2:(i + 1) * W2, 0:n_keys], v_ref[0, pl.ds(windows[i], n_keys), :],
                        preferred_element_type=F32)
            o = (o + oc_ref[i * W2:(i + 1) * W2, :]) * r_ref[i * W2:(i + 1) * W2, :]
            o = jnp.where(masks[0], o[0:GRID_W], o[GRID_W:W2])
            out_rows = pl.ds(q0 + i * GRID_W, GRID_W)
            gate = g_ref[0, out_rows, :].astype(F32)
            o_ref[0, out_rows, :] = (o * _silu(gate)).astype(o_ref.dtype)
        return 0

    lax.fori_loop(0, rows // G, group, 0)


def _nat(yb, kc, vc, rel_bias):
    B, T, _ = yb.shape
    Tc = kc.shape[2]
    n_pair = H_B // 2
    rows = T // GRID_W
    n_stack = NA_GROUP * 2 * GRID_W
    n_keys = min(NA_KH, rows) * GRID_W

    def col(k):
        return pl.BlockSpec((1, T, LANES), lambda p, b, k=k: (b, 0, k * n_pair + p))

    ctx = pl.BlockSpec((1, 2, Tc, DH_B), lambda p, b: (b, p, 0, 0))
    return pl.pallas_call(
        functools.partial(_nat_kernel, rows=rows),
        out_shape=jax.ShapeDtypeStruct((B, T, H_B * DH_B), BF16),
        grid=(n_pair, B),
        in_specs=[pl.BlockSpec(memory_space=pltpu.SMEM), col(0), col(1), col(2), col(3), ctx, ctx],
        out_specs=pl.BlockSpec((1, T, LANES), lambda p, b: (b, 0, p)),
        scratch_shapes=[pltpu.VMEM((2, N_TAB, GRID_W, LANES), F32),
                        pltpu.VMEM((n_stack, LANES), BF16),
                        pltpu.VMEM((n_stack, n_keys + Tc), F32),
                        pltpu.VMEM((n_stack, n_keys + Tc), BF16),
                        pltpu.VMEM((n_stack, LANES), F32),
                        pltpu.VMEM((n_stack, LANES), F32)],
        compiler_params=_cparams(2),
        name="nbr_attn",
    )(rel_bias.reshape(-1), yb, yb, yb, yb, kc, vc)


def _seg_len(seq):
    length = -(-seq // N_SEG)
    while length % 8 != 4:
        length += 1
    return length


def _step_block(seg_len):
    return max(d for d in range(1, seg_len + 1) if seg_len % d == 0 and d * N_SEG <= RG_ROWS * 3 // 2)


def _rglru_kernel(x_ref, g_ref, cw_ref, cb_ref, wg_ref, lam_ref, *rest, seq, has_s0, emit_state):
    rest = list(rest)
    s0_ref = rest.pop(0) if has_s0 else None
    o_ref = rest.pop(0)
    hfin_ref = rest.pop(0) if emit_state else None
    xpad_ref, a_ref, u_ref, h_ref, p_ref, hn_ref = rest
    L = _seg_len(seq)
    n_rows = N_SEG * L
    RB = RG_ROWS
    CB = RG_SLABS
    TB = _step_block(L)
    n_tile = N_SEG // 8
    chains = [(d, j, s) for d in range(2) for j in range(CB) for s in range(n_tile)]

    for j in range(CB):
        xpad_ref[j, 0:8, :] = jnp.zeros((8, LANES), F32)
        xpad_ref[j, 8:seq + 8, :] = x_ref[0, :, j * LANES:(j + 1) * LANES]
        xpad_ref[j, seq + 8:n_rows + 16, :] = jnp.zeros((n_rows + 8 - seq, LANES), F32)

    nl = -lam_ref[...]
    sp = jnp.maximum(nl, 0.0) + jnp.log1p(jnp.exp(-jnp.abs(nl)))
    cw = cw_ref[...]
    cbias = cb_ref[...]
    ones2 = jnp.where(lax.broadcasted_iota(jnp.int32, (TB * N_SEG, LANES), 1) < 2, 1.0, 0.0).astype(BF16)

    def gates(blk, _):
        t0 = blk * TB
        r0 = pl.multiple_of(blk * (TB * N_SEG), TB * N_SEG)
        for j in range(CB):
            lanes = slice(j * LANES, (j + 1) * LANES)
            tiles = []
            for tt in range(TB):
                for s in range(n_tile):
                    taps = [xpad_ref[j, pl.ds(6 + k + t0 + tt + s * 8 * L, 8, stride=L), :] for k in range(4)]
                    xt = cw[0:1, lanes] * taps[0] + cw[1:2, lanes] * taps[1]
                    xt = xt + cw[2:3, lanes] * taps[2]
                    tiles.append(xt + cw[3:4, lanes] * taps[3] + cbias[:, lanes])
            xj = jnp.concatenate(tiles, axis=0)
            gt = jnp.dot(jnp.concatenate([xj.astype(BF16), ones2], axis=1), wg_ref[j],
                         preferred_element_type=F32)
            xh = 0.5 * xj
            for d in range(2):
                th_r = jnp.tanh(gt[:, (2 * d) * LANES:(2 * d + 1) * LANES])
                th_i = jnp.tanh(gt[:, (2 * d + 1) * LANES:(2 * d + 2) * LANES])
                half = (-0.5 * RG_C) * sp[d:d + 1, lanes]
                la = half + half * th_r
                a = jnp.exp(la)
                y = -jnp.tanh(la) * (1.0 + a * a)
                root = jnp.where(y > 0.0, y * lax.rsqrt(y), 0.0)
                a_ref[d, j, pl.ds(r0, TB * N_SEG), :] = a
                u_ref[d, j, pl.ds(r0, TB * N_SEG), :] = root * (xh + xh * th_i)
        return 0

    lax.fori_loop(0, L // TB, gates, 0)

    first_pad = [[min(max(seq - (s * 8 + r) * L, 0), L) for r in range(8)] for s in range(n_tile)]
    sub = lax.broadcasted_iota(jnp.int32, (8, LANES), 0)
    pad_from = []
    for s in range(n_tile):
        if all(f == L for f in first_pad[s]):
            pad_from.append(None)
        else:
            vec = jnp.full((8, LANES), L, jnp.int32)
            for r in range(8):
                vec = jnp.where(sub == r, first_pad[s][r], vec)
            pad_from.append(vec)

    def step_rows(step, s):
        return pl.ds(pl.multiple_of(step * N_SEG + s * 8, 8), 8)

    unroll = 4

    def scan(i, carry):
        carry = list(carry)
        for k in range(unroll):
            for n, (d, j, s) in enumerate(chains):
                h, pr = carry[n]
                t = i * unroll + k
                if d == 1:
                    t = L - 1 - t
                idx = step_rows(t, s)
                a = a_ref[d, j, idx, :]
                u = u_ref[d, j, idx, :]
                if pad_from[s] is not None:
                    live = t < pad_from[s]
                    a = jnp.where(live, a, 1.0)
                    u = jnp.where(live, u, 0.0)
                h = a * h + u
                pr = pr * a
                h_ref[d, j, idx, :] = h
                p_ref[d, j, idx, :] = pr
                carry[n] = (h, pr)
        return tuple(carry)

    zero = jnp.zeros((8, LANES), F32)
    one = jnp.ones((8, LANES), F32)
    ends = lax.fori_loop(0, L // unroll, scan, ((zero, one),) * len(chains))

    cins = {}
    finals = [[None] * CB for _ in range(2)]
    for d in range(2):
        for j in range(CB):
            if has_s0:
                c = s0_ref[0, d:d + 1, j * LANES:(j + 1) * LANES]
            else:
                c = jnp.zeros((1, LANES), F32)
            cin = [None] * N_SEG
            for kk in range(N_SEG):
                seg = kk if d == 0 else N_SEG - 1 - kk
                s, row = divmod(seg, 8)
                h_end, p_end = ends[chains.index((d, j, s))]
                cin[seg] = c
                c = h_end[row:row + 1, :] + p_end[row:row + 1, :] * c
            finals[d][j] = c
            for s in range(n_tile):
                cins[(d, j, s)] = jnp.concatenate(cin[s * 8:(s + 1) * 8], axis=0)

    def fix(i, _):
        for k in range(unroll):
            t = i * unroll + k
            for j in range(CB):
                for s in range(n_tile):
                    idx = step_rows(t, s)
                    parts = [h_ref[d, j, idx, :] + p_ref[d, j, idx, :] * cins[(d, j, s)] for d in range(2)]
                    hn_ref[j, pl.ds(t + s * 8 * L, 8, stride=L), :] = parts[0] + parts[1]
        return 0

    lax.fori_loop(0, L // unroll, fix, 0)

    if emit_state:
        hfin_ref[0] = jnp.concatenate([jnp.concatenate(finals[d], axis=1) for d in range(2)], axis=0)

    def combine(blk, _):
        rows = pl.ds(pl.multiple_of(blk * RB, RB), RB)
        hs = jnp.concatenate([hn_ref[j, rows, :] for j in range(CB)], axis=1)
        o_ref[0, rows, :] = (hs * _silu(g_ref[0, rows, :])).astype(o_ref.dtype)
        return 0

    lax.fori_loop(0, seq // RB, combine, 0)


def _gate_weights(gate_w, gate_b):
    w = (0.5 * gate_w).transpose(2, 3, 0, 1, 4).reshape(H_C, BW_C, 4 * BW_C).astype(BF16)
    b = (0.5 * gate_b).reshape(2, 2, H_C, BW_C).transpose(2, 0, 1, 3).reshape(H_C, 1, 4 * BW_C).astype(F32)
    hi = b.astype(BF16)
    lo = (b - hi.astype(F32)).astype(BF16)
    zeros = jnp.zeros((H_C, BW_C - 2, 4 * BW_C), BF16)
    return jnp.concatenate([w, hi, lo, zeros], axis=1)


def _rglru(xg, conv_w, conv_b, wg, lam, s0, emit_state):
    B, T, _ = xg.shape
    has_s0 = s0 is not None
    n_rows = N_SEG * _seg_len(T)
    CB = RG_SLABS
    wide = CB * LANES
    n_steps = H_C // CB
    in_specs = [
        pl.BlockSpec((1, T, wide), lambda b, c: (b, 0, c)),
        pl.BlockSpec((1, T, wide), lambda b, c: (b, 0, n_steps + c)),
        pl.BlockSpec((4, wide), lambda b, c: (0, c)),
        pl.BlockSpec((1, wide), lambda b, c: (0, c)),
        pl.BlockSpec((CB, 2 * BW_C, 4 * BW_C), lambda b, c: (c, 0, 0)),
        pl.BlockSpec((2, wide), lambda b, c: (0, c)),
    ]
    args = [xg, xg, conv_w, conv_b, wg, lam]
    if has_s0:
        in_specs.append(pl.BlockSpec((1, 2, wide), lambda b, c: (b, 0, c)))
        args.append(s0)
    out_shape = [jax.ShapeDtypeStruct((B, T, W_C), BF16)]
    out_specs = [pl.BlockSpec((1, T, wide), lambda b, c: (b, 0, c))]
    if emit_state:
        out_shape.append(jax.ShapeDtypeStruct((B, 2, W_C), F32))
        out_specs.append(pl.BlockSpec((1, 2, wide), lambda b, c: (b, 0, c)))
    res = pl.pallas_call(
        functools.partial(_rglru_kernel, seq=T, has_s0=has_s0, emit_state=emit_state),
        out_shape=out_shape,
        grid=(B, n_steps),
        in_specs=in_specs,
        out_specs=out_specs,
        scratch_shapes=[pltpu.VMEM((CB, n_rows + 16, LANES), F32)]
        + [pltpu.VMEM((2, CB, n_rows, LANES), F32)] * 4 + [pltpu.VMEM((CB, n_rows, LANES), F32)],
        compiler_params=_cparams(2),
        name="rglru",
    )(*args)
    return res if emit_state else (res[0], None)


A_COLS = 5 * H_A * DK_A
B_COLS = 4 * H_B * DH_B


def kernel(x_prompt, x_sample, state_hgrn, cache_na_k, cache_na_v, state_rglru, c, c_ctx, norm_gain, w_mod, b_mod, w_in_even, w_out_even, hgrn_lb_logits, hgrn_out_gain, na_rel_bias, w_in_odd, w_out_odd, conv_w, conv_b, rg_gate_w, rg_gate_b, rg_lambda, final_gain):
    n_ctx = x_prompt.shape[0]
    n_lat = x_sample.shape[0]
    depth = w_mod.shape[0]

    cond = jnp.zeros((16, D_MODEL), F32).at[0].set(c_ctx).at[1:1 + n_lat].set(c)
    mod = _modulation(cond, w_mod, b_mod.reshape(depth, 1, 3 * D_MODEL))
    mod = mod.reshape(depth, 16, 3, D_MODEL)

    t_ctx = x_prompt.shape[1]

    def flat(a):
        return a.reshape(1, n_ctx * t_ctx, a.shape[-1])

    def unflat(a):
        return a.reshape(n_ctx, t_ctx, a.shape[-1])

    def in_proj_params(l):
        if l % 2 == 0:
            outs_s = ((0, A_COLS, F32), (A_COLS, B_COLS, BF16))
            outs_c = outs_s + ((A_COLS + H_B * DH_B, 2 * H_B * DH_B, F32),)
            a_key = H_A * DK_A
            col = jnp.arange(w_in_even.shape[-1])
            halve = jnp.where((col >= a_key) & (col < 3 * a_key), 0.5, 1.0).astype(F32)
            return (w_in_even[l // 2] * halve).astype(BF16), outs_c, outs_s
        outs = ((0, 2 * W_C, F32),)
        return w_in_odd[l // 2].astype(BF16), outs, outs

    xc, xs = x_prompt, x_sample
    new_hgrn, new_k, new_v, new_rg = [], [], [], []
    proj_c = proj_s = None
    for l in range(depth):
        j = l // 2
        mod_c, mod_s = mod[l, 0:1], mod[l, 1:1 + n_lat]
        if proj_c is None:
            gain = norm_gain[l].reshape(1, D_MODEL)
            w_in, outs_c, outs_s = in_proj_params(l)
            proj_c = [unflat(t) for t in _inproj(flat(xc), mod_c, gain, w_in, outs_c, 512, True)]
            proj_s = _inproj(xs, mod_s, gain, w_in, outs_s, 512, False)
        if l % 2 == 0:
            w_out = w_out_even[j].astype(BF16)
            (ya_c, yb_c, kv_c), (ya_s, yb_s) = proj_c, proj_s
            hgain = hgrn_out_gain[j].reshape(H_A, 1, DK_A)
            oa_c, s_fin = _hgrn(ya_c, hgrn_lb_logits, j, hgain, None)
            oa_s, _ = _hgrn(ya_s, hgrn_lb_logits, j, hgain, state_hgrn[:, j])
            ob_c, k_c, v_c = _ctx_attn(yb_c, kv_c)
            ob_s = _nat(yb_s, cache_na_k[:, j], cache_na_v[:, j], na_rel_bias[j])
            ys_c, ys_s = (oa_c, ob_c), (oa_s, ob_s)
            new_hgrn.append(s_fin)
            new_k.append(k_c)
            new_v.append(v_c)
        else:
            w_out = w_out_odd[j].astype(BF16)
            (xg_c,), (xg_s,) = proj_c, proj_s
            wg = _gate_weights(rg_gate_w[j], rg_gate_b[j])
            cb = conv_b[j].reshape(1, W_C)
            y_c, h_fin = _rglru(xg_c, conv_w[j], cb, wg, rg_lambda[j], None, True)
            y_s, _ = _rglru(xg_s, conv_w[j], cb, wg, rg_lambda[j], state_rglru[:, j], False)
            ys_c, ys_s = (y_c,), (y_s,)
            new_rg.append(h_fin)
        ys_c = tuple(flat(y) for y in ys_c)
        if l == depth - 1:
            fgain = final_gain.reshape(1, D_MODEL)
            (xc,) = _outproj(ys_c, flat(xc), mod_c, w_out, 1024, True, final_gain=fgain)
            (xs,) = _outproj(ys_s, xs, mod_s, w_out, 1024, False, final_gain=fgain)
            xc = unflat(xc)
        else:
            gain_n = norm_gain[l + 1].reshape(1, D_MODEL)
            w_n, outs_c, outs_s = in_proj_params(l + 1)
            mod_cn, mod_sn = mod[l + 1, 0:1], mod[l + 1, 1:1 + n_lat]
            xc, *proj_c = _outproj(ys_c, flat(xc), mod_c, w_out, 512, True,
                                   next_proj=(mod_cn, gain_n, w_n, outs_c))
            xs, *proj_s = _outproj(ys_s, xs, mod_s, w_out, 512, False,
                                   next_proj=(mod_sn, gain_n, w_n, outs_s))
            xc = unflat(xc)
            proj_c = [unflat(t) for t in proj_c]
    return (xc, xs, jnp.stack(new_hgrn, axis=1), jnp.stack(new_k, axis=1),
            jnp.stack(new_v, axis=1), jnp.stack(new_rg, axis=1))
```

```python
import functools

import jax
import jax.numpy as jnp
from jax import lax
from jax.experimental import pallas as pl
from jax.experimental.pallas import tpu as pltpu

F32 = jnp.float32
BF16 = jnp.bfloat16

D_MODEL = 1024
EPS = 1e-6
NEG_INF = -1e30
H_A = 4
DK_A = 128
HGRN_CHUNK = 32
HGRN_ROWS = 256
H_B = 8
DH_B = 64
GRID_W = 64
NA_KH = 8
NA_KW = 16
NA_GROUP = 8
W_C = 1024
H_C = 8
BW_C = W_C // H_C
RG_C = 8.0
RG_ROWS = 256
RG_SLABS = 2
N_SEG = 16
LANES = 128
VMEM_LIMIT = 48 * 1024 * 1024

NT_DIMS = (((1,), (1,)), ((), ()))
TN_DIMS = (((0,), (0,)), ((), ()))


def _silu(x):
    half = 0.5 * x
    return half + half * jnp.tanh(half)


def _cparams(n_axes):
    return pltpu.CompilerParams(dimension_semantics=("arbitrary",) * n_axes,
                                vmem_limit_bytes=VMEM_LIMIT)


def _mod_kernel(cond_ref, w_ref, b_ref, o_ref):
    s = _silu(cond_ref[...])
    o_ref[0] = jnp.dot(s.astype(BF16), w_ref[0].astype(BF16), preferred_element_type=F32) + b_ref[0]


def _modulation(cond, w_mod, b_mod):
    depth = w_mod.shape[0]
    n_rows = cond.shape[0]
    return pl.pallas_call(
        _mod_kernel,
        out_shape=jax.ShapeDtypeStruct((depth, n_rows, 3 * D_MODEL), F32),
        grid=(depth, 3),
        in_specs=[
            pl.BlockSpec((n_rows, D_MODEL), lambda l, n: (0, 0)),
            pl.BlockSpec((1, D_MODEL, D_MODEL), lambda l, n: (l, 0, n)),
            pl.BlockSpec((1, 1, D_MODEL), lambda l, n: (l, 0, n)),
        ],
        out_specs=pl.BlockSpec((1, n_rows, D_MODEL), lambda l, n: (l, 0, n)),
        compiler_params=_cparams(2),
        name="adaln_mod",
    )(cond, w_mod, b_mod)


def _project(x, mod_ref, gain_ref, w_ref, out_refs, outs):
    var = jnp.mean(x * x, axis=-1, keepdims=True)
    y = x * lax.rsqrt(var + EPS) * gain_ref[...]
    h = y * (1.0 + mod_ref[0, 1:2, :]) + mod_ref[0, 0:1, :]
    hb = h.astype(BF16)
    step = 512
    for c in range(0, w_ref.shape[1], step):
        users = [(o_ref, c - col0) for o_ref, (col0, width, _) in zip(out_refs, outs)
                 if col0 <= c < col0 + width]
        if users:
            r = jnp.dot(hb, w_ref[:, c:c + step], preferred_element_type=F32)
            for o_ref, off in users:
                o_ref[0, :, off:off + step] = r.astype(o_ref.dtype)


def _inproj_kernel(x_ref, mod_ref, gain_ref, w_ref, *out_refs, outs):
    _project(x_ref[0], mod_ref, gain_ref, w_ref, out_refs, outs)


def _inproj(x, mod, gain, w, outs, tm, shared_mod):
    B, T, _ = x.shape
    n_cols = w.shape[1]
    mod_map = (lambda b, t: (0, 0, 0)) if shared_mod else (lambda b, t: (b, 0, 0))
    return pl.pallas_call(
        functools.partial(_inproj_kernel, outs=outs),
        out_shape=[jax.ShapeDtypeStruct((B, T, wd), dt) for _, wd, dt in outs],
        grid=(B, T // tm),
        in_specs=[
            pl.BlockSpec((1, tm, D_MODEL), lambda b, t: (b, t, 0)),
            pl.BlockSpec((1, 3, D_MODEL), mod_map),
            pl.BlockSpec((1, D_MODEL), lambda b, t: (0, 0)),
            pl.BlockSpec((D_MODEL, n_cols), lambda b, t: (0, 0)),
        ],
        out_specs=[pl.BlockSpec((1, tm, wd), lambda b, t: (b, t, 0)) for _, wd, _ in outs],
        compiler_params=_cparams(2),
        name="in_proj",
    )(x, mod, gain, w)


def _outproj_kernel(*refs, n_y, final, next_outs):
    y_refs, (x_ref, mod_ref, w_ref), rest = refs[:n_y], refs[n_y:n_y + 3], refs[n_y + 3:]
    m = None
    row = 0
    for y_ref in y_refs:
        width = y_ref.shape[-1]
        part = jnp.dot(y_ref[0], w_ref[row:row + width, :], preferred_element_type=F32)
        m = part if m is None else m + part
        row += width
    xn = x_ref[0] + mod_ref[0, 2:3, :] * m
    if final:
        gain_ref, o_ref = rest
        var = jnp.mean(xn * xn, axis=-1, keepdims=True)
        o_ref[0] = xn * lax.rsqrt(var + EPS) * gain_ref[...]
    else:
        modn_ref, gainn_ref, wn_ref, o_ref = rest[:4]
        o_ref[0] = xn
        _project(xn, modn_ref, gainn_ref, wn_ref, rest[4:], next_outs)


def _outproj(ys, x, mod, w, tm, shared_mod, final_gain=None, next_proj=None):
    B, T, _ = x.shape
    final = final_gain is not None
    mod_map = (lambda b, t: (0, 0, 0)) if shared_mod else (lambda b, t: (b, 0, 0))
    row_block = pl.BlockSpec((1, tm, D_MODEL), lambda b, t: (b, t, 0))
    vec = pl.BlockSpec((1, D_MODEL), lambda b, t: (0, 0))
    in_specs = [pl.BlockSpec((1, tm, y.shape[-1]), lambda b, t: (b, t, 0)) for y in ys] + [
        row_block,
        pl.BlockSpec((1, 3, D_MODEL), mod_map),
        pl.BlockSpec((w.shape[0], D_MODEL), lambda b, t: (0, 0)),
    ]
    args = list(ys) + [x, mod, w]
    out_shape = [jax.ShapeDtypeStruct((B, T, D_MODEL), F32)]
    out_specs = [row_block]
    next_outs = None
    if final:
        in_specs.append(vec)
        args.append(final_gain)
    else:
        mod_n, gain_n, w_n, next_outs = next_proj
        in_specs += [pl.BlockSpec((1, 3, D_MODEL), mod_map), vec,
                     pl.BlockSpec((D_MODEL, w_n.shape[1]), lambda b, t: (0, 0))]
        args += [mod_n, gain_n, w_n]
        out_shape += [jax.ShapeDtypeStruct((B, T, wd), dt) for _, wd, dt in next_outs]
        out_specs += [pl.BlockSpec((1, tm, wd), lambda b, t: (b, t, 0)) for _, wd, _ in next_outs]
    return pl.pallas_call(
        functools.partial(_outproj_kernel, n_y=len(ys), final=final, next_outs=next_outs),
        out_shape=out_shape,
        grid=(B, T // tm),
        in_specs=in_specs,
        out_specs=out_specs,
        compiler_params=_cparams(2),
        name="out_proj",
    )(*args)


def _hgrn_kernel(q_ref, zf_ref, zb_ref, v_ref, g_ref, lgt_ref, gain_ref, *rest, seq, layer, per_block):
    rest = list(rest)
    s0_ref = None if per_block else rest.pop(0)
    o_ref = rest.pop(0)
    sfin_ref = rest.pop(0) if per_block else None
    acc_ref, qd_ref, ki_ref, kd_ref, kv_ref, st_ref, dec_ref, mst_ref, msk_ref, mexp_ref = rest
    R = HGRN_ROWS
    C = HGRN_CHUNK
    n_blk = seq // R
    n_chunk = R // C
    n_all = seq // C

    @pl.when((pl.program_id(0) == 0) & (pl.program_id(1) == 0))
    def _build_masks():
        ti = lax.broadcasted_iota(jnp.int32, (R, R), 0)
        tj = lax.broadcasted_iota(jnp.int32, (R, R), 1)
        shift = C.bit_length() - 1
        same = lax.shift_right_logical(ti, shift) == lax.shift_right_logical(tj, shift)
        one = jnp.ones((R, R), F32)
        zero = jnp.zeros((R, R), F32)
        incl = (jnp.where(same, jnp.where(tj <= ti, one, zero), zero),
                jnp.where(same, jnp.where(tj >= ti, one, zero), zero))
        for d in range(2):
            msk_ref[d] = incl[d]
            mst_ref[d] = incl[d].astype(BF16)
        rr = lax.broadcasted_iota(jnp.int32, (R, n_chunk * LANES), 0)
        cc = lax.broadcasted_iota(jnp.int32, (R, n_chunk * LANES), 1)
        own = lax.shift_right_logical(rr, shift) == lax.shift_right_logical(cc, LANES.bit_length() - 1)
        mexp_ref[...] = jnp.where(own, 1.0, 0.0).astype(BF16)

    lgt = [lgt_ref[:, i, :] for i in range(lgt_ref.shape[1])]
    lmax = functools.reduce(jnp.maximum, lgt)
    ex = [jnp.exp(t - lmax) for t in lgt]
    lb_all = sum(ex[:layer + 1]) / sum(ex)
    gain = gain_ref[0]

    blocks_per_trip = 2 if n_blk % 2 == 0 else 1

    def gates(i, _):
        for u in range(blocks_per_trip):
            blk = i * blocks_per_trip + u
            rows = pl.ds(pl.multiple_of(blk * R, R), R)
            q = q_ref[0, rows, :]
            for d in range(2):
                th = jnp.tanh((zf_ref if d == 0 else zb_ref)[0, rows, :])
                lb = lb_all[d:d + 1, :]
                c = 0.5 * (1.0 - lb)
                ct = c * th
                f = (lb + c) + ct
                k = c - ct
                logf = jnp.log(f)
                hi = logf.astype(BF16)
                lo = (logf - hi.astype(F32)).astype(BF16)
                cs = jnp.dot(mst_ref[d], jnp.concatenate([hi, lo], axis=1), preferred_element_type=F32)
                b = cs[:, 0:LANES] + cs[:, LANES:2 * LANES]
                ends = [c * C + (C - 1 if d == 0 else 0) for c in range(n_chunk)]
                btot = jnp.concatenate([jnp.broadcast_to(b[t:t + 1, :], (C, LANES)) for t in ends], axis=0)
                qd_ref[d, rows, :] = (q * jnp.exp(b)).astype(BF16)
                ki_ref[d, rows, :] = (k * jnp.exp(-b)).astype(BF16)
                kd_ref[d, rows, :] = (k * jnp.exp(btot - b)).astype(BF16)
                for c in range(n_chunk):
                    dec_ref[d, blk * n_chunk + c] = jnp.exp(btot[c * C:c * C + 8, :])
        return 0

    lax.fori_loop(0, n_blk // blocks_per_trip, gates, 0)

    def intra(i, _):
        for u in range(blocks_per_trip):
            blk = i * blocks_per_trip + u
            rows = pl.ds(pl.multiple_of(blk * R, R), R)
            v = v_ref[0, rows, :]
            vb = v.astype(BF16)
            vt = v.T.astype(BF16)
            att_sum = None
            for d in range(2):
                att = lax.dot_general(qd_ref[d, rows, :], ki_ref[d, rows, :], NT_DIMS,
                                      preferred_element_type=F32)
                att = jnp.where(msk_ref[d] > 0.5, att, 0.0)
                att_sum = att if att_sum is None else att_sum + att
                kd_exp = jnp.concatenate([kd_ref[d, rows, :]] * n_chunk, axis=1) * mexp_ref[...]
                kv_all = jnp.dot(vt, kd_exp, preferred_element_type=F32)
                for c in range(n_chunk):
                    kv_ref[d, blk * n_chunk + c] = kv_all[:, c * LANES:(c + 1) * LANES]
            acc_ref[rows, :] = jnp.dot(att_sum.astype(BF16), vb, preferred_element_type=F32)
        return 0

    lax.fori_loop(0, n_blk // blocks_per_trip, intra, 0)

    unroll = 4

    def states(i, sts):
        sts = list(sts)
        for u in range(unroll):
            n = i * unroll + u
            for d in range(2):
                c = n if d == 0 else n_all - 1 - n
                st_ref[d, c] = sts[d].astype(BF16)
                dec = jnp.concatenate([dec_ref[d, c]] * (DK_A // 8), axis=0)
                sts[d] = sts[d] * dec + kv_ref[d, c]
        return tuple(sts)

    def block_states(blk, _):
        for d in range(2):
            st = jnp.zeros((DK_A, DK_A), F32)
            for cc in range(n_chunk):
                c = blk * n_chunk + (cc if d == 0 else n_chunk - 1 - cc)
                st_ref[d, c] = st.astype(BF16)
                dec = jnp.concatenate([dec_ref[d, c]] * (DK_A // 8), axis=0)
                st = st * dec + kv_ref[d, c]
            sfin_ref[blk, d, 0] = st.T
        return 0

    if per_block:
        lax.fori_loop(0, n_blk, block_states, 0)
    else:
        st0 = (s0_ref[0, 0, 0].T, s0_ref[0, 1, 0].T)
        lax.fori_loop(0, n_all // unroll, states, st0)

    def finish(blk, _):
        rows = pl.ds(pl.multiple_of(blk * R, R), R)
        tot = acc_ref[rows, :]
        for d in range(2):
            pieces = []
            for c in range(n_chunk):
                crow = pl.ds(pl.multiple_of(blk * R + c * C, C), C)
                pieces.append(lax.dot_general(qd_ref[d, crow, :], st_ref[d, blk * n_chunk + c], NT_DIMS,
                                              preferred_element_type=F32))
            tot = tot + jnp.concatenate(pieces, axis=0)
        var = jnp.mean(tot * tot, axis=-1, keepdims=True)
        y = tot * lax.rsqrt(var + EPS) * gain
        o_ref[0, rows, :] = (y * _silu(g_ref[0, rows, :])).astype(o_ref.dtype)
        return 0

    lax.fori_loop(0, n_blk, finish, 0)


def _hgrn(ya, lgt, layer, gain, s0):
    B, T, width = ya.shape
    per_block = s0 is None
    if per_block:
        assert T == HGRN_ROWS
        group = 8 if B % 8 == 0 else 1
        n_seq, B, T = B, B // group, group * T
        ya = ya.reshape(B, T, width)

    def col(k):
        return pl.BlockSpec((1, T, LANES), lambda b, h, k=k: (b, 0, k * H_A + h))

    in_specs = [col(0), col(1), col(2), col(3), col(4),
                pl.BlockSpec((2, lgt.shape[1], LANES), lambda b, h: (0, 0, h)),
                pl.BlockSpec((1, 1, LANES), lambda b, h: (h, 0, 0))]
    args = [ya, ya, ya, ya, ya, lgt, gain]
    out_shape = [jax.ShapeDtypeStruct((B, T, H_A * DK_A), BF16)]
    out_specs = [pl.BlockSpec((1, T, LANES), lambda b, h: (b, 0, h))]
    if per_block:
        out_shape.append(jax.ShapeDtypeStruct((n_seq, 2, H_A, DK_A, DK_A), F32))
        out_specs.append(pl.BlockSpec((T // HGRN_ROWS, 2, 1, DK_A, DK_A), lambda b, h: (b, 0, h, 0, 0)))
    else:
        in_specs.append(pl.BlockSpec((1, 2, 1, DK_A, DK_A), lambda b, h: (b, 0, h, 0, 0)))
        args.append(s0)
    res = pl.pallas_call(
        functools.partial(_hgrn_kernel, seq=T, layer=layer, per_block=per_block),
        out_shape=out_shape,
        grid=(B, H_A),
        in_specs=in_specs,
        out_specs=out_specs,
        scratch_shapes=[pltpu.VMEM((T, LANES), F32),
                        pltpu.VMEM((2, T, LANES), BF16),
                        pltpu.VMEM((2, T, LANES), BF16),
                        pltpu.VMEM((2, T, LANES), BF16),
                        pltpu.VMEM((2, T // HGRN_CHUNK, DK_A, DK_A), F32),
                        pltpu.VMEM((2, T // HGRN_CHUNK, DK_A, DK_A), BF16),
                        pltpu.VMEM((2, T // HGRN_CHUNK, 8, LANES), F32),
                        pltpu.VMEM((2, HGRN_ROWS, HGRN_ROWS), BF16),
                        pltpu.VMEM((2, HGRN_ROWS, HGRN_ROWS), F32),
                        pltpu.VMEM((HGRN_ROWS, HGRN_ROWS // HGRN_CHUNK * LANES), BF16)],
        compiler_params=_cparams(2),
        name="hgrn2",
    )(*args)
    if per_block:
        return res[0].reshape(n_seq, HGRN_ROWS, H_A * DK_A), res[1]
    return res[0], None


def _head_masks():
    lane = lax.broadcasted_iota(jnp.int32, (1, LANES), 1)
    return lane < DH_B, lane >= DH_B


def _ctx_attn_kernel(q_ref, k_ref, v_ref, g_ref, kv_ref, o_ref, newk_ref, newv_ref):
    scale = DH_B ** -0.5
    masks = _head_masks()
    T = q_ref.shape[1]
    for h in range(H_B):
        newk_ref[0, h] = kv_ref[0, :, h * DH_B:(h + 1) * DH_B]
        newv_ref[0, h] = kv_ref[0, :, (H_B + h) * DH_B:(H_B + h + 1) * DH_B]
    for p in range(H_B // 2):
        cols = slice(p * LANES, (p + 1) * LANES)
        q = q_ref[0, :, cols] * scale
        qs = jnp.concatenate([jnp.where(masks[h], q, jnp.zeros_like(q)) for h in range(2)], axis=0)
        s = lax.dot_general(qs, k_ref[0, :, cols], NT_DIMS, preferred_element_type=F32)
        e = jnp.exp(s - jnp.max(s, axis=-1, keepdims=True))
        pr = e / jnp.sum(e, axis=-1, keepdims=True)
        o = jnp.dot(pr.astype(BF16), v_ref[0, :, cols], preferred_element_type=F32)
        o = jnp.where(masks[0], o[0:T], o[T:2 * T])
        o_ref[0, :, cols] = (o * _silu(g_ref[0, :, cols].astype(F32))).astype(o_ref.dtype)


def _ctx_attn(yb, kv):
    B, T, _ = yb.shape
    width = H_B * DH_B

    def col(k):
        return pl.BlockSpec((1, T, width), lambda b, k=k: (b, 0, k))

    cache = pl.BlockSpec((1, H_B, T, DH_B), lambda b: (b, 0, 0, 0))
    return pl.pallas_call(
        _ctx_attn_kernel,
        out_shape=[jax.ShapeDtypeStruct((B, T, width), BF16),
                   jax.ShapeDtypeStruct((B, H_B, T, DH_B), F32),
                   jax.ShapeDtypeStruct((B, H_B, T, DH_B), F32)],
        grid=(B,),
        in_specs=[col(0), col(1), col(2), col(3), pl.BlockSpec((1, T, 2 * width), lambda b: (b, 0, 0))],
        out_specs=[pl.BlockSpec((1, T, width), lambda b: (b, 0, 0)), cache, cache],
        compiler_params=_cparams(1),
        name="ctx_attn",
    )(yb, yb, yb, yb, kv)


N_DR = 2 * NA_KH - 1
N_DC = 2 * NA_KW - 1
N_TAB = N_DR - 1


def _nat_kernel(rb_ref, q_ref, k_ref, v_ref, g_ref, kc_ref, vc_ref, o_ref,
                tab_ref, qs_ref, s_ref, p_ref, r_ref, *, rows):
    p = pl.program_id(0)
    scale = DH_B ** -0.5
    kh = min(NA_KH, rows)
    masks = _head_masks()

    @pl.when(pl.program_id(1) == 0)
    def _build_tables():
        c = lax.broadcasted_iota(jnp.int32, (GRID_W, LANES), 0)
        lane = lax.broadcasted_iota(jnp.int32, (GRID_W, LANES), 1)
        kcol = lane & (GRID_W - 1)
        upper = lane >= GRID_W
        ws = jnp.clip(c - NA_KW // 2, 0, GRID_W - NA_KW)
        neg = jnp.full((GRID_W, LANES), NEG_INF, F32)
        diag = kcol - c + (NA_KW - 1)
        for h in range(2):
            base = (2 * p + h) * (N_DR * N_DC)

            def per_dr(i, _, base=base, h=h):
                def per_dc(dd, acc):
                    lo = rb_ref[base + i * N_DC + dd]
                    hi = rb_ref[base + (i + 1) * N_DC + dd]
                    return jnp.where(diag == dd, jnp.where(upper, hi, lo), acc)

                acc = lax.fori_loop(0, N_DC, per_dc, neg)
                acc = jnp.where(kcol >= ws, jnp.where(kcol < ws + NA_KW, acc, neg), neg)
                tab_ref[h, i] = acc
                return 0

            lax.fori_loop(0, N_TAB, per_dr, 0)

    kc = jnp.concatenate([kc_ref[0, 0], kc_ref[0, 1]], axis=1).astype(BF16)
    vc = jnp.concatenate([vc_ref[0, 0], vc_ref[0, 1]], axis=1).astype(BF16)
    n_keys = kh * GRID_W
    n_ctx = kc.shape[0]
    G = NA_GROUP
    W2 = 2 * GRID_W

    def group(gi, _):
        r_first = gi * G
        q0 = pl.multiple_of(r_first * GRID_W, G * GRID_W)
        for i in range(G):
            qi = q_ref[0, pl.ds(q0 + i * GRID_W, GRID_W), :] * scale
            for h in range(2):
                qs_ref[i * W2 + h * GRID_W:i * W2 + (h + 1) * GRID_W, :] = jnp.where(
                    masks[h], qi, jnp.zeros_like(qi))
        s_ref[:, n_keys:n_keys + n_ctx] = lax.dot_general(qs_ref[...], kc, NT_DIMS,
                                                          preferred_element_type=F32)
        windows = {}

        def local_scores(i):
            r = r_first + i
            rs = jnp.clip(r - kh // 2, 0, rows - kh)
            k0 = pl.multiple_of(rs * GRID_W, GRID_W)
            windows[i] = k0
            dr0 = rs - r + (NA_KH - 1)
            bias = jnp.concatenate(
                [jnp.concatenate([tab_ref[h, dr0 + 2 * m] for m in range(kh // 2)], axis=1)
                 for h in range(2)], axis=0)
            s_ref[i * W2:(i + 1) * W2, 0:n_keys] = lax.dot_general(
                qs_ref[i * W2:(i + 1) * W2, :], k_ref[0, pl.ds(k0, n_keys), :], NT_DIMS,
                preferred_element_type=F32) + bias

        def numerators(i):
            s = s_ref[i * W2:(i + 1) * W2, :]
            e = jnp.exp(s - jnp.max(s, axis=-1, keepdims=True))
            p_ref[i * W2:(i + 1) * W2, :] = e.astype(BF16)
            rinv = 1.0 / jnp.sum(e, axis=-1, keepdims=True)
            r_ref[i * W2:(i + 1) * W2, :] = jnp.broadcast_to(rinv, (W2, LANES))

        def weighted_values(i):
            vals = jnp.concatenate([v_ref[0, pl.ds(windows[i], n_keys), :], vc], axis=0)
            o = jnp.dot(p_ref[i * W2:(i + 1) * W2, :], vals, preferred_element_type=F32)
            o = o * r_ref[i * W2:(i + 1) * W2, :]
            o = jnp.where(masks[0], o[0:GRID_W], o[GRID_W:W2])
            out_rows = pl.ds(q0 + i * GRID_W, GRID_W)
            gate = g_ref[0, out_rows, :].astype(F32)
            o_ref[0, out_rows, :] = (o * _silu(gate)).astype(o_ref.dtype)

        for step in range(G + 2):
            if step < G:
                local_scores(step)
            if 0 <= step - 1 < G:
                numerators(step - 1)
            if 0 <= step - 2 < G:
                weighted_values(step - 2)
        return 0

    lax.fori_loop(0, rows // G, group, 0)


def _nat(yb, kc, vc, rel_bias):
    B, T, _ = yb.shape
    Tc = kc.shape[2]
    n_pair = H_B // 2
    rows = T // GRID_W
    n_stack = NA_GROUP * 2 * GRID_W
    n_keys = min(NA_KH, rows) * GRID_W

    def col(k):
        return pl.BlockSpec((1, T, LANES), lambda p, b, k=k: (b, 0, k * n_pair + p))

    ctx = pl.BlockSpec((1, 2, Tc, DH_B), lambda p, b: (b, p, 0, 0))
    return pl.pallas_call(
        functools.partial(_nat_kernel, rows=rows),
        out_shape=jax.ShapeDtypeStruct((B, T, H_B * DH_B), BF16),
        grid=(n_pair, B),
        in_specs=[pl.BlockSpec(memory_space=pltpu.SMEM), col(0), col(1), col(2), col(3), ctx, ctx],
        out_specs=pl.BlockSpec((1, T, LANES), lambda p, b: (b, 0, p)),
        scratch_shapes=[pltpu.VMEM((2, N_TAB, GRID_W, LANES), F32),
                        pltpu.VMEM((n_stack, LANES), BF16),
                        pltpu.VMEM((n_stack, n_keys + Tc), F32),
                        pltpu.VMEM((n_stack, n_keys + Tc), BF16),
                        pltpu.VMEM((n_stack, LANES), F32)],
        compiler_params=_cparams(2),
        name="nbr_attn",
    )(rel_bias.reshape(-1), yb, yb, yb, yb, kc, vc)


def _seg_len(seq):
    length = -(-seq // N_SEG)
    while length % 8 != 4:
        length += 1
    return length


def _step_block(seg_len):
    return max(d for d in range(1, seg_len + 1) if seg_len % d == 0 and d * N_SEG <= RG_ROWS * 3 // 2)


def _rglru_kernel(x_ref, g_ref, cw_ref, cb_ref, wg_ref, lam_ref, *rest, seq, slabs, has_s0, emit_state):
    rest = list(rest)
    s0_ref = rest.pop(0) if has_s0 else None
    o_ref = rest.pop(0)
    hfin_ref = rest.pop(0) if emit_state else None
    xpad_ref, a_ref, u_ref, h_ref, p_ref, hn_ref = rest
    L = _seg_len(seq)
    n_rows = N_SEG * L
    RB = RG_ROWS
    CB = slabs
    TB = _step_block(L)
    n_tile = N_SEG // 8
    chains = [(d, j, s) for d in range(2) for j in range(CB) for s in range(n_tile)]

    for j in range(CB):
        xpad_ref[j, 0:8, :] = jnp.zeros((8, LANES), F32)
        xpad_ref[j, 8:seq + 8, :] = x_ref[0, :, j * LANES:(j + 1) * LANES]
        xpad_ref[j, seq + 8:n_rows + 16, :] = jnp.zeros((n_rows + 8 - seq, LANES), F32)

    nl = -lam_ref[...]
    sp = jnp.maximum(nl, 0.0) + jnp.log1p(jnp.exp(-jnp.abs(nl)))
    cw = cw_ref[...]
    cbias = cb_ref[...]
    ones2 = jnp.where(lax.broadcasted_iota(jnp.int32, (TB * N_SEG, LANES), 1) < 2, 1.0, 0.0).astype(BF16)

    def gates(blk, _):
        t0 = blk * TB
        r0 = pl.multiple_of(blk * (TB * N_SEG), TB * N_SEG)
        for j in range(CB):
            lanes = slice(j * LANES, (j + 1) * LANES)
            tiles = []
            for tt in range(TB):
                for s in range(n_tile):
                    taps = [xpad_ref[j, pl.ds(6 + k + t0 + tt + s * 8 * L, 8, stride=L), :] for k in range(4)]
                    xt = cw[0:1, lanes] * taps[0] + cw[1:2, lanes] * taps[1]
                    xt = xt + cw[2:3, lanes] * taps[2]
                    tiles.append(xt + cw[3:4, lanes] * taps[3] + cbias[:, lanes])
            xj = jnp.concatenate(tiles, axis=0)
            gt = jnp.dot(jnp.concatenate([xj.astype(BF16), ones2], axis=1), wg_ref[j],
                         preferred_element_type=F32)
            xh = 0.5 * xj
            for d in range(2):
                th_r = jnp.tanh(gt[:, (2 * d) * LANES:(2 * d + 1) * LANES])
                th_i = jnp.tanh(gt[:, (2 * d + 1) * LANES:(2 * d + 2) * LANES])
                half = (-0.5 * RG_C) * sp[d:d + 1, lanes]
                la = half + half * th_r
                a = jnp.exp(la)
                y = -jnp.tanh(la) * (1.0 + a * a)
                root = jnp.where(y > 0.0, y * lax.rsqrt(y), 0.0)
                a_ref[d, j, pl.ds(r0, TB * N_SEG), :] = a
                u_ref[d, j, pl.ds(r0, TB * N_SEG), :] = root * (xh + xh * th_i)
        return 0

    lax.fori_loop(0, L // TB, gates, 0)

    first_pad = [[min(max(seq - (s * 8 + r) * L, 0), L) for r in range(8)] for s in range(n_tile)]
    sub = lax.broadcasted_iota(jnp.int32, (8, LANES), 0)
    pad_from = []
    for s in range(n_tile):
        if all(f == L for f in first_pad[s]):
            pad_from.append(None)
        else:
            vec = jnp.full((8, LANES), L, jnp.int32)
            for r in range(8):
                vec = jnp.where(sub == r, first_pad[s][r], vec)
            pad_from.append(vec)

    def step_rows(step, s):
        return pl.ds(pl.multiple_of(step * N_SEG + s * 8, 8), 8)

    unroll = 4

    def scan(i, carry):
        carry = list(carry)
        for k in range(unroll):
            for n, (d, j, s) in enumerate(chains):
                h, pr = carry[n]
                t = i * unroll + k
                if d == 1:
                    t = L - 1 - t
                idx = step_rows(t, s)
                a = a_ref[d, j, idx, :]
                u = u_ref[d, j, idx, :]
                if pad_from[s] is not None:
                    live = t < pad_from[s]
                    a = jnp.where(live, a, 1.0)
                    u = jnp.where(live, u, 0.0)
                h = a * h + u
                pr = pr * a
                h_ref[d, j, idx, :] = h
                p_ref[d, j, idx, :] = pr
                carry[n] = (h, pr)
        return tuple(carry)

    zero = jnp.zeros((8, LANES), F32)
    one = jnp.ones((8, LANES), F32)
    ends = lax.fori_loop(0, L // unroll, scan, ((zero, one),) * len(chains))

    cins = {}
    finals = [[None] * CB for _ in range(2)]
    for d in range(2):
        for j in range(CB):
            if has_s0:
                c = s0_ref[0, d:d + 1, j * LANES:(j + 1) * LANES]
            else:
                c = jnp.zeros((1, LANES), F32)
            cin = [None] * N_SEG
            for kk in range(N_SEG):
                seg = kk if d == 0 else N_SEG - 1 - kk
                s, row = divmod(seg, 8)
                h_end, p_end = ends[chains.index((d, j, s))]
                cin[seg] = c
                c = h_end[row:row + 1, :] + p_end[row:row + 1, :] * c
            finals[d][j] = c
            for s in range(n_tile):
                cins[(d, j, s)] = jnp.concatenate(cin[s * 8:(s + 1) * 8], axis=0)

    def fix(i, _):
        for k in range(unroll):
            t = i * unroll + k
            for j in range(CB):
                for s in range(n_tile):
                    idx = step_rows(t, s)
                    parts = [h_ref[d, j, idx, :] + p_ref[d, j, idx, :] * cins[(d, j, s)] for d in range(2)]
                    hn_ref[j, pl.ds(t + s * 8 * L, 8, stride=L), :] = parts[0] + parts[1]
        return 0

    lax.fori_loop(0, L // unroll, fix, 0)

    if emit_state:
        hfin_ref[0] = jnp.concatenate([jnp.concatenate(finals[d], axis=1) for d in range(2)], axis=0)

    def combine(blk, _):
        rows = pl.ds(pl.multiple_of(blk * RB, RB), RB)
        hs = jnp.concatenate([hn_ref[j, rows, :] for j in range(CB)], axis=1)
        o_ref[0, rows, :] = (hs * _silu(g_ref[0, rows, :])).astype(o_ref.dtype)
        return 0

    lax.fori_loop(0, seq // RB, combine, 0)


def _gate_weights(gate_w, gate_b):
    w = (0.5 * gate_w).transpose(2, 3, 0, 1, 4).reshape(H_C, BW_C, 4 * BW_C).astype(BF16)
    b = (0.5 * gate_b).reshape(2, 2, H_C, BW_C).transpose(2, 0, 1, 3).reshape(H_C, 1, 4 * BW_C).astype(F32)
    hi = b.astype(BF16)
    lo = (b - hi.astype(F32)).astype(BF16)
    zeros = jnp.zeros((H_C, BW_C - 2, 4 * BW_C), BF16)
    return jnp.concatenate([w, hi, lo, zeros], axis=1)


def _rglru(xg, conv_w, conv_b, wg, lam, s0, emit_state):
    B, T, _ = xg.shape
    has_s0 = s0 is not None
    n_rows = N_SEG * _seg_len(T)
    CB = RG_SLABS if T > 4 * RG_ROWS else 2 * RG_SLABS
    wide = CB * LANES
    n_steps = H_C // CB
    in_specs = [
        pl.BlockSpec((1, T, wide), lambda b, c: (b, 0, c)),
        pl.BlockSpec((1, T, wide), lambda b, c: (b, 0, n_steps + c)),
        pl.BlockSpec((4, wide), lambda b, c: (0, c)),
        pl.BlockSpec((1, wide), lambda b, c: (0, c)),
        pl.BlockSpec((CB, 2 * BW_C, 4 * BW_C), lambda b, c: (c, 0, 0)),
        pl.BlockSpec((2, wide), lambda b, c: (0, c)),
    ]
    args = [xg, xg, conv_w, conv_b, wg, lam]
    if has_s0:
        in_specs.append(pl.BlockSpec((1, 2, wide), lambda b, c: (b, 0, c)))
        args.append(s0)
    out_shape = [jax.ShapeDtypeStruct((B, T, W_C), BF16)]
    out_specs = [pl.BlockSpec((1, T, wide), lambda b, c: (b, 0, c))]
    if emit_state:
        out_shape.append(jax.ShapeDtypeStruct((B, 2, W_C), F32))
        out_specs.append(pl.BlockSpec((1, 2, wide), lambda b, c: (b, 0, c)))
    res = pl.pallas_call(
        functools.partial(_rglru_kernel, seq=T, slabs=CB, has_s0=has_s0, emit_state=emit_state),
        out_shape=out_shape,
        grid=(B, n_steps),
        in_specs=in_specs,
        out_specs=out_specs,
        scratch_shapes=[pltpu.VMEM((CB, n_rows + 16, LANES), F32)]
        + [pltpu.VMEM((2, CB, n_rows, LANES), F32)] * 4 + [pltpu.VMEM((CB, n_rows, LANES), F32)],
        compiler_params=_cparams(2),
        name="rglru",
    )(*args)
    return res if emit_state else (res[0], None)


A_COLS = 5 * H_A * DK_A
B_COLS = 4 * H_B * DH_B


def kernel(x_prompt, x_sample, state_hgrn, cache_na_k, cache_na_v, state_rglru, c, c_ctx, norm_gain, w_mod, b_mod, w_in_even, w_out_even, hgrn_lb_logits, hgrn_out_gain, na_rel_bias, w_in_odd, w_out_odd, conv_w, conv_b, rg_gate_w, rg_gate_b, rg_lambda, final_gain):
    n_ctx = x_prompt.shape[0]
    n_lat = x_sample.shape[0]
    depth = w_mod.shape[0]

    cond = jnp.zeros((16, D_MODEL), F32).at[0].set(c_ctx).at[1:1 + n_lat].set(c)
    mod = _modulation(cond, w_mod, b_mod.reshape(depth, 1, 3 * D_MODEL))
    mod = mod.reshape(depth, 16, 3, D_MODEL)

    t_ctx = x_prompt.shape[1]

    def flat(a):
        return a.reshape(1, n_ctx * t_ctx, a.shape[-1])

    def unflat(a):
        return a.reshape(n_ctx, t_ctx, a.shape[-1])

    def in_proj_params(l):
        if l % 2 == 0:
            outs_s = ((0, A_COLS, F32), (A_COLS, B_COLS, BF16))
            outs_c = outs_s + ((A_COLS + H_B * DH_B, 2 * H_B * DH_B, F32),)
            a_key = H_A * DK_A
            col = jnp.arange(w_in_even.shape[-1])
            halve = jnp.where((col >= a_key) & (col < 3 * a_key), 0.5, 1.0).astype(F32)
            return (w_in_even[l // 2] * halve).astype(BF16), outs_c, outs_s
        outs = ((0, 2 * W_C, F32),)
        return w_in_odd[l // 2].astype(BF16), outs, outs

    xc, xs = x_prompt, x_sample
    new_hgrn, new_k, new_v, new_rg = [], [], [], []
    proj_c = proj_s = None
    for l in range(depth):
        j = l // 2
        mod_c, mod_s = mod[l, 0:1], mod[l, 1:1 + n_lat]
        if proj_c is None:
            gain = norm_gain[l].reshape(1, D_MODEL)
            w_in, outs_c, outs_s = in_proj_params(l)
            proj_c = [unflat(t) for t in _inproj(flat(xc), mod_c, gain, w_in, outs_c, 512, True)]
            proj_s = _inproj(xs, mod_s, gain, w_in, outs_s, 512, False)
        if l % 2 == 0:
            w_out = w_out_even[j].astype(BF16)
            (ya_c, yb_c, kv_c), (ya_s, yb_s) = proj_c, proj_s
            hgain = hgrn_out_gain[j].reshape(H_A, 1, DK_A)
            oa_c, s_fin = _hgrn(ya_c, hgrn_lb_logits, j, hgain, None)
            oa_s, _ = _hgrn(ya_s, hgrn_lb_logits, j, hgain, state_hgrn[:, j])
            ob_c, k_c, v_c = _ctx_attn(yb_c, kv_c)
            ob_s = _nat(yb_s, cache_na_k[:, j], cache_na_v[:, j], na_rel_bias[j])
            ys_c, ys_s = (oa_c, ob_c), (oa_s, ob_s)
            new_hgrn.append(s_fin)
            new_k.append(k_c)
            new_v.append(v_c)
        else:
            w_out = w_out_odd[j].astype(BF16)
            (xg_c,), (xg_s,) = proj_c, proj_s
            wg = _gate_weights(rg_gate_w[j], rg_gate_b[j])
            cb = conv_b[j].reshape(1, W_C)
            y_c, h_fin = _rglru(xg_c, conv_w[j], cb, wg, rg_lambda[j], None, True)
            y_s, _ = _rglru(xg_s, conv_w[j], cb, wg, rg_lambda[j], state_rglru[:, j], False)
            ys_c, ys_s = (y_c,), (y_s,)
            new_rg.append(h_fin)
        ys_c = tuple(flat(y) for y in ys_c)
        if l == depth - 1:
            fgain = final_gain.reshape(1, D_MODEL)
            (xc,) = _outproj(ys_c, flat(xc), mod_c, w_out, 1024, True, final_gain=fgain)
            (xs,) = _outproj(ys_s, xs, mod_s, w_out, 1024, False, final_gain=fgain)
            xc = unflat(xc)
        else:
            gain_n = norm_gain[l + 1].reshape(1, D_MODEL)
            w_n, outs_c, outs_s = in_proj_params(l + 1)
            mod_cn, mod_sn = mod[l + 1, 0:1], mod[l + 1, 1:1 + n_lat]
            xc, *proj_c = _outproj(ys_c, flat(xc), mod_c, w_out, 512, True,
                                   next_proj=(mod_cn, gain_n, w_n, outs_c))
            xs, *proj_s = _outproj(ys_s, xs, mod_s, w_out, 512, False,
                                   next_proj=(mod_sn, gain_n, w_n, outs_s))
            xc = unflat(xc)
            proj_c = [unflat(t) for t in proj_c]
    return (xc, xs, jnp.stack(new_hgrn, axis=1), jnp.stack(new_k, axis=1),
            jnp.stack(new_v, axis=1), jnp.stack(new_rg, axis=1))
```

```python
import functools

import jax
import jax.numpy as jnp
from jax import lax
from jax.experimental import pallas as pl
from jax.experimental.pallas import tpu as pltpu

F32 = jnp.float32
BF16 = jnp.bfloat16

D_MODEL = 1024
EPS = 1e-6
NEG_INF = -1e30
H_A = 4
DK_A = 128
HGRN_CHUNK = 32
HGRN_ROWS = 256
H_B = 8
DH_B = 64
GRID_W = 64
NA_KH = 8
NA_KW = 16
NA_GROUP = 8
W_C = 1024
H_C = 8
BW_C = W_C // H_C
RG_C = 8.0
RG_ROWS = 256
RG_SLABS = 2
N_SEG = 16
LANES = 128
VMEM_LIMIT = 48 * 1024 * 1024

NT_DIMS = (((1,), (1,)), ((), ()))
TN_DIMS = (((0,), (0,)), ((), ()))


def _silu(x):
    half = 0.5 * x
    return half + half * jnp.tanh(half)


def _cparams(n_axes):
    return pltpu.CompilerParams(dimension_semantics=("arbitrary",) * n_axes,
                                vmem_limit_bytes=VMEM_LIMIT)


def _mod_kernel(cond_ref, w_ref, b_ref, o_ref):
    s = _silu(cond_ref[...])
    o_ref[0] = jnp.dot(s.astype(BF16), w_ref[0].astype(BF16), preferred_element_type=F32) + b_ref[0]


def _modulation(cond, w_mod, b_mod):
    depth = w_mod.shape[0]
    n_rows = cond.shape[0]
    return pl.pallas_call(
        _mod_kernel,
        out_shape=jax.ShapeDtypeStruct((depth, n_rows, 3 * D_MODEL), F32),
        grid=(depth, 3),
        in_specs=[
            pl.BlockSpec((n_rows, D_MODEL), lambda l, n: (0, 0)),
            pl.BlockSpec((1, D_MODEL, D_MODEL), lambda l, n: (l, 0, n)),
            pl.BlockSpec((1, 1, D_MODEL), lambda l, n: (l, 0, n)),
        ],
        out_specs=pl.BlockSpec((1, n_rows, D_MODEL), lambda l, n: (l, 0, n)),
        compiler_params=_cparams(2),
        name="adaln_mod",
    )(cond, w_mod, b_mod)


def _project(x, mod_ref, gain_ref, w_ref, out_refs, outs):
    var = jnp.mean(x * x, axis=-1, keepdims=True)
    y = x * lax.rsqrt(var + EPS) * gain_ref[...]
    h = y * (1.0 + mod_ref[0, 1:2, :]) + mod_ref[0, 0:1, :]
    hb = h.astype(BF16)
    step = 512
    for c in range(0, w_ref.shape[1], step):
        users = [(o_ref, c - col0) for o_ref, (col0, width, _) in zip(out_refs, outs)
                 if col0 <= c < col0 + width]
        if users:
            r = jnp.dot(hb, w_ref[:, c:c + step], preferred_element_type=F32)
            for o_ref, off in users:
                o_ref[0, :, off:off + step] = r.astype(o_ref.dtype)


def _inproj_kernel(x_ref, mod_ref, gain_ref, w_ref, *out_refs, outs):
    _project(x_ref[0], mod_ref, gain_ref, w_ref, out_refs, outs)


def _inproj(x, mod, gain, w, outs, tm, shared_mod):
    B, T, _ = x.shape
    n_cols = w.shape[1]
    mod_map = (lambda b, t: (0, 0, 0)) if shared_mod else (lambda b, t: (b, 0, 0))
    return pl.pallas_call(
        functools.partial(_inproj_kernel, outs=outs),
        out_shape=[jax.ShapeDtypeStruct((B, T, wd), dt) for _, wd, dt in outs],
        grid=(B, T // tm),
        in_specs=[
            pl.BlockSpec((1, tm, D_MODEL), lambda b, t: (b, t, 0)),
            pl.BlockSpec((1, 3, D_MODEL), mod_map),
            pl.BlockSpec((1, D_MODEL), lambda b, t: (0, 0)),
            pl.BlockSpec((D_MODEL, n_cols), lambda b, t: (0, 0)),
        ],
        out_specs=[pl.BlockSpec((1, tm, wd), lambda b, t: (b, t, 0)) for _, wd, _ in outs],
        compiler_params=_cparams(2),
        name="in_proj",
    )(x, mod, gain, w)


def _outproj_kernel(*refs, n_y, final, next_outs):
    y_refs, (x_ref, mod_ref, w_ref), rest = refs[:n_y], refs[n_y:n_y + 3], refs[n_y + 3:]
    m = None
    row = 0
    for y_ref in y_refs:
        width = y_ref.shape[-1]
        part = jnp.dot(y_ref[0], w_ref[row:row + width, :], preferred_element_type=F32)
        m = part if m is None else m + part
        row += width
    xn = x_ref[0] + mod_ref[0, 2:3, :] * m
    if final:
        gain_ref, o_ref = rest
        var = jnp.mean(xn * xn, axis=-1, keepdims=True)
        o_ref[0] = xn * lax.rsqrt(var + EPS) * gain_ref[...]
    else:
        modn_ref, gainn_ref, wn_ref, o_ref = rest[:4]
        o_ref[0] = xn
        _project(xn, modn_ref, gainn_ref, wn_ref, rest[4:], next_outs)


def _outproj(ys, x, mod, w, tm, shared_mod, final_gain=None, next_proj=None):
    B, T, _ = x.shape
    final = final_gain is not None
    mod_map = (lambda b, t: (0, 0, 0)) if shared_mod else (lambda b, t: (b, 0, 0))
    row_block = pl.BlockSpec((1, tm, D_MODEL), lambda b, t: (b, t, 0))
    vec = pl.BlockSpec((1, D_MODEL), lambda b, t: (0, 0))
    in_specs = [pl.BlockSpec((1, tm, y.shape[-1]), lambda b, t: (b, t, 0)) for y in ys] + [
        row_block,
        pl.BlockSpec((1, 3, D_MODEL), mod_map),
        pl.BlockSpec((w.shape[0], D_MODEL), lambda b, t: (0, 0)),
    ]
    args = list(ys) + [x, mod, w]
    out_shape = [jax.ShapeDtypeStruct((B, T, D_MODEL), F32)]
    out_specs = [row_block]
    next_outs = None
    if final:
        in_specs.append(vec)
        args.append(final_gain)
    else:
        mod_n, gain_n, w_n, next_outs = next_proj
        in_specs += [pl.BlockSpec((1, 3, D_MODEL), mod_map), vec,
                     pl.BlockSpec((D_MODEL, w_n.shape[1]), lambda b, t: (0, 0))]
        args += [mod_n, gain_n, w_n]
        out_shape += [jax.ShapeDtypeStruct((B, T, wd), dt) for _, wd, dt in next_outs]
        out_specs += [pl.BlockSpec((1, tm, wd), lambda b, t: (b, t, 0)) for _, wd, _ in next_outs]
    return pl.pallas_call(
        functools.partial(_outproj_kernel, n_y=len(ys), final=final, next_outs=next_outs),
        out_shape=out_shape,
        grid=(B, T // tm),
        in_specs=in_specs,
        out_specs=out_specs,
        compiler_params=_cparams(2),
        name="out_proj",
    )(*args)


def _hgrn_kernel(q_ref, zf_ref, zb_ref, v_ref, g_ref, lgt_ref, gain_ref, *rest, seq, layer, per_block):
    rest = list(rest)
    s0_ref = None if per_block else rest.pop(0)
    o_ref = rest.pop(0)
    sfin_ref = rest.pop(0) if per_block else None
    acc_ref, qd_ref, ki_ref, kd_ref, kv_ref, st_ref, dec_ref, mst_ref, msk_ref, mexp_ref = rest
    R = HGRN_ROWS
    C = HGRN_CHUNK
    n_blk = seq // R
    n_chunk = R // C
    n_all = seq // C

    @pl.when((pl.program_id(0) == 0) & (pl.program_id(1) == 0))
    def _build_masks():
        ti = lax.broadcasted_iota(jnp.int32, (R, R), 0)
        tj = lax.broadcasted_iota(jnp.int32, (R, R), 1)
        shift = C.bit_length() - 1
        same = lax.shift_right_logical(ti, shift) == lax.shift_right_logical(tj, shift)
        one = jnp.ones((R, R), F32)
        zero = jnp.zeros((R, R), F32)
        incl = (jnp.where(same, jnp.where(tj <= ti, one, zero), zero),
                jnp.where(same, jnp.where(tj >= ti, one, zero), zero))
        for d in range(2):
            msk_ref[d] = incl[d]
            mst_ref[d] = incl[d].astype(BF16)
        rr = lax.broadcasted_iota(jnp.int32, (R, n_chunk * LANES), 0)
        cc = lax.broadcasted_iota(jnp.int32, (R, n_chunk * LANES), 1)
        own = lax.shift_right_logical(rr, shift) == lax.shift_right_logical(cc, LANES.bit_length() - 1)
        mexp_ref[...] = jnp.where(own, 1.0, 0.0).astype(BF16)

    lgt = [lgt_ref[:, i, :] for i in range(lgt_ref.shape[1])]
    lmax = functools.reduce(jnp.maximum, lgt)
    ex = [jnp.exp(t - lmax) for t in lgt]
    lb_all = sum(ex[:layer + 1]) / sum(ex)
    gain = gain_ref[0]

    blocks_per_trip = 2 if n_blk % 2 == 0 else 1

    def gates(i, _):
        for u in range(blocks_per_trip):
            blk = i * blocks_per_trip + u
            rows = pl.ds(pl.multiple_of(blk * R, R), R)
            q = q_ref[0, rows, :]
            for d in range(2):
                th = jnp.tanh((zf_ref if d == 0 else zb_ref)[0, rows, :])
                lb = lb_all[d:d + 1, :]
                c = 0.5 * (1.0 - lb)
                ct = c * th
                f = (lb + c) + ct
                k = c - ct
                logf = jnp.log(f)
                hi = logf.astype(BF16)
                lo = (logf - hi.astype(F32)).astype(BF16)
                cs = jnp.dot(mst_ref[d], jnp.concatenate([hi, lo], axis=1), preferred_element_type=F32)
                b = cs[:, 0:LANES] + cs[:, LANES:2 * LANES]
                ends = [c * C + (C - 1 if d == 0 else 0) for c in range(n_chunk)]
                btot = jnp.concatenate([jnp.broadcast_to(b[t:t + 1, :], (C, LANES)) for t in ends], axis=0)
                qd_ref[d, rows, :] = (q * jnp.exp(b)).astype(BF16)
                ki_ref[d, rows, :] = (k * jnp.exp(-b)).astype(BF16)
                kd_ref[d, rows, :] = (k * jnp.exp(btot - b)).astype(BF16)
                for c in range(n_chunk):
                    dec_ref[d, blk * n_chunk + c] = jnp.exp(btot[c * C:c * C + 8, :])
        return 0

    lax.fori_loop(0, n_blk // blocks_per_trip, gates, 0)

    def intra(i, _):
        for u in range(blocks_per_trip):
            blk = i * blocks_per_trip + u
            rows = pl.ds(pl.multiple_of(blk * R, R), R)
            v = v_ref[0, rows, :]
            vb = v.astype(BF16)
            vt = v.T.astype(BF16)
            att_sum = None
            for d in range(2):
                att = lax.dot_general(qd_ref[d, rows, :], ki_ref[d, rows, :], NT_DIMS,
                                      preferred_element_type=F32)
                att = jnp.where(msk_ref[d] > 0.5, att, 0.0)
                att_sum = att if att_sum is None else att_sum + att
                kd_exp = jnp.concatenate([kd_ref[d, rows, :]] * n_chunk, axis=1) * mexp_ref[...]
                kv_all = jnp.dot(vt, kd_exp, preferred_element_type=F32)
                for c in range(n_chunk):
                    kv_ref[d, blk * n_chunk + c] = kv_all[:, c * LANES:(c + 1) * LANES]
            acc_ref[rows, :] = jnp.dot(att_sum.astype(BF16), vb, preferred_element_type=F32)
        return 0

    lax.fori_loop(0, n_blk // blocks_per_trip, intra, 0)

    unroll = 4

    def states(i, sts):
        sts = list(sts)
        for u in range(unroll):
            n = i * unroll + u
            for d in range(2):
                c = n if d == 0 else n_all - 1 - n
                st_ref[d, c] = sts[d].astype(BF16)
                dec = jnp.concatenate([dec_ref[d, c]] * (DK_A // 8), axis=0)
                sts[d] = sts[d] * dec + kv_ref[d, c]
        return tuple(sts)

    def block_states(blk, _):
        for d in range(2):
            st = jnp.zeros((DK_A, DK_A), F32)
            for cc in range(n_chunk):
                c = blk * n_chunk + (cc if d == 0 else n_chunk - 1 - cc)
                st_ref[d, c] = st.astype(BF16)
                dec = jnp.concatenate([dec_ref[d, c]] * (DK_A // 8), axis=0)
                st = st * dec + kv_ref[d, c]
            sfin_ref[blk, d, 0] = st.T
        return 0

    if per_block:
        lax.fori_loop(0, n_blk, block_states, 0)
    else:
        st0 = (s0_ref[0, 0, 0].T, s0_ref[0, 1, 0].T)
        lax.fori_loop(0, n_all // unroll, states, st0)

    def finish(blk, _):
        rows = pl.ds(pl.multiple_of(blk * R, R), R)
        tot = acc_ref[rows, :]
        for d in range(2):
            pieces = []
            for c in range(n_chunk):
                crow = pl.ds(pl.multiple_of(blk * R + c * C, C), C)
                pieces.append(lax.dot_general(qd_ref[d, crow, :], st_ref[d, blk * n_chunk + c], NT_DIMS,
                                              preferred_element_type=F32))
            tot = tot + jnp.concatenate(pieces, axis=0)
        var = jnp.mean(tot * tot, axis=-1, keepdims=True)
        y = tot * lax.rsqrt(var + EPS) * gain
        o_ref[0, rows, :] = (y * _silu(g_ref[0, rows, :])).astype(o_ref.dtype)
        return 0

    lax.fori_loop(0, n_blk, finish, 0)


def _hgrn(ya, lgt, layer, gain, s0):
    B, T, width = ya.shape
    per_block = s0 is None
    if per_block:
        assert T == HGRN_ROWS
        group = 8 if B % 8 == 0 else 1
        n_seq, B, T = B, B // group, group * T
        ya = ya.reshape(B, T, width)

    def col(k):
        return pl.BlockSpec((1, T, LANES), lambda b, h, k=k: (b, 0, k * H_A + h))

    in_specs = [col(0), col(1), col(2), col(3), col(4),
                pl.BlockSpec((2, lgt.shape[1], LANES), lambda b, h: (0, 0, h)),
                pl.BlockSpec((1, 1, LANES), lambda b, h: (h, 0, 0))]
    args = [ya, ya, ya, ya, ya, lgt, gain]
    out_shape = [jax.ShapeDtypeStruct((B, T, H_A * DK_A), BF16)]
    out_specs = [pl.BlockSpec((1, T, LANES), lambda b, h: (b, 0, h))]
    if per_block:
        out_shape.append(jax.ShapeDtypeStruct((n_seq, 2, H_A, DK_A, DK_A), F32))
        out_specs.append(pl.BlockSpec((T // HGRN_ROWS, 2, 1, DK_A, DK_A), lambda b, h: (b, 0, h, 0, 0)))
    else:
        in_specs.append(pl.BlockSpec((1, 2, 1, DK_A, DK_A), lambda b, h: (b, 0, h, 0, 0)))
        args.append(s0)
    res = pl.pallas_call(
        functools.partial(_hgrn_kernel, seq=T, layer=layer, per_block=per_block),
        out_shape=out_shape,
        grid=(B, H_A),
        in_specs=in_specs,
        out_specs=out_specs,
        scratch_shapes=[pltpu.VMEM((T, LANES), F32),
                        pltpu.VMEM((2, T, LANES), BF16),
                        pltpu.VMEM((2, T, LANES), BF16),
                        pltpu.VMEM((2, T, LANES), BF16),
                        pltpu.VMEM((2, T // HGRN_CHUNK, DK_A, DK_A), F32),
                        pltpu.VMEM((2, T // HGRN_CHUNK, DK_A, DK_A), BF16),
                        pltpu.VMEM((2, T // HGRN_CHUNK, 8, LANES), F32),
                        pltpu.VMEM((2, HGRN_ROWS, HGRN_ROWS), BF16),
                        pltpu.VMEM((2, HGRN_ROWS, HGRN_ROWS), F32),
                        pltpu.VMEM((HGRN_ROWS, HGRN_ROWS // HGRN_CHUNK * LANES), BF16)],
        compiler_params=_cparams(2),
        name="hgrn2",
    )(*args)
    if per_block:
        return res[0].reshape(n_seq, HGRN_ROWS, H_A * DK_A), res[1]
    return res[0], None


def _head_masks():
    lane = lax.broadcasted_iota(jnp.int32, (1, LANES), 1)
    return lane < DH_B, lane >= DH_B


def _ctx_attn_kernel(q_ref, k_ref, v_ref, g_ref, kv_ref, o_ref, newk_ref, newv_ref):
    scale = DH_B ** -0.5
    masks = _head_masks()
    T = q_ref.shape[1]
    for h in range(H_B):
        newk_ref[0, h] = kv_ref[0, :, h * DH_B:(h + 1) * DH_B]
        newv_ref[0, h] = kv_ref[0, :, (H_B + h) * DH_B:(H_B + h + 1) * DH_B]
    for p in range(H_B // 2):
        cols = slice(p * LANES, (p + 1) * LANES)
        q = q_ref[0, :, cols] * scale
        qs = jnp.concatenate([jnp.where(masks[h], q, jnp.zeros_like(q)) for h in range(2)], axis=0)
        s = lax.dot_general(qs, k_ref[0, :, cols], NT_DIMS, preferred_element_type=F32)
        e = jnp.exp(s - jnp.max(s, axis=-1, keepdims=True))
        pr = e / jnp.sum(e, axis=-1, keepdims=True)
        o = jnp.dot(pr.astype(BF16), v_ref[0, :, cols], preferred_element_type=F32)
        o = jnp.where(masks[0], o[0:T], o[T:2 * T])
        o_ref[0, :, cols] = (o * _silu(g_ref[0, :, cols].astype(F32))).astype(o_ref.dtype)


def _ctx_attn(yb, kv):
    B, T, _ = yb.shape
    width = H_B * DH_B

    def col(k):
        return pl.BlockSpec((1, T, width), lambda b, k=k: (b, 0, k))

    cache = pl.BlockSpec((1, H_B, T, DH_B), lambda b: (b, 0, 0, 0))
    return pl.pallas_call(
        _ctx_attn_kernel,
        out_shape=[jax.ShapeDtypeStruct((B, T, width), BF16),
                   jax.ShapeDtypeStruct((B, H_B, T, DH_B), F32),
                   jax.ShapeDtypeStruct((B, H_B, T, DH_B), F32)],
        grid=(B,),
        in_specs=[col(0), col(1), col(2), col(3), pl.BlockSpec((1, T, 2 * width), lambda b: (b, 0, 0))],
        out_specs=[pl.BlockSpec((1, T, width), lambda b: (b, 0, 0)), cache, cache],
        compiler_params=_cparams(1),
        name="ctx_attn",
    )(yb, yb, yb, yb, kv)


N_DR = 2 * NA_KH - 1
N_DC = 2 * NA_KW - 1
N_TAB = N_DR - 1


def _nat_kernel(rb_ref, q_ref, k_ref, v_ref, g_ref, kc_ref, vc_ref, o_ref,
                tab_ref, qs_ref, s_ref, p_ref, r_ref, *, rows):
    p = pl.program_id(0)
    scale = DH_B ** -0.5
    kh = min(NA_KH, rows)
    masks = _head_masks()

    @pl.when(pl.program_id(1) == 0)
    def _build_tables():
        c = lax.broadcasted_iota(jnp.int32, (GRID_W, LANES), 0)
        lane = lax.broadcasted_iota(jnp.int32, (GRID_W, LANES), 1)
        kcol = lane & (GRID_W - 1)
        ws = jnp.clip(c - NA_KW // 2, 0, GRID_W - NA_KW)
        neg = jnp.full((GRID_W, LANES), NEG_INF, F32)
        inside = jnp.where(kcol >= ws, jnp.where(kcol < ws + NA_KW, 1.0, 0.0), 0.0) > 0.5
        for h in range(2):
            for i in range(N_TAB):
                row = jnp.broadcast_to(rb_ref[h, i:i + 1, :], (GRID_W, LANES))
                toeplitz = pltpu.roll(row, LANES - (NA_KW - 1), 1, stride=1, stride_axis=0)
                tab_ref[h, i] = jnp.where(inside, toeplitz, neg)

    kc = jnp.concatenate([kc_ref[0, 0], kc_ref[0, 1]], axis=1).astype(BF16)
    vc = jnp.concatenate([vc_ref[0, 0], vc_ref[0, 1]], axis=1).astype(BF16)
    n_keys = kh * GRID_W
    n_ctx = kc.shape[0]
    G = NA_GROUP
    W2 = 2 * GRID_W

    def group(gi, _):
        r_first = gi * G
        q0 = pl.multiple_of(r_first * GRID_W, G * GRID_W)
        for i in range(G):
            qi = q_ref[0, pl.ds(q0 + i * GRID_W, GRID_W), :] * scale
            for h in range(2):
                qs_ref[i * W2 + h * GRID_W:i * W2 + (h + 1) * GRID_W, :] = jnp.where(
                    masks[h], qi, jnp.zeros_like(qi))
        s_ref[:, n_keys:n_keys + n_ctx] = lax.dot_general(qs_ref[...], kc, NT_DIMS,
                                                          preferred_element_type=F32)
        windows = {}

        def local_scores(i):
            r = r_first + i
            rs = jnp.clip(r - kh // 2, 0, rows - kh)
            k0 = pl.multiple_of(rs * GRID_W, GRID_W)
            windows[i] = k0
            dr0 = rs - r + (NA_KH - 1)
            bias = jnp.concatenate(
                [jnp.concatenate([tab_ref[h, dr0 + 2 * m] for m in range(kh // 2)], axis=1)
                 for h in range(2)], axis=0)
            s_ref[i * W2:(i + 1) * W2, 0:n_keys] = lax.dot_general(
                qs_ref[i * W2:(i + 1) * W2, :], k_ref[0, pl.ds(k0, n_keys), :], NT_DIMS,
                preferred_element_type=F32) + bias

        def numerators(i):
            s = s_ref[i * W2:(i + 1) * W2, :]
            e = jnp.exp(s - jnp.max(s, axis=-1, keepdims=True))
            p_ref[i * W2:(i + 1) * W2, :] = e.astype(BF16)
            rinv = 1.0 / jnp.sum(e, axis=-1, keepdims=True)
            r_ref[i * W2:(i + 1) * W2, :] = jnp.broadcast_to(rinv, (W2, LANES))

        def weighted_values(i):
            vals = jnp.concatenate([v_ref[0, pl.ds(windows[i], n_keys), :], vc], axis=0)
            o = jnp.dot(p_ref[i * W2:(i + 1) * W2, :], vals, preferred_element_type=F32)
            o = o * r_ref[i * W2:(i + 1) * W2, :]
            o = jnp.where(masks[0], o[0:GRID_W], o[GRID_W:W2])
            out_rows = pl.ds(q0 + i * GRID_W, GRID_W)
            gate = g_ref[0, out_rows, :].astype(F32)
            o_ref[0, out_rows, :] = (o * _silu(gate)).astype(o_ref.dtype)

        for step in range(G + 2):
            if step < G:
                local_scores(step)
            if 0 <= step - 1 < G:
                numerators(step - 1)
            if 0 <= step - 2 < G:
                weighted_values(step - 2)
        return 0

    lax.fori_loop(0, rows // G, group, 0)


def _nat(yb, kc, vc, rel_bias):
    B, T, _ = yb.shape
    Tc = kc.shape[2]
    n_pair = H_B // 2
    rows = T // GRID_W
    n_stack = NA_GROUP * 2 * GRID_W
    n_keys = min(NA_KH, rows) * GRID_W
    pad = jnp.zeros((H_B, N_TAB, GRID_W - N_DC), F32)
    rel = rel_bias.astype(F32)
    rb_rows = jnp.concatenate([rel[:, 0:N_TAB], pad, rel[:, 1:N_TAB + 1], pad], axis=-1)

    def col(k):
        return pl.BlockSpec((1, T, LANES), lambda p, b, k=k: (b, 0, k * n_pair + p))

    ctx = pl.BlockSpec((1, 2, Tc, DH_B), lambda p, b: (b, p, 0, 0))
    return pl.pallas_call(
        functools.partial(_nat_kernel, rows=rows),
        out_shape=jax.ShapeDtypeStruct((B, T, H_B * DH_B), BF16),
        grid=(n_pair, B),
        in_specs=[pl.BlockSpec((2, N_TAB, LANES), lambda p, b: (p, 0, 0)),
                  col(0), col(1), col(2), col(3), ctx, ctx],
        out_specs=pl.BlockSpec((1, T, LANES), lambda p, b: (b, 0, p)),
        scratch_shapes=[pltpu.VMEM((2, N_TAB, GRID_W, LANES), F32),
                        pltpu.VMEM((n_stack, LANES), BF16),
                        pltpu.VMEM((n_stack, n_keys + Tc), F32),
                        pltpu.VMEM((n_stack, n_keys + Tc), BF16),
                        pltpu.VMEM((n_stack, LANES), F32)],
        compiler_params=_cparams(2),
        name="nbr_attn",
    )(rb_rows, yb, yb, yb, yb, kc, vc)


def _seg_len(seq):
    length = -(-seq // N_SEG)
    while length % 8 != 4:
        length += 1
    return length


def _step_block(seg_len):
    return max(d for d in range(1, seg_len + 1) if seg_len % d == 0 and d * N_SEG <= RG_ROWS * 3 // 2)


def _rglru_kernel(x_ref, g_ref, cw_ref, cb_ref, wg_ref, lam_ref, *rest, seq, slabs, has_s0, emit_state):
    rest = list(rest)
    s0_ref = rest.pop(0) if has_s0 else None
    o_ref = rest.pop(0)
    hfin_ref = rest.pop(0) if emit_state else None
    xpad_ref, a_ref, u_ref, h_ref, p_ref, hn_ref = rest
    L = _seg_len(seq)
    n_rows = N_SEG * L
    RB = RG_ROWS
    CB = slabs
    TB = _step_block(L)
    n_tile = N_SEG // 8
    chains = [(d, j, s) for d in range(2) for j in range(CB) for s in range(n_tile)]

    for j in range(CB):
        xpad_ref[j, 0:8, :] = jnp.zeros((8, LANES), F32)
        xpad_ref[j, 8:seq + 8, :] = x_ref[0, :, j * LANES:(j + 1) * LANES]
        xpad_ref[j, seq + 8:n_rows + 16, :] = jnp.zeros((n_rows + 8 - seq, LANES), F32)

    nl = -lam_ref[...]
    sp = jnp.maximum(nl, 0.0) + jnp.log1p(jnp.exp(-jnp.abs(nl)))
    cw = cw_ref[...]
    cbias = cb_ref[...]
    ones2 = jnp.where(lax.broadcasted_iota(jnp.int32, (TB * N_SEG, LANES), 1) < 2, 1.0, 0.0).astype(BF16)

    def gates(blk, _):
        t0 = blk * TB
        r0 = pl.multiple_of(blk * (TB * N_SEG), TB * N_SEG)
        for j in range(CB):
            lanes = slice(j * LANES, (j + 1) * LANES)
            tiles = []
            for tt in range(TB):
                for s in range(n_tile):
                    taps = [xpad_ref[j, pl.ds(6 + k + t0 + tt + s * 8 * L, 8, stride=L), :] for k in range(4)]
                    xt = cw[0:1, lanes] * taps[0] + cw[1:2, lanes] * taps[1]
                    xt = xt + cw[2:3, lanes] * taps[2]
                    tiles.append(xt + cw[3:4, lanes] * taps[3] + cbias[:, lanes])
            xj = jnp.concatenate(tiles, axis=0)
            gt = jnp.dot(jnp.concatenate([xj.astype(BF16), ones2], axis=1), wg_ref[j],
                         preferred_element_type=F32)
            xh = 0.5 * xj
            for d in range(2):
                th_r = jnp.tanh(gt[:, (2 * d) * LANES:(2 * d + 1) * LANES])
                th_i = jnp.tanh(gt[:, (2 * d + 1) * LANES:(2 * d + 2) * LANES])
                half = (-0.5 * RG_C) * sp[d:d + 1, lanes]
                la = half + half * th_r
                a = jnp.exp(la)
                y = -jnp.tanh(la) * (1.0 + a * a)
                root = jnp.where(y > 0.0, y * lax.rsqrt(y), 0.0)
                a_ref[d, j, pl.ds(r0, TB * N_SEG), :] = a
                u_ref[d, j, pl.ds(r0, TB * N_SEG), :] = root * (xh + xh * th_i)
        return 0

    lax.fori_loop(0, L // TB, gates, 0)

    first_pad = [[min(max(seq - (s * 8 + r) * L, 0), L) for r in range(8)] for s in range(n_tile)]
    sub = lax.broadcasted_iota(jnp.int32, (8, LANES), 0)
    pad_from = []
    for s in range(n_tile):
        if all(f == L for f in first_pad[s]):
            pad_from.append(None)
        else:
            vec = jnp.full((8, LANES), L, jnp.int32)
            for r in range(8):
                vec = jnp.where(sub == r, first_pad[s][r], vec)
            pad_from.append(vec)

    def step_rows(step, s):
        return pl.ds(pl.multiple_of(step * N_SEG + s * 8, 8), 8)

    unroll = 4

    def scan(i, carry):
        carry = list(carry)
        for k in range(unroll):
            for n, (d, j, s) in enumerate(chains):
                h, pr = carry[n]
                t = i * unroll + k
                if d == 1:
                    t = L - 1 - t
                idx = step_rows(t, s)
                a = a_ref[d, j, idx, :]
                u = u_ref[d, j, idx, :]
                if pad_from[s] is not None:
                    live = t < pad_from[s]
                    a = jnp.where(live, a, 1.0)
                    u = jnp.where(live, u, 0.0)
                h = a * h + u
                pr = pr * a
                h_ref[d, j, idx, :] = h
                p_ref[d, j, idx, :] = pr
                carry[n] = (h, pr)
        return tuple(carry)

    zero = jnp.zeros((8, LANES), F32)
    one = jnp.ones((8, LANES), F32)
    ends = lax.fori_loop(0, L // unroll, scan, ((zero, one),) * len(chains))

    cins = {}
    finals = [[None] * CB for _ in range(2)]
    for d in range(2):
        for j in range(CB):
            if has_s0:
                c = s0_ref[0, d:d + 1, j * LANES:(j + 1) * LANES]
            else:
                c = jnp.zeros((1, LANES), F32)
            cin = [None] * N_SEG
            for kk in range(N_SEG):
                seg = kk if d == 0 else N_SEG - 1 - kk
                s, row = divmod(seg, 8)
                h_end, p_end = ends[chains.index((d, j, s))]
                cin[seg] = c
                c = h_end[row:row + 1, :] + p_end[row:row + 1, :] * c
            finals[d][j] = c
            for s in range(n_tile):
                cins[(d, j, s)] = jnp.concatenate(cin[s * 8:(s + 1) * 8], axis=0)

    def fix(i, _):
        for k in range(unroll):
            t = i * unroll + k
            for j in range(CB):
                for s in range(n_tile):
                    idx = step_rows(t, s)
                    parts = [h_ref[d, j, idx, :] + p_ref[d, j, idx, :] * cins[(d, j, s)] for d in range(2)]
                    hn_ref[j, pl.ds(t + s * 8 * L, 8, stride=L), :] = parts[0] + parts[1]
        return 0

    lax.fori_loop(0, L // unroll, fix, 0)

    if emit_state:
        hfin_ref[0] = jnp.concatenate([jnp.concatenate(finals[d], axis=1) for d in range(2)], axis=0)

    def combine(blk, _):
        rows = pl.ds(pl.multiple_of(blk * RB, RB), RB)
        hs = jnp.concatenate([hn_ref[j, rows, :] for j in range(CB)], axis=1)
        o_ref[0, rows, :] = (hs * _silu(g_ref[0, rows, :])).astype(o_ref.dtype)
        return 0

    lax.fori_loop(0, seq // RB, combine, 0)


def _gate_weights(gate_w, gate_b):
    w = (0.5 * gate_w).transpose(2, 3, 0, 1, 4).reshape(H_C, BW_C, 4 * BW_C).astype(BF16)
    b = (0.5 * gate_b).reshape(2, 2, H_C, BW_C).transpose(2, 0, 1, 3).reshape(H_C, 1, 4 * BW_C).astype(F32)
    hi = b.astype(BF16)
    lo = (b - hi.astype(F32)).astype(BF16)
    zeros = jnp.zeros((H_C, BW_C - 2, 4 * BW_C), BF16)
    return jnp.concatenate([w, hi, lo, zeros], axis=1)


def _rglru(xg, conv_w, conv_b, wg, lam, s0, emit_state):
    B, T, _ = xg.shape
    has_s0 = s0 is not None
    n_rows = N_SEG * _seg_len(T)
    CB = RG_SLABS if T > 4 * RG_ROWS else 2 * RG_SLABS
    wide = CB * LANES
    n_steps = H_C // CB
    in_specs = [
        pl.BlockSpec((1, T, wide), lambda b, c: (b, 0, c)),
        pl.BlockSpec((1, T, wide), lambda b, c: (b, 0, n_steps + c)),
        pl.BlockSpec((4, wide), lambda b, c: (0, c)),
        pl.BlockSpec((1, wide), lambda b, c: (0, c)),
        pl.BlockSpec((CB, 2 * BW_C, 4 * BW_C), lambda b, c: (c, 0, 0)),
        pl.BlockSpec((2, wide), lambda b, c: (0, c)),
    ]
    args = [xg, xg, conv_w, conv_b, wg, lam]
    if has_s0:
        in_specs.append(pl.BlockSpec((1, 2, wide), lambda b, c: (b, 0, c)))
        args.append(s0)
    out_shape = [jax.ShapeDtypeStruct((B, T, W_C), BF16)]
    out_specs = [pl.BlockSpec((1, T, wide), lambda b, c: (b, 0, c))]
    if emit_state:
        out_shape.append(jax.ShapeDtypeStruct((B, 2, W_C), F32))
        out_specs.append(pl.BlockSpec((1, 2, wide), lambda b, c: (b, 0, c)))
    res = pl.pallas_call(
        functools.partial(_rglru_kernel, seq=T, slabs=CB, has_s0=has_s0, emit_state=emit_state),
        out_shape=out_shape,
        grid=(B, n_steps),
        in_specs=in_specs,
        out_specs=out_specs,
        scratch_shapes=[pltpu.VMEM((CB, n_rows + 16, LANES), F32)]
        + [pltpu.VMEM((2, CB, n_rows, LANES), F32)] * 4 + [pltpu.VMEM((CB, n_rows, LANES), F32)],
        compiler_params=_cparams(2),
        name="rglru",
    )(*args)
    return res if emit_state else (res[0], None)


A_COLS = 5 * H_A * DK_A
B_COLS = 4 * H_B * DH_B


def kernel(x_prompt, x_sample, state_hgrn, cache_na_k, cache_na_v, state_rglru, c, c_ctx, norm_gain, w_mod, b_mod, w_in_even, w_out_even, hgrn_lb_logits, hgrn_out_gain, na_rel_bias, w_in_odd, w_out_odd, conv_w, conv_b, rg_gate_w, rg_gate_b, rg_lambda, final_gain):
    n_ctx = x_prompt.shape[0]
    n_lat = x_sample.shape[0]
    depth = w_mod.shape[0]

    cond = jnp.zeros((16, D_MODEL), F32).at[0].set(c_ctx).at[1:1 + n_lat].set(c)
    mod = _modulation(cond, w_mod, b_mod.reshape(depth, 1, 3 * D_MODEL))
    mod = mod.reshape(depth, 16, 3, D_MODEL)

    t_ctx = x_prompt.shape[1]

    def flat(a):
        return a.reshape(1, n_ctx * t_ctx, a.shape[-1])

    def unflat(a):
        return a.reshape(n_ctx, t_ctx, a.shape[-1])

    def in_proj_params(l):
        if l % 2 == 0:
            outs_s = ((0, A_COLS, F32), (A_COLS, B_COLS, BF16))
            outs_c = outs_s + ((A_COLS + H_B * DH_B, 2 * H_B * DH_B, F32),)
            a_key = H_A * DK_A
            col = jnp.arange(w_in_even.shape[-1])
            halve = jnp.where((col >= a_key) & (col < 3 * a_key), 0.5, 1.0).astype(F32)
            return (w_in_even[l // 2] * halve).astype(BF16), outs_c, outs_s
        outs = ((0, 2 * W_C, F32),)
        return w_in_odd[l // 2].astype(BF16), outs, outs

    xc, xs = x_prompt, x_sample
    new_hgrn, new_k, new_v, new_rg = [], [], [], []
    proj_c = proj_s = None
    for l in range(depth):
        j = l // 2
        mod_c, mod_s = mod[l, 0:1], mod[l, 1:1 + n_lat]
        if proj_c is None:
            gain = norm_gain[l].reshape(1, D_MODEL)
            w_in, outs_c, outs_s = in_proj_params(l)
            proj_c = [unflat(t) for t in _inproj(flat(xc), mod_c, gain, w_in, outs_c, 512, True)]
            proj_s = _inproj(xs, mod_s, gain, w_in, outs_s, 512, False)
        if l % 2 == 0:
            w_out = w_out_even[j].astype(BF16)
            (ya_c, yb_c, kv_c), (ya_s, yb_s) = proj_c, proj_s
            hgain = hgrn_out_gain[j].reshape(H_A, 1, DK_A)
            oa_c, s_fin = _hgrn(ya_c, hgrn_lb_logits, j, hgain, None)
            oa_s, _ = _hgrn(ya_s, hgrn_lb_logits, j, hgain, state_hgrn[:, j])
            ob_c, k_c, v_c = _ctx_attn(yb_c, kv_c)
            ob_s = _nat(yb_s, cache_na_k[:, j], cache_na_v[:, j], na_rel_bias[j])
            ys_c, ys_s = (oa_c, ob_c), (oa_s, ob_s)
            new_hgrn.append(s_fin)
            new_k.append(k_c)
            new_v.append(v_c)
        else:
            w_out = w_out_odd[j].astype(BF16)
            (xg_c,), (xg_s,) = proj_c, proj_s
            wg = _gate_weights(rg_gate_w[j], rg_gate_b[j])
            cb = conv_b[j].reshape(1, W_C)
            y_c, h_fin = _rglru(xg_c, conv_w[j], cb, wg, rg_lambda[j], None, True)
            y_s, _ = _rglru(xg_s, conv_w[j], cb, wg, rg_lambda[j], state_rglru[:, j], False)
            ys_c, ys_s = (y_c,), (y_s,)
            new_rg.append(h_fin)
        ys_c = tuple(flat(y) for y in ys_c)
        if l == depth - 1:
            fgain = final_gain.reshape(1, D_MODEL)
            (xc,) = _outproj(ys_c, flat(xc), mod_c, w_out, 1024, True, final_gain=fgain)
            (xs,) = _outproj(ys_s, xs, mod_s, w_out, 1024, False, final_gain=fgain)
            xc = unflat(xc)
        else:
            gain_n = norm_gain[l + 1].reshape(1, D_MODEL)
            w_n, outs_c, outs_s = in_proj_params(l + 1)
            mod_cn, mod_sn = mod[l + 1, 0:1], mod[l + 1, 1:1 + n_lat]
            xc, *proj_c = _outproj(ys_c, flat(xc), mod_c, w_out, 512, True,
                                   next_proj=(mod_cn, gain_n, w_n, outs_c))
            xs, *proj_s = _outproj(ys_s, xs, mod_s, w_out, 512, False,
                                   next_proj=(mod_sn, gain_n, w_n, outs_s))
            xc = unflat(xc)
            proj_c = [unflat(t) for t in proj_c]
    return (xc, xs, jnp.stack(new_hgrn, axis=1), jnp.stack(new_k, axis=1),
            jnp.stack(new_v, axis=1), jnp.stack(new_rg, axis=1))
```

```python
import functools

import jax
import jax.numpy as jnp
from jax import lax
from jax.experimental import pallas as pl
from jax.experimental.pallas import tpu as pltpu

F32 = jnp.float32
BF16 = jnp.bfloat16

D_MODEL = 1024
EPS = 1e-6
NEG_INF = -1e30
H_A = 4
DK_A = 128
HGRN_CHUNK = 32
HGRN_ROWS = 256
H_B = 8
DH_B = 64
GRID_W = 64
NA_KH = 8
NA_KW = 16
NA_GROUP = 8
W_C = 1024
H_C = 8
BW_C = W_C // H_C
RG_C = 8.0
RG_ROWS = 256
RG_SLABS = 2
N_SEG = 16
LANES = 128
VMEM_LIMIT = 48 * 1024 * 1024

NT_DIMS = (((1,), (1,)), ((), ()))


def _silu(x):
    half = 0.5 * x
    return half + half * jnp.tanh(half)


def _cparams(n_axes):
    return pltpu.CompilerParams(dimension_semantics=("arbitrary",) * n_axes,
                                vmem_limit_bytes=VMEM_LIMIT)


def _mod_kernel(cond_ref, w_ref, b_ref, o_ref):
    s = _silu(cond_ref[...])
    o_ref[0] = jnp.dot(s.astype(BF16), w_ref[0].astype(BF16), preferred_element_type=F32) + b_ref[0]


def _modulation(cond, w_mod, b_mod):
    depth = w_mod.shape[0]
    n_rows = cond.shape[0]
    return pl.pallas_call(
        _mod_kernel,
        out_shape=jax.ShapeDtypeStruct((depth, n_rows, 3 * D_MODEL), F32),
        grid=(depth, 3),
        in_specs=[
            pl.BlockSpec((n_rows, D_MODEL), lambda l, n: (0, 0)),
            pl.BlockSpec((1, D_MODEL, D_MODEL), lambda l, n: (l, 0, n)),
            pl.BlockSpec((1, 1, D_MODEL), lambda l, n: (l, 0, n)),
        ],
        out_specs=pl.BlockSpec((1, n_rows, D_MODEL), lambda l, n: (l, 0, n)),
        compiler_params=_cparams(2),
        name="adaln_mod",
    )(cond, w_mod, b_mod)


def _project(x, mod_ref, gain_ref, w_ref, out_refs, outs):
    var = jnp.mean(x * x, axis=-1, keepdims=True)
    y = x * lax.rsqrt(var + EPS) * gain_ref[...]
    h = y * (1.0 + mod_ref[0, 1:2, :]) + mod_ref[0, 0:1, :]
    hb = h.astype(BF16)
    step = 512
    for c in range(0, w_ref.shape[1], step):
        users = [(o_ref, c - col0) for o_ref, (col0, width, _) in zip(out_refs, outs)
                 if col0 <= c < col0 + width]
        if users:
            r = jnp.dot(hb, w_ref[:, c:c + step], preferred_element_type=F32)
            for o_ref, off in users:
                o_ref[0, :, off:off + step] = r.astype(o_ref.dtype)


def _inproj_kernel(x_ref, mod_ref, gain_ref, w_ref, *out_refs, outs):
    _project(x_ref[0], mod_ref, gain_ref, w_ref, out_refs, outs)


def _inproj(x, mod, gain, w, outs, tm, shared_mod):
    B, T, _ = x.shape
    n_cols = w.shape[1]
    mod_map = (lambda b, t: (0, 0, 0)) if shared_mod else (lambda b, t: (b, 0, 0))
    return pl.pallas_call(
        functools.partial(_inproj_kernel, outs=outs),
        out_shape=[jax.ShapeDtypeStruct((B, T, wd), dt) for _, wd, dt in outs],
        grid=(B, T // tm),
        in_specs=[
            pl.BlockSpec((1, tm, D_MODEL), lambda b, t: (b, t, 0)),
            pl.BlockSpec((1, 3, D_MODEL), mod_map),
            pl.BlockSpec((1, D_MODEL), lambda b, t: (0, 0)),
            pl.BlockSpec((D_MODEL, n_cols), lambda b, t: (0, 0)),
        ],
        out_specs=[pl.BlockSpec((1, tm, wd), lambda b, t: (b, t, 0)) for _, wd, _ in outs],
        compiler_params=_cparams(2),
        name="in_proj",
    )(x, mod, gain, w)


def _outproj_kernel(*refs, n_y, final, next_outs):
    y_refs, (x_ref, mod_ref, w_ref), rest = refs[:n_y], refs[n_y:n_y + 3], refs[n_y + 3:]
    m = None
    row = 0
    for y_ref in y_refs:
        width = y_ref.shape[-1]
        part = jnp.dot(y_ref[0], w_ref[row:row + width, :], preferred_element_type=F32)
        m = part if m is None else m + part
        row += width
    xn = x_ref[0] + mod_ref[0, 2:3, :] * m
    if final:
        gain_ref, o_ref = rest
        var = jnp.mean(xn * xn, axis=-1, keepdims=True)
        o_ref[0] = xn * lax.rsqrt(var + EPS) * gain_ref[...]
    else:
        modn_ref, gainn_ref, wn_ref, o_ref = rest[:4]
        o_ref[0] = xn
        _project(xn, modn_ref, gainn_ref, wn_ref, rest[4:], next_outs)


def _outproj(ys, x, mod, w, tm, shared_mod, final_gain=None, next_proj=None):
    B, T, _ = x.shape
    final = final_gain is not None
    mod_map = (lambda b, t: (0, 0, 0)) if shared_mod else (lambda b, t: (b, 0, 0))
    row_block = pl.BlockSpec((1, tm, D_MODEL), lambda b, t: (b, t, 0))
    vec = pl.BlockSpec((1, D_MODEL), lambda b, t: (0, 0))
    in_specs = [pl.BlockSpec((1, tm, y.shape[-1]), lambda b, t: (b, t, 0)) for y in ys] + [
        row_block,
        pl.BlockSpec((1, 3, D_MODEL), mod_map),
        pl.BlockSpec((w.shape[0], D_MODEL), lambda b, t: (0, 0)),
    ]
    args = list(ys) + [x, mod, w]
    out_shape = [jax.ShapeDtypeStruct((B, T, D_MODEL), F32)]
    out_specs = [row_block]
    next_outs = None
    if final:
        in_specs.append(vec)
        args.append(final_gain)
    else:
        mod_n, gain_n, w_n, next_outs = next_proj
        in_specs += [pl.BlockSpec((1, 3, D_MODEL), mod_map), vec,
                     pl.BlockSpec((D_MODEL, w_n.shape[1]), lambda b, t: (0, 0))]
        args += [mod_n, gain_n, w_n]
        out_shape += [jax.ShapeDtypeStruct((B, T, wd), dt) for _, wd, dt in next_outs]
        out_specs += [pl.BlockSpec((1, tm, wd), lambda b, t: (b, t, 0)) for _, wd, _ in next_outs]
    return pl.pallas_call(
        functools.partial(_outproj_kernel, n_y=len(ys), final=final, next_outs=next_outs),
        out_shape=out_shape,
        grid=(B, T // tm),
        in_specs=in_specs,
        out_specs=out_specs,
        compiler_params=_cparams(2),
        name="out_proj",
    )(*args)


def _hgrn_kernel(q_ref, zf_ref, zb_ref, v_ref, g_ref, lgt_ref, gain_ref, *rest, seq, layer, per_block):
    rest = list(rest)
    s0_ref = None if per_block else rest.pop(0)
    o_ref = rest.pop(0)
    sfin_ref = rest.pop(0) if per_block else None
    acc_ref, qd_ref, ki_ref, kd_ref, kv_ref, st_ref, dec_ref, mst_ref, msk_ref, mexp_ref = rest
    R = HGRN_ROWS
    C = HGRN_CHUNK
    n_blk = seq // R
    n_chunk = R // C
    n_all = seq // C

    @pl.when((pl.program_id(0) == 0) & (pl.program_id(1) == 0))
    def _build_masks():
        ti = lax.broadcasted_iota(jnp.int32, (R, R), 0)
        tj = lax.broadcasted_iota(jnp.int32, (R, R), 1)
        shift = C.bit_length() - 1
        same = lax.shift_right_logical(ti, shift) == lax.shift_right_logical(tj, shift)
        one = jnp.ones((R, R), F32)
        zero = jnp.zeros((R, R), F32)
        incl = (jnp.where(same, jnp.where(tj <= ti, one, zero), zero),
                jnp.where(same, jnp.where(tj >= ti, one, zero), zero))
        for d in range(2):
            msk_ref[d] = incl[d]
            mst_ref[d] = incl[d].astype(BF16)
        rr = lax.broadcasted_iota(jnp.int32, (R, n_chunk * LANES), 0)
        cc = lax.broadcasted_iota(jnp.int32, (R, n_chunk * LANES), 1)
        own = lax.shift_right_logical(rr, shift) == lax.shift_right_logical(cc, LANES.bit_length() - 1)
        mexp_ref[...] = jnp.where(own, 1.0, 0.0).astype(BF16)

    lgt = [lgt_ref[:, i, :] for i in range(lgt_ref.shape[1])]
    lmax = functools.reduce(jnp.maximum, lgt)
    ex = [jnp.exp(t - lmax) for t in lgt]
    lb_all = sum(ex[:layer + 1]) / sum(ex)
    gain = gain_ref[0]

    blocks_per_trip = 2 if n_blk % 2 == 0 else 1

    def gates(i, _):
        for u in range(blocks_per_trip):
            blk = i * blocks_per_trip + u
            rows = pl.ds(pl.multiple_of(blk * R, R), R)
            q = q_ref[0, rows, :]
            for d in range(2):
                th = jnp.tanh((zf_ref if d == 0 else zb_ref)[0, rows, :])
                lb = lb_all[d:d + 1, :]
                c = 0.5 * (1.0 - lb)
                ct = c * th
                f = (lb + c) + ct
                k = c - ct
                logf = jnp.log(f)
                hi = logf.astype(BF16)
                lo = (logf - hi.astype(F32)).astype(BF16)
                cs = jnp.dot(mst_ref[d], jnp.concatenate([hi, lo], axis=1), preferred_element_type=F32)
                b = cs[:, 0:LANES] + cs[:, LANES:2 * LANES]
                ends = [c * C + (C - 1 if d == 0 else 0) for c in range(n_chunk)]
                btot = jnp.concatenate([jnp.broadcast_to(b[t:t + 1, :], (C, LANES)) for t in ends], axis=0)
                qd_ref[d, rows, :] = (q * jnp.exp(b)).astype(BF16)
                ki_ref[d, rows, :] = (k * jnp.exp(-b)).astype(BF16)
                kd_ref[d, rows, :] = (k * jnp.exp(btot - b)).astype(BF16)
                for c in range(n_chunk):
                    dec_ref[d, blk * n_chunk + c] = jnp.exp(btot[c * C:c * C + 8, :])
        return 0

    lax.fori_loop(0, n_blk // blocks_per_trip, gates, 0)

    def intra(i, _):
        for u in range(blocks_per_trip):
            blk = i * blocks_per_trip + u
            rows = pl.ds(pl.multiple_of(blk * R, R), R)
            v = v_ref[0, rows, :]
            vb = v.astype(BF16)
            vt = v.T.astype(BF16)
            att_sum = None
            for d in range(2):
                att = lax.dot_general(qd_ref[d, rows, :], ki_ref[d, rows, :], NT_DIMS,
                                      preferred_element_type=F32)
                att = jnp.where(msk_ref[d] > 0.5, att, 0.0)
                att_sum = att if att_sum is None else att_sum + att
                kd_exp = jnp.concatenate([kd_ref[d, rows, :]] * n_chunk, axis=1) * mexp_ref[...]
                kv_all = jnp.dot(vt, kd_exp, preferred_element_type=F32)
                for c in range(n_chunk):
                    kv_ref[d, blk * n_chunk + c] = kv_all[:, c * LANES:(c + 1) * LANES]
            acc_ref[rows, :] = jnp.dot(att_sum.astype(BF16), vb, preferred_element_type=F32)
        return 0

    lax.fori_loop(0, n_blk // blocks_per_trip, intra, 0)

    unroll = 4

    def states(i, sts):
        sts = list(sts)
        for u in range(unroll):
            n = i * unroll + u
            for d in range(2):
                c = n if d == 0 else n_all - 1 - n
                st_ref[d, c] = sts[d].astype(BF16)
                dec = jnp.concatenate([dec_ref[d, c]] * (DK_A // 8), axis=0)
                sts[d] = sts[d] * dec + kv_ref[d, c]
        return tuple(sts)

    def block_states(blk, _):
        for d in range(2):
            st = jnp.zeros((DK_A, DK_A), F32)
            for cc in range(n_chunk):
                c = blk * n_chunk + (cc if d == 0 else n_chunk - 1 - cc)
                st_ref[d, c] = st.astype(BF16)
                dec = jnp.concatenate([dec_ref[d, c]] * (DK_A // 8), axis=0)
                st = st * dec + kv_ref[d, c]
            sfin_ref[blk, d, 0] = st.T
        return 0

    if per_block:
        lax.fori_loop(0, n_blk, block_states, 0)
    else:
        st0 = (s0_ref[0, 0, 0].T, s0_ref[0, 1, 0].T)
        lax.fori_loop(0, n_all // unroll, states, st0)

    lane_chunk = lax.shift_right_logical(lax.broadcasted_iota(jnp.int32, (DK_A, R), 1), C.bit_length() - 1)

    finish_blocks = 4 if n_blk % 4 == 0 else blocks_per_trip

    def finish(i, _):
        slabs = []
        for u in range(finish_blocks):
            blk = i * finish_blocks + u
            rows = pl.ds(pl.multiple_of(blk * R, R), R)
            inter_t = None
            for d in range(2):
                sts = st_ref[d, pl.ds(blk * n_chunk, n_chunk)].reshape(n_chunk * DK_A, DK_A)
                res = lax.dot_general(sts, qd_ref[d, rows, :], NT_DIMS, preferred_element_type=F32)
                picked = res[0:DK_A]
                for c in range(1, n_chunk):
                    picked = jnp.where(lane_chunk == c, res[c * DK_A:(c + 1) * DK_A], picked)
                inter_t = picked if inter_t is None else inter_t + picked
            slabs.append((rows, inter_t))
        for rows, inter_t in slabs:
            tot = acc_ref[rows, :] + inter_t.T
            var = jnp.mean(tot * tot, axis=-1, keepdims=True)
            y = tot * lax.rsqrt(var + EPS) * gain
            o_ref[0, rows, :] = (y * _silu(g_ref[0, rows, :])).astype(o_ref.dtype)
        return 0

    lax.fori_loop(0, n_blk // finish_blocks, finish, 0)


def _hgrn(ya, lgt, layer, gain, s0):
    B, T, width = ya.shape
    per_block = s0 is None
    if per_block:
        assert T == HGRN_ROWS
        group = 8 if B % 8 == 0 else 1
        n_seq, B, T = B, B // group, group * T
        ya = ya.reshape(B, T, width)

    def col(k):
        return pl.BlockSpec((1, T, LANES), lambda b, h, k=k: (b, 0, k * H_A + h))

    in_specs = [col(0), col(1), col(2), col(3), col(4),
                pl.BlockSpec((2, lgt.shape[1], LANES), lambda b, h: (0, 0, h)),
                pl.BlockSpec((1, 1, LANES), lambda b, h: (h, 0, 0))]
    args = [ya, ya, ya, ya, ya, lgt, gain]
    out_shape = [jax.ShapeDtypeStruct((B, T, H_A * DK_A), BF16)]
    out_specs = [pl.BlockSpec((1, T, LANES), lambda b, h: (b, 0, h))]
    if per_block:
        out_shape.append(jax.ShapeDtypeStruct((n_seq, 2, H_A, DK_A, DK_A), F32))
        out_specs.append(pl.BlockSpec((T // HGRN_ROWS, 2, 1, DK_A, DK_A), lambda b, h: (b, 0, h, 0, 0)))
    else:
        in_specs.append(pl.BlockSpec((1, 2, 1, DK_A, DK_A), lambda b, h: (b, 0, h, 0, 0)))
        args.append(s0)
    res = pl.pallas_call(
        functools.partial(_hgrn_kernel, seq=T, layer=layer, per_block=per_block),
        out_shape=out_shape,
        grid=(B, H_A),
        in_specs=in_specs,
        out_specs=out_specs,
        scratch_shapes=[pltpu.VMEM((T, LANES), F32),
                        pltpu.VMEM((2, T, LANES), BF16),
                        pltpu.VMEM((2, T, LANES), BF16),
                        pltpu.VMEM((2, T, LANES), BF16),
                        pltpu.VMEM((2, T // HGRN_CHUNK, DK_A, DK_A), F32),
                        pltpu.VMEM((2, T // HGRN_CHUNK, DK_A, DK_A), BF16),
                        pltpu.VMEM((2, T // HGRN_CHUNK, 8, LANES), F32),
                        pltpu.VMEM((2, HGRN_ROWS, HGRN_ROWS), BF16),
                        pltpu.VMEM((2, HGRN_ROWS, HGRN_ROWS), F32),
                        pltpu.VMEM((HGRN_ROWS, HGRN_ROWS // HGRN_CHUNK * LANES), BF16)],
        compiler_params=_cparams(2),
        name="hgrn2",
    )(*args)
    if per_block:
        return res[0].reshape(n_seq, HGRN_ROWS, H_A * DK_A), res[1]
    return res[0], None


def _head_masks():
    lane = lax.broadcasted_iota(jnp.int32, (1, LANES), 1)
    return lane < DH_B, lane >= DH_B


def _ctx_attn_kernel(q_ref, k_ref, v_ref, g_ref, kv_ref, o_ref, newk_ref, newv_ref):
    scale = DH_B ** -0.5
    masks = _head_masks()
    T = q_ref.shape[1]
    for h in range(H_B):
        newk_ref[0, h] = kv_ref[0, :, h * DH_B:(h + 1) * DH_B]
        newv_ref[0, h] = kv_ref[0, :, (H_B + h) * DH_B:(H_B + h + 1) * DH_B]
    for p in range(H_B // 2):
        cols = slice(p * LANES, (p + 1) * LANES)
        q = q_ref[0, :, cols] * scale
        qs = jnp.concatenate([jnp.where(masks[h], q, jnp.zeros_like(q)) for h in range(2)], axis=0)
        s = lax.dot_general(qs, k_ref[0, :, cols], NT_DIMS, preferred_element_type=F32)
        e = jnp.exp(s - jnp.max(s, axis=-1, keepdims=True))
        pr = e / jnp.sum(e, axis=-1, keepdims=True)
        o = jnp.dot(pr.astype(BF16), v_ref[0, :, cols], preferred_element_type=F32)
        o = jnp.where(masks[0], o[0:T], o[T:2 * T])
        o_ref[0, :, cols] = (o * _silu(g_ref[0, :, cols].astype(F32))).astype(o_ref.dtype)


def _ctx_attn(yb, kv):
    B, T, _ = yb.shape
    width = H_B * DH_B

    def col(k):
        return pl.BlockSpec((1, T, width), lambda b, k=k: (b, 0, k))

    cache = pl.BlockSpec((1, H_B, T, DH_B), lambda b: (b, 0, 0, 0))
    return pl.pallas_call(
        _ctx_attn_kernel,
        out_shape=[jax.ShapeDtypeStruct((B, T, width), BF16),
                   jax.ShapeDtypeStruct((B, H_B, T, DH_B), F32),
                   jax.ShapeDtypeStruct((B, H_B, T, DH_B), F32)],
        grid=(B,),
        in_specs=[col(0), col(1), col(2), col(3), pl.BlockSpec((1, T, 2 * width), lambda b: (b, 0, 0))],
        out_specs=[pl.BlockSpec((1, T, width), lambda b: (b, 0, 0)), cache, cache],
        compiler_params=_cparams(1),
        name="ctx_attn",
    )(yb, yb, yb, yb, kv)


N_DR = 2 * NA_KH - 1
N_DC = 2 * NA_KW - 1
N_TAB = N_DR - 1


def _nat_kernel(rb_ref, q_ref, k_ref, v_ref, g_ref, kc_ref, vc_ref, o_ref,
                tab_ref, qs_ref, s_ref, p_ref, r_ref, *, rows):
    scale = DH_B ** -0.5
    kh = min(NA_KH, rows)
    masks = _head_masks()

    @pl.when(pl.program_id(1) == 0)
    def _build_tables():
        c = lax.broadcasted_iota(jnp.int32, (GRID_W, LANES), 0)
        lane = lax.broadcasted_iota(jnp.int32, (GRID_W, LANES), 1)
        kcol = lane & (GRID_W - 1)
        ws = jnp.clip(c - NA_KW // 2, 0, GRID_W - NA_KW)
        neg = jnp.full((GRID_W, LANES), NEG_INF, F32)
        inside = jnp.where(kcol >= ws, jnp.where(kcol < ws + NA_KW, 1.0, 0.0), 0.0) > 0.5
        for h in range(2):
            for i in range(N_TAB):
                row = jnp.broadcast_to(rb_ref[h, i:i + 1, :], (GRID_W, LANES))
                toeplitz = pltpu.roll(row, LANES - (NA_KW - 1), 1, stride=1, stride_axis=0)
                tab_ref[h, i] = jnp.where(inside, toeplitz, neg)

    kc = jnp.concatenate([kc_ref[0, 0], kc_ref[0, 1]], axis=1).astype(BF16)
    vc = jnp.concatenate([vc_ref[0, 0], vc_ref[0, 1]], axis=1).astype(BF16)
    n_keys = kh * GRID_W
    n_ctx = kc.shape[0]
    G = NA_GROUP
    W2 = 2 * GRID_W

    def group(gi, _):
        r_first = gi * G
        q0 = pl.multiple_of(r_first * GRID_W, G * GRID_W)
        for i in range(G):
            qi = q_ref[0, pl.ds(q0 + i * GRID_W, GRID_W), :] * scale
            for h in range(2):
                qs_ref[i * W2 + h * GRID_W:i * W2 + (h + 1) * GRID_W, :] = jnp.where(
                    masks[h], qi, jnp.zeros_like(qi))
        s_ref[:, n_keys:n_keys + n_ctx] = lax.dot_general(qs_ref[...], kc, NT_DIMS,
                                                          preferred_element_type=F32)
        windows = {}

        def local_scores(i):
            r = r_first + i
            rs = jnp.clip(r - kh // 2, 0, rows - kh)
            k0 = pl.multiple_of(rs * GRID_W, GRID_W)
            windows[i] = k0
            dr0 = rs - r + (NA_KH - 1)
            bias = jnp.concatenate(
                [jnp.concatenate([tab_ref[h, dr0 + 2 * m] for m in range(kh // 2)], axis=1)
                 for h in range(2)], axis=0)
            s_ref[i * W2:(i + 1) * W2, 0:n_keys] = lax.dot_general(
                qs_ref[i * W2:(i + 1) * W2, :], k_ref[0, pl.ds(k0, n_keys), :], NT_DIMS,
                preferred_element_type=F32) + bias

        def numerators(i):
            s = s_ref[i * W2:(i + 1) * W2, :]
            e = jnp.exp(s - jnp.max(s, axis=-1, keepdims=True))
            p_ref[i * W2:(i + 1) * W2, :] = e.astype(BF16)
            rinv = 1.0 / jnp.sum(e, axis=-1, keepdims=True)
            r_ref[i * W2:(i + 1) * W2, :] = jnp.broadcast_to(rinv, (W2, LANES))

        def weighted_values(i):
            vals = jnp.concatenate([v_ref[0, pl.ds(windows[i], n_keys), :], vc], axis=0)
            o = jnp.dot(p_ref[i * W2:(i + 1) * W2, :], vals, preferred_element_type=F32)
            o = o * r_ref[i * W2:(i + 1) * W2, :]
            o = jnp.where(masks[0], o[0:GRID_W], o[GRID_W:W2])
            out_rows = pl.ds(q0 + i * GRID_W, GRID_W)
            gate = g_ref[0, out_rows, :].astype(F32)
            o_ref[0, out_rows, :] = (o * _silu(gate)).astype(o_ref.dtype)

        for step in range(G + 2):
            if step < G:
                local_scores(step)
            if 0 <= step - 1 < G:
                numerators(step - 1)
            if 0 <= step - 2 < G:
                weighted_values(step - 2)
        return 0

    lax.fori_loop(0, rows // G, group, 0)


def _nat(yb, kc, vc, rel_bias):
    B, T, _ = yb.shape
    Tc = kc.shape[2]
    n_pair = H_B // 2
    rows = T // GRID_W
    n_stack = NA_GROUP * 2 * GRID_W
    n_keys = min(NA_KH, rows) * GRID_W
    pad = jnp.zeros((H_B, N_TAB, GRID_W - N_DC), F32)
    rel = rel_bias.astype(F32)
    rb_rows = jnp.concatenate([rel[:, 0:N_TAB], pad, rel[:, 1:N_TAB + 1], pad], axis=-1)

    def col(k):
        return pl.BlockSpec((1, T, LANES), lambda p, b, k=k: (b, 0, k * n_pair + p))

    ctx = pl.BlockSpec((1, 2, Tc, DH_B), lambda p, b: (b, p, 0, 0))
    return pl.pallas_call(
        functools.partial(_nat_kernel, rows=rows),
        out_shape=jax.ShapeDtypeStruct((B, T, H_B * DH_B), BF16),
        grid=(n_pair, B),
        in_specs=[pl.BlockSpec((2, N_TAB, LANES), lambda p, b: (p, 0, 0)),
                  col(0), col(1), col(2), col(3), ctx, ctx],
        out_specs=pl.BlockSpec((1, T, LANES), lambda p, b: (b, 0, p)),
        scratch_shapes=[pltpu.VMEM((2, N_TAB, GRID_W, LANES), F32),
                        pltpu.VMEM((n_stack, LANES), BF16),
                        pltpu.VMEM((n_stack, n_keys + Tc), F32),
                        pltpu.VMEM((n_stack, n_keys + Tc), BF16),
                        pltpu.VMEM((n_stack, LANES), F32)],
        compiler_params=_cparams(2),
        name="nbr_attn",
    )(rb_rows, yb, yb, yb, yb, kc, vc)


def _seg_len(seq):
    length = -(-seq // N_SEG)
    while length % 8 != 4:
        length += 1
    return length


def _step_block(seg_len):
    return max(d for d in range(1, seg_len + 1) if seg_len % d == 0 and d * N_SEG <= RG_ROWS * 3 // 2)


def _rglru_kernel(x_ref, g_ref, cw_ref, cb_ref, wg_ref, lam_ref, *rest, seq, slabs, has_s0, emit_state):
    rest = list(rest)
    s0_ref = rest.pop(0) if has_s0 else None
    o_ref = rest.pop(0)
    hfin_ref = rest.pop(0) if emit_state else None
    xpad_ref, a_ref, u_ref, h_ref, p_ref, hn_ref = rest
    L = _seg_len(seq)
    n_rows = N_SEG * L
    RB = RG_ROWS
    CB = slabs
    TB = _step_block(L)
    n_tile = N_SEG // 8
    chains = [(d, j, s) for d in range(2) for j in range(CB) for s in range(n_tile)]

    for j in range(CB):
        xpad_ref[j, 0:8, :] = jnp.zeros((8, LANES), F32)
        xpad_ref[j, 8:seq + 8, :] = x_ref[0, :, j * LANES:(j + 1) * LANES]
        xpad_ref[j, seq + 8:n_rows + 16, :] = jnp.zeros((n_rows + 8 - seq, LANES), F32)

    nl = -lam_ref[...]
    sp = jnp.maximum(nl, 0.0) + jnp.log1p(jnp.exp(-jnp.abs(nl)))
    cw = cw_ref[...]
    cbias = cb_ref[...]
    ones2 = jnp.where(lax.broadcasted_iota(jnp.int32, (TB * N_SEG, LANES), 1) < 2, 1.0, 0.0).astype(BF16)

    def gates(blk, _):
        t0 = blk * TB
        r0 = pl.multiple_of(blk * (TB * N_SEG), TB * N_SEG)
        for j in range(CB):
            lanes = slice(j * LANES, (j + 1) * LANES)
            tiles = []
            for tt in range(TB):
                for s in range(n_tile):
                    taps = [xpad_ref[j, pl.ds(6 + k + t0 + tt + s * 8 * L, 8, stride=L), :] for k in range(4)]
                    xt = cw[0:1, lanes] * taps[0] + cw[1:2, lanes] * taps[1]
                    xt = xt + cw[2:3, lanes] * taps[2]
                    tiles.append(xt + cw[3:4, lanes] * taps[3] + cbias[:, lanes])
            xj = jnp.concatenate(tiles, axis=0)
            gt = jnp.dot(jnp.concatenate([xj.astype(BF16), ones2], axis=1), wg_ref[j],
                         preferred_element_type=F32)
            xh = 0.5 * xj
            for d in range(2):
                th_r = jnp.tanh(gt[:, (2 * d) * LANES:(2 * d + 1) * LANES])
                th_i = jnp.tanh(gt[:, (2 * d + 1) * LANES:(2 * d + 2) * LANES])
                half = (-0.5 * RG_C) * sp[d:d + 1, lanes]
                la = half + half * th_r
                a = jnp.exp(la)
                y = -jnp.tanh(la) * (1.0 + a * a)
                root = jnp.where(y > 0.0, y * lax.rsqrt(y), 0.0)
                a_ref[d, j, pl.ds(r0, TB * N_SEG), :] = a
                u_ref[d, j, pl.ds(r0, TB * N_SEG), :] = root * (xh + xh * th_i)
        return 0

    lax.fori_loop(0, L // TB, gates, 0)

    first_pad = [[min(max(seq - (s * 8 + r) * L, 0), L) for r in range(8)] for s in range(n_tile)]
    sub = lax.broadcasted_iota(jnp.int32, (8, LANES), 0)
    pad_from = []
    for s in range(n_tile):
        if all(f == L for f in first_pad[s]):
            pad_from.append(None)
        else:
            vec = jnp.full((8, LANES), L, jnp.int32)
            for r in range(8):
                vec = jnp.where(sub == r, first_pad[s][r], vec)
            pad_from.append(vec)

    def step_rows(step, s):
        return pl.ds(pl.multiple_of(step * N_SEG + s * 8, 8), 8)

    unroll = 4

    def scan(i, carry):
        carry = list(carry)
        for k in range(unroll):
            for n, (d, j, s) in enumerate(chains):
                h, pr = carry[n]
                t = i * unroll + k
                if d == 1:
                    t = L - 1 - t
                idx = step_rows(t, s)
                a = a_ref[d, j, idx, :]
                u = u_ref[d, j, idx, :]
                if pad_from[s] is not None:
                    live = t < pad_from[s]
                    a = jnp.where(live, a, 1.0)
                    u = jnp.where(live, u, 0.0)
                h = a * h + u
                pr = pr * a
                h_ref[d, j, idx, :] = h
                p_ref[d, j, idx, :] = pr
                carry[n] = (h, pr)
        return tuple(carry)

    zero = jnp.zeros((8, LANES), F32)
    one = jnp.ones((8, LANES), F32)
    ends = lax.fori_loop(0, L // unroll, scan, ((zero, one),) * len(chains))

    cins = {}
    finals = [[None] * CB for _ in range(2)]
    for d in range(2):
        for j in range(CB):
            if has_s0:
                c = s0_ref[0, d:d + 1, j * LANES:(j + 1) * LANES]
            else:
                c = jnp.zeros((1, LANES), F32)
            cin = [None] * N_SEG
            for kk in range(N_SEG):
                seg = kk if d == 0 else N_SEG - 1 - kk
                s, row = divmod(seg, 8)
                h_end, p_end = ends[chains.index((d, j, s))]
                cin[seg] = c
                c = h_end[row:row + 1, :] + p_end[row:row + 1, :] * c
            finals[d][j] = c
            for s in range(n_tile):
                cins[(d, j, s)] = jnp.concatenate(cin[s * 8:(s + 1) * 8], axis=0)

    def fix(i, _):
        for k in range(unroll):
            t = i * unroll + k
            for j in range(CB):
                for s in range(n_tile):
                    idx = step_rows(t, s)
                    parts = [h_ref[d, j, idx, :] + p_ref[d, j, idx, :] * cins[(d, j, s)] for d in range(2)]
                    hn_ref[j, pl.ds(t + s * 8 * L, 8, stride=L), :] = parts[0] + parts[1]
        return 0

    lax.fori_loop(0, L // unroll, fix, 0)

    if emit_state:
        hfin_ref[0] = jnp.concatenate([jnp.concatenate(finals[d], axis=1) for d in range(2)], axis=0)

    def combine(blk, _):
        rows = pl.ds(pl.multiple_of(blk * RB, RB), RB)
        hs = jnp.concatenate([hn_ref[j, rows, :] for j in range(CB)], axis=1)
        o_ref[0, rows, :] = (hs * _silu(g_ref[0, rows, :])).astype(o_ref.dtype)
        return 0

    lax.fori_loop(0, seq // RB, combine, 0)


def _gate_weights(gate_w, gate_b):
    w = (0.5 * gate_w).transpose(2, 3, 0, 1, 4).reshape(H_C, BW_C, 4 * BW_C).astype(BF16)
    b = (0.5 * gate_b).reshape(2, 2, H_C, BW_C).transpose(2, 0, 1, 3).reshape(H_C, 1, 4 * BW_C).astype(F32)
    hi = b.astype(BF16)
    lo = (b - hi.astype(F32)).astype(BF16)
    zeros = jnp.zeros((H_C, BW_C - 2, 4 * BW_C), BF16)
    return jnp.concatenate([w, hi, lo, zeros], axis=1)


def _rglru(xg, conv_w, conv_b, wg, lam, s0, emit_state):
    B, T, _ = xg.shape
    has_s0 = s0 is not None
    n_rows = N_SEG * _seg_len(T)
    CB = RG_SLABS if T > 4 * RG_ROWS else 2 * RG_SLABS
    wide = CB * LANES
    n_steps = H_C // CB
    in_specs = [
        pl.BlockSpec((1, T, wide), lambda b, c: (b, 0, c)),
        pl.BlockSpec((1, T, wide), lambda b, c: (b, 0, n_steps + c)),
        pl.BlockSpec((4, wide), lambda b, c: (0, c)),
        pl.BlockSpec((1, wide), lambda b, c: (0, c)),
        pl.BlockSpec((CB, 2 * BW_C, 4 * BW_C), lambda b, c: (c, 0, 0)),
        pl.BlockSpec((2, wide), lambda b, c: (0, c)),
    ]
    args = [xg, xg, conv_w, conv_b, wg, lam]
    if has_s0:
        in_specs.append(pl.BlockSpec((1, 2, wide), lambda b, c: (b, 0, c)))
        args.append(s0)
    out_shape = [jax.ShapeDtypeStruct((B, T, W_C), BF16)]
    out_specs = [pl.BlockSpec((1, T, wide), lambda b, c: (b, 0, c))]
    if emit_state:
        out_shape.append(jax.ShapeDtypeStruct((B, 2, W_C), F32))
        out_specs.append(pl.BlockSpec((1, 2, wide), lambda b, c: (b, 0, c)))
    res = pl.pallas_call(
        functools.partial(_rglru_kernel, seq=T, slabs=CB, has_s0=has_s0, emit_state=emit_state),
        out_shape=out_shape,
        grid=(B, n_steps),
        in_specs=in_specs,
        out_specs=out_specs,
        scratch_shapes=[pltpu.VMEM((CB, n_rows + 16, LANES), F32)]
        + [pltpu.VMEM((2, CB, n_rows, LANES), F32)] * 4 + [pltpu.VMEM((CB, n_rows, LANES), F32)],
        compiler_params=_cparams(2),
        name="rglru",
    )(*args)
    return res if emit_state else (res[0], None)


A_COLS = 5 * H_A * DK_A
B_COLS = 4 * H_B * DH_B


def kernel(x_prompt, x_sample, state_hgrn, cache_na_k, cache_na_v, state_rglru, c, c_ctx, norm_gain, w_mod, b_mod, w_in_even, w_out_even, hgrn_lb_logits, hgrn_out_gain, na_rel_bias, w_in_odd, w_out_odd, conv_w, conv_b, rg_gate_w, rg_gate_b, rg_lambda, final_gain):
    n_ctx = x_prompt.shape[0]
    n_lat = x_sample.shape[0]
    depth = w_mod.shape[0]

    cond = jnp.zeros((16, D_MODEL), F32).at[0].set(c_ctx).at[1:1 + n_lat].set(c)
    mod = _modulation(cond, w_mod, b_mod.reshape(depth, 1, 3 * D_MODEL))
    mod = mod.reshape(depth, 16, 3, D_MODEL)

    t_ctx = x_prompt.shape[1]

    def flat(a):
        return a.reshape(1, n_ctx * t_ctx, a.shape[-1])

    def unflat(a):
        return a.reshape(n_ctx, t_ctx, a.shape[-1])

    def in_proj_params(l):
        if l % 2 == 0:
            outs_s = ((0, A_COLS, F32), (A_COLS, B_COLS, BF16))
            outs_c = outs_s + ((A_COLS + H_B * DH_B, 2 * H_B * DH_B, F32),)
            a_key = H_A * DK_A
            col = jnp.arange(w_in_even.shape[-1])
            halve = jnp.where((col >= a_key) & (col < 3 * a_key), 0.5, 1.0).astype(F32)
            return (w_in_even[l // 2] * halve).astype(BF16), outs_c, outs_s
        outs = ((0, 2 * W_C, F32),)
        return w_in_odd[l // 2].astype(BF16), outs, outs

    xc, xs = x_prompt, x_sample
    new_hgrn, new_k, new_v, new_rg = [], [], [], []
    proj_c = proj_s = None
    for l in range(depth):
        j = l // 2
        mod_c, mod_s = mod[l, 0:1], mod[l, 1:1 + n_lat]
        if proj_c is None:
            gain = norm_gain[l].reshape(1, D_MODEL)
            w_in, outs_c, outs_s = in_proj_params(l)
            proj_c = [unflat(t) for t in _inproj(flat(xc), mod_c, gain, w_in, outs_c, 512, True)]
            proj_s = _inproj(xs, mod_s, gain, w_in, outs_s, 512, False)
        if l % 2 == 0:
            w_out = w_out_even[j].astype(BF16)
            (ya_c, yb_c, kv_c), (ya_s, yb_s) = proj_c, proj_s
            hgain = hgrn_out_gain[j].reshape(H_A, 1, DK_A)
            oa_c, s_fin = _hgrn(ya_c, hgrn_lb_logits, j, hgain, None)
            oa_s, _ = _hgrn(ya_s, hgrn_lb_logits, j, hgain, state_hgrn[:, j])
            ob_c, k_c, v_c = _ctx_attn(yb_c, kv_c)
            ob_s = _nat(yb_s, cache_na_k[:, j], cache_na_v[:, j], na_rel_bias[j])
            ys_c, ys_s = (oa_c, ob_c), (oa_s, ob_s)
            new_hgrn.append(s_fin)
            new_k.append(k_c)
            new_v.append(v_c)
        else:
            w_out = w_out_odd[j].astype(BF16)
            (xg_c,), (xg_s,) = proj_c, proj_s
            wg = _gate_weights(rg_gate_w[j], rg_gate_b[j])
            cb = conv_b[j].reshape(1, W_C)
            y_c, h_fin = _rglru(xg_c, conv_w[j], cb, wg, rg_lambda[j], None, True)
            y_s, _ = _rglru(xg_s, conv_w[j], cb, wg, rg_lambda[j], state_rglru[:, j], False)
            ys_c, ys_s = (y_c,), (y_s,)
            new_rg.append(h_fin)
        ys_c = tuple(flat(y) for y in ys_c)
        if l == depth - 1:
            fgain = final_gain.reshape(1, D_MODEL)
            (xc,) = _outproj(ys_c, flat(xc), mod_c, w_out, 1024, True, final_gain=fgain)
            (xs,) = _outproj(ys_s, xs, mod_s, w_out, 1024, False, final_gain=fgain)
            xc = unflat(xc)
        else:
            gain_n = norm_gain[l + 1].reshape(1, D_MODEL)
            w_n, outs_c, outs_s = in_proj_params(l + 1)
            mod_cn, mod_sn = mod[l + 1, 0:1], mod[l + 1, 1:1 + n_lat]
            xc, *proj_c = _outproj(ys_c, flat(xc), mod_c, w_out, 512, True,
                                   next_proj=(mod_cn, gain_n, w_n, outs_c))
            xs, *proj_s = _outproj(ys_s, xs, mod_s, w_out, 512, False,
                                   next_proj=(mod_sn, gain_n, w_n, outs_s))
            xc = unflat(xc)
            proj_c = [unflat(t) for t in proj_c]
    return (xc, xs, jnp.stack(new_hgrn, axis=1), jnp.stack(new_k, axis=1),
            jnp.stack(new_v, axis=1), jnp.stack(new_rg, axis=1))
```

```python
import functools

import jax
import jax.numpy as jnp
from jax import lax
from jax.experimental import pallas as pl
from jax.experimental.pallas import tpu as pltpu

F32 = jnp.float32
BF16 = jnp.bfloat16

D_MODEL = 1024
EPS = 1e-6
NEG_INF = -1e30
H_A = 4
DK_A = 128
HGRN_CHUNK = 32
HGRN_ROWS = 256
H_B = 8
DH_B = 64
GRID_W = 64
NA_KH = 8
NA_KW = 16
NA_GROUP = 8
W_C = 1024
H_C = 8
BW_C = W_C // H_C
RG_C = 8.0
RG_ROWS = 256
RG_SLABS = 2
N_SEG = 16
LANES = 128
VMEM_LIMIT = 56 * 1024 * 1024

NT_DIMS = (((1,), (1,)), ((), ()))


def _silu(x):
    half = 0.5 * x
    return half + half * jnp.tanh(half)


def _cparams(n_axes):
    return pltpu.CompilerParams(dimension_semantics=("arbitrary",) * n_axes,
                                vmem_limit_bytes=VMEM_LIMIT)


def _mod_kernel(cond_ref, w_ref, b_ref, o_ref):
    s = _silu(cond_ref[...])
    o_ref[0] = jnp.dot(s.astype(BF16), w_ref[0].astype(BF16), preferred_element_type=F32) + b_ref[0]


def _modulation(cond, w_mod, b_mod):
    depth = w_mod.shape[0]
    n_rows = cond.shape[0]
    return pl.pallas_call(
        _mod_kernel,
        out_shape=jax.ShapeDtypeStruct((depth, n_rows, 3 * D_MODEL), F32),
        grid=(depth, 3),
        in_specs=[
            pl.BlockSpec((n_rows, D_MODEL), lambda l, n: (0, 0)),
            pl.BlockSpec((1, D_MODEL, D_MODEL), lambda l, n: (l, 0, n)),
            pl.BlockSpec((1, 1, D_MODEL), lambda l, n: (l, 0, n)),
        ],
        out_specs=pl.BlockSpec((1, n_rows, D_MODEL), lambda l, n: (l, 0, n)),
        compiler_params=_cparams(2),
        name="adaln_mod",
    )(cond, w_mod, b_mod)


def _project(x, mod_ref, gain_ref, w_ref, out_refs, outs):
    var = jnp.mean(x * x, axis=-1, keepdims=True)
    y = x * lax.rsqrt(var + EPS) * gain_ref[...]
    h = y * (1.0 + mod_ref[0, 1:2, :]) + mod_ref[0, 0:1, :]
    hb = h.astype(BF16)
    step = 512
    for c in range(0, w_ref.shape[1], step):
        users = [(o_ref, c - col0) for o_ref, (col0, width, _) in zip(out_refs, outs)
                 if col0 <= c < col0 + width]
        if users:
            r = jnp.dot(hb, w_ref[:, c:c + step], preferred_element_type=F32)
            for o_ref, off in users:
                o_ref[0, :, off:off + step] = r.astype(o_ref.dtype)


def _inproj_kernel(x_ref, mod_ref, gain_ref, w_ref, *out_refs, outs):
    _project(x_ref[0], mod_ref, gain_ref, w_ref, out_refs, outs)


def _inproj(x, mod, gain, w, outs, tm, shared_mod):
    B, T, _ = x.shape
    n_cols = w.shape[1]
    mod_map = (lambda b, t: (0, 0, 0)) if shared_mod else (lambda b, t: (b, 0, 0))
    return pl.pallas_call(
        functools.partial(_inproj_kernel, outs=outs),
        out_shape=[jax.ShapeDtypeStruct((B, T, wd), dt) for _, wd, dt in outs],
        grid=(B, T // tm),
        in_specs=[
            pl.BlockSpec((1, tm, D_MODEL), lambda b, t: (b, t, 0)),
            pl.BlockSpec((1, 3, D_MODEL), mod_map),
            pl.BlockSpec((1, D_MODEL), lambda b, t: (0, 0)),
            pl.BlockSpec((D_MODEL, n_cols), lambda b, t: (0, 0), pipeline_mode=pl.Buffered(1)),
        ],
        out_specs=[pl.BlockSpec((1, tm, wd), lambda b, t: (b, t, 0)) for _, wd, _ in outs],
        compiler_params=_cparams(2),
        name="in_proj",
    )(x, mod, gain, w)


def _outproj_kernel(*refs, n_y, final, next_outs):
    y_refs, (x_ref, mod_ref, w_ref), rest = refs[:n_y], refs[n_y:n_y + 3], refs[n_y + 3:]
    m = None
    row = 0
    for y_ref in y_refs:
        width = y_ref.shape[-1]
        part = jnp.dot(y_ref[0], w_ref[row:row + width, :], preferred_element_type=F32)
        m = part if m is None else m + part
        row += width
    xn = x_ref[0] + mod_ref[0, 2:3, :] * m
    if final:
        gain_ref, o_ref = rest
        var = jnp.mean(xn * xn, axis=-1, keepdims=True)
        o_ref[0] = xn * lax.rsqrt(var + EPS) * gain_ref[...]
    else:
        modn_ref, gainn_ref, wn_ref, o_ref = rest[:4]
        o_ref[0] = xn
        _project(xn, modn_ref, gainn_ref, wn_ref, rest[4:], next_outs)


def _outproj(ys, x, mod, w, tm, shared_mod, final_gain=None, next_proj=None):
    B, T, _ = x.shape
    final = final_gain is not None
    mod_map = (lambda b, t: (0, 0, 0)) if shared_mod else (lambda b, t: (b, 0, 0))
    row_block = pl.BlockSpec((1, tm, D_MODEL), lambda b, t: (b, t, 0))
    vec = pl.BlockSpec((1, D_MODEL), lambda b, t: (0, 0))
    in_specs = [pl.BlockSpec((1, tm, y.shape[-1]), lambda b, t: (b, t, 0)) for y in ys] + [
        row_block,
        pl.BlockSpec((1, 3, D_MODEL), mod_map),
        pl.BlockSpec((w.shape[0], D_MODEL), lambda b, t: (0, 0), pipeline_mode=pl.Buffered(1)),
    ]
    args = list(ys) + [x, mod, w]
    out_shape = [jax.ShapeDtypeStruct((B, T, D_MODEL), F32)]
    out_specs = [row_block]
    next_outs = None
    if final:
        in_specs.append(vec)
        args.append(final_gain)
    else:
        mod_n, gain_n, w_n, next_outs = next_proj
        in_specs += [pl.BlockSpec((1, 3, D_MODEL), mod_map), vec,
                     pl.BlockSpec((D_MODEL, w_n.shape[1]), lambda b, t: (0, 0), pipeline_mode=pl.Buffered(1))]
        args += [mod_n, gain_n, w_n]
        out_shape += [jax.ShapeDtypeStruct((B, T, wd), dt) for _, wd, dt in next_outs]
        out_specs += [pl.BlockSpec((1, tm, wd), lambda b, t: (b, t, 0)) for _, wd, _ in next_outs]
    return pl.pallas_call(
        functools.partial(_outproj_kernel, n_y=len(ys), final=final, next_outs=next_outs),
        out_shape=out_shape,
        grid=(B, T // tm),
        in_specs=in_specs,
        out_specs=out_specs,
        compiler_params=_cparams(2),
        name="out_proj",
    )(*args)


def _hgrn_kernel(q_ref, zf_ref, zb_ref, v_ref, g_ref, lgt_ref, gain_ref, *rest, seq, layer, per_block):
    rest = list(rest)
    s0_ref = None if per_block else rest.pop(0)
    o_ref = rest.pop(0)
    sfin_ref = rest.pop(0) if per_block else None
    acc_ref, qd_ref, ki_ref, kd_ref, kv_ref, st_ref, dec_ref, mst_ref, msk_ref, mexp_ref = rest
    R = HGRN_ROWS
    C = HGRN_CHUNK
    n_blk = seq // R
    n_chunk = R // C
    n_all = seq // C

    @pl.when((pl.program_id(0) == 0) & (pl.program_id(1) == 0))
    def _build_masks():
        ti = lax.broadcasted_iota(jnp.int32, (R, R), 0)
        tj = lax.broadcasted_iota(jnp.int32, (R, R), 1)
        shift = C.bit_length() - 1
        same = lax.shift_right_logical(ti, shift) == lax.shift_right_logical(tj, shift)
        one = jnp.ones((R, R), F32)
        zero = jnp.zeros((R, R), F32)
        incl = (jnp.where(same, jnp.where(tj <= ti, one, zero), zero),
                jnp.where(same, jnp.where(tj >= ti, one, zero), zero))
        for d in range(2):
            msk_ref[d] = incl[d]
            mst_ref[d] = incl[d].astype(BF16)
        rr = lax.broadcasted_iota(jnp.int32, (R, n_chunk * LANES), 0)
        cc = lax.broadcasted_iota(jnp.int32, (R, n_chunk * LANES), 1)
        own = lax.shift_right_logical(rr, shift) == lax.shift_right_logical(cc, LANES.bit_length() - 1)
        mexp_ref[...] = jnp.where(own, 1.0, 0.0).astype(BF16)

    lgt = [lgt_ref[:, i, :] for i in range(lgt_ref.shape[1])]
    lmax = functools.reduce(jnp.maximum, lgt)
    ex = [jnp.exp(t - lmax) for t in lgt]
    lb_all = sum(ex[:layer + 1]) / sum(ex)
    gain = gain_ref[0]

    blocks_per_trip = 2 if n_blk % 2 == 0 else 1

    def gates(i, _):
        for u in range(blocks_per_trip):
            blk = i * blocks_per_trip + u
            rows = pl.ds(pl.multiple_of(blk * R, R), R)
            q = q_ref[0, rows, :]
            for d in range(2):
                th = jnp.tanh((zf_ref if d == 0 else zb_ref)[0, rows, :])
                lb = lb_all[d:d + 1, :]
                c = 0.5 * (1.0 - lb)
                ct = c * th
                f = (lb + c) + ct
                k = c - ct
                logf = jnp.log(f)
                hi = logf.astype(BF16)
                lo = (logf - hi.astype(F32)).astype(BF16)
                cs = jnp.dot(mst_ref[d], jnp.concatenate([hi, lo], axis=1), preferred_element_type=F32)
                b = cs[:, 0:LANES] + cs[:, LANES:2 * LANES]
                ends = [c * C + (C - 1 if d == 0 else 0) for c in range(n_chunk)]
                btot = jnp.concatenate([jnp.broadcast_to(b[t:t + 1, :], (C, LANES)) for t in ends], axis=0)
                qd_ref[d, rows, :] = (q * jnp.exp(b)).astype(BF16)
                ki_ref[d, rows, :] = (k * jnp.exp(-b)).astype(BF16)
                kd_ref[d, rows, :] = (k * jnp.exp(btot - b)).astype(BF16)
                for c in range(n_chunk):
                    dec_ref[d, blk * n_chunk + c] = jnp.exp(btot[c * C:c * C + 8, :])
        return 0

    lax.fori_loop(0, n_blk // blocks_per_trip, gates, 0)

    def intra(i, _):
        for u in range(blocks_per_trip):
            blk = i * blocks_per_trip + u
            rows = pl.ds(pl.multiple_of(blk * R, R), R)
            v = v_ref[0, rows, :]
            vb = v.astype(BF16)
            vt = v.T.astype(BF16)
            att_sum = None
            for d in range(2):
                att = lax.dot_general(qd_ref[d, rows, :], ki_ref[d, rows, :], NT_DIMS,
                                      preferred_element_type=F32)
                att = jnp.where(msk_ref[d] > 0.5, att, 0.0)
                att_sum = att if att_sum is None else att_sum + att
                kd_exp = jnp.concatenate([kd_ref[d, rows, :]] * n_chunk, axis=1) * mexp_ref[...]
                kv_all = jnp.dot(vt, kd_exp, preferred_element_type=F32)
                for c in range(n_chunk):
                    kv_ref[d, blk * n_chunk + c] = kv_all[:, c * LANES:(c + 1) * LANES]
            acc_ref[rows, :] = jnp.dot(att_sum.astype(BF16), vb, preferred_element_type=F32)
        return 0

    lax.fori_loop(0, n_blk // blocks_per_trip, intra, 0)

    unroll = 4

    def states(i, sts):
        sts = list(sts)
        for u in range(unroll):
            n = i * unroll + u
            for d in range(2):
                c = n if d == 0 else n_all - 1 - n
                st_ref[d, c] = sts[d].astype(BF16)
                dec = jnp.concatenate([dec_ref[d, c]] * (DK_A // 8), axis=0)
                sts[d] = sts[d] * dec + kv_ref[d, c]
        return tuple(sts)

    def block_states(blk, _):
        for d in range(2):
            st = jnp.zeros((DK_A, DK_A), F32)
            for cc in range(n_chunk):
                c = blk * n_chunk + (cc if d == 0 else n_chunk - 1 - cc)
                st_ref[d, c] = st.astype(BF16)
                dec = jnp.concatenate([dec_ref[d, c]] * (DK_A // 8), axis=0)
                st = st * dec + kv_ref[d, c]
            sfin_ref[blk, d, 0] = st.T
        return 0

    if per_block:
        lax.fori_loop(0, n_blk, block_states, 0)
    else:
        st0 = (s0_ref[0, 0, 0].T, s0_ref[0, 1, 0].T)
        lax.fori_loop(0, n_all // unroll, states, st0)

    lane_chunk = lax.shift_right_logical(lax.broadcasted_iota(jnp.int32, (DK_A, R), 1), C.bit_length() - 1)

    finish_blocks = 4 if n_blk % 4 == 0 else blocks_per_trip

    def finish(i, _):
        slabs = []
        for u in range(finish_blocks):
            blk = i * finish_blocks + u
            rows = pl.ds(pl.multiple_of(blk * R, R), R)
            inter_t = None
            for d in range(2):
                sts = st_ref[d, pl.ds(blk * n_chunk, n_chunk)].reshape(n_chunk * DK_A, DK_A)
                res = lax.dot_general(sts, qd_ref[d, rows, :], NT_DIMS, preferred_element_type=F32)
                picked = res[0:DK_A]
                for c in range(1, n_chunk):
                    picked = jnp.where(lane_chunk == c, res[c * DK_A:(c + 1) * DK_A], picked)
                inter_t = picked if inter_t is None else inter_t + picked
            slabs.append((rows, inter_t))
        for rows, inter_t in slabs:
            tot = acc_ref[rows, :] + inter_t.T
            var = jnp.mean(tot * tot, axis=-1, keepdims=True)
            y = tot * lax.rsqrt(var + EPS) * gain
            o_ref[0, rows, :] = (y * _silu(g_ref[0, rows, :])).astype(o_ref.dtype)
        return 0

    lax.fori_loop(0, n_blk // finish_blocks, finish, 0)


def _hgrn(ya, lgt, layer, gain, s0):
    B, T, width = ya.shape
    per_block = s0 is None
    if per_block:
        assert T == HGRN_ROWS
        group = 8 if B % 8 == 0 else 1
        n_seq, B, T = B, B // group, group * T
        ya = ya.reshape(B, T, width)

    def col(k):
        return pl.BlockSpec((1, T, LANES), lambda b, h, k=k: (b, 0, k * H_A + h))

    in_specs = [col(0), col(1), col(2), col(3), col(4),
                pl.BlockSpec((2, lgt.shape[1], LANES), lambda b, h: (0, 0, h)),
                pl.BlockSpec((1, 1, LANES), lambda b, h: (h, 0, 0))]
    args = [ya, ya, ya, ya, ya, lgt, gain]
    out_shape = [jax.ShapeDtypeStruct((B, T, H_A * DK_A), BF16)]
    out_specs = [pl.BlockSpec((1, T, LANES), lambda b, h: (b, 0, h))]
    if per_block:
        out_shape.append(jax.ShapeDtypeStruct((n_seq, 2, H_A, DK_A, DK_A), F32))
        out_specs.append(pl.BlockSpec((T // HGRN_ROWS, 2, 1, DK_A, DK_A), lambda b, h: (b, 0, h, 0, 0)))
    else:
        in_specs.append(pl.BlockSpec((1, 2, 1, DK_A, DK_A), lambda b, h: (b, 0, h, 0, 0)))
        args.append(s0)
    res = pl.pallas_call(
        functools.partial(_hgrn_kernel, seq=T, layer=layer, per_block=per_block),
        out_shape=out_shape,
        grid=(B, H_A),
        in_specs=in_specs,
        out_specs=out_specs,
        scratch_shapes=[pltpu.VMEM((T, LANES), F32),
                        pltpu.VMEM((2, T, LANES), BF16),
                        pltpu.VMEM((2, T, LANES), BF16),
                        pltpu.VMEM((2, T, LANES), BF16),
                        pltpu.VMEM((2, T // HGRN_CHUNK, DK_A, DK_A), F32),
                        pltpu.VMEM((2, T // HGRN_CHUNK, DK_A, DK_A), BF16),
                        pltpu.VMEM((2, T // HGRN_CHUNK, 8, LANES), F32),
                        pltpu.VMEM((2, HGRN_ROWS, HGRN_ROWS), BF16),
                        pltpu.VMEM((2, HGRN_ROWS, HGRN_ROWS), F32),
                        pltpu.VMEM((HGRN_ROWS, HGRN_ROWS // HGRN_CHUNK * LANES), BF16)],
        compiler_params=_cparams(2),
        name="hgrn2",
    )(*args)
    if per_block:
        return res[0].reshape(n_seq, HGRN_ROWS, H_A * DK_A), res[1]
    return res[0], None


def _head_masks():
    lane = lax.broadcasted_iota(jnp.int32, (1, LANES), 1)
    return lane < DH_B, lane >= DH_B


def _ctx_attn_kernel(q_ref, k_ref, v_ref, g_ref, kv_ref, o_ref, newk_ref, newv_ref):
    scale = DH_B ** -0.5
    masks = _head_masks()
    T = q_ref.shape[1]
    for h in range(H_B):
        newk_ref[0, h] = kv_ref[0, :, h * DH_B:(h + 1) * DH_B]
        newv_ref[0, h] = kv_ref[0, :, (H_B + h) * DH_B:(H_B + h + 1) * DH_B]
    for p in range(H_B // 2):
        cols = slice(p * LANES, (p + 1) * LANES)
        q = q_ref[0, :, cols] * scale
        qs = jnp.concatenate([jnp.where(masks[h], q, jnp.zeros_like(q)) for h in range(2)], axis=0)
        s = lax.dot_general(qs, k_ref[0, :, cols], NT_DIMS, preferred_element_type=F32)
        e = jnp.exp(s - jnp.max(s, axis=-1, keepdims=True))
        pr = e / jnp.sum(e, axis=-1, keepdims=True)
        o = jnp.dot(pr.astype(BF16), v_ref[0, :, cols], preferred_element_type=F32)
        o = jnp.where(masks[0], o[0:T], o[T:2 * T])
        o_ref[0, :, cols] = (o * _silu(g_ref[0, :, cols].astype(F32))).astype(o_ref.dtype)


def _ctx_attn(yb, kv):
    B, T, _ = yb.shape
    width = H_B * DH_B

    def col(k):
        return pl.BlockSpec((1, T, width), lambda b, k=k: (b, 0, k))

    cache = pl.BlockSpec((1, H_B, T, DH_B), lambda b: (b, 0, 0, 0))
    return pl.pallas_call(
        _ctx_attn_kernel,
        out_shape=[jax.ShapeDtypeStruct((B, T, width), BF16),
                   jax.ShapeDtypeStruct((B, H_B, T, DH_B), F32),
                   jax.ShapeDtypeStruct((B, H_B, T, DH_B), F32)],
        grid=(B,),
        in_specs=[col(0), col(1), col(2), col(3), pl.BlockSpec((1, T, 2 * width), lambda b: (b, 0, 0))],
        out_specs=[pl.BlockSpec((1, T, width), lambda b: (b, 0, 0)), cache, cache],
        compiler_params=_cparams(1),
        name="ctx_attn",
    )(yb, yb, yb, yb, kv)


N_DR = 2 * NA_KH - 1
N_DC = 2 * NA_KW - 1
N_TAB = N_DR - 1


def _nat_kernel(rb_ref, q_ref, k_ref, v_ref, g_ref, kc_ref, vc_ref, o_ref,
                tab_ref, qs_ref, s_ref, p_ref, r_ref, *, rows):
    scale = DH_B ** -0.5
    kh = min(NA_KH, rows)
    masks = _head_masks()

    @pl.when(pl.program_id(1) == 0)
    def _build_tables():
        c = lax.broadcasted_iota(jnp.int32, (GRID_W, LANES), 0)
        lane = lax.broadcasted_iota(jnp.int32, (GRID_W, LANES), 1)
        kcol = lane & (GRID_W - 1)
        ws = jnp.clip(c - NA_KW // 2, 0, GRID_W - NA_KW)
        neg = jnp.full((GRID_W, LANES), NEG_INF, F32)
        inside = jnp.where(kcol >= ws, jnp.where(kcol < ws + NA_KW, 1.0, 0.0), 0.0) > 0.5
        for h in range(2):
            for i in range(N_TAB):
                row = jnp.broadcast_to(rb_ref[h, i:i + 1, :], (GRID_W, LANES))
                toeplitz = pltpu.roll(row, LANES - (NA_KW - 1), 1, stride=1, stride_axis=0)
                tab_ref[h, i] = jnp.where(inside, toeplitz, neg)

    kc = jnp.concatenate([kc_ref[0, 0], kc_ref[0, 1]], axis=1).astype(BF16)
    vc = jnp.concatenate([vc_ref[0, 0], vc_ref[0, 1]], axis=1).astype(BF16)
    n_keys = kh * GRID_W
    n_ctx = kc.shape[0]
    G = NA_GROUP
    W2 = 2 * GRID_W

    def group(gi, _):
        r_first = gi * G
        q0 = pl.multiple_of(r_first * GRID_W, G * GRID_W)
        for i in range(G):
            qi = q_ref[0, pl.ds(q0 + i * GRID_W, GRID_W), :] * scale
            for h in range(2):
                qs_ref[i * W2 + h * GRID_W:i * W2 + (h + 1) * GRID_W, :] = jnp.where(
                    masks[h], qi, jnp.zeros_like(qi))
        s_ref[:, n_keys:n_keys + n_ctx] = lax.dot_general(qs_ref[...], kc, NT_DIMS,
                                                          preferred_element_type=F32)
        windows = {}

        def local_scores(i):
            r = r_first + i
            rs = jnp.clip(r - kh // 2, 0, rows - kh)
            k0 = pl.multiple_of(rs * GRID_W, GRID_W)
            windows[i] = k0
            dr0 = rs - r + (NA_KH - 1)
            bias = jnp.concatenate(
                [jnp.concatenate([tab_ref[h, dr0 + 2 * m] for m in range(kh // 2)], axis=1)
                 for h in range(2)], axis=0)
            s_ref[i * W2:(i + 1) * W2, 0:n_keys] = lax.dot_general(
                qs_ref[i * W2:(i + 1) * W2, :], k_ref[0, pl.ds(k0, n_keys), :], NT_DIMS,
                preferred_element_type=F32) + bias

        def numerators(i):
            s = s_ref[i * W2:(i + 1) * W2, :]
            e = jnp.exp(s - jnp.max(s, axis=-1, keepdims=True))
            p_ref[i * W2:(i + 1) * W2, :] = e.astype(BF16)
            rinv = 1.0 / jnp.sum(e, axis=-1, keepdims=True)
            r_ref[i * W2:(i + 1) * W2, :] = jnp.broadcast_to(rinv, (W2, LANES))

        def weighted_values(i):
            vals = jnp.concatenate([v_ref[0, pl.ds(windows[i], n_keys), :], vc], axis=0)
            o = jnp.dot(p_ref[i * W2:(i + 1) * W2, :], vals, preferred_element_type=F32)
            o = o * r_ref[i * W2:(i + 1) * W2, :]
            o = jnp.where(masks[0], o[0:GRID_W], o[GRID_W:W2])
            out_rows = pl.ds(q0 + i * GRID_W, GRID_W)
            gate = g_ref[0, out_rows, :].astype(F32)
            o_ref[0, out_rows, :] = (o * _silu(gate)).astype(o_ref.dtype)

        for step in range(G + 2):
            if step < G:
                local_scores(step)
            if 0 <= step - 1 < G:
                numerators(step - 1)
            if 0 <= step - 2 < G:
                weighted_values(step - 2)
        return 0

    lax.fori_loop(0, rows // G, group, 0)


def _nat(yb, kc, vc, rel_bias):
    B, T, _ = yb.shape
    Tc = kc.shape[2]
    n_pair = H_B // 2
    rows = T // GRID_W
    n_stack = NA_GROUP * 2 * GRID_W
    n_keys = min(NA_KH, rows) * GRID_W
    pad = jnp.zeros((H_B, N_TAB, GRID_W - N_DC), F32)
    rel = rel_bias.astype(F32)
    rb_rows = jnp.concatenate([rel[:, 0:N_TAB], pad, rel[:, 1:N_TAB + 1], pad], axis=-1)

    def col(k):
        return pl.BlockSpec((1, T, LANES), lambda p, b, k=k: (b, 0, k * n_pair + p))

    ctx = pl.BlockSpec((1, 2, Tc, DH_B), lambda p, b: (b, p, 0, 0))
    return pl.pallas_call(
        functools.partial(_nat_kernel, rows=rows),
        out_shape=jax.ShapeDtypeStruct((B, T, H_B * DH_B), BF16),
        grid=(n_pair, B),
        in_specs=[pl.BlockSpec((2, N_TAB, LANES), lambda p, b: (p, 0, 0)),
                  col(0), col(1), col(2), col(3), ctx, ctx],
        out_specs=pl.BlockSpec((1, T, LANES), lambda p, b: (b, 0, p)),
        scratch_shapes=[pltpu.VMEM((2, N_TAB, GRID_W, LANES), F32),
                        pltpu.VMEM((n_stack, LANES), BF16),
                        pltpu.VMEM((n_stack, n_keys + Tc), F32),
                        pltpu.VMEM((n_stack, n_keys + Tc), BF16),
                        pltpu.VMEM((n_stack, LANES), F32)],
        compiler_params=_cparams(2),
        name="nbr_attn",
    )(rb_rows, yb, yb, yb, yb, kc, vc)


def _seg_len(seq):
    length = -(-seq // N_SEG)
    while length % 8 != 4:
        length += 1
    return length


def _step_block(seg_len):
    return max(d for d in range(1, seg_len + 1) if seg_len % d == 0 and d * N_SEG <= RG_ROWS * 3 // 2)


def _rglru_kernel(x_ref, g_ref, cw_ref, cb_ref, wg_ref, lam_ref, *rest, seq, slabs, has_s0, emit_state):
    rest = list(rest)
    s0_ref = rest.pop(0) if has_s0 else None
    o_ref = rest.pop(0)
    hfin_ref = rest.pop(0) if emit_state else None
    xpad_ref, a_ref, u_ref, h_ref, p_ref, hn_ref = rest
    L = _seg_len(seq)
    n_rows = N_SEG * L
    RB = RG_ROWS
    CB = slabs
    TB = _step_block(L)
    n_tile = N_SEG // 8
    chains = [(d, j, s) for d in range(2) for j in range(CB) for s in range(n_tile)]

    for j in range(CB):
        xpad_ref[j, 0:8, :] = jnp.zeros((8, LANES), F32)
        xpad_ref[j, 8:seq + 8, :] = x_ref[0, :, j * LANES:(j + 1) * LANES]
        xpad_ref[j, seq + 8:n_rows + 16, :] = jnp.zeros((n_rows + 8 - seq, LANES), F32)

    nl = -lam_ref[...]
    sp = jnp.maximum(nl, 0.0) + jnp.log1p(jnp.exp(-jnp.abs(nl)))
    cw = cw_ref[...]
    cbias = cb_ref[...]
    ones2 = jnp.where(lax.broadcasted_iota(jnp.int32, (TB * N_SEG, LANES), 1) < 2, 1.0, 0.0).astype(BF16)

    def gates(blk, _):
        t0 = blk * TB
        r0 = pl.multiple_of(blk * (TB * N_SEG), TB * N_SEG)
        for j in range(CB):
            lanes = slice(j * LANES, (j + 1) * LANES)
            tiles = []
            for tt in range(TB):
                for s in range(n_tile):
                    taps = [xpad_ref[j, pl.ds(6 + k + t0 + tt + s * 8 * L, 8, stride=L), :] for k in range(4)]
                    xt = cw[0:1, lanes] * taps[0] + cw[1:2, lanes] * taps[1]
                    xt = xt + cw[2:3, lanes] * taps[2]
                    tiles.append(xt + cw[3:4, lanes] * taps[3] + cbias[:, lanes])
            xj = jnp.concatenate(tiles, axis=0)
            gt = jnp.dot(jnp.concatenate([xj.astype(BF16), ones2], axis=1), wg_ref[j],
                         preferred_element_type=F32)
            xh = 0.5 * xj
            for d in range(2):
                th_r = jnp.tanh(gt[:, (2 * d) * LANES:(2 * d + 1) * LANES])
                th_i = jnp.tanh(gt[:, (2 * d + 1) * LANES:(2 * d + 2) * LANES])
                half = (-0.5 * RG_C) * sp[d:d + 1, lanes]
                la = half + half * th_r
                a = jnp.exp(la)
                y = -jnp.tanh(la) * (1.0 + a * a)
                root = jnp.where(y > 0.0, y * lax.rsqrt(y), 0.0)
                a_ref[d, j, pl.ds(r0, TB * N_SEG), :] = a
                u_ref[d, j, pl.ds(r0, TB * N_SEG), :] = root * (xh + xh * th_i)
        return 0

    lax.fori_loop(0, L // TB, gates, 0)

    first_pad = [[min(max(seq - (s * 8 + r) * L, 0), L) for r in range(8)] for s in range(n_tile)]
    sub = lax.broadcasted_iota(jnp.int32, (8, LANES), 0)
    pad_from = []
    for s in range(n_tile):
        if all(f == L for f in first_pad[s]):
            pad_from.append(None)
        else:
            vec = jnp.full((8, LANES), L, jnp.int32)
            for r in range(8):
                vec = jnp.where(sub == r, first_pad[s][r], vec)
            pad_from.append(vec)

    def step_rows(step, s):
        return pl.ds(pl.multiple_of(step * N_SEG + s * 8, 8), 8)

    unroll = 4

    def scan(i, carry):
        carry = list(carry)
        for k in range(unroll):
            for n, (d, j, s) in enumerate(chains):
                h, pr = carry[n]
                t = i * unroll + k
                if d == 1:
                    t = L - 1 - t
                idx = step_rows(t, s)
                a = a_ref[d, j, idx, :]
                u = u_ref[d, j, idx, :]
                if pad_from[s] is not None:
                    live = t < pad_from[s]
                    a = jnp.where(live, a, 1.0)
                    u = jnp.where(live, u, 0.0)
                h = a * h + u
                pr = pr * a
                h_ref[d, j, idx, :] = h
                p_ref[d, j, idx, :] = pr
                carry[n] = (h, pr)
        return tuple(carry)

    zero = jnp.zeros((8, LANES), F32)
    one = jnp.ones((8, LANES), F32)
    ends = lax.fori_loop(0, L // unroll, scan, ((zero, one),) * len(chains))

    cins = {}
    finals = [[None] * CB for _ in range(2)]
    for d in range(2):
        for j in range(CB):
            if has_s0:
                c = s0_ref[0, d:d + 1, j * LANES:(j + 1) * LANES]
            else:
                c = jnp.zeros((1, LANES), F32)
            cin = [None] * N_SEG
            for kk in range(N_SEG):
                seg = kk if d == 0 else N_SEG - 1 - kk
                s, row = divmod(seg, 8)
                h_end, p_end = ends[chains.index((d, j, s))]
                cin[seg] = c
                c = h_end[row:row + 1, :] + p_end[row:row + 1, :] * c
            finals[d][j] = c
            for s in range(n_tile):
                cins[(d, j, s)] = jnp.concatenate(cin[s * 8:(s + 1) * 8], axis=0)

    def fix(i, _):
        for k in range(unroll):
            t = i * unroll + k
            for j in range(CB):
                for s in range(n_tile):
                    idx = step_rows(t, s)
                    parts = [h_ref[d, j, idx, :] + p_ref[d, j, idx, :] * cins[(d, j, s)] for d in range(2)]
                    hn_ref[j, pl.ds(t + s * 8 * L, 8, stride=L), :] = parts[0] + parts[1]
        return 0

    lax.fori_loop(0, L // unroll, fix, 0)

    if emit_state:
        hfin_ref[0] = jnp.concatenate([jnp.concatenate(finals[d], axis=1) for d in range(2)], axis=0)

    def combine(blk, _):
        rows = pl.ds(pl.multiple_of(blk * RB, RB), RB)
        hs = jnp.concatenate([hn_ref[j, rows, :] for j in range(CB)], axis=1)
        o_ref[0, rows, :] = (hs * _silu(g_ref[0, rows, :])).astype(o_ref.dtype)
        return 0

    lax.fori_loop(0, seq // RB, combine, 0)


def _gate_weights(gate_w, gate_b):
    w = (0.5 * gate_w).transpose(2, 3, 0, 1, 4).reshape(H_C, BW_C, 4 * BW_C).astype(BF16)
    b = (0.5 * gate_b).reshape(2, 2, H_C, BW_C).transpose(2, 0, 1, 3).reshape(H_C, 1, 4 * BW_C).astype(F32)
    hi = b.astype(BF16)
    lo = (b - hi.astype(F32)).astype(BF16)
    zeros = jnp.zeros((H_C, BW_C - 2, 4 * BW_C), BF16)
    return jnp.concatenate([w, hi, lo, zeros], axis=1)


def _rglru(xg, conv_w, conv_b, wg, lam, s0, emit_state):
    B, T, _ = xg.shape
    has_s0 = s0 is not None
    n_rows = N_SEG * _seg_len(T)
    CB = RG_SLABS if T > 4 * RG_ROWS else 2 * RG_SLABS
    wide = CB * LANES
    n_steps = H_C // CB
    in_specs = [
        pl.BlockSpec((1, T, wide), lambda b, c: (b, 0, c)),
        pl.BlockSpec((1, T, wide), lambda b, c: (b, 0, n_steps + c)),
        pl.BlockSpec((4, wide), lambda b, c: (0, c)),
        pl.BlockSpec((1, wide), lambda b, c: (0, c)),
        pl.BlockSpec((CB, 2 * BW_C, 4 * BW_C), lambda b, c: (c, 0, 0)),
        pl.BlockSpec((2, wide), lambda b, c: (0, c)),
    ]
    args = [xg, xg, conv_w, conv_b, wg, lam]
    if has_s0:
        in_specs.append(pl.BlockSpec((1, 2, wide), lambda b, c: (b, 0, c)))
        args.append(s0)
    out_shape = [jax.ShapeDtypeStruct((B, T, W_C), BF16)]
    out_specs = [pl.BlockSpec((1, T, wide), lambda b, c: (b, 0, c))]
    if emit_state:
        out_shape.append(jax.ShapeDtypeStruct((B, 2, W_C), F32))
        out_specs.append(pl.BlockSpec((1, 2, wide), lambda b, c: (b, 0, c)))
    res = pl.pallas_call(
        functools.partial(_rglru_kernel, seq=T, slabs=CB, has_s0=has_s0, emit_state=emit_state),
        out_shape=out_shape,
        grid=(B, n_steps),
        in_specs=in_specs,
        out_specs=out_specs,
        scratch_shapes=[pltpu.VMEM((CB, n_rows + 16, LANES), F32)]
        + [pltpu.VMEM((2, CB, n_rows, LANES), F32)] * 4 + [pltpu.VMEM((CB, n_rows, LANES), F32)],
        compiler_params=_cparams(2),
        name="rglru",
    )(*args)
    return res if emit_state else (res[0], None)


A_COLS = 5 * H_A * DK_A
B_COLS = 4 * H_B * DH_B


def kernel(x_prompt, x_sample, state_hgrn, cache_na_k, cache_na_v, state_rglru, c, c_ctx, norm_gain, w_mod, b_mod, w_in_even, w_out_even, hgrn_lb_logits, hgrn_out_gain, na_rel_bias, w_in_odd, w_out_odd, conv_w, conv_b, rg_gate_w, rg_gate_b, rg_lambda, final_gain):
    n_ctx = x_prompt.shape[0]
    n_lat = x_sample.shape[0]
    depth = w_mod.shape[0]

    cond = jnp.zeros((16, D_MODEL), F32).at[0].set(c_ctx).at[1:1 + n_lat].set(c)
    mod = _modulation(cond, w_mod, b_mod.reshape(depth, 1, 3 * D_MODEL))
    mod = mod.reshape(depth, 16, 3, D_MODEL)

    t_ctx = x_prompt.shape[1]

    def flat(a):
        return a.reshape(1, n_ctx * t_ctx, a.shape[-1])

    def unflat(a):
        return a.reshape(n_ctx, t_ctx, a.shape[-1])

    def in_proj_params(l):
        if l % 2 == 0:
            outs_s = ((0, A_COLS, F32), (A_COLS, B_COLS, BF16))
            outs_c = outs_s + ((A_COLS + H_B * DH_B, 2 * H_B * DH_B, F32),)
            a_key = H_A * DK_A
            col = jnp.arange(w_in_even.shape[-1])
            halve = jnp.where((col >= a_key) & (col < 3 * a_key), 0.5, 1.0).astype(F32)
            return (w_in_even[l // 2] * halve).astype(BF16), outs_c, outs_s
        outs = ((0, 2 * W_C, F32),)
        return w_in_odd[l // 2].astype(BF16), outs, outs

    xc, xs = x_prompt, x_sample
    new_hgrn, new_k, new_v, new_rg = [], [], [], []
    proj_c = proj_s = None
    for l in range(depth):
        j = l // 2
        mod_c, mod_s = mod[l, 0:1], mod[l, 1:1 + n_lat]
        if proj_c is None:
            gain = norm_gain[l].reshape(1, D_MODEL)
            w_in, outs_c, outs_s = in_proj_params(l)
            proj_c = [unflat(t) for t in _inproj(flat(xc), mod_c, gain, w_in, outs_c, 512, True)]
            proj_s = _inproj(xs, mod_s, gain, w_in, outs_s, 1024, False)
        if l % 2 == 0:
            w_out = w_out_even[j].astype(BF16)
            (ya_c, yb_c, kv_c), (ya_s, yb_s) = proj_c, proj_s
            hgain = hgrn_out_gain[j].reshape(H_A, 1, DK_A)
            oa_c, s_fin = _hgrn(ya_c, hgrn_lb_logits, j, hgain, None)
            oa_s, _ = _hgrn(ya_s, hgrn_lb_logits, j, hgain, state_hgrn[:, j])
            ob_c, k_c, v_c = _ctx_attn(yb_c, kv_c)
            ob_s = _nat(yb_s, cache_na_k[:, j], cache_na_v[:, j], na_rel_bias[j])
            ys_c, ys_s = (oa_c, ob_c), (oa_s, ob_s)
            new_hgrn.append(s_fin)
            new_k.append(k_c)
            new_v.append(v_c)
        else:
            w_out = w_out_odd[j].astype(BF16)
            (xg_c,), (xg_s,) = proj_c, proj_s
            wg = _gate_weights(rg_gate_w[j], rg_gate_b[j])
            cb = conv_b[j].reshape(1, W_C)
            y_c, h_fin = _rglru(xg_c, conv_w[j], cb, wg, rg_lambda[j], None, True)
            y_s, _ = _rglru(xg_s, conv_w[j], cb, wg, rg_lambda[j], state_rglru[:, j], False)
            ys_c, ys_s = (y_c,), (y_s,)
            new_rg.append(h_fin)
        ys_c = tuple(flat(y) for y in ys_c)
        if l == depth - 1:
            fgain = final_gain.reshape(1, D_MODEL)
            (xc,) = _outproj(ys_c, flat(xc), mod_c, w_out, 1024, True, final_gain=fgain)
            (xs,) = _outproj(ys_s, xs, mod_s, w_out, 1024, False, final_gain=fgain)
            xc = unflat(xc)
        else:
            gain_n = norm_gain[l + 1].reshape(1, D_MODEL)
            w_n, outs_c, outs_s = in_proj_params(l + 1)
            mod_cn, mod_sn = mod[l + 1, 0:1], mod[l + 1, 1:1 + n_lat]
            xc, *proj_c = _outproj(ys_c, flat(xc), mod_c, w_out, 512, True,
                                   next_proj=(mod_cn, gain_n, w_n, outs_c))
            xs, *proj_s = _outproj(ys_s, xs, mod_s, w_out, 1024, False,
                                   next_proj=(mod_sn, gain_n, w_n, outs_s))
            xc = unflat(xc)
            proj_c = [unflat(t) for t in proj_c]
    return (xc, xs, jnp.stack(new_hgrn, axis=1), jnp.stack(new_k, axis=1),
            jnp.stack(new_v, axis=1), jnp.stack(new_rg, axis=1))
```

```python
import functools

import jax
import jax.numpy as jnp
from jax import lax
from jax.experimental import pallas as pl
from jax.experimental.pallas import tpu as pltpu

F32 = jnp.float32
BF16 = jnp.bfloat16

D_MODEL = 1024
EPS = 1e-6
NEG_INF = -1e30
H_A = 4
DK_A = 128
HGRN_CHUNK = 32
HGRN_ROWS = 256
H_B = 8
DH_B = 64
GRID_W = 64
NA_KH = 8
NA_KW = 16
NA_GROUP = 16
W_C = 1024
H_C = 8
BW_C = W_C // H_C
RG_C = 8.0
RG_ROWS = 256
RG_SLABS = 2
N_SEG = 16
LANES = 128
SUBLANES = 8
VMEM_LIMIT = 48 * 1024 * 1024

NT_DIMS = (((1,), (1,)), ((), ()))


def _silu(x):
    half = 0.5 * x
    return half + half * jnp.tanh(half)


def _cparams(n_axes):
    return pltpu.CompilerParams(dimension_semantics=("arbitrary",) * n_axes,
                                vmem_limit_bytes=VMEM_LIMIT)


def _mod_kernel(cond_ref, w_ref, b_ref, o_ref):
    s = _silu(cond_ref[...])
    o_ref[0] = jnp.dot(s.astype(BF16), w_ref[0].astype(BF16), preferred_element_type=F32) + b_ref[0]


def _modulation(cond, w_mod, b_mod):
    depth = w_mod.shape[0]
    n_rows = cond.shape[0]
    return pl.pallas_call(
        _mod_kernel,
        out_shape=jax.ShapeDtypeStruct((depth, n_rows, 3 * D_MODEL), F32),
        grid=(depth, 3),
        in_specs=[
            pl.BlockSpec((n_rows, D_MODEL), lambda l, n: (0, 0)),
            pl.BlockSpec((1, D_MODEL, D_MODEL), lambda l, n: (l, 0, n)),
            pl.BlockSpec((1, 1, D_MODEL), lambda l, n: (l, 0, n)),
        ],
        out_specs=pl.BlockSpec((1, n_rows, D_MODEL), lambda l, n: (l, 0, n)),
        compiler_params=_cparams(2),
        name="adaln_mod",
    )(cond, w_mod, b_mod)


def _project(x, mod_ref, gain_ref, w_ref, out_refs, outs):
    var = jnp.mean(x * x, axis=-1, keepdims=True)
    y = x * lax.rsqrt(var + EPS) * gain_ref[...]
    h = y * (1.0 + mod_ref[0, 1:2, :]) + mod_ref[0, 0:1, :]
    hb = h.astype(BF16)
    step = 512
    for c in range(0, w_ref.shape[1], step):
        users = [(o_ref, c - col0) for o_ref, (col0, width, _) in zip(out_refs, outs)
                 if col0 <= c < col0 + width]
        if users:
            r = jnp.dot(hb, w_ref[:, c:c + step], preferred_element_type=F32)
            for o_ref, off in users:
                o_ref[0, :, off:off + step] = r.astype(o_ref.dtype)


def _inproj_kernel(x_ref, mod_ref, gain_ref, w_ref, *out_refs, outs):
    _project(x_ref[0], mod_ref, gain_ref, w_ref, out_refs, outs)


def _inproj(x, mod, gain, w, outs, tm, shared_mod):
    B, T, _ = x.shape
    n_cols = w.shape[1]
    mod_map = (lambda b, t: (0, 0, 0)) if shared_mod else (lambda b, t: (b, 0, 0))
    return pl.pallas_call(
        functools.partial(_inproj_kernel, outs=outs),
        out_shape=[jax.ShapeDtypeStruct((B, T, wd), dt) for _, wd, dt in outs],
        grid=(B, T // tm),
        in_specs=[
            pl.BlockSpec((1, tm, D_MODEL), lambda b, t: (b, t, 0)),
            pl.BlockSpec((1, 3, D_MODEL), mod_map),
            pl.BlockSpec((1, D_MODEL), lambda b, t: (0, 0)),
            pl.BlockSpec((D_MODEL, n_cols), lambda b, t: (0, 0)),
        ],
        out_specs=[pl.BlockSpec((1, tm, wd), lambda b, t: (b, t, 0)) for _, wd, _ in outs],
        compiler_params=_cparams(2),
        name="in_proj",
    )(x, mod, gain, w)


def _outproj_kernel(*refs, n_y, final, next_outs):
    y_refs, (x_ref, mod_ref, w_ref), rest = refs[:n_y], refs[n_y:n_y + 3], refs[n_y + 3:]
    m = None
    row = 0
    for y_ref in y_refs:
        width = y_ref.shape[-1]
        part = jnp.dot(y_ref[0], w_ref[row:row + width, :], preferred_element_type=F32)
        m = part if m is None else m + part
        row += width
    xn = x_ref[0] + mod_ref[0, 2:3, :] * m
    if final:
        gain_ref, o_ref = rest
        var = jnp.mean(xn * xn, axis=-1, keepdims=True)
        o_ref[0] = xn * lax.rsqrt(var + EPS) * gain_ref[...]
    else:
        modn_ref, gainn_ref, wn_ref, o_ref = rest[:4]
        o_ref[0] = xn
        _project(xn, modn_ref, gainn_ref, wn_ref, rest[4:], next_outs)


def _outproj(ys, x, mod, w, tm, shared_mod, final_gain=None, next_proj=None):
    B, T, _ = x.shape
    final = final_gain is not None
    mod_map = (lambda b, t: (0, 0, 0)) if shared_mod else (lambda b, t: (b, 0, 0))
    row_block = pl.BlockSpec((1, tm, D_MODEL), lambda b, t: (b, t, 0))
    vec = pl.BlockSpec((1, D_MODEL), lambda b, t: (0, 0))
    in_specs = [pl.BlockSpec((1, tm, y.shape[-1]), lambda b, t: (b, t, 0)) for y in ys] + [
        row_block,
        pl.BlockSpec((1, 3, D_MODEL), mod_map),
        pl.BlockSpec((w.shape[0], D_MODEL), lambda b, t: (0, 0)),
    ]
    args = list(ys) + [x, mod, w]
    out_shape = [jax.ShapeDtypeStruct((B, T, D_MODEL), F32)]
    out_specs = [row_block]
    next_outs = None
    if final:
        in_specs.append(vec)
        args.append(final_gain)
    else:
        mod_n, gain_n, w_n, next_outs = next_proj
        in_specs += [pl.BlockSpec((1, 3, D_MODEL), mod_map), vec,
                     pl.BlockSpec((D_MODEL, w_n.shape[1]), lambda b, t: (0, 0))]
        args += [mod_n, gain_n, w_n]
        out_shape += [jax.ShapeDtypeStruct((B, T, wd), dt) for _, wd, dt in next_outs]
        out_specs += [pl.BlockSpec((1, tm, wd), lambda b, t: (b, t, 0)) for _, wd, _ in next_outs]
    return pl.pallas_call(
        functools.partial(_outproj_kernel, n_y=len(ys), final=final, next_outs=next_outs),
        out_shape=out_shape,
        grid=(B, T // tm),
        in_specs=in_specs,
        out_specs=out_specs,
        compiler_params=_cparams(2),
        name="out_proj",
    )(*args)


def _hgrn_kernel(q_ref, zf_ref, zb_ref, v_ref, g_ref, lgt_ref, gain_ref, *rest, seq, layer, per_block):
    rest = list(rest)
    s0_ref = None if per_block else rest.pop(0)
    o_ref = rest.pop(0)
    sfin_ref = rest.pop(0) if per_block else None
    acc_ref, qd_ref, ki_ref, kd_ref, kv_ref, st_ref, dec_ref, mst_ref, msk_ref, mexp_ref = rest
    R = HGRN_ROWS
    C = HGRN_CHUNK
    n_blk = seq // R
    n_chunk = R // C
    n_all = seq // C

    @pl.when((pl.program_id(0) == 0) & (pl.program_id(1) == 0))
    def _build_masks():
        ti = lax.broadcasted_iota(jnp.int32, (R, R), 0)
        tj = lax.broadcasted_iota(jnp.int32, (R, R), 1)
        shift = C.bit_length() - 1
        same = lax.shift_right_logical(ti, shift) == lax.shift_right_logical(tj, shift)
        one = jnp.ones((R, R), F32)
        zero = jnp.zeros((R, R), F32)
        incl = (jnp.where(same, jnp.where(tj <= ti, one, zero), zero),
                jnp.where(same, jnp.where(tj >= ti, one, zero), zero))
        for d in range(2):
            msk_ref[d] = incl[d]
            mst_ref[d] = incl[d].astype(BF16)
        rr = lax.broadcasted_iota(jnp.int32, (R, n_chunk * LANES), 0)
        cc = lax.broadcasted_iota(jnp.int32, (R, n_chunk * LANES), 1)
        own = lax.shift_right_logical(rr, shift) == lax.shift_right_logical(cc, LANES.bit_length() - 1)
        mexp_ref[...] = jnp.where(own, 1.0, 0.0).astype(BF16)

    lgt = [lgt_ref[:, i, :] for i in range(lgt_ref.shape[1])]
    lmax = functools.reduce(jnp.maximum, lgt)
    ex = [jnp.exp(t - lmax) for t in lgt]
    lb_all = sum(ex[:layer + 1]) / sum(ex)
    gain = gain_ref[0]

    blocks_per_trip = 2 if n_blk % 2 == 0 else 1

    def gates(i, _):
        for u in range(blocks_per_trip):
            blk = i * blocks_per_trip + u
            rows = pl.ds(pl.multiple_of(blk * R, R), R)
            q = q_ref[0, rows, :]
            for d in range(2):
                th = jnp.tanh((zf_ref if d == 0 else zb_ref)[0, rows, :])
                lb = lb_all[d:d + 1, :]
                c = 0.5 * (1.0 - lb)
                ct = c * th
                f = (lb + c) + ct
                k = c - ct
                logf = jnp.log(f)
                hi = logf.astype(BF16)
                lo = (logf - hi.astype(F32)).astype(BF16)
                cs = jnp.dot(mst_ref[d], jnp.concatenate([hi, lo], axis=1), preferred_element_type=F32)
                b = cs[:, 0:LANES] + cs[:, LANES:2 * LANES]
                ends = [c * C + (C - 1 if d == 0 else 0) for c in range(n_chunk)]
                btot = jnp.concatenate([jnp.broadcast_to(b[t:t + 1, :], (C, LANES)) for t in ends], axis=0)
                qd_ref[d, rows, :] = (q * jnp.exp(b)).astype(BF16)
                ki_ref[d, rows, :] = (k * jnp.exp(-b)).astype(BF16)
                kd_ref[d, rows, :] = (k * jnp.exp(btot - b)).astype(BF16)
                for c in range(n_chunk):
                    dec_ref[d, blk * n_chunk + c] = jnp.exp(btot[c * C:c * C + SUBLANES, :])
        return 0

    lax.fori_loop(0, n_blk // blocks_per_trip, gates, 0)

    def intra(i, _):
        for u in range(blocks_per_trip):
            blk = i * blocks_per_trip + u
            rows = pl.ds(pl.multiple_of(blk * R, R), R)
            v = v_ref[0, rows, :]
            vb = v.astype(BF16)
            vt = v.T.astype(BF16)
            att_sum = None
            for d in range(2):
                att = lax.dot_general(qd_ref[d, rows, :], ki_ref[d, rows, :], NT_DIMS,
                                      preferred_element_type=F32)
                att = jnp.where(msk_ref[d] > 0.5, att, 0.0)
                att_sum = att if att_sum is None else att_sum + att
                kd_exp = jnp.concatenate([kd_ref[d, rows, :]] * n_chunk, axis=1) * mexp_ref[...]
                kv_all = jnp.dot(vt, kd_exp, preferred_element_type=F32)
                for c in range(n_chunk):
                    kv_ref[d, blk * n_chunk + c] = kv_all[:, c * LANES:(c + 1) * LANES]
            acc_ref[rows, :] = jnp.dot(att_sum.astype(BF16), vb, preferred_element_type=F32)
        return 0

    lax.fori_loop(0, n_blk // blocks_per_trip, intra, 0)

    unroll = 4

    def states(i, sts):
        sts = list(sts)
        for u in range(unroll):
            n = i * unroll + u
            for d in range(2):
                c = n if d == 0 else n_all - 1 - n
                st_ref[d, c] = sts[d].astype(BF16)
                dec = jnp.concatenate([dec_ref[d, c]] * (DK_A // SUBLANES), axis=0)
                sts[d] = sts[d] * dec + kv_ref[d, c]
        return tuple(sts)

    def block_states(blk, _):
        for d in range(2):
            st = jnp.zeros((DK_A, DK_A), F32)
            for cc in range(n_chunk):
                c = blk * n_chunk + (cc if d == 0 else n_chunk - 1 - cc)
                st_ref[d, c] = st.astype(BF16)
                dec = jnp.concatenate([dec_ref[d, c]] * (DK_A // SUBLANES), axis=0)
                st = st * dec + kv_ref[d, c]
            sfin_ref[blk, d, 0] = st.T
        return 0

    if per_block:
        lax.fori_loop(0, n_blk, block_states, 0)
    else:
        st0 = (s0_ref[0, 0, 0].T, s0_ref[0, 1, 0].T)
        lax.fori_loop(0, n_all // unroll, states, st0)

    lane_chunk = lax.shift_right_logical(lax.broadcasted_iota(jnp.int32, (DK_A, R), 1), C.bit_length() - 1)

    finish_blocks = 4 if n_blk % 4 == 0 else blocks_per_trip

    def finish(i, _):
        slabs = []
        for u in range(finish_blocks):
            blk = i * finish_blocks + u
            rows = pl.ds(pl.multiple_of(blk * R, R), R)
            inter_t = None
            for d in range(2):
                sts = st_ref[d, pl.ds(blk * n_chunk, n_chunk)].reshape(n_chunk * DK_A, DK_A)
                res = lax.dot_general(sts, qd_ref[d, rows, :], NT_DIMS, preferred_element_type=F32)
                picked = res[0:DK_A]
                for c in range(1, n_chunk):
                    picked = jnp.where(lane_chunk == c, res[c * DK_A:(c + 1) * DK_A], picked)
                inter_t = picked if inter_t is None else inter_t + picked
            slabs.append((rows, inter_t))
        for rows, inter_t in slabs:
            tot = acc_ref[rows, :] + inter_t.T
            var = jnp.mean(tot * tot, axis=-1, keepdims=True)
            y = tot * lax.rsqrt(var + EPS) * gain
            o_ref[0, rows, :] = (y * _silu(g_ref[0, rows, :])).astype(o_ref.dtype)
        return 0

    lax.fori_loop(0, n_blk // finish_blocks, finish, 0)


def _hgrn(ya, lgt, layer, gain, s0):
    B, T, width = ya.shape
    per_block = s0 is None
    if per_block:
        assert T == HGRN_ROWS
        group = 8 if B % 8 == 0 else 1
        n_seq, B, T = B, B // group, group * T
        ya = ya.reshape(B, T, width)

    def col(k):
        return pl.BlockSpec((1, T, LANES), lambda b, h, k=k: (b, 0, k * H_A + h))

    in_specs = [col(0), col(1), col(2), col(3), col(4),
                pl.BlockSpec((2, lgt.shape[1], LANES), lambda b, h: (0, 0, h)),
                pl.BlockSpec((1, 1, LANES), lambda b, h: (h, 0, 0))]
    args = [ya, ya, ya, ya, ya, lgt, gain]
    out_shape = [jax.ShapeDtypeStruct((B, T, H_A * DK_A), BF16)]
    out_specs = [pl.BlockSpec((1, T, LANES), lambda b, h: (b, 0, h))]
    if per_block:
        out_shape.append(jax.ShapeDtypeStruct((n_seq, 2, H_A, DK_A, DK_A), F32))
        out_specs.append(pl.BlockSpec((T // HGRN_ROWS, 2, 1, DK_A, DK_A), lambda b, h: (b, 0, h, 0, 0)))
    else:
        in_specs.append(pl.BlockSpec((1, 2, 1, DK_A, DK_A), lambda b, h: (b, 0, h, 0, 0)))
        args.append(s0)
    res = pl.pallas_call(
        functools.partial(_hgrn_kernel, seq=T, layer=layer, per_block=per_block),
        out_shape=out_shape,
        grid=(B, H_A),
        in_specs=in_specs,
        out_specs=out_specs,
        scratch_shapes=[pltpu.VMEM((T, LANES), F32),
                        pltpu.VMEM((2, T, LANES), BF16),
                        pltpu.VMEM((2, T, LANES), BF16),
                        pltpu.VMEM((2, T, LANES), BF16),
                        pltpu.VMEM((2, T // HGRN_CHUNK, DK_A, DK_A), F32),
                        pltpu.VMEM((2, T // HGRN_CHUNK, DK_A, DK_A), BF16),
                        pltpu.VMEM((2, T // HGRN_CHUNK, SUBLANES, LANES), F32),
                        pltpu.VMEM((2, HGRN_ROWS, HGRN_ROWS), BF16),
                        pltpu.VMEM((2, HGRN_ROWS, HGRN_ROWS), F32),
                        pltpu.VMEM((HGRN_ROWS, HGRN_ROWS // HGRN_CHUNK * LANES), BF16)],
        compiler_params=_cparams(2),
        name="hgrn2",
    )(*args)
    if per_block:
        return res[0].reshape(n_seq, HGRN_ROWS, H_A * DK_A), res[1]
    return res[0], None


def _head_masks():
    lane = lax.broadcasted_iota(jnp.int32, (1, LANES), 1)
    return lane < DH_B, lane >= DH_B


def _ctx_attn_kernel(q_ref, k_ref, v_ref, g_ref, kv_ref, o_ref, newk_ref, newv_ref):
    scale = DH_B ** -0.5
    masks = _head_masks()
    T = q_ref.shape[1]
    for h in range(H_B):
        newk_ref[0, h] = kv_ref[0, :, h * DH_B:(h + 1) * DH_B]
        newv_ref[0, h] = kv_ref[0, :, (H_B + h) * DH_B:(H_B + h + 1) * DH_B]
    for p in range(H_B // 2):
        cols = slice(p * LANES, (p + 1) * LANES)
        q = q_ref[0, :, cols] * scale
        qs = jnp.concatenate([jnp.where(masks[h], q, jnp.zeros_like(q)) for h in range(2)], axis=0)
        s = lax.dot_general(qs, k_ref[0, :, cols], NT_DIMS, preferred_element_type=F32)
        e = jnp.exp(s - jnp.max(s, axis=-1, keepdims=True))
        pr = e / jnp.sum(e, axis=-1, keepdims=True)
        o = jnp.dot(pr.astype(BF16), v_ref[0, :, cols], preferred_element_type=F32)
        o = jnp.where(masks[0], o[0:T], o[T:2 * T])
        o_ref[0, :, cols] = (o * _silu(g_ref[0, :, cols].astype(F32))).astype(o_ref.dtype)


def _ctx_attn(yb, kv):
    B, T, _ = yb.shape
    width = H_B * DH_B

    def col(k):
        return pl.BlockSpec((1, T, width), lambda b, k=k: (b, 0, k))

    cache = pl.BlockSpec((1, H_B, T, DH_B), lambda b: (b, 0, 0, 0))
    return pl.pallas_call(
        _ctx_attn_kernel,
        out_shape=[jax.ShapeDtypeStruct((B, T, width), BF16),
                   jax.ShapeDtypeStruct((B, H_B, T, DH_B), F32),
                   jax.ShapeDtypeStruct((B, H_B, T, DH_B), F32)],
        grid=(B,),
        in_specs=[col(0), col(1), col(2), col(3), pl.BlockSpec((1, T, 2 * width), lambda b: (b, 0, 0))],
        out_specs=[pl.BlockSpec((1, T, width), lambda b: (b, 0, 0)), cache, cache],
        compiler_params=_cparams(1),
        name="ctx_attn",
    )(yb, yb, yb, yb, kv)


N_DR = 2 * NA_KH - 1
N_DC = 2 * NA_KW - 1
N_TAB = N_DR - 1


def _nat_kernel(rb_ref, q_ref, k_ref, v_ref, g_ref, kc_ref, vc_ref, o_ref,
                tab_ref, qs_ref, s_ref, p_ref, r_ref, *, rows):
    scale = DH_B ** -0.5
    kh = min(NA_KH, rows)
    masks = _head_masks()

    @pl.when(pl.program_id(1) == 0)
    def _build_tables():
        c = lax.broadcasted_iota(jnp.int32, (GRID_W, LANES), 0)
        lane = lax.broadcasted_iota(jnp.int32, (GRID_W, LANES), 1)
        kcol = lane & (GRID_W - 1)
        ws = jnp.clip(c - NA_KW // 2, 0, GRID_W - NA_KW)
        neg = jnp.full((GRID_W, LANES), NEG_INF, F32)
        inside = jnp.where(kcol >= ws, jnp.where(kcol < ws + NA_KW, 1.0, 0.0), 0.0) > 0.5
        for h in range(2):
            for i in range(N_TAB):
                row = jnp.broadcast_to(rb_ref[h, i:i + 1, :], (GRID_W, LANES))
                toeplitz = pltpu.roll(row, LANES - (NA_KW - 1), 1, stride=1, stride_axis=0)
                tab_ref[h, i] = jnp.where(inside, toeplitz, neg)

    kc = jnp.concatenate([kc_ref[0, 0], kc_ref[0, 1]], axis=1).astype(BF16)
    vc = jnp.concatenate([vc_ref[0, 0], vc_ref[0, 1]], axis=1).astype(BF16)
    n_keys = kh * GRID_W
    n_ctx = kc.shape[0]
    G = NA_GROUP
    W2 = 2 * GRID_W

    def group(gi, _):
        r_first = gi * G
        q0 = pl.multiple_of(r_first * GRID_W, G * GRID_W)
        for i in range(G):
            qi = q_ref[0, pl.ds(q0 + i * GRID_W, GRID_W), :] * scale
            for h in range(2):
                qs_ref[i * W2 + h * GRID_W:i * W2 + (h + 1) * GRID_W, :] = jnp.where(
                    masks[h], qi, jnp.zeros_like(qi))
        s_ref[:, n_keys:n_keys + n_ctx] = lax.dot_general(qs_ref[...], kc, NT_DIMS,
                                                          preferred_element_type=F32)
        windows = {}

        def local_scores(i):
            r = r_first + i
            rs = jnp.clip(r - kh // 2, 0, rows - kh)
            k0 = pl.multiple_of(rs * GRID_W, GRID_W)
            windows[i] = k0
            dr0 = rs - r + (NA_KH - 1)
            bias = jnp.concatenate(
                [jnp.concatenate([tab_ref[h, dr0 + 2 * m] for m in range(kh // 2)], axis=1)
                 for h in range(2)], axis=0)
            s_ref[i * W2:(i + 1) * W2, 0:n_keys] = lax.dot_general(
                qs_ref[i * W2:(i + 1) * W2, :], k_ref[0, pl.ds(k0, n_keys), :], NT_DIMS,
                preferred_element_type=F32) + bias

        def numerators(i):
            s = s_ref[i * W2:(i + 1) * W2, :]
            e = jnp.exp(s - jnp.max(s, axis=-1, keepdims=True))
            p_ref[i * W2:(i + 1) * W2, :] = e.astype(BF16)
            rinv = 1.0 / jnp.sum(e, axis=-1, keepdims=True)
            r_ref[i * W2:(i + 1) * W2, :] = jnp.broadcast_to(rinv, (W2, LANES))

        def weighted_values(i):
            vals = jnp.concatenate([v_ref[0, pl.ds(windows[i], n_keys), :], vc], axis=0)
            o = jnp.dot(p_ref[i * W2:(i + 1) * W2, :], vals, preferred_element_type=F32)
            o = o * r_ref[i * W2:(i + 1) * W2, :]
            o = jnp.where(masks[0], o[0:GRID_W], o[GRID_W:W2])
            out_rows = pl.ds(q0 + i * GRID_W, GRID_W)
            gate = g_ref[0, out_rows, :].astype(F32)
            o_ref[0, out_rows, :] = (o * _silu(gate)).astype(o_ref.dtype)

        for step in range(G + 2):
            if step < G:
                local_scores(step)
            if 0 <= step - 1 < G:
                numerators(step - 1)
            if 0 <= step - 2 < G:
                weighted_values(step - 2)
        return 0

    lax.fori_loop(0, rows // G, group, 0)


def _nat(yb, kc, vc, rel_bias):
    B, T, _ = yb.shape
    Tc = kc.shape[2]
    n_pair = H_B // 2
    rows = T // GRID_W
    n_stack = NA_GROUP * 2 * GRID_W
    n_keys = min(NA_KH, rows) * GRID_W
    pad = jnp.zeros((H_B, N_TAB, GRID_W - N_DC), F32)
    rel = rel_bias.astype(F32)
    rb_rows = jnp.concatenate([rel[:, 0:N_TAB], pad, rel[:, 1:N_TAB + 1], pad], axis=-1)

    def col(k):
        return pl.BlockSpec((1, T, LANES), lambda p, b, k=k: (b, 0, k * n_pair + p))

    ctx = pl.BlockSpec((1, 2, Tc, DH_B), lambda p, b: (b, p, 0, 0))
    return pl.pallas_call(
        functools.partial(_nat_kernel, rows=rows),
        out_shape=jax.ShapeDtypeStruct((B, T, H_B * DH_B), BF16),
        grid=(n_pair, B),
        in_specs=[pl.BlockSpec((2, N_TAB, LANES), lambda p, b: (p, 0, 0)),
                  col(0), col(1), col(2), col(3), ctx, ctx],
        out_specs=pl.BlockSpec((1, T, LANES), lambda p, b: (b, 0, p)),
        scratch_shapes=[pltpu.VMEM((2, N_TAB, GRID_W, LANES), F32),
                        pltpu.VMEM((n_stack, LANES), BF16),
                        pltpu.VMEM((n_stack, n_keys + Tc), F32),
                        pltpu.VMEM((n_stack, n_keys + Tc), BF16),
                        pltpu.VMEM((n_stack, LANES), F32)],
        compiler_params=_cparams(2),
        name="nbr_attn",
    )(rb_rows, yb, yb, yb, yb, kc, vc)


def _seg_len(seq):
    length = -(-seq // N_SEG)
    while length % 8 != 4:
        length += 1
    return length


def _step_block(seg_len):
    return max(d for d in range(1, seg_len + 1) if seg_len % d == 0 and d * N_SEG <= RG_ROWS * 3 // 2)


def _rglru_kernel(x_ref, g_ref, cw_ref, cb_ref, wg_ref, lam_ref, *rest, seq, slabs, has_s0, emit_state):
    rest = list(rest)
    s0_ref = rest.pop(0) if has_s0 else None
    o_ref = rest.pop(0)
    hfin_ref = rest.pop(0) if emit_state else None
    xpad_ref, a_ref, u_ref, h_ref, p_ref, hn_ref = rest
    L = _seg_len(seq)
    n_rows = N_SEG * L
    RB = RG_ROWS
    CB = slabs
    TB = _step_block(L)
    n_tile = N_SEG // SUBLANES
    lead = SUBLANES
    chains = [(d, j, s) for d in range(2) for j in range(CB) for s in range(n_tile)]

    for j in range(CB):
        xpad_ref[j, 0:lead, :] = jnp.zeros((lead, LANES), F32)
        xpad_ref[j, lead:seq + lead, :] = x_ref[0, :, j * LANES:(j + 1) * LANES]
        xpad_ref[j, seq + lead:n_rows + 2 * lead, :] = jnp.zeros((n_rows + lead - seq, LANES), F32)

    nl = -lam_ref[...]
    sp = jnp.maximum(nl, 0.0) + jnp.log1p(jnp.exp(-jnp.abs(nl)))
    cw = cw_ref[...]
    cbias = cb_ref[...]
    ones2 = jnp.where(lax.broadcasted_iota(jnp.int32, (TB * N_SEG, LANES), 1) < 2, 1.0, 0.0).astype(BF16)

    def gates(blk, _):
        t0 = blk * TB
        r0 = pl.multiple_of(blk * (TB * N_SEG), TB * N_SEG)
        for j in range(CB):
            lanes = slice(j * LANES, (j + 1) * LANES)
            tiles = []
            for tt in range(TB):
                for s in range(n_tile):
                    first = lead - 2 + t0 + tt + s * SUBLANES * L
                    taps = [xpad_ref[j, pl.ds(first + k, SUBLANES, stride=L), :] for k in range(4)]
                    xt = cw[0:1, lanes] * taps[0] + cw[1:2, lanes] * taps[1]
                    xt = xt + cw[2:3, lanes] * taps[2]
                    tiles.append(xt + cw[3:4, lanes] * taps[3] + cbias[:, lanes])
            xj = jnp.concatenate(tiles, axis=0)
            gt = jnp.dot(jnp.concatenate([xj.astype(BF16), ones2], axis=1), wg_ref[j],
                         preferred_element_type=F32)
            xh = 0.5 * xj
            for d in range(2):
                th_r = jnp.tanh(gt[:, (2 * d) * LANES:(2 * d + 1) * LANES])
                th_i = jnp.tanh(gt[:, (2 * d + 1) * LANES:(2 * d + 2) * LANES])
                half = (-0.5 * RG_C) * sp[d:d + 1, lanes]
                la = half + half * th_r
                a = jnp.exp(la)
                y = -jnp.tanh(la) * (1.0 + a * a)
                root = jnp.where(y > 0.0, y * lax.rsqrt(y), 0.0)
                a_ref[d, j, pl.ds(r0, TB * N_SEG), :] = a
                u_ref[d, j, pl.ds(r0, TB * N_SEG), :] = root * (xh + xh * th_i)
        return 0

    lax.fori_loop(0, L // TB, gates, 0)

    first_pad = [[min(max(seq - (s * SUBLANES + r) * L, 0), L) for r in range(SUBLANES)]
                 for s in range(n_tile)]
    sub = lax.broadcasted_iota(jnp.int32, (SUBLANES, LANES), 0)
    pad_from = []
    for s in range(n_tile):
        if all(f == L for f in first_pad[s]):
            pad_from.append(None)
        else:
            vec = jnp.full((SUBLANES, LANES), L, jnp.int32)
            for r in range(SUBLANES):
                vec = jnp.where(sub == r, first_pad[s][r], vec)
            pad_from.append(vec)

    def step_rows(step, s):
        return pl.ds(pl.multiple_of(step * N_SEG + s * SUBLANES, SUBLANES), SUBLANES)

    unroll = 4

    def scan(i, carry):
        carry = list(carry)
        for k in range(unroll):
            for n, (d, j, s) in enumerate(chains):
                h, pr = carry[n]
                t = i * unroll + k
                if d == 1:
                    t = L - 1 - t
                idx = step_rows(t, s)
                a = a_ref[d, j, idx, :]
                u = u_ref[d, j, idx, :]
                if pad_from[s] is not None:
                    live = t < pad_from[s]
                    a = jnp.where(live, a, 1.0)
                    u = jnp.where(live, u, 0.0)
                h = a * h + u
                pr = pr * a
                h_ref[d, j, idx, :] = h
                p_ref[d, j, idx, :] = pr
                carry[n] = (h, pr)
        return tuple(carry)

    zero = jnp.zeros((SUBLANES, LANES), F32)
    one = jnp.ones((SUBLANES, LANES), F32)
    ends = lax.fori_loop(0, L // unroll, scan, ((zero, one),) * len(chains))

    cins = {}
    finals = [[None] * CB for _ in range(2)]
    for d in range(2):
        for j in range(CB):
            if has_s0:
                c = s0_ref[0, d:d + 1, j * LANES:(j + 1) * LANES]
            else:
                c = jnp.zeros((1, LANES), F32)
            cin = [None] * N_SEG
            for kk in range(N_SEG):
                seg = kk if d == 0 else N_SEG - 1 - kk
                s, row = divmod(seg, SUBLANES)
                h_end, p_end = ends[chains.index((d, j, s))]
                cin[seg] = c
                c = h_end[row:row + 1, :] + p_end[row:row + 1, :] * c
            finals[d][j] = c
            for s in range(n_tile):
                cins[(d, j, s)] = jnp.concatenate(cin[s * SUBLANES:(s + 1) * SUBLANES], axis=0)

    def fix(i, _):
        for k in range(unroll):
            t = i * unroll + k
            for j in range(CB):
                for s in range(n_tile):
                    idx = step_rows(t, s)
                    parts = [h_ref[d, j, idx, :] + p_ref[d, j, idx, :] * cins[(d, j, s)] for d in range(2)]
                    hn_ref[j, pl.ds(t + s * SUBLANES * L, SUBLANES, stride=L), :] = parts[0] + parts[1]
        return 0

    lax.fori_loop(0, L // unroll, fix, 0)

    if emit_state:
        hfin_ref[0] = jnp.concatenate([jnp.concatenate(finals[d], axis=1) for d in range(2)], axis=0)

    def combine(blk, _):
        rows = pl.ds(pl.multiple_of(blk * RB, RB), RB)
        hs = jnp.concatenate([hn_ref[j, rows, :] for j in range(CB)], axis=1)
        o_ref[0, rows, :] = (hs * _silu(g_ref[0, rows, :])).astype(o_ref.dtype)
        return 0

    lax.fori_loop(0, seq // RB, combine, 0)


def _gate_weights(gate_w, gate_b):
    w = (0.5 * gate_w).transpose(2, 3, 0, 1, 4).reshape(H_C, BW_C, 4 * BW_C).astype(BF16)
    b = (0.5 * gate_b).reshape(2, 2, H_C, BW_C).transpose(2, 0, 1, 3).reshape(H_C, 1, 4 * BW_C).astype(F32)
    hi = b.astype(BF16)
    lo = (b - hi.astype(F32)).astype(BF16)
    zeros = jnp.zeros((H_C, BW_C - 2, 4 * BW_C), BF16)
    return jnp.concatenate([w, hi, lo, zeros], axis=1)


def _rglru(xg, conv_w, conv_b, wg, lam, s0, emit_state):
    B, T, _ = xg.shape
    has_s0 = s0 is not None
    n_rows = N_SEG * _seg_len(T)
    CB = RG_SLABS if T > 4 * RG_ROWS else 2 * RG_SLABS
    wide = CB * LANES
    n_steps = H_C // CB
    in_specs = [
        pl.BlockSpec((1, T, wide), lambda b, c: (b, 0, c)),
        pl.BlockSpec((1, T, wide), lambda b, c: (b, 0, n_steps + c)),
        pl.BlockSpec((4, wide), lambda b, c: (0, c)),
        pl.BlockSpec((1, wide), lambda b, c: (0, c)),
        pl.BlockSpec((CB, 2 * BW_C, 4 * BW_C), lambda b, c: (c, 0, 0)),
        pl.BlockSpec((2, wide), lambda b, c: (0, c)),
    ]
    args = [xg, xg, conv_w, conv_b, wg, lam]
    if has_s0:
        in_specs.append(pl.BlockSpec((1, 2, wide), lambda b, c: (b, 0, c)))
        args.append(s0)
    out_shape = [jax.ShapeDtypeStruct((B, T, W_C), BF16)]
    out_specs = [pl.BlockSpec((1, T, wide), lambda b, c: (b, 0, c))]
    if emit_state:
        out_shape.append(jax.ShapeDtypeStruct((B, 2, W_C), F32))
        out_specs.append(pl.BlockSpec((1, 2, wide), lambda b, c: (b, 0, c)))
    res = pl.pallas_call(
        functools.partial(_rglru_kernel, seq=T, slabs=CB, has_s0=has_s0, emit_state=emit_state),
        out_shape=out_shape,
        grid=(B, n_steps),
        in_specs=in_specs,
        out_specs=out_specs,
        scratch_shapes=[pltpu.VMEM((CB, n_rows + 2 * SUBLANES, LANES), F32)]
        + [pltpu.VMEM((2, CB, n_rows, LANES), F32)] * 4 + [pltpu.VMEM((CB, n_rows, LANES), F32)],
        compiler_params=_cparams(2),
        name="rglru",
    )(*args)
    return res if emit_state else (res[0], None)


A_COLS = 5 * H_A * DK_A
B_COLS = 4 * H_B * DH_B


def kernel(x_prompt, x_sample, state_hgrn, cache_na_k, cache_na_v, state_rglru, c, c_ctx, norm_gain, w_mod, b_mod, w_in_even, w_out_even, hgrn_lb_logits, hgrn_out_gain, na_rel_bias, w_in_odd, w_out_odd, conv_w, conv_b, rg_gate_w, rg_gate_b, rg_lambda, final_gain):
    n_ctx = x_prompt.shape[0]
    n_lat = x_sample.shape[0]
    depth = w_mod.shape[0]

    cond = jnp.zeros((16, D_MODEL), F32).at[0].set(c_ctx).at[1:1 + n_lat].set(c)
    mod = _modulation(cond, w_mod, b_mod.reshape(depth, 1, 3 * D_MODEL))
    mod = mod.reshape(depth, 16, 3, D_MODEL)

    t_ctx = x_prompt.shape[1]

    def flat(a):
        return a.reshape(1, n_ctx * t_ctx, a.shape[-1])

    def unflat(a):
        return a.reshape(n_ctx, t_ctx, a.shape[-1])

    def in_proj_params(l):
        if l % 2 == 0:
            outs_s = ((0, A_COLS, F32), (A_COLS, B_COLS, BF16))
            outs_c = outs_s + ((A_COLS + H_B * DH_B, 2 * H_B * DH_B, F32),)
            a_key = H_A * DK_A
            col = jnp.arange(w_in_even.shape[-1])
            halve = jnp.where((col >= a_key) & (col < 3 * a_key), 0.5, 1.0).astype(F32)
            return (w_in_even[l // 2] * halve).astype(BF16), outs_c, outs_s
        outs = ((0, 2 * W_C, F32),)
        return w_in_odd[l // 2].astype(BF16), outs, outs

    xc, xs = x_prompt, x_sample
    new_hgrn, new_k, new_v, new_rg = [], [], [], []
    proj_c = proj_s = None
    for l in range(depth):
        j = l // 2
        mod_c, mod_s = mod[l, 0:1], mod[l, 1:1 + n_lat]
        if proj_c is None:
            gain = norm_gain[l].reshape(1, D_MODEL)
            w_in, outs_c, outs_s = in_proj_params(l)
            proj_c = [unflat(t) for t in _inproj(flat(xc), mod_c, gain, w_in, outs_c, 512, True)]
            proj_s = _inproj(xs, mod_s, gain, w_in, outs_s, 512, False)
        if l % 2 == 0:
            w_out = w_out_even[j].astype(BF16)
            (ya_c, yb_c, kv_c), (ya_s, yb_s) = proj_c, proj_s
            hgain = hgrn_out_gain[j].reshape(H_A, 1, DK_A)
            oa_c, s_fin = _hgrn(ya_c, hgrn_lb_logits, j, hgain, None)
            oa_s, _ = _hgrn(ya_s, hgrn_lb_logits, j, hgain, state_hgrn[:, j])
            ob_c, k_c, v_c = _ctx_attn(yb_c, kv_c)
            ob_s = _nat(yb_s, cache_na_k[:, j], cache_na_v[:, j], na_rel_bias[j])
            ys_c, ys_s = (oa_c, ob_c), (oa_s, ob_s)
            new_hgrn.append(s_fin)
            new_k.append(k_c)
            new_v.append(v_c)
        else:
            w_out = w_out_odd[j].astype(BF16)
            (xg_c,), (xg_s,) = proj_c, proj_s
            wg = _gate_weights(rg_gate_w[j], rg_gate_b[j])
            cb = conv_b[j].reshape(1, W_C)
            y_c, h_fin = _rglru(xg_c, conv_w[j], cb, wg, rg_lambda[j], None, True)
            y_s, _ = _rglru(xg_s, conv_w[j], cb, wg, rg_lambda[j], state_rglru[:, j], False)
            ys_c, ys_s = (y_c,), (y_s,)
            new_rg.append(h_fin)
        ys_c = tuple(flat(y) for y in ys_c)
        if l == depth - 1:
            fgain = final_gain.reshape(1, D_MODEL)
            (xc,) = _outproj(ys_c, flat(xc), mod_c, w_out, 1024, True, final_gain=fgain)
            (xs,) = _outproj(ys_s, xs, mod_s, w_out, 1024, False, final_gain=fgain)
            xc = unflat(xc)
        else:
            gain_n = norm_gain[l + 1].reshape(1, D_MODEL)
            w_n, outs_c, outs_s = in_proj_params(l + 1)
            mod_cn, mod_sn = mod[l + 1, 0:1], mod[l + 1, 1:1 + n_lat]
            xc, *proj_c = _outproj(ys_c, flat(xc), mod_c, w_out, 512, True,
                                   next_proj=(mod_cn, gain_n, w_n, outs_c))
            xs, *proj_s = _outproj(ys_s, xs, mod_s, w_out, 512, False,
                                   next_proj=(mod_sn, gain_n, w_n, outs_s))
            xc = unflat(xc)
            proj_c = [unflat(t) for t in proj_c]
    return (xc, xs, jnp.stack(new_hgrn, axis=1), jnp.stack(new_k, axis=1),
            jnp.stack(new_v, axis=1), jnp.stack(new_rg, axis=1))
```

```python
import functools

import jax
import jax.numpy as jnp
from jax import lax
from jax.experimental import pallas as pl
from jax.experimental.pallas import tpu as pltpu

F32 = jnp.float32
BF16 = jnp.bfloat16

D_MODEL = 1024
EPS = 1e-6
NEG_INF = -1e30
H_A = 4
DK_A = 128
HGRN_CHUNK = 32
HGRN_ROWS = 256
H_B = 8
DH_B = 64
GRID_W = 64
NA_KH = 8
NA_KW = 16
NA_GROUP = 16
W_C = 1024
H_C = 8
BW_C = W_C // H_C
RG_C = 8.0
RG_ROWS = 256
RG_SLABS = 2
N_SEG = 16
LANES = 128
SUBLANES = 8
VMEM_LIMIT = 48 * 1024 * 1024

NT_DIMS = (((1,), (1,)), ((), ()))


def _silu(x):
    half = 0.5 * x
    return half + half * jnp.tanh(half)


def _cparams(n_axes):
    return pltpu.CompilerParams(dimension_semantics=("arbitrary",) * n_axes,
                                vmem_limit_bytes=VMEM_LIMIT)


def _mod_kernel(cond_ref, w_ref, b_ref, o_ref):
    s = _silu(cond_ref[...])
    o_ref[0] = jnp.dot(s.astype(BF16), w_ref[0].astype(BF16), preferred_element_type=F32) + b_ref[0]


def _modulation(cond, w_mod, b_mod):
    depth = w_mod.shape[0]
    n_rows = cond.shape[0]
    return pl.pallas_call(
        _mod_kernel,
        out_shape=jax.ShapeDtypeStruct((depth, n_rows, 3 * D_MODEL), F32),
        grid=(depth, 3),
        in_specs=[
            pl.BlockSpec((n_rows, D_MODEL), lambda l, n: (0, 0)),
            pl.BlockSpec((1, D_MODEL, D_MODEL), lambda l, n: (l, 0, n)),
            pl.BlockSpec((1, 1, D_MODEL), lambda l, n: (l, 0, n)),
        ],
        out_specs=pl.BlockSpec((1, n_rows, D_MODEL), lambda l, n: (l, 0, n)),
        compiler_params=_cparams(2),
        name="adaln_mod",
    )(cond, w_mod, b_mod)


def _project(x, mod_ref, gain_ref, w_ref, out_refs, outs):
    var = jnp.mean(x * x, axis=-1, keepdims=True)
    y = x * lax.rsqrt(var + EPS) * gain_ref[...]
    h = y * (1.0 + mod_ref[0, 1:2, :]) + mod_ref[0, 0:1, :]
    hb = h.astype(BF16)
    step = 512
    for c in range(0, w_ref.shape[1], step):
        users = [(o_ref, c - col0) for o_ref, (col0, width, _) in zip(out_refs, outs)
                 if col0 <= c < col0 + width]
        if users:
            r = jnp.dot(hb, w_ref[:, c:c + step], preferred_element_type=F32)
            for o_ref, off in users:
                o_ref[0, :, off:off + step] = r.astype(o_ref.dtype)


def _inproj_kernel(x_ref, mod_ref, gain_ref, w_ref, *out_refs, outs):
    _project(x_ref[0], mod_ref, gain_ref, w_ref, out_refs, outs)


def _inproj(x, mod, gain, w, outs, tm, shared_mod):
    B, T, _ = x.shape
    n_cols = w.shape[1]
    mod_map = (lambda b, t: (0, 0, 0)) if shared_mod else (lambda b, t: (b, 0, 0))
    return pl.pallas_call(
        functools.partial(_inproj_kernel, outs=outs),
        out_shape=[jax.ShapeDtypeStruct((B, T, wd), dt) for _, wd, dt in outs],
        grid=(B, T // tm),
        in_specs=[
            pl.BlockSpec((1, tm, D_MODEL), lambda b, t: (b, t, 0)),
            pl.BlockSpec((1, 3, D_MODEL), mod_map),
            pl.BlockSpec((1, D_MODEL), lambda b, t: (0, 0)),
            pl.BlockSpec((D_MODEL, n_cols), lambda b, t: (0, 0)),
        ],
        out_specs=[pl.BlockSpec((1, tm, wd), lambda b, t: (b, t, 0)) for _, wd, _ in outs],
        compiler_params=_cparams(2),
        name="in_proj",
    )(x, mod, gain, w)


def _outproj_kernel(*refs, n_y, final, next_outs):
    y_refs, (x_ref, mod_ref, w_ref), rest = refs[:n_y], refs[n_y:n_y + 3], refs[n_y + 3:]
    m = None
    row = 0
    for y_ref in y_refs:
        width = y_ref.shape[-1]
        part = jnp.dot(y_ref[0], w_ref[row:row + width, :], preferred_element_type=F32)
        m = part if m is None else m + part
        row += width
    xn = x_ref[0] + mod_ref[0, 2:3, :] * m
    if final:
        gain_ref, o_ref = rest
        var = jnp.mean(xn * xn, axis=-1, keepdims=True)
        o_ref[0] = xn * lax.rsqrt(var + EPS) * gain_ref[...]
    else:
        modn_ref, gainn_ref, wn_ref, o_ref = rest[:4]
        o_ref[0] = xn
        _project(xn, modn_ref, gainn_ref, wn_ref, rest[4:], next_outs)


def _outproj(ys, x, mod, w, tm, shared_mod, final_gain=None, next_proj=None):
    B, T, _ = x.shape
    final = final_gain is not None
    mod_map = (lambda b, t: (0, 0, 0)) if shared_mod else (lambda b, t: (b, 0, 0))
    row_block = pl.BlockSpec((1, tm, D_MODEL), lambda b, t: (b, t, 0))
    vec = pl.BlockSpec((1, D_MODEL), lambda b, t: (0, 0))
    in_specs = [pl.BlockSpec((1, tm, y.shape[-1]), lambda b, t: (b, t, 0)) for y in ys] + [
        row_block,
        pl.BlockSpec((1, 3, D_MODEL), mod_map),
        pl.BlockSpec((w.shape[0], D_MODEL), lambda b, t: (0, 0)),
    ]
    args = list(ys) + [x, mod, w]
    out_shape = [jax.ShapeDtypeStruct((B, T, D_MODEL), F32)]
    out_specs = [row_block]
    next_outs = None
    if final:
        in_specs.append(vec)
        args.append(final_gain)
    else:
        mod_n, gain_n, w_n, next_outs = next_proj
        in_specs += [pl.BlockSpec((1, 3, D_MODEL), mod_map), vec,
                     pl.BlockSpec((D_MODEL, w_n.shape[1]), lambda b, t: (0, 0))]
        args += [mod_n, gain_n, w_n]
        out_shape += [jax.ShapeDtypeStruct((B, T, wd), dt) for _, wd, dt in next_outs]
        out_specs += [pl.BlockSpec((1, tm, wd), lambda b, t: (b, t, 0)) for _, wd, _ in next_outs]
    return pl.pallas_call(
        functools.partial(_outproj_kernel, n_y=len(ys), final=final, next_outs=next_outs),
        out_shape=out_shape,
        grid=(B, T // tm),
        in_specs=in_specs,
        out_specs=out_specs,
        compiler_params=_cparams(2),
        name="out_proj",
    )(*args)


def _hgrn_kernel(q_ref, zf_ref, zb_ref, v_ref, g_ref, lgt_ref, gain_ref, *rest, seq, layer, per_block):
    rest = list(rest)
    s0_ref = None if per_block else rest.pop(0)
    o_ref = rest.pop(0)
    sfin_ref = rest.pop(0) if per_block else None
    acc_ref, qd_ref, ki_ref, kd_ref, kv_ref, st_ref, dec_ref, mst_ref, msk_ref, mexp_ref = rest
    R = HGRN_ROWS
    C = HGRN_CHUNK
    n_blk = seq // R
    n_chunk = R // C
    n_all = seq // C

    @pl.when((pl.program_id(0) == 0) & (pl.program_id(1) == 0))
    def _build_masks():
        ti = lax.broadcasted_iota(jnp.int32, (R, R), 0)
        tj = lax.broadcasted_iota(jnp.int32, (R, R), 1)
        shift = C.bit_length() - 1
        same = lax.shift_right_logical(ti, shift) == lax.shift_right_logical(tj, shift)
        one = jnp.ones((R, R), F32)
        zero = jnp.zeros((R, R), F32)
        incl = (jnp.where(same, jnp.where(tj <= ti, one, zero), zero),
                jnp.where(same, jnp.where(tj >= ti, one, zero), zero))
        for d in range(2):
            msk_ref[d] = incl[d]
            mst_ref[d] = incl[d].astype(BF16)
        rr = lax.broadcasted_iota(jnp.int32, (R, n_chunk * LANES), 0)
        cc = lax.broadcasted_iota(jnp.int32, (R, n_chunk * LANES), 1)
        own = lax.shift_right_logical(rr, shift) == lax.shift_right_logical(cc, LANES.bit_length() - 1)
        mexp_ref[...] = jnp.where(own, 1.0, 0.0).astype(BF16)

    lgt = [lgt_ref[:, i, :] for i in range(lgt_ref.shape[1])]
    lmax = functools.reduce(jnp.maximum, lgt)
    ex = [jnp.exp(t - lmax) for t in lgt]
    lb_all = sum(ex[:layer + 1]) / sum(ex)
    gain = gain_ref[0]

    blocks_per_trip = 2 if n_blk % 2 == 0 else 1
    finish_blocks = 4 if n_blk % 4 == 0 else blocks_per_trip

    def gates(i, _):
        for u in range(finish_blocks):
            blk = i * finish_blocks + u
            rows = pl.ds(pl.multiple_of(blk * R, R), R)
            q = q_ref[0, rows, :]
            for d in range(2):
                th = jnp.tanh((zf_ref if d == 0 else zb_ref)[0, rows, :])
                lb = lb_all[d:d + 1, :]
                c = 0.5 * (1.0 - lb)
                ct = c * th
                f = (lb + c) + ct
                k = c - ct
                logf = jnp.log(f)
                hi = logf.astype(BF16)
                lo = (logf - hi.astype(F32)).astype(BF16)
                cs = jnp.dot(mst_ref[d], jnp.concatenate([hi, lo], axis=1), preferred_element_type=F32)
                b = cs[:, 0:LANES] + cs[:, LANES:2 * LANES]
                ends = [c * C + (C - 1 if d == 0 else 0) for c in range(n_chunk)]
                btot = jnp.concatenate([jnp.broadcast_to(b[t:t + 1, :], (C, LANES)) for t in ends], axis=0)
                qd_ref[d, rows, :] = (q * jnp.exp(b)).astype(BF16)
                ki_ref[d, rows, :] = (k * jnp.exp(-b)).astype(BF16)
                kd_ref[d, rows, :] = (k * jnp.exp(btot - b)).astype(BF16)
                for c in range(n_chunk):
                    dec_ref[d, blk * n_chunk + c] = jnp.exp(btot[c * C:c * C + SUBLANES, :])
        return 0

    lax.fori_loop(0, n_blk // finish_blocks, gates, 0)

    def intra(i, _):
        for u in range(finish_blocks):
            blk = i * finish_blocks + u
            rows = pl.ds(pl.multiple_of(blk * R, R), R)
            v = v_ref[0, rows, :]
            vb = v.astype(BF16)
            vt = v.T.astype(BF16)
            att_sum = None
            for d in range(2):
                att = lax.dot_general(qd_ref[d, rows, :], ki_ref[d, rows, :], NT_DIMS,
                                      preferred_element_type=F32)
                att = jnp.where(msk_ref[d] > 0.5, att, 0.0)
                att_sum = att if att_sum is None else att_sum + att
                kd_exp = jnp.concatenate([kd_ref[d, rows, :]] * n_chunk, axis=1) * mexp_ref[...]
                kv_all = jnp.dot(vt, kd_exp, preferred_element_type=F32)
                for c in range(n_chunk):
                    kv_ref[d, blk * n_chunk + c] = kv_all[:, c * LANES:(c + 1) * LANES]
            acc_ref[rows, :] = jnp.dot(att_sum.astype(BF16), vb, preferred_element_type=F32)
        return 0

    lax.fori_loop(0, n_blk // finish_blocks, intra, 0)

    unroll = 4

    def states(i, sts):
        sts = list(sts)
        for u in range(unroll):
            n = i * unroll + u
            for d in range(2):
                c = n if d == 0 else n_all - 1 - n
                st_ref[d, c] = sts[d].astype(BF16)
                dec = jnp.concatenate([dec_ref[d, c]] * (DK_A // SUBLANES), axis=0)
                sts[d] = sts[d] * dec + kv_ref[d, c]
        return tuple(sts)

    def block_states(blk, _):
        for d in range(2):
            st = jnp.zeros((DK_A, DK_A), F32)
            for cc in range(n_chunk):
                c = blk * n_chunk + (cc if d == 0 else n_chunk - 1 - cc)
                st_ref[d, c] = st.astype(BF16)
                dec = jnp.concatenate([dec_ref[d, c]] * (DK_A // SUBLANES), axis=0)
                st = st * dec + kv_ref[d, c]
            sfin_ref[blk, d, 0] = st.T
        return 0

    if per_block:
        lax.fori_loop(0, n_blk, block_states, 0)
    else:
        st0 = (s0_ref[0, 0, 0].T, s0_ref[0, 1, 0].T)
        lax.fori_loop(0, n_all // unroll, states, st0)

    lane_chunk = lax.shift_right_logical(lax.broadcasted_iota(jnp.int32, (DK_A, R), 1), C.bit_length() - 1)

    def finish(i, _):
        slabs = []
        for u in range(finish_blocks):
            blk = i * finish_blocks + u
            rows = pl.ds(pl.multiple_of(blk * R, R), R)
            inter_t = None
            for d in range(2):
                sts = st_ref[d, pl.ds(blk * n_chunk, n_chunk)].reshape(n_chunk * DK_A, DK_A)
                res = lax.dot_general(sts, qd_ref[d, rows, :], NT_DIMS, preferred_element_type=F32)
                picked = res[0:DK_A]
                for c in range(1, n_chunk):
                    picked = jnp.where(lane_chunk == c, res[c * DK_A:(c + 1) * DK_A], picked)
                inter_t = picked if inter_t is None else inter_t + picked
            slabs.append((rows, inter_t))
        for rows, inter_t in slabs:
            tot = acc_ref[rows, :] + inter_t.T
            var = jnp.mean(tot * tot, axis=-1, keepdims=True)
            y = tot * lax.rsqrt(var + EPS) * gain
            o_ref[0, rows, :] = (y * _silu(g_ref[0, rows, :])).astype(o_ref.dtype)
        return 0

    lax.fori_loop(0, n_blk // finish_blocks, finish, 0)


def _hgrn(ya, lgt, layer, gain, s0):
    B, T, width = ya.shape
    per_block = s0 is None
    if per_block:
        assert T == HGRN_ROWS
        group = 8 if B % 8 == 0 else 1
        n_seq, B, T = B, B // group, group * T
        ya = ya.reshape(B, T, width)

    def col(k):
        return pl.BlockSpec((1, T, LANES), lambda b, h, k=k: (b, 0, k * H_A + h))

    in_specs = [col(0), col(1), col(2), col(3), col(4),
                pl.BlockSpec((2, lgt.shape[1], LANES), lambda b, h: (0, 0, h)),
                pl.BlockSpec((1, 1, LANES), lambda b, h: (h, 0, 0))]
    args = [ya, ya, ya, ya, ya, lgt, gain]
    out_shape = [jax.ShapeDtypeStruct((B, T, H_A * DK_A), BF16)]
    out_specs = [pl.BlockSpec((1, T, LANES), lambda b, h: (b, 0, h))]
    if per_block:
        out_shape.append(jax.ShapeDtypeStruct((n_seq, 2, H_A, DK_A, DK_A), F32))
        out_specs.append(pl.BlockSpec((T // HGRN_ROWS, 2, 1, DK_A, DK_A), lambda b, h: (b, 0, h, 0, 0)))
    else:
        in_specs.append(pl.BlockSpec((1, 2, 1, DK_A, DK_A), lambda b, h: (b, 0, h, 0, 0)))
        args.append(s0)
    res = pl.pallas_call(
        functools.partial(_hgrn_kernel, seq=T, layer=layer, per_block=per_block),
        out_shape=out_shape,
        grid=(B, H_A),
        in_specs=in_specs,
        out_specs=out_specs,
        scratch_shapes=[pltpu.VMEM((T, LANES), F32),
                        pltpu.VMEM((2, T, LANES), BF16),
                        pltpu.VMEM((2, T, LANES), BF16),
                        pltpu.VMEM((2, T, LANES), BF16),
                        pltpu.VMEM((2, T // HGRN_CHUNK, DK_A, DK_A), F32),
                        pltpu.VMEM((2, T // HGRN_CHUNK, DK_A, DK_A), BF16),
                        pltpu.VMEM((2, T // HGRN_CHUNK, SUBLANES, LANES), F32),
                        pltpu.VMEM((2, HGRN_ROWS, HGRN_ROWS), BF16),
                        pltpu.VMEM((2, HGRN_ROWS, HGRN_ROWS), F32),
                        pltpu.VMEM((HGRN_ROWS, HGRN_ROWS // HGRN_CHUNK * LANES), BF16)],
        compiler_params=_cparams(2),
        name="hgrn2",
    )(*args)
    if per_block:
        return res[0].reshape(n_seq, HGRN_ROWS, H_A * DK_A), res[1]
    return res[0], None


def _head_masks():
    lane = lax.broadcasted_iota(jnp.int32, (1, LANES), 1)
    return lane < DH_B, lane >= DH_B


def _ctx_attn_kernel(q_ref, k_ref, v_ref, g_ref, kv_ref, o_ref, newk_ref, newv_ref):
    scale = DH_B ** -0.5
    masks = _head_masks()
    T = q_ref.shape[1]
    for h in range(H_B):
        newk_ref[0, h] = kv_ref[0, :, h * DH_B:(h + 1) * DH_B]
        newv_ref[0, h] = kv_ref[0, :, (H_B + h) * DH_B:(H_B + h + 1) * DH_B]
    for p in range(H_B // 2):
        cols = slice(p * LANES, (p + 1) * LANES)
        q = q_ref[0, :, cols] * scale
        qs = jnp.concatenate([jnp.where(masks[h], q, jnp.zeros_like(q)) for h in range(2)], axis=0)
        s = lax.dot_general(qs, k_ref[0, :, cols], NT_DIMS, preferred_element_type=F32)
        e = jnp.exp(s - jnp.max(s, axis=-1, keepdims=True))
        pr = e / jnp.sum(e, axis=-1, keepdims=True)
        o = jnp.dot(pr.astype(BF16), v_ref[0, :, cols], preferred_element_type=F32)
        o = jnp.where(masks[0], o[0:T], o[T:2 * T])
        o_ref[0, :, cols] = (o * _silu(g_ref[0, :, cols].astype(F32))).astype(o_ref.dtype)


def _ctx_attn(yb, kv):
    B, T, _ = yb.shape
    width = H_B * DH_B

    def col(k):
        return pl.BlockSpec((1, T, width), lambda b, k=k: (b, 0, k))

    cache = pl.BlockSpec((1, H_B, T, DH_B), lambda b: (b, 0, 0, 0))
    return pl.pallas_call(
        _ctx_attn_kernel,
        out_shape=[jax.ShapeDtypeStruct((B, T, width), BF16),
                   jax.ShapeDtypeStruct((B, H_B, T, DH_B), F32),
                   jax.ShapeDtypeStruct((B, H_B, T, DH_B), F32)],
        grid=(B,),
        in_specs=[col(0), col(1), col(2), col(3), pl.BlockSpec((1, T, 2 * width), lambda b: (b, 0, 0))],
        out_specs=[pl.BlockSpec((1, T, width), lambda b: (b, 0, 0)), cache, cache],
        compiler_params=_cparams(1),
        name="ctx_attn",
    )(yb, yb, yb, yb, kv)


N_DR = 2 * NA_KH - 1
N_DC = 2 * NA_KW - 1
N_TAB = N_DR - 1


def _nat_kernel(rb_ref, q_ref, k_ref, v_ref, g_ref, kc_ref, vc_ref, o_ref,
                tab_ref, qs_ref, s_ref, p_ref, r_ref, *, rows):
    scale = DH_B ** -0.5
    kh = min(NA_KH, rows)
    masks = _head_masks()

    @pl.when(pl.program_id(1) == 0)
    def _build_tables():
        c = lax.broadcasted_iota(jnp.int32, (GRID_W, LANES), 0)
        lane = lax.broadcasted_iota(jnp.int32, (GRID_W, LANES), 1)
        kcol = lane & (GRID_W - 1)
        ws = jnp.clip(c - NA_KW // 2, 0, GRID_W - NA_KW)
        neg = jnp.full((GRID_W, LANES), NEG_INF, F32)
        inside = jnp.where(kcol >= ws, jnp.where(kcol < ws + NA_KW, 1.0, 0.0), 0.0) > 0.5
        for h in range(2):
            for i in range(N_TAB):
                row = jnp.broadcast_to(rb_ref[h, i:i + 1, :], (GRID_W, LANES))
                toeplitz = pltpu.roll(row, LANES - (NA_KW - 1), 1, stride=1, stride_axis=0)
                tab_ref[h, i] = jnp.where(inside, toeplitz, neg)

    kc = jnp.concatenate([kc_ref[0, 0], kc_ref[0, 1]], axis=1).astype(BF16)
    vc = jnp.concatenate([vc_ref[0, 0], vc_ref[0, 1]], axis=1).astype(BF16)
    n_keys = kh * GRID_W
    n_ctx = kc.shape[0]
    G = NA_GROUP
    W2 = 2 * GRID_W

    def group(gi, _):
        r_first = gi * G
        q0 = pl.multiple_of(r_first * GRID_W, G * GRID_W)
        for i in range(G):
            qi = q_ref[0, pl.ds(q0 + i * GRID_W, GRID_W), :] * scale
            for h in range(2):
                qs_ref[i * W2 + h * GRID_W:i * W2 + (h + 1) * GRID_W, :] = jnp.where(
                    masks[h], qi, jnp.zeros_like(qi))
        s_ref[:, n_keys:n_keys + n_ctx] = lax.dot_general(qs_ref[...], kc, NT_DIMS,
                                                          preferred_element_type=F32)
        windows = {}

        def local_scores(i):
            r = r_first + i
            rs = jnp.clip(r - kh // 2, 0, rows - kh)
            k0 = pl.multiple_of(rs * GRID_W, GRID_W)
            windows[i] = k0
            dr0 = rs - r + (NA_KH - 1)
            bias = jnp.concatenate(
                [jnp.concatenate([tab_ref[h, dr0 + 2 * m] for m in range(kh // 2)], axis=1)
                 for h in range(2)], axis=0)
            s_ref[i * W2:(i + 1) * W2, 0:n_keys] = lax.dot_general(
                qs_ref[i * W2:(i + 1) * W2, :], k_ref[0, pl.ds(k0, n_keys), :], NT_DIMS,
                preferred_element_type=F32) + bias

        def numerators(i):
            s = s_ref[i * W2:(i + 1) * W2, :]
            e = jnp.exp(s - jnp.max(s, axis=-1, keepdims=True))
            p_ref[i * W2:(i + 1) * W2, :] = e.astype(BF16)
            rinv = 1.0 / jnp.sum(e, axis=-1, keepdims=True)
            r_ref[i * W2:(i + 1) * W2, :] = jnp.broadcast_to(rinv, (W2, LANES))

        def weighted_values(i):
            vals = jnp.concatenate([v_ref[0, pl.ds(windows[i], n_keys), :], vc], axis=0)
            o = jnp.dot(p_ref[i * W2:(i + 1) * W2, :], vals, preferred_element_type=F32)
            o = o * r_ref[i * W2:(i + 1) * W2, :]
            o = jnp.where(masks[0], o[0:GRID_W], o[GRID_W:W2])
            out_rows = pl.ds(q0 + i * GRID_W, GRID_W)
            gate = g_ref[0, out_rows, :].astype(F32)
            o_ref[0, out_rows, :] = (o * _silu(gate)).astype(o_ref.dtype)

        for step in range(G + 2):
            if step < G:
                local_scores(step)
            if 0 <= step - 1 < G:
                numerators(step - 1)
            if 0 <= step - 2 < G:
                weighted_values(step - 2)
        return 0

    lax.fori_loop(0, rows // G, group, 0)


def _nat(yb, kc, vc, rel_bias):
    B, T, _ = yb.shape
    Tc = kc.shape[2]
    n_pair = H_B // 2
    rows = T // GRID_W
    n_stack = NA_GROUP * 2 * GRID_W
    n_keys = min(NA_KH, rows) * GRID_W
    pad = jnp.zeros((H_B, N_TAB, GRID_W - N_DC), F32)
    rel = rel_bias.astype(F32)
    rb_rows = jnp.concatenate([rel[:, 0:N_TAB], pad, rel[:, 1:N_TAB + 1], pad], axis=-1)

    def col(k):
        return pl.BlockSpec((1, T, LANES), lambda p, b, k=k: (b, 0, k * n_pair + p))

    ctx = pl.BlockSpec((1, 2, Tc, DH_B), lambda p, b: (b, p, 0, 0))
    return pl.pallas_call(
        functools.partial(_nat_kernel, rows=rows),
        out_shape=jax.ShapeDtypeStruct((B, T, H_B * DH_B), BF16),
        grid=(n_pair, B),
        in_specs=[pl.BlockSpec((2, N_TAB, LANES), lambda p, b: (p, 0, 0)),
                  col(0), col(1), col(2), col(3), ctx, ctx],
        out_specs=pl.BlockSpec((1, T, LANES), lambda p, b: (b, 0, p)),
        scratch_shapes=[pltpu.VMEM((2, N_TAB, GRID_W, LANES), F32),
                        pltpu.VMEM((n_stack, LANES), BF16),
                        pltpu.VMEM((n_stack, n_keys + Tc), F32),
                        pltpu.VMEM((n_stack, n_keys + Tc), BF16),
                        pltpu.VMEM((n_stack, LANES), F32)],
        compiler_params=_cparams(2),
        name="nbr_attn",
    )(rb_rows, yb, yb, yb, yb, kc, vc)


def _seg_len(seq):
    length = -(-seq // N_SEG)
    while length % 8 != 4:
        length += 1
    return length


def _step_block(seg_len):
    return max(d for d in range(1, seg_len + 1) if seg_len % d == 0 and d * N_SEG <= RG_ROWS * 3)


def _rglru_kernel(x_ref, g_ref, cw_ref, cb_ref, wg_ref, lam_ref, *rest, seq, slabs, has_s0, emit_state):
    rest = list(rest)
    s0_ref = rest.pop(0) if has_s0 else None
    o_ref = rest.pop(0)
    hfin_ref = rest.pop(0) if emit_state else None
    xpad_ref, a_ref, u_ref, h_ref, p_ref, hn_ref = rest
    L = _seg_len(seq)
    n_rows = N_SEG * L
    RB = RG_ROWS
    CB = slabs
    TB = _step_block(L)
    n_tile = N_SEG // SUBLANES
    lead = SUBLANES
    chains = [(d, j, s) for d in range(2) for j in range(CB) for s in range(n_tile)]

    for j in range(CB):
        xpad_ref[j, 0:lead, :] = jnp.zeros((lead, LANES), F32)
        xpad_ref[j, lead:seq + lead, :] = x_ref[0, :, j * LANES:(j + 1) * LANES]
        xpad_ref[j, seq + lead:n_rows + 2 * lead, :] = jnp.zeros((n_rows + lead - seq, LANES), F32)

    nl = -lam_ref[...]
    sp = jnp.maximum(nl, 0.0) + jnp.log1p(jnp.exp(-jnp.abs(nl)))
    cw = cw_ref[...]
    cbias = cb_ref[...]
    ones2 = jnp.where(lax.broadcasted_iota(jnp.int32, (TB * N_SEG, LANES), 1) < 2, 1.0, 0.0).astype(BF16)

    def gates(blk, _):
        t0 = blk * TB
        r0 = pl.multiple_of(blk * (TB * N_SEG), TB * N_SEG)
        for j in range(CB):
            lanes = slice(j * LANES, (j + 1) * LANES)
            tiles = []
            for tt in range(TB):
                for s in range(n_tile):
                    first = lead - 2 + t0 + tt + s * SUBLANES * L
                    taps = [xpad_ref[j, pl.ds(first + k, SUBLANES, stride=L), :] for k in range(4)]
                    xt = cw[0:1, lanes] * taps[0] + cw[1:2, lanes] * taps[1]
                    xt = xt + cw[2:3, lanes] * taps[2]
                    tiles.append(xt + cw[3:4, lanes] * taps[3] + cbias[:, lanes])
            xj = jnp.concatenate(tiles, axis=0)
            gt = jnp.dot(jnp.concatenate([xj.astype(BF16), ones2], axis=1), wg_ref[j],
                         preferred_element_type=F32)
            xh = 0.5 * xj
            for d in range(2):
                th_r = jnp.tanh(gt[:, (2 * d) * LANES:(2 * d + 1) * LANES])
                th_i = jnp.tanh(gt[:, (2 * d + 1) * LANES:(2 * d + 2) * LANES])
                half = (-0.5 * RG_C) * sp[d:d + 1, lanes]
                la = half + half * th_r
                a = jnp.exp(la)
                y = -jnp.tanh(la) * (1.0 + a * a)
                root = jnp.where(y > 0.0, y * lax.rsqrt(y), 0.0)
                a_ref[d, j, pl.ds(r0, TB * N_SEG), :] = a
                u_ref[d, j, pl.ds(r0, TB * N_SEG), :] = root * (xh + xh * th_i)
        return 0

    lax.fori_loop(0, L // TB, gates, 0)

    first_pad = [[min(max(seq - (s * SUBLANES + r) * L, 0), L) for r in range(SUBLANES)]
                 for s in range(n_tile)]
    sub = lax.broadcasted_iota(jnp.int32, (SUBLANES, LANES), 0)
    pad_from = []
    for s in range(n_tile):
        if all(f == L for f in first_pad[s]):
            pad_from.append(None)
        else:
            vec = jnp.full((SUBLANES, LANES), L, jnp.int32)
            for r in range(SUBLANES):
                vec = jnp.where(sub == r, first_pad[s][r], vec)
            pad_from.append(vec)

    def step_rows(step, s):
        return pl.ds(pl.multiple_of(step * N_SEG + s * SUBLANES, SUBLANES), SUBLANES)

    unroll = 4

    def scan(i, carry):
        carry = list(carry)
        for k in range(unroll):
            for n, (d, j, s) in enumerate(chains):
                h, pr = carry[n]
                t = i * unroll + k
                if d == 1:
                    t = L - 1 - t
                idx = step_rows(t, s)
                a = a_ref[d, j, idx, :]
                u = u_ref[d, j, idx, :]
                if pad_from[s] is not None:
                    live = t < pad_from[s]
                    a = jnp.where(live, a, 1.0)
                    u = jnp.where(live, u, 0.0)
                h = a * h + u
                pr = pr * a
                h_ref[d, j, idx, :] = h
                p_ref[d, j, idx, :] = pr
                carry[n] = (h, pr)
        return tuple(carry)

    zero = jnp.zeros((SUBLANES, LANES), F32)
    one = jnp.ones((SUBLANES, LANES), F32)
    ends = lax.fori_loop(0, L // unroll, scan, ((zero, one),) * len(chains))

    cins = {}
    finals = [[None] * CB for _ in range(2)]
    for d in range(2):
        for j in range(CB):
            if has_s0:
                c = s0_ref[0, d:d + 1, j * LANES:(j + 1) * LANES]
            else:
                c = jnp.zeros((1, LANES), F32)
            cin = [None] * N_SEG
            for kk in range(N_SEG):
                seg = kk if d == 0 else N_SEG - 1 - kk
                s, row = divmod(seg, SUBLANES)
                h_end, p_end = ends[chains.index((d, j, s))]
                cin[seg] = c
                c = h_end[row:row + 1, :] + p_end[row:row + 1, :] * c
            finals[d][j] = c
            for s in range(n_tile):
                cins[(d, j, s)] = jnp.concatenate(cin[s * SUBLANES:(s + 1) * SUBLANES], axis=0)

    def fix(i, _):
        for k in range(unroll):
            t = i * unroll + k
            for j in range(CB):
                for s in range(n_tile):
                    idx = step_rows(t, s)
                    parts = [h_ref[d, j, idx, :] + p_ref[d, j, idx, :] * cins[(d, j, s)] for d in range(2)]
                    hn_ref[j, pl.ds(t + s * SUBLANES * L, SUBLANES, stride=L), :] = parts[0] + parts[1]
        return 0

    lax.fori_loop(0, L // unroll, fix, 0)

    if emit_state:
        hfin_ref[0] = jnp.concatenate([jnp.concatenate(finals[d], axis=1) for d in range(2)], axis=0)

    def combine(blk, _):
        rows = pl.ds(pl.multiple_of(blk * RB, RB), RB)
        hs = jnp.concatenate([hn_ref[j, rows, :] for j in range(CB)], axis=1)
        o_ref[0, rows, :] = (hs * _silu(g_ref[0, rows, :])).astype(o_ref.dtype)
        return 0

    lax.fori_loop(0, seq // RB, combine, 0)


def _gate_weights(gate_w, gate_b):
    w = (0.5 * gate_w).transpose(2, 3, 0, 1, 4).reshape(H_C, BW_C, 4 * BW_C).astype(BF16)
    b = (0.5 * gate_b).reshape(2, 2, H_C, BW_C).transpose(2, 0, 1, 3).reshape(H_C, 1, 4 * BW_C).astype(F32)
    hi = b.astype(BF16)
    lo = (b - hi.astype(F32)).astype(BF16)
    zeros = jnp.zeros((H_C, BW_C - 2, 4 * BW_C), BF16)
    return jnp.concatenate([w, hi, lo, zeros], axis=1)


def _rglru(xg, conv_w, conv_b, wg, lam, s0, emit_state):
    B, T, _ = xg.shape
    has_s0 = s0 is not None
    n_rows = N_SEG * _seg_len(T)
    CB = RG_SLABS if T > 4 * RG_ROWS else 2 * RG_SLABS
    wide = CB * LANES
    n_steps = H_C // CB
    in_specs = [
        pl.BlockSpec((1, T, wide), lambda b, c: (b, 0, c)),
        pl.BlockSpec((1, T, wide), lambda b, c: (b, 0, n_steps + c)),
        pl.BlockSpec((4, wide), lambda b, c: (0, c)),
        pl.BlockSpec((1, wide), lambda b, c: (0, c)),
        pl.BlockSpec((CB, 2 * BW_C, 4 * BW_C), lambda b, c: (c, 0, 0)),
        pl.BlockSpec((2, wide), lambda b, c: (0, c)),
    ]
    args = [xg, xg, conv_w, conv_b, wg, lam]
    if has_s0:
        in_specs.append(pl.BlockSpec((1, 2, wide), lambda b, c: (b, 0, c)))
        args.append(s0)
    out_shape = [jax.ShapeDtypeStruct((B, T, W_C), BF16)]
    out_specs = [pl.BlockSpec((1, T, wide), lambda b, c: (b, 0, c))]
    if emit_state:
        out_shape.append(jax.ShapeDtypeStruct((B, 2, W_C), F32))
        out_specs.append(pl.BlockSpec((1, 2, wide), lambda b, c: (b, 0, c)))
    res = pl.pallas_call(
        functools.partial(_rglru_kernel, seq=T, slabs=CB, has_s0=has_s0, emit_state=emit_state),
        out_shape=out_shape,
        grid=(B, n_steps),
        in_specs=in_specs,
        out_specs=out_specs,
        scratch_shapes=[pltpu.VMEM((CB, n_rows + 2 * SUBLANES, LANES), F32)]
        + [pltpu.VMEM((2, CB, n_rows, LANES), F32)] * 4 + [pltpu.VMEM((CB, n_rows, LANES), F32)],
        compiler_params=_cparams(2),
        name="rglru",
    )(*args)
    return res if emit_state else (res[0], None)


A_COLS = 5 * H_A * DK_A
B_COLS = 4 * H_B * DH_B


def kernel(x_prompt, x_sample, state_hgrn, cache_na_k, cache_na_v, state_rglru, c, c_ctx, norm_gain, w_mod, b_mod, w_in_even, w_out_even, hgrn_lb_logits, hgrn_out_gain, na_rel_bias, w_in_odd, w_out_odd, conv_w, conv_b, rg_gate_w, rg_gate_b, rg_lambda, final_gain):
    n_ctx = x_prompt.shape[0]
    n_lat = x_sample.shape[0]
    depth = w_mod.shape[0]

    cond = jnp.zeros((16, D_MODEL), F32).at[0].set(c_ctx).at[1:1 + n_lat].set(c)
    mod = _modulation(cond, w_mod, b_mod.reshape(depth, 1, 3 * D_MODEL))
    mod = mod.reshape(depth, 16, 3, D_MODEL)

    t_ctx = x_prompt.shape[1]

    def flat(a):
        return a.reshape(1, n_ctx * t_ctx, a.shape[-1])

    def unflat(a):
        return a.reshape(n_ctx, t_ctx, a.shape[-1])

    def in_proj_params(l):
        if l % 2 == 0:
            outs_s = ((0, A_COLS, F32), (A_COLS, B_COLS, BF16))
            outs_c = outs_s + ((A_COLS + H_B * DH_B, 2 * H_B * DH_B, F32),)
            a_key = H_A * DK_A
            col = jnp.arange(w_in_even.shape[-1])
            halve = jnp.where((col >= a_key) & (col < 3 * a_key), 0.5, 1.0).astype(F32)
            return (w_in_even[l // 2] * halve).astype(BF16), outs_c, outs_s
        outs = ((0, 2 * W_C, F32),)
        return w_in_odd[l // 2].astype(BF16), outs, outs

    xc, xs = x_prompt, x_sample
    new_hgrn, new_k, new_v, new_rg = [], [], [], []
    proj_c = proj_s = None
    for l in range(depth):
        j = l // 2
        mod_c, mod_s = mod[l, 0:1], mod[l, 1:1 + n_lat]
        if proj_c is None:
            gain = norm_gain[l].reshape(1, D_MODEL)
            w_in, outs_c, outs_s = in_proj_params(l)
            proj_c = [unflat(t) for t in _inproj(flat(xc), mod_c, gain, w_in, outs_c, 512, True)]
            proj_s = _inproj(xs, mod_s, gain, w_in, outs_s, 512, False)
        if l % 2 == 0:
            w_out = w_out_even[j].astype(BF16)
            (ya_c, yb_c, kv_c), (ya_s, yb_s) = proj_c, proj_s
            hgain = hgrn_out_gain[j].reshape(H_A, 1, DK_A)
            oa_c, s_fin = _hgrn(ya_c, hgrn_lb_logits, j, hgain, None)
            oa_s, _ = _hgrn(ya_s, hgrn_lb_logits, j, hgain, state_hgrn[:, j])
            ob_c, k_c, v_c = _ctx_attn(yb_c, kv_c)
            ob_s = _nat(yb_s, cache_na_k[:, j], cache_na_v[:, j], na_rel_bias[j])
            ys_c, ys_s = (oa_c, ob_c), (oa_s, ob_s)
            new_hgrn.append(s_fin)
            new_k.append(k_c)
            new_v.append(v_c)
        else:
            w_out = w_out_odd[j].astype(BF16)
            (xg_c,), (xg_s,) = proj_c, proj_s
            wg = _gate_weights(rg_gate_w[j], rg_gate_b[j])
            cb = conv_b[j].reshape(1, W_C)
            y_c, h_fin = _rglru(xg_c, conv_w[j], cb, wg, rg_lambda[j], None, True)
            y_s, _ = _rglru(xg_s, conv_w[j], cb, wg, rg_lambda[j], state_rglru[:, j], False)
            ys_c, ys_s = (y_c,), (y_s,)
            new_rg.append(h_fin)
        ys_c = tuple(flat(y) for y in ys_c)
        if l == depth - 1:
            fgain = final_gain.reshape(1, D_MODEL)
            (xc,) = _outproj(ys_c, flat(xc), mod_c, w_out, 1024, True, final_gain=fgain)
            (xs,) = _outproj(ys_s, xs, mod_s, w_out, 1024, False, final_gain=fgain)
            xc = unflat(xc)
        else:
            gain_n = norm_gain[l + 1].reshape(1, D_MODEL)
            w_n, outs_c, outs_s = in_proj_params(l + 1)
            mod_cn, mod_sn = mod[l + 1, 0:1], mod[l + 1, 1:1 + n_lat]
            xc, *proj_c = _outproj(ys_c, flat(xc), mod_c, w_out, 512, True,
                                   next_proj=(mod_cn, gain_n, w_n, outs_c))
            xs, *proj_s = _outproj(ys_s, xs, mod_s, w_out, 512, False,
                                   next_proj=(mod_sn, gain_n, w_n, outs_s))
            xc = unflat(xc)
            proj_c = [unflat(t) for t in proj_c]
    return (xc, xs, jnp.stack(new_hgrn, axis=1), jnp.stack(new_k, axis=1),
            jnp.stack(new_v, axis=1), jnp.stack(new_rg, axis=1))
```

```python
import functools

import jax
import jax.numpy as jnp
from jax import lax
from jax.experimental import pallas as pl
from jax.experimental.pallas import tpu as pltpu

F32 = jnp.float32
BF16 = jnp.bfloat16

D_MODEL = 1024
EPS = 1e-6
NEG_INF = -1e30
H_A = 4
DK_A = 128
HGRN_CHUNK = 32
HGRN_ROWS = 256
H_B = 8
DH_B = 64
GRID_W = 64
NA_KH = 8
NA_KW = 16
NA_GROUP = 16
W_C = 1024
H_C = 8
BW_C = W_C // H_C
RG_C = 8.0
RG_ROWS = 256
RG_SLABS = 2
N_SEG = 16
LANES = 128
SUBLANES = 8
VMEM_LIMIT = 48 * 1024 * 1024

NT_DIMS = (((1,), (1,)), ((), ()))


def _silu(x):
    half = 0.5 * x
    return half + half * jnp.tanh(half)


def _cparams(n_axes):
    return pltpu.CompilerParams(dimension_semantics=("arbitrary",) * n_axes,
                                vmem_limit_bytes=VMEM_LIMIT)


def _mod_kernel(cond_ref, w_ref, b_ref, o_ref):
    s = _silu(cond_ref[...])
    o_ref[0] = jnp.dot(s.astype(BF16), w_ref[0].astype(BF16), preferred_element_type=F32) + b_ref[0]


def _modulation(cond, w_mod, b_mod):
    depth = w_mod.shape[0]
    n_rows = cond.shape[0]
    return pl.pallas_call(
        _mod_kernel,
        out_shape=jax.ShapeDtypeStruct((depth, n_rows, 3 * D_MODEL), F32),
        grid=(depth, 3),
        in_specs=[
            pl.BlockSpec((n_rows, D_MODEL), lambda l, n: (0, 0)),
            pl.BlockSpec((1, D_MODEL, D_MODEL), lambda l, n: (l, 0, n)),
            pl.BlockSpec((1, 1, D_MODEL), lambda l, n: (l, 0, n)),
        ],
        out_specs=pl.BlockSpec((1, n_rows, D_MODEL), lambda l, n: (l, 0, n)),
        compiler_params=_cparams(2),
        name="adaln_mod",
    )(cond, w_mod, b_mod)


def _project(x, mod_ref, gain_ref, w_ref, out_refs, outs):
    var = jnp.mean(x * x, axis=-1, keepdims=True)
    y = x * lax.rsqrt(var + EPS) * gain_ref[...]
    h = y * (1.0 + mod_ref[0, 1:2, :]) + mod_ref[0, 0:1, :]
    hb = h.astype(BF16)
    step = 512
    for c in range(0, w_ref.shape[1], step):
        users = [(o_ref, c - col0) for o_ref, (col0, width, _) in zip(out_refs, outs)
                 if col0 <= c < col0 + width]
        if users:
            r = jnp.dot(hb, w_ref[:, c:c + step], preferred_element_type=F32)
            for o_ref, off in users:
                o_ref[0, :, off:off + step] = r.astype(o_ref.dtype)


def _inproj_kernel(x_ref, mod_ref, gain_ref, w_ref, *out_refs, outs):
    _project(x_ref[0], mod_ref, gain_ref, w_ref, out_refs, outs)


def _inproj(x, mod, gain, w, outs, tm, shared_mod):
    B, T, _ = x.shape
    n_cols = w.shape[1]
    mod_map = (lambda b, t: (0, 0, 0)) if shared_mod else (lambda b, t: (b, 0, 0))
    return pl.pallas_call(
        functools.partial(_inproj_kernel, outs=outs),
        out_shape=[jax.ShapeDtypeStruct((B, T, wd), dt) for _, wd, dt in outs],
        grid=(B, T // tm),
        in_specs=[
            pl.BlockSpec((1, tm, D_MODEL), lambda b, t: (b, t, 0)),
            pl.BlockSpec((1, 3, D_MODEL), mod_map),
            pl.BlockSpec((1, D_MODEL), lambda b, t: (0, 0)),
            pl.BlockSpec((D_MODEL, n_cols), lambda b, t: (0, 0)),
        ],
        out_specs=[pl.BlockSpec((1, tm, wd), lambda b, t: (b, t, 0)) for _, wd, _ in outs],
        compiler_params=_cparams(2),
        name="in_proj",
    )(x, mod, gain, w)


def _outproj_kernel(*refs, n_y, final, next_outs):
    y_refs, (x_ref, mod_ref, w_ref), rest = refs[:n_y], refs[n_y:n_y + 3], refs[n_y + 3:]
    m = None
    row = 0
    for y_ref in y_refs:
        width = y_ref.shape[-1]
        part = jnp.dot(y_ref[0], w_ref[row:row + width, :], preferred_element_type=F32)
        m = part if m is None else m + part
        row += width
    xn = x_ref[0] + mod_ref[0, 2:3, :] * m
    if final:
        gain_ref, o_ref = rest
        var = jnp.mean(xn * xn, axis=-1, keepdims=True)
        o_ref[0] = xn * lax.rsqrt(var + EPS) * gain_ref[...]
    else:
        modn_ref, gainn_ref, wn_ref, o_ref = rest[:4]
        o_ref[0] = xn
        _project(xn, modn_ref, gainn_ref, wn_ref, rest[4:], next_outs)


def _outproj(ys, x, mod, w, tm, shared_mod, final_gain=None, next_proj=None):
    B, T, _ = x.shape
    final = final_gain is not None
    mod_map = (lambda b, t: (0, 0, 0)) if shared_mod else (lambda b, t: (b, 0, 0))
    row_block = pl.BlockSpec((1, tm, D_MODEL), lambda b, t: (b, t, 0))
    vec = pl.BlockSpec((1, D_MODEL), lambda b, t: (0, 0))
    in_specs = [pl.BlockSpec((1, tm, y.shape[-1]), lambda b, t: (b, t, 0)) for y in ys] + [
        row_block,
        pl.BlockSpec((1, 3, D_MODEL), mod_map),
        pl.BlockSpec((w.shape[0], D_MODEL), lambda b, t: (0, 0)),
    ]
    args = list(ys) + [x, mod, w]
    out_shape = [jax.ShapeDtypeStruct((B, T, D_MODEL), F32)]
    out_specs = [row_block]
    next_outs = None
    if final:
        in_specs.append(vec)
        args.append(final_gain)
    else:
        mod_n, gain_n, w_n, next_outs = next_proj
        in_specs += [pl.BlockSpec((1, 3, D_MODEL), mod_map), vec,
                     pl.BlockSpec((D_MODEL, w_n.shape[1]), lambda b, t: (0, 0))]
        args += [mod_n, gain_n, w_n]
        out_shape += [jax.ShapeDtypeStruct((B, T, wd), dt) for _, wd, dt in next_outs]
        out_specs += [pl.BlockSpec((1, tm, wd), lambda b, t: (b, t, 0)) for _, wd, _ in next_outs]
    return pl.pallas_call(
        functools.partial(_outproj_kernel, n_y=len(ys), final=final, next_outs=next_outs),
        out_shape=out_shape,
        grid=(B, T // tm),
        in_specs=in_specs,
        out_specs=out_specs,
        compiler_params=_cparams(2),
        name="out_proj",
    )(*args)


def _hgrn_kernel(q_ref, zf_ref, zb_ref, v_ref, g_ref, lgt_ref, gain_ref, *rest, seq, layer, per_block):
    rest = list(rest)
    s0_ref = None if per_block else rest.pop(0)
    o_ref = rest.pop(0)
    sfin_ref = rest.pop(0) if per_block else None
    acc_ref, qd_ref, ki_ref, kd_ref, kv_ref, st_ref, dec_ref, mst_ref, msk_ref, mexp_ref = rest
    R = HGRN_ROWS
    C = HGRN_CHUNK
    n_blk = seq // R
    n_chunk = R // C
    n_all = seq // C

    @pl.when((pl.program_id(0) == 0) & (pl.program_id(1) == 0))
    def _build_masks():
        ti = lax.broadcasted_iota(jnp.int32, (R, R), 0)
        tj = lax.broadcasted_iota(jnp.int32, (R, R), 1)
        shift = C.bit_length() - 1
        same = lax.shift_right_logical(ti, shift) == lax.shift_right_logical(tj, shift)
        one = jnp.ones((R, R), F32)
        zero = jnp.zeros((R, R), F32)
        incl = (jnp.where(same, jnp.where(tj <= ti, one, zero), zero),
                jnp.where(same, jnp.where(tj >= ti, one, zero), zero))
        for d in range(2):
            msk_ref[d] = incl[d]
            mst_ref[d] = incl[d].astype(BF16)
        rr = lax.broadcasted_iota(jnp.int32, (R, n_chunk * LANES), 0)
        cc = lax.broadcasted_iota(jnp.int32, (R, n_chunk * LANES), 1)
        own = lax.shift_right_logical(rr, shift) == lax.shift_right_logical(cc, LANES.bit_length() - 1)
        mexp_ref[...] = jnp.where(own, 1.0, 0.0).astype(BF16)

    lgt = [lgt_ref[:, i, :] for i in range(lgt_ref.shape[1])]
    lmax = functools.reduce(jnp.maximum, lgt)
    ex = [jnp.exp(t - lmax) for t in lgt]
    lb_all = sum(ex[:layer + 1]) / sum(ex)
    gain = gain_ref[0]

    blocks_per_trip = 2 if n_blk % 2 == 0 else 1
    finish_blocks = 8 if n_blk % 8 == 0 else blocks_per_trip

    def gates(i, _):
        for u in range(finish_blocks):
            blk = i * finish_blocks + u
            rows = pl.ds(pl.multiple_of(blk * R, R), R)
            q = q_ref[0, rows, :]
            for d in range(2):
                th = jnp.tanh((zf_ref if d == 0 else zb_ref)[0, rows, :])
                lb = lb_all[d:d + 1, :]
                c = 0.5 * (1.0 - lb)
                ct = c * th
                f = (lb + c) + ct
                k = c - ct
                logf = jnp.log(f)
                hi = logf.astype(BF16)
                lo = (logf - hi.astype(F32)).astype(BF16)
                cs = jnp.dot(mst_ref[d], jnp.concatenate([hi, lo], axis=1), preferred_element_type=F32)
                b = cs[:, 0:LANES] + cs[:, LANES:2 * LANES]
                ends = [c * C + (C - 1 if d == 0 else 0) for c in range(n_chunk)]
                btot = jnp.concatenate([jnp.broadcast_to(b[t:t + 1, :], (C, LANES)) for t in ends], axis=0)
                qd_ref[d, rows, :] = (q * jnp.exp(b)).astype(BF16)
                ki_ref[d, rows, :] = (k * jnp.exp(-b)).astype(BF16)
                kd_ref[d, rows, :] = (k * jnp.exp(btot - b)).astype(BF16)
                for c in range(n_chunk):
                    dec_ref[d, blk * n_chunk + c] = jnp.exp(btot[c * C:c * C + SUBLANES, :])
        return 0

    lax.fori_loop(0, n_blk // finish_blocks, gates, 0)

    def intra(i, _):
        for u in range(finish_blocks):
            blk = i * finish_blocks + u
            rows = pl.ds(pl.multiple_of(blk * R, R), R)
            v = v_ref[0, rows, :]
            vb = v.astype(BF16)
            vt = v.T.astype(BF16)
            att_sum = None
            for d in range(2):
                att = lax.dot_general(qd_ref[d, rows, :], ki_ref[d, rows, :], NT_DIMS,
                                      preferred_element_type=F32)
                att = jnp.where(msk_ref[d] > 0.5, att, 0.0)
                att_sum = att if att_sum is None else att_sum + att
                kd_exp = jnp.concatenate([kd_ref[d, rows, :]] * n_chunk, axis=1) * mexp_ref[...]
                kv_all = jnp.dot(vt, kd_exp, preferred_element_type=F32)
                for c in range(n_chunk):
                    kv_ref[d, blk * n_chunk + c] = kv_all[:, c * LANES:(c + 1) * LANES]
            acc_ref[rows, :] = jnp.dot(att_sum.astype(BF16), vb, preferred_element_type=F32)
        return 0

    lax.fori_loop(0, n_blk // finish_blocks, intra, 0)

    unroll = 4

    def states(i, sts):
        sts = list(sts)
        for u in range(unroll):
            n = i * unroll + u
            for d in range(2):
                c = n if d == 0 else n_all - 1 - n
                st_ref[d, c] = sts[d].astype(BF16)
                dec = jnp.concatenate([dec_ref[d, c]] * (DK_A // SUBLANES), axis=0)
                sts[d] = sts[d] * dec + kv_ref[d, c]
        return tuple(sts)

    def block_states(blk, _):
        for d in range(2):
            st = jnp.zeros((DK_A, DK_A), F32)
            for cc in range(n_chunk):
                c = blk * n_chunk + (cc if d == 0 else n_chunk - 1 - cc)
                st_ref[d, c] = st.astype(BF16)
                dec = jnp.concatenate([dec_ref[d, c]] * (DK_A // SUBLANES), axis=0)
                st = st * dec + kv_ref[d, c]
            sfin_ref[blk, d, 0] = st.T
        return 0

    if per_block:
        lax.fori_loop(0, n_blk, block_states, 0)
    else:
        st0 = (s0_ref[0, 0, 0].T, s0_ref[0, 1, 0].T)
        lax.fori_loop(0, n_all // unroll, states, st0)

    lane_chunk = lax.shift_right_logical(lax.broadcasted_iota(jnp.int32, (DK_A, R), 1), C.bit_length() - 1)

    def finish(i, _):
        slabs = []
        for u in range(finish_blocks):
            blk = i * finish_blocks + u
            rows = pl.ds(pl.multiple_of(blk * R, R), R)
            inter_t = None
            for d in range(2):
                sts = st_ref[d, pl.ds(blk * n_chunk, n_chunk)].reshape(n_chunk * DK_A, DK_A)
                res = lax.dot_general(sts, qd_ref[d, rows, :], NT_DIMS, preferred_element_type=F32)
                picked = res[0:DK_A]
                for c in range(1, n_chunk):
                    picked = jnp.where(lane_chunk == c, res[c * DK_A:(c + 1) * DK_A], picked)
                inter_t = picked if inter_t is None else inter_t + picked
            slabs.append((rows, inter_t))
        for rows, inter_t in slabs:
            tot = acc_ref[rows, :] + inter_t.T
            var = jnp.mean(tot * tot, axis=-1, keepdims=True)
            y = tot * lax.rsqrt(var + EPS) * gain
            o_ref[0, rows, :] = (y * _silu(g_ref[0, rows, :])).astype(o_ref.dtype)
        return 0

    lax.fori_loop(0, n_blk // finish_blocks, finish, 0)


def _hgrn(ya, lgt, layer, gain, s0):
    B, T, width = ya.shape
    per_block = s0 is None
    if per_block:
        assert T == HGRN_ROWS
        group = 8 if B % 8 == 0 else 1
        n_seq, B, T = B, B // group, group * T
        ya = ya.reshape(B, T, width)

    def col(k):
        return pl.BlockSpec((1, T, LANES), lambda b, h, k=k: (b, 0, k * H_A + h))

    in_specs = [col(0), col(1), col(2), col(3), col(4),
                pl.BlockSpec((2, lgt.shape[1], LANES), lambda b, h: (0, 0, h)),
                pl.BlockSpec((1, 1, LANES), lambda b, h: (h, 0, 0))]
    args = [ya, ya, ya, ya, ya, lgt, gain]
    out_shape = [jax.ShapeDtypeStruct((B, T, H_A * DK_A), BF16)]
    out_specs = [pl.BlockSpec((1, T, LANES), lambda b, h: (b, 0, h))]
    if per_block:
        out_shape.append(jax.ShapeDtypeStruct((n_seq, 2, H_A, DK_A, DK_A), F32))
        out_specs.append(pl.BlockSpec((T // HGRN_ROWS, 2, 1, DK_A, DK_A), lambda b, h: (b, 0, h, 0, 0)))
    else:
        in_specs.append(pl.BlockSpec((1, 2, 1, DK_A, DK_A), lambda b, h: (b, 0, h, 0, 0)))
        args.append(s0)
    res = pl.pallas_call(
        functools.partial(_hgrn_kernel, seq=T, layer=layer, per_block=per_block),
        out_shape=out_shape,
        grid=(B, H_A),
        in_specs=in_specs,
        out_specs=out_specs,
        scratch_shapes=[pltpu.VMEM((T, LANES), F32),
                        pltpu.VMEM((2, T, LANES), BF16),
                        pltpu.VMEM((2, T, LANES), BF16),
                        pltpu.VMEM((2, T, LANES), BF16),
                        pltpu.VMEM((2, T // HGRN_CHUNK, DK_A, DK_A), F32),
                        pltpu.VMEM((2, T // HGRN_CHUNK, DK_A, DK_A), BF16),
                        pltpu.VMEM((2, T // HGRN_CHUNK, SUBLANES, LANES), F32),
                        pltpu.VMEM((2, HGRN_ROWS, HGRN_ROWS), BF16),
                        pltpu.VMEM((2, HGRN_ROWS, HGRN_ROWS), F32),
                        pltpu.VMEM((HGRN_ROWS, HGRN_ROWS // HGRN_CHUNK * LANES), BF16)],
        compiler_params=_cparams(2),
        name="hgrn2",
    )(*args)
    if per_block:
        return res[0].reshape(n_seq, HGRN_ROWS, H_A * DK_A), res[1]
    return res[0], None


def _head_masks():
    lane = lax.broadcasted_iota(jnp.int32, (1, LANES), 1)
    return lane < DH_B, lane >= DH_B


def _ctx_attn_kernel(q_ref, k_ref, v_ref, g_ref, kv_ref, o_ref, newk_ref, newv_ref):
    scale = DH_B ** -0.5
    masks = _head_masks()
    T = q_ref.shape[1]
    for h in range(H_B):
        newk_ref[0, h] = kv_ref[0, :, h * DH_B:(h + 1) * DH_B]
        newv_ref[0, h] = kv_ref[0, :, (H_B + h) * DH_B:(H_B + h + 1) * DH_B]
    for p in range(H_B // 2):
        cols = slice(p * LANES, (p + 1) * LANES)
        q = q_ref[0, :, cols] * scale
        qs = jnp.concatenate([jnp.where(masks[h], q, jnp.zeros_like(q)) for h in range(2)], axis=0)
        s = lax.dot_general(qs, k_ref[0, :, cols], NT_DIMS, preferred_element_type=F32)
        e = jnp.exp(s - jnp.max(s, axis=-1, keepdims=True))
        pr = e / jnp.sum(e, axis=-1, keepdims=True)
        o = jnp.dot(pr.astype(BF16), v_ref[0, :, cols], preferred_element_type=F32)
        o = jnp.where(masks[0], o[0:T], o[T:2 * T])
        o_ref[0, :, cols] = (o * _silu(g_ref[0, :, cols].astype(F32))).astype(o_ref.dtype)


def _ctx_attn(yb, kv):
    B, T, _ = yb.shape
    width = H_B * DH_B

    def col(k):
        return pl.BlockSpec((1, T, width), lambda b, k=k: (b, 0, k))

    cache = pl.BlockSpec((1, H_B, T, DH_B), lambda b: (b, 0, 0, 0))
    return pl.pallas_call(
        _ctx_attn_kernel,
        out_shape=[jax.ShapeDtypeStruct((B, T, width), BF16),
                   jax.ShapeDtypeStruct((B, H_B, T, DH_B), F32),
                   jax.ShapeDtypeStruct((B, H_B, T, DH_B), F32)],
        grid=(B,),
        in_specs=[col(0), col(1), col(2), col(3), pl.BlockSpec((1, T, 2 * width), lambda b: (b, 0, 0))],
        out_specs=[pl.BlockSpec((1, T, width), lambda b: (b, 0, 0)), cache, cache],
        compiler_params=_cparams(1),
        name="ctx_attn",
    )(yb, yb, yb, yb, kv)


N_DR = 2 * NA_KH - 1
N_DC = 2 * NA_KW - 1
N_TAB = N_DR - 1


def _nat_kernel(rb_ref, q_ref, k_ref, v_ref, g_ref, kc_ref, vc_ref, o_ref,
                tab_ref, qs_ref, s_ref, p_ref, r_ref, *, rows):
    scale = DH_B ** -0.5
    kh = min(NA_KH, rows)
    masks = _head_masks()

    @pl.when(pl.program_id(1) == 0)
    def _build_tables():
        c = lax.broadcasted_iota(jnp.int32, (GRID_W, LANES), 0)
        lane = lax.broadcasted_iota(jnp.int32, (GRID_W, LANES), 1)
        kcol = lane & (GRID_W - 1)
        ws = jnp.clip(c - NA_KW // 2, 0, GRID_W - NA_KW)
        neg = jnp.full((GRID_W, LANES), NEG_INF, F32)
        inside = jnp.where(kcol >= ws, jnp.where(kcol < ws + NA_KW, 1.0, 0.0), 0.0) > 0.5
        for h in range(2):
            for i in range(N_TAB):
                row = jnp.broadcast_to(rb_ref[h, i:i + 1, :], (GRID_W, LANES))
                toeplitz = pltpu.roll(row, LANES - (NA_KW - 1), 1, stride=1, stride_axis=0)
                tab_ref[h, i] = jnp.where(inside, toeplitz, neg)

    kc = jnp.concatenate([kc_ref[0, 0], kc_ref[0, 1]], axis=1).astype(BF16)
    vc = jnp.concatenate([vc_ref[0, 0], vc_ref[0, 1]], axis=1).astype(BF16)
    n_keys = kh * GRID_W
    n_ctx = kc.shape[0]
    G = NA_GROUP
    W2 = 2 * GRID_W

    def group(gi, _):
        r_first = gi * G
        q0 = pl.multiple_of(r_first * GRID_W, G * GRID_W)
        for i in range(G):
            qi = q_ref[0, pl.ds(q0 + i * GRID_W, GRID_W), :] * scale
            for h in range(2):
                qs_ref[i * W2 + h * GRID_W:i * W2 + (h + 1) * GRID_W, :] = jnp.where(
                    masks[h], qi, jnp.zeros_like(qi))
        s_ref[:, n_keys:n_keys + n_ctx] = lax.dot_general(qs_ref[...], kc, NT_DIMS,
                                                          preferred_element_type=F32)
        windows = {}

        def local_scores(i):
            r = r_first + i
            rs = jnp.clip(r - kh // 2, 0, rows - kh)
            k0 = pl.multiple_of(rs * GRID_W, GRID_W)
            windows[i] = k0
            dr0 = rs - r + (NA_KH - 1)
            bias = jnp.concatenate(
                [jnp.concatenate([tab_ref[h, dr0 + 2 * m] for m in range(kh // 2)], axis=1)
                 for h in range(2)], axis=0)
            s_ref[i * W2:(i + 1) * W2, 0:n_keys] = lax.dot_general(
                qs_ref[i * W2:(i + 1) * W2, :], k_ref[0, pl.ds(k0, n_keys), :], NT_DIMS,
                preferred_element_type=F32) + bias

        def numerators(i):
            s = s_ref[i * W2:(i + 1) * W2, :]
            e = jnp.exp(s - jnp.max(s, axis=-1, keepdims=True))
            p_ref[i * W2:(i + 1) * W2, :] = e.astype(BF16)
            rinv = 1.0 / jnp.sum(e, axis=-1, keepdims=True)
            r_ref[i * W2:(i + 1) * W2, :] = jnp.broadcast_to(rinv, (W2, LANES))

        def weighted_values(i):
            vals = jnp.concatenate([v_ref[0, pl.ds(windows[i], n_keys), :], vc], axis=0)
            o = jnp.dot(p_ref[i * W2:(i + 1) * W2, :], vals, preferred_element_type=F32)
            o = o * r_ref[i * W2:(i + 1) * W2, :]
            o = jnp.where(masks[0], o[0:GRID_W], o[GRID_W:W2])
            out_rows = pl.ds(q0 + i * GRID_W, GRID_W)
            gate = g_ref[0, out_rows, :].astype(F32)
            o_ref[0, out_rows, :] = (o * _silu(gate)).astype(o_ref.dtype)

        for step in range(G + 2):
            if step < G:
                local_scores(step)
            if 0 <= step - 1 < G:
                numerators(step - 1)
            if 0 <= step - 2 < G:
                weighted_values(step - 2)
        return 0

    lax.fori_loop(0, rows // G, group, 0)


def _nat(yb, kc, vc, rel_bias):
    B, T, _ = yb.shape
    Tc = kc.shape[2]
    n_pair = H_B // 2
    rows = T // GRID_W
    n_stack = NA_GROUP * 2 * GRID_W
    n_keys = min(NA_KH, rows) * GRID_W
    pad = jnp.zeros((H_B, N_TAB, GRID_W - N_DC), F32)
    rel = rel_bias.astype(F32)
    rb_rows = jnp.concatenate([rel[:, 0:N_TAB], pad, rel[:, 1:N_TAB + 1], pad], axis=-1)

    def col(k):
        return pl.BlockSpec((1, T, LANES), lambda p, b, k=k: (b, 0, k * n_pair + p))

    ctx = pl.BlockSpec((1, 2, Tc, DH_B), lambda p, b: (b, p, 0, 0))
    return pl.pallas_call(
        functools.partial(_nat_kernel, rows=rows),
        out_shape=jax.ShapeDtypeStruct((B, T, H_B * DH_B), BF16),
        grid=(n_pair, B),
        in_specs=[pl.BlockSpec((2, N_TAB, LANES), lambda p, b: (p, 0, 0)),
                  col(0), col(1), col(2), col(3), ctx, ctx],
        out_specs=pl.BlockSpec((1, T, LANES), lambda p, b: (b, 0, p)),
        scratch_shapes=[pltpu.VMEM((2, N_TAB, GRID_W, LANES), F32),
                        pltpu.VMEM((n_stack, LANES), BF16),
                        pltpu.VMEM((n_stack, n_keys + Tc), F32),
                        pltpu.VMEM((n_stack, n_keys + Tc), BF16),
                        pltpu.VMEM((n_stack, LANES), F32)],
        compiler_params=_cparams(2),
        name="nbr_attn",
    )(rb_rows, yb, yb, yb, yb, kc, vc)


def _seg_len(seq):
    length = -(-seq // N_SEG)
    while length % 8 != 4:
        length += 1
    return length


def _step_block(seg_len):
    return max(d for d in range(1, seg_len + 1) if seg_len % d == 0 and d * N_SEG <= RG_ROWS * 5)


def _rglru_kernel(x_ref, g_ref, cw_ref, cb_ref, wg_ref, lam_ref, *rest, seq, slabs, has_s0, emit_state):
    rest = list(rest)
    s0_ref = rest.pop(0) if has_s0 else None
    o_ref = rest.pop(0)
    hfin_ref = rest.pop(0) if emit_state else None
    xpad_ref, a_ref, u_ref, h_ref, p_ref, hn_ref = rest
    L = _seg_len(seq)
    n_rows = N_SEG * L
    RB = RG_ROWS
    CB = slabs
    TB = _step_block(L)
    n_tile = N_SEG // SUBLANES
    lead = SUBLANES
    chains = [(d, j, s) for d in range(2) for j in range(CB) for s in range(n_tile)]

    for j in range(CB):
        xpad_ref[j, 0:lead, :] = jnp.zeros((lead, LANES), F32)
        xpad_ref[j, lead:seq + lead, :] = x_ref[0, :, j * LANES:(j + 1) * LANES]
        xpad_ref[j, seq + lead:n_rows + 2 * lead, :] = jnp.zeros((n_rows + lead - seq, LANES), F32)

    nl = -lam_ref[...]
    sp = jnp.maximum(nl, 0.0) + jnp.log1p(jnp.exp(-jnp.abs(nl)))
    cw = cw_ref[...]
    cbias = cb_ref[...]
    ones2 = jnp.where(lax.broadcasted_iota(jnp.int32, (TB * N_SEG, LANES), 1) < 2, 1.0, 0.0).astype(BF16)

    def gates(blk, _):
        t0 = blk * TB
        r0 = pl.multiple_of(blk * (TB * N_SEG), TB * N_SEG)
        for j in range(CB):
            lanes = slice(j * LANES, (j + 1) * LANES)
            tiles = []
            for tt in range(TB):
                for s in range(n_tile):
                    first = lead - 2 + t0 + tt + s * SUBLANES * L
                    taps = [xpad_ref[j, pl.ds(first + k, SUBLANES, stride=L), :] for k in range(4)]
                    xt = cw[0:1, lanes] * taps[0] + cw[1:2, lanes] * taps[1]
                    xt = xt + cw[2:3, lanes] * taps[2]
                    tiles.append(xt + cw[3:4, lanes] * taps[3] + cbias[:, lanes])
            xj = jnp.concatenate(tiles, axis=0)
            gt = jnp.dot(jnp.concatenate([xj.astype(BF16), ones2], axis=1), wg_ref[j],
                         preferred_element_type=F32)
            xh = 0.5 * xj
            for d in range(2):
                th_r = jnp.tanh(gt[:, (2 * d) * LANES:(2 * d + 1) * LANES])
                th_i = jnp.tanh(gt[:, (2 * d + 1) * LANES:(2 * d + 2) * LANES])
                half = (-0.5 * RG_C) * sp[d:d + 1, lanes]
                la = half + half * th_r
                a = jnp.exp(la)
                y = -jnp.tanh(la) * (1.0 + a * a)
                root = jnp.where(y > 0.0, y * lax.rsqrt(y), 0.0)
                a_ref[d, j, pl.ds(r0, TB * N_SEG), :] = a
                u_ref[d, j, pl.ds(r0, TB * N_SEG), :] = root * (xh + xh * th_i)
        return 0

    lax.fori_loop(0, L // TB, gates, 0)

    first_pad = [[min(max(seq - (s * SUBLANES + r) * L, 0), L) for r in range(SUBLANES)]
                 for s in range(n_tile)]
    sub = lax.broadcasted_iota(jnp.int32, (SUBLANES, LANES), 0)
    pad_from = []
    for s in range(n_tile):
        if all(f == L for f in first_pad[s]):
            pad_from.append(None)
        else:
            vec = jnp.full((SUBLANES, LANES), L, jnp.int32)
            for r in range(SUBLANES):
                vec = jnp.where(sub == r, first_pad[s][r], vec)
            pad_from.append(vec)

    def step_rows(step, s):
        return pl.ds(pl.multiple_of(step * N_SEG + s * SUBLANES, SUBLANES), SUBLANES)

    unroll = 4

    def scan(i, carry):
        carry = list(carry)
        for k in range(unroll):
            for n, (d, j, s) in enumerate(chains):
                h, pr = carry[n]
                t = i * unroll + k
                if d == 1:
                    t = L - 1 - t
                idx = step_rows(t, s)
                a = a_ref[d, j, idx, :]
                u = u_ref[d, j, idx, :]
                if pad_from[s] is not None:
                    live = t < pad_from[s]
                    a = jnp.where(live, a, 1.0)
                    u = jnp.where(live, u, 0.0)
                h = a * h + u
                pr = pr * a
                h_ref[d, j, idx, :] = h
                p_ref[d, j, idx, :] = pr
                carry[n] = (h, pr)
        return tuple(carry)

    zero = jnp.zeros((SUBLANES, LANES), F32)
    one = jnp.ones((SUBLANES, LANES), F32)
    ends = lax.fori_loop(0, L // unroll, scan, ((zero, one),) * len(chains))

    cins = {}
    finals = [[None] * CB for _ in range(2)]
    for d in range(2):
        for j in range(CB):
            if has_s0:
                c = s0_ref[0, d:d + 1, j * LANES:(j + 1) * LANES]
            else:
                c = jnp.zeros((1, LANES), F32)
            cin = [None] * N_SEG
            for kk in range(N_SEG):
                seg = kk if d == 0 else N_SEG - 1 - kk
                s, row = divmod(seg, SUBLANES)
                h_end, p_end = ends[chains.index((d, j, s))]
                cin[seg] = c
                c = h_end[row:row + 1, :] + p_end[row:row + 1, :] * c
            finals[d][j] = c
            for s in range(n_tile):
                cins[(d, j, s)] = jnp.concatenate(cin[s * SUBLANES:(s + 1) * SUBLANES], axis=0)

    def fix(i, _):
        for k in range(unroll):
            t = i * unroll + k
            for j in range(CB):
                for s in range(n_tile):
                    idx = step_rows(t, s)
                    parts = [h_ref[d, j, idx, :] + p_ref[d, j, idx, :] * cins[(d, j, s)] for d in range(2)]
                    hn_ref[j, pl.ds(t + s * SUBLANES * L, SUBLANES, stride=L), :] = parts[0] + parts[1]
        return 0

    lax.fori_loop(0, L // unroll, fix, 0)

    if emit_state:
        hfin_ref[0] = jnp.concatenate([jnp.concatenate(finals[d], axis=1) for d in range(2)], axis=0)

    def combine(blk, _):
        rows = pl.ds(pl.multiple_of(blk * RB, RB), RB)
        hs = jnp.concatenate([hn_ref[j, rows, :] for j in range(CB)], axis=1)
        o_ref[0, rows, :] = (hs * _silu(g_ref[0, rows, :])).astype(o_ref.dtype)
        return 0

    lax.fori_loop(0, seq // RB, combine, 0)


def _gate_weights(gate_w, gate_b):
    w = (0.5 * gate_w).transpose(2, 3, 0, 1, 4).reshape(H_C, BW_C, 4 * BW_C).astype(BF16)
    b = (0.5 * gate_b).reshape(2, 2, H_C, BW_C).transpose(2, 0, 1, 3).reshape(H_C, 1, 4 * BW_C).astype(F32)
    hi = b.astype(BF16)
    lo = (b - hi.astype(F32)).astype(BF16)
    zeros = jnp.zeros((H_C, BW_C - 2, 4 * BW_C), BF16)
    return jnp.concatenate([w, hi, lo, zeros], axis=1)


def _rglru(xg, conv_w, conv_b, wg, lam, s0, emit_state):
    B, T, _ = xg.shape
    has_s0 = s0 is not None
    n_rows = N_SEG * _seg_len(T)
    CB = RG_SLABS if T > 4 * RG_ROWS else 2 * RG_SLABS
    wide = CB * LANES
    n_steps = H_C // CB
    in_specs = [
        pl.BlockSpec((1, T, wide), lambda b, c: (b, 0, c)),
        pl.BlockSpec((1, T, wide), lambda b, c: (b, 0, n_steps + c)),
        pl.BlockSpec((4, wide), lambda b, c: (0, c)),
        pl.BlockSpec((1, wide), lambda b, c: (0, c)),
        pl.BlockSpec((CB, 2 * BW_C, 4 * BW_C), lambda b, c: (c, 0, 0)),
        pl.BlockSpec((2, wide), lambda b, c: (0, c)),
    ]
    args = [xg, xg, conv_w, conv_b, wg, lam]
    if has_s0:
        in_specs.append(pl.BlockSpec((1, 2, wide), lambda b, c: (b, 0, c)))
        args.append(s0)
    out_shape = [jax.ShapeDtypeStruct((B, T, W_C), BF16)]
    out_specs = [pl.BlockSpec((1, T, wide), lambda b, c: (b, 0, c))]
    if emit_state:
        out_shape.append(jax.ShapeDtypeStruct((B, 2, W_C), F32))
        out_specs.append(pl.BlockSpec((1, 2, wide), lambda b, c: (b, 0, c)))
    res = pl.pallas_call(
        functools.partial(_rglru_kernel, seq=T, slabs=CB, has_s0=has_s0, emit_state=emit_state),
        out_shape=out_shape,
        grid=(B, n_steps),
        in_specs=in_specs,
        out_specs=out_specs,
        scratch_shapes=[pltpu.VMEM((CB, n_rows + 2 * SUBLANES, LANES), F32)]
        + [pltpu.VMEM((2, CB, n_rows, LANES), F32)] * 4 + [pltpu.VMEM((CB, n_rows, LANES), F32)],
        compiler_params=_cparams(2),
        name="rglru",
    )(*args)
    return res if emit_state else (res[0], None)


A_COLS = 5 * H_A * DK_A
B_COLS = 4 * H_B * DH_B


def kernel(x_prompt, x_sample, state_hgrn, cache_na_k, cache_na_v, state_rglru, c, c_ctx, norm_gain, w_mod, b_mod, w_in_even, w_out_even, hgrn_lb_logits, hgrn_out_gain, na_rel_bias, w_in_odd, w_out_odd, conv_w, conv_b, rg_gate_w, rg_gate_b, rg_lambda, final_gain):
    n_ctx = x_prompt.shape[0]
    n_lat = x_sample.shape[0]
    depth = w_mod.shape[0]

    cond = jnp.zeros((16, D_MODEL), F32).at[0].set(c_ctx).at[1:1 + n_lat].set(c)
    mod = _modulation(cond, w_mod, b_mod.reshape(depth, 1, 3 * D_MODEL))
    mod = mod.reshape(depth, 16, 3, D_MODEL)

    t_ctx = x_prompt.shape[1]

    def flat(a):
        return a.reshape(1, n_ctx * t_ctx, a.shape[-1])

    def unflat(a):
        return a.reshape(n_ctx, t_ctx, a.shape[-1])

    def in_proj_params(l):
        if l % 2 == 0:
            outs_s = ((0, A_COLS, F32), (A_COLS, B_COLS, BF16))
            outs_c = outs_s + ((A_COLS + H_B * DH_B, 2 * H_B * DH_B, F32),)
            a_key = H_A * DK_A
            col = jnp.arange(w_in_even.shape[-1])
            halve = jnp.where((col >= a_key) & (col < 3 * a_key), 0.5, 1.0).astype(F32)
            return (w_in_even[l // 2] * halve).astype(BF16), outs_c, outs_s
        outs = ((0, 2 * W_C, F32),)
        return w_in_odd[l // 2].astype(BF16), outs, outs

    xc, xs = x_prompt, x_sample
    new_hgrn, new_k, new_v, new_rg = [], [], [], []
    proj_c = proj_s = None
    for l in range(depth):
        j = l // 2
        mod_c, mod_s = mod[l, 0:1], mod[l, 1:1 + n_lat]
        if proj_c is None:
            gain = norm_gain[l].reshape(1, D_MODEL)
            w_in, outs_c, outs_s = in_proj_params(l)
            proj_c = [unflat(t) for t in _inproj(flat(xc), mod_c, gain, w_in, outs_c, 512, True)]
            proj_s = _inproj(xs, mod_s, gain, w_in, outs_s, 512, False)
        if l % 2 == 0:
            w_out = w_out_even[j].astype(BF16)
            (ya_c, yb_c, kv_c), (ya_s, yb_s) = proj_c, proj_s
            hgain = hgrn_out_gain[j].reshape(H_A, 1, DK_A)
            oa_c, s_fin = _hgrn(ya_c, hgrn_lb_logits, j, hgain, None)
            oa_s, _ = _hgrn(ya_s, hgrn_lb_logits, j, hgain, state_hgrn[:, j])
            ob_c, k_c, v_c = _ctx_attn(yb_c, kv_c)
            ob_s = _nat(yb_s, cache_na_k[:, j], cache_na_v[:, j], na_rel_bias[j])
            ys_c, ys_s = (oa_c, ob_c), (oa_s, ob_s)
            new_hgrn.append(s_fin)
            new_k.append(k_c)
            new_v.append(v_c)
        else:
            w_out = w_out_odd[j].astype(BF16)
            (xg_c,), (xg_s,) = proj_c, proj_s
            wg = _gate_weights(rg_gate_w[j], rg_gate_b[j])
            cb = conv_b[j].reshape(1, W_C)
            y_c, h_fin = _rglru(xg_c, conv_w[j], cb, wg, rg_lambda[j], None, True)
            y_s, _ = _rglru(xg_s, conv_w[j], cb, wg, rg_lambda[j], state_rglru[:, j], False)
            ys_c, ys_s = (y_c,), (y_s,)
            new_rg.append(h_fin)
        ys_c = tuple(flat(y) for y in ys_c)
        if l == depth - 1:
            fgain = final_gain.reshape(1, D_MODEL)
            (xc,) = _outproj(ys_c, flat(xc), mod_c, w_out, 1024, True, final_gain=fgain)
            (xs,) = _outproj(ys_s, xs, mod_s, w_out, 1024, False, final_gain=fgain)
            xc = unflat(xc)
        else:
            gain_n = norm_gain[l + 1].reshape(1, D_MODEL)
            w_n, outs_c, outs_s = in_proj_params(l + 1)
            mod_cn, mod_sn = mod[l + 1, 0:1], mod[l + 1, 1:1 + n_lat]
            xc, *proj_c = _outproj(ys_c, flat(xc), mod_c, w_out, 512, True,
                                   next_proj=(mod_cn, gain_n, w_n, outs_c))
            xs, *proj_s = _outproj(ys_s, xs, mod_s, w_out, 512, False,
                                   next_proj=(mod_sn, gain_n, w_n, outs_s))
            xc = unflat(xc)
            proj_c = [unflat(t) for t in proj_c]
    return (xc, xs, jnp.stack(new_hgrn, axis=1), jnp.stack(new_k, axis=1),
            jnp.stack(new_v, axis=1), jnp.stack(new_rg, axis=1))
```

```python
import functools

import jax
import jax.numpy as jnp
from jax import lax
from jax.experimental import pallas as pl
from jax.experimental.pallas import tpu as pltpu

F32 = jnp.float32
BF16 = jnp.bfloat16

D_MODEL = 1024
EPS = 1e-6
NEG_INF = -1e30
H_A = 4
DK_A = 128
HGRN_CHUNK = 32
HGRN_ROWS = 256
H_B = 8
DH_B = 64
GRID_W = 64
NA_KH = 8
NA_KW = 16
NA_GROUP = 32
W_C = 1024
H_C = 8
BW_C = W_C // H_C
RG_C = 8.0
RG_ROWS = 256
RG_SLABS = 2
N_SEG = 16
LANES = 128
SUBLANES = 8
VMEM_LIMIT = 48 * 1024 * 1024

NT_DIMS = (((1,), (1,)), ((), ()))


def _silu(x):
    half = 0.5 * x
    return half + half * jnp.tanh(half)


def _cparams(n_axes):
    return pltpu.CompilerParams(dimension_semantics=("arbitrary",) * n_axes,
                                vmem_limit_bytes=VMEM_LIMIT)


def _mod_kernel(cond_ref, w_ref, b_ref, o_ref):
    s = _silu(cond_ref[...])
    o_ref[0] = jnp.dot(s.astype(BF16), w_ref[0].astype(BF16), preferred_element_type=F32) + b_ref[0]


def _modulation(cond, w_mod, b_mod):
    depth = w_mod.shape[0]
    n_rows = cond.shape[0]
    return pl.pallas_call(
        _mod_kernel,
        out_shape=jax.ShapeDtypeStruct((depth, n_rows, 3 * D_MODEL), F32),
        grid=(depth, 3),
        in_specs=[
            pl.BlockSpec((n_rows, D_MODEL), lambda l, n: (0, 0)),
            pl.BlockSpec((1, D_MODEL, D_MODEL), lambda l, n: (l, 0, n)),
            pl.BlockSpec((1, 1, D_MODEL), lambda l, n: (l, 0, n)),
        ],
        out_specs=pl.BlockSpec((1, n_rows, D_MODEL), lambda l, n: (l, 0, n)),
        compiler_params=_cparams(2),
        name="adaln_mod",
    )(cond, w_mod, b_mod)


def _project(x, mod_ref, gain_ref, w_ref, out_refs, outs):
    var = jnp.mean(x * x, axis=-1, keepdims=True)
    y = x * lax.rsqrt(var + EPS) * gain_ref[...]
    h = y * (1.0 + mod_ref[0, 1:2, :]) + mod_ref[0, 0:1, :]
    hb = h.astype(BF16)
    step = 512
    for c in range(0, w_ref.shape[1], step):
        users = [(o_ref, c - col0) for o_ref, (col0, width, _) in zip(out_refs, outs)
                 if col0 <= c < col0 + width]
        if users:
            r = jnp.dot(hb, w_ref[:, c:c + step], preferred_element_type=F32)
            for o_ref, off in users:
                o_ref[0, :, off:off + step] = r.astype(o_ref.dtype)


def _inproj_kernel(x_ref, mod_ref, gain_ref, w_ref, *out_refs, outs):
    _project(x_ref[0], mod_ref, gain_ref, w_ref, out_refs, outs)


def _inproj(x, mod, gain, w, outs, tm, shared_mod):
    B, T, _ = x.shape
    n_cols = w.shape[1]
    mod_map = (lambda b, t: (0, 0, 0)) if shared_mod else (lambda b, t: (b, 0, 0))
    return pl.pallas_call(
        functools.partial(_inproj_kernel, outs=outs),
        out_shape=[jax.ShapeDtypeStruct((B, T, wd), dt) for _, wd, dt in outs],
        grid=(B, T // tm),
        in_specs=[
            pl.BlockSpec((1, tm, D_MODEL), lambda b, t: (b, t, 0)),
            pl.BlockSpec((1, 3, D_MODEL), mod_map),
            pl.BlockSpec((1, D_MODEL), lambda b, t: (0, 0)),
            pl.BlockSpec((D_MODEL, n_cols), lambda b, t: (0, 0)),
        ],
        out_specs=[pl.BlockSpec((1, tm, wd), lambda b, t: (b, t, 0)) for _, wd, _ in outs],
        compiler_params=_cparams(2),
        name="in_proj",
    )(x, mod, gain, w)


def _outproj_kernel(*refs, n_y, final, next_outs):
    y_refs, (x_ref, mod_ref, w_ref), rest = refs[:n_y], refs[n_y:n_y + 3], refs[n_y + 3:]
    m = None
    row = 0
    for y_ref in y_refs:
        width = y_ref.shape[-1]
        part = jnp.dot(y_ref[0], w_ref[row:row + width, :], preferred_element_type=F32)
        m = part if m is None else m + part
        row += width
    xn = x_ref[0] + mod_ref[0, 2:3, :] * m
    if final:
        gain_ref, o_ref = rest
        var = jnp.mean(xn * xn, axis=-1, keepdims=True)
        o_ref[0] = xn * lax.rsqrt(var + EPS) * gain_ref[...]
    else:
        modn_ref, gainn_ref, wn_ref, o_ref = rest[:4]
        o_ref[0] = xn
        _project(xn, modn_ref, gainn_ref, wn_ref, rest[4:], next_outs)


def _outproj(ys, x, mod, w, tm, shared_mod, final_gain=None, next_proj=None):
    B, T, _ = x.shape
    final = final_gain is not None
    mod_map = (lambda b, t: (0, 0, 0)) if shared_mod else (lambda b, t: (b, 0, 0))
    row_block = pl.BlockSpec((1, tm, D_MODEL), lambda b, t: (b, t, 0))
    vec = pl.BlockSpec((1, D_MODEL), lambda b, t: (0, 0))
    in_specs = [pl.BlockSpec((1, tm, y.shape[-1]), lambda b, t: (b, t, 0)) for y in ys] + [
        row_block,
        pl.BlockSpec((1, 3, D_MODEL), mod_map),
        pl.BlockSpec((w.shape[0], D_MODEL), lambda b, t: (0, 0)),
    ]
    args = list(ys) + [x, mod, w]
    out_shape = [jax.ShapeDtypeStruct((B, T, D_MODEL), F32)]
    out_specs = [row_block]
    next_outs = None
    if final:
        in_specs.append(vec)
        args.append(final_gain)
    else:
        mod_n, gain_n, w_n, next_outs = next_proj
        in_specs += [pl.BlockSpec((1, 3, D_MODEL), mod_map), vec,
                     pl.BlockSpec((D_MODEL, w_n.shape[1]), lambda b, t: (0, 0))]
        args += [mod_n, gain_n, w_n]
        out_shape += [jax.ShapeDtypeStruct((B, T, wd), dt) for _, wd, dt in next_outs]
        out_specs += [pl.BlockSpec((1, tm, wd), lambda b, t: (b, t, 0)) for _, wd, _ in next_outs]
    return pl.pallas_call(
        functools.partial(_outproj_kernel, n_y=len(ys), final=final, next_outs=next_outs),
        out_shape=out_shape,
        grid=(B, T // tm),
        in_specs=in_specs,
        out_specs=out_specs,
        compiler_params=_cparams(2),
        name="out_proj",
    )(*args)


def _hgrn_kernel(q_ref, zf_ref, zb_ref, v_ref, g_ref, lgt_ref, gain_ref, *rest, seq, layer, per_block):
    rest = list(rest)
    s0_ref = None if per_block else rest.pop(0)
    o_ref = rest.pop(0)
    sfin_ref = rest.pop(0) if per_block else None
    acc_ref, qd_ref, ki_ref, kd_ref, kv_ref, st_ref, dec_ref, mst_ref, msk_ref, mexp_ref = rest
    R = HGRN_ROWS
    C = HGRN_CHUNK
    n_blk = seq // R
    n_chunk = R // C
    n_all = seq // C

    @pl.when((pl.program_id(0) == 0) & (pl.program_id(1) == 0))
    def _build_masks():
        ti = lax.broadcasted_iota(jnp.int32, (R, R), 0)
        tj = lax.broadcasted_iota(jnp.int32, (R, R), 1)
        shift = C.bit_length() - 1
        same = lax.shift_right_logical(ti, shift) == lax.shift_right_logical(tj, shift)
        one = jnp.ones((R, R), F32)
        zero = jnp.zeros((R, R), F32)
        incl = (jnp.where(same, jnp.where(tj <= ti, one, zero), zero),
                jnp.where(same, jnp.where(tj >= ti, one, zero), zero))
        for d in range(2):
            msk_ref[d] = incl[d]
            mst_ref[d] = incl[d].astype(BF16)
        rr = lax.broadcasted_iota(jnp.int32, (R, n_chunk * LANES), 0)
        cc = lax.broadcasted_iota(jnp.int32, (R, n_chunk * LANES), 1)
        own = lax.shift_right_logical(rr, shift) == lax.shift_right_logical(cc, LANES.bit_length() - 1)
        mexp_ref[...] = jnp.where(own, 1.0, 0.0).astype(BF16)

    lgt = [lgt_ref[:, i, :] for i in range(lgt_ref.shape[1])]
    lmax = functools.reduce(jnp.maximum, lgt)
    ex = [jnp.exp(t - lmax) for t in lgt]
    lb_all = sum(ex[:layer + 1]) / sum(ex)
    gain = gain_ref[0]

    blocks_per_trip = 2 if n_blk % 2 == 0 else 1
    finish_blocks = 8 if n_blk % 8 == 0 else blocks_per_trip

    def gates(i, _):
        for u in range(finish_blocks):
            blk = i * finish_blocks + u
            rows = pl.ds(pl.multiple_of(blk * R, R), R)
            q = q_ref[0, rows, :]
            for d in range(2):
                th = jnp.tanh((zf_ref if d == 0 else zb_ref)[0, rows, :])
                lb = lb_all[d:d + 1, :]
                c = 0.5 * (1.0 - lb)
                ct = c * th
                f = (lb + c) + ct
                k = c - ct
                logf = jnp.log(f)
                hi = logf.astype(BF16)
                lo = (logf - hi.astype(F32)).astype(BF16)
                cs = jnp.dot(mst_ref[d], jnp.concatenate([hi, lo], axis=1), preferred_element_type=F32)
                b = cs[:, 0:LANES] + cs[:, LANES:2 * LANES]
                ends = [c * C + (C - 1 if d == 0 else 0) for c in range(n_chunk)]
                btot = jnp.concatenate([jnp.broadcast_to(b[t:t + 1, :], (C, LANES)) for t in ends], axis=0)
                qd_ref[d, rows, :] = (q * jnp.exp(b)).astype(BF16)
                ki_ref[d, rows, :] = (k * jnp.exp(-b)).astype(BF16)
                kd_ref[d, rows, :] = (k * jnp.exp(btot - b)).astype(BF16)
                for c in range(n_chunk):
                    dec_ref[d, blk * n_chunk + c] = jnp.exp(btot[c * C:c * C + SUBLANES, :])
        return 0

    lax.fori_loop(0, n_blk // finish_blocks, gates, 0)

    def intra(i, _):
        for u in range(finish_blocks):
            blk = i * finish_blocks + u
            rows = pl.ds(pl.multiple_of(blk * R, R), R)
            v = v_ref[0, rows, :]
            vb = v.astype(BF16)
            vt = v.T.astype(BF16)
            att_sum = None
            for d in range(2):
                att = lax.dot_general(qd_ref[d, rows, :], ki_ref[d, rows, :], NT_DIMS,
                                      preferred_element_type=F32)
                att = jnp.where(msk_ref[d] > 0.5, att, 0.0)
                att_sum = att if att_sum is None else att_sum + att
                kd_exp = jnp.concatenate([kd_ref[d, rows, :]] * n_chunk, axis=1) * mexp_ref[...]
                kv_all = jnp.dot(vt, kd_exp, preferred_element_type=F32)
                for c in range(n_chunk):
                    kv_ref[d, blk * n_chunk + c] = kv_all[:, c * LANES:(c + 1) * LANES]
            acc_ref[rows, :] = jnp.dot(att_sum.astype(BF16), vb, preferred_element_type=F32)
        return 0

    lax.fori_loop(0, n_blk // finish_blocks, intra, 0)

    unroll = 4

    def states(i, sts):
        sts = list(sts)
        for u in range(unroll):
            n = i * unroll + u
            for d in range(2):
                c = n if d == 0 else n_all - 1 - n
                st_ref[d, c] = sts[d].astype(BF16)
                dec = jnp.concatenate([dec_ref[d, c]] * (DK_A // SUBLANES), axis=0)
                sts[d] = sts[d] * dec + kv_ref[d, c]
        return tuple(sts)

    def block_states(blk, _):
        for d in range(2):
            st = jnp.zeros((DK_A, DK_A), F32)
            for cc in range(n_chunk):
                c = blk * n_chunk + (cc if d == 0 else n_chunk - 1 - cc)
                st_ref[d, c] = st.astype(BF16)
                dec = jnp.concatenate([dec_ref[d, c]] * (DK_A // SUBLANES), axis=0)
                st = st * dec + kv_ref[d, c]
            sfin_ref[blk, d, 0] = st.T
        return 0

    if per_block:
        lax.fori_loop(0, n_blk, block_states, 0)
    else:
        st0 = (s0_ref[0, 0, 0].T, s0_ref[0, 1, 0].T)
        lax.fori_loop(0, n_all // unroll, states, st0)

    lane_chunk = lax.shift_right_logical(lax.broadcasted_iota(jnp.int32, (DK_A, R), 1), C.bit_length() - 1)

    def finish(i, _):
        slabs = []
        for u in range(finish_blocks):
            blk = i * finish_blocks + u
            rows = pl.ds(pl.multiple_of(blk * R, R), R)
            inter_t = None
            for d in range(2):
                sts = st_ref[d, pl.ds(blk * n_chunk, n_chunk)].reshape(n_chunk * DK_A, DK_A)
                res = lax.dot_general(sts, qd_ref[d, rows, :], NT_DIMS, preferred_element_type=F32)
                picked = res[0:DK_A]
                for c in range(1, n_chunk):
                    picked = jnp.where(lane_chunk == c, res[c * DK_A:(c + 1) * DK_A], picked)
                inter_t = picked if inter_t is None else inter_t + picked
            slabs.append((rows, inter_t))
        for rows, inter_t in slabs:
            tot = acc_ref[rows, :] + inter_t.T
            var = jnp.mean(tot * tot, axis=-1, keepdims=True)
            y = tot * lax.rsqrt(var + EPS) * gain
            o_ref[0, rows, :] = (y * _silu(g_ref[0, rows, :])).astype(o_ref.dtype)
        return 0

    lax.fori_loop(0, n_blk // finish_blocks, finish, 0)


def _hgrn(ya, lgt, layer, gain, s0):
    B, T, width = ya.shape
    per_block = s0 is None
    if per_block:
        assert T == HGRN_ROWS
        group = 8 if B % 8 == 0 else 1
        n_seq, B, T = B, B // group, group * T
        ya = ya.reshape(B, T, width)

    def col(k):
        return pl.BlockSpec((1, T, LANES), lambda b, h, k=k: (b, 0, k * H_A + h))

    in_specs = [col(0), col(1), col(2), col(3), col(4),
                pl.BlockSpec((2, lgt.shape[1], LANES), lambda b, h: (0, 0, h)),
                pl.BlockSpec((1, 1, LANES), lambda b, h: (h, 0, 0))]
    args = [ya, ya, ya, ya, ya, lgt, gain]
    out_shape = [jax.ShapeDtypeStruct((B, T, H_A * DK_A), BF16)]
    out_specs = [pl.BlockSpec((1, T, LANES), lambda b, h: (b, 0, h))]
    if per_block:
        out_shape.append(jax.ShapeDtypeStruct((n_seq, 2, H_A, DK_A, DK_A), F32))
        out_specs.append(pl.BlockSpec((T // HGRN_ROWS, 2, 1, DK_A, DK_A), lambda b, h: (b, 0, h, 0, 0)))
    else:
        in_specs.append(pl.BlockSpec((1, 2, 1, DK_A, DK_A), lambda b, h: (b, 0, h, 0, 0)))
        args.append(s0)
    res = pl.pallas_call(
        functools.partial(_hgrn_kernel, seq=T, layer=layer, per_block=per_block),
        out_shape=out_shape,
        grid=(B, H_A),
        in_specs=in_specs,
        out_specs=out_specs,
        scratch_shapes=[pltpu.VMEM((T, LANES), F32),
                        pltpu.VMEM((2, T, LANES), BF16),
                        pltpu.VMEM((2, T, LANES), BF16),
                        pltpu.VMEM((2, T, LANES), BF16),
                        pltpu.VMEM((2, T // HGRN_CHUNK, DK_A, DK_A), F32),
                        pltpu.VMEM((2, T // HGRN_CHUNK, DK_A, DK_A), BF16),
                        pltpu.VMEM((2, T // HGRN_CHUNK, SUBLANES, LANES), F32),
                        pltpu.VMEM((2, HGRN_ROWS, HGRN_ROWS), BF16),
                        pltpu.VMEM((2, HGRN_ROWS, HGRN_ROWS), F32),
                        pltpu.VMEM((HGRN_ROWS, HGRN_ROWS // HGRN_CHUNK * LANES), BF16)],
        compiler_params=_cparams(2),
        name="hgrn2",
    )(*args)
    if per_block:
        return res[0].reshape(n_seq, HGRN_ROWS, H_A * DK_A), res[1]
    return res[0], None


def _head_masks():
    lane = lax.broadcasted_iota(jnp.int32, (1, LANES), 1)
    return lane < DH_B, lane >= DH_B


def _ctx_attn_kernel(q_ref, k_ref, v_ref, g_ref, kv_ref, o_ref, newk_ref, newv_ref):
    scale = DH_B ** -0.5
    masks = _head_masks()
    T = q_ref.shape[1]
    for h in range(H_B):
        newk_ref[0, h] = kv_ref[0, :, h * DH_B:(h + 1) * DH_B]
        newv_ref[0, h] = kv_ref[0, :, (H_B + h) * DH_B:(H_B + h + 1) * DH_B]
    for p in range(H_B // 2):
        cols = slice(p * LANES, (p + 1) * LANES)
        q = q_ref[0, :, cols] * scale
        qs = jnp.concatenate([jnp.where(masks[h], q, jnp.zeros_like(q)) for h in range(2)], axis=0)
        s = lax.dot_general(qs, k_ref[0, :, cols], NT_DIMS, preferred_element_type=F32)
        e = jnp.exp(s - jnp.max(s, axis=-1, keepdims=True))
        pr = e / jnp.sum(e, axis=-1, keepdims=True)
        o = jnp.dot(pr.astype(BF16), v_ref[0, :, cols], preferred_element_type=F32)
        o = jnp.where(masks[0], o[0:T], o[T:2 * T])
        o_ref[0, :, cols] = (o * _silu(g_ref[0, :, cols].astype(F32))).astype(o_ref.dtype)


def _ctx_attn(yb, kv):
    B, T, _ = yb.shape
    width = H_B * DH_B

    def col(k):
        return pl.BlockSpec((1, T, width), lambda b, k=k: (b, 0, k))

    cache = pl.BlockSpec((1, H_B, T, DH_B), lambda b: (b, 0, 0, 0))
    return pl.pallas_call(
        _ctx_attn_kernel,
        out_shape=[jax.ShapeDtypeStruct((B, T, width), BF16),
                   jax.ShapeDtypeStruct((B, H_B, T, DH_B), F32),
                   jax.ShapeDtypeStruct((B, H_B, T, DH_B), F32)],
        grid=(B,),
        in_specs=[col(0), col(1), col(2), col(3), pl.BlockSpec((1, T, 2 * width), lambda b: (b, 0, 0))],
        out_specs=[pl.BlockSpec((1, T, width), lambda b: (b, 0, 0)), cache, cache],
        compiler_params=_cparams(1),
        name="ctx_attn",
    )(yb, yb, yb, yb, kv)


N_DR = 2 * NA_KH - 1
N_DC = 2 * NA_KW - 1
N_TAB = N_DR - 1


def _nat_kernel(rb_ref, q_ref, k_ref, v_ref, g_ref, kc_ref, vc_ref, o_ref,
                tab_ref, qs_ref, s_ref, p_ref, r_ref, *, rows):
    scale = DH_B ** -0.5
    kh = min(NA_KH, rows)
    masks = _head_masks()

    @pl.when(pl.program_id(1) == 0)
    def _build_tables():
        c = lax.broadcasted_iota(jnp.int32, (GRID_W, LANES), 0)
        lane = lax.broadcasted_iota(jnp.int32, (GRID_W, LANES), 1)
        kcol = lane & (GRID_W - 1)
        ws = jnp.clip(c - NA_KW // 2, 0, GRID_W - NA_KW)
        neg = jnp.full((GRID_W, LANES), NEG_INF, F32)
        inside = jnp.where(kcol >= ws, jnp.where(kcol < ws + NA_KW, 1.0, 0.0), 0.0) > 0.5
        for h in range(2):
            for i in range(N_TAB):
                row = jnp.broadcast_to(rb_ref[h, i:i + 1, :], (GRID_W, LANES))
                toeplitz = pltpu.roll(row, LANES - (NA_KW - 1), 1, stride=1, stride_axis=0)
                tab_ref[h, i] = jnp.where(inside, toeplitz, neg)

    kc = jnp.concatenate([kc_ref[0, 0], kc_ref[0, 1]], axis=1).astype(BF16)
    vc = jnp.concatenate([vc_ref[0, 0], vc_ref[0, 1]], axis=1).astype(BF16)
    n_keys = kh * GRID_W
    n_ctx = kc.shape[0]
    G = NA_GROUP
    W2 = 2 * GRID_W

    def group(gi, _):
        r_first = gi * G
        q0 = pl.multiple_of(r_first * GRID_W, G * GRID_W)
        for i in range(G):
            qi = q_ref[0, pl.ds(q0 + i * GRID_W, GRID_W), :] * scale
            for h in range(2):
                qs_ref[i * W2 + h * GRID_W:i * W2 + (h + 1) * GRID_W, :] = jnp.where(
                    masks[h], qi, jnp.zeros_like(qi))
        s_ref[:, n_keys:n_keys + n_ctx] = lax.dot_general(qs_ref[...], kc, NT_DIMS,
                                                          preferred_element_type=F32)
        windows = {}

        def local_scores(i):
            r = r_first + i
            rs = jnp.clip(r - kh // 2, 0, rows - kh)
            k0 = pl.multiple_of(rs * GRID_W, GRID_W)
            windows[i] = k0
            dr0 = rs - r + (NA_KH - 1)
            bias = jnp.concatenate(
                [jnp.concatenate([tab_ref[h, dr0 + 2 * m] for m in range(kh // 2)], axis=1)
                 for h in range(2)], axis=0)
            s_ref[i * W2:(i + 1) * W2, 0:n_keys] = lax.dot_general(
                qs_ref[i * W2:(i + 1) * W2, :], k_ref[0, pl.ds(k0, n_keys), :], NT_DIMS,
                preferred_element_type=F32) + bias

        def numerators(i):
            s = s_ref[i * W2:(i + 1) * W2, :]
            e = jnp.exp(s - jnp.max(s, axis=-1, keepdims=True))
            p_ref[i * W2:(i + 1) * W2, :] = e.astype(BF16)
            rinv = 1.0 / jnp.sum(e, axis=-1, keepdims=True)
            r_ref[i * W2:(i + 1) * W2, :] = jnp.broadcast_to(rinv, (W2, LANES))

        def weighted_values(i):
            vals = jnp.concatenate([v_ref[0, pl.ds(windows[i], n_keys), :], vc], axis=0)
            o = jnp.dot(p_ref[i * W2:(i + 1) * W2, :], vals, preferred_element_type=F32)
            o = o * r_ref[i * W2:(i + 1) * W2, :]
            o = jnp.where(masks[0], o[0:GRID_W], o[GRID_W:W2])
            out_rows = pl.ds(q0 + i * GRID_W, GRID_W)
            gate = g_ref[0, out_rows, :].astype(F32)
            o_ref[0, out_rows, :] = (o * _silu(gate)).astype(o_ref.dtype)

        for step in range(G + 2):
            if step < G:
                local_scores(step)
            if 0 <= step - 1 < G:
                numerators(step - 1)
            if 0 <= step - 2 < G:
                weighted_values(step - 2)
        return 0

    lax.fori_loop(0, rows // G, group, 0)


def _nat(yb, kc, vc, rel_bias):
    B, T, _ = yb.shape
    Tc = kc.shape[2]
    n_pair = H_B // 2
    rows = T // GRID_W
    n_stack = NA_GROUP * 2 * GRID_W
    n_keys = min(NA_KH, rows) * GRID_W
    pad = jnp.zeros((H_B, N_TAB, GRID_W - N_DC), F32)
    rel = rel_bias.astype(F32)
    rb_rows = jnp.concatenate([rel[:, 0:N_TAB], pad, rel[:, 1:N_TAB + 1], pad], axis=-1)

    def col(k):
        return pl.BlockSpec((1, T, LANES), lambda p, b, k=k: (b, 0, k * n_pair + p))

    ctx = pl.BlockSpec((1, 2, Tc, DH_B), lambda p, b: (b, p, 0, 0))
    return pl.pallas_call(
        functools.partial(_nat_kernel, rows=rows),
        out_shape=jax.ShapeDtypeStruct((B, T, H_B * DH_B), BF16),
        grid=(n_pair, B),
        in_specs=[pl.BlockSpec((2, N_TAB, LANES), lambda p, b: (p, 0, 0)),
                  col(0), col(1), col(2), col(3), ctx, ctx],
        out_specs=pl.BlockSpec((1, T, LANES), lambda p, b: (b, 0, p)),
        scratch_shapes=[pltpu.VMEM((2, N_TAB, GRID_W, LANES), F32),
                        pltpu.VMEM((n_stack, LANES), BF16),
                        pltpu.VMEM((n_stack, n_keys + Tc), F32),
                        pltpu.VMEM((n_stack, n_keys + Tc), BF16),
                        pltpu.VMEM((n_stack, LANES), F32)],
        compiler_params=_cparams(2),
        name="nbr_attn",
    )(rb_rows, yb, yb, yb, yb, kc, vc)


def _seg_len(seq):
    length = -(-seq // N_SEG)
    while length % 8 != 4:
        length += 1
    return length


def _step_block(seg_len):
    return max(d for d in range(1, seg_len + 1) if seg_len % d == 0 and d * N_SEG <= RG_ROWS * 9)


def _rglru_kernel(x_ref, g_ref, cw_ref, cb_ref, wg_ref, lam_ref, *rest, seq, slabs, has_s0, emit_state):
    rest = list(rest)
    s0_ref = rest.pop(0) if has_s0 else None
    o_ref = rest.pop(0)
    hfin_ref = rest.pop(0) if emit_state else None
    xpad_ref, a_ref, u_ref, h_ref, p_ref, hn_ref = rest
    L = _seg_len(seq)
    n_rows = N_SEG * L
    RB = RG_ROWS
    CB = slabs
    TB = _step_block(L)
    n_tile = N_SEG // SUBLANES
    lead = SUBLANES
    chains = [(d, j, s) for d in range(2) for j in range(CB) for s in range(n_tile)]

    for j in range(CB):
        xpad_ref[j, 0:lead, :] = jnp.zeros((lead, LANES), F32)
        xpad_ref[j, lead:seq + lead, :] = x_ref[0, :, j * LANES:(j + 1) * LANES]
        xpad_ref[j, seq + lead:n_rows + 2 * lead, :] = jnp.zeros((n_rows + lead - seq, LANES), F32)

    nl = -lam_ref[...]
    sp = jnp.maximum(nl, 0.0) + jnp.log1p(jnp.exp(-jnp.abs(nl)))
    cw = cw_ref[...]
    cbias = cb_ref[...]
    ones2 = jnp.where(lax.broadcasted_iota(jnp.int32, (TB * N_SEG, LANES), 1) < 2, 1.0, 0.0).astype(BF16)

    def gates(blk, _):
        t0 = blk * TB
        r0 = pl.multiple_of(blk * (TB * N_SEG), TB * N_SEG)
        for j in range(CB):
            lanes = slice(j * LANES, (j + 1) * LANES)
            tiles = []
            for tt in range(TB):
                for s in range(n_tile):
                    first = lead - 2 + t0 + tt + s * SUBLANES * L
                    taps = [xpad_ref[j, pl.ds(first + k, SUBLANES, stride=L), :] for k in range(4)]
                    xt = cw[0:1, lanes] * taps[0] + cw[1:2, lanes] * taps[1]
                    xt = xt + cw[2:3, lanes] * taps[2]
                    tiles.append(xt + cw[3:4, lanes] * taps[3] + cbias[:, lanes])
            xj = jnp.concatenate(tiles, axis=0)
            gt = jnp.dot(jnp.concatenate([xj.astype(BF16), ones2], axis=1), wg_ref[j],
                         preferred_element_type=F32)
            xh = 0.5 * xj
            for d in range(2):
                th_r = jnp.tanh(gt[:, (2 * d) * LANES:(2 * d + 1) * LANES])
                th_i = jnp.tanh(gt[:, (2 * d + 1) * LANES:(2 * d + 2) * LANES])
                half = (-0.5 * RG_C) * sp[d:d + 1, lanes]
                la = half + half * th_r
                a = jnp.exp(la)
                y = -jnp.tanh(la) * (1.0 + a * a)
                root = jnp.where(y > 0.0, y * lax.rsqrt(y), 0.0)
                a_ref[d, j, pl.ds(r0, TB * N_SEG), :] = a
                u_ref[d, j, pl.ds(r0, TB * N_SEG), :] = root * (xh + xh * th_i)
        return 0

    lax.fori_loop(0, L // TB, gates, 0)

    first_pad = [[min(max(seq - (s * SUBLANES + r) * L, 0), L) for r in range(SUBLANES)]
                 for s in range(n_tile)]
    sub = lax.broadcasted_iota(jnp.int32, (SUBLANES, LANES), 0)
    pad_from = []
    for s in range(n_tile):
        if all(f == L for f in first_pad[s]):
            pad_from.append(None)
        else:
            vec = jnp.full((SUBLANES, LANES), L, jnp.int32)
            for r in range(SUBLANES):
                vec = jnp.where(sub == r, first_pad[s][r], vec)
            pad_from.append(vec)

    def step_rows(step, s):
        return pl.ds(pl.multiple_of(step * N_SEG + s * SUBLANES, SUBLANES), SUBLANES)

    unroll = 4

    def scan(i, carry):
        carry = list(carry)
        for k in range(unroll):
            for n, (d, j, s) in enumerate(chains):
                h, pr = carry[n]
                t = i * unroll + k
                if d == 1:
                    t = L - 1 - t
                idx = step_rows(t, s)
                a = a_ref[d, j, idx, :]
                u = u_ref[d, j, idx, :]
                if pad_from[s] is not None:
                    live = t < pad_from[s]
                    a = jnp.where(live, a, 1.0)
                    u = jnp.where(live, u, 0.0)
                h = a * h + u
                pr = pr * a
                h_ref[d, j, idx, :] = h
                p_ref[d, j, idx, :] = pr
                carry[n] = (h, pr)
        return tuple(carry)

    zero = jnp.zeros((SUBLANES, LANES), F32)
    one = jnp.ones((SUBLANES, LANES), F32)
    ends = lax.fori_loop(0, L // unroll, scan, ((zero, one),) * len(chains))

    cins = {}
    finals = [[None] * CB for _ in range(2)]
    for d in range(2):
        for j in range(CB):
            if has_s0:
                c = s0_ref[0, d:d + 1, j * LANES:(j + 1) * LANES]
            else:
                c = jnp.zeros((1, LANES), F32)
            cin = [None] * N_SEG
            for kk in range(N_SEG):
                seg = kk if d == 0 else N_SEG - 1 - kk
                s, row = divmod(seg, SUBLANES)
                h_end, p_end = ends[chains.index((d, j, s))]
                cin[seg] = c
                c = h_end[row:row + 1, :] + p_end[row:row + 1, :] * c
            finals[d][j] = c
            for s in range(n_tile):
                cins[(d, j, s)] = jnp.concatenate(cin[s * SUBLANES:(s + 1) * SUBLANES], axis=0)

    def fix(i, _):
        for k in range(unroll):
            t = i * unroll + k
            for j in range(CB):
                for s in range(n_tile):
                    idx = step_rows(t, s)
                    parts = [h_ref[d, j, idx, :] + p_ref[d, j, idx, :] * cins[(d, j, s)] for d in range(2)]
                    hn_ref[j, pl.ds(t + s * SUBLANES * L, SUBLANES, stride=L), :] = parts[0] + parts[1]
        return 0

    lax.fori_loop(0, L // unroll, fix, 0)

    if emit_state:
        hfin_ref[0] = jnp.concatenate([jnp.concatenate(finals[d], axis=1) for d in range(2)], axis=0)

    def combine(blk, _):
        rows = pl.ds(pl.multiple_of(blk * RB, RB), RB)
        hs = jnp.concatenate([hn_ref[j, rows, :] for j in range(CB)], axis=1)
        o_ref[0, rows, :] = (hs * _silu(g_ref[0, rows, :])).astype(o_ref.dtype)
        return 0

    lax.fori_loop(0, seq // RB, combine, 0)


def _gate_weights(gate_w, gate_b):
    w = (0.5 * gate_w).transpose(2, 3, 0, 1, 4).reshape(H_C, BW_C, 4 * BW_C).astype(BF16)
    b = (0.5 * gate_b).reshape(2, 2, H_C, BW_C).transpose(2, 0, 1, 3).reshape(H_C, 1, 4 * BW_C).astype(F32)
    hi = b.astype(BF16)
    lo = (b - hi.astype(F32)).astype(BF16)
    zeros = jnp.zeros((H_C, BW_C - 2, 4 * BW_C), BF16)
    return jnp.concatenate([w, hi, lo, zeros], axis=1)


def _rglru(xg, conv_w, conv_b, wg, lam, s0, emit_state):
    B, T, _ = xg.shape
    has_s0 = s0 is not None
    n_rows = N_SEG * _seg_len(T)
    CB = RG_SLABS if T > 4 * RG_ROWS else 2 * RG_SLABS
    wide = CB * LANES
    n_steps = H_C // CB
    in_specs = [
        pl.BlockSpec((1, T, wide), lambda b, c: (b, 0, c)),
        pl.BlockSpec((1, T, wide), lambda b, c: (b, 0, n_steps + c)),
        pl.BlockSpec((4, wide), lambda b, c: (0, c)),
        pl.BlockSpec((1, wide), lambda b, c: (0, c)),
        pl.BlockSpec((CB, 2 * BW_C, 4 * BW_C), lambda b, c: (c, 0, 0)),
        pl.BlockSpec((2, wide), lambda b, c: (0, c)),
    ]
    args = [xg, xg, conv_w, conv_b, wg, lam]
    if has_s0:
        in_specs.append(pl.BlockSpec((1, 2, wide), lambda b, c: (b, 0, c)))
        args.append(s0)
    out_shape = [jax.ShapeDtypeStruct((B, T, W_C), BF16)]
    out_specs = [pl.BlockSpec((1, T, wide), lambda b, c: (b, 0, c))]
    if emit_state:
        out_shape.append(jax.ShapeDtypeStruct((B, 2, W_C), F32))
        out_specs.append(pl.BlockSpec((1, 2, wide), lambda b, c: (b, 0, c)))
    res = pl.pallas_call(
        functools.partial(_rglru_kernel, seq=T, slabs=CB, has_s0=has_s0, emit_state=emit_state),
        out_shape=out_shape,
        grid=(B, n_steps),
        in_specs=in_specs,
        out_specs=out_specs,
        scratch_shapes=[pltpu.VMEM((CB, n_rows + 2 * SUBLANES, LANES), F32)]
        + [pltpu.VMEM((2, CB, n_rows, LANES), F32)] * 4 + [pltpu.VMEM((CB, n_rows, LANES), F32)],
        compiler_params=_cparams(2),
        name="rglru",
    )(*args)
    return res if emit_state else (res[0], None)


A_COLS = 5 * H_A * DK_A
B_COLS = 4 * H_B * DH_B


def kernel(x_prompt, x_sample, state_hgrn, cache_na_k, cache_na_v, state_rglru, c, c_ctx, norm_gain, w_mod, b_mod, w_in_even, w_out_even, hgrn_lb_logits, hgrn_out_gain, na_rel_bias, w_in_odd, w_out_odd, conv_w, conv_b, rg_gate_w, rg_gate_b, rg_lambda, final_gain):
    n_ctx = x_prompt.shape[0]
    n_lat = x_sample.shape[0]
    depth = w_mod.shape[0]

    cond = jnp.zeros((16, D_MODEL), F32).at[0].set(c_ctx).at[1:1 + n_lat].set(c)
    mod = _modulation(cond, w_mod, b_mod.reshape(depth, 1, 3 * D_MODEL))
    mod = mod.reshape(depth, 16, 3, D_MODEL)

    t_ctx = x_prompt.shape[1]

    def flat(a):
        return a.reshape(1, n_ctx * t_ctx, a.shape[-1])

    def unflat(a):
        return a.reshape(n_ctx, t_ctx, a.shape[-1])

    def in_proj_params(l):
        if l % 2 == 0:
            outs_s = ((0, A_COLS, F32), (A_COLS, B_COLS, BF16))
            outs_c = outs_s + ((A_COLS + H_B * DH_B, 2 * H_B * DH_B, F32),)
            a_key = H_A * DK_A
            col = jnp.arange(w_in_even.shape[-1])
            halve = jnp.where((col >= a_key) & (col < 3 * a_key), 0.5, 1.0).astype(F32)
            return (w_in_even[l // 2] * halve).astype(BF16), outs_c, outs_s
        outs = ((0, 2 * W_C, F32),)
        return w_in_odd[l // 2].astype(BF16), outs, outs

    xc, xs = x_prompt, x_sample
    new_hgrn, new_k, new_v, new_rg = [], [], [], []
    proj_c = proj_s = None
    for l in range(depth):
        j = l // 2
        mod_c, mod_s = mod[l, 0:1], mod[l, 1:1 + n_lat]
        if proj_c is None:
            gain = norm_gain[l].reshape(1, D_MODEL)
            w_in, outs_c, outs_s = in_proj_params(l)
            proj_c = [unflat(t) for t in _inproj(flat(xc), mod_c, gain, w_in, outs_c, 512, True)]
            proj_s = _inproj(xs, mod_s, gain, w_in, outs_s, 512, False)
        if l % 2 == 0:
            w_out = w_out_even[j].astype(BF16)
            (ya_c, yb_c, kv_c), (ya_s, yb_s) = proj_c, proj_s
            hgain = hgrn_out_gain[j].reshape(H_A, 1, DK_A)
            oa_c, s_fin = _hgrn(ya_c, hgrn_lb_logits, j, hgain, None)
            oa_s, _ = _hgrn(ya_s, hgrn_lb_logits, j, hgain, state_hgrn[:, j])
            ob_c, k_c, v_c = _ctx_attn(yb_c, kv_c)
            ob_s = _nat(yb_s, cache_na_k[:, j], cache_na_v[:, j], na_rel_bias[j])
            ys_c, ys_s = (oa_c, ob_c), (oa_s, ob_s)
            new_hgrn.append(s_fin)
            new_k.append(k_c)
            new_v.append(v_c)
        else:
            w_out = w_out_odd[j].astype(BF16)
            (xg_c,), (xg_s,) = proj_c, proj_s
            wg = _gate_weights(rg_gate_w[j], rg_gate_b[j])
            cb = conv_b[j].reshape(1, W_C)
            y_c, h_fin = _rglru(xg_c, conv_w[j], cb, wg, rg_lambda[j], None, True)
            y_s, _ = _rglru(xg_s, conv_w[j], cb, wg, rg_lambda[j], state_rglru[:, j], False)
            ys_c, ys_s = (y_c,), (y_s,)
            new_rg.append(h_fin)
        ys_c = tuple(flat(y) for y in ys_c)
        if l == depth - 1:
            fgain = final_gain.reshape(1, D_MODEL)
            (xc,) = _outproj(ys_c, flat(xc), mod_c, w_out, 1024, True, final_gain=fgain)
            (xs,) = _outproj(ys_s, xs, mod_s, w_out, 1024, False, final_gain=fgain)
            xc = unflat(xc)
        else:
            gain_n = norm_gain[l + 1].reshape(1, D_MODEL)
            w_n, outs_c, outs_s = in_proj_params(l + 1)
            mod_cn, mod_sn = mod[l + 1, 0:1], mod[l + 1, 1:1 + n_lat]
            xc, *proj_c = _outproj(ys_c, flat(xc), mod_c, w_out, 512, True,
                                   next_proj=(mod_cn, gain_n, w_n, outs_c))
            xs, *proj_s = _outproj(ys_s, xs, mod_s, w_out, 512, False,
                                   next_proj=(mod_sn, gain_n, w_n, outs_s))
            xc = unflat(xc)
            proj_c = [unflat(t) for t in proj_c]
    return (xc, xs, jnp.stack(new_hgrn, axis=1), jnp.stack(new_k, axis=1),
            jnp.stack(new_v, axis=1), jnp.stack(new_rg, axis=1))
```

```python
import functools

import jax
import jax.numpy as jnp
from jax import lax
from jax.experimental import pallas as pl
from jax.experimental.pallas import tpu as pltpu

F32 = jnp.float32
BF16 = jnp.bfloat16

D_MODEL = 1024
EPS = 1e-6
NEG_INF = -1e30
LOG2E = 1.4426950408889634
H_A = 4
DK_A = 128
HGRN_CHUNK = 32
HGRN_ROWS = 256
H_B = 8
DH_B = 64
GRID_W = 64
NA_KH = 8
NA_KW = 16
NA_GROUP = 32
W_C = 1024
H_C = 8
BW_C = W_C // H_C
RG_C = 8.0
RG_ROWS = 256
RG_SLABS = 2
N_SEG = 16
LANES = 128
SUBLANES = 8
VMEM_LIMIT = 48 * 1024 * 1024

NT_DIMS = (((1,), (1,)), ((), ()))


def _silu(x):
    half = 0.5 * x
    return half + half * jnp.tanh(half)


def _cparams(n_axes):
    return pltpu.CompilerParams(dimension_semantics=("arbitrary",) * n_axes,
                                vmem_limit_bytes=VMEM_LIMIT)


def _mod_kernel(cond_ref, w_ref, b_ref, o_ref):
    s = _silu(cond_ref[...])
    o_ref[0] = jnp.dot(s.astype(BF16), w_ref[0].astype(BF16), preferred_element_type=F32) + b_ref[0]


def _modulation(cond, w_mod, b_mod):
    depth = w_mod.shape[0]
    n_rows = cond.shape[0]
    return pl.pallas_call(
        _mod_kernel,
        out_shape=jax.ShapeDtypeStruct((depth, n_rows, 3 * D_MODEL), F32),
        grid=(depth, 3),
        in_specs=[
            pl.BlockSpec((n_rows, D_MODEL), lambda l, n: (0, 0)),
            pl.BlockSpec((1, D_MODEL, D_MODEL), lambda l, n: (l, 0, n)),
            pl.BlockSpec((1, 1, D_MODEL), lambda l, n: (l, 0, n)),
        ],
        out_specs=pl.BlockSpec((1, n_rows, D_MODEL), lambda l, n: (l, 0, n)),
        compiler_params=_cparams(2),
        name="adaln_mod",
    )(cond, w_mod, b_mod)


def _project(x, mod_ref, gain_ref, w_ref, out_refs, outs):
    var = jnp.mean(x * x, axis=-1, keepdims=True)
    y = x * lax.rsqrt(var + EPS) * gain_ref[...]
    h = y * (1.0 + mod_ref[0, 1:2, :]) + mod_ref[0, 0:1, :]
    hb = h.astype(BF16)
    step = 512
    for c in range(0, w_ref.shape[1], step):
        users = [(o_ref, c - col0) for o_ref, (col0, width, _) in zip(out_refs, outs)
                 if col0 <= c < col0 + width]
        if users:
            r = jnp.dot(hb, w_ref[:, c:c + step], preferred_element_type=F32)
            for o_ref, off in users:
                o_ref[0, :, off:off + step] = r.astype(o_ref.dtype)


def _inproj_kernel(x_ref, mod_ref, gain_ref, w_ref, *out_refs, outs):
    _project(x_ref[0], mod_ref, gain_ref, w_ref, out_refs, outs)


def _inproj(x, mod, gain, w, outs, tm, shared_mod):
    B, T, _ = x.shape
    n_cols = w.shape[1]
    mod_map = (lambda b, t: (0, 0, 0)) if shared_mod else (lambda b, t: (b, 0, 0))
    return pl.pallas_call(
        functools.partial(_inproj_kernel, outs=outs),
        out_shape=[jax.ShapeDtypeStruct((B, T, wd), dt) for _, wd, dt in outs],
        grid=(B, T // tm),
        in_specs=[
            pl.BlockSpec((1, tm, D_MODEL), lambda b, t: (b, t, 0)),
            pl.BlockSpec((1, 3, D_MODEL), mod_map),
            pl.BlockSpec((1, D_MODEL), lambda b, t: (0, 0)),
            pl.BlockSpec((D_MODEL, n_cols), lambda b, t: (0, 0)),
        ],
        out_specs=[pl.BlockSpec((1, tm, wd), lambda b, t: (b, t, 0)) for _, wd, _ in outs],
        compiler_params=_cparams(2),
        name="in_proj",
    )(x, mod, gain, w)


def _outproj_kernel(*refs, n_y, final, next_outs):
    y_refs, (x_ref, mod_ref, w_ref), rest = refs[:n_y], refs[n_y:n_y + 3], refs[n_y + 3:]
    m = None
    row = 0
    for y_ref in y_refs:
        width = y_ref.shape[-1]
        part = jnp.dot(y_ref[0], w_ref[row:row + width, :], preferred_element_type=F32)
        m = part if m is None else m + part
        row += width
    xn = x_ref[0] + mod_ref[0, 2:3, :] * m
    if final:
        gain_ref, o_ref = rest
        var = jnp.mean(xn * xn, axis=-1, keepdims=True)
        o_ref[0] = xn * lax.rsqrt(var + EPS) * gain_ref[...]
    else:
        modn_ref, gainn_ref, wn_ref, o_ref = rest[:4]
        o_ref[0] = xn
        _project(xn, modn_ref, gainn_ref, wn_ref, rest[4:], next_outs)


def _outproj(ys, x, mod, w, tm, shared_mod, final_gain=None, next_proj=None):
    B, T, _ = x.shape
    final = final_gain is not None
    mod_map = (lambda b, t: (0, 0, 0)) if shared_mod else (lambda b, t: (b, 0, 0))
    row_block = pl.BlockSpec((1, tm, D_MODEL), lambda b, t: (b, t, 0))
    vec = pl.BlockSpec((1, D_MODEL), lambda b, t: (0, 0))
    in_specs = [pl.BlockSpec((1, tm, y.shape[-1]), lambda b, t: (b, t, 0)) for y in ys] + [
        row_block,
        pl.BlockSpec((1, 3, D_MODEL), mod_map),
        pl.BlockSpec((w.shape[0], D_MODEL), lambda b, t: (0, 0)),
    ]
    args = list(ys) + [x, mod, w]
    out_shape = [jax.ShapeDtypeStruct((B, T, D_MODEL), F32)]
    out_specs = [row_block]
    next_outs = None
    if final:
        in_specs.append(vec)
        args.append(final_gain)
    else:
        mod_n, gain_n, w_n, next_outs = next_proj
        in_specs += [pl.BlockSpec((1, 3, D_MODEL), mod_map), vec,
                     pl.BlockSpec((D_MODEL, w_n.shape[1]), lambda b, t: (0, 0))]
        args += [mod_n, gain_n, w_n]
        out_shape += [jax.ShapeDtypeStruct((B, T, wd), dt) for _, wd, dt in next_outs]
        out_specs += [pl.BlockSpec((1, tm, wd), lambda b, t: (b, t, 0)) for _, wd, _ in next_outs]
    return pl.pallas_call(
        functools.partial(_outproj_kernel, n_y=len(ys), final=final, next_outs=next_outs),
        out_shape=out_shape,
        grid=(B, T // tm),
        in_specs=in_specs,
        out_specs=out_specs,
        compiler_params=_cparams(2),
        name="out_proj",
    )(*args)


def _hgrn_kernel(q_ref, zf_ref, zb_ref, v_ref, g_ref, lgt_ref, gain_ref, *rest, seq, layer, per_block):
    rest = list(rest)
    s0_ref = None if per_block else rest.pop(0)
    o_ref = rest.pop(0)
    sfin_ref = rest.pop(0) if per_block else None
    acc_ref, qd_ref, ki_ref, kd_ref, kv_ref, st_ref, dec_ref, mst_ref, msk_ref, mexp_ref = rest
    R = HGRN_ROWS
    C = HGRN_CHUNK
    n_blk = seq // R
    n_chunk = R // C
    n_all = seq // C

    @pl.when((pl.program_id(0) == 0) & (pl.program_id(1) == 0))
    def _build_masks():
        ti = lax.broadcasted_iota(jnp.int32, (R, R), 0)
        tj = lax.broadcasted_iota(jnp.int32, (R, R), 1)
        shift = C.bit_length() - 1
        same = lax.shift_right_logical(ti, shift) == lax.shift_right_logical(tj, shift)
        one = jnp.ones((R, R), F32)
        zero = jnp.zeros((R, R), F32)
        incl = (jnp.where(same, jnp.where(tj <= ti, one, zero), zero),
                jnp.where(same, jnp.where(tj >= ti, one, zero), zero))
        for d in range(2):
            msk_ref[d] = incl[d]
            mst_ref[d] = incl[d].astype(BF16)
        rr = lax.broadcasted_iota(jnp.int32, (R, n_chunk * LANES), 0)
        cc = lax.broadcasted_iota(jnp.int32, (R, n_chunk * LANES), 1)
        own = lax.shift_right_logical(rr, shift) == lax.shift_right_logical(cc, LANES.bit_length() - 1)
        mexp_ref[...] = jnp.where(own, 1.0, 0.0).astype(BF16)

    lgt = [lgt_ref[:, i, :] for i in range(lgt_ref.shape[1])]
    lmax = functools.reduce(jnp.maximum, lgt)
    ex = [jnp.exp(t - lmax) for t in lgt]
    lb_all = sum(ex[:layer + 1]) / sum(ex)
    gain = gain_ref[0]

    blocks_per_trip = 2 if n_blk % 2 == 0 else 1
    finish_blocks = 8 if n_blk % 8 == 0 else blocks_per_trip

    def gates(i, _):
        for u in range(finish_blocks):
            blk = i * finish_blocks + u
            rows = pl.ds(pl.multiple_of(blk * R, R), R)
            q = q_ref[0, rows, :]
            for d in range(2):
                th = jnp.tanh((zf_ref if d == 0 else zb_ref)[0, rows, :])
                lb = lb_all[d:d + 1, :]
                c = 0.5 * (1.0 - lb)
                ct = c * th
                f = (lb + c) + ct
                k = c - ct
                logf = jnp.log(f)
                hi = logf.astype(BF16)
                lo = (logf - hi.astype(F32)).astype(BF16)
                cs = jnp.dot(mst_ref[d], jnp.concatenate([hi, lo], axis=1), preferred_element_type=F32)
                b = cs[:, 0:LANES] + cs[:, LANES:2 * LANES]
                ends = [c * C + (C - 1 if d == 0 else 0) for c in range(n_chunk)]
                btot = jnp.concatenate([jnp.broadcast_to(b[t:t + 1, :], (C, LANES)) for t in ends], axis=0)
                qd_ref[d, rows, :] = (q * jnp.exp(b)).astype(BF16)
                ki_ref[d, rows, :] = (k * jnp.exp2(b * -LOG2E)).astype(BF16)
                kd_ref[d, rows, :] = (k * jnp.exp(btot - b)).astype(BF16)
                for c in range(n_chunk):
                    dec_ref[d, blk * n_chunk + c] = jnp.exp(btot[c * C:c * C + SUBLANES, :])
        return 0

    lax.fori_loop(0, n_blk // finish_blocks, gates, 0)

    def intra(i, _):
        for u in range(finish_blocks):
            blk = i * finish_blocks + u
            rows = pl.ds(pl.multiple_of(blk * R, R), R)
            v = v_ref[0, rows, :]
            vb = v.astype(BF16)
            vt = v.T.astype(BF16)
            att_sum = None
            for d in range(2):
                att = lax.dot_general(qd_ref[d, rows, :], ki_ref[d, rows, :], NT_DIMS,
                                      preferred_element_type=F32)
                att = jnp.where(msk_ref[d] > 0.5, att, 0.0)
                att_sum = att if att_sum is None else att_sum + att
                kd_exp = jnp.concatenate([kd_ref[d, rows, :]] * n_chunk, axis=1) * mexp_ref[...]
                kv_all = jnp.dot(vt, kd_exp, preferred_element_type=F32)
                for c in range(n_chunk):
                    kv_ref[d, blk * n_chunk + c] = kv_all[:, c * LANES:(c + 1) * LANES]
            acc_ref[rows, :] = jnp.dot(att_sum.astype(BF16), vb, preferred_element_type=F32)
        return 0

    lax.fori_loop(0, n_blk // finish_blocks, intra, 0)

    unroll = 4

    def states(i, sts):
        sts = list(sts)
        for u in range(unroll):
            n = i * unroll + u
            for d in range(2):
                c = n if d == 0 else n_all - 1 - n
                st_ref[d, c] = sts[d].astype(BF16)
                dec = jnp.concatenate([dec_ref[d, c]] * (DK_A // SUBLANES), axis=0)
                sts[d] = sts[d] * dec + kv_ref[d, c]
        return tuple(sts)

    def block_states(blk, _):
        for d in range(2):
            st = jnp.zeros((DK_A, DK_A), F32)
            for cc in range(n_chunk):
                c = blk * n_chunk + (cc if d == 0 else n_chunk - 1 - cc)
                st_ref[d, c] = st.astype(BF16)
                dec = jnp.concatenate([dec_ref[d, c]] * (DK_A // SUBLANES), axis=0)
                st = st * dec + kv_ref[d, c]
            sfin_ref[blk, d, 0] = st.T
        return 0

    if per_block:
        lax.fori_loop(0, n_blk, block_states, 0)
    else:
        st0 = (s0_ref[0, 0, 0].T, s0_ref[0, 1, 0].T)
        lax.fori_loop(0, n_all // unroll, states, st0)

    lane_chunk = lax.shift_right_logical(lax.broadcasted_iota(jnp.int32, (DK_A, R), 1), C.bit_length() - 1)

    def finish(i, _):
        slabs = []
        for u in range(finish_blocks):
            blk = i * finish_blocks + u
            rows = pl.ds(pl.multiple_of(blk * R, R), R)
            inter_t = None
            for d in range(2):
                sts = st_ref[d, pl.ds(blk * n_chunk, n_chunk)].reshape(n_chunk * DK_A, DK_A)
                res = lax.dot_general(sts, qd_ref[d, rows, :], NT_DIMS, preferred_element_type=F32)
                picked = res[0:DK_A]
                for c in range(1, n_chunk):
                    picked = jnp.where(lane_chunk == c, res[c * DK_A:(c + 1) * DK_A], picked)
                inter_t = picked if inter_t is None else inter_t + picked
            slabs.append((rows, inter_t))
        for rows, inter_t in slabs:
            tot = acc_ref[rows, :] + inter_t.T
            var = jnp.mean(tot * tot, axis=-1, keepdims=True)
            y = tot * lax.rsqrt(var + EPS) * gain
            o_ref[0, rows, :] = (y * _silu(g_ref[0, rows, :])).astype(o_ref.dtype)
        return 0

    lax.fori_loop(0, n_blk // finish_blocks, finish, 0)


def _hgrn(ya, lgt, layer, gain, s0):
    B, T, width = ya.shape
    per_block = s0 is None
    if per_block:
        assert T == HGRN_ROWS
        group = 8 if B % 8 == 0 else 1
        n_seq, B, T = B, B // group, group * T
        ya = ya.reshape(B, T, width)

    def col(k):
        return pl.BlockSpec((1, T, LANES), lambda b, h, k=k: (b, 0, k * H_A + h))

    in_specs = [col(0), col(1), col(2), col(3), col(4),
                pl.BlockSpec((2, lgt.shape[1], LANES), lambda b, h: (0, 0, h)),
                pl.BlockSpec((1, 1, LANES), lambda b, h: (h, 0, 0))]
    args = [ya, ya, ya, ya, ya, lgt, gain]
    out_shape = [jax.ShapeDtypeStruct((B, T, H_A * DK_A), BF16)]
    out_specs = [pl.BlockSpec((1, T, LANES), lambda b, h: (b, 0, h))]
    if per_block:
        out_shape.append(jax.ShapeDtypeStruct((n_seq, 2, H_A, DK_A, DK_A), F32))
        out_specs.append(pl.BlockSpec((T // HGRN_ROWS, 2, 1, DK_A, DK_A), lambda b, h: (b, 0, h, 0, 0)))
    else:
        in_specs.append(pl.BlockSpec((1, 2, 1, DK_A, DK_A), lambda b, h: (b, 0, h, 0, 0)))
        args.append(s0)
    res = pl.pallas_call(
        functools.partial(_hgrn_kernel, seq=T, layer=layer, per_block=per_block),
        out_shape=out_shape,
        grid=(B, H_A),
        in_specs=in_specs,
        out_specs=out_specs,
        scratch_shapes=[pltpu.VMEM((T, LANES), F32),
                        pltpu.VMEM((2, T, LANES), BF16),
                        pltpu.VMEM((2, T, LANES), BF16),
                        pltpu.VMEM((2, T, LANES), BF16),
                        pltpu.VMEM((2, T // HGRN_CHUNK, DK_A, DK_A), F32),
                        pltpu.VMEM((2, T // HGRN_CHUNK, DK_A, DK_A), BF16),
                        pltpu.VMEM((2, T // HGRN_CHUNK, SUBLANES, LANES), F32),
                        pltpu.VMEM((2, HGRN_ROWS, HGRN_ROWS), BF16),
                        pltpu.VMEM((2, HGRN_ROWS, HGRN_ROWS), F32),
                        pltpu.VMEM((HGRN_ROWS, HGRN_ROWS // HGRN_CHUNK * LANES), BF16)],
        compiler_params=_cparams(2),
        name="hgrn2",
    )(*args)
    if per_block:
        return res[0].reshape(n_seq, HGRN_ROWS, H_A * DK_A), res[1]
    return res[0], None


def _head_masks():
    lane = lax.broadcasted_iota(jnp.int32, (1, LANES), 1)
    return lane < DH_B, lane >= DH_B


def _ctx_attn_kernel(q_ref, k_ref, v_ref, g_ref, kv_ref, o_ref, newk_ref, newv_ref):
    scale = DH_B ** -0.5
    masks = _head_masks()
    T = q_ref.shape[1]
    for h in range(H_B):
        newk_ref[0, h] = kv_ref[0, :, h * DH_B:(h + 1) * DH_B]
        newv_ref[0, h] = kv_ref[0, :, (H_B + h) * DH_B:(H_B + h + 1) * DH_B]
    for p in range(H_B // 2):
        cols = slice(p * LANES, (p + 1) * LANES)
        q = q_ref[0, :, cols] * scale
        qs = jnp.concatenate([jnp.where(masks[h], q, jnp.zeros_like(q)) for h in range(2)], axis=0)
        s = lax.dot_general(qs, k_ref[0, :, cols], NT_DIMS, preferred_element_type=F32)
        e = jnp.exp(s - jnp.max(s, axis=-1, keepdims=True))
        pr = e / jnp.sum(e, axis=-1, keepdims=True)
        o = jnp.dot(pr.astype(BF16), v_ref[0, :, cols], preferred_element_type=F32)
        o = jnp.where(masks[0], o[0:T], o[T:2 * T])
        o_ref[0, :, cols] = (o * _silu(g_ref[0, :, cols].astype(F32))).astype(o_ref.dtype)


def _ctx_attn(yb, kv):
    B, T, _ = yb.shape
    width = H_B * DH_B

    def col(k):
        return pl.BlockSpec((1, T, width), lambda b, k=k: (b, 0, k))

    cache = pl.BlockSpec((1, H_B, T, DH_B), lambda b: (b, 0, 0, 0))
    return pl.pallas_call(
        _ctx_attn_kernel,
        out_shape=[jax.ShapeDtypeStruct((B, T, width), BF16),
                   jax.ShapeDtypeStruct((B, H_B, T, DH_B), F32),
                   jax.ShapeDtypeStruct((B, H_B, T, DH_B), F32)],
        grid=(B,),
        in_specs=[col(0), col(1), col(2), col(3), pl.BlockSpec((1, T, 2 * width), lambda b: (b, 0, 0))],
        out_specs=[pl.BlockSpec((1, T, width), lambda b: (b, 0, 0)), cache, cache],
        compiler_params=_cparams(1),
        name="ctx_attn",
    )(yb, yb, yb, yb, kv)


N_DR = 2 * NA_KH - 1
N_DC = 2 * NA_KW - 1
N_TAB = N_DR - 1


def _nat_kernel(rb_ref, q_ref, k_ref, v_ref, g_ref, kc_ref, vc_ref, o_ref,
                tab_ref, qs_ref, s_ref, p_ref, r_ref, *, rows):
    scale = DH_B ** -0.5
    kh = min(NA_KH, rows)
    masks = _head_masks()

    @pl.when(pl.program_id(1) == 0)
    def _build_tables():
        c = lax.broadcasted_iota(jnp.int32, (GRID_W, LANES), 0)
        lane = lax.broadcasted_iota(jnp.int32, (GRID_W, LANES), 1)
        kcol = lane & (GRID_W - 1)
        ws = jnp.clip(c - NA_KW // 2, 0, GRID_W - NA_KW)
        neg = jnp.full((GRID_W, LANES), NEG_INF, F32)
        inside = jnp.where(kcol >= ws, jnp.where(kcol < ws + NA_KW, 1.0, 0.0), 0.0) > 0.5
        for h in range(2):
            for i in range(N_TAB):
                row = jnp.broadcast_to(rb_ref[h, i:i + 1, :], (GRID_W, LANES))
                toeplitz = pltpu.roll(row, LANES - (NA_KW - 1), 1, stride=1, stride_axis=0)
                tab_ref[h, i] = jnp.where(inside, toeplitz, neg)

    kc = jnp.concatenate([kc_ref[0, 0], kc_ref[0, 1]], axis=1).astype(BF16)
    vc = jnp.concatenate([vc_ref[0, 0], vc_ref[0, 1]], axis=1).astype(BF16)
    n_keys = kh * GRID_W
    n_ctx = kc.shape[0]
    G = NA_GROUP
    W2 = 2 * GRID_W

    def group(gi, _):
        r_first = gi * G
        q0 = pl.multiple_of(r_first * GRID_W, G * GRID_W)
        for i in range(G):
            qi = q_ref[0, pl.ds(q0 + i * GRID_W, GRID_W), :] * scale
            for h in range(2):
                qs_ref[i * W2 + h * GRID_W:i * W2 + (h + 1) * GRID_W, :] = jnp.where(
                    masks[h], qi, jnp.zeros_like(qi))
        s_ref[:, n_keys:n_keys + n_ctx] = lax.dot_general(qs_ref[...], kc, NT_DIMS,
                                                          preferred_element_type=F32)
        windows = {}

        def local_scores(i):
            r = r_first + i
            rs = jnp.clip(r - kh // 2, 0, rows - kh)
            k0 = pl.multiple_of(rs * GRID_W, GRID_W)
            windows[i] = k0
            dr0 = rs - r + (NA_KH - 1)
            bias = jnp.concatenate(
                [jnp.concatenate([tab_ref[h, dr0 + 2 * m] for m in range(kh // 2)], axis=1)
                 for h in range(2)], axis=0)
            s_ref[i * W2:(i + 1) * W2, 0:n_keys] = lax.dot_general(
                qs_ref[i * W2:(i + 1) * W2, :], k_ref[0, pl.ds(k0, n_keys), :], NT_DIMS,
                preferred_element_type=F32) + bias

        def numerators(i):
            s = s_ref[i * W2:(i + 1) * W2, :]
            e = jnp.exp(s - jnp.max(s, axis=-1, keepdims=True))
            p_ref[i * W2:(i + 1) * W2, :] = e.astype(BF16)
            rinv = 1.0 / jnp.sum(e, axis=-1, keepdims=True)
            r_ref[i * W2:(i + 1) * W2, :] = jnp.broadcast_to(rinv, (W2, LANES))

        def weighted_values(i):
            vals = jnp.concatenate([v_ref[0, pl.ds(windows[i], n_keys), :], vc], axis=0)
            o = jnp.dot(p_ref[i * W2:(i + 1) * W2, :], vals, preferred_element_type=F32)
            o = o * r_ref[i * W2:(i + 1) * W2, :]
            o = jnp.where(masks[0], o[0:GRID_W], o[GRID_W:W2])
            out_rows = pl.ds(q0 + i * GRID_W, GRID_W)
            gate = g_ref[0, out_rows, :].astype(F32)
            o_ref[0, out_rows, :] = (o * _silu(gate)).astype(o_ref.dtype)

        for step in range(G + 2):
            if step < G:
                local_scores(step)
            if 0 <= step - 1 < G:
                numerators(step - 1)
            if 0 <= step - 2 < G:
                weighted_values(step - 2)
        return 0

    lax.fori_loop(0, rows // G, group, 0)


def _nat(yb, kc, vc, rel_bias):
    B, T, _ = yb.shape
    Tc = kc.shape[2]
    n_pair = H_B // 2
    rows = T // GRID_W
    n_stack = NA_GROUP * 2 * GRID_W
    n_keys = min(NA_KH, rows) * GRID_W
    pad = jnp.zeros((H_B, N_TAB, GRID_W - N_DC), F32)
    rel = rel_bias.astype(F32)
    rb_rows = jnp.concatenate([rel[:, 0:N_TAB], pad, rel[:, 1:N_TAB + 1], pad], axis=-1)

    def col(k):
        return pl.BlockSpec((1, T, LANES), lambda p, b, k=k: (b, 0, k * n_pair + p))

    ctx = pl.BlockSpec((1, 2, Tc, DH_B), lambda p, b: (b, p, 0, 0))
    return pl.pallas_call(
        functools.partial(_nat_kernel, rows=rows),
        out_shape=jax.ShapeDtypeStruct((B, T, H_B * DH_B), BF16),
        grid=(n_pair, B),
        in_specs=[pl.BlockSpec((2, N_TAB, LANES), lambda p, b: (p, 0, 0)),
                  col(0), col(1), col(2), col(3), ctx, ctx],
        out_specs=pl.BlockSpec((1, T, LANES), lambda p, b: (b, 0, p)),
        scratch_shapes=[pltpu.VMEM((2, N_TAB, GRID_W, LANES), F32),
                        pltpu.VMEM((n_stack, LANES), BF16),
                        pltpu.VMEM((n_stack, n_keys + Tc), F32),
                        pltpu.VMEM((n_stack, n_keys + Tc), BF16),
                        pltpu.VMEM((n_stack, LANES), F32)],
        compiler_params=_cparams(2),
        name="nbr_attn",
    )(rb_rows, yb, yb, yb, yb, kc, vc)


def _seg_len(seq):
    length = -(-seq // N_SEG)
    while length % 8 != 4:
        length += 1
    return length


def _step_block(seg_len):
    return max(d for d in range(1, seg_len + 1) if seg_len % d == 0 and d * N_SEG <= RG_ROWS * 9)


def _rglru_kernel(x_ref, g_ref, cw_ref, cb_ref, wg_ref, lam_ref, *rest, seq, slabs, has_s0, emit_state):
    rest = list(rest)
    s0_ref = rest.pop(0) if has_s0 else None
    o_ref = rest.pop(0)
    hfin_ref = rest.pop(0) if emit_state else None
    xpad_ref, a_ref, u_ref, h_ref, p_ref, hn_ref = rest
    L = _seg_len(seq)
    n_rows = N_SEG * L
    RB = RG_ROWS
    CB = slabs
    TB = _step_block(L)
    n_tile = N_SEG // SUBLANES
    lead = SUBLANES
    chains = [(d, j, s) for d in range(2) for j in range(CB) for s in range(n_tile)]

    for j in range(CB):
        xpad_ref[j, 0:lead, :] = jnp.zeros((lead, LANES), F32)
        xpad_ref[j, lead:seq + lead, :] = x_ref[0, :, j * LANES:(j + 1) * LANES]
        xpad_ref[j, seq + lead:n_rows + 2 * lead, :] = jnp.zeros((n_rows + lead - seq, LANES), F32)

    nl = -lam_ref[...]
    sp = jnp.maximum(nl, 0.0) + jnp.log1p(jnp.exp(-jnp.abs(nl)))
    cw = cw_ref[...]
    cbias = cb_ref[...]
    ones2 = jnp.where(lax.broadcasted_iota(jnp.int32, (TB * N_SEG, LANES), 1) < 2, 1.0, 0.0).astype(BF16)

    def gates(blk, _):
        t0 = blk * TB
        r0 = pl.multiple_of(blk * (TB * N_SEG), TB * N_SEG)
        for j in range(CB):
            lanes = slice(j * LANES, (j + 1) * LANES)
            tiles = []
            for tt in range(TB):
                for s in range(n_tile):
                    first = lead - 2 + t0 + tt + s * SUBLANES * L
                    taps = [xpad_ref[j, pl.ds(first + k, SUBLANES, stride=L), :] for k in range(4)]
                    xt = cw[0:1, lanes] * taps[0] + cw[1:2, lanes] * taps[1]
                    xt = xt + cw[2:3, lanes] * taps[2]
                    tiles.append(xt + cw[3:4, lanes] * taps[3] + cbias[:, lanes])
            xj = jnp.concatenate(tiles, axis=0)
            gt = jnp.dot(jnp.concatenate([xj.astype(BF16), ones2], axis=1), wg_ref[j],
                         preferred_element_type=F32)
            xh = 0.5 * xj
            for d in range(2):
                th_r = jnp.tanh(gt[:, (2 * d) * LANES:(2 * d + 1) * LANES])
                th_i = jnp.tanh(gt[:, (2 * d + 1) * LANES:(2 * d + 2) * LANES])
                half = (0.5 * RG_C) * sp[d:d + 1, lanes]
                nla = half + half * th_r
                a = jnp.exp2(nla * -LOG2E)
                y = jnp.tanh(nla) * (1.0 + a * a)
                root = jnp.where(y > 0.0, y * lax.rsqrt(y), 0.0)
                a_ref[d, j, pl.ds(r0, TB * N_SEG), :] = a
                u_ref[d, j, pl.ds(r0, TB * N_SEG), :] = root * (xh + xh * th_i)
        return 0

    lax.fori_loop(0, L // TB, gates, 0)

    first_pad = [[min(max(seq - (s * SUBLANES + r) * L, 0), L) for r in range(SUBLANES)]
                 for s in range(n_tile)]
    sub = lax.broadcasted_iota(jnp.int32, (SUBLANES, LANES), 0)
    pad_from = []
    for s in range(n_tile):
        if all(f == L for f in first_pad[s]):
            pad_from.append(None)
        else:
            vec = jnp.full((SUBLANES, LANES), L, jnp.int32)
            for r in range(SUBLANES):
                vec = jnp.where(sub == r, first_pad[s][r], vec)
            pad_from.append(vec)

    def step_rows(step, s):
        return pl.ds(pl.multiple_of(step * N_SEG + s * SUBLANES, SUBLANES), SUBLANES)

    unroll = 4

    def scan(i, carry):
        carry = list(carry)
        for k in range(unroll):
            for n, (d, j, s) in enumerate(chains):
                h, pr = carry[n]
                t = i * unroll + k
                if d == 1:
                    t = L - 1 - t
                idx = step_rows(t, s)
                a = a_ref[d, j, idx, :]
                u = u_ref[d, j, idx, :]
                if pad_from[s] is not None:
                    live = t < pad_from[s]
                    a = jnp.where(live, a, 1.0)
                    u = jnp.where(live, u, 0.0)
                h = a * h + u
                pr = pr * a
                h_ref[d, j, idx, :] = h
                p_ref[d, j, idx, :] = pr
                carry[n] = (h, pr)
        return tuple(carry)

    zero = jnp.zeros((SUBLANES, LANES), F32)
    one = jnp.ones((SUBLANES, LANES), F32)
    ends = lax.fori_loop(0, L // unroll, scan, ((zero, one),) * len(chains))

    cins = {}
    finals = [[None] * CB for _ in range(2)]
    for d in range(2):
        for j in range(CB):
            if has_s0:
                c = s0_ref[0, d:d + 1, j * LANES:(j + 1) * LANES]
            else:
                c = jnp.zeros((1, LANES), F32)
            cin = [None] * N_SEG
            for kk in range(N_SEG):
                seg = kk if d == 0 else N_SEG - 1 - kk
                s, row = divmod(seg, SUBLANES)
                h_end, p_end = ends[chains.index((d, j, s))]
                cin[seg] = c
                c = h_end[row:row + 1, :] + p_end[row:row + 1, :] * c
            finals[d][j] = c
            for s in range(n_tile):
                cins[(d, j, s)] = jnp.concatenate(cin[s * SUBLANES:(s + 1) * SUBLANES], axis=0)

    def fix(i, _):
        for k in range(unroll):
            t = i * unroll + k
            for j in range(CB):
                for s in range(n_tile):
                    idx = step_rows(t, s)
                    parts = [h_ref[d, j, idx, :] + p_ref[d, j, idx, :] * cins[(d, j, s)] for d in range(2)]
                    hn_ref[j, pl.ds(t + s * SUBLANES * L, SUBLANES, stride=L), :] = parts[0] + parts[1]
        return 0

    lax.fori_loop(0, L // unroll, fix, 0)

    if emit_state:
        hfin_ref[0] = jnp.concatenate([jnp.concatenate(finals[d], axis=1) for d in range(2)], axis=0)

    def combine(blk, _):
        rows = pl.ds(pl.multiple_of(blk * RB, RB), RB)
        hs = jnp.concatenate([hn_ref[j, rows, :] for j in range(CB)], axis=1)
        o_ref[0, rows, :] = (hs * _silu(g_ref[0, rows, :])).astype(o_ref.dtype)
        return 0

    lax.fori_loop(0, seq // RB, combine, 0)


def _gate_weights(gate_w, gate_b):
    w = (0.5 * gate_w).transpose(2, 3, 0, 1, 4).reshape(H_C, BW_C, 4 * BW_C).astype(BF16)
    b = (0.5 * gate_b).reshape(2, 2, H_C, BW_C).transpose(2, 0, 1, 3).reshape(H_C, 1, 4 * BW_C).astype(F32)
    hi = b.astype(BF16)
    lo = (b - hi.astype(F32)).astype(BF16)
    zeros = jnp.zeros((H_C, BW_C - 2, 4 * BW_C), BF16)
    return jnp.concatenate([w, hi, lo, zeros], axis=1)


def _rglru(xg, conv_w, conv_b, wg, lam, s0, emit_state):
    B, T, _ = xg.shape
    has_s0 = s0 is not None
    n_rows = N_SEG * _seg_len(T)
    CB = RG_SLABS if T > 4 * RG_ROWS else 2 * RG_SLABS
    wide = CB * LANES
    n_steps = H_C // CB
    in_specs = [
        pl.BlockSpec((1, T, wide), lambda b, c: (b, 0, c)),
        pl.BlockSpec((1, T, wide), lambda b, c: (b, 0, n_steps + c)),
        pl.BlockSpec((4, wide), lambda b, c: (0, c)),
        pl.BlockSpec((1, wide), lambda b, c: (0, c)),
        pl.BlockSpec((CB, 2 * BW_C, 4 * BW_C), lambda b, c: (c, 0, 0)),
        pl.BlockSpec((2, wide), lambda b, c: (0, c)),
    ]
    args = [xg, xg, conv_w, conv_b, wg, lam]
    if has_s0:
        in_specs.append(pl.BlockSpec((1, 2, wide), lambda b, c: (b, 0, c)))
        args.append(s0)
    out_shape = [jax.ShapeDtypeStruct((B, T, W_C), BF16)]
    out_specs = [pl.BlockSpec((1, T, wide), lambda b, c: (b, 0, c))]
    if emit_state:
        out_shape.append(jax.ShapeDtypeStruct((B, 2, W_C), F32))
        out_specs.append(pl.BlockSpec((1, 2, wide), lambda b, c: (b, 0, c)))
    res = pl.pallas_call(
        functools.partial(_rglru_kernel, seq=T, slabs=CB, has_s0=has_s0, emit_state=emit_state),
        out_shape=out_shape,
        grid=(B, n_steps),
        in_specs=in_specs,
        out_specs=out_specs,
        scratch_shapes=[pltpu.VMEM((CB, n_rows + 2 * SUBLANES, LANES), F32)]
        + [pltpu.VMEM((2, CB, n_rows, LANES), F32)] * 4 + [pltpu.VMEM((CB, n_rows, LANES), F32)],
        compiler_params=_cparams(2),
        name="rglru",
    )(*args)
    return res if emit_state else (res[0], None)


A_COLS = 5 * H_A * DK_A
B_COLS = 4 * H_B * DH_B


def kernel(x_prompt, x_sample, state_hgrn, cache_na_k, cache_na_v, state_rglru, c, c_ctx, norm_gain, w_mod, b_mod, w_in_even, w_out_even, hgrn_lb_logits, hgrn_out_gain, na_rel_bias, w_in_odd, w_out_odd, conv_w, conv_b, rg_gate_w, rg_gate_b, rg_lambda, final_gain):
    n_ctx = x_prompt.shape[0]
    n_lat = x_sample.shape[0]
    depth = w_mod.shape[0]

    cond = jnp.zeros((16, D_MODEL), F32).at[0].set(c_ctx).at[1:1 + n_lat].set(c)
    mod = _modulation(cond, w_mod, b_mod.reshape(depth, 1, 3 * D_MODEL))
    mod = mod.reshape(depth, 16, 3, D_MODEL)

    t_ctx = x_prompt.shape[1]

    def flat(a):
        return a.reshape(1, n_ctx * t_ctx, a.shape[-1])

    def unflat(a):
        return a.reshape(n_ctx, t_ctx, a.shape[-1])

    def in_proj_params(l):
        if l % 2 == 0:
            outs_s = ((0, A_COLS, F32), (A_COLS, B_COLS, BF16))
            outs_c = outs_s + ((A_COLS + H_B * DH_B, 2 * H_B * DH_B, F32),)
            a_key = H_A * DK_A
            col = jnp.arange(w_in_even.shape[-1])
            halve = jnp.where((col >= a_key) & (col < 3 * a_key), 0.5, 1.0).astype(F32)
            return (w_in_even[l // 2] * halve).astype(BF16), outs_c, outs_s
        outs = ((0, 2 * W_C, F32),)
        return w_in_odd[l // 2].astype(BF16), outs, outs

    xc, xs = x_prompt, x_sample
    new_hgrn, new_k, new_v, new_rg = [], [], [], []
    proj_c = proj_s = None
    for l in range(depth):
        j = l // 2
        mod_c, mod_s = mod[l, 0:1], mod[l, 1:1 + n_lat]
        if proj_c is None:
            gain = norm_gain[l].reshape(1, D_MODEL)
            w_in, outs_c, outs_s = in_proj_params(l)
            proj_c = [unflat(t) for t in _inproj(flat(xc), mod_c, gain, w_in, outs_c, 512, True)]
            proj_s = _inproj(xs, mod_s, gain, w_in, outs_s, 512, False)
        if l % 2 == 0:
            w_out = w_out_even[j].astype(BF16)
            (ya_c, yb_c, kv_c), (ya_s, yb_s) = proj_c, proj_s
            hgain = hgrn_out_gain[j].reshape(H_A, 1, DK_A)
            oa_c, s_fin = _hgrn(ya_c, hgrn_lb_logits, j, hgain, None)
            oa_s, _ = _hgrn(ya_s, hgrn_lb_logits, j, hgain, state_hgrn[:, j])
            ob_c, k_c, v_c = _ctx_attn(yb_c, kv_c)
            ob_s = _nat(yb_s, cache_na_k[:, j], cache_na_v[:, j], na_rel_bias[j])
            ys_c, ys_s = (oa_c, ob_c), (oa_s, ob_s)
            new_hgrn.append(s_fin)
            new_k.append(k_c)
            new_v.append(v_c)
        else:
            w_out = w_out_odd[j].astype(BF16)
            (xg_c,), (xg_s,) = proj_c, proj_s
            wg = _gate_weights(rg_gate_w[j], rg_gate_b[j])
            cb = conv_b[j].reshape(1, W_C)
            y_c, h_fin = _rglru(xg_c, conv_w[j], cb, wg, rg_lambda[j], None, True)
            y_s, _ = _rglru(xg_s, conv_w[j], cb, wg, rg_lambda[j], state_rglru[:, j], False)
            ys_c, ys_s = (y_c,), (y_s,)
            new_rg.append(h_fin)
        ys_c = tuple(flat(y) for y in ys_c)
        if l == depth - 1:
            fgain = final_gain.reshape(1, D_MODEL)
            (xc,) = _outproj(ys_c, flat(xc), mod_c, w_out, 1024, True, final_gain=fgain)
            (xs,) = _outproj(ys_s, xs, mod_s, w_out, 1024, False, final_gain=fgain)
            xc = unflat(xc)
        else:
            gain_n = norm_gain[l + 1].reshape(1, D_MODEL)
            w_n, outs_c, outs_s = in_proj_params(l + 1)
            mod_cn, mod_sn = mod[l + 1, 0:1], mod[l + 1, 1:1 + n_lat]
            xc, *proj_c = _outproj(ys_c, flat(xc), mod_c, w_out, 512, True,
                                   next_proj=(mod_cn, gain_n, w_n, outs_c))
            xs, *proj_s = _outproj(ys_s, xs, mod_s, w_out, 512, False,
                                   next_proj=(mod_sn, gain_n, w_n, outs_s))
            xc = unflat(xc)
            proj_c = [unflat(t) for t in proj_c]
    return (xc, xs, jnp.stack(new_hgrn, axis=1), jnp.stack(new_k, axis=1),
            jnp.stack(new_v, axis=1), jnp.stack(new_rg, axis=1))
```

```python
import functools

import jax
import jax.numpy as jnp
from jax import lax
from jax.experimental import pallas as pl
from jax.experimental.pallas import tpu as pltpu

F32 = jnp.float32
BF16 = jnp.bfloat16

D_MODEL = 1024
EPS = 1e-6
NEG_INF = -1e30
LOG2E = 1.4426950408889634
H_A = 4
DK_A = 128
HGRN_CHUNK = 32
HGRN_ROWS = 256
H_B = 8
DH_B = 64
GRID_W = 64
NA_KH = 8
NA_KW = 16
NA_GROUP = 32
W_C = 1024
H_C = 8
BW_C = W_C // H_C
RG_C = 8.0
RG_ROWS = 256
RG_SLABS = 2
N_SEG = 16
LANES = 128
SUBLANES = 8
VMEM_LIMIT = 48 * 1024 * 1024

NT_DIMS = (((1,), (1,)), ((), ()))


def _silu(x):
    half = 0.5 * x
    return half + half * jnp.tanh(half)


def _cparams(n_axes):
    return pltpu.CompilerParams(dimension_semantics=("arbitrary",) * n_axes,
                                vmem_limit_bytes=VMEM_LIMIT)


def _mod_kernel(cond_ref, w_ref, b_ref, o_ref):
    s = _silu(cond_ref[...])
    o_ref[0] = jnp.dot(s.astype(BF16), w_ref[0].astype(BF16), preferred_element_type=F32) + b_ref[0]


def _modulation(cond, w_mod, b_mod):
    depth = w_mod.shape[0]
    n_rows = cond.shape[0]
    return pl.pallas_call(
        _mod_kernel,
        out_shape=jax.ShapeDtypeStruct((depth, n_rows, 3 * D_MODEL), F32),
        grid=(depth, 3),
        in_specs=[
            pl.BlockSpec((n_rows, D_MODEL), lambda l, n: (0, 0)),
            pl.BlockSpec((1, D_MODEL, D_MODEL), lambda l, n: (l, 0, n)),
            pl.BlockSpec((1, 1, D_MODEL), lambda l, n: (l, 0, n)),
        ],
        out_specs=pl.BlockSpec((1, n_rows, D_MODEL), lambda l, n: (l, 0, n)),
        compiler_params=_cparams(2),
        name="adaln_mod",
    )(cond, w_mod, b_mod)


def _project(x, mod_ref, gain_ref, w_ref, out_refs, outs):
    var = jnp.mean(x * x, axis=-1, keepdims=True)
    y = x * lax.rsqrt(var + EPS) * gain_ref[...]
    h = y * (1.0 + mod_ref[0, 1:2, :]) + mod_ref[0, 0:1, :]
    hb = h.astype(BF16)
    step = 512
    for c in range(0, w_ref.shape[1], step):
        users = [(o_ref, c - col0) for o_ref, (col0, width, _) in zip(out_refs, outs)
                 if col0 <= c < col0 + width]
        if users:
            r = jnp.dot(hb, w_ref[:, c:c + step], preferred_element_type=F32)
            for o_ref, off in users:
                o_ref[0, :, off:off + step] = r.astype(o_ref.dtype)


def _inproj_kernel(x_ref, mod_ref, gain_ref, w_ref, *out_refs, outs):
    _project(x_ref[0], mod_ref, gain_ref, w_ref, out_refs, outs)


def _inproj(x, mod, gain, w, outs, tm, shared_mod):
    B, T, _ = x.shape
    n_cols = w.shape[1]
    mod_map = (lambda b, t: (0, 0, 0)) if shared_mod else (lambda b, t: (b, 0, 0))
    return pl.pallas_call(
        functools.partial(_inproj_kernel, outs=outs),
        out_shape=[jax.ShapeDtypeStruct((B, T, wd), dt) for _, wd, dt in outs],
        grid=(B, T // tm),
        in_specs=[
            pl.BlockSpec((1, tm, D_MODEL), lambda b, t: (b, t, 0)),
            pl.BlockSpec((1, 3, D_MODEL), mod_map),
            pl.BlockSpec((1, D_MODEL), lambda b, t: (0, 0)),
            pl.BlockSpec((D_MODEL, n_cols), lambda b, t: (0, 0)),
        ],
        out_specs=[pl.BlockSpec((1, tm, wd), lambda b, t: (b, t, 0)) for _, wd, _ in outs],
        compiler_params=_cparams(2),
        name="in_proj",
    )(x, mod, gain, w)


def _outproj_kernel(*refs, n_y, final, next_outs):
    y_refs, (x_ref, mod_ref, w_ref), rest = refs[:n_y], refs[n_y:n_y + 3], refs[n_y + 3:]
    m = None
    row = 0
    for y_ref in y_refs:
        width = y_ref.shape[-1]
        part = jnp.dot(y_ref[0], w_ref[row:row + width, :].astype(BF16), preferred_element_type=F32)
        m = part if m is None else m + part
        row += width
    xn = x_ref[0] + mod_ref[0, 2:3, :] * m
    if final:
        gain_ref, o_ref = rest
        var = jnp.mean(xn * xn, axis=-1, keepdims=True)
        o_ref[0] = xn * lax.rsqrt(var + EPS) * gain_ref[...]
    else:
        modn_ref, gainn_ref, wn_ref, o_ref = rest[:4]
        o_ref[0] = xn
        _project(xn, modn_ref, gainn_ref, wn_ref, rest[4:], next_outs)


def _outproj(ys, x, mod, w, tm, shared_mod, final_gain=None, next_proj=None):
    B, T, _ = x.shape
    final = final_gain is not None
    mod_map = (lambda b, t: (0, 0, 0)) if shared_mod else (lambda b, t: (b, 0, 0))
    row_block = pl.BlockSpec((1, tm, D_MODEL), lambda b, t: (b, t, 0))
    vec = pl.BlockSpec((1, D_MODEL), lambda b, t: (0, 0))
    in_specs = [pl.BlockSpec((1, tm, y.shape[-1]), lambda b, t: (b, t, 0)) for y in ys] + [
        row_block,
        pl.BlockSpec((1, 3, D_MODEL), mod_map),
        pl.BlockSpec((w.shape[0], D_MODEL), lambda b, t: (0, 0)),
    ]
    args = list(ys) + [x, mod, w]
    out_shape = [jax.ShapeDtypeStruct((B, T, D_MODEL), F32)]
    out_specs = [row_block]
    next_outs = None
    if final:
        in_specs.append(vec)
        args.append(final_gain)
    else:
        mod_n, gain_n, w_n, next_outs = next_proj
        in_specs += [pl.BlockSpec((1, 3, D_MODEL), mod_map), vec,
                     pl.BlockSpec((D_MODEL, w_n.shape[1]), lambda b, t: (0, 0))]
        args += [mod_n, gain_n, w_n]
        out_shape += [jax.ShapeDtypeStruct((B, T, wd), dt) for _, wd, dt in next_outs]
        out_specs += [pl.BlockSpec((1, tm, wd), lambda b, t: (b, t, 0)) for _, wd, _ in next_outs]
    return pl.pallas_call(
        functools.partial(_outproj_kernel, n_y=len(ys), final=final, next_outs=next_outs),
        out_shape=out_shape,
        grid=(B, T // tm),
        in_specs=in_specs,
        out_specs=out_specs,
        compiler_params=_cparams(2),
        name="out_proj",
    )(*args)


def _hgrn_kernel(q_ref, zf_ref, zb_ref, v_ref, g_ref, lgt_ref, gain_ref, *rest, seq, layer, per_block):
    rest = list(rest)
    s0_ref = None if per_block else rest.pop(0)
    o_ref = rest.pop(0)
    sfin_ref = rest.pop(0) if per_block else None
    acc_ref, qd_ref, ki_ref, kd_ref, kv_ref, st_ref, dec_ref, mst_ref, msk_ref, mexp_ref = rest
    R = HGRN_ROWS
    C = HGRN_CHUNK
    n_blk = seq // R
    n_chunk = R // C
    n_all = seq // C

    @pl.when((pl.program_id(0) == 0) & (pl.program_id(1) == 0))
    def _build_masks():
        ti = lax.broadcasted_iota(jnp.int32, (R, R), 0)
        tj = lax.broadcasted_iota(jnp.int32, (R, R), 1)
        shift = C.bit_length() - 1
        same = lax.shift_right_logical(ti, shift) == lax.shift_right_logical(tj, shift)
        one = jnp.ones((R, R), F32)
        zero = jnp.zeros((R, R), F32)
        incl = (jnp.where(same, jnp.where(tj <= ti, one, zero), zero),
                jnp.where(same, jnp.where(tj >= ti, one, zero), zero))
        for d in range(2):
            msk_ref[d] = incl[d]
            mst_ref[d] = incl[d].astype(BF16)
        rr = lax.broadcasted_iota(jnp.int32, (R, n_chunk * LANES), 0)
        cc = lax.broadcasted_iota(jnp.int32, (R, n_chunk * LANES), 1)
        own = lax.shift_right_logical(rr, shift) == lax.shift_right_logical(cc, LANES.bit_length() - 1)
        mexp_ref[...] = jnp.where(own, 1.0, 0.0).astype(BF16)

    lgt = [lgt_ref[:, i, :] for i in range(lgt_ref.shape[1])]
    lmax = functools.reduce(jnp.maximum, lgt)
    ex = [jnp.exp(t - lmax) for t in lgt]
    lb_all = sum(ex[:layer + 1]) / sum(ex)
    gain = gain_ref[0]

    blocks_per_trip = 2 if n_blk % 2 == 0 else 1
    finish_blocks = 8 if n_blk % 8 == 0 else blocks_per_trip

    def gates(i, _):
        for u in range(finish_blocks):
            blk = i * finish_blocks + u
            rows = pl.ds(pl.multiple_of(blk * R, R), R)
            q = q_ref[0, rows, :]
            for d in range(2):
                th = jnp.tanh((zf_ref if d == 0 else zb_ref)[0, rows, :])
                lb = lb_all[d:d + 1, :]
                c = 0.5 * (1.0 - lb)
                ct = c * th
                f = (lb + c) + ct
                k = c - ct
                logf = jnp.log(f)
                hi = logf.astype(BF16)
                lo = (logf - hi.astype(F32)).astype(BF16)
                cs = jnp.dot(mst_ref[d], jnp.concatenate([hi, lo], axis=1), preferred_element_type=F32)
                b = cs[:, 0:LANES] + cs[:, LANES:2 * LANES]
                ends = [c * C + (C - 1 if d == 0 else 0) for c in range(n_chunk)]
                btot = jnp.concatenate([jnp.broadcast_to(b[t:t + 1, :], (C, LANES)) for t in ends], axis=0)
                qd_ref[d, rows, :] = (q * jnp.exp(b)).astype(BF16)
                ki_ref[d, rows, :] = (k * jnp.exp2(b * -LOG2E)).astype(BF16)
                kd_ref[d, rows, :] = (k * jnp.exp(btot - b)).astype(BF16)
                for c in range(n_chunk):
                    dec_ref[d, blk * n_chunk + c] = jnp.exp(btot[c * C:c * C + SUBLANES, :])
        return 0

    lax.fori_loop(0, n_blk // finish_blocks, gates, 0)

    def intra(i, _):
        for u in range(finish_blocks):
            blk = i * finish_blocks + u
            rows = pl.ds(pl.multiple_of(blk * R, R), R)
            v = v_ref[0, rows, :]
            vb = v.astype(BF16)
            vt = v.T.astype(BF16)
            att_sum = None
            for d in range(2):
                att = lax.dot_general(qd_ref[d, rows, :], ki_ref[d, rows, :], NT_DIMS,
                                      preferred_element_type=F32)
                att = jnp.where(msk_ref[d] > 0.5, att, 0.0)
                att_sum = att if att_sum is None else att_sum + att
                kd_exp = jnp.concatenate([kd_ref[d, rows, :]] * n_chunk, axis=1) * mexp_ref[...]
                kv_all = jnp.dot(vt, kd_exp, preferred_element_type=F32)
                for c in range(n_chunk):
                    kv_ref[d, blk * n_chunk + c] = kv_all[:, c * LANES:(c + 1) * LANES]
            acc_ref[rows, :] = jnp.dot(att_sum.astype(BF16), vb, preferred_element_type=F32)
        return 0

    lax.fori_loop(0, n_blk // finish_blocks, intra, 0)

    unroll = 4

    def states(i, sts):
        sts = list(sts)
        for u in range(unroll):
            n = i * unroll + u
            for d in range(2):
                c = n if d == 0 else n_all - 1 - n
                st_ref[d, c] = sts[d].astype(BF16)
                dec = jnp.concatenate([dec_ref[d, c]] * (DK_A // SUBLANES), axis=0)
                sts[d] = sts[d] * dec + kv_ref[d, c]
        return tuple(sts)

    def block_states(blk, _):
        for d in range(2):
            st = jnp.zeros((DK_A, DK_A), F32)
            for cc in range(n_chunk):
                c = blk * n_chunk + (cc if d == 0 else n_chunk - 1 - cc)
                st_ref[d, c] = st.astype(BF16)
                dec = jnp.concatenate([dec_ref[d, c]] * (DK_A // SUBLANES), axis=0)
                st = st * dec + kv_ref[d, c]
            sfin_ref[blk, d, 0] = st.T
        return 0

    if per_block:
        lax.fori_loop(0, n_blk, block_states, 0)
    else:
        st0 = (s0_ref[0, 0, 0].T, s0_ref[0, 1, 0].T)
        lax.fori_loop(0, n_all // unroll, states, st0)

    lane_chunk = lax.shift_right_logical(lax.broadcasted_iota(jnp.int32, (DK_A, R), 1), C.bit_length() - 1)

    def finish(i, _):
        slabs = []
        for u in range(finish_blocks):
            blk = i * finish_blocks + u
            rows = pl.ds(pl.multiple_of(blk * R, R), R)
            inter_t = None
            for d in range(2):
                sts = st_ref[d, pl.ds(blk * n_chunk, n_chunk)].reshape(n_chunk * DK_A, DK_A)
                res = lax.dot_general(sts, qd_ref[d, rows, :], NT_DIMS, preferred_element_type=F32)
                picked = res[0:DK_A]
                for c in range(1, n_chunk):
                    picked = jnp.where(lane_chunk == c, res[c * DK_A:(c + 1) * DK_A], picked)
                inter_t = picked if inter_t is None else inter_t + picked
            slabs.append((rows, inter_t))
        for rows, inter_t in slabs:
            tot = acc_ref[rows, :] + inter_t.T
            var = jnp.mean(tot * tot, axis=-1, keepdims=True)
            y = tot * lax.rsqrt(var + EPS) * gain
            o_ref[0, rows, :] = (y * _silu(g_ref[0, rows, :])).astype(o_ref.dtype)
        return 0

    lax.fori_loop(0, n_blk // finish_blocks, finish, 0)


def _hgrn(ya, lgt, layer, gain, s0):
    B, T, width = ya.shape
    per_block = s0 is None
    if per_block:
        assert T == HGRN_ROWS
        group = 8 if B % 8 == 0 else 1
        n_seq, B, T = B, B // group, group * T
        ya = ya.reshape(B, T, width)

    def col(k):
        return pl.BlockSpec((1, T, LANES), lambda b, h, k=k: (b, 0, k * H_A + h))

    in_specs = [col(0), col(1), col(2), col(3), col(4),
                pl.BlockSpec((2, lgt.shape[1], LANES), lambda b, h: (0, 0, h)),
                pl.BlockSpec((1, 1, LANES), lambda b, h: (h, 0, 0))]
    args = [ya, ya, ya, ya, ya, lgt, gain]
    out_shape = [jax.ShapeDtypeStruct((B, T, H_A * DK_A), BF16)]
    out_specs = [pl.BlockSpec((1, T, LANES), lambda b, h: (b, 0, h))]
    if per_block:
        out_shape.append(jax.ShapeDtypeStruct((n_seq, 2, H_A, DK_A, DK_A), F32))
        out_specs.append(pl.BlockSpec((T // HGRN_ROWS, 2, 1, DK_A, DK_A), lambda b, h: (b, 0, h, 0, 0)))
    else:
        in_specs.append(pl.BlockSpec((1, 2, 1, DK_A, DK_A), lambda b, h: (b, 0, h, 0, 0)))
        args.append(s0)
    res = pl.pallas_call(
        functools.partial(_hgrn_kernel, seq=T, layer=layer, per_block=per_block),
        out_shape=out_shape,
        grid=(B, H_A),
        in_specs=in_specs,
        out_specs=out_specs,
        scratch_shapes=[pltpu.VMEM((T, LANES), F32),
                        pltpu.VMEM((2, T, LANES), BF16),
                        pltpu.VMEM((2, T, LANES), BF16),
                        pltpu.VMEM((2, T, LANES), BF16),
                        pltpu.VMEM((2, T // HGRN_CHUNK, DK_A, DK_A), F32),
                        pltpu.VMEM((2, T // HGRN_CHUNK, DK_A, DK_A), BF16),
                        pltpu.VMEM((2, T // HGRN_CHUNK, SUBLANES, LANES), F32),
                        pltpu.VMEM((2, HGRN_ROWS, HGRN_ROWS), BF16),
                        pltpu.VMEM((2, HGRN_ROWS, HGRN_ROWS), F32),
                        pltpu.VMEM((HGRN_ROWS, HGRN_ROWS // HGRN_CHUNK * LANES), BF16)],
        compiler_params=_cparams(2),
        name="hgrn2",
    )(*args)
    if per_block:
        return res[0].reshape(n_seq, HGRN_ROWS, H_A * DK_A), res[1]
    return res[0], None


def _head_masks():
    lane = lax.broadcasted_iota(jnp.int32, (1, LANES), 1)
    return lane < DH_B, lane >= DH_B


def _ctx_attn_kernel(q_ref, k_ref, v_ref, g_ref, kv_ref, o_ref, newk_ref, newv_ref):
    scale = DH_B ** -0.5
    masks = _head_masks()
    T = q_ref.shape[1]
    for b in range(q_ref.shape[0]):
        for h in range(H_B):
            newk_ref[b, h] = kv_ref[b, :, h * DH_B:(h + 1) * DH_B]
            newv_ref[b, h] = kv_ref[b, :, (H_B + h) * DH_B:(H_B + h + 1) * DH_B]
        for p in range(H_B // 2):
            cols = slice(p * LANES, (p + 1) * LANES)
            q = q_ref[b, :, cols] * scale
            qs = jnp.concatenate([jnp.where(masks[h], q, jnp.zeros_like(q)) for h in range(2)], axis=0)
            s = lax.dot_general(qs, k_ref[b, :, cols], NT_DIMS, preferred_element_type=F32)
            e = jnp.exp(s - jnp.max(s, axis=-1, keepdims=True))
            pr = e / jnp.sum(e, axis=-1, keepdims=True)
            o = jnp.dot(pr.astype(BF16), v_ref[b, :, cols], preferred_element_type=F32)
            o = jnp.where(masks[0], o[0:T], o[T:2 * T])
            o_ref[b, :, cols] = (o * _silu(g_ref[b, :, cols].astype(F32))).astype(o_ref.dtype)


def _ctx_attn(yb, kv):
    B, T, _ = yb.shape
    width = H_B * DH_B
    nb = 2 if B % 2 == 0 else 1

    def col(k):
        return pl.BlockSpec((nb, T, width), lambda b, k=k: (b, 0, k))

    cache = pl.BlockSpec((nb, H_B, T, DH_B), lambda b: (b, 0, 0, 0))
    return pl.pallas_call(
        _ctx_attn_kernel,
        out_shape=[jax.ShapeDtypeStruct((B, T, width), BF16),
                   jax.ShapeDtypeStruct((B, H_B, T, DH_B), F32),
                   jax.ShapeDtypeStruct((B, H_B, T, DH_B), F32)],
        grid=(B // nb,),
        in_specs=[col(0), col(1), col(2), col(3), pl.BlockSpec((nb, T, 2 * width), lambda b: (b, 0, 0))],
        out_specs=[pl.BlockSpec((nb, T, width), lambda b: (b, 0, 0)), cache, cache],
        compiler_params=_cparams(1),
        name="ctx_attn",
    )(yb, yb, yb, yb, kv)


N_DR = 2 * NA_KH - 1
N_DC = 2 * NA_KW - 1
N_TAB = N_DR - 1


def _nat_kernel(rb_ref, q_ref, k_ref, v_ref, g_ref, kc_ref, vc_ref, o_ref,
                tab_ref, qs_ref, s_ref, p_ref, r_ref, *, rows):
    scale = DH_B ** -0.5
    kh = min(NA_KH, rows)
    masks = _head_masks()

    @pl.when(pl.program_id(1) == 0)
    def _build_tables():
        c = lax.broadcasted_iota(jnp.int32, (GRID_W, LANES), 0)
        lane = lax.broadcasted_iota(jnp.int32, (GRID_W, LANES), 1)
        kcol = lane & (GRID_W - 1)
        ws = jnp.clip(c - NA_KW // 2, 0, GRID_W - NA_KW)
        neg = jnp.full((GRID_W, LANES), NEG_INF, F32)
        inside = jnp.where(kcol >= ws, jnp.where(kcol < ws + NA_KW, 1.0, 0.0), 0.0) > 0.5
        for h in range(2):
            for i in range(N_TAB):
                row = jnp.broadcast_to(rb_ref[h, i:i + 1, :], (GRID_W, LANES))
                toeplitz = pltpu.roll(row, LANES - (NA_KW - 1), 1, stride=1, stride_axis=0)
                tab_ref[h, i] = jnp.where(inside, toeplitz, neg)

    kc = jnp.concatenate([kc_ref[0, 0], kc_ref[0, 1]], axis=1).astype(BF16)
    vc = jnp.concatenate([vc_ref[0, 0], vc_ref[0, 1]], axis=1).astype(BF16)
    n_keys = kh * GRID_W
    n_ctx = kc.shape[0]
    G = NA_GROUP
    W2 = 2 * GRID_W

    def group(gi, _):
        r_first = gi * G
        q0 = pl.multiple_of(r_first * GRID_W, G * GRID_W)
        for i in range(G):
            qi = q_ref[0, pl.ds(q0 + i * GRID_W, GRID_W), :] * scale
            for h in range(2):
                qs_ref[i * W2 + h * GRID_W:i * W2 + (h + 1) * GRID_W, :] = jnp.where(
                    masks[h], qi, jnp.zeros_like(qi))
        s_ref[:, n_keys:n_keys + n_ctx] = lax.dot_general(qs_ref[...], kc, NT_DIMS,
                                                          preferred_element_type=F32)
        windows = {}

        def local_scores(i):
            r = r_first + i
            rs = jnp.clip(r - kh // 2, 0, rows - kh)
            k0 = pl.multiple_of(rs * GRID_W, GRID_W)
            windows[i] = k0
            dr0 = rs - r + (NA_KH - 1)
            bias = jnp.concatenate(
                [jnp.concatenate([tab_ref[h, dr0 + 2 * m] for m in range(kh // 2)], axis=1)
                 for h in range(2)], axis=0)
            s_ref[i * W2:(i + 1) * W2, 0:n_keys] = lax.dot_general(
                qs_ref[i * W2:(i + 1) * W2, :], k_ref[0, pl.ds(k0, n_keys), :], NT_DIMS,
                preferred_element_type=F32) + bias

        def numerators(i):
            s = s_ref[i * W2:(i + 1) * W2, :]
            e = jnp.exp(s - jnp.max(s, axis=-1, keepdims=True))
            p_ref[i * W2:(i + 1) * W2, :] = e.astype(BF16)
            rinv = 1.0 / jnp.sum(e, axis=-1, keepdims=True)
            r_ref[i * W2:(i + 1) * W2, :] = jnp.broadcast_to(rinv, (W2, LANES))

        def weighted_values(i):
            vals = jnp.concatenate([v_ref[0, pl.ds(windows[i], n_keys), :], vc], axis=0)
            o = jnp.dot(p_ref[i * W2:(i + 1) * W2, :], vals, preferred_element_type=F32)
            o = o * r_ref[i * W2:(i + 1) * W2, :]
            o = jnp.where(masks[0], o[0:GRID_W], o[GRID_W:W2])
            out_rows = pl.ds(q0 + i * GRID_W, GRID_W)
            gate = g_ref[0, out_rows, :].astype(F32)
            o_ref[0, out_rows, :] = (o * _silu(gate)).astype(o_ref.dtype)

        for step in range(G + 2):
            if step < G:
                local_scores(step)
            if 0 <= step - 1 < G:
                numerators(step - 1)
            if 0 <= step - 2 < G:
                weighted_values(step - 2)
        return 0

    lax.fori_loop(0, rows // G, group, 0)


def _nat(yb, kc, vc, rel_bias):
    B, T, _ = yb.shape
    Tc = kc.shape[2]
    n_pair = H_B // 2
    rows = T // GRID_W
    n_stack = NA_GROUP * 2 * GRID_W
    n_keys = min(NA_KH, rows) * GRID_W
    pad = jnp.zeros((H_B, N_TAB, GRID_W - N_DC), F32)
    rel = rel_bias.astype(F32)
    rb_rows = jnp.concatenate([rel[:, 0:N_TAB], pad, rel[:, 1:N_TAB + 1], pad], axis=-1)

    def col(k):
        return pl.BlockSpec((1, T, LANES), lambda p, b, k=k: (b, 0, k * n_pair + p))

    ctx = pl.BlockSpec((1, 2, Tc, DH_B), lambda p, b: (b, p, 0, 0))
    return pl.pallas_call(
        functools.partial(_nat_kernel, rows=rows),
        out_shape=jax.ShapeDtypeStruct((B, T, H_B * DH_B), BF16),
        grid=(n_pair, B),
        in_specs=[pl.BlockSpec((2, N_TAB, LANES), lambda p, b: (p, 0, 0)),
                  col(0), col(1), col(2), col(3), ctx, ctx],
        out_specs=pl.BlockSpec((1, T, LANES), lambda p, b: (b, 0, p)),
        scratch_shapes=[pltpu.VMEM((2, N_TAB, GRID_W, LANES), F32),
                        pltpu.VMEM((n_stack, LANES), BF16),
                        pltpu.VMEM((n_stack, n_keys + Tc), F32),
                        pltpu.VMEM((n_stack, n_keys + Tc), BF16),
                        pltpu.VMEM((n_stack, LANES), F32)],
        compiler_params=_cparams(2),
        name="nbr_attn",
    )(rb_rows, yb, yb, yb, yb, kc, vc)


def _seg_len(seq):
    length = -(-seq // N_SEG)
    while length % 8 != 4:
        length += 1
    return length


def _step_block(seg_len):
    return max(d for d in range(1, seg_len + 1) if seg_len % d == 0 and d * N_SEG <= RG_ROWS * 9)


def _rglru_kernel(x_ref, g_ref, cw_ref, cb_ref, wg_ref, lam_ref, *rest, seq, slabs, has_s0, emit_state):
    rest = list(rest)
    s0_ref = rest.pop(0) if has_s0 else None
    o_ref = rest.pop(0)
    hfin_ref = rest.pop(0) if emit_state else None
    xpad_ref, a_ref, u_ref, h_ref, p_ref, hn_ref = rest
    L = _seg_len(seq)
    n_rows = N_SEG * L
    RB = RG_ROWS
    CB = slabs
    TB = _step_block(L)
    n_tile = N_SEG // SUBLANES
    lead = SUBLANES
    chains = [(d, j, s) for d in range(2) for j in range(CB) for s in range(n_tile)]

    for j in range(CB):
        xpad_ref[j, 0:lead, :] = jnp.zeros((lead, LANES), F32)
        xpad_ref[j, lead:seq + lead, :] = x_ref[0, :, j * LANES:(j + 1) * LANES]
        xpad_ref[j, seq + lead:n_rows + 2 * lead, :] = jnp.zeros((n_rows + lead - seq, LANES), F32)

    nl = -lam_ref[...]
    sp = jnp.maximum(nl, 0.0) + jnp.log1p(jnp.exp(-jnp.abs(nl)))
    cw = cw_ref[...]
    cbias = cb_ref[...]
    ones2 = jnp.where(lax.broadcasted_iota(jnp.int32, (TB * N_SEG, LANES), 1) < 2, 1.0, 0.0).astype(BF16)

    def gates(blk, _):
        t0 = blk * TB
        r0 = pl.multiple_of(blk * (TB * N_SEG), TB * N_SEG)
        for j in range(CB):
            lanes = slice(j * LANES, (j + 1) * LANES)
            tiles = []
            for tt in range(TB):
                for s in range(n_tile):
                    first = lead - 2 + t0 + tt + s * SUBLANES * L
                    taps = [xpad_ref[j, pl.ds(first + k, SUBLANES, stride=L), :] for k in range(4)]
                    xt = cw[0:1, lanes] * taps[0] + cw[1:2, lanes] * taps[1]
                    xt = xt + cw[2:3, lanes] * taps[2]
                    tiles.append(xt + cw[3:4, lanes] * taps[3] + cbias[:, lanes])
            xj = jnp.concatenate(tiles, axis=0)
            gt = jnp.dot(jnp.concatenate([xj.astype(BF16), ones2], axis=1), wg_ref[j],
                         preferred_element_type=F32)
            xh = 0.5 * xj
            for d in range(2):
                th_r = jnp.tanh(gt[:, (2 * d) * LANES:(2 * d + 1) * LANES])
                th_i = jnp.tanh(gt[:, (2 * d + 1) * LANES:(2 * d + 2) * LANES])
                half = (0.5 * RG_C) * sp[d:d + 1, lanes]
                nla = half + half * th_r
                a = jnp.exp2(nla * -LOG2E)
                y = jnp.tanh(nla) * (1.0 + a * a)
                root = jnp.where(y > 0.0, y * lax.rsqrt(y), 0.0)
                a_ref[d, j, pl.ds(r0, TB * N_SEG), :] = a
                u_ref[d, j, pl.ds(r0, TB * N_SEG), :] = root * (xh + xh * th_i)
        return 0

    lax.fori_loop(0, L // TB, gates, 0)

    first_pad = [[min(max(seq - (s * SUBLANES + r) * L, 0), L) for r in range(SUBLANES)]
                 for s in range(n_tile)]
    sub = lax.broadcasted_iota(jnp.int32, (SUBLANES, LANES), 0)
    pad_from = []
    for s in range(n_tile):
        if all(f == L for f in first_pad[s]):
            pad_from.append(None)
        else:
            vec = jnp.full((SUBLANES, LANES), L, jnp.int32)
            for r in range(SUBLANES):
                vec = jnp.where(sub == r, first_pad[s][r], vec)
            pad_from.append(vec)

    def step_rows(step, s):
        return pl.ds(pl.multiple_of(step * N_SEG + s * SUBLANES, SUBLANES), SUBLANES)

    unroll = 4

    def scan(i, carry):
        carry = list(carry)
        for k in range(unroll):
            for n, (d, j, s) in enumerate(chains):
                h, pr = carry[n]
                t = i * unroll + k
                if d == 1:
                    t = L - 1 - t
                idx = step_rows(t, s)
                a = a_ref[d, j, idx, :]
                u = u_ref[d, j, idx, :]
                if pad_from[s] is not None:
                    live = t < pad_from[s]
                    a = jnp.where(live, a, 1.0)
                    u = jnp.where(live, u, 0.0)
                h = a * h + u
                pr = pr * a
                h_ref[d, j, idx, :] = h
                p_ref[d, j, idx, :] = pr
                carry[n] = (h, pr)
        return tuple(carry)

    zero = jnp.zeros((SUBLANES, LANES), F32)
    one = jnp.ones((SUBLANES, LANES), F32)
    ends = lax.fori_loop(0, L // unroll, scan, ((zero, one),) * len(chains))

    cins = {}
    finals = [[None] * CB for _ in range(2)]
    for d in range(2):
        for j in range(CB):
            if has_s0:
                c = s0_ref[0, d:d + 1, j * LANES:(j + 1) * LANES]
            else:
                c = jnp.zeros((1, LANES), F32)
            cin = [None] * N_SEG
            for kk in range(N_SEG):
                seg = kk if d == 0 else N_SEG - 1 - kk
                s, row = divmod(seg, SUBLANES)
                h_end, p_end = ends[chains.index((d, j, s))]
                cin[seg] = c
                c = h_end[row:row + 1, :] + p_end[row:row + 1, :] * c
            finals[d][j] = c
            for s in range(n_tile):
                cins[(d, j, s)] = jnp.concatenate(cin[s * SUBLANES:(s + 1) * SUBLANES], axis=0)

    def fix(i, _):
        for k in range(unroll):
            t = i * unroll + k
            for j in range(CB):
                for s in range(n_tile):
                    idx = step_rows(t, s)
                    parts = [h_ref[d, j, idx, :] + p_ref[d, j, idx, :] * cins[(d, j, s)] for d in range(2)]
                    hn_ref[j, pl.ds(t + s * SUBLANES * L, SUBLANES, stride=L), :] = parts[0] + parts[1]
        return 0

    lax.fori_loop(0, L // unroll, fix, 0)

    if emit_state:
        hfin_ref[0] = jnp.concatenate([jnp.concatenate(finals[d], axis=1) for d in range(2)], axis=0)

    def combine(blk, _):
        rows = pl.ds(pl.multiple_of(blk * RB, RB), RB)
        hs = jnp.concatenate([hn_ref[j, rows, :] for j in range(CB)], axis=1)
        o_ref[0, rows, :] = (hs * _silu(g_ref[0, rows, :])).astype(o_ref.dtype)
        return 0

    lax.fori_loop(0, seq // RB, combine, 0)


def _gate_weights(gate_w, gate_b):
    w = (0.5 * gate_w).transpose(2, 3, 0, 1, 4).reshape(H_C, BW_C, 4 * BW_C).astype(BF16)
    b = (0.5 * gate_b).reshape(2, 2, H_C, BW_C).transpose(2, 0, 1, 3).reshape(H_C, 1, 4 * BW_C).astype(F32)
    hi = b.astype(BF16)
    lo = (b - hi.astype(F32)).astype(BF16)
    zeros = jnp.zeros((H_C, BW_C - 2, 4 * BW_C), BF16)
    return jnp.concatenate([w, hi, lo, zeros], axis=1)


def _rglru(xg, conv_w, conv_b, wg, lam, s0, emit_state):
    B, T, _ = xg.shape
    has_s0 = s0 is not None
    n_rows = N_SEG * _seg_len(T)
    CB = RG_SLABS if T > 4 * RG_ROWS else 2 * RG_SLABS
    wide = CB * LANES
    n_steps = H_C // CB
    in_specs = [
        pl.BlockSpec((1, T, wide), lambda b, c: (b, 0, c)),
        pl.BlockSpec((1, T, wide), lambda b, c: (b, 0, n_steps + c)),
        pl.BlockSpec((4, wide), lambda b, c: (0, c)),
        pl.BlockSpec((1, wide), lambda b, c: (0, c)),
        pl.BlockSpec((CB, 2 * BW_C, 4 * BW_C), lambda b, c: (c, 0, 0)),
        pl.BlockSpec((2, wide), lambda b, c: (0, c)),
    ]
    args = [xg, xg, conv_w, conv_b, wg, lam]
    if has_s0:
        in_specs.append(pl.BlockSpec((1, 2, wide), lambda b, c: (b, 0, c)))
        args.append(s0)
    out_shape = [jax.ShapeDtypeStruct((B, T, W_C), BF16)]
    out_specs = [pl.BlockSpec((1, T, wide), lambda b, c: (b, 0, c))]
    if emit_state:
        out_shape.append(jax.ShapeDtypeStruct((B, 2, W_C), F32))
        out_specs.append(pl.BlockSpec((1, 2, wide), lambda b, c: (b, 0, c)))
    res = pl.pallas_call(
        functools.partial(_rglru_kernel, seq=T, slabs=CB, has_s0=has_s0, emit_state=emit_state),
        out_shape=out_shape,
        grid=(B, n_steps),
        in_specs=in_specs,
        out_specs=out_specs,
        scratch_shapes=[pltpu.VMEM((CB, n_rows + 2 * SUBLANES, LANES), F32)]
        + [pltpu.VMEM((2, CB, n_rows, LANES), F32)] * 4 + [pltpu.VMEM((CB, n_rows, LANES), F32)],
        compiler_params=_cparams(2),
        name="rglru",
    )(*args)
    return res if emit_state else (res[0], None)


A_COLS = 5 * H_A * DK_A
B_COLS = 4 * H_B * DH_B


def kernel(x_prompt, x_sample, state_hgrn, cache_na_k, cache_na_v, state_rglru, c, c_ctx, norm_gain, w_mod, b_mod, w_in_even, w_out_even, hgrn_lb_logits, hgrn_out_gain, na_rel_bias, w_in_odd, w_out_odd, conv_w, conv_b, rg_gate_w, rg_gate_b, rg_lambda, final_gain):
    n_ctx = x_prompt.shape[0]
    n_lat = x_sample.shape[0]
    depth = w_mod.shape[0]

    cond = jnp.zeros((16, D_MODEL), F32).at[0].set(c_ctx).at[1:1 + n_lat].set(c)
    mod = _modulation(cond, w_mod, b_mod.reshape(depth, 1, 3 * D_MODEL))
    mod = mod.reshape(depth, 16, 3, D_MODEL)

    t_ctx = x_prompt.shape[1]

    def flat(a):
        return a.reshape(1, n_ctx * t_ctx, a.shape[-1])

    def unflat(a):
        return a.reshape(n_ctx, t_ctx, a.shape[-1])

    def in_proj_params(l):
        if l % 2 == 0:
            outs_s = ((0, A_COLS, F32), (A_COLS, B_COLS, BF16))
            outs_c = outs_s + ((A_COLS + H_B * DH_B, 2 * H_B * DH_B, F32),)
            a_key = H_A * DK_A
            col = jnp.arange(w_in_even.shape[-1])
            halve = jnp.where((col >= a_key) & (col < 3 * a_key), 0.5, 1.0).astype(F32)
            return (w_in_even[l // 2] * halve).astype(BF16), outs_c, outs_s
        outs = ((0, 2 * W_C, F32),)
        return w_in_odd[l // 2].astype(BF16), outs, outs

    xc, xs = x_prompt, x_sample
    new_hgrn, new_k, new_v, new_rg = [], [], [], []
    proj_c = proj_s = None
    for l in range(depth):
        j = l // 2
        mod_c, mod_s = mod[l, 0:1], mod[l, 1:1 + n_lat]
        if proj_c is None:
            gain = norm_gain[l].reshape(1, D_MODEL)
            w_in, outs_c, outs_s = in_proj_params(l)
            proj_c = [unflat(t) for t in _inproj(flat(xc), mod_c, gain, w_in, outs_c, 512, True)]
            proj_s = _inproj(xs, mod_s, gain, w_in, outs_s, 512, False)
        if l % 2 == 0:
            w_out = w_out_even[j]
            (ya_c, yb_c, kv_c), (ya_s, yb_s) = proj_c, proj_s
            hgain = hgrn_out_gain[j].reshape(H_A, 1, DK_A)
            oa_c, s_fin = _hgrn(ya_c, hgrn_lb_logits, j, hgain, None)
            oa_s, _ = _hgrn(ya_s, hgrn_lb_logits, j, hgain, state_hgrn[:, j])
            ob_c, k_c, v_c = _ctx_attn(yb_c, kv_c)
            ob_s = _nat(yb_s, cache_na_k[:, j], cache_na_v[:, j], na_rel_bias[j])
            ys_c, ys_s = (oa_c, ob_c), (oa_s, ob_s)
            new_hgrn.append(s_fin)
            new_k.append(k_c)
            new_v.append(v_c)
        else:
            w_out = w_out_odd[j]
            (xg_c,), (xg_s,) = proj_c, proj_s
            wg = _gate_weights(rg_gate_w[j], rg_gate_b[j])
            cb = conv_b[j].reshape(1, W_C)
            y_c, h_fin = _rglru(xg_c, conv_w[j], cb, wg, rg_lambda[j], None, True)
            y_s, _ = _rglru(xg_s, conv_w[j], cb, wg, rg_lambda[j], state_rglru[:, j], False)
            ys_c, ys_s = (y_c,), (y_s,)
            new_rg.append(h_fin)
        ys_c = tuple(flat(y) for y in ys_c)
        if l == depth - 1:
            fgain = final_gain.reshape(1, D_MODEL)
            (xc,) = _outproj(ys_c, flat(xc), mod_c, w_out, 1024, True, final_gain=fgain)
            (xs,) = _outproj(ys_s, xs, mod_s, w_out, 1024, False, final_gain=fgain)
            xc = unflat(xc)
        else:
            gain_n = norm_gain[l + 1].reshape(1, D_MODEL)
            w_n, outs_c, outs_s = in_proj_params(l + 1)
            mod_cn, mod_sn = mod[l + 1, 0:1], mod[l + 1, 1:1 + n_lat]
            xc, *proj_c = _outproj(ys_c, flat(xc), mod_c, w_out, 512, True,
                                   next_proj=(mod_cn, gain_n, w_n, outs_c))
            xs, *proj_s = _outproj(ys_s, xs, mod_s, w_out, 512, False,
                                   next_proj=(mod_sn, gain_n, w_n, outs_s))
            xc = unflat(xc)
            proj_c = [unflat(t) for t in proj_c]
    return (xc, xs, jnp.stack(new_hgrn, axis=1), jnp.stack(new_k, axis=1),
            jnp.stack(new_v, axis=1), jnp.stack(new_rg, axis=1))
```

```python
import functools

import jax
import jax.numpy as jnp
from jax import lax
from jax.experimental import pallas as pl
from jax.experimental.pallas import tpu as pltpu

F32 = jnp.float32
BF16 = jnp.bfloat16

D_MODEL = 1024
EPS = 1e-6
NEG_INF = -1e30
LOG2E = 1.4426950408889634
H_A = 4
DK_A = 128
HGRN_CHUNK = 32
HGRN_ROWS = 256
H_B = 8
DH_B = 64
GRID_W = 64
NA_KH = 8
NA_KW = 16
NA_GROUP = 32
W_C = 1024
H_C = 8
BW_C = W_C // H_C
RG_C = 8.0
RG_ROWS = 256
RG_SLABS = 2
N_SEG = 16
LANES = 128
SUBLANES = 8
VMEM_LIMIT = 48 * 1024 * 1024

NT_DIMS = (((1,), (1,)), ((), ()))


def _silu(x):
    half = 0.5 * x
    return half + half * jnp.tanh(half)


def _cparams(n_axes):
    return pltpu.CompilerParams(dimension_semantics=("arbitrary",) * n_axes,
                                vmem_limit_bytes=VMEM_LIMIT)


def _mod_kernel(cond_ref, w_ref, b_ref, o_ref):
    s = _silu(cond_ref[...])
    o_ref[0] = jnp.dot(s.astype(BF16), w_ref[0].astype(BF16), preferred_element_type=F32) + b_ref[0]


def _modulation(cond, w_mod, b_mod):
    depth = w_mod.shape[0]
    n_rows = cond.shape[0]
    return pl.pallas_call(
        _mod_kernel,
        out_shape=jax.ShapeDtypeStruct((depth, n_rows, 3 * D_MODEL), F32),
        grid=(depth, 3),
        in_specs=[
            pl.BlockSpec((n_rows, D_MODEL), lambda l, n: (0, 0)),
            pl.BlockSpec((1, D_MODEL, D_MODEL), lambda l, n: (l, 0, n)),
            pl.BlockSpec((1, 1, D_MODEL), lambda l, n: (l, 0, n)),
        ],
        out_specs=pl.BlockSpec((1, n_rows, D_MODEL), lambda l, n: (l, 0, n)),
        compiler_params=_cparams(2),
        name="adaln_mod",
    )(cond, w_mod, b_mod)


def _project(x, mod_ref, gain_ref, w_ref, out_refs, outs):
    var = jnp.mean(x * x, axis=-1, keepdims=True)
    y = x * lax.rsqrt(var + EPS) * gain_ref[...]
    h = y * (1.0 + mod_ref[0, 1:2, :]) + mod_ref[0, 0:1, :]
    hb = h.astype(BF16)
    step = 512
    for c in range(0, w_ref.shape[1], step):
        users = [(o_ref, c - col0) for o_ref, (col0, width, _) in zip(out_refs, outs)
                 if col0 <= c < col0 + width]
        if users:
            r = jnp.dot(hb, w_ref[:, c:c + step], preferred_element_type=F32)
            for o_ref, off in users:
                o_ref[0, :, off:off + step] = r.astype(o_ref.dtype)


def _inproj_kernel(x_ref, mod_ref, gain_ref, w_ref, *out_refs, outs):
    _project(x_ref[0], mod_ref, gain_ref, w_ref, out_refs, outs)


def _inproj(x, mod, gain, w, outs, tm, shared_mod):
    B, T, _ = x.shape
    n_cols = w.shape[1]
    mod_map = (lambda b, t: (0, 0, 0)) if shared_mod else (lambda b, t: (b, 0, 0))
    return pl.pallas_call(
        functools.partial(_inproj_kernel, outs=outs),
        out_shape=[jax.ShapeDtypeStruct((B, T, wd), dt) for _, wd, dt in outs],
        grid=(B, T // tm),
        in_specs=[
            pl.BlockSpec((1, tm, D_MODEL), lambda b, t: (b, t, 0)),
            pl.BlockSpec((1, 3, D_MODEL), mod_map),
            pl.BlockSpec((1, D_MODEL), lambda b, t: (0, 0)),
            pl.BlockSpec((D_MODEL, n_cols), lambda b, t: (0, 0)),
        ],
        out_specs=[pl.BlockSpec((1, tm, wd), lambda b, t: (b, t, 0)) for _, wd, _ in outs],
        compiler_params=_cparams(2),
        name="in_proj",
    )(x, mod, gain, w)


def _outproj_kernel(*refs, n_y, final, next_outs):
    y_refs, (x_ref, mod_ref, w_ref), rest = refs[:n_y], refs[n_y:n_y + 3], refs[n_y + 3:]
    m = None
    row = 0
    for y_ref in y_refs:
        width = y_ref.shape[-1]
        part = jnp.dot(y_ref[0], w_ref[row:row + width, :].astype(BF16), preferred_element_type=F32)
        m = part if m is None else m + part
        row += width
    xn = x_ref[0] + mod_ref[0, 2:3, :] * m
    if final:
        gain_ref, o_ref = rest
        var = jnp.mean(xn * xn, axis=-1, keepdims=True)
        o_ref[0] = xn * lax.rsqrt(var + EPS) * gain_ref[...]
    else:
        modn_ref, gainn_ref, wn_ref, o_ref = rest[:4]
        o_ref[0] = xn
        _project(xn, modn_ref, gainn_ref, wn_ref, rest[4:], next_outs)


def _outproj(ys, x, mod, w, tm, shared_mod, final_gain=None, next_proj=None):
    B, T, _ = x.shape
    final = final_gain is not None
    mod_map = (lambda b, t: (0, 0, 0)) if shared_mod else (lambda b, t: (b, 0, 0))
    row_block = pl.BlockSpec((1, tm, D_MODEL), lambda b, t: (b, t, 0))
    vec = pl.BlockSpec((1, D_MODEL), lambda b, t: (0, 0))
    in_specs = [pl.BlockSpec((1, tm, y.shape[-1]), lambda b, t: (b, t, 0)) for y in ys] + [
        row_block,
        pl.BlockSpec((1, 3, D_MODEL), mod_map),
        pl.BlockSpec((w.shape[0], D_MODEL), lambda b, t: (0, 0)),
    ]
    args = list(ys) + [x, mod, w]
    out_shape = [jax.ShapeDtypeStruct((B, T, D_MODEL), F32)]
    out_specs = [row_block]
    next_outs = None
    if final:
        in_specs.append(vec)
        args.append(final_gain)
    else:
        mod_n, gain_n, w_n, next_outs = next_proj
        in_specs += [pl.BlockSpec((1, 3, D_MODEL), mod_map), vec,
                     pl.BlockSpec((D_MODEL, w_n.shape[1]), lambda b, t: (0, 0))]
        args += [mod_n, gain_n, w_n]
        out_shape += [jax.ShapeDtypeStruct((B, T, wd), dt) for _, wd, dt in next_outs]
        out_specs += [pl.BlockSpec((1, tm, wd), lambda b, t: (b, t, 0)) for _, wd, _ in next_outs]
    return pl.pallas_call(
        functools.partial(_outproj_kernel, n_y=len(ys), final=final, next_outs=next_outs),
        out_shape=out_shape,
        grid=(B, T // tm),
        in_specs=in_specs,
        out_specs=out_specs,
        compiler_params=_cparams(2),
        name="out_proj",
    )(*args)


def _hgrn_kernel(q_ref, zf_ref, zb_ref, v_ref, g_ref, lgt_ref, gain_ref, *rest, seq, layer, per_block):
    rest = list(rest)
    s0_ref = None if per_block else rest.pop(0)
    o_ref = rest.pop(0)
    sfin_ref = rest.pop(0) if per_block else None
    acc_ref, qd_ref, ki_ref, kd_ref, kv_ref, st_ref, dec_ref, mst_ref, msk_ref, mexp_ref = rest
    R = HGRN_ROWS
    C = HGRN_CHUNK
    n_blk = seq // R
    n_chunk = R // C
    n_all = seq // C

    @pl.when((pl.program_id(0) == 0) & (pl.program_id(1) == 0))
    def _build_masks():
        ti = lax.broadcasted_iota(jnp.int32, (R, R), 0)
        tj = lax.broadcasted_iota(jnp.int32, (R, R), 1)
        shift = C.bit_length() - 1
        same = lax.shift_right_logical(ti, shift) == lax.shift_right_logical(tj, shift)
        one = jnp.ones((R, R), F32)
        zero = jnp.zeros((R, R), F32)
        incl = (jnp.where(same, jnp.where(tj <= ti, one, zero), zero),
                jnp.where(same, jnp.where(tj >= ti, one, zero), zero))
        for d in range(2):
            msk_ref[d] = incl[d]
            mst_ref[d] = incl[d].astype(BF16)
        rr = lax.broadcasted_iota(jnp.int32, (R, n_chunk * LANES), 0)
        cc = lax.broadcasted_iota(jnp.int32, (R, n_chunk * LANES), 1)
        own = lax.shift_right_logical(rr, shift) == lax.shift_right_logical(cc, LANES.bit_length() - 1)
        mexp_ref[...] = jnp.where(own, 1.0, 0.0).astype(BF16)

    lgt = [lgt_ref[:, i, :] for i in range(lgt_ref.shape[1])]
    lmax = functools.reduce(jnp.maximum, lgt)
    ex = [jnp.exp(t - lmax) for t in lgt]
    lb_all = sum(ex[:layer + 1]) / sum(ex)
    gain = gain_ref[0]

    blocks_per_trip = 2 if n_blk % 2 == 0 else 1
    finish_blocks = 8 if n_blk % 8 == 0 else blocks_per_trip

    def gates(i, _):
        for u in range(finish_blocks):
            blk = i * finish_blocks + u
            rows = pl.ds(pl.multiple_of(blk * R, R), R)
            q = q_ref[0, rows, :]
            for d in range(2):
                th = jnp.tanh((zf_ref if d == 0 else zb_ref)[0, rows, :])
                lb = lb_all[d:d + 1, :]
                c = 0.5 * (1.0 - lb)
                ct = c * th
                f = (lb + c) + ct
                k = c - ct
                logf = jnp.log(f)
                hi = logf.astype(BF16)
                lo = (logf - hi.astype(F32)).astype(BF16)
                cs = jnp.dot(mst_ref[d], jnp.concatenate([hi, lo], axis=1), preferred_element_type=F32)
                b = cs[:, 0:LANES] + cs[:, LANES:2 * LANES]
                ends = [c * C + (C - 1 if d == 0 else 0) for c in range(n_chunk)]
                btot = jnp.concatenate([jnp.broadcast_to(b[t:t + 1, :], (C, LANES)) for t in ends], axis=0)
                qd_ref[d, rows, :] = (q * jnp.exp(b)).astype(BF16)
                ki_ref[d, rows, :] = (k * jnp.exp2(b * -LOG2E)).astype(BF16)
                kd_ref[d, rows, :] = (k * jnp.exp(btot - b)).astype(BF16)
                for c in range(n_chunk):
                    dec_ref[d, blk * n_chunk + c] = jnp.exp(btot[c * C:c * C + SUBLANES, :])
        return 0

    lax.fori_loop(0, n_blk // finish_blocks, gates, 0)

    def intra(i, _):
        for u in range(finish_blocks):
            blk = i * finish_blocks + u
            rows = pl.ds(pl.multiple_of(blk * R, R), R)
            v = v_ref[0, rows, :]
            vb = v.astype(BF16)
            vt = v.T.astype(BF16)
            att_sum = None
            for d in range(2):
                att = lax.dot_general(qd_ref[d, rows, :], ki_ref[d, rows, :], NT_DIMS,
                                      preferred_element_type=F32)
                att = jnp.where(msk_ref[d] > 0.5, att, 0.0)
                att_sum = att if att_sum is None else att_sum + att
                kd_exp = jnp.concatenate([kd_ref[d, rows, :]] * n_chunk, axis=1) * mexp_ref[...]
                kv_all = jnp.dot(vt, kd_exp, preferred_element_type=F32)
                for c in range(n_chunk):
                    kv_ref[d, blk * n_chunk + c] = kv_all[:, c * LANES:(c + 1) * LANES]
            acc_ref[rows, :] = jnp.dot(att_sum.astype(BF16), vb, preferred_element_type=F32)
        return 0

    lax.fori_loop(0, n_blk // finish_blocks, intra, 0)

    unroll = 4

    def states(i, sts):
        sts = list(sts)
        for u in range(unroll):
            n = i * unroll + u
            for d in range(2):
                c = n if d == 0 else n_all - 1 - n
                st_ref[d, c] = sts[d].astype(BF16)
                dec = jnp.concatenate([dec_ref[d, c]] * (DK_A // SUBLANES), axis=0)
                sts[d] = sts[d] * dec + kv_ref[d, c]
        return tuple(sts)

    def block_states(blk, _):
        for d in range(2):
            st = jnp.zeros((DK_A, DK_A), F32)
            for cc in range(n_chunk):
                c = blk * n_chunk + (cc if d == 0 else n_chunk - 1 - cc)
                st_ref[d, c] = st.astype(BF16)
                dec = jnp.concatenate([dec_ref[d, c]] * (DK_A // SUBLANES), axis=0)
                st = st * dec + kv_ref[d, c]
            sfin_ref[blk, d, 0] = st.T
        return 0

    if per_block:
        lax.fori_loop(0, n_blk, block_states, 0)
    else:
        st0 = (s0_ref[0, 0, 0].T, s0_ref[0, 1, 0].T)
        lax.fori_loop(0, n_all // unroll, states, st0)

    lane_chunk = lax.shift_right_logical(lax.broadcasted_iota(jnp.int32, (DK_A, R), 1), C.bit_length() - 1)

    def finish(i, _):
        slabs = []
        for u in range(finish_blocks):
            blk = i * finish_blocks + u
            rows = pl.ds(pl.multiple_of(blk * R, R), R)
            inter_t = None
            for d in range(2):
                sts = st_ref[d, pl.ds(blk * n_chunk, n_chunk)].reshape(n_chunk * DK_A, DK_A)
                res = lax.dot_general(sts, qd_ref[d, rows, :], NT_DIMS, preferred_element_type=F32)
                picked = res[0:DK_A]
                for c in range(1, n_chunk):
                    picked = jnp.where(lane_chunk == c, res[c * DK_A:(c + 1) * DK_A], picked)
                inter_t = picked if inter_t is None else inter_t + picked
            slabs.append((rows, inter_t))
        for rows, inter_t in slabs:
            tot = acc_ref[rows, :] + inter_t.T
            var = jnp.mean(tot * tot, axis=-1, keepdims=True)
            y = tot * lax.rsqrt(var + EPS) * gain
            o_ref[0, rows, :] = (y * _silu(g_ref[0, rows, :])).astype(o_ref.dtype)
        return 0

    lax.fori_loop(0, n_blk // finish_blocks, finish, 0)


def _hgrn(ya, lgt, layer, gain, s0):
    B, T, width = ya.shape
    per_block = s0 is None
    if per_block:
        assert T == HGRN_ROWS
        group = 8 if B % 8 == 0 else 1
        n_seq, B, T = B, B // group, group * T
        ya = ya.reshape(B, T, width)

    def col(k):
        return pl.BlockSpec((1, T, LANES), lambda b, h, k=k: (b, 0, k * H_A + h))

    in_specs = [col(0), col(1), col(2), col(3), col(4),
                pl.BlockSpec((2, lgt.shape[1], LANES), lambda b, h: (0, 0, h)),
                pl.BlockSpec((1, 1, LANES), lambda b, h: (h, 0, 0))]
    args = [ya, ya, ya, ya, ya, lgt, gain]
    out_shape = [jax.ShapeDtypeStruct((B, T, H_A * DK_A), BF16)]
    out_specs = [pl.BlockSpec((1, T, LANES), lambda b, h: (b, 0, h))]
    if per_block:
        out_shape.append(jax.ShapeDtypeStruct((n_seq, 2, H_A, DK_A, DK_A), F32))
        out_specs.append(pl.BlockSpec((T // HGRN_ROWS, 2, 1, DK_A, DK_A), lambda b, h: (b, 0, h, 0, 0)))
    else:
        in_specs.append(pl.BlockSpec((1, 2, 1, DK_A, DK_A), lambda b, h: (b, 0, h, 0, 0)))
        args.append(s0)
    res = pl.pallas_call(
        functools.partial(_hgrn_kernel, seq=T, layer=layer, per_block=per_block),
        out_shape=out_shape,
        grid=(B, H_A),
        in_specs=in_specs,
        out_specs=out_specs,
        scratch_shapes=[pltpu.VMEM((T, LANES), F32),
                        pltpu.VMEM((2, T, LANES), BF16),
                        pltpu.VMEM((2, T, LANES), BF16),
                        pltpu.VMEM((2, T, LANES), BF16),
                        pltpu.VMEM((2, T // HGRN_CHUNK, DK_A, DK_A), F32),
                        pltpu.VMEM((2, T // HGRN_CHUNK, DK_A, DK_A), BF16),
                        pltpu.VMEM((2, T // HGRN_CHUNK, SUBLANES, LANES), F32),
                        pltpu.VMEM((2, HGRN_ROWS, HGRN_ROWS), BF16),
                        pltpu.VMEM((2, HGRN_ROWS, HGRN_ROWS), F32),
                        pltpu.VMEM((HGRN_ROWS, HGRN_ROWS // HGRN_CHUNK * LANES), BF16)],
        compiler_params=_cparams(2),
        name="hgrn2",
    )(*args)
    if per_block:
        return res[0].reshape(n_seq, HGRN_ROWS, H_A * DK_A), res[1]
    return res[0], None


def _head_masks():
    lane = lax.broadcasted_iota(jnp.int32, (1, LANES), 1)
    return lane < DH_B, lane >= DH_B


def _ctx_attn_kernel(q_ref, k_ref, v_ref, g_ref, kv_ref, o_ref, newk_ref, newv_ref):
    scale = DH_B ** -0.5
    masks = _head_masks()
    T = q_ref.shape[1]
    for b in range(q_ref.shape[0]):
        for h in range(H_B):
            newk_ref[b, h] = kv_ref[b, :, h * DH_B:(h + 1) * DH_B]
            newv_ref[b, h] = kv_ref[b, :, (H_B + h) * DH_B:(H_B + h + 1) * DH_B]
        for p in range(H_B // 2):
            cols = slice(p * LANES, (p + 1) * LANES)
            q = q_ref[b, :, cols] * scale
            qs = jnp.concatenate([jnp.where(masks[h], q, jnp.zeros_like(q)) for h in range(2)], axis=0)
            s = lax.dot_general(qs, k_ref[b, :, cols], NT_DIMS, preferred_element_type=F32)
            e = jnp.exp(s - jnp.max(s, axis=-1, keepdims=True))
            pr = e / jnp.sum(e, axis=-1, keepdims=True)
            o = jnp.dot(pr.astype(BF16), v_ref[b, :, cols], preferred_element_type=F32)
            o = jnp.where(masks[0], o[0:T], o[T:2 * T])
            o_ref[b, :, cols] = (o * _silu(g_ref[b, :, cols].astype(F32))).astype(o_ref.dtype)


def _ctx_attn(yb, kv):
    B, T, _ = yb.shape
    width = H_B * DH_B
    nb = 4 if B % 4 == 0 else 1

    def col(k):
        return pl.BlockSpec((nb, T, width), lambda b, k=k: (b, 0, k))

    cache = pl.BlockSpec((nb, H_B, T, DH_B), lambda b: (b, 0, 0, 0))
    return pl.pallas_call(
        _ctx_attn_kernel,
        out_shape=[jax.ShapeDtypeStruct((B, T, width), BF16),
                   jax.ShapeDtypeStruct((B, H_B, T, DH_B), F32),
                   jax.ShapeDtypeStruct((B, H_B, T, DH_B), F32)],
        grid=(B // nb,),
        in_specs=[col(0), col(1), col(2), col(3), pl.BlockSpec((nb, T, 2 * width), lambda b: (b, 0, 0))],
        out_specs=[pl.BlockSpec((nb, T, width), lambda b: (b, 0, 0)), cache, cache],
        compiler_params=_cparams(1),
        name="ctx_attn",
    )(yb, yb, yb, yb, kv)


N_DR = 2 * NA_KH - 1
N_DC = 2 * NA_KW - 1
N_TAB = N_DR - 1


def _nat_kernel(rb_ref, q_ref, k_ref, v_ref, g_ref, kc_ref, vc_ref, o_ref,
                tab_ref, qs_ref, s_ref, p_ref, r_ref, *, rows):
    scale = DH_B ** -0.5
    kh = min(NA_KH, rows)
    masks = _head_masks()

    @pl.when(pl.program_id(1) == 0)
    def _build_tables():
        c = lax.broadcasted_iota(jnp.int32, (GRID_W, LANES), 0)
        lane = lax.broadcasted_iota(jnp.int32, (GRID_W, LANES), 1)
        kcol = lane & (GRID_W - 1)
        ws = jnp.clip(c - NA_KW // 2, 0, GRID_W - NA_KW)
        neg = jnp.full((GRID_W, LANES), NEG_INF, F32)
        inside = jnp.where(kcol >= ws, jnp.where(kcol < ws + NA_KW, 1.0, 0.0), 0.0) > 0.5
        for h in range(2):
            for i in range(N_TAB):
                row = jnp.broadcast_to(rb_ref[h, i:i + 1, :], (GRID_W, LANES))
                toeplitz = pltpu.roll(row, LANES - (NA_KW - 1), 1, stride=1, stride_axis=0)
                tab_ref[h, i] = jnp.where(inside, toeplitz, neg)

    kc = jnp.concatenate([kc_ref[0, 0], kc_ref[0, 1]], axis=1).astype(BF16)
    vc = jnp.concatenate([vc_ref[0, 0], vc_ref[0, 1]], axis=1).astype(BF16)
    n_keys = kh * GRID_W
    n_ctx = kc.shape[0]
    G = NA_GROUP
    W2 = 2 * GRID_W

    def group(gi, _):
        r_first = gi * G
        q0 = pl.multiple_of(r_first * GRID_W, G * GRID_W)
        for i in range(G):
            qi = q_ref[0, pl.ds(q0 + i * GRID_W, GRID_W), :] * scale
            for h in range(2):
                qs_ref[i * W2 + h * GRID_W:i * W2 + (h + 1) * GRID_W, :] = jnp.where(
                    masks[h], qi, jnp.zeros_like(qi))
        s_ref[:, n_keys:n_keys + n_ctx] = lax.dot_general(qs_ref[...], kc, NT_DIMS,
                                                          preferred_element_type=F32)
        windows = {}

        def local_scores(i):
            r = r_first + i
            rs = jnp.clip(r - kh // 2, 0, rows - kh)
            k0 = pl.multiple_of(rs * GRID_W, GRID_W)
            windows[i] = k0
            dr0 = rs - r + (NA_KH - 1)
            bias = jnp.concatenate(
                [jnp.concatenate([tab_ref[h, dr0 + 2 * m] for m in range(kh // 2)], axis=1)
                 for h in range(2)], axis=0)
            s_ref[i * W2:(i + 1) * W2, 0:n_keys] = lax.dot_general(
                qs_ref[i * W2:(i + 1) * W2, :], k_ref[0, pl.ds(k0, n_keys), :], NT_DIMS,
                preferred_element_type=F32) + bias

        def numerators(i):
            s = s_ref[i * W2:(i + 1) * W2, :]
            e = jnp.exp(s - jnp.max(s, axis=-1, keepdims=True))
            p_ref[i * W2:(i + 1) * W2, :] = e.astype(BF16)
            rinv = 1.0 / jnp.sum(e, axis=-1, keepdims=True)
            r_ref[i * W2:(i + 1) * W2, :] = jnp.broadcast_to(rinv, (W2, LANES))

        def weighted_values(i):
            vals = jnp.concatenate([v_ref[0, pl.ds(windows[i], n_keys), :], vc], axis=0)
            o = jnp.dot(p_ref[i * W2:(i + 1) * W2, :], vals, preferred_element_type=F32)
            o = o * r_ref[i * W2:(i + 1) * W2, :]
            o = jnp.where(masks[0], o[0:GRID_W], o[GRID_W:W2])
            out_rows = pl.ds(q0 + i * GRID_W, GRID_W)
            gate = g_ref[0, out_rows, :].astype(F32)
            o_ref[0, out_rows, :] = (o * _silu(gate)).astype(o_ref.dtype)

        for step in range(G + 2):
            if step < G:
                local_scores(step)
            if 0 <= step - 1 < G:
                numerators(step - 1)
            if 0 <= step - 2 < G:
                weighted_values(step - 2)
        return 0

    lax.fori_loop(0, rows // G, group, 0)


def _nat(yb, kc, vc, rel_bias):
    B, T, _ = yb.shape
    Tc = kc.shape[2]
    n_pair = H_B // 2
    rows = T // GRID_W
    n_stack = NA_GROUP * 2 * GRID_W
    n_keys = min(NA_KH, rows) * GRID_W
    pad = jnp.zeros((H_B, N_TAB, GRID_W - N_DC), F32)
    rel = rel_bias.astype(F32)
    rb_rows = jnp.concatenate([rel[:, 0:N_TAB], pad, rel[:, 1:N_TAB + 1], pad], axis=-1)

    def col(k):
        return pl.BlockSpec((1, T, LANES), lambda p, b, k=k: (b, 0, k * n_pair + p))

    ctx = pl.BlockSpec((1, 2, Tc, DH_B), lambda p, b: (b, p, 0, 0))
    return pl.pallas_call(
        functools.partial(_nat_kernel, rows=rows),
        out_shape=jax.ShapeDtypeStruct((B, T, H_B * DH_B), BF16),
        grid=(n_pair, B),
        in_specs=[pl.BlockSpec((2, N_TAB, LANES), lambda p, b: (p, 0, 0)),
                  col(0), col(1), col(2), col(3), ctx, ctx],
        out_specs=pl.BlockSpec((1, T, LANES), lambda p, b: (b, 0, p)),
        scratch_shapes=[pltpu.VMEM((2, N_TAB, GRID_W, LANES), F32),
                        pltpu.VMEM((n_stack, LANES), BF16),
                        pltpu.VMEM((n_stack, n_keys + Tc), F32),
                        pltpu.VMEM((n_stack, n_keys + Tc), BF16),
                        pltpu.VMEM((n_stack, LANES), F32)],
        compiler_params=_cparams(2),
        name="nbr_attn",
    )(rb_rows, yb, yb, yb, yb, kc, vc)


def _seg_len(seq):
    length = -(-seq // N_SEG)
    while length % 8 != 4:
        length += 1
    return length


def _step_block(seg_len):
    return max(d for d in range(1, seg_len + 1) if seg_len % d == 0 and d * N_SEG <= RG_ROWS * 9)


def _rglru_kernel(x_ref, g_ref, cw_ref, cb_ref, wg_ref, lam_ref, *rest, seq, slabs, has_s0, emit_state):
    rest = list(rest)
    s0_ref = rest.pop(0) if has_s0 else None
    o_ref = rest.pop(0)
    hfin_ref = rest.pop(0) if emit_state else None
    xpad_ref, a_ref, u_ref, h_ref, p_ref, hn_ref = rest
    L = _seg_len(seq)
    n_rows = N_SEG * L
    RB = RG_ROWS
    CB = slabs
    TB = _step_block(L)
    n_tile = N_SEG // SUBLANES
    lead = SUBLANES
    chains = [(d, j, s) for d in range(2) for j in range(CB) for s in range(n_tile)]

    for j in range(CB):
        xpad_ref[j, 0:lead, :] = jnp.zeros((lead, LANES), F32)
        xpad_ref[j, lead:seq + lead, :] = x_ref[0, :, j * LANES:(j + 1) * LANES]
        xpad_ref[j, seq + lead:n_rows + 2 * lead, :] = jnp.zeros((n_rows + lead - seq, LANES), F32)

    nl = -lam_ref[...]
    sp = jnp.maximum(nl, 0.0) + jnp.log1p(jnp.exp(-jnp.abs(nl)))
    cw = cw_ref[...]
    cbias = cb_ref[...]
    ones2 = jnp.where(lax.broadcasted_iota(jnp.int32, (TB * N_SEG, LANES), 1) < 2, 1.0, 0.0).astype(BF16)

    def gates(blk, _):
        t0 = blk * TB
        r0 = pl.multiple_of(blk * (TB * N_SEG), TB * N_SEG)
        for j in range(CB):
            lanes = slice(j * LANES, (j + 1) * LANES)
            tiles = []
            for tt in range(TB):
                for s in range(n_tile):
                    first = lead - 2 + t0 + tt + s * SUBLANES * L
                    taps = [xpad_ref[j, pl.ds(first + k, SUBLANES, stride=L), :] for k in range(4)]
                    xt = cw[0:1, lanes] * taps[0] + cw[1:2, lanes] * taps[1]
                    xt = xt + cw[2:3, lanes] * taps[2]
                    tiles.append(xt + cw[3:4, lanes] * taps[3] + cbias[:, lanes])
            xj = jnp.concatenate(tiles, axis=0)
            gt = jnp.dot(jnp.concatenate([xj.astype(BF16), ones2], axis=1), wg_ref[j],
                         preferred_element_type=F32)
            xh = 0.5 * xj
            for d in range(2):
                th_r = jnp.tanh(gt[:, (2 * d) * LANES:(2 * d + 1) * LANES])
                th_i = jnp.tanh(gt[:, (2 * d + 1) * LANES:(2 * d + 2) * LANES])
                half = (0.5 * RG_C) * sp[d:d + 1, lanes]
                nla = half + half * th_r
                a = jnp.exp2(nla * -LOG2E)
                y = jnp.tanh(nla) * (1.0 + a * a)
                root = jnp.where(y > 0.0, y * lax.rsqrt(y), 0.0)
                a_ref[d, j, pl.ds(r0, TB * N_SEG), :] = a
                u_ref[d, j, pl.ds(r0, TB * N_SEG), :] = root * (xh + xh * th_i)
        return 0

    lax.fori_loop(0, L // TB, gates, 0)

    first_pad = [[min(max(seq - (s * SUBLANES + r) * L, 0), L) for r in range(SUBLANES)]
                 for s in range(n_tile)]
    sub = lax.broadcasted_iota(jnp.int32, (SUBLANES, LANES), 0)
    pad_from = []
    for s in range(n_tile):
        if all(f == L for f in first_pad[s]):
            pad_from.append(None)
        else:
            vec = jnp.full((SUBLANES, LANES), L, jnp.int32)
            for r in range(SUBLANES):
                vec = jnp.where(sub == r, first_pad[s][r], vec)
            pad_from.append(vec)

    def step_rows(step, s):
        return pl.ds(pl.multiple_of(step * N_SEG + s * SUBLANES, SUBLANES), SUBLANES)

    unroll = 4

    def scan(i, carry):
        carry = list(carry)
        for k in range(unroll):
            for n, (d, j, s) in enumerate(chains):
                h, pr = carry[n]
                t = i * unroll + k
                if d == 1:
                    t = L - 1 - t
                idx = step_rows(t, s)
                a = a_ref[d, j, idx, :]
                u = u_ref[d, j, idx, :]
                if pad_from[s] is not None:
                    live = t < pad_from[s]
                    a = jnp.where(live, a, 1.0)
                    u = jnp.where(live, u, 0.0)
                h = a * h + u
                pr = pr * a
                h_ref[d, j, idx, :] = h
                p_ref[d, j, idx, :] = pr
                carry[n] = (h, pr)
        return tuple(carry)

    zero = jnp.zeros((SUBLANES, LANES), F32)
    one = jnp.ones((SUBLANES, LANES), F32)
    ends = lax.fori_loop(0, L // unroll, scan, ((zero, one),) * len(chains))

    cins = {}
    finals = [[None] * CB for _ in range(2)]
    for d in range(2):
        for j in range(CB):
            if has_s0:
                c = s0_ref[0, d:d + 1, j * LANES:(j + 1) * LANES]
            else:
                c = jnp.zeros((1, LANES), F32)
            cin = [None] * N_SEG
            for kk in range(N_SEG):
                seg = kk if d == 0 else N_SEG - 1 - kk
                s, row = divmod(seg, SUBLANES)
                h_end, p_end = ends[chains.index((d, j, s))]
                cin[seg] = c
                c = h_end[row:row + 1, :] + p_end[row:row + 1, :] * c
            finals[d][j] = c
            for s in range(n_tile):
                cins[(d, j, s)] = jnp.concatenate(cin[s * SUBLANES:(s + 1) * SUBLANES], axis=0)

    def fix(i, _):
        for k in range(unroll):
            t = i * unroll + k
            for j in range(CB):
                for s in range(n_tile):
                    idx = step_rows(t, s)
                    parts = [h_ref[d, j, idx, :] + p_ref[d, j, idx, :] * cins[(d, j, s)] for d in range(2)]
                    hn_ref[j, pl.ds(t + s * SUBLANES * L, SUBLANES, stride=L), :] = parts[0] + parts[1]
        return 0

    lax.fori_loop(0, L // unroll, fix, 0)

    if emit_state:
        hfin_ref[0] = jnp.concatenate([jnp.concatenate(finals[d], axis=1) for d in range(2)], axis=0)

    def combine(blk, _):
        rows = pl.ds(pl.multiple_of(blk * RB, RB), RB)
        hs = jnp.concatenate([hn_ref[j, rows, :] for j in range(CB)], axis=1)
        o_ref[0, rows, :] = (hs * _silu(g_ref[0, rows, :])).astype(o_ref.dtype)
        return 0

    lax.fori_loop(0, seq // RB, combine, 0)


def _gate_weights(gate_w, gate_b):
    w = (0.5 * gate_w).transpose(2, 3, 0, 1, 4).reshape(H_C, BW_C, 4 * BW_C).astype(BF16)
    b = (0.5 * gate_b).reshape(2, 2, H_C, BW_C).transpose(2, 0, 1, 3).reshape(H_C, 1, 4 * BW_C).astype(F32)
    hi = b.astype(BF16)
    lo = (b - hi.astype(F32)).astype(BF16)
    zeros = jnp.zeros((H_C, BW_C - 2, 4 * BW_C), BF16)
    return jnp.concatenate([w, hi, lo, zeros], axis=1)


def _rglru(xg, conv_w, conv_b, wg, lam, s0, emit_state):
    B, T, _ = xg.shape
    has_s0 = s0 is not None
    n_rows = N_SEG * _seg_len(T)
    CB = RG_SLABS if T > 4 * RG_ROWS else 2 * RG_SLABS
    wide = CB * LANES
    n_steps = H_C // CB
    in_specs = [
        pl.BlockSpec((1, T, wide), lambda b, c: (b, 0, c)),
        pl.BlockSpec((1, T, wide), lambda b, c: (b, 0, n_steps + c)),
        pl.BlockSpec((4, wide), lambda b, c: (0, c)),
        pl.BlockSpec((1, wide), lambda b, c: (0, c)),
        pl.BlockSpec((CB, 2 * BW_C, 4 * BW_C), lambda b, c: (c, 0, 0)),
        pl.BlockSpec((2, wide), lambda b, c: (0, c)),
    ]
    args = [xg, xg, conv_w, conv_b, wg, lam]
    if has_s0:
        in_specs.append(pl.BlockSpec((1, 2, wide), lambda b, c: (b, 0, c)))
        args.append(s0)
    out_shape = [jax.ShapeDtypeStruct((B, T, W_C), BF16)]
    out_specs = [pl.BlockSpec((1, T, wide), lambda b, c: (b, 0, c))]
    if emit_state:
        out_shape.append(jax.ShapeDtypeStruct((B, 2, W_C), F32))
        out_specs.append(pl.BlockSpec((1, 2, wide), lambda b, c: (b, 0, c)))
    res = pl.pallas_call(
        functools.partial(_rglru_kernel, seq=T, slabs=CB, has_s0=has_s0, emit_state=emit_state),
        out_shape=out_shape,
        grid=(B, n_steps),
        in_specs=in_specs,
        out_specs=out_specs,
        scratch_shapes=[pltpu.VMEM((CB, n_rows + 2 * SUBLANES, LANES), F32)]
        + [pltpu.VMEM((2, CB, n_rows, LANES), F32)] * 4 + [pltpu.VMEM((CB, n_rows, LANES), F32)],
        compiler_params=_cparams(2),
        name="rglru",
    )(*args)
    return res if emit_state else (res[0], None)


A_COLS = 5 * H_A * DK_A
B_COLS = 4 * H_B * DH_B


def kernel(x_prompt, x_sample, state_hgrn, cache_na_k, cache_na_v, state_rglru, c, c_ctx, norm_gain, w_mod, b_mod, w_in_even, w_out_even, hgrn_lb_logits, hgrn_out_gain, na_rel_bias, w_in_odd, w_out_odd, conv_w, conv_b, rg_gate_w, rg_gate_b, rg_lambda, final_gain):
    n_ctx = x_prompt.shape[0]
    n_lat = x_sample.shape[0]
    depth = w_mod.shape[0]

    cond = jnp.zeros((16, D_MODEL), F32).at[0].set(c_ctx).at[1:1 + n_lat].set(c)
    mod = _modulation(cond, w_mod, b_mod.reshape(depth, 1, 3 * D_MODEL))
    mod = mod.reshape(depth, 16, 3, D_MODEL)

    t_ctx = x_prompt.shape[1]

    def flat(a):
        return a.reshape(1, n_ctx * t_ctx, a.shape[-1])

    def unflat(a):
        return a.reshape(n_ctx, t_ctx, a.shape[-1])

    def in_proj_params(l):
        if l % 2 == 0:
            outs_s = ((0, A_COLS, F32), (A_COLS, B_COLS, BF16))
            outs_c = outs_s + ((A_COLS + H_B * DH_B, 2 * H_B * DH_B, F32),)
            a_key = H_A * DK_A
            col = jnp.arange(w_in_even.shape[-1])
            halve = jnp.where((col >= a_key) & (col < 3 * a_key), 0.5, 1.0).astype(F32)
            return (w_in_even[l // 2] * halve).astype(BF16), outs_c, outs_s
        outs = ((0, 2 * W_C, F32),)
        return w_in_odd[l // 2].astype(BF16), outs, outs

    xc, xs = x_prompt, x_sample
    new_hgrn, new_k, new_v, new_rg = [], [], [], []
    proj_c = proj_s = None
    for l in range(depth):
        j = l // 2
        mod_c, mod_s = mod[l, 0:1], mod[l, 1:1 + n_lat]
        if proj_c is None:
            gain = norm_gain[l].reshape(1, D_MODEL)
            w_in, outs_c, outs_s = in_proj_params(l)
            proj_c = [unflat(t) for t in _inproj(flat(xc), mod_c, gain, w_in, outs_c, 512, True)]
            proj_s = _inproj(xs, mod_s, gain, w_in, outs_s, 512, False)
        if l % 2 == 0:
            w_out = w_out_even[j]
            (ya_c, yb_c, kv_c), (ya_s, yb_s) = proj_c, proj_s
            hgain = hgrn_out_gain[j].reshape(H_A, 1, DK_A)
            oa_c, s_fin = _hgrn(ya_c, hgrn_lb_logits, j, hgain, None)
            oa_s, _ = _hgrn(ya_s, hgrn_lb_logits, j, hgain, state_hgrn[:, j])
            ob_c, k_c, v_c = _ctx_attn(yb_c, kv_c)
            ob_s = _nat(yb_s, cache_na_k[:, j], cache_na_v[:, j], na_rel_bias[j])
            ys_c, ys_s = (oa_c, ob_c), (oa_s, ob_s)
            new_hgrn.append(s_fin)
            new_k.append(k_c)
            new_v.append(v_c)
        else:
            w_out = w_out_odd[j]
            (xg_c,), (xg_s,) = proj_c, proj_s
            wg = _gate_weights(rg_gate_w[j], rg_gate_b[j])
            cb = conv_b[j].reshape(1, W_C)
            y_c, h_fin = _rglru(xg_c, conv_w[j], cb, wg, rg_lambda[j], None, True)
            y_s, _ = _rglru(xg_s, conv_w[j], cb, wg, rg_lambda[j], state_rglru[:, j], False)
            ys_c, ys_s = (y_c,), (y_s,)
            new_rg.append(h_fin)
        ys_c = tuple(flat(y) for y in ys_c)
        if l == depth - 1:
            fgain = final_gain.reshape(1, D_MODEL)
            (xc,) = _outproj(ys_c, flat(xc), mod_c, w_out, 1024, True, final_gain=fgain)
            (xs,) = _outproj(ys_s, xs, mod_s, w_out, 1024, False, final_gain=fgain)
            xc = unflat(xc)
        else:
            gain_n = norm_gain[l + 1].reshape(1, D_MODEL)
            w_n, outs_c, outs_s = in_proj_params(l + 1)
            mod_cn, mod_sn = mod[l + 1, 0:1], mod[l + 1, 1:1 + n_lat]
            xc, *proj_c = _outproj(ys_c, flat(xc), mod_c, w_out, 512, True,
                                   next_proj=(mod_cn, gain_n, w_n, outs_c))
            xs, *proj_s = _outproj(ys_s, xs, mod_s, w_out, 512, False,
                                   next_proj=(mod_sn, gain_n, w_n, outs_s))
            xc = unflat(xc)
            proj_c = [unflat(t) for t in proj_c]
    return (xc, xs, jnp.stack(new_hgrn, axis=1), jnp.stack(new_k, axis=1),
            jnp.stack(new_v, axis=1), jnp.stack(new_rg, axis=1))
```
